```python
import jax, jax.numpy as jnp
from jax import lax
import numpy as np

D_MODEL = 1024
BATCH = 32
SEQ = 256
DEPTH = 1
DEC_BATCH = 8
DEC_SEQ = 1024
PAST_LEN = 512

GRID_W = 64
EPS = 1e-6
HG_HEADS = 4
HG_DK = 128
HG_DV = 128
HG_WIDTH = HG_HEADS * HG_DK
HG_CHUNK = 32
N_HEADS = 8
N_KV = 2
HEAD_DIM = 64
Q_WIDTH = N_HEADS * HEAD_DIM
KV_WIDTH = N_KV * HEAD_DIM
ATTN_BLOCK = 128
ROPE_THETA = 10000.0
N_BRANCH = 2
D_FF = 4 * D_MODEL
N_MOD = 6
D_IN = 5 * HG_WIDTH + Q_WIDTH + 2 * KV_WIDTH + N_BRANCH * D_MODEL

kernel_name = "hybrid_hgrn2_gqa_diffusion_step"

F32 = jnp.float32


def _split_points():
    sizes = [HG_WIDTH] * 5 + [Q_WIDTH, KV_WIDTH, KV_WIDTH]
    return [int(v) for v in np.cumsum(sizes)]


def rms_norm(x, w):
    xf = x.astype(F32)
    y = xf * lax.rsqrt(jnp.mean(xf * xf, axis=-1, keepdims=True) + EPS)
    return (y * w.astype(F32)).astype(x.dtype)


def adaln(cond, w_ada, b_ada):
    m = jax.nn.silu(cond.astype(F32)) @ w_ada.astype(F32) + b_ada.astype(F32)
    return jnp.split(m[..., None, :], N_MOD, axis=-1)


def _rope_1d(x, cos, sin):
    x1, x2 = jnp.split(x, 2, axis=-1)
    c = cos[:, None, :]
    s = sin[:, None, :]
    return jnp.concatenate([x1 * c - x2 * s, x1 * s + x2 * c], axis=-1)


def axial_rope(x):
    n_tokens = x.shape[1]
    rows = n_tokens // GRID_W
    row = jnp.repeat(jnp.arange(rows, dtype=F32), GRID_W)
    col = jnp.tile(jnp.arange(GRID_W, dtype=F32), rows)
    axis_dim = HEAD_DIM // 2
    freqs = ROPE_THETA ** (-jnp.arange(0, axis_dim, 2, dtype=F32) / axis_dim)
    ang_r = row[:, None] * freqs
    ang_c = col[:, None] * freqs
    xf = x.astype(F32)
    x_row, x_col = jnp.split(xf, 2, axis=-1)
    out = jnp.concatenate([_rope_1d(x_row, jnp.cos(ang_r), jnp.sin(ang_r)),
                           _rope_1d(x_col, jnp.cos(ang_c), jnp.sin(ang_c))], axis=-1)
    return out.astype(x.dtype)


def block_attention(q, k, v):
    b, lq = q.shape[:2]
    nb = lq // ATTN_BLOCK
    qb = q.astype(F32).reshape(b, nb, ATTN_BLOCK, N_KV, N_HEADS // N_KV, HEAD_DIM)
    qb = jnp.moveaxis(qb, 1, 0)
    kf = k.astype(F32)
    vf = v.astype(F32)
    scale = HEAD_DIM ** -0.5

    def one_block(qi):
        s = jnp.einsum('bqkgd,bskd->bkgqs', qi, kf) * scale
        p = jax.nn.softmax(s, axis=-1)
        return jnp.einsum('bkgqs,bskd->bqkgd', p, vf)

    o = lax.map(one_block, qb)
    return jnp.moveaxis(o, 0, 1).reshape(b, lq, Q_WIDTH)


def hgrn_scan(q, k, v, log_f, s0):
    b, l = q.shape[:2]
    n = l // HG_CHUNK

    def chunks(t):
        return t.astype(F32).reshape(b, n, HG_CHUNK, HG_HEADS, -1).transpose(0, 3, 1, 2, 4)

    q, k, v, log_f = chunks(q), chunks(k), chunks(v), chunks(log_f)
    a = jnp.cumsum(log_f, axis=3)
    a_last = a[:, :, :, -1]
    q_dec = q * jnp.exp(a)
    k_dec = k * jnp.exp(-a)
    mask = jnp.tril(jnp.ones((HG_CHUNK, HG_CHUNK), dtype=bool))
    scores = jnp.where(mask, jnp.einsum('bhnik,bhnjk->bhnij', q_dec, k_dec), 0.0)
    o_intra = jnp.einsum('bhnij,bhnjv->bhniv', scores, v)
    k_end = k * jnp.exp(a_last[:, :, :, None, :] - a)
    kv = jnp.einsum('bhnck,bhncv->bhnkv', k_end, v)
    decay = jnp.exp(a_last)

    def step(s, inp):
        dec, kv_n = inp
        return dec[..., None] * s + kv_n, s

    s_final, s_starts = lax.scan(step, s0.astype(F32),
                                 (jnp.moveaxis(decay, 2, 0), jnp.moveaxis(kv, 2, 0)))
    s_starts = jnp.moveaxis(s_starts, 0, 2)
    o_inter = jnp.einsum('bhnck,bhnkv->bhncv', q_dec, s_starts)
    o = (o_intra + o_inter).transpose(0, 2, 3, 1, 4).reshape(b, l, HG_HEADS, HG_DV)
    return o, s_final


def hgrn_branch(hq, hf_fwd, hf_bwd, hi, hg, lb_fwd, lb_bwd, norm_w, s0_fwd, s0_bwd):
    b, l = hq.shape[:2]

    def heads(t):
        return t.astype(F32).reshape(b, l, HG_HEADS, -1)

    q = heads(hq) * HG_DK ** -0.5
    v = heads(hi)

    def gate(h_f, lb):
        f = lb + (1.0 - lb) * jax.nn.sigmoid(h_f.astype(F32))
        return heads(jnp.log(f)), heads(1.0 - f)

    logf_f, k_f = gate(hf_fwd, lb_fwd)
    logf_b, k_b = gate(hf_bwd, lb_bwd)
    o_f, s_f = hgrn_scan(q, k_f, v, logf_f, s0_fwd)

    def flip(t):
        return jnp.flip(t, axis=1)

    o_b, s_b = hgrn_scan(flip(q), flip(k_b), flip(v), flip(logf_b), s0_bwd)
    o = o_f + flip(o_b)
    o = rms_norm(o, norm_w) * jax.nn.silu(heads(hg))
    return o.reshape(b, l, HG_WIDTH), jnp.stack([s_f, s_b], axis=1)


def token_mixer(h, w_in, q_norm_w, k_norm_w, lb_fwd, lb_bwd, hg_norm_w, w_ho, w_ao, w_out,
                ctx_k, ctx_v, s0_fwd, s0_bwd, latent):
    b, l = h.shape[:2]
    z = h @ w_in
    hq, hf_f, hf_b, hi, hg, aq, ak, av, gl = jnp.split(z, _split_points(), axis=-1)
    q = rms_norm(aq.reshape(b, l, N_HEADS, HEAD_DIM), q_norm_w)
    k = rms_norm(ak.reshape(b, l, N_KV, HEAD_DIM), k_norm_w)
    v = av.reshape(b, l, N_KV, HEAD_DIM)
    if latent:
        q = axial_rope(q)
        k_lat = axial_rope(k)
        k_all = jnp.concatenate([ctx_k.astype(k_lat.dtype), k_lat], axis=1)
        v_all = jnp.concatenate([ctx_v.astype(v.dtype), v], axis=1)
    else:
        k_all, v_all = k, v
    o_a = block_attention(q, k_all, v_all)
    o_h, s_final = hgrn_branch(hq, hf_f, hf_b, hi, hg, lb_fwd, lb_bwd, hg_norm_w, s0_fwd, s0_bwd)
    g = jax.nn.sigmoid(gl.astype(F32)).reshape(b, l, N_BRANCH, D_MODEL)
    merged = g[:, :, 0] * (o_h @ w_ho) + g[:, :, 1] * (o_a @ w_ao)
    return merged @ w_out, k, v, s_final


def trunk_layer(x, mod, norm_mix_w, mix_w, ffn_w, ctx_k, ctx_v, s0_fwd, s0_bwd, latent):
    sh1, sc1, g1, sh2, sc2, g2 = mod
    h = rms_norm(x, norm_mix_w) * (1.0 + sc1) + sh1
    out, k, v, s = token_mixer(h, *mix_w, ctx_k, ctx_v, s0_fwd, s0_bwd, latent)
    x = x + g1 * out
    norm_ffn_w, w_ff1, w_ff2 = ffn_w
    h = rms_norm(x, norm_ffn_w) * (1.0 + sc2) + sh2
    x = x + g2 * (jnp.square(jax.nn.relu(h @ w_ff1)) @ w_ff2)
    return x, k, v, s


def setup_inputs(seed: int = 0) -> dict:
    key = jax.random.key(seed)
    ks = jax.random.split(key, 24)
    nrm = jax.random.normal
    return {
        "x_prompt": nrm(ks[0], (BATCH, SEQ, D_MODEL), F32),
        "x_sample": nrm(ks[1], (DEC_BATCH, DEC_SEQ, D_MODEL), F32),
        "cache_k": nrm(ks[2], (DEC_BATCH, DEPTH, PAST_LEN, N_KV, HEAD_DIM), F32),
        "cache_v": nrm(ks[3], (DEC_BATCH, DEPTH, PAST_LEN, N_KV, HEAD_DIM), F32),
        "state_hgrn": 0.5 * nrm(ks[4], (DEC_BATCH, DEPTH, 2, HG_HEADS, HG_DK, HG_DV), F32),
        "c": nrm(ks[5], (DEC_BATCH, D_MODEL), F32),
        "c_ctx": nrm(ks[6], (D_MODEL,), F32),
        "w_ada": nrm(ks[7], (DEPTH, D_MODEL, N_MOD * D_MODEL), F32) * D_MODEL ** -0.5,
        "b_ada": 0.02 * nrm(ks[8], (DEPTH, N_MOD * D_MODEL), F32),
        "norm_mix_w": 1.0 + 0.05 * nrm(ks[9], (DEPTH, D_MODEL), F32),
        "w_in": nrm(ks[10], (DEPTH, D_MODEL, D_IN), F32) * D_MODEL ** -0.5,
        "q_norm_w": 1.0 + 0.05 * nrm(ks[11], (DEPTH, HEAD_DIM), F32),
        "k_norm_w": 1.0 + 0.05 * nrm(ks[12], (DEPTH, HEAD_DIM), F32),
        "hgrn_lb_logits": 0.1 * nrm(ks[13], (2, DEPTH + 1, HG_WIDTH), F32),
        "hgrn_norm_w": 1.0 + 0.05 * nrm(ks[14], (DEPTH, HG_DV), F32),
        "w_hgrn_out": nrm(ks[15], (DEPTH, HG_WIDTH, D_MODEL), F32) * HG_WIDTH ** -0.5,
        "w_attn_out": nrm(ks[16], (DEPTH, Q_WIDTH, D_MODEL), F32) * Q_WIDTH ** -0.5,
        "w_out": nrm(ks[17], (DEPTH, D_MODEL, D_MODEL), F32) * D_MODEL ** -0.5,
        "norm_ffn_w": 1.0 + 0.05 * nrm(ks[18], (DEPTH, D_MODEL), F32),
        "w_ff1": nrm(ks[19], (DEPTH, D_MODEL, D_FF), F32) * D_MODEL ** -0.5,
        "w_ff2": nrm(ks[20], (DEPTH, D_FF, D_MODEL), F32) * D_FF ** -0.5,
        "final_norm_w": 1.0 + 0.05 * nrm(ks[21], (D_MODEL,), F32),
    }


def reference(x_prompt, x_sample, cache_k, cache_v, state_hgrn, c, c_ctx, w_ada, b_ada,
              norm_mix_w, w_in, q_norm_w, k_norm_w, hgrn_lb_logits, hgrn_norm_w,
              w_hgrn_out, w_attn_out, w_out, norm_ffn_w, w_ff1, w_ff2, final_norm_w):
    lb = jnp.cumsum(jax.nn.softmax(hgrn_lb_logits.astype(F32), axis=1), axis=1)
    xp, xs = x_prompt, x_sample
    zero_state = jnp.zeros((xp.shape[0], HG_HEADS, HG_DK, HG_DV), F32)
    new_k, new_v, new_s = [], [], []
    for l in range(DEPTH):
        mix_w = (w_in[l], q_norm_w[l], k_norm_w[l], lb[0, l], lb[1, l], hgrn_norm_w[l],
                 w_hgrn_out[l], w_attn_out[l], w_out[l])
        ffn_w = (norm_ffn_w[l], w_ff1[l], w_ff2[l])
        mod_p = adaln(c_ctx, w_ada[l], b_ada[l])
        xp, k_ctx, v_ctx, s_ctx = trunk_layer(xp, mod_p, norm_mix_w[l], mix_w, ffn_w,
                                              None, None, zero_state, zero_state, False)
        new_k.append(k_ctx)
        new_v.append(v_ctx)
        new_s.append(s_ctx)
        mod_s = adaln(c, w_ada[l], b_ada[l])
        xs, _, _, _ = trunk_layer(xs, mod_s, norm_mix_w[l], mix_w, ffn_w,
                                  cache_k[:, l], cache_v[:, l],
                                  state_hgrn[:, l, 0], state_hgrn[:, l, 1], True)
    y_prompt = rms_norm(xp, final_norm_w)
    y_sample = rms_norm(xs, final_norm_w)
    new_cache_k = jnp.stack(new_k, axis=1)
    new_cache_v = jnp.stack(new_v, axis=1)
    new_state_hgrn = jnp.stack(new_s, axis=1)
    return (y_prompt, y_sample, new_cache_k, new_cache_v, new_state_hgrn)
```

```python
import functools

import jax
import jax.numpy as jnp
from jax import lax
from jax.experimental import pallas as pl
from jax.experimental.pallas import tpu as pltpu

F32 = jnp.float32
BF16 = jnp.bfloat16

D_MODEL = 1024
GRID_W = 64
EPS = 1e-6
HG_HEADS = 4
HG_DK = 128
HG_DV = 128
HG_WIDTH = HG_HEADS * HG_DK
HG_CHUNK = 32
N_HEADS = 8
N_KV = 2
HEAD_DIM = 64
Q_WIDTH = N_HEADS * HEAD_DIM
KV_WIDTH = N_KV * HEAD_DIM
ROPE_THETA = 10000.0
D_FF = 4 * D_MODEL
N_MOD = 6
ZH_WIDTH = 5 * HG_WIDTH
D_IN = ZH_WIDTH + Q_WIDTH + 2 * KV_WIDTH + 2 * D_MODEL

V7X_VMEM_LIMIT_BYTES = 56 * 1024 * 1024
HG_BLOCK = 256
COND_ROWS = 16
IN_TILE = 512
OUT_TILE = 256
ATTN_Q_TILE = 256


def _dot(a, b):
    return jnp.dot(a, b, preferred_element_type=F32)


def _dot_nt(a, b):
    return lax.dot_general(a, b, (((1,), (1,)), ((), ())), preferred_element_type=F32)


def _split_bf16(x):
    hi = x.astype(BF16)
    lo = (x - hi.astype(F32)).astype(BF16)
    return hi, lo


def _params(n_grid):
    return pltpu.CompilerParams(dimension_semantics=("arbitrary",) * n_grid,
                                vmem_limit_bytes=V7X_VMEM_LIMIT_BYTES)


def _mod_spec(mod3, tile, seq_len):
    if mod3.shape[0] == 1:
        return pl.BlockSpec((1, 1, N_MOD * D_MODEL), lambda i: (0, 0, 0))
    assert seq_len % tile == 0
    per_seq = seq_len // tile
    return pl.BlockSpec((1, 1, N_MOD * D_MODEL), lambda i: (i // per_seq, 0, 0))


def _const_spec(shape):
    nd = len(shape)
    return pl.BlockSpec(shape, lambda *_: (0,) * nd, pipeline_mode=pl.Buffered(1))


def _mod_kernel(cond_ref, w_ref, b_ref, o_ref):
    c = cond_ref[...]
    x = c * jax.nn.sigmoid(c)
    x_hi, x_lo = _split_bf16(x)
    w_hi, w_lo = _split_bf16(w_ref[...])
    acc = _dot(x_hi, w_hi) + _dot(x_lo, w_hi) + _dot(x_hi, w_lo)
    o_ref[...] = acc + b_ref[...]


def _modulation(cond, w_ada, b_ada):
    n = N_MOD * D_MODEL
    tn = 768
    return pl.pallas_call(
        _mod_kernel,
        grid=(n // tn,),
        in_specs=[pl.BlockSpec((COND_ROWS, D_MODEL), lambda j: (0, 0)),
                  pl.BlockSpec((D_MODEL, tn), lambda j: (0, j)),
                  pl.BlockSpec((1, tn), lambda j: (0, j))],
        out_specs=pl.BlockSpec((COND_ROWS, tn), lambda j: (0, j)),
        out_shape=jax.ShapeDtypeStruct((COND_ROWS, n), F32),
        compiler_params=_params(1),
        name="adaln_mod",
    )(cond, w_ada, b_ada)


def _rms_rows(x, w):
    return x * lax.rsqrt(jnp.mean(x * x, axis=-1, keepdims=True) + EPS) * w


def _head_rms(a, ones_ref, w):
    hi, lo = _split_bf16(a * a)
    ss = _dot(hi, ones_ref[...]) + _dot(lo, ones_ref[...])
    return a * lax.rsqrt(ss * (1.0 / HEAD_DIM) + EPS) * w


def _rope(x, cos, s_up, s_dn):
    cols = []
    for j in range(x.shape[1] // 128):
        sl = slice(j * 128, (j + 1) * 128)
        xj = x[:, sl]
        cols.append(xj * cos[:, sl] + pltpu.roll(xj, 112, 1) * s_up[:, sl]
                    + pltpu.roll(xj, 16, 1) * s_dn[:, sl])
    return jnp.concatenate(cols, axis=1) if len(cols) > 1 else cols[0]


def _in_kernel(latent, x_ref, mod_ref, nw_ref, w_ref, qw_ref, kw_ref, oq_ref, ok_ref, *rest):
    if latent:
        cos_ref, sup_ref, sdn_ref, zh_ref, q_ref, k_ref, v_ref, g_ref = rest
    else:
        zh_ref, q_ref, k_ref, kf_ref, v_ref, g_ref = rest
    m = mod_ref[0]
    sh1 = m[:, 0:D_MODEL]
    sc1 = m[:, D_MODEL:2 * D_MODEL]
    h = _rms_rows(x_ref[...], nw_ref[...]) * (1.0 + sc1) + sh1
    hb = h.astype(BF16)
    c0 = ZH_WIDTH
    c1 = c0 + Q_WIDTH
    c2 = c1 + KV_WIDTH
    c3 = c2 + KV_WIDTH
    zh_ref[...] = _dot(hb, w_ref[:, 0:c0])
    qn = _head_rms(_dot(hb, w_ref[:, c0:c1]), oq_ref, qw_ref[...])
    kn = _head_rms(_dot(hb, w_ref[:, c1:c2]), ok_ref, kw_ref[...])
    v_ref[...] = _dot(hb, w_ref[:, c2:c3])
    g_ref[...] = jax.nn.sigmoid(_dot(hb, w_ref[:, c3:D_IN])).astype(BF16)
    if latent:
        cos, sup, sdn = cos_ref[...], sup_ref[...], sdn_ref[...]
        qn = _rope(qn, cos, sup, sdn)
        kn = _rope(kn, cos[:, 0:KV_WIDTH], sup[:, 0:KV_WIDTH], sdn[:, 0:KV_WIDTH])
    else:
        kf_ref[...] = kn
    q_ref[...] = (qn * HEAD_DIM ** -0.5).astype(BF16)
    k_ref[...] = kn.astype(BF16)


def _in_proj(x2, mod3, seq_len, norm_w, w_in_b, qw, kw, ones_q, ones_k, rope):
    t = x2.shape[0]
    tm = IN_TILE
    latent = rope is not None
    per_seq = max(seq_len // tm, 1)
    row = lambda i: (i, 0)
    in_specs = [pl.BlockSpec((tm, D_MODEL), row),
                _mod_spec(mod3, tm, seq_len),
                _const_spec((1, D_MODEL)),
                _const_spec((D_MODEL, D_IN)),
                _const_spec((1, Q_WIDTH)),
                _const_spec((1, KV_WIDTH)),
                _const_spec((Q_WIDTH, Q_WIDTH)),
                _const_spec((KV_WIDTH, KV_WIDTH))]
    args = [x2, mod3, norm_w, w_in_b, qw, kw, ones_q, ones_k]
    out_specs = [pl.BlockSpec((tm, ZH_WIDTH), row), pl.BlockSpec((tm, Q_WIDTH), row),
                 pl.BlockSpec((tm, KV_WIDTH), row)]
    out_shape = [jax.ShapeDtypeStruct((t, ZH_WIDTH), F32), jax.ShapeDtypeStruct((t, Q_WIDTH), BF16),
                 jax.ShapeDtypeStruct((t, KV_WIDTH), BF16)]
    if latent:
        in_specs += [pl.BlockSpec((tm, Q_WIDTH), lambda i: (i % per_seq, 0))] * 3
        args += list(rope)
    else:
        out_specs.append(pl.BlockSpec((tm, KV_WIDTH), row))
        out_shape.append(jax.ShapeDtypeStruct((t, KV_WIDTH), F32))
    out_specs += [pl.BlockSpec((tm, KV_WIDTH), row), pl.BlockSpec((tm, 2 * D_MODEL), row)]
    out_shape += [jax.ShapeDtypeStruct((t, KV_WIDTH), F32), jax.ShapeDtypeStruct((t, 2 * D_MODEL), BF16)]
    return pl.pallas_call(
        functools.partial(_in_kernel, latent),
        grid=(t // tm,),
        in_specs=in_specs,
        out_specs=out_specs,
        out_shape=out_shape,
        compiler_params=_params(1),
        name="in_proj_latent" if latent else "in_proj_ctx",
    )(*args)


def _attn_kernel(n_seg, q_ref, *refs):
    kv_refs, o_ref = refs[:2 * n_seg], refs[2 * n_seg]
    q = q_ref[...]
    ks = [kv_refs[2 * s][...].astype(BF16) for s in range(n_seg)]
    vts = [kv_refs[2 * s + 1][...].T.astype(BF16) for s in range(n_seg)]
    outs = []
    for h in range(N_HEADS):
        g = h // (N_HEADS // N_KV)
        qh = q[:, h * HEAD_DIM:(h + 1) * HEAD_DIM]
        gs = slice(g * HEAD_DIM, (g + 1) * HEAD_DIM)
        st = [_dot_nt(k[:, gs], qh) for k in ks]
        m = functools.reduce(jnp.maximum, [jnp.max(s, axis=0, keepdims=True) for s in st])
        pt = [jnp.exp(s - m) for s in st]
        denom = functools.reduce(jnp.add, [jnp.sum(p, axis=0, keepdims=True) for p in pt])
        ot = functools.reduce(jnp.add, [_dot(vt[gs, :], p.astype(BF16)) for vt, p in zip(vts, pt)])
        outs.append(ot / denom)
    o_ref[...] = jnp.concatenate(outs, axis=0).T.astype(BF16)


def _attention(q, kv_segments, n_batch, seq_len, name):
    t = q.shape[0]
    tq = ATTN_Q_TILE
    per_seq = seq_len // tq
    in_specs = [pl.BlockSpec((tq, Q_WIDTH), lambda b, j: (b * per_seq + j, 0))]
    args = [q]
    for k, v, lk in kv_segments:
        in_specs += [pl.BlockSpec((lk, KV_WIDTH), lambda b, j: (b, 0))] * 2
        args += [k, v]
    return pl.pallas_call(
        functools.partial(_attn_kernel, len(kv_segments)),
        grid=(n_batch, per_seq),
        in_specs=in_specs,
        out_specs=pl.BlockSpec((tq, Q_WIDTH), lambda b, j: (b * per_seq + j, 0)),
        out_shape=jax.ShapeDtypeStruct((t, Q_WIDTH), BF16),
        compiler_params=_params(2),
        name=name,
    )(*args)


def _hgrn_kernel(seq_len, has_s0, has_sfin, lbl_ref, q_ref, ff_ref, fb_ref, v_ref, g_ref, nw_ref, *rest):
    rest = list(rest)
    s0_ref = rest.pop(0) if has_s0 else None
    o_ref = rest.pop(0)
    sfin_ref = rest.pop(0) if has_sfin else None
    kv_scr, ss_scr, qd_scr, oi_scr, tot_scr = rest
    n_blk = seq_len // HG_BLOCK
    n_chunk = seq_len // HG_CHUNK
    per_blk = HG_BLOCK // HG_CHUNK
    shift = HG_CHUNK.bit_length() - 1

    logits = lbl_ref[...]
    e = jnp.exp(logits - jnp.max(logits, axis=1, keepdims=True))
    lb = e[:, 0, :] / (e[:, 0, :] + e[:, 1, :])
    lb_f, lb_b = lb[0:1, :], lb[1:2, :]

    r_id = lax.broadcasted_iota(jnp.int32, (HG_BLOCK, HG_BLOCK), 0)
    c_id = lax.broadcasted_iota(jnp.int32, (HG_BLOCK, HG_BLOCK), 1)
    r_chunk = r_id >> shift
    same = r_chunk == (c_id >> shift)
    d_lo = jnp.where(same, r_id - c_id, -1)
    d_up = jnp.where(same, c_id - r_id, -1)
    t_lo = jnp.where(d_lo >= 0, 1.0, 0.0).astype(BF16)
    t_up = jnp.where(d_up >= 0, 1.0, 0.0).astype(BF16)
    t_all = jnp.where(same, 1.0, 0.0).astype(BF16)

    def chunk_sum(t_mat, x):
        hi, lo = _split_bf16(x)
        r = _dot(t_mat, jnp.concatenate([hi, lo], axis=1))
        return r[:, 0:HG_DK] + r[:, HG_DK:2 * HG_DK]

    def gate(hf, lbv):
        f = lbv + (1.0 - lbv) * jax.nn.sigmoid(hf)
        return jnp.log(f), 1.0 - f

    def block_body(i, carry):
        rows = pl.ds(pl.multiple_of(i * HG_BLOCK, HG_BLOCK), HG_BLOCK)
        q = q_ref[rows, :] * HG_DK ** -0.5
        v = v_ref[rows, :]
        logf_f, k_f = gate(ff_ref[rows, :], lb_f)
        logf_b, k_b = gate(fb_ref[rows, :], lb_b)
        a_f = chunk_sum(t_lo, logf_f)
        a_b = chunk_sum(t_up, logf_b)
        tot_f = chunk_sum(t_all, logf_f)
        tot_b = chunk_sum(t_all, logf_b)
        qd_f = q * jnp.exp(a_f)
        qd_b = q * jnp.exp(a_b)
        kd_f = k_f * jnp.exp(-a_f)
        kd_b = k_b * jnp.exp(-a_b)
        ke = jnp.concatenate([k_f * jnp.exp(tot_f - a_f), k_b * jnp.exp(tot_b - a_b)], axis=1).astype(BF16)
        scores = (jnp.where(d_lo >= 0, _dot_nt(qd_f.astype(BF16), kd_f.astype(BF16)), 0.0)
                  + jnp.where(d_up >= 0, _dot_nt(qd_b.astype(BF16), kd_b.astype(BF16)), 0.0))
        oi_scr[rows, :] = _dot(scores.astype(BF16), v.astype(BF16))
        qd_scr[rows, :] = jnp.concatenate([qd_f, qd_b], axis=1).astype(BF16)
        tot_scr[rows, :] = jnp.concatenate([tot_f, tot_b], axis=1)
        vt = v.T.astype(BF16)
        for c in range(per_blk):
            kv_scr[i * per_blk + c] = _dot(vt, jnp.where(r_chunk == c, ke, jnp.zeros_like(ke)))
        return carry

    lax.fori_loop(0, n_blk, block_body, 0)

    if has_s0:
        s_f0, s_b0 = s0_ref[0, 0, 0].T, s0_ref[0, 1, 0].T
    else:
        s_f0 = s_b0 = jnp.zeros((HG_DV, HG_DK), F32)

    def scan_body(i, carry):
        s_f, s_b = carry
        cf = i
        cb = n_chunk - 1 - i
        ss_scr[cf, :, 0:HG_DK] = s_f.astype(BF16)
        dec_f = jnp.exp(tot_scr[pl.ds(cf * HG_CHUNK, 1), 0:HG_DK])
        s_f = dec_f * s_f + kv_scr[cf, :, 0:HG_DK]
        ss_scr[cb, :, HG_DK:2 * HG_DK] = s_b.astype(BF16)
        dec_b = jnp.exp(tot_scr[pl.ds(cb * HG_CHUNK, 1), HG_DK:2 * HG_DK])
        s_b = dec_b * s_b + kv_scr[cb, :, HG_DK:2 * HG_DK]
        return s_f, s_b

    s_f, s_b = lax.fori_loop(0, n_chunk, scan_body, (s_f0, s_b0))
    if has_sfin:
        sfin_ref[0, 0, 0] = s_f.T
        sfin_ref[0, 1, 0] = s_b.T

    def inter_body(c, carry):
        rows = pl.ds(pl.multiple_of(c * HG_CHUNK, HG_CHUNK), HG_CHUNK)
        oi_scr[rows, :] = oi_scr[rows, :] + _dot_nt(qd_scr[rows, :], ss_scr[c])
        return carry

    lax.fori_loop(0, n_chunk, inter_body, 0)

    o = _rms_rows(oi_scr[...], nw_ref[...])
    g = g_ref[...]
    o_ref[...] = (o * (g * jax.nn.sigmoid(g))).astype(BF16)


def _hgrn(zh, lb_logits, norm_w, n_batch, seq_len, s0, want_final, name):
    t = zh.shape[0]
    n_chunk = seq_len // HG_CHUNK

    def seg_spec(seg):
        return pl.BlockSpec((seq_len, HG_DK), lambda b, h: (b, seg * HG_HEADS + h))

    state_spec = pl.BlockSpec((1, 2, 1, HG_DK, HG_DV), lambda b, h: (b, 0, h, 0, 0))
    in_specs = [pl.BlockSpec((2, lb_logits.shape[1], HG_DK), lambda b, h: (0, 0, h)),
                seg_spec(0), seg_spec(1), seg_spec(2), seg_spec(3), seg_spec(4),
                pl.BlockSpec((1, HG_DV), lambda b, h: (0, 0))]
    args = [lb_logits, zh, zh, zh, zh, zh, norm_w]
    if s0 is not None:
        in_specs.append(state_spec)
        args.append(s0)
    out_specs = [pl.BlockSpec((seq_len, HG_DV), lambda b, h: (b, h))]
    out_shape = [jax.ShapeDtypeStruct((t, HG_WIDTH), BF16)]
    if want_final:
        out_specs.append(state_spec)
        out_shape.append(jax.ShapeDtypeStruct((n_batch, 2, HG_HEADS, HG_DK, HG_DV), F32))
    return pl.pallas_call(
        functools.partial(_hgrn_kernel, seq_len, s0 is not None, want_final),
        grid=(n_batch, HG_HEADS),
        in_specs=in_specs,
        out_specs=out_specs,
        out_shape=out_shape,
        scratch_shapes=[pltpu.VMEM((n_chunk, HG_DV, 2 * HG_DK), F32),
                        pltpu.VMEM((n_chunk, HG_DV, 2 * HG_DK), BF16),
                        pltpu.VMEM((seq_len, 2 * HG_DK), BF16),
                        pltpu.VMEM((seq_len, HG_DV), F32),
                        pltpu.VMEM((seq_len, 2 * HG_DK), F32)],
        compiler_params=_params(2),
        name=name,
    )(*args)


def _out_kernel(x_ref, mod_ref, oh_ref, oa_ref, g_ref, who_ref, wao_ref, wout_ref, nfw_ref,
                wff1_ref, wff2_ref, fnw_ref, y_ref):
    m = mod_ref[0]
    g1 = m[:, 2 * D_MODEL:3 * D_MODEL]
    sh2 = m[:, 3 * D_MODEL:4 * D_MODEL]
    sc2 = m[:, 4 * D_MODEL:5 * D_MODEL]
    g2 = m[:, 5 * D_MODEL:6 * D_MODEL]
    gates = g_ref[...].astype(F32)
    merged = (gates[:, 0:D_MODEL] * _dot(oh_ref[...], who_ref[...])
              + gates[:, D_MODEL:2 * D_MODEL] * _dot(oa_ref[...], wao_ref[...]))
    x1 = x_ref[...] + g1 * _dot(merged.astype(BF16), wout_ref[...])
    h2 = (_rms_rows(x1, nfw_ref[...]) * (1.0 + sc2) + sh2).astype(BF16)
    acc = jnp.zeros_like(x1)
    for j in range(D_FF // D_MODEL):
        cols = slice(j * D_MODEL, (j + 1) * D_MODEL)
        hj = jnp.maximum(_dot(h2, wff1_ref[:, cols]), 0.0)
        acc = acc + _dot((hj * hj).astype(BF16), wff2_ref[cols, :])
    y_ref[...] = _rms_rows(x1 + g2 * acc, fnw_ref[...])


def _out_proj(x2, mod3, seq_len, oh, oa, gates, w_ho, w_ao, w_out, nfw, w_ff1, w_ff2, fnw, name):
    t = x2.shape[0]
    tm = OUT_TILE
    row = lambda i: (i, 0)
    return pl.pallas_call(
        _out_kernel,
        grid=(t // tm,),
        in_specs=[pl.BlockSpec((tm, D_MODEL), row),
                  _mod_spec(mod3, tm, seq_len),
                  pl.BlockSpec((tm, HG_WIDTH), row),
                  pl.BlockSpec((tm, Q_WIDTH), row),
                  pl.BlockSpec((tm, 2 * D_MODEL), row),
                  _const_spec((HG_WIDTH, D_MODEL)),
                  _const_spec((Q_WIDTH, D_MODEL)),
                  _const_spec((D_MODEL, D_MODEL)),
                  _const_spec((1, D_MODEL)),
                  _const_spec((D_MODEL, D_FF)),
                  _const_spec((D_FF, D_MODEL)),
                  _const_spec((1, D_MODEL))],
        out_specs=pl.BlockSpec((tm, D_MODEL), row),
        out_shape=jax.ShapeDtypeStruct((t, D_MODEL), F32),
        compiler_params=_params(1),
        name=name,
    )(x2, mod3, oh, oa, gates, w_ho, w_ao, w_out, nfw, w_ff1, w_ff2, fnw)


def _rope_tables(n_tokens):
    rows = n_tokens // GRID_W
    row = jnp.repeat(jnp.arange(rows, dtype=F32), GRID_W)
    col = jnp.tile(jnp.arange(GRID_W, dtype=F32), rows)
    axis_dim = HEAD_DIM // 2
    freqs = ROPE_THETA ** (-jnp.arange(0, axis_dim, 2, dtype=F32) / axis_dim)
    ang_r = row[:, None] * freqs
    ang_c = col[:, None] * freqs
    cr, sr, cc, sc = jnp.cos(ang_r), jnp.sin(ang_r), jnp.cos(ang_c), jnp.sin(ang_c)
    zero = jnp.zeros_like(sr)
    cos = jnp.concatenate([cr, cr, cc, cc], axis=-1)
    s_up = jnp.concatenate([-sr, zero, -sc, zero], axis=-1)
    s_dn = jnp.concatenate([zero, sr, zero, sc], axis=-1)
    return tuple(jnp.tile(a, (1, N_HEADS)) for a in (cos, s_up, s_dn))


def _block_ones(width):
    idx = jnp.arange(width) // HEAD_DIM
    return (idx[:, None] == idx[None, :]).astype(BF16)


def kernel(x_prompt, x_sample, cache_k, cache_v, state_hgrn, c, c_ctx, w_ada, b_ada, norm_mix_w, w_in, q_norm_w, k_norm_w, hgrn_lb_logits, hgrn_norm_w, w_hgrn_out, w_attn_out, w_out, norm_ffn_w, w_ff1, w_ff2, final_norm_w):
    n_p, l_p, _ = x_prompt.shape
    n_s, l_s, _ = x_sample.shape
    past = cache_k.shape[2]
    layer = 0

    cond = jnp.concatenate([c_ctx[None, :], c, jnp.zeros((COND_ROWS - 1 - n_s, D_MODEL), F32)], axis=0)
    mod = _modulation(cond, w_ada[layer], b_ada[layer][None, :])
    mod_p = mod[0:1].reshape(1, 1, N_MOD * D_MODEL)
    mod_s = mod[1:1 + n_s].reshape(n_s, 1, N_MOD * D_MODEL)

    w_in_b = w_in[layer].astype(BF16)
    w_ho_b = w_hgrn_out[layer].astype(BF16)
    w_ao_b = w_attn_out[layer].astype(BF16)
    w_out_b = w_out[layer].astype(BF16)
    w_ff1_b = w_ff1[layer].astype(BF16)
    w_ff2_b = w_ff2[layer].astype(BF16)
    nmw = norm_mix_w[layer][None, :]
    nfw = norm_ffn_w[layer][None, :]
    fnw = final_norm_w[None, :]
    qw = jnp.tile(q_norm_w[layer], N_HEADS)[None, :]
    kw = jnp.tile(k_norm_w[layer], N_KV)[None, :]
    hnw = hgrn_norm_w[layer][None, :]
    ones_q = _block_ones(Q_WIDTH)
    ones_k = _block_ones(KV_WIDTH)

    outs = []
    for latent in (False, True):
        if latent:
            x, n_b, seq, mod3 = x_sample, n_s, l_s, mod_s
        else:
            x, n_b, seq, mod3 = x_prompt, n_p, l_p, mod_p
        x2 = x.reshape(n_b * seq, D_MODEL)
        proj = _in_proj(x2, mod3, seq, nmw, w_in_b, qw, kw, ones_q, ones_k,
                        _rope_tables(seq) if latent else None)
        if latent:
            zh, q, k, v, gates = proj
            segs = [(cache_k[:, layer].reshape(n_b * past, KV_WIDTH),
                     cache_v[:, layer].reshape(n_b * past, KV_WIDTH), past), (k, v, seq)]
            oa = _attention(q, segs, n_b, seq, "attn_latent")
            (oh,) = _hgrn(zh, hgrn_lb_logits, hnw, n_b, seq, state_hgrn[:, layer], False, "hgrn_latent")
        else:
            zh, q, k, k_f32, v, gates = proj
            oa = _attention(q, [(k, v, seq)], n_b, seq, "attn_ctx")
            oh, s_fin = _hgrn(zh, hgrn_lb_logits, hnw, n_b, seq, None, True, "hgrn_ctx")
            new_k = k_f32.reshape(n_b, 1, seq, N_KV, HEAD_DIM)
            new_v = v.reshape(n_b, 1, seq, N_KV, HEAD_DIM)
            new_s = s_fin.reshape(n_b, 1, 2, HG_HEADS, HG_DK, HG_DV)
        y = _out_proj(x2, mod3, seq, oh, oa, gates, w_ho_b, w_ao_b, w_out_b, nfw, w_ff1_b, w_ff2_b, fnw,
                      "out_latent" if latent else "out_ctx")
        outs.append(y.reshape(n_b, seq, D_MODEL))
    return (outs[0], outs[1], new_k, new_v, new_s)
```

```python
import functools

import numpy as np
import jax
import jax.numpy as jnp
from jax import lax
from jax.experimental import pallas as pl
from jax.experimental.pallas import tpu as pltpu

F32 = jnp.float32
BF16 = jnp.bfloat16

D_MODEL = 1024
GRID_W = 64
EPS = 1e-6
HG_HEADS = 4
HG_DK = 128
HG_DV = 128
HG_WIDTH = HG_HEADS * HG_DK
HG_CHUNK = 32
N_HEADS = 8
N_KV = 2
HEAD_DIM = 64
Q_WIDTH = N_HEADS * HEAD_DIM
KV_WIDTH = N_KV * HEAD_DIM
ROPE_THETA = 10000.0
D_FF = 4 * D_MODEL
N_MOD = 6
ZH_WIDTH = 5 * HG_WIDTH
D_IN = ZH_WIDTH + Q_WIDTH + 2 * KV_WIDTH + 2 * D_MODEL

V7X_VMEM_LIMIT_BYTES = 56 * 1024 * 1024
HG_BLOCK = 256
HG_HEADS_PER_STEP_CTX = 4
HG_HEADS_PER_STEP_LATENT = 2
COND_ROWS = 16
IN_TILE = 512
OUT_TILE = 256
ATTN_Q_TILE = 256


def _dot(a, b):
    return jnp.dot(a, b, preferred_element_type=F32)


def _dot_nt(a, b):
    return lax.dot_general(a, b, (((1,), (1,)), ((), ())), preferred_element_type=F32)


def _dot_tn(a, b):
    return lax.dot_general(a, b, (((0,), (0,)), ((), ())), preferred_element_type=F32)


def _split_bf16(x):
    hi = x.astype(BF16)
    lo = (x - hi.astype(F32)).astype(BF16)
    return hi, lo


def _params(n_grid):
    return pltpu.CompilerParams(dimension_semantics=("arbitrary",) * n_grid,
                                vmem_limit_bytes=V7X_VMEM_LIMIT_BYTES)


def _mod_spec(mod3, tile, seq_len):
    if mod3.shape[0] == 1:
        return pl.BlockSpec((1, 1, N_MOD * D_MODEL), lambda i: (0, 0, 0))
    assert seq_len % tile == 0
    per_seq = seq_len // tile
    return pl.BlockSpec((1, 1, N_MOD * D_MODEL), lambda i: (i // per_seq, 0, 0))


def _const_spec(shape):
    nd = len(shape)
    return pl.BlockSpec(shape, lambda *_: (0,) * nd, pipeline_mode=pl.Buffered(1))


def _mod_kernel(cond_ref, w_ref, b_ref, o_ref):
    c = cond_ref[...]
    x = c * jax.nn.sigmoid(c)
    x_hi, x_lo = _split_bf16(x)
    w_hi, w_lo = _split_bf16(w_ref[...])
    acc = _dot(x_hi, w_hi) + _dot(x_lo, w_hi) + _dot(x_hi, w_lo)
    o_ref[...] = acc + b_ref[...]


def _modulation(cond, w_ada, b_ada):
    n = N_MOD * D_MODEL
    tn = 768
    return pl.pallas_call(
        _mod_kernel,
        grid=(n // tn,),
        in_specs=[pl.BlockSpec((COND_ROWS, D_MODEL), lambda j: (0, 0)),
                  pl.BlockSpec((D_MODEL, tn), lambda j: (0, j)),
                  pl.BlockSpec((1, tn), lambda j: (0, j))],
        out_specs=pl.BlockSpec((COND_ROWS, tn), lambda j: (0, j)),
        out_shape=jax.ShapeDtypeStruct((COND_ROWS, n), F32),
        compiler_params=_params(1),
        name="adaln_mod",
    )(cond, w_ada, b_ada)


def _rms_rows(x, w):
    return x * lax.rsqrt(jnp.mean(x * x, axis=-1, keepdims=True) + EPS) * w


def _head_rms(a, ones_ref, w):
    hi, lo = _split_bf16(a * a)
    ss = _dot(hi, ones_ref[...]) + _dot(lo, ones_ref[...])
    return a * lax.rsqrt(ss * (1.0 / HEAD_DIM) + EPS) * w


def _rope(x, cos, s_up, s_dn):
    cols = []
    for j in range(x.shape[1] // 128):
        sl = slice(j * 128, (j + 1) * 128)
        xj = x[:, sl]
        cols.append(xj * cos[:, sl] + pltpu.roll(xj, 112, 1) * s_up[:, sl]
                    + pltpu.roll(xj, 16, 1) * s_dn[:, sl])
    return jnp.concatenate(cols, axis=1) if len(cols) > 1 else cols[0]


def _in_kernel(latent, x_ref, mod_ref, nw_ref, w_ref, qw_ref, kw_ref, oq_ref, ok_ref, *rest):
    if latent:
        cos_ref, sup_ref, sdn_ref, zh_ref, q_ref, k_ref, v_ref, g_ref = rest
    else:
        zh_ref, q_ref, k_ref, kf_ref, v_ref, g_ref = rest
    m = mod_ref[0]
    sh1 = m[:, 0:D_MODEL]
    sc1 = m[:, D_MODEL:2 * D_MODEL]
    h = _rms_rows(x_ref[...], nw_ref[...]) * (1.0 + sc1) + sh1
    hb = h.astype(BF16)
    c0 = ZH_WIDTH
    c1 = c0 + Q_WIDTH
    c2 = c1 + KV_WIDTH
    c3 = c2 + KV_WIDTH
    zh_ref[...] = _dot(hb, w_ref[:, 0:c0])
    qn = _head_rms(_dot(hb, w_ref[:, c0:c1]), oq_ref, qw_ref[...])
    kn = _head_rms(_dot(hb, w_ref[:, c1:c2]), ok_ref, kw_ref[...])
    v_ref[...] = _dot(hb, w_ref[:, c2:c3])
    g_ref[...] = jax.nn.sigmoid(_dot(hb, w_ref[:, c3:D_IN])).astype(BF16)
    if latent:
        cos, sup, sdn = cos_ref[...], sup_ref[...], sdn_ref[...]
        qn = _rope(qn, cos, sup, sdn)
        kn = _rope(kn, cos[:, 0:KV_WIDTH], sup[:, 0:KV_WIDTH], sdn[:, 0:KV_WIDTH])
    else:
        kf_ref[...] = kn
    q_ref[...] = (qn * HEAD_DIM ** -0.5).astype(BF16)
    k_ref[...] = kn.astype(BF16)


def _in_proj(x2, mod3, seq_len, norm_w, w_in_b, qw, kw, ones_q, ones_k, rope):
    t = x2.shape[0]
    tm = IN_TILE
    latent = rope is not None
    per_seq = max(seq_len // tm, 1)
    row = lambda i: (i, 0)
    in_specs = [pl.BlockSpec((tm, D_MODEL), row),
                _mod_spec(mod3, tm, seq_len),
                _const_spec((1, D_MODEL)),
                _const_spec((D_MODEL, D_IN)),
                _const_spec((1, Q_WIDTH)),
                _const_spec((1, KV_WIDTH)),
                _const_spec((Q_WIDTH, Q_WIDTH)),
                _const_spec((KV_WIDTH, KV_WIDTH))]
    args = [x2, mod3, norm_w, w_in_b, qw, kw, ones_q, ones_k]
    out_specs = [pl.BlockSpec((tm, ZH_WIDTH), row), pl.BlockSpec((tm, Q_WIDTH), row),
                 pl.BlockSpec((tm, KV_WIDTH), row)]
    out_shape = [jax.ShapeDtypeStruct((t, ZH_WIDTH), F32), jax.ShapeDtypeStruct((t, Q_WIDTH), BF16),
                 jax.ShapeDtypeStruct((t, KV_WIDTH), BF16)]
    if latent:
        in_specs += [pl.BlockSpec((tm, Q_WIDTH), lambda i: (i % per_seq, 0))] * 3
        args += list(rope)
    else:
        out_specs.append(pl.BlockSpec((tm, KV_WIDTH), row))
        out_shape.append(jax.ShapeDtypeStruct((t, KV_WIDTH), F32))
    out_specs += [pl.BlockSpec((tm, KV_WIDTH), row), pl.BlockSpec((tm, 2 * D_MODEL), row)]
    out_shape += [jax.ShapeDtypeStruct((t, KV_WIDTH), F32), jax.ShapeDtypeStruct((t, 2 * D_MODEL), BF16)]
    return pl.pallas_call(
        functools.partial(_in_kernel, latent),
        grid=(t // tm,),
        in_specs=in_specs,
        out_specs=out_specs,
        out_shape=out_shape,
        compiler_params=_params(1),
        name="in_proj_latent" if latent else "in_proj_ctx",
    )(*args)


def _attn_kernel(n_seg, q_ref, *refs):
    kv_refs, o_ref = refs[:2 * n_seg], refs[2 * n_seg]
    q = q_ref[...]
    ks = [kv_refs[2 * s][...].astype(BF16) for s in range(n_seg)]
    vts = [kv_refs[2 * s + 1][...].T.astype(BF16) for s in range(n_seg)]
    outs = []
    for h in range(N_HEADS):
        g = h // (N_HEADS // N_KV)
        qh = q[:, h * HEAD_DIM:(h + 1) * HEAD_DIM]
        gs = slice(g * HEAD_DIM, (g + 1) * HEAD_DIM)
        st = [_dot_nt(k[:, gs], qh) for k in ks]
        m = functools.reduce(jnp.maximum, [jnp.max(s, axis=0, keepdims=True) for s in st])
        pt = [jnp.exp(s - m) for s in st]
        denom = functools.reduce(jnp.add, [jnp.sum(p, axis=0, keepdims=True) for p in pt])
        ot = functools.reduce(jnp.add, [_dot(vt[gs, :], p.astype(BF16)) for vt, p in zip(vts, pt)])
        outs.append(ot / denom)
    o_ref[...] = jnp.concatenate(outs, axis=0).T.astype(BF16)


def _attention(q, kv_segments, n_batch, seq_len, name):
    t = q.shape[0]
    tq = ATTN_Q_TILE
    per_seq = seq_len // tq
    in_specs = [pl.BlockSpec((tq, Q_WIDTH), lambda b, j: (b * per_seq + j, 0))]
    args = [q]
    for k, v, lk in kv_segments:
        in_specs += [pl.BlockSpec((lk, KV_WIDTH), lambda b, j: (b, 0))] * 2
        args += [k, v]
    return pl.pallas_call(
        functools.partial(_attn_kernel, len(kv_segments)),
        grid=(n_batch, per_seq),
        in_specs=in_specs,
        out_specs=pl.BlockSpec((tq, Q_WIDTH), lambda b, j: (b * per_seq + j, 0)),
        out_shape=jax.ShapeDtypeStruct((t, Q_WIDTH), BF16),
        compiler_params=_params(2),
        name=name,
    )(*args)


def _hgrn_kernel(seq_len, hps, has_s0, has_sfin, lbl_ref, tf_ref, tb_ref, q_ref, ff_ref, fb_ref, v_ref, g_ref,
                 nw_ref, *rest):
    rest = list(rest)
    s0_ref = rest.pop(0) if has_s0 else None
    o_ref = rest.pop(0)
    sfin_ref = rest.pop(0) if has_sfin else None
    kv_scr, ss_scr, qd_scr, oi_scr, tot_scr = rest
    n_blk = seq_len // HG_BLOCK
    n_chunk = seq_len // HG_CHUNK
    per_blk = HG_BLOCK // HG_CHUNK

    logits = lbl_ref[...]
    e = jnp.exp(logits - jnp.max(logits, axis=1, keepdims=True))
    lb = e[:, 0, :] / (e[:, 0, :] + e[:, 1, :])

    def chunk_sums(t_ref, x):
        hi, lo = _split_bf16(x)
        r = _dot(t_ref[...], jnp.concatenate([hi, lo], axis=1))
        r = r[:, 0:HG_DK] + r[:, HG_DK:2 * HG_DK]
        return r[0:HG_BLOCK], r[HG_BLOCK:2 * HG_BLOCK]

    def gate(hf, lbv):
        f = lbv + (1.0 - lbv) * jax.nn.sigmoid(hf)
        return jnp.log(f), 1.0 - f

    def block_body(i, carry):
        rows = pl.ds(pl.multiple_of(i * HG_BLOCK, HG_BLOCK), HG_BLOCK)
        for h in range(hps):
            cols = slice(h * HG_DK, (h + 1) * HG_DK)
            q = q_ref[rows, cols] * HG_DK ** -0.5
            vb = v_ref[rows, cols].astype(BF16)
            logf_f, k_f = gate(ff_ref[rows, cols], lb[0:1, cols])
            logf_b, k_b = gate(fb_ref[rows, cols], lb[1:2, cols])
            a_f, tot_f = chunk_sums(tf_ref, logf_f)
            a_b, tot_b = chunk_sums(tb_ref, logf_b)
            qd_f = (q * jnp.exp(a_f)).astype(BF16)
            qd_b = (q * jnp.exp(a_b)).astype(BF16)
            kd_f = (k_f * jnp.exp(-a_f)).astype(BF16)
            kd_b = (k_b * jnp.exp(-a_b)).astype(BF16)
            ke = jnp.concatenate([k_f * jnp.exp(tot_f - a_f), k_b * jnp.exp(tot_b - a_b)], axis=1).astype(BF16)
            s_f = _dot_nt(qd_f, kd_f).astype(BF16) * tf_ref[0:HG_BLOCK, :]
            s_b = _dot_nt(qd_b, kd_b).astype(BF16) * tb_ref[0:HG_BLOCK, :]
            oi_scr[rows, cols] = _dot(s_f, vb) + _dot(s_b, vb)
            qd_scr[h, rows, :] = jnp.concatenate([qd_f, qd_b], axis=1)
            tot_scr[h, rows, :] = jnp.concatenate([tot_f, tot_b], axis=1)
            for c in range(per_blk):
                cr = slice(c * HG_CHUNK, (c + 1) * HG_CHUNK)
                kv_scr[h, i * per_blk + c] = _dot_tn(vb[cr, :], ke[cr, :])
        return carry

    lax.fori_loop(0, n_blk, block_body, 0)

    if has_s0:
        init = tuple(s0_ref[0, d, h].T for h in range(hps) for d in range(2))
    else:
        init = tuple(jnp.zeros((HG_DV, HG_DK), F32) for _ in range(2 * hps))

    def scan_body(i, carry):
        cf = i
        cb = n_chunk - 1 - i
        new = []
        for h in range(hps):
            s_f, s_b = carry[2 * h], carry[2 * h + 1]
            ss_scr[h, cf, :, 0:HG_DK] = s_f.astype(BF16)
            dec_f = jnp.exp(tot_scr[h, pl.ds(cf * HG_CHUNK, 1), 0:HG_DK])
            new.append(dec_f * s_f + kv_scr[h, cf, :, 0:HG_DK])
            ss_scr[h, cb, :, HG_DK:2 * HG_DK] = s_b.astype(BF16)
            dec_b = jnp.exp(tot_scr[h, pl.ds(cb * HG_CHUNK, 1), HG_DK:2 * HG_DK])
            new.append(dec_b * s_b + kv_scr[h, cb, :, HG_DK:2 * HG_DK])
        return tuple(new)

    final = lax.fori_loop(0, n_chunk, scan_body, init)
    if has_sfin:
        for h in range(hps):
            sfin_ref[0, 0, h] = final[2 * h].T
            sfin_ref[0, 1, h] = final[2 * h + 1].T

    def inter_body(i, carry):
        for h in range(hps):
            cols = slice(h * HG_DK, (h + 1) * HG_DK)
            for c in range(per_blk):
                rows = pl.ds(pl.multiple_of(i * HG_BLOCK + c * HG_CHUNK, HG_CHUNK), HG_CHUNK)
                oi_scr[rows, cols] = oi_scr[rows, cols] + _dot_nt(qd_scr[h, rows, :], ss_scr[h, i * per_blk + c])
        return carry

    lax.fori_loop(0, n_blk, inter_body, 0)

    for h in range(hps):
        cols = slice(h * HG_DK, (h + 1) * HG_DK)
        o = _rms_rows(oi_scr[:, cols], nw_ref[...])
        g = g_ref[:, cols]
        o_ref[:, cols] = (o * (g * jax.nn.sigmoid(g))).astype(BF16)


def _chunk_masks():
    r = np.arange(HG_BLOCK)
    same = (r[:, None] // HG_CHUNK) == (r[None, :] // HG_CHUNK)
    lower = same & (r[None, :] <= r[:, None])
    upper = same & (r[None, :] >= r[:, None])
    t_fwd = np.concatenate([lower, same], axis=0).astype(np.float32)
    t_bwd = np.concatenate([upper, same], axis=0).astype(np.float32)
    return jnp.asarray(t_fwd, dtype=BF16), jnp.asarray(t_bwd, dtype=BF16)


def _hgrn(zh, lb_logits, norm_w, n_batch, seq_len, hps, s0, want_final, name):
    t = zh.shape[0]
    n_chunk = seq_len // HG_CHUNK
    width = hps * HG_DK
    n_hb = HG_HEADS // hps
    t_fwd, t_bwd = _chunk_masks()

    def seg_spec(seg):
        return pl.BlockSpec((seq_len, width), lambda b, h: (b, seg * n_hb + h))

    state_spec = pl.BlockSpec((1, 2, hps, HG_DK, HG_DV), lambda b, h: (b, 0, h, 0, 0))
    mask_spec = pl.BlockSpec((2 * HG_BLOCK, HG_BLOCK), lambda b, h: (0, 0))
    in_specs = [pl.BlockSpec((2, lb_logits.shape[1], width), lambda b, h: (0, 0, h)),
                mask_spec, mask_spec,
                seg_spec(0), seg_spec(1), seg_spec(2), seg_spec(3), seg_spec(4),
                pl.BlockSpec((1, HG_DV), lambda b, h: (0, 0))]
    args = [lb_logits, t_fwd, t_bwd, zh, zh, zh, zh, zh, norm_w]
    if s0 is not None:
        in_specs.append(state_spec)
        args.append(s0)
    out_specs = [pl.BlockSpec((seq_len, width), lambda b, h: (b, h))]
    out_shape = [jax.ShapeDtypeStruct((t, HG_WIDTH), BF16)]
    if want_final:
        out_specs.append(state_spec)
        out_shape.append(jax.ShapeDtypeStruct((n_batch, 2, HG_HEADS, HG_DK, HG_DV), F32))
    return pl.pallas_call(
        functools.partial(_hgrn_kernel, seq_len, hps, s0 is not None, want_final),
        grid=(n_batch, n_hb),
        in_specs=in_specs,
        out_specs=out_specs,
        out_shape=out_shape,
        scratch_shapes=[pltpu.VMEM((hps, n_chunk, HG_DV, 2 * HG_DK), F32),
                        pltpu.VMEM((hps, n_chunk, HG_DV, 2 * HG_DK), BF16),
                        pltpu.VMEM((hps, seq_len, 2 * HG_DK), BF16),
                        pltpu.VMEM((seq_len, width), F32),
                        pltpu.VMEM((hps, seq_len, 2 * HG_DK), F32)],
        compiler_params=_params(2),
        name=name,
    )(*args)


def _out_kernel(x_ref, mod_ref, oh_ref, oa_ref, g_ref, who_ref, wao_ref, wout_ref, nfw_ref,
                wff1_ref, wff2_ref, fnw_ref, y_ref):
    m = mod_ref[0]
    g1 = m[:, 2 * D_MODEL:3 * D_MODEL]
    sh2 = m[:, 3 * D_MODEL:4 * D_MODEL]
    sc2 = m[:, 4 * D_MODEL:5 * D_MODEL]
    g2 = m[:, 5 * D_MODEL:6 * D_MODEL]
    gates = g_ref[...].astype(F32)
    merged = (gates[:, 0:D_MODEL] * _dot(oh_ref[...], who_ref[...])
              + gates[:, D_MODEL:2 * D_MODEL] * _dot(oa_ref[...], wao_ref[...]))
    x1 = x_ref[...] + g1 * _dot(merged.astype(BF16), wout_ref[...])
    h2 = (_rms_rows(x1, nfw_ref[...]) * (1.0 + sc2) + sh2).astype(BF16)
    acc = jnp.zeros_like(x1)
    for j in range(D_FF // D_MODEL):
        cols = slice(j * D_MODEL, (j + 1) * D_MODEL)
        hj = jnp.maximum(_dot(h2, wff1_ref[:, cols]), 0.0)
        acc = acc + _dot((hj * hj).astype(BF16), wff2_ref[cols, :])
    y_ref[...] = _rms_rows(x1 + g2 * acc, fnw_ref[...])


def _out_proj(x2, mod3, seq_len, oh, oa, gates, w_ho, w_ao, w_out, nfw, w_ff1, w_ff2, fnw, name):
    t = x2.shape[0]
    tm = OUT_TILE
    row = lambda i: (i, 0)
    return pl.pallas_call(
        _out_kernel,
        grid=(t // tm,),
        in_specs=[pl.BlockSpec((tm, D_MODEL), row),
                  _mod_spec(mod3, tm, seq_len),
                  pl.BlockSpec((tm, HG_WIDTH), row),
                  pl.BlockSpec((tm, Q_WIDTH), row),
                  pl.BlockSpec((tm, 2 * D_MODEL), row),
                  _const_spec((HG_WIDTH, D_MODEL)),
                  _const_spec((Q_WIDTH, D_MODEL)),
                  _const_spec((D_MODEL, D_MODEL)),
                  _const_spec((1, D_MODEL)),
                  _const_spec((D_MODEL, D_FF)),
                  _const_spec((D_FF, D_MODEL)),
                  _const_spec((1, D_MODEL))],
        out_specs=pl.BlockSpec((tm, D_MODEL), row),
        out_shape=jax.ShapeDtypeStruct((t, D_MODEL), F32),
        compiler_params=_params(1),
        name=name,
    )(x2, mod3, oh, oa, gates, w_ho, w_ao, w_out, nfw, w_ff1, w_ff2, fnw)


def _rope_tables(n_tokens):
    rows = n_tokens // GRID_W
    row = jnp.repeat(jnp.arange(rows, dtype=F32), GRID_W)
    col = jnp.tile(jnp.arange(GRID_W, dtype=F32), rows)
    axis_dim = HEAD_DIM // 2
    freqs = ROPE_THETA ** (-jnp.arange(0, axis_dim, 2, dtype=F32) / axis_dim)
    ang_r = row[:, None] * freqs
    ang_c = col[:, None] * freqs
    cr, sr, cc, sc = jnp.cos(ang_r), jnp.sin(ang_r), jnp.cos(ang_c), jnp.sin(ang_c)
    zero = jnp.zeros_like(sr)
    cos = jnp.concatenate([cr, cr, cc, cc], axis=-1)
    s_up = jnp.concatenate([-sr, zero, -sc, zero], axis=-1)
    s_dn = jnp.concatenate([zero, sr, zero, sc], axis=-1)
    return tuple(jnp.tile(a, (1, N_HEADS)) for a in (cos, s_up, s_dn))


def _block_ones(width):
    idx = jnp.arange(width) // HEAD_DIM
    return (idx[:, None] == idx[None, :]).astype(BF16)


def kernel(x_prompt, x_sample, cache_k, cache_v, state_hgrn, c, c_ctx, w_ada, b_ada, norm_mix_w, w_in, q_norm_w, k_norm_w, hgrn_lb_logits, hgrn_norm_w, w_hgrn_out, w_attn_out, w_out, norm_ffn_w, w_ff1, w_ff2, final_norm_w):
    n_p, l_p, _ = x_prompt.shape
    n_s, l_s, _ = x_sample.shape
    past = cache_k.shape[2]
    layer = 0

    cond = jnp.concatenate([c_ctx[None, :], c, jnp.zeros((COND_ROWS - 1 - n_s, D_MODEL), F32)], axis=0)
    mod = _modulation(cond, w_ada[layer], b_ada[layer][None, :])
    mod_p = mod[0:1].reshape(1, 1, N_MOD * D_MODEL)
    mod_s = mod[1:1 + n_s].reshape(n_s, 1, N_MOD * D_MODEL)

    w_in_b = w_in[layer].astype(BF16)
    w_ho_b = w_hgrn_out[layer].astype(BF16)
    w_ao_b = w_attn_out[layer].astype(BF16)
    w_out_b = w_out[layer].astype(BF16)
    w_ff1_b = w_ff1[layer].astype(BF16)
    w_ff2_b = w_ff2[layer].astype(BF16)
    nmw = norm_mix_w[layer][None, :]
    nfw = norm_ffn_w[layer][None, :]
    fnw = final_norm_w[None, :]
    qw = jnp.tile(q_norm_w[layer], N_HEADS)[None, :]
    kw = jnp.tile(k_norm_w[layer], N_KV)[None, :]
    hnw = hgrn_norm_w[layer][None, :]
    ones_q = _block_ones(Q_WIDTH)
    ones_k = _block_ones(KV_WIDTH)

    outs = []
    for latent in (False, True):
        if latent:
            x, n_b, seq, mod3 = x_sample, n_s, l_s, mod_s
        else:
            x, n_b, seq, mod3 = x_prompt, n_p, l_p, mod_p
        x2 = x.reshape(n_b * seq, D_MODEL)
        proj = _in_proj(x2, mod3, seq, nmw, w_in_b, qw, kw, ones_q, ones_k,
                        _rope_tables(seq) if latent else None)
        if latent:
            zh, q, k, v, gates = proj
            segs = [(cache_k[:, layer].reshape(n_b * past, KV_WIDTH),
                     cache_v[:, layer].reshape(n_b * past, KV_WIDTH), past), (k, v, seq)]
            oa = _attention(q, segs, n_b, seq, "attn_latent")
            (oh,) = _hgrn(zh, hgrn_lb_logits, hnw, n_b, seq, HG_HEADS_PER_STEP_LATENT, state_hgrn[:, layer], False,
                          "hgrn_latent")
        else:
            zh, q, k, k_f32, v, gates = proj
            oa = _attention(q, [(k, v, seq)], n_b, seq, "attn_ctx")
            oh, s_fin = _hgrn(zh, hgrn_lb_logits, hnw, n_b, seq, HG_HEADS_PER_STEP_CTX, None, True, "hgrn_ctx")
            new_k = k_f32.reshape(n_b, 1, seq, N_KV, HEAD_DIM)
            new_v = v.reshape(n_b, 1, seq, N_KV, HEAD_DIM)
            new_s = s_fin.reshape(n_b, 1, 2, HG_HEADS, HG_DK, HG_DV)
        y = _out_proj(x2, mod3, seq, oh, oa, gates, w_ho_b, w_ao_b, w_out_b, nfw, w_ff1_b, w_ff2_b, fnw,
                      "out_latent" if latent else "out_ctx")
        outs.append(y.reshape(n_b, seq, D_MODEL))
    return (outs[0], outs[1], new_k, new_v, new_s)
```

```python
import functools

import numpy as np
import jax
import jax.numpy as jnp
from jax import lax
from jax.experimental import pallas as pl
from jax.experimental.pallas import tpu as pltpu

F32 = jnp.float32
BF16 = jnp.bfloat16

D_MODEL = 1024
GRID_W = 64
EPS = 1e-6
HG_HEADS = 4
HG_DK = 128
HG_DV = 128
HG_WIDTH = HG_HEADS * HG_DK
HG_CHUNK = 32
N_HEADS = 8
N_KV = 2
HEAD_DIM = 64
Q_WIDTH = N_HEADS * HEAD_DIM
KV_WIDTH = N_KV * HEAD_DIM
ROPE_THETA = 10000.0
D_FF = 4 * D_MODEL
N_MOD = 6
ZH_WIDTH = 5 * HG_WIDTH
D_IN = ZH_WIDTH + Q_WIDTH + 2 * KV_WIDTH + 2 * D_MODEL

V7X_VMEM_LIMIT_BYTES = 56 * 1024 * 1024
HG_BLOCK = 256
HG_HEADS_PER_STEP_CTX = 4
HG_HEADS_PER_STEP_LATENT = 2
COND_ROWS = 16
ONES_ROWS = 16
LOG2_E = float(np.log2(np.e))
IN_TILE = 512
OUT_TILE = 256
ATTN_Q_TILE = 256
ATTN_LOOKAHEAD = 2


def _dot(a, b):
    return jnp.dot(a, b, preferred_element_type=F32)


def _dot_nt(a, b):
    return lax.dot_general(a, b, (((1,), (1,)), ((), ())), preferred_element_type=F32)


def _dot_tn(a, b):
    return lax.dot_general(a, b, (((0,), (0,)), ((), ())), preferred_element_type=F32)


def _split_bf16(x):
    hi = x.astype(BF16)
    lo = (x - hi.astype(F32)).astype(BF16)
    return hi, lo


def _params(n_grid):
    return pltpu.CompilerParams(dimension_semantics=("arbitrary",) * n_grid,
                                vmem_limit_bytes=V7X_VMEM_LIMIT_BYTES)


def _mod_spec(mod3, tile, seq_len):
    if mod3.shape[0] == 1:
        return pl.BlockSpec((1, 1, N_MOD * D_MODEL), lambda i: (0, 0, 0))
    assert seq_len % tile == 0
    per_seq = seq_len // tile
    return pl.BlockSpec((1, 1, N_MOD * D_MODEL), lambda i: (i // per_seq, 0, 0))


def _const_spec(shape):
    nd = len(shape)
    return pl.BlockSpec(shape, lambda *_: (0,) * nd, pipeline_mode=pl.Buffered(1))


def _mod_kernel(cond_ref, w_ref, b_ref, o_ref):
    c = cond_ref[...]
    x = c * jax.nn.sigmoid(c)
    x_hi, x_lo = _split_bf16(x)
    w_hi, w_lo = _split_bf16(w_ref[...])
    acc = _dot(x_hi, w_hi) + _dot(x_lo, w_hi) + _dot(x_hi, w_lo)
    o_ref[...] = acc + b_ref[...]


def _modulation(cond, w_ada, b_ada):
    n = N_MOD * D_MODEL
    tn = 768
    return pl.pallas_call(
        _mod_kernel,
        grid=(n // tn,),
        in_specs=[pl.BlockSpec((COND_ROWS, D_MODEL), lambda j: (0, 0)),
                  pl.BlockSpec((D_MODEL, tn), lambda j: (0, j)),
                  pl.BlockSpec((1, tn), lambda j: (0, j))],
        out_specs=pl.BlockSpec((COND_ROWS, tn), lambda j: (0, j)),
        out_shape=jax.ShapeDtypeStruct((COND_ROWS, n), F32),
        compiler_params=_params(1),
        name="adaln_mod",
    )(cond, w_ada, b_ada)


def _rms_rows(x, w):
    return x * lax.rsqrt(jnp.mean(x * x, axis=-1, keepdims=True) + EPS) * w


def _head_rms(a, ones_ref, w):
    hi, lo = _split_bf16(a * a)
    ss = _dot(hi, ones_ref[...]) + _dot(lo, ones_ref[...])
    return a * lax.rsqrt(ss * (1.0 / HEAD_DIM) + EPS) * w


def _rope(x, cos, s_up, s_dn):
    cols = []
    for j in range(x.shape[1] // 128):
        sl = slice(j * 128, (j + 1) * 128)
        xj = x[:, sl]
        cols.append(xj * cos[:, sl] + pltpu.roll(xj, 112, 1) * s_up[:, sl]
                    + pltpu.roll(xj, 16, 1) * s_dn[:, sl])
    return jnp.concatenate(cols, axis=1) if len(cols) > 1 else cols[0]


def _in_kernel(latent, x_ref, mod_ref, nw_ref, w_ref, qw_ref, kw_ref, oq_ref, ok_ref, *rest):
    if latent:
        cos_ref, sup_ref, sdn_ref, zh_ref, q_ref, k_ref, v_ref, g_ref = rest
    else:
        zh_ref, q_ref, k_ref, kf_ref, v_ref, g_ref = rest
    m = mod_ref[0]
    sh1 = m[:, 0:D_MODEL]
    sc1 = m[:, D_MODEL:2 * D_MODEL]
    h = _rms_rows(x_ref[...], nw_ref[...]) * (1.0 + sc1) + sh1
    hb = h.astype(BF16)
    c0 = ZH_WIDTH
    c1 = c0 + Q_WIDTH
    c2 = c1 + KV_WIDTH
    c3 = c2 + KV_WIDTH
    zh_ref[...] = _dot(hb, w_ref[:, 0:c0])
    qn = _head_rms(_dot(hb, w_ref[:, c0:c1]), oq_ref, qw_ref[...])
    kn = _head_rms(_dot(hb, w_ref[:, c1:c2]), ok_ref, kw_ref[...])
    v_ref[...] = _dot(hb, w_ref[:, c2:c3])
    g_ref[...] = jax.nn.sigmoid(_dot(hb, w_ref[:, c3:D_IN])).astype(BF16)
    if latent:
        cos, sup, sdn = cos_ref[...], sup_ref[...], sdn_ref[...]
        qn = _rope(qn, cos, sup, sdn)
        kn = _rope(kn, cos[:, 0:KV_WIDTH], sup[:, 0:KV_WIDTH], sdn[:, 0:KV_WIDTH])
    else:
        kf_ref[...] = kn
    q_ref[...] = (qn * (HEAD_DIM ** -0.5 * LOG2_E)).astype(BF16)
    k_ref[...] = kn.astype(BF16)


def _in_proj(x2, mod3, seq_len, norm_w, w_in_b, qw, kw, ones_q, ones_k, rope):
    t = x2.shape[0]
    tm = IN_TILE
    latent = rope is not None
    per_seq = max(seq_len // tm, 1)
    row = lambda i: (i, 0)
    in_specs = [pl.BlockSpec((tm, D_MODEL), row),
                _mod_spec(mod3, tm, seq_len),
                _const_spec((1, D_MODEL)),
                _const_spec((D_MODEL, D_IN)),
                _const_spec((1, Q_WIDTH)),
                _const_spec((1, KV_WIDTH)),
                _const_spec((Q_WIDTH, Q_WIDTH)),
                _const_spec((KV_WIDTH, KV_WIDTH))]
    args = [x2, mod3, norm_w, w_in_b, qw, kw, ones_q, ones_k]
    out_specs = [pl.BlockSpec((tm, ZH_WIDTH), row), pl.BlockSpec((tm, Q_WIDTH), row),
                 pl.BlockSpec((tm, KV_WIDTH), row)]
    out_shape = [jax.ShapeDtypeStruct((t, ZH_WIDTH), F32), jax.ShapeDtypeStruct((t, Q_WIDTH), BF16),
                 jax.ShapeDtypeStruct((t, KV_WIDTH), BF16)]
    if latent:
        in_specs += [pl.BlockSpec((tm, Q_WIDTH), lambda i: (i % per_seq, 0))] * 3
        args += list(rope)
    else:
        out_specs.append(pl.BlockSpec((tm, KV_WIDTH), row))
        out_shape.append(jax.ShapeDtypeStruct((t, KV_WIDTH), F32))
    out_specs += [pl.BlockSpec((tm, KV_WIDTH), row), pl.BlockSpec((tm, 2 * D_MODEL), row)]
    out_shape += [jax.ShapeDtypeStruct((t, KV_WIDTH), F32), jax.ShapeDtypeStruct((t, 2 * D_MODEL), BF16)]
    return pl.pallas_call(
        functools.partial(_in_kernel, latent),
        grid=(t // tm,),
        in_specs=in_specs,
        out_specs=out_specs,
        out_shape=out_shape,
        compiler_params=_params(1),
        name="in_proj_latent" if latent else "in_proj_ctx",
    )(*args)


def _attn_kernel(n_seg, q_ref, *refs):
    kv_refs, o_ref = refs[:2 * n_seg], refs[2 * n_seg]
    q = q_ref[...]
    ks = [kv_refs[2 * s][...].astype(BF16) for s in range(n_seg)]
    vts = [kv_refs[2 * s + 1][...].T.astype(BF16) for s in range(n_seg)]
    vts = [[jnp.concatenate([vt[g * HEAD_DIM:(g + 1) * HEAD_DIM, :],
                             jnp.ones((ONES_ROWS, vt.shape[1]), BF16)], axis=0) for g in range(N_KV)]
           for vt in vts]

    def scores(h):
        g = h // (N_HEADS // N_KV)
        qh = q[:, h * HEAD_DIM:(h + 1) * HEAD_DIM]
        return [_dot_nt(k[:, g * HEAD_DIM:(g + 1) * HEAD_DIM], qh) for k in ks]

    pending = [scores(h) for h in range(ATTN_LOOKAHEAD)]
    outs = []
    for h in range(N_HEADS):
        if h + ATTN_LOOKAHEAD < N_HEADS:
            pending.append(scores(h + ATTN_LOOKAHEAD))
        st = pending.pop(0)
        g = h // (N_HEADS // N_KV)
        m = functools.reduce(jnp.maximum, [jnp.max(s, axis=0, keepdims=True) for s in st])
        ot = functools.reduce(jnp.add, [_dot(vt[g], jnp.exp2(s - m).astype(BF16)) for vt, s in zip(vts, st)])
        outs.append(ot[0:HEAD_DIM, :] / ot[HEAD_DIM:HEAD_DIM + 1, :])
    o_ref[...] = jnp.concatenate(outs, axis=0).T.astype(BF16)


def _attention(q, kv_segments, n_batch, seq_len, name):
    t = q.shape[0]
    tq = ATTN_Q_TILE
    per_seq = seq_len // tq
    in_specs = [pl.BlockSpec((tq, Q_WIDTH), lambda b, j: (b * per_seq + j, 0))]
    args = [q]
    for k, v, lk in kv_segments:
        in_specs += [pl.BlockSpec((lk, KV_WIDTH), lambda b, j: (b, 0))] * 2
        args += [k, v]
    return pl.pallas_call(
        functools.partial(_attn_kernel, len(kv_segments)),
        grid=(n_batch, per_seq),
        in_specs=in_specs,
        out_specs=pl.BlockSpec((tq, Q_WIDTH), lambda b, j: (b * per_seq + j, 0)),
        out_shape=jax.ShapeDtypeStruct((t, Q_WIDTH), BF16),
        compiler_params=_params(2),
        name=name,
    )(*args)


def _hgrn_kernel(seq_len, hps, has_s0, has_sfin, lbl_ref, tf_ref, tb_ref, q_ref, ff_ref, fb_ref, v_ref, g_ref,
                 nw_ref, *rest):
    rest = list(rest)
    s0_ref = rest.pop(0) if has_s0 else None
    o_ref = rest.pop(0)
    sfin_ref = rest.pop(0) if has_sfin else None
    kv_scr, ss_scr, qd_scr, oi_scr, tot_scr = rest
    n_blk = seq_len // HG_BLOCK
    n_chunk = seq_len // HG_CHUNK
    per_blk = HG_BLOCK // HG_CHUNK

    logits = lbl_ref[...]
    e = jnp.exp(logits - jnp.max(logits, axis=1, keepdims=True))
    lb = e[:, 0, :] / (e[:, 0, :] + e[:, 1, :])

    def chunk_sums(t_ref, x):
        hi, lo = _split_bf16(x)
        r = _dot(t_ref[...], jnp.concatenate([hi, lo], axis=1))
        r = r[:, 0:HG_DK] + r[:, HG_DK:2 * HG_DK]
        return r[0:HG_BLOCK], r[HG_BLOCK:2 * HG_BLOCK]

    def gate(hf, lbv):
        f = lbv + (1.0 - lbv) * jax.nn.sigmoid(hf)
        return jnp.log(f), 1.0 - f

    def block_body(i, carry):
        rows = pl.ds(pl.multiple_of(i * HG_BLOCK, HG_BLOCK), HG_BLOCK)
        for h in range(hps):
            cols = slice(h * HG_DK, (h + 1) * HG_DK)
            q = q_ref[rows, cols] * HG_DK ** -0.5
            vb = v_ref[rows, cols].astype(BF16)
            logf_f, k_f = gate(ff_ref[rows, cols], lb[0:1, cols])
            logf_b, k_b = gate(fb_ref[rows, cols], lb[1:2, cols])
            a_f, tot_f = chunk_sums(tf_ref, logf_f)
            a_b, tot_b = chunk_sums(tb_ref, logf_b)
            qd_f = (q * jnp.exp(a_f)).astype(BF16)
            qd_b = (q * jnp.exp(a_b)).astype(BF16)
            kd_f = (k_f * jnp.exp(-a_f)).astype(BF16)
            kd_b = (k_b * jnp.exp(-a_b)).astype(BF16)
            ke = jnp.concatenate([k_f * jnp.exp(tot_f - a_f), k_b * jnp.exp(tot_b - a_b)], axis=1).astype(BF16)
            s_f = _dot_nt(qd_f, kd_f).astype(BF16) * tf_ref[0:HG_BLOCK, :]
            s_b = _dot_nt(qd_b, kd_b).astype(BF16) * tb_ref[0:HG_BLOCK, :]
            oi_scr[rows, cols] = _dot(s_f, vb) + _dot(s_b, vb)
            qd_scr[h, rows, :] = jnp.concatenate([qd_f, qd_b], axis=1)
            tot_scr[h, rows, :] = jnp.concatenate([tot_f, tot_b], axis=1)
            for c in range(per_blk):
                cr = slice(c * HG_CHUNK, (c + 1) * HG_CHUNK)
                kv_scr[h, i * per_blk + c] = _dot_tn(vb[cr, :], ke[cr, :])
        return carry

    lax.fori_loop(0, n_blk, block_body, 0)

    if has_s0:
        init = tuple(s0_ref[0, d, h].T for h in range(hps) for d in range(2))
    else:
        init = tuple(jnp.zeros((HG_DV, HG_DK), F32) for _ in range(2 * hps))

    def scan_body(i, carry):
        cf = i
        cb = n_chunk - 1 - i
        new = []
        for h in range(hps):
            s_f, s_b = carry[2 * h], carry[2 * h + 1]
            ss_scr[h, cf, :, 0:HG_DK] = s_f.astype(BF16)
            dec_f = jnp.exp(tot_scr[h, pl.ds(cf * HG_CHUNK, 1), 0:HG_DK])
            new.append(dec_f * s_f + kv_scr[h, cf, :, 0:HG_DK])
            ss_scr[h, cb, :, HG_DK:2 * HG_DK] = s_b.astype(BF16)
            dec_b = jnp.exp(tot_scr[h, pl.ds(cb * HG_CHUNK, 1), HG_DK:2 * HG_DK])
            new.append(dec_b * s_b + kv_scr[h, cb, :, HG_DK:2 * HG_DK])
        return tuple(new)

    final = lax.fori_loop(0, n_chunk, scan_body, init)
    if has_sfin:
        for h in range(hps):
            sfin_ref[0, 0, h] = final[2 * h].T
            sfin_ref[0, 1, h] = final[2 * h + 1].T

    def inter_body(i, carry):
        for h in range(hps):
            cols = slice(h * HG_DK, (h + 1) * HG_DK)
            for c in range(per_blk):
                rows = pl.ds(pl.multiple_of(i * HG_BLOCK + c * HG_CHUNK, HG_CHUNK), HG_CHUNK)
                oi_scr[rows, cols] = oi_scr[rows, cols] + _dot_nt(qd_scr[h, rows, :], ss_scr[h, i * per_blk + c])
        return carry

    lax.fori_loop(0, n_blk, inter_body, 0)

    for h in range(hps):
        cols = slice(h * HG_DK, (h + 1) * HG_DK)
        o = _rms_rows(oi_scr[:, cols], nw_ref[...])
        g = g_ref[:, cols]
        o_ref[:, cols] = (o * (g * jax.nn.sigmoid(g))).astype(BF16)


def _chunk_masks():
    r = np.arange(HG_BLOCK)
    same = (r[:, None] // HG_CHUNK) == (r[None, :] // HG_CHUNK)
    lower = same & (r[None, :] <= r[:, None])
    upper = same & (r[None, :] >= r[:, None])
    t_fwd = np.concatenate([lower, same], axis=0).astype(np.float32)
    t_bwd = np.concatenate([upper, same], axis=0).astype(np.float32)
    return jnp.asarray(t_fwd, dtype=BF16), jnp.asarray(t_bwd, dtype=BF16)


def _hgrn(zh, lb_logits, norm_w, n_batch, seq_len, hps, s0, want_final, name):
    t = zh.shape[0]
    n_chunk = seq_len // HG_CHUNK
    width = hps * HG_DK
    n_hb = HG_HEADS // hps
    t_fwd, t_bwd = _chunk_masks()

    def seg_spec(seg):
        return pl.BlockSpec((seq_len, width), lambda b, h: (b, seg * n_hb + h))

    state_spec = pl.BlockSpec((1, 2, hps, HG_DK, HG_DV), lambda b, h: (b, 0, h, 0, 0))
    mask_spec = pl.BlockSpec((2 * HG_BLOCK, HG_BLOCK), lambda b, h: (0, 0))
    in_specs = [pl.BlockSpec((2, lb_logits.shape[1], width), lambda b, h: (0, 0, h)),
                mask_spec, mask_spec,
                seg_spec(0), seg_spec(1), seg_spec(2), seg_spec(3), seg_spec(4),
                pl.BlockSpec((1, HG_DV), lambda b, h: (0, 0))]
    args = [lb_logits, t_fwd, t_bwd, zh, zh, zh, zh, zh, norm_w]
    if s0 is not None:
        in_specs.append(state_spec)
        args.append(s0)
    out_specs = [pl.BlockSpec((seq_len, width), lambda b, h: (b, h))]
    out_shape = [jax.ShapeDtypeStruct((t, HG_WIDTH), BF16)]
    if want_final:
        out_specs.append(state_spec)
        out_shape.append(jax.ShapeDtypeStruct((n_batch, 2, HG_HEADS, HG_DK, HG_DV), F32))
    return pl.pallas_call(
        functools.partial(_hgrn_kernel, seq_len, hps, s0 is not None, want_final),
        grid=(n_batch, n_hb),
        in_specs=in_specs,
        out_specs=out_specs,
        out_shape=out_shape,
        scratch_shapes=[pltpu.VMEM((hps, n_chunk, HG_DV, 2 * HG_DK), F32),
                        pltpu.VMEM((hps, n_chunk, HG_DV, 2 * HG_DK), BF16),
                        pltpu.VMEM((hps, seq_len, 2 * HG_DK), BF16),
                        pltpu.VMEM((seq_len, width), F32),
                        pltpu.VMEM((hps, seq_len, 2 * HG_DK), F32)],
        compiler_params=_params(2),
        name=name,
    )(*args)


def _out_kernel(x_ref, mod_ref, oh_ref, oa_ref, g_ref, who_ref, wao_ref, wout_ref, nfw_ref,
                wff1_ref, wff2_ref, fnw_ref, y_ref):
    m = mod_ref[0]
    g1 = m[:, 2 * D_MODEL:3 * D_MODEL]
    sh2 = m[:, 3 * D_MODEL:4 * D_MODEL]
    sc2 = m[:, 4 * D_MODEL:5 * D_MODEL]
    g2 = m[:, 5 * D_MODEL:6 * D_MODEL]
    gates = g_ref[...].astype(F32)
    merged = (gates[:, 0:D_MODEL] * _dot(oh_ref[...], who_ref[...])
              + gates[:, D_MODEL:2 * D_MODEL] * _dot(oa_ref[...], wao_ref[...]))
    x1 = x_ref[...] + g1 * _dot(merged.astype(BF16), wout_ref[...])
    h2 = (_rms_rows(x1, nfw_ref[...]) * (1.0 + sc2) + sh2).astype(BF16)
    acc = jnp.zeros_like(x1)
    for j in range(D_FF // D_MODEL):
        cols = slice(j * D_MODEL, (j + 1) * D_MODEL)
        hj = jnp.maximum(_dot(h2, wff1_ref[:, cols]), 0.0)
        acc = acc + _dot((hj * hj).astype(BF16), wff2_ref[cols, :])
    y_ref[...] = _rms_rows(x1 + g2 * acc, fnw_ref[...])


def _out_proj(x2, mod3, seq_len, oh, oa, gates, w_ho, w_ao, w_out, nfw, w_ff1, w_ff2, fnw, name):
    t = x2.shape[0]
    tm = OUT_TILE
    row = lambda i: (i, 0)
    return pl.pallas_call(
        _out_kernel,
        grid=(t // tm,),
        in_specs=[pl.BlockSpec((tm, D_MODEL), row),
                  _mod_spec(mod3, tm, seq_len),
                  pl.BlockSpec((tm, HG_WIDTH), row),
                  pl.BlockSpec((tm, Q_WIDTH), row),
                  pl.BlockSpec((tm, 2 * D_MODEL), row),
                  _const_spec((HG_WIDTH, D_MODEL)),
                  _const_spec((Q_WIDTH, D_MODEL)),
                  _const_spec((D_MODEL, D_MODEL)),
                  _const_spec((1, D_MODEL)),
                  _const_spec((D_MODEL, D_FF)),
                  _const_spec((D_FF, D_MODEL)),
                  _const_spec((1, D_MODEL))],
        out_specs=pl.BlockSpec((tm, D_MODEL), row),
        out_shape=jax.ShapeDtypeStruct((t, D_MODEL), F32),
        compiler_params=_params(1),
        name=name,
    )(x2, mod3, oh, oa, gates, w_ho, w_ao, w_out, nfw, w_ff1, w_ff2, fnw)


def _rope_tables(n_tokens):
    rows = n_tokens // GRID_W
    row = jnp.repeat(jnp.arange(rows, dtype=F32), GRID_W)
    col = jnp.tile(jnp.arange(GRID_W, dtype=F32), rows)
    axis_dim = HEAD_DIM // 2
    freqs = ROPE_THETA ** (-jnp.arange(0, axis_dim, 2, dtype=F32) / axis_dim)
    ang_r = row[:, None] * freqs
    ang_c = col[:, None] * freqs
    cr, sr, cc, sc = jnp.cos(ang_r), jnp.sin(ang_r), jnp.cos(ang_c), jnp.sin(ang_c)
    zero = jnp.zeros_like(sr)
    cos = jnp.concatenate([cr, cr, cc, cc], axis=-1)
    s_up = jnp.concatenate([-sr, zero, -sc, zero], axis=-1)
    s_dn = jnp.concatenate([zero, sr, zero, sc], axis=-1)
    return tuple(jnp.tile(a, (1, N_HEADS)) for a in (cos, s_up, s_dn))


def _block_ones(width):
    idx = jnp.arange(width) // HEAD_DIM
    return (idx[:, None] == idx[None, :]).astype(BF16)


def kernel(x_prompt, x_sample, cache_k, cache_v, state_hgrn, c, c_ctx, w_ada, b_ada, norm_mix_w, w_in, q_norm_w, k_norm_w, hgrn_lb_logits, hgrn_norm_w, w_hgrn_out, w_attn_out, w_out, norm_ffn_w, w_ff1, w_ff2, final_norm_w):
    n_p, l_p, _ = x_prompt.shape
    n_s, l_s, _ = x_sample.shape
    past = cache_k.shape[2]
    layer = 0

    cond = jnp.concatenate([c_ctx[None, :], c, jnp.zeros((COND_ROWS - 1 - n_s, D_MODEL), F32)], axis=0)
    mod = _modulation(cond, w_ada[layer], b_ada[layer][None, :])
    mod_p = mod[0:1].reshape(1, 1, N_MOD * D_MODEL)
    mod_s = mod[1:1 + n_s].reshape(n_s, 1, N_MOD * D_MODEL)

    w_in_b = w_in[layer].astype(BF16)
    w_ho_b = w_hgrn_out[layer].astype(BF16)
    w_ao_b = w_attn_out[layer].astype(BF16)
    w_out_b = w_out[layer].astype(BF16)
    w_ff1_b = w_ff1[layer].astype(BF16)
    w_ff2_b = w_ff2[layer].astype(BF16)
    nmw = norm_mix_w[layer][None, :]
    nfw = norm_ffn_w[layer][None, :]
    fnw = final_norm_w[None, :]
    qw = jnp.tile(q_norm_w[layer], N_HEADS)[None, :]
    kw = jnp.tile(k_norm_w[layer], N_KV)[None, :]
    hnw = hgrn_norm_w[layer][None, :]
    ones_q = _block_ones(Q_WIDTH)
    ones_k = _block_ones(KV_WIDTH)

    outs = []
    for latent in (False, True):
        if latent:
            x, n_b, seq, mod3 = x_sample, n_s, l_s, mod_s
        else:
            x, n_b, seq, mod3 = x_prompt, n_p, l_p, mod_p
        x2 = x.reshape(n_b * seq, D_MODEL)
        proj = _in_proj(x2, mod3, seq, nmw, w_in_b, qw, kw, ones_q, ones_k,
                        _rope_tables(seq) if latent else None)
        if latent:
            zh, q, k, v, gates = proj
            segs = [(cache_k[:, layer].reshape(n_b * past, KV_WIDTH),
                     cache_v[:, layer].reshape(n_b * past, KV_WIDTH), past), (k, v, seq)]
            oa = _attention(q, segs, n_b, seq, "attn_latent")
            (oh,) = _hgrn(zh, hgrn_lb_logits, hnw, n_b, seq, HG_HEADS_PER_STEP_LATENT, state_hgrn[:, layer], False,
                          "hgrn_latent")
        else:
            zh, q, k, k_f32, v, gates = proj
            oa = _attention(q, [(k, v, seq)], n_b, seq, "attn_ctx")
            oh, s_fin = _hgrn(zh, hgrn_lb_logits, hnw, n_b, seq, HG_HEADS_PER_STEP_CTX, None, True, "hgrn_ctx")
            new_k = k_f32.reshape(n_b, 1, seq, N_KV, HEAD_DIM)
            new_v = v.reshape(n_b, 1, seq, N_KV, HEAD_DIM)
            new_s = s_fin.reshape(n_b, 1, 2, HG_HEADS, HG_DK, HG_DV)
        y = _out_proj(x2, mod3, seq, oh, oa, gates, w_ho_b, w_ao_b, w_out_b, nfw, w_ff1_b, w_ff2_b, fnw,
                      "out_latent" if latent else "out_ctx")
        outs.append(y.reshape(n_b, seq, D_MODEL))
    return (outs[0], outs[1], new_k, new_v, new_s)
```

```python
import functools

import numpy as np
import jax
import jax.numpy as jnp
from jax import lax
from jax.experimental import pallas as pl
from jax.experimental.pallas import tpu as pltpu

F32 = jnp.float32
BF16 = jnp.bfloat16

D_MODEL = 1024
GRID_W = 64
EPS = 1e-6
HG_HEADS = 4
HG_DK = 128
HG_DV = 128
HG_WIDTH = HG_HEADS * HG_DK
HG_CHUNK = 32
N_HEADS = 8
N_KV = 2
HEAD_DIM = 64
Q_WIDTH = N_HEADS * HEAD_DIM
KV_WIDTH = N_KV * HEAD_DIM
ROPE_THETA = 10000.0
D_FF = 4 * D_MODEL
N_MOD = 6
ZH_WIDTH = 5 * HG_WIDTH
D_IN = ZH_WIDTH + Q_WIDTH + 2 * KV_WIDTH + 2 * D_MODEL

V7X_VMEM_LIMIT_BYTES = 56 * 1024 * 1024
HG_BLOCK = 256
HG_HEADS_PER_STEP_CTX = 4
HG_HEADS_PER_STEP_LATENT = 2
COND_ROWS = 16
ONES_ROWS = 16
LOG2_E = float(np.log2(np.e))
IN_TILE = 512
OUT_TILE = 256
ATTN_Q_TILE = 256
ATTN_LOOKAHEAD = 2


def _dot(a, b):
    return jnp.dot(a, b, preferred_element_type=F32)


def _dot_nt(a, b):
    return lax.dot_general(a, b, (((1,), (1,)), ((), ())), preferred_element_type=F32)


def _dot_tn(a, b):
    return lax.dot_general(a, b, (((0,), (0,)), ((), ())), preferred_element_type=F32)


def _split_bf16(x):
    hi = x.astype(BF16)
    lo = (x - hi.astype(F32)).astype(BF16)
    return hi, lo


def _params(n_grid):
    return pltpu.CompilerParams(dimension_semantics=("arbitrary",) * n_grid,
                                vmem_limit_bytes=V7X_VMEM_LIMIT_BYTES)


def _mod_spec(mod3, tile, seq_len):
    if mod3.shape[0] == 1:
        return pl.BlockSpec((1, 1, N_MOD * D_MODEL), lambda i: (0, 0, 0))
    assert seq_len % tile == 0
    per_seq = seq_len // tile
    return pl.BlockSpec((1, 1, N_MOD * D_MODEL), lambda i: (i // per_seq, 0, 0))


def _const_spec(shape):
    nd = len(shape)
    return pl.BlockSpec(shape, lambda *_: (0,) * nd, pipeline_mode=pl.Buffered(1))


def _mod_kernel(cond_ref, w_ref, b_ref, o_ref):
    c = cond_ref[...]
    x = c * jax.nn.sigmoid(c)
    x_hi, x_lo = _split_bf16(x)
    w_hi, w_lo = _split_bf16(w_ref[...])
    acc = _dot(x_hi, w_hi) + _dot(x_lo, w_hi) + _dot(x_hi, w_lo)
    o_ref[...] = acc + b_ref[...]


def _modulation(cond, w_ada, b_ada):
    n = N_MOD * D_MODEL
    tn = 768
    return pl.pallas_call(
        _mod_kernel,
        grid=(n // tn,),
        in_specs=[pl.BlockSpec((COND_ROWS, D_MODEL), lambda j: (0, 0)),
                  pl.BlockSpec((D_MODEL, tn), lambda j: (0, j)),
                  pl.BlockSpec((1, tn), lambda j: (0, j))],
        out_specs=pl.BlockSpec((COND_ROWS, tn), lambda j: (0, j)),
        out_shape=jax.ShapeDtypeStruct((COND_ROWS, n), F32),
        compiler_params=_params(1),
        name="adaln_mod",
    )(cond, w_ada, b_ada)


def _rms_rows(x, w):
    return x * lax.rsqrt(jnp.mean(x * x, axis=-1, keepdims=True) + EPS) * w


def _head_sumsq(a, ones_ref):
    return _dot((a * a).astype(BF16), ones_ref[...])


def _head_rms(a, sumsq, w):
    return a * lax.rsqrt(sumsq * (1.0 / HEAD_DIM) + EPS) * w


def _rope(x, cos, s_up, s_dn):
    cols = []
    for j in range(x.shape[1] // 128):
        sl = slice(j * 128, (j + 1) * 128)
        xj = x[:, sl]
        cols.append(xj * cos[:, sl] + pltpu.roll(xj, 112, 1) * s_up[:, sl]
                    + pltpu.roll(xj, 16, 1) * s_dn[:, sl])
    return jnp.concatenate(cols, axis=1) if len(cols) > 1 else cols[0]


def _in_kernel(latent, x_ref, mod_ref, nw_ref, w_ref, qw_ref, kw_ref, oq_ref, ok_ref, *rest):
    if latent:
        cos_ref, sup_ref, sdn_ref, zh_ref, q_ref, k_ref, vt_ref, g_ref = rest
    else:
        zh_ref, q_ref, k_ref, kt_ref, vt_ref, g_ref = rest
    m = mod_ref[0]
    sh1 = m[:, 0:D_MODEL]
    sc1 = m[:, D_MODEL:2 * D_MODEL]
    h = _rms_rows(x_ref[...], nw_ref[...]) * (1.0 + sc1) + sh1
    hb = h.astype(BF16)
    c0 = ZH_WIDTH
    c1 = c0 + Q_WIDTH
    c2 = c1 + KV_WIDTH
    c3 = c2 + KV_WIDTH
    aq = _dot(hb, w_ref[:, c0:c1])
    ak = _dot(hb, w_ref[:, c1:c2])
    av = _dot(hb, w_ref[:, c2:c3])
    gl = _dot(hb, w_ref[:, c3:D_IN])
    ssq = _head_sumsq(aq, oq_ref)
    ssk = _head_sumsq(ak, ok_ref)
    zh_ref[...] = _dot(hb, w_ref[:, 0:c0])
    g_ref[...] = jax.nn.sigmoid(gl).astype(BF16)
    qn = _head_rms(aq, ssq, qw_ref[...])
    kn = _head_rms(ak, ssk, kw_ref[...])
    seq = vt_ref.shape[2]
    avt = av.T
    for s in range(vt_ref.shape[0]):
        vt_ref[s] = avt[:, s * seq:(s + 1) * seq].astype(vt_ref.dtype)
    if latent:
        cos, sup, sdn = cos_ref[...], sup_ref[...], sdn_ref[...]
        qn = _rope(qn, cos, sup, sdn)
        kn = _rope(kn, cos[:, 0:KV_WIDTH], sup[:, 0:KV_WIDTH], sdn[:, 0:KV_WIDTH])
    else:
        knt = kn.T
        for s in range(kt_ref.shape[0]):
            kt_ref[s] = knt[:, s * seq:(s + 1) * seq]
    q_ref[...] = (qn * (HEAD_DIM ** -0.5 * LOG2_E)).astype(BF16)
    k_ref[...] = kn.astype(BF16)


def _in_proj(x2, mod3, seq_len, norm_w, w_in_b, qw, kw, ones_q, ones_k, rope):
    t = x2.shape[0]
    tm = IN_TILE
    latent = rope is not None
    per_seq = max(seq_len // tm, 1)
    row = lambda i: (i, 0)
    in_specs = [pl.BlockSpec((tm, D_MODEL), row),
                _mod_spec(mod3, tm, seq_len),
                _const_spec((1, D_MODEL)),
                _const_spec((D_MODEL, D_IN)),
                _const_spec((1, Q_WIDTH)),
                _const_spec((1, KV_WIDTH)),
                _const_spec((Q_WIDTH, Q_WIDTH)),
                _const_spec((KV_WIDTH, KV_WIDTH))]
    args = [x2, mod3, norm_w, w_in_b, qw, kw, ones_q, ones_k]
    out_specs = [pl.BlockSpec((tm, ZH_WIDTH), row), pl.BlockSpec((tm, Q_WIDTH), row),
                 pl.BlockSpec((tm, KV_WIDTH), row)]
    out_shape = [jax.ShapeDtypeStruct((t, ZH_WIDTH), F32), jax.ShapeDtypeStruct((t, Q_WIDTH), BF16),
                 jax.ShapeDtypeStruct((t, KV_WIDTH), BF16)]
    if seq_len <= tm:
        assert tm % seq_len == 0
        t_spec = pl.BlockSpec((tm // seq_len, KV_WIDTH, seq_len), lambda i: (i, 0, 0))
    else:
        t_spec = pl.BlockSpec((1, KV_WIDTH, tm), lambda i: (i // per_seq, 0, i % per_seq))
    if latent:
        in_specs += [pl.BlockSpec((tm, Q_WIDTH), lambda i: (i % per_seq, 0))] * 3
        args += list(rope)
        out_specs.append(t_spec)
        out_shape.append(jax.ShapeDtypeStruct((t // seq_len, KV_WIDTH, seq_len), BF16))
    else:
        out_specs += [t_spec, t_spec]
        out_shape += [jax.ShapeDtypeStruct((t // seq_len, KV_WIDTH, seq_len), F32)] * 2
    out_specs.append(pl.BlockSpec((tm, 2 * D_MODEL), row))
    out_shape.append(jax.ShapeDtypeStruct((t, 2 * D_MODEL), BF16))
    return pl.pallas_call(
        functools.partial(_in_kernel, latent),
        grid=(t // tm,),
        in_specs=in_specs,
        out_specs=out_specs,
        out_shape=out_shape,
        compiler_params=_params(1),
        name="in_proj_latent" if latent else "in_proj_ctx",
    )(*args)


def _attn_kernel(k_transposed, q_ref, *refs):
    n_seg = len(k_transposed)
    kv_refs, o_ref = refs[:2 * n_seg], refs[2 * n_seg]
    q = q_ref[...]
    ks = [(kv_refs[2 * s][0].T if k_transposed[s] else kv_refs[2 * s][...]).astype(BF16) for s in range(n_seg)]
    vts = [kv_refs[2 * s + 1][0].astype(BF16) for s in range(n_seg)]
    vts = [[jnp.concatenate([vt[g * HEAD_DIM:(g + 1) * HEAD_DIM, :],
                             jnp.ones((ONES_ROWS, vt.shape[1]), BF16)], axis=0) for g in range(N_KV)]
           for vt in vts]

    def scores(h):
        g = h // (N_HEADS // N_KV)
        qh = q[:, h * HEAD_DIM:(h + 1) * HEAD_DIM]
        return [_dot_nt(k[:, g * HEAD_DIM:(g + 1) * HEAD_DIM], qh) for k in ks]

    pending = [scores(h) for h in range(ATTN_LOOKAHEAD)]
    outs = []
    for h in range(N_HEADS):
        if h + ATTN_LOOKAHEAD < N_HEADS:
            pending.append(scores(h + ATTN_LOOKAHEAD))
        st = pending.pop(0)
        g = h // (N_HEADS // N_KV)
        m = functools.reduce(jnp.maximum, [jnp.max(s, axis=0, keepdims=True) for s in st])
        ot = functools.reduce(jnp.add, [_dot(vt[g], jnp.exp2(s - m).astype(BF16)) for vt, s in zip(vts, st)])
        outs.append(ot[0:HEAD_DIM, :] / ot[HEAD_DIM:HEAD_DIM + 1, :])
    o_ref[...] = jnp.concatenate(outs, axis=0).T.astype(BF16)


def _attention(q, kv_segments, n_batch, seq_len, name):
    t = q.shape[0]
    tq = ATTN_Q_TILE
    per_seq = seq_len // tq
    in_specs = [pl.BlockSpec((tq, Q_WIDTH), lambda b, j: (b * per_seq + j, 0))]
    args = [q]
    for k, v_t in kv_segments:
        slab = pl.BlockSpec((1,) + v_t.shape[1:], lambda b, j: (b, 0, 0))
        k_spec = slab if k.ndim == 3 else pl.BlockSpec((v_t.shape[2], KV_WIDTH), lambda b, j: (b, 0))
        in_specs += [k_spec, slab]
        args += [k, v_t]
    return pl.pallas_call(
        functools.partial(_attn_kernel, tuple(k.ndim == 3 for k, _ in kv_segments)),
        grid=(n_batch, per_seq),
        in_specs=in_specs,
        out_specs=pl.BlockSpec((tq, Q_WIDTH), lambda b, j: (b * per_seq + j, 0)),
        out_shape=jax.ShapeDtypeStruct((t, Q_WIDTH), BF16),
        compiler_params=_params(2),
        name=name,
    )(*args)


def _hgrn_kernel(seq_len, hps, has_s0, has_sfin, lbl_ref, tf_ref, tb_ref, q_ref, ff_ref, fb_ref, v_ref, g_ref,
                 nw_ref, *rest):
    rest = list(rest)
    s0_ref = rest.pop(0) if has_s0 else None
    o_ref = rest.pop(0)
    sfin_ref = rest.pop(0) if has_sfin else None
    kv_scr, ss_scr, qd_scr, oi_scr, tot_scr = rest
    n_blk = seq_len // HG_BLOCK
    n_chunk = seq_len // HG_CHUNK
    per_blk = HG_BLOCK // HG_CHUNK

    logits = lbl_ref[...]
    e = jnp.exp(logits - jnp.max(logits, axis=1, keepdims=True))
    lb = e[:, 0, :] / (e[:, 0, :] + e[:, 1, :])

    def chunk_sums(t_ref, x):
        hi, lo = _split_bf16(x)
        r = _dot(t_ref[...], jnp.concatenate([hi, lo], axis=1))
        r = r[:, 0:HG_DK] + r[:, HG_DK:2 * HG_DK]
        return r[0:HG_BLOCK], r[HG_BLOCK:2 * HG_BLOCK]

    def gate(hf, lbv):
        f = lbv + (1.0 - lbv) * jax.nn.sigmoid(hf)
        return jnp.log(f), 1.0 - f

    def block_body(i, carry):
        rows = pl.ds(pl.multiple_of(i * HG_BLOCK, HG_BLOCK), HG_BLOCK)
        for h in range(hps):
            cols = slice(h * HG_DK, (h + 1) * HG_DK)
            q = q_ref[rows, cols] * HG_DK ** -0.5
            vb = v_ref[rows, cols].astype(BF16)
            logf_f, k_f = gate(ff_ref[rows, cols], lb[0:1, cols])
            logf_b, k_b = gate(fb_ref[rows, cols], lb[1:2, cols])
            a_f, tot_f = chunk_sums(tf_ref, logf_f)
            a_b, tot_b = chunk_sums(tb_ref, logf_b)
            qd_f = (q * jnp.exp(a_f)).astype(BF16)
            qd_b = (q * jnp.exp(a_b)).astype(BF16)
            kd_f = (k_f * jnp.exp(-a_f)).astype(BF16)
            kd_b = (k_b * jnp.exp(-a_b)).astype(BF16)
            ke = jnp.concatenate([k_f * jnp.exp(tot_f - a_f), k_b * jnp.exp(tot_b - a_b)], axis=1).astype(BF16)
            s_f = _dot_nt(qd_f, kd_f).astype(BF16) * tf_ref[0:HG_BLOCK, :]
            s_b = _dot_nt(qd_b, kd_b).astype(BF16) * tb_ref[0:HG_BLOCK, :]
            oi_scr[rows, cols] = _dot(s_f, vb) + _dot(s_b, vb)
            qd_scr[h, rows, :] = jnp.concatenate([qd_f, qd_b], axis=1)
            tot_scr[h, rows, :] = jnp.concatenate([tot_f, tot_b], axis=1)
            for c in range(per_blk):
                cr = slice(c * HG_CHUNK, (c + 1) * HG_CHUNK)
                kv_scr[h, i * per_blk + c] = _dot_tn(vb[cr, :], ke[cr, :])
        return carry

    lax.fori_loop(0, n_blk, block_body, 0)

    if has_s0:
        init = tuple(s0_ref[0, d, h].T for h in range(hps) for d in range(2))
    else:
        init = tuple(jnp.zeros((HG_DV, HG_DK), F32) for _ in range(2 * hps))

    def scan_body(i, carry):
        cf = i
        cb = n_chunk - 1 - i
        new = []
        for h in range(hps):
            s_f, s_b = carry[2 * h], carry[2 * h + 1]
            ss_scr[h, cf, :, 0:HG_DK] = s_f.astype(BF16)
            dec_f = jnp.exp(tot_scr[h, pl.ds(cf * HG_CHUNK, 1), 0:HG_DK])
            new.append(dec_f * s_f + kv_scr[h, cf, :, 0:HG_DK])
            ss_scr[h, cb, :, HG_DK:2 * HG_DK] = s_b.astype(BF16)
            dec_b = jnp.exp(tot_scr[h, pl.ds(cb * HG_CHUNK, 1), HG_DK:2 * HG_DK])
            new.append(dec_b * s_b + kv_scr[h, cb, :, HG_DK:2 * HG_DK])
        return tuple(new)

    final = lax.fori_loop(0, n_chunk, scan_body, init)
    if has_sfin:
        for h in range(hps):
            sfin_ref[0, 0, h] = final[2 * h].T
            sfin_ref[0, 1, h] = final[2 * h + 1].T

    def inter_body(i, carry):
        for h in range(hps):
            cols = slice(h * HG_DK, (h + 1) * HG_DK)
            for c in range(per_blk):
                rows = pl.ds(pl.multiple_of(i * HG_BLOCK + c * HG_CHUNK, HG_CHUNK), HG_CHUNK)
                oi_scr[rows, cols] = oi_scr[rows, cols] + _dot_nt(qd_scr[h, rows, :], ss_scr[h, i * per_blk + c])
        return carry

    lax.fori_loop(0, n_blk, inter_body, 0)

    for h in range(hps):
        cols = slice(h * HG_DK, (h + 1) * HG_DK)
        o = _rms_rows(oi_scr[:, cols], nw_ref[...])
        g = g_ref[:, cols]
        o_ref[:, cols] = (o * (g * jax.nn.sigmoid(g))).astype(BF16)


def _chunk_masks():
    r = np.arange(HG_BLOCK)
    same = (r[:, None] // HG_CHUNK) == (r[None, :] // HG_CHUNK)
    lower = same & (r[None, :] <= r[:, None])
    upper = same & (r[None, :] >= r[:, None])
    t_fwd = np.concatenate([lower, same], axis=0).astype(np.float32)
    t_bwd = np.concatenate([upper, same], axis=0).astype(np.float32)
    return jnp.asarray(t_fwd, dtype=BF16), jnp.asarray(t_bwd, dtype=BF16)


def _hgrn(zh, lb_logits, norm_w, n_batch, seq_len, hps, s0, want_final, name):
    t = zh.shape[0]
    n_chunk = seq_len // HG_CHUNK
    width = hps * HG_DK
    n_hb = HG_HEADS // hps
    t_fwd, t_bwd = _chunk_masks()

    def seg_spec(seg):
        return pl.BlockSpec((seq_len, width), lambda b, h: (b, seg * n_hb + h))

    state_spec = pl.BlockSpec((1, 2, hps, HG_DK, HG_DV), lambda b, h: (b, 0, h, 0, 0))
    mask_spec = pl.BlockSpec((2 * HG_BLOCK, HG_BLOCK), lambda b, h: (0, 0))
    in_specs = [pl.BlockSpec((2, lb_logits.shape[1], width), lambda b, h: (0, 0, h)),
                mask_spec, mask_spec,
                seg_spec(0), seg_spec(1), seg_spec(2), seg_spec(3), seg_spec(4),
                pl.BlockSpec((1, HG_DV), lambda b, h: (0, 0))]
    args = [lb_logits, t_fwd, t_bwd, zh, zh, zh, zh, zh, norm_w]
    if s0 is not None:
        in_specs.append(state_spec)
        args.append(s0)
    out_specs = [pl.BlockSpec((seq_len, width), lambda b, h: (b, h))]
    out_shape = [jax.ShapeDtypeStruct((t, HG_WIDTH), BF16)]
    if want_final:
        out_specs.append(state_spec)
        out_shape.append(jax.ShapeDtypeStruct((n_batch, 2, HG_HEADS, HG_DK, HG_DV), F32))
    return pl.pallas_call(
        functools.partial(_hgrn_kernel, seq_len, hps, s0 is not None, want_final),
        grid=(n_batch, n_hb),
        in_specs=in_specs,
        out_specs=out_specs,
        out_shape=out_shape,
        scratch_shapes=[pltpu.VMEM((hps, n_chunk, HG_DV, 2 * HG_DK), F32),
                        pltpu.VMEM((hps, n_chunk, HG_DV, 2 * HG_DK), BF16),
                        pltpu.VMEM((hps, seq_len, 2 * HG_DK), BF16),
                        pltpu.VMEM((seq_len, width), F32),
                        pltpu.VMEM((hps, seq_len, 2 * HG_DK), F32)],
        compiler_params=_params(2),
        name=name,
    )(*args)


def _out_kernel(x_ref, mod_ref, oh_ref, oa_ref, g_ref, who_ref, wao_ref, wout_ref, nfw_ref,
                wff1_ref, wff2_ref, fnw_ref, y_ref):
    m = mod_ref[0]
    g1 = m[:, 2 * D_MODEL:3 * D_MODEL]
    sh2 = m[:, 3 * D_MODEL:4 * D_MODEL]
    sc2 = m[:, 4 * D_MODEL:5 * D_MODEL]
    g2 = m[:, 5 * D_MODEL:6 * D_MODEL]
    gates = g_ref[...].astype(F32)
    merged = (gates[:, 0:D_MODEL] * _dot(oh_ref[...], who_ref[...])
              + gates[:, D_MODEL:2 * D_MODEL] * _dot(oa_ref[...], wao_ref[...]))
    x1 = x_ref[...] + g1 * _dot(merged.astype(BF16), wout_ref[...])
    h2 = (_rms_rows(x1, nfw_ref[...]) * (1.0 + sc2) + sh2).astype(BF16)
    acc = jnp.zeros_like(x1)
    for j in range(D_FF // D_MODEL):
        cols = slice(j * D_MODEL, (j + 1) * D_MODEL)
        hj = jnp.maximum(_dot(h2, wff1_ref[:, cols]), 0.0)
        acc = acc + _dot((hj * hj).astype(BF16), wff2_ref[cols, :])
    y_ref[...] = _rms_rows(x1 + g2 * acc, fnw_ref[...])


def _out_proj(x2, mod3, seq_len, oh, oa, gates, w_ho, w_ao, w_out, nfw, w_ff1, w_ff2, fnw, name):
    t = x2.shape[0]
    tm = OUT_TILE
    row = lambda i: (i, 0)
    return pl.pallas_call(
        _out_kernel,
        grid=(t // tm,),
        in_specs=[pl.BlockSpec((tm, D_MODEL), row),
                  _mod_spec(mod3, tm, seq_len),
                  pl.BlockSpec((tm, HG_WIDTH), row),
                  pl.BlockSpec((tm, Q_WIDTH), row),
                  pl.BlockSpec((tm, 2 * D_MODEL), row),
                  _const_spec((HG_WIDTH, D_MODEL)),
                  _const_spec((Q_WIDTH, D_MODEL)),
                  _const_spec((D_MODEL, D_MODEL)),
                  _const_spec((1, D_MODEL)),
                  _const_spec((D_MODEL, D_FF)),
                  _const_spec((D_FF, D_MODEL)),
                  _const_spec((1, D_MODEL))],
        out_specs=pl.BlockSpec((tm, D_MODEL), row),
        out_shape=jax.ShapeDtypeStruct((t, D_MODEL), F32),
        compiler_params=_params(1),
        name=name,
    )(x2, mod3, oh, oa, gates, w_ho, w_ao, w_out, nfw, w_ff1, w_ff2, fnw)


def _rope_tables(n_tokens):
    rows = n_tokens // GRID_W
    row = jnp.repeat(jnp.arange(rows, dtype=F32), GRID_W)
    col = jnp.tile(jnp.arange(GRID_W, dtype=F32), rows)
    axis_dim = HEAD_DIM // 2
    freqs = ROPE_THETA ** (-jnp.arange(0, axis_dim, 2, dtype=F32) / axis_dim)
    ang_r = row[:, None] * freqs
    ang_c = col[:, None] * freqs
    cr, sr, cc, sc = jnp.cos(ang_r), jnp.sin(ang_r), jnp.cos(ang_c), jnp.sin(ang_c)
    zero = jnp.zeros_like(sr)
    cos = jnp.concatenate([cr, cr, cc, cc], axis=-1)
    s_up = jnp.concatenate([-sr, zero, -sc, zero], axis=-1)
    s_dn = jnp.concatenate([zero, sr, zero, sc], axis=-1)
    return tuple(jnp.tile(a, (1, N_HEADS)) for a in (cos, s_up, s_dn))


def _to_slab(kv):
    n, length = kv.shape[:2]
    return jnp.transpose(kv, (0, 2, 3, 1)).reshape(n, KV_WIDTH, length)


def _from_slab(slab):
    n, _, length = slab.shape
    return jnp.transpose(slab.reshape(n, N_KV, HEAD_DIM, length), (0, 3, 1, 2))


def _block_ones(width):
    idx = jnp.arange(width) // HEAD_DIM
    return (idx[:, None] == idx[None, :]).astype(BF16)


def kernel(x_prompt, x_sample, cache_k, cache_v, state_hgrn, c, c_ctx, w_ada, b_ada, norm_mix_w, w_in, q_norm_w, k_norm_w, hgrn_lb_logits, hgrn_norm_w, w_hgrn_out, w_attn_out, w_out, norm_ffn_w, w_ff1, w_ff2, final_norm_w):
    n_p, l_p, _ = x_prompt.shape
    n_s, l_s, _ = x_sample.shape
    past = cache_k.shape[2]
    layer = 0

    cond = jnp.concatenate([c_ctx[None, :], c, jnp.zeros((COND_ROWS - 1 - n_s, D_MODEL), F32)], axis=0)
    mod = _modulation(cond, w_ada[layer], b_ada[layer][None, :])
    mod_p = mod[0:1].reshape(1, 1, N_MOD * D_MODEL)
    mod_s = mod[1:1 + n_s].reshape(n_s, 1, N_MOD * D_MODEL)

    w_in_b = w_in[layer].astype(BF16)
    w_ho_b = w_hgrn_out[layer].astype(BF16)
    w_ao_b = w_attn_out[layer].astype(BF16)
    w_out_b = w_out[layer].astype(BF16)
    w_ff1_b = w_ff1[layer].astype(BF16)
    w_ff2_b = w_ff2[layer].astype(BF16)
    nmw = norm_mix_w[layer][None, :]
    nfw = norm_ffn_w[layer][None, :]
    fnw = final_norm_w[None, :]
    qw = jnp.tile(q_norm_w[layer], N_HEADS)[None, :]
    kw = jnp.tile(k_norm_w[layer], N_KV)[None, :]
    hnw = hgrn_norm_w[layer][None, :]
    ones_q = _block_ones(Q_WIDTH)
    ones_k = _block_ones(KV_WIDTH)

    outs = []
    for latent in (False, True):
        if latent:
            x, n_b, seq, mod3 = x_sample, n_s, l_s, mod_s
        else:
            x, n_b, seq, mod3 = x_prompt, n_p, l_p, mod_p
        x2 = x.reshape(n_b * seq, D_MODEL)
        proj = _in_proj(x2, mod3, seq, nmw, w_in_b, qw, kw, ones_q, ones_k,
                        _rope_tables(seq) if latent else None)
        if latent:
            zh, q, k, v_t, gates = proj
            segs = [(_to_slab(cache_k[:, layer]), _to_slab(cache_v[:, layer])), (k, v_t)]
            oa = _attention(q, segs, n_b, seq, "attn_latent")
            (oh,) = _hgrn(zh, hgrn_lb_logits, hnw, n_b, seq, HG_HEADS_PER_STEP_LATENT, state_hgrn[:, layer], False,
                          "hgrn_latent")
        else:
            zh, q, k, k_t, v_t, gates = proj
            oa = _attention(q, [(k, v_t)], n_b, seq, "attn_ctx")
            oh, s_fin = _hgrn(zh, hgrn_lb_logits, hnw, n_b, seq, HG_HEADS_PER_STEP_CTX, None, True, "hgrn_ctx")
            new_k = _from_slab(k_t)[:, None]
            new_v = _from_slab(v_t)[:, None]
            new_s = s_fin.reshape(n_b, 1, 2, HG_HEADS, HG_DK, HG_DV)
        y = _out_proj(x2, mod3, seq, oh, oa, gates, w_ho_b, w_ao_b, w_out_b, nfw, w_ff1_b, w_ff2_b, fnw,
                      "out_latent" if latent else "out_ctx")
        outs.append(y.reshape(n_b, seq, D_MODEL))
    return (outs[0], outs[1], new_k, new_v, new_s)
```

```python
import functools

import numpy as np
import jax
import jax.numpy as jnp
from jax import lax
from jax.experimental import pallas as pl
from jax.experimental.pallas import tpu as pltpu

F32 = jnp.float32
BF16 = jnp.bfloat16

D_MODEL = 1024
GRID_W = 64
EPS = 1e-6
HG_HEADS = 4
HG_DK = 128
HG_DV = 128
HG_WIDTH = HG_HEADS * HG_DK
HG_CHUNK = 32
N_HEADS = 8
N_KV = 2
HEAD_DIM = 64
Q_WIDTH = N_HEADS * HEAD_DIM
KV_WIDTH = N_KV * HEAD_DIM
ROPE_THETA = 10000.0
D_FF = 4 * D_MODEL
N_MOD = 6
ZH_WIDTH = 5 * HG_WIDTH
D_IN = ZH_WIDTH + Q_WIDTH + 2 * KV_WIDTH + 2 * D_MODEL

V7X_VMEM_LIMIT_BYTES = 56 * 1024 * 1024
SUBLANES = 8
HG_BLOCK = 256
HG_HEADS_PER_STEP_CTX = 4
HG_HEADS_PER_STEP_LATENT = 2
HG_UNITS_PER_TRIP = 4
HG_SCAN_UNROLL = 4
COND_ROWS = 16
ONES_ROWS = 16
LOG2_E = float(np.log2(np.e))
IN_TILE = 512
OUT_TILE = 256
ATTN_Q_TILE = 256
ATTN_LOOKAHEAD = 2


def _dot(a, b):
    return jnp.dot(a, b, preferred_element_type=F32)


def _dot_nt(a, b):
    return lax.dot_general(a, b, (((1,), (1,)), ((), ())), preferred_element_type=F32)


def _dot_tn(a, b):
    return lax.dot_general(a, b, (((0,), (0,)), ((), ())), preferred_element_type=F32)


def _split_bf16(x):
    hi = x.astype(BF16)
    lo = (x - hi.astype(F32)).astype(BF16)
    return hi, lo


def _params(n_grid):
    return pltpu.CompilerParams(dimension_semantics=("arbitrary",) * n_grid,
                                vmem_limit_bytes=V7X_VMEM_LIMIT_BYTES)


def _mod_spec(mod3, tile, seq_len):
    if mod3.shape[0] == 1:
        return pl.BlockSpec((1, 1, N_MOD * D_MODEL), lambda i: (0, 0, 0))
    assert seq_len % tile == 0
    per_seq = seq_len // tile
    return pl.BlockSpec((1, 1, N_MOD * D_MODEL), lambda i: (i // per_seq, 0, 0))


def _const_spec(shape):
    nd = len(shape)
    return pl.BlockSpec(shape, lambda *_: (0,) * nd, pipeline_mode=pl.Buffered(1))


def _mod_kernel(cond_ref, w_ref, b_ref, o_ref):
    c = cond_ref[...]
    x = c * jax.nn.sigmoid(c)
    x_hi, x_lo = _split_bf16(x)
    w_hi, w_lo = _split_bf16(w_ref[...])
    acc = _dot(x_hi, w_hi) + _dot(x_lo, w_hi) + _dot(x_hi, w_lo)
    o_ref[...] = acc + b_ref[...]


def _modulation(cond, w_ada, b_ada):
    n = N_MOD * D_MODEL
    tn = 768
    return pl.pallas_call(
        _mod_kernel,
        grid=(n // tn,),
        in_specs=[pl.BlockSpec((COND_ROWS, D_MODEL), lambda j: (0, 0)),
                  pl.BlockSpec((D_MODEL, tn), lambda j: (0, j)),
                  pl.BlockSpec((1, tn), lambda j: (0, j))],
        out_specs=pl.BlockSpec((COND_ROWS, tn), lambda j: (0, j)),
        out_shape=jax.ShapeDtypeStruct((COND_ROWS, n), F32),
        compiler_params=_params(1),
        name="adaln_mod",
    )(cond, w_ada, b_ada)


def _rms_rows(x, w):
    return x * lax.rsqrt(jnp.mean(x * x, axis=-1, keepdims=True) + EPS) * w


def _head_sumsq(a, ones_ref):
    return _dot((a * a).astype(BF16), ones_ref[...])


def _head_rms(a, sumsq, w):
    return a * lax.rsqrt(sumsq * (1.0 / HEAD_DIM) + EPS) * w


def _rope(x, cos, s_up, s_dn):
    cols = []
    for j in range(x.shape[1] // 128):
        sl = slice(j * 128, (j + 1) * 128)
        xj = x[:, sl]
        cols.append(xj * cos[:, sl] + pltpu.roll(xj, 112, 1) * s_up[:, sl]
                    + pltpu.roll(xj, 16, 1) * s_dn[:, sl])
    return jnp.concatenate(cols, axis=1) if len(cols) > 1 else cols[0]


def _in_kernel(latent, x_ref, mod_ref, nw_ref, w_ref, qw_ref, kw_ref, oq_ref, ok_ref, *rest):
    if latent:
        cos_ref, sup_ref, sdn_ref, zh_ref, q_ref, k_ref, vt_ref, g_ref = rest
    else:
        zh_ref, q_ref, k_ref, kt_ref, vt_ref, g_ref = rest
    m = mod_ref[0]
    sh1 = m[:, 0:D_MODEL]
    sc1 = m[:, D_MODEL:2 * D_MODEL]
    h = _rms_rows(x_ref[...], nw_ref[...]) * (1.0 + sc1) + sh1
    hb = h.astype(BF16)
    c0 = ZH_WIDTH
    c1 = c0 + Q_WIDTH
    c2 = c1 + KV_WIDTH
    c3 = c2 + KV_WIDTH
    aq = _dot(hb, w_ref[:, c0:c1])
    ak = _dot(hb, w_ref[:, c1:c2])
    av = _dot(hb, w_ref[:, c2:c3])
    gl = _dot(hb, w_ref[:, c3:D_IN])
    ssq = _head_sumsq(aq, oq_ref)
    ssk = _head_sumsq(ak, ok_ref)
    zh_ref[...] = _dot(hb, w_ref[:, 0:c0])
    g_ref[...] = jax.nn.sigmoid(gl).astype(BF16)
    qn = _head_rms(aq, ssq, qw_ref[...])
    kn = _head_rms(ak, ssk, kw_ref[...])
    seq = vt_ref.shape[2]
    avt = av.T
    for s in range(vt_ref.shape[0]):
        vt_ref[s] = avt[:, s * seq:(s + 1) * seq].astype(vt_ref.dtype)
    if latent:
        cos, sup, sdn = cos_ref[...], sup_ref[...], sdn_ref[...]
        qn = _rope(qn, cos, sup, sdn)
        kn = _rope(kn, cos[:, 0:KV_WIDTH], sup[:, 0:KV_WIDTH], sdn[:, 0:KV_WIDTH])
    else:
        knt = kn.T
        for s in range(kt_ref.shape[0]):
            kt_ref[s] = knt[:, s * seq:(s + 1) * seq]
    q_ref[...] = (qn * (HEAD_DIM ** -0.5 * LOG2_E)).astype(BF16)
    k_ref[...] = kn.astype(BF16)


def _in_proj(x2, mod3, seq_len, norm_w, w_in_b, qw, kw, ones_q, ones_k, rope):
    t = x2.shape[0]
    tm = IN_TILE
    latent = rope is not None
    per_seq = max(seq_len // tm, 1)
    row = lambda i: (i, 0)
    in_specs = [pl.BlockSpec((tm, D_MODEL), row),
                _mod_spec(mod3, tm, seq_len),
                _const_spec((1, D_MODEL)),
                _const_spec((D_MODEL, D_IN)),
                _const_spec((1, Q_WIDTH)),
                _const_spec((1, KV_WIDTH)),
                _const_spec((Q_WIDTH, Q_WIDTH)),
                _const_spec((KV_WIDTH, KV_WIDTH))]
    args = [x2, mod3, norm_w, w_in_b, qw, kw, ones_q, ones_k]
    out_specs = [pl.BlockSpec((tm, ZH_WIDTH), row), pl.BlockSpec((tm, Q_WIDTH), row),
                 pl.BlockSpec((tm, KV_WIDTH), row)]
    out_shape = [jax.ShapeDtypeStruct((t, ZH_WIDTH), F32), jax.ShapeDtypeStruct((t, Q_WIDTH), BF16),
                 jax.ShapeDtypeStruct((t, KV_WIDTH), BF16)]
    if seq_len <= tm:
        assert tm % seq_len == 0
        t_spec = pl.BlockSpec((tm // seq_len, KV_WIDTH, seq_len), lambda i: (i, 0, 0))
    else:
        t_spec = pl.BlockSpec((1, KV_WIDTH, tm), lambda i: (i // per_seq, 0, i % per_seq))
    if latent:
        in_specs += [pl.BlockSpec((tm, Q_WIDTH), lambda i: (i % per_seq, 0))] * 3
        args += list(rope)
        out_specs.append(t_spec)
        out_shape.append(jax.ShapeDtypeStruct((t // seq_len, KV_WIDTH, seq_len), BF16))
    else:
        out_specs += [t_spec, t_spec]
        out_shape += [jax.ShapeDtypeStruct((t // seq_len, KV_WIDTH, seq_len), F32)] * 2
    out_specs.append(pl.BlockSpec((tm, 2 * D_MODEL), row))
    out_shape.append(jax.ShapeDtypeStruct((t, 2 * D_MODEL), BF16))
    return pl.pallas_call(
        functools.partial(_in_kernel, latent),
        grid=(t // tm,),
        in_specs=in_specs,
        out_specs=out_specs,
        out_shape=out_shape,
        compiler_params=_params(1),
        name="in_proj_latent" if latent else "in_proj_ctx",
    )(*args)


def _attn_kernel(k_transposed, q_ref, *refs):
    n_seg = len(k_transposed)
    kv_refs, o_ref = refs[:2 * n_seg], refs[2 * n_seg]
    q = q_ref[...]
    ks = [(kv_refs[2 * s][0].T if k_transposed[s] else kv_refs[2 * s][...]).astype(BF16) for s in range(n_seg)]
    vts = [kv_refs[2 * s + 1][0].astype(BF16) for s in range(n_seg)]
    vts = [[jnp.concatenate([vt[g * HEAD_DIM:(g + 1) * HEAD_DIM, :],
                             jnp.ones((ONES_ROWS, vt.shape[1]), BF16)], axis=0) for g in range(N_KV)]
           for vt in vts]

    def scores(h):
        g = h // (N_HEADS // N_KV)
        qh = q[:, h * HEAD_DIM:(h + 1) * HEAD_DIM]
        return [_dot_nt(k[:, g * HEAD_DIM:(g + 1) * HEAD_DIM], qh) for k in ks]

    pending = [scores(h) for h in range(ATTN_LOOKAHEAD)]
    outs = []
    for h in range(N_HEADS):
        if h + ATTN_LOOKAHEAD < N_HEADS:
            pending.append(scores(h + ATTN_LOOKAHEAD))
        st = pending.pop(0)
        g = h // (N_HEADS // N_KV)
        m = functools.reduce(jnp.maximum, [jnp.max(s, axis=0, keepdims=True) for s in st])
        ot = functools.reduce(jnp.add, [_dot(vt[g], jnp.exp2(s - m).astype(BF16)) for vt, s in zip(vts, st)])
        outs.append(ot[0:HEAD_DIM, :] / ot[HEAD_DIM:HEAD_DIM + 1, :])
    o_ref[...] = jnp.concatenate(outs, axis=0).T.astype(BF16)


def _attention(q, kv_segments, n_batch, seq_len, name):
    t = q.shape[0]
    tq = ATTN_Q_TILE
    per_seq = seq_len // tq
    in_specs = [pl.BlockSpec((tq, Q_WIDTH), lambda b, j: (b * per_seq + j, 0))]
    args = [q]
    for k, v_t in kv_segments:
        slab = pl.BlockSpec((1,) + v_t.shape[1:], lambda b, j: (b, 0, 0))
        k_spec = slab if k.ndim == 3 else pl.BlockSpec((v_t.shape[2], KV_WIDTH), lambda b, j: (b, 0))
        in_specs += [k_spec, slab]
        args += [k, v_t]
    return pl.pallas_call(
        functools.partial(_attn_kernel, tuple(k.ndim == 3 for k, _ in kv_segments)),
        grid=(n_batch, per_seq),
        in_specs=in_specs,
        out_specs=pl.BlockSpec((tq, Q_WIDTH), lambda b, j: (b * per_seq + j, 0)),
        out_shape=jax.ShapeDtypeStruct((t, Q_WIDTH), BF16),
        compiler_params=_params(2),
        name=name,
    )(*args)


def _hgrn_kernel(seq_len, hps, has_s0, has_sfin, lbl_ref, tf_ref, tb_ref, q_ref, ff_ref, fb_ref, v_ref, g_ref,
                 nw_ref, *rest):
    rest = list(rest)
    s0_ref = rest.pop(0) if has_s0 else None
    o_ref = rest.pop(0)
    sfin_ref = rest.pop(0) if has_sfin else None
    kv_scr, ss_scr, qd_scr, oi_scr, dec_scr = rest
    n_blk = seq_len // HG_BLOCK
    n_chunk = seq_len // HG_CHUNK
    per_blk = HG_BLOCK // HG_CHUNK
    bpi = HG_UNITS_PER_TRIP // hps
    assert n_blk % bpi == 0

    logits = lbl_ref[...]
    e = jnp.exp(logits - jnp.max(logits, axis=1, keepdims=True))
    lb = e[:, 0, :] / (e[:, 0, :] + e[:, 1, :])

    def chunk_cumsum(t_ref, x):
        hi, lo = _split_bf16(x)
        r = _dot(t_ref[...], jnp.concatenate([hi, lo], axis=1))
        return r[:, 0:HG_DK] + r[:, HG_DK:2 * HG_DK]

    def gate(hf, lbv):
        f = lbv + (1.0 - lbv) * jax.nn.sigmoid(hf)
        return jnp.log(f), 1.0 - f

    def chunk_edge_rows(x, first):
        off = 0 if first else HG_CHUNK - 1
        return [x[c * HG_CHUNK + off:c * HG_CHUNK + off + 1, :] for c in range(per_blk)]

    def spread_rows(rows):
        return jnp.concatenate([jnp.broadcast_to(r, (HG_CHUNK, HG_DK)) for r in rows], axis=0)

    def block_body(i, carry):
        units = [(i * bpi + b, h) for b in range(bpi) for h in range(hps)]
        stage1 = []
        for blk, h in units:
            rows = pl.ds(pl.multiple_of(blk * HG_BLOCK, HG_BLOCK), HG_BLOCK)
            cols = slice(h * HG_DK, (h + 1) * HG_DK)
            logf_f, k_f = gate(ff_ref[rows, cols], lb[0:1, cols])
            logf_b, k_b = gate(fb_ref[rows, cols], lb[1:2, cols])
            a_f = chunk_cumsum(tf_ref, logf_f)
            a_b = chunk_cumsum(tb_ref, logf_b)
            stage1.append((rows, cols, k_f, k_b, a_f, a_b))
        stage2 = []
        for (blk, h), (rows, cols, k_f, k_b, a_f, a_b) in zip(units, stage1):
            q = q_ref[rows, cols] * HG_DK ** -0.5
            vb = v_ref[rows, cols].astype(BF16)
            ea_f = jnp.exp(a_f)
            ea_b = jnp.exp(a_b)
            qd_f = (q * ea_f).astype(BF16)
            qd_b = (q * ea_b).astype(BF16)
            kd_f = k_f * jnp.exp(-a_f)
            kd_b = k_b * jnp.exp(-a_b)
            dec_f = chunk_edge_rows(ea_f, first=False)
            dec_b = chunk_edge_rows(ea_b, first=True)
            ke = jnp.concatenate([kd_f * spread_rows(dec_f), kd_b * spread_rows(dec_b)], axis=1).astype(BF16)
            sc_f = _dot_nt(qd_f, kd_f.astype(BF16))
            sc_b = _dot_nt(qd_b, kd_b.astype(BF16))
            for c in range(per_blk):
                cr = slice(c * HG_CHUNK, (c + 1) * HG_CHUNK)
                kv_scr[h, blk * per_blk + c] = _dot_tn(vb[cr, :], ke[cr, :])
                dec_scr[h, blk * per_blk + c, 0:1, :] = jnp.concatenate([dec_f[c], dec_b[c]], axis=1)
            qd_scr[h, rows, :] = jnp.concatenate([qd_f, qd_b], axis=1)
            stage2.append((rows, cols, vb, sc_f, sc_b))
        for rows, cols, vb, sc_f, sc_b in stage2:
            s = sc_f.astype(BF16) * tf_ref[...] + sc_b.astype(BF16) * tb_ref[...]
            oi_scr[rows, cols] = _dot(s, vb)
        return carry

    lax.fori_loop(0, n_blk // bpi, block_body, 0)

    for h in range(hps):
        if has_s0:
            init = (s0_ref[0, 0, h].T, s0_ref[0, 1, h].T)
        else:
            init = (jnp.zeros((HG_DV, HG_DK), F32), jnp.zeros((HG_DV, HG_DK), F32))

        def scan_body(i, carry, h=h):
            s_f, s_b = carry
            cf = i
            cb = n_chunk - 1 - i
            ss_scr[h, cf, :, 0:HG_DK] = s_f.astype(BF16)
            s_f = dec_scr[h, cf, 0:1, 0:HG_DK] * s_f + kv_scr[h, cf, :, 0:HG_DK]
            ss_scr[h, cb, :, HG_DK:2 * HG_DK] = s_b.astype(BF16)
            s_b = dec_scr[h, cb, 0:1, HG_DK:2 * HG_DK] * s_b + kv_scr[h, cb, :, HG_DK:2 * HG_DK]
            return s_f, s_b

        s_f, s_b = lax.fori_loop(0, n_chunk, scan_body, init, unroll=HG_SCAN_UNROLL)
        if has_sfin:
            sfin_ref[0, 0, h] = s_f.T
            sfin_ref[0, 1, h] = s_b.T

    def inter_body(i, carry):
        for h in range(hps):
            cols = slice(h * HG_DK, (h + 1) * HG_DK)
            for c in range(per_blk):
                rows = pl.ds(pl.multiple_of(i * HG_BLOCK + c * HG_CHUNK, HG_CHUNK), HG_CHUNK)
                oi_scr[rows, cols] = oi_scr[rows, cols] + _dot_nt(qd_scr[h, rows, :], ss_scr[h, i * per_blk + c])
        return carry

    lax.fori_loop(0, n_blk, inter_body, 0)

    for h in range(hps):
        cols = slice(h * HG_DK, (h + 1) * HG_DK)
        o = _rms_rows(oi_scr[:, cols], nw_ref[...])
        g = g_ref[:, cols]
        o_ref[:, cols] = (o * (g * jax.nn.sigmoid(g))).astype(BF16)


def _chunk_masks():
    r = np.arange(HG_BLOCK)
    same = (r[:, None] // HG_CHUNK) == (r[None, :] // HG_CHUNK)
    lower = same & (r[None, :] <= r[:, None])
    upper = same & (r[None, :] >= r[:, None])
    return (jnp.asarray(lower.astype(np.float32), dtype=BF16),
            jnp.asarray(upper.astype(np.float32), dtype=BF16))


def _hgrn(zh, lb_logits, norm_w, n_batch, seq_len, hps, s0, want_final, name):
    t = zh.shape[0]
    n_chunk = seq_len // HG_CHUNK
    width = hps * HG_DK
    n_hb = HG_HEADS // hps
    t_fwd, t_bwd = _chunk_masks()

    def seg_spec(seg):
        return pl.BlockSpec((seq_len, width), lambda b, h: (b, seg * n_hb + h))

    state_spec = pl.BlockSpec((1, 2, hps, HG_DK, HG_DV), lambda b, h: (b, 0, h, 0, 0))
    mask_spec = pl.BlockSpec((HG_BLOCK, HG_BLOCK), lambda b, h: (0, 0))
    in_specs = [pl.BlockSpec((2, lb_logits.shape[1], width), lambda b, h: (0, 0, h)),
                mask_spec, mask_spec,
                seg_spec(0), seg_spec(1), seg_spec(2), seg_spec(3), seg_spec(4),
                pl.BlockSpec((1, HG_DV), lambda b, h: (0, 0))]
    args = [lb_logits, t_fwd, t_bwd, zh, zh, zh, zh, zh, norm_w]
    if s0 is not None:
        in_specs.append(state_spec)
        args.append(s0)
    out_specs = [pl.BlockSpec((seq_len, width), lambda b, h: (b, h))]
    out_shape = [jax.ShapeDtypeStruct((t, HG_WIDTH), BF16)]
    if want_final:
        out_specs.append(state_spec)
        out_shape.append(jax.ShapeDtypeStruct((n_batch, 2, HG_HEADS, HG_DK, HG_DV), F32))
    return pl.pallas_call(
        functools.partial(_hgrn_kernel, seq_len, hps, s0 is not None, want_final),
        grid=(n_batch, n_hb),
        in_specs=in_specs,
        out_specs=out_specs,
        out_shape=out_shape,
        scratch_shapes=[pltpu.VMEM((hps, n_chunk, HG_DV, 2 * HG_DK), F32),
                        pltpu.VMEM((hps, n_chunk, HG_DV, 2 * HG_DK), BF16),
                        pltpu.VMEM((hps, seq_len, 2 * HG_DK), BF16),
                        pltpu.VMEM((seq_len, width), F32),
                        pltpu.VMEM((hps, n_chunk, SUBLANES, 2 * HG_DK), F32)],
        compiler_params=_params(2),
        name=name,
    )(*args)


def _out_kernel(x_ref, mod_ref, oh_ref, oa_ref, g_ref, who_ref, wao_ref, wout_ref, nfw_ref,
                wff1_ref, wff2_ref, fnw_ref, y_ref):
    m = mod_ref[0]
    g1 = m[:, 2 * D_MODEL:3 * D_MODEL]
    sh2 = m[:, 3 * D_MODEL:4 * D_MODEL]
    sc2 = m[:, 4 * D_MODEL:5 * D_MODEL]
    g2 = m[:, 5 * D_MODEL:6 * D_MODEL]
    gates = g_ref[...].astype(F32)
    merged = (gates[:, 0:D_MODEL] * _dot(oh_ref[...], who_ref[...])
              + gates[:, D_MODEL:2 * D_MODEL] * _dot(oa_ref[...], wao_ref[...]))
    x1 = x_ref[...] + g1 * _dot(merged.astype(BF16), wout_ref[...])
    h2 = (_rms_rows(x1, nfw_ref[...]) * (1.0 + sc2) + sh2).astype(BF16)
    acc = jnp.zeros_like(x1)
    for j in range(D_FF // D_MODEL):
        cols = slice(j * D_MODEL, (j + 1) * D_MODEL)
        hj = jnp.maximum(_dot(h2, wff1_ref[:, cols]), 0.0)
        acc = acc + _dot((hj * hj).astype(BF16), wff2_ref[cols, :])
    y_ref[...] = _rms_rows(x1 + g2 * acc, fnw_ref[...])


def _out_proj(x2, mod3, seq_len, oh, oa, gates, w_ho, w_ao, w_out, nfw, w_ff1, w_ff2, fnw, name):
    t = x2.shape[0]
    tm = OUT_TILE
    row = lambda i: (i, 0)
    return pl.pallas_call(
        _out_kernel,
        grid=(t // tm,),
        in_specs=[pl.BlockSpec((tm, D_MODEL), row),
                  _mod_spec(mod3, tm, seq_len),
                  pl.BlockSpec((tm, HG_WIDTH), row),
                  pl.BlockSpec((tm, Q_WIDTH), row),
                  pl.BlockSpec((tm, 2 * D_MODEL), row),
                  _const_spec((HG_WIDTH, D_MODEL)),
                  _const_spec((Q_WIDTH, D_MODEL)),
                  _const_spec((D_MODEL, D_MODEL)),
                  _const_spec((1, D_MODEL)),
                  _const_spec((D_MODEL, D_FF)),
                  _const_spec((D_FF, D_MODEL)),
                  _const_spec((1, D_MODEL))],
        out_specs=pl.BlockSpec((tm, D_MODEL), row),
        out_shape=jax.ShapeDtypeStruct((t, D_MODEL), F32),
        compiler_params=_params(1),
        name=name,
    )(x2, mod3, oh, oa, gates, w_ho, w_ao, w_out, nfw, w_ff1, w_ff2, fnw)


def _rope_tables(n_tokens):
    rows = n_tokens // GRID_W
    row = jnp.repeat(jnp.arange(rows, dtype=F32), GRID_W)
    col = jnp.tile(jnp.arange(GRID_W, dtype=F32), rows)
    axis_dim = HEAD_DIM // 2
    freqs = ROPE_THETA ** (-jnp.arange(0, axis_dim, 2, dtype=F32) / axis_dim)
    ang_r = row[:, None] * freqs
    ang_c = col[:, None] * freqs
    cr, sr, cc, sc = jnp.cos(ang_r), jnp.sin(ang_r), jnp.cos(ang_c), jnp.sin(ang_c)
    zero = jnp.zeros_like(sr)
    cos = jnp.concatenate([cr, cr, cc, cc], axis=-1)
    s_up = jnp.concatenate([-sr, zero, -sc, zero], axis=-1)
    s_dn = jnp.concatenate([zero, sr, zero, sc], axis=-1)
    return tuple(jnp.tile(a, (1, N_HEADS)) for a in (cos, s_up, s_dn))


def _to_slab(kv):
    n, length = kv.shape[:2]
    return jnp.transpose(kv, (0, 2, 3, 1)).reshape(n, KV_WIDTH, length)


def _from_slab(slab):
    n, _, length = slab.shape
    return jnp.transpose(slab.reshape(n, N_KV, HEAD_DIM, length), (0, 3, 1, 2))


def _block_ones(width):
    idx = jnp.arange(width) // HEAD_DIM
    return (idx[:, None] == idx[None, :]).astype(BF16)


def kernel(x_prompt, x_sample, cache_k, cache_v, state_hgrn, c, c_ctx, w_ada, b_ada, norm_mix_w, w_in, q_norm_w, k_norm_w, hgrn_lb_logits, hgrn_norm_w, w_hgrn_out, w_attn_out, w_out, norm_ffn_w, w_ff1, w_ff2, final_norm_w):
    n_p, l_p, _ = x_prompt.shape
    n_s, l_s, _ = x_sample.shape
    past = cache_k.shape[2]
    layer = 0

    cond = jnp.concatenate([c_ctx[None, :], c, jnp.zeros((COND_ROWS - 1 - n_s, D_MODEL), F32)], axis=0)
    mod = _modulation(cond, w_ada[layer], b_ada[layer][None, :])
    mod_p = mod[0:1].reshape(1, 1, N_MOD * D_MODEL)
    mod_s = mod[1:1 + n_s].reshape(n_s, 1, N_MOD * D_MODEL)

    w_in_b = w_in[layer].astype(BF16)
    w_ho_b = w_hgrn_out[layer].astype(BF16)
    w_ao_b = w_attn_out[layer].astype(BF16)
    w_out_b = w_out[layer].astype(BF16)
    w_ff1_b = w_ff1[layer].astype(BF16)
    w_ff2_b = w_ff2[layer].astype(BF16)
    nmw = norm_mix_w[layer][None, :]
    nfw = norm_ffn_w[layer][None, :]
    fnw = final_norm_w[None, :]
    qw = jnp.tile(q_norm_w[layer], N_HEADS)[None, :]
    kw = jnp.tile(k_norm_w[layer], N_KV)[None, :]
    hnw = hgrn_norm_w[layer][None, :]
    ones_q = _block_ones(Q_WIDTH)
    ones_k = _block_ones(KV_WIDTH)

    outs = []
    for latent in (False, True):
        if latent:
            x, n_b, seq, mod3 = x_sample, n_s, l_s, mod_s
        else:
            x, n_b, seq, mod3 = x_prompt, n_p, l_p, mod_p
        x2 = x.reshape(n_b * seq, D_MODEL)
        proj = _in_proj(x2, mod3, seq, nmw, w_in_b, qw, kw, ones_q, ones_k,
                        _rope_tables(seq) if latent else None)
        if latent:
            zh, q, k, v_t, gates = proj
            segs = [(_to_slab(cache_k[:, layer]), _to_slab(cache_v[:, layer])), (k, v_t)]
            oa = _attention(q, segs, n_b, seq, "attn_latent")
            (oh,) = _hgrn(zh, hgrn_lb_logits, hnw, n_b, seq, HG_HEADS_PER_STEP_LATENT, state_hgrn[:, layer], False,
                          "hgrn_latent")
        else:
            zh, q, k, k_t, v_t, gates = proj
            oa = _attention(q, [(k, v_t)], n_b, seq, "attn_ctx")
            oh, s_fin = _hgrn(zh, hgrn_lb_logits, hnw, n_b, seq, HG_HEADS_PER_STEP_CTX, None, True, "hgrn_ctx")
            new_k = _from_slab(k_t)[:, None]
            new_v = _from_slab(v_t)[:, None]
            new_s = s_fin.reshape(n_b, 1, 2, HG_HEADS, HG_DK, HG_DV)
        y = _out_proj(x2, mod3, seq, oh, oa, gates, w_ho_b, w_ao_b, w_out_b, nfw, w_ff1_b, w_ff2_b, fnw,
                      "out_latent" if latent else "out_ctx")
        outs.append(y.reshape(n_b, seq, D_MODEL))
    return (outs[0], outs[1], new_k, new_v, new_s)
```

```python
import functools
from typing import Any, Callable, NamedTuple

import numpy as np
import jax
import jax.numpy as jnp
from jax import lax
from jax.experimental import pallas as pl
from jax.experimental.pallas import tpu as pltpu

F32 = jnp.float32
BF16 = jnp.bfloat16

D_MODEL = 1024
GRID_W = 64
EPS = 1e-6
HG_HEADS = 4
HG_DK = 128
HG_DV = 128
HG_WIDTH = HG_HEADS * HG_DK
HG_CHUNK = 32
N_HEADS = 8
N_KV = 2
HEAD_DIM = 64
Q_WIDTH = N_HEADS * HEAD_DIM
KV_WIDTH = N_KV * HEAD_DIM
ROPE_THETA = 10000.0
D_FF = 4 * D_MODEL
N_MOD = 6
ZH_WIDTH = 5 * HG_WIDTH
D_IN = ZH_WIDTH + Q_WIDTH + 2 * KV_WIDTH + 2 * D_MODEL

V7X_VMEM_LIMIT_BYTES = 56 * 1024 * 1024
SUBLANES = 8
MXU_ROWS = 16
HG_BLOCK = 256
COND_ROWS = 16
ONES_ROWS = 16
LOG2_E = float(np.log2(np.e))
TOKEN_TILE = 256
ATTN_Q_TILE = 256
ATTN_LOOKAHEAD = 2


def _dot(a, b):
    return jnp.dot(a, b, preferred_element_type=F32)


def _dot_nt(a, b):
    return lax.dot_general(a, b, (((1,), (1,)), ((), ())), preferred_element_type=F32)


def _dot_tn(a, b):
    return lax.dot_general(a, b, (((0,), (0,)), ((), ())), preferred_element_type=F32)


def _mm_cost(m, k, n):
    return (m // MXU_ROWS) * -(-k // 256) * -(-n // 256)


def _split_bf16(x):
    hi = x.astype(BF16)
    lo = (x - hi.astype(F32)).astype(BF16)
    return hi, lo


def _const_spec(shape):
    nd = len(shape)
    return pl.BlockSpec(shape, lambda *_: (0,) * nd, pipeline_mode=pl.Buffered(1))


class _Part(NamedTuple):
    stages: Callable[..., Any]
    args: list
    in_specs: list
    out_specs: list
    out_shape: list
    scratch: list
    cost: int


def _interleave(gens, totals):
    done = [0] * len(gens)
    live = list(range(len(gens)))
    while live:
        i = min(live, key=lambda j: done[j] / totals[j])
        try:
            done[i] += next(gens[i])
        except StopIteration:
            live.remove(i)


def _launch(name, n_steps, parts):
    n_in = [len(p.args) for p in parts]
    n_out = [len(p.out_shape) for p in parts]
    n_scr = [len(p.scratch) for p in parts]

    def body(*refs):
        refs = list(refs)
        ins = [[refs.pop(0) for _ in range(n)] for n in n_in]
        outs = [[refs.pop(0) for _ in range(n)] for n in n_out]
        scrs = [[refs.pop(0) for _ in range(n)] for n in n_scr]
        _interleave([p.stages(i, o, s) for p, i, o, s in zip(parts, ins, outs, scrs)], [p.cost for p in parts])

    flat = pl.pallas_call(
        body,
        grid=(n_steps,),
        in_specs=[s for p in parts for s in p.in_specs],
        out_specs=[s for p in parts for s in p.out_specs],
        out_shape=[s for p in parts for s in p.out_shape],
        scratch_shapes=[s for p in parts for s in p.scratch],
        compiler_params=pltpu.CompilerParams(dimension_semantics=("arbitrary",),
                                             vmem_limit_bytes=V7X_VMEM_LIMIT_BYTES),
        name=name,
    )(*[a for p in parts for a in p.args])
    flat = list(flat)
    return [[flat.pop(0) for _ in range(n)] for n in n_out]


def _mod_spec(mod3, tile, seq_len):
    if mod3.shape[0] == 1:
        return pl.BlockSpec((1, 1, N_MOD * D_MODEL), lambda i: (0, 0, 0))
    assert seq_len % tile == 0
    per_seq = seq_len // tile
    return pl.BlockSpec((1, 1, N_MOD * D_MODEL), lambda i: (i // per_seq, 0, 0))


def _mod_kernel(cond_ref, w_ref, b_ref, o_ref):
    c = cond_ref[...]
    x = c * jax.nn.sigmoid(c)
    x_hi, x_lo = _split_bf16(x)
    w_hi, w_lo = _split_bf16(w_ref[...])
    acc = _dot(x_hi, w_hi) + _dot(x_lo, w_hi) + _dot(x_hi, w_lo)
    o_ref[...] = acc + b_ref[...]


def _modulation(cond, w_ada, b_ada):
    n = N_MOD * D_MODEL
    tn = 768
    return pl.pallas_call(
        _mod_kernel,
        grid=(n // tn,),
        in_specs=[pl.BlockSpec((COND_ROWS, D_MODEL), lambda j: (0, 0)),
                  pl.BlockSpec((D_MODEL, tn), lambda j: (0, j)),
                  pl.BlockSpec((1, tn), lambda j: (0, j))],
        out_specs=pl.BlockSpec((COND_ROWS, tn), lambda j: (0, j)),
        out_shape=jax.ShapeDtypeStruct((COND_ROWS, n), F32),
        compiler_params=pltpu.CompilerParams(dimension_semantics=("arbitrary",),
                                             vmem_limit_bytes=V7X_VMEM_LIMIT_BYTES),
        name="adaln_mod",
    )(cond, w_ada, b_ada)


def _rms_rows(x, w):
    return x * lax.rsqrt(jnp.mean(x * x, axis=-1, keepdims=True) + EPS) * w


def _head_sumsq(a, ones_ref):
    return _dot((a * a).astype(BF16), ones_ref[...])


def _head_rms(a, sumsq, w):
    return a * lax.rsqrt(sumsq * (1.0 / HEAD_DIM) + EPS) * w


def _rope(x, cos, s_up, s_dn):
    cols = []
    for j in range(x.shape[1] // 128):
        sl = slice(j * 128, (j + 1) * 128)
        xj = x[:, sl]
        cols.append(xj * cos[:, sl] + pltpu.roll(xj, 112, 1) * s_up[:, sl]
                    + pltpu.roll(xj, 16, 1) * s_dn[:, sl])
    return jnp.concatenate(cols, axis=1) if len(cols) > 1 else cols[0]


def _in_stages(latent, ins, outs, _):
    x_ref, mod_ref, nw_ref, w_ref, qw_ref, kw_ref, oq_ref, ok_ref = ins[:8]
    if latent:
        cos_ref, sup_ref, sdn_ref = ins[8:]
        zh_ref, q_ref, k_ref, vt_ref, g_ref = outs
    else:
        zh_ref, q_ref, k_ref, kt_ref, vt_ref, g_ref = outs
    tm = x_ref.shape[0]
    m = mod_ref[0]
    sh1 = m[:, 0:D_MODEL]
    sc1 = m[:, D_MODEL:2 * D_MODEL]
    h = _rms_rows(x_ref[...], nw_ref[...]) * (1.0 + sc1) + sh1
    hb = h.astype(BF16)
    c0 = ZH_WIDTH
    c1 = c0 + Q_WIDTH
    c2 = c1 + KV_WIDTH
    c3 = c2 + KV_WIDTH
    aq = _dot(hb, w_ref[:, c0:c1])
    ak = _dot(hb, w_ref[:, c1:c2])
    av = _dot(hb, w_ref[:, c2:c3])
    yield _mm_cost(tm, D_MODEL, c3 - c0)
    gl = _dot(hb, w_ref[:, c3:D_IN])
    yield _mm_cost(tm, D_MODEL, D_IN - c3)
    ssq = _head_sumsq(aq, oq_ref)
    ssk = _head_sumsq(ak, ok_ref)
    yield _mm_cost(tm, Q_WIDTH, Q_WIDTH) + _mm_cost(tm, KV_WIDTH, KV_WIDTH)
    half = c0 // 2
    zh_ref[:, 0:half] = _dot(hb, w_ref[:, 0:half])
    g_ref[...] = jax.nn.sigmoid(gl).astype(BF16)
    qn = _head_rms(aq, ssq, qw_ref[...])
    kn = _head_rms(ak, ssk, kw_ref[...])
    seq = vt_ref.shape[2]
    avt = av.T
    for s in range(vt_ref.shape[0]):
        vt_ref[s] = avt[:, s * seq:(s + 1) * seq].astype(vt_ref.dtype)
    yield _mm_cost(tm, D_MODEL, half)
    zh_ref[:, half:c0] = _dot(hb, w_ref[:, half:c0])
    if latent:
        cos, sup, sdn = cos_ref[...], sup_ref[...], sdn_ref[...]
        qn = _rope(qn, cos, sup, sdn)
        kn = _rope(kn, cos[:, 0:KV_WIDTH], sup[:, 0:KV_WIDTH], sdn[:, 0:KV_WIDTH])
    else:
        knt = kn.T
        for s in range(kt_ref.shape[0]):
            kt_ref[s] = knt[:, s * seq:(s + 1) * seq]
    q_ref[...] = (qn * (HEAD_DIM ** -0.5 * LOG2_E)).astype(BF16)
    k_ref[...] = kn.astype(BF16)
    yield _mm_cost(tm, D_MODEL, c0 - half)


def _in_part(x2, mod3, seq_len, norm_w, w_in_b, qw, kw, ones_q, ones_k, rope):
    t = x2.shape[0]
    tm = TOKEN_TILE
    latent = rope is not None
    per_seq = max(seq_len // tm, 1)
    row = lambda i: (i, 0)
    in_specs = [pl.BlockSpec((tm, D_MODEL), row),
                _mod_spec(mod3, tm, seq_len),
                _const_spec((1, D_MODEL)),
                _const_spec((D_MODEL, D_IN)),
                _const_spec((1, Q_WIDTH)),
                _const_spec((1, KV_WIDTH)),
                _const_spec((Q_WIDTH, Q_WIDTH)),
                _const_spec((KV_WIDTH, KV_WIDTH))]
    args = [x2, mod3, norm_w, w_in_b, qw, kw, ones_q, ones_k]
    out_specs = [pl.BlockSpec((tm, ZH_WIDTH), row), pl.BlockSpec((tm, Q_WIDTH), row),
                 pl.BlockSpec((tm, KV_WIDTH), row)]
    out_shape = [jax.ShapeDtypeStruct((t, ZH_WIDTH), F32), jax.ShapeDtypeStruct((t, Q_WIDTH), BF16),
                 jax.ShapeDtypeStruct((t, KV_WIDTH), BF16)]
    if seq_len <= tm:
        assert tm % seq_len == 0
        t_spec = pl.BlockSpec((tm // seq_len, KV_WIDTH, seq_len), lambda i: (i, 0, 0))
    else:
        t_spec = pl.BlockSpec((1, KV_WIDTH, tm), lambda i: (i // per_seq, 0, i % per_seq))
    if latent:
        in_specs += [pl.BlockSpec((tm, Q_WIDTH), lambda i: (i % per_seq, 0))] * 3
        args += list(rope)
        out_specs.append(t_spec)
        out_shape.append(jax.ShapeDtypeStruct((t // seq_len, KV_WIDTH, seq_len), BF16))
    else:
        out_specs += [t_spec, t_spec]
        out_shape += [jax.ShapeDtypeStruct((t // seq_len, KV_WIDTH, seq_len), F32)] * 2
    out_specs.append(pl.BlockSpec((tm, 2 * D_MODEL), row))
    out_shape.append(jax.ShapeDtypeStruct((t, 2 * D_MODEL), BF16))
    cost = (_mm_cost(tm, D_MODEL, D_IN) + _mm_cost(tm, Q_WIDTH, Q_WIDTH) + _mm_cost(tm, KV_WIDTH, KV_WIDTH))
    return t // tm, _Part(functools.partial(_in_stages, latent), args, in_specs, out_specs, out_shape, [], cost)


def _attn_cost(lq, lks):
    per_head = sum(_mm_cost(lk, HEAD_DIM, lq) + _mm_cost(HEAD_DIM + ONES_ROWS, lk, lq) for lk in lks)
    return N_HEADS * per_head


def _attn_stages(k_transposed, ins, outs, _):
    n_seg = len(k_transposed)
    q_ref, kv_refs = ins[0], ins[1:]
    (o_ref,) = outs
    q = q_ref[...]
    lq = q.shape[0]
    ks = [(kv_refs[2 * s][0].T if k_transposed[s] else kv_refs[2 * s][...]).astype(BF16) for s in range(n_seg)]
    vts = [kv_refs[2 * s + 1][0].astype(BF16) for s in range(n_seg)]
    vts = [[jnp.concatenate([vt[g * HEAD_DIM:(g + 1) * HEAD_DIM, :],
                             jnp.ones((ONES_ROWS, vt.shape[1]), BF16)], axis=0) for g in range(N_KV)]
           for vt in vts]
    score_cost = sum(_mm_cost(k.shape[0], HEAD_DIM, lq) for k in ks)
    value_cost = sum(_mm_cost(HEAD_DIM + ONES_ROWS, k.shape[0], lq) for k in ks)

    def scores(h):
        g = h // (N_HEADS // N_KV)
        qh = q[:, h * HEAD_DIM:(h + 1) * HEAD_DIM]
        return [_dot_nt(k[:, g * HEAD_DIM:(g + 1) * HEAD_DIM], qh) for k in ks]

    pending = []
    for h in range(ATTN_LOOKAHEAD):
        pending.append(scores(h))
        yield score_cost
    outs_t = []
    for h in range(N_HEADS):
        if h + ATTN_LOOKAHEAD < N_HEADS:
            pending.append(scores(h + ATTN_LOOKAHEAD))
            yield score_cost
        st = pending.pop(0)
        g = h // (N_HEADS // N_KV)
        m = functools.reduce(jnp.maximum, [jnp.max(s, axis=0, keepdims=True) for s in st])
        ot = functools.reduce(jnp.add, [_dot(vt[g], jnp.exp2(s - m).astype(BF16)) for vt, s in zip(vts, st)])
        outs_t.append(ot[0:HEAD_DIM, :] / ot[HEAD_DIM:HEAD_DIM + 1, :])
        if h == N_HEADS - 1:
            o_ref[...] = jnp.concatenate(outs_t, axis=0).T.astype(BF16)
        yield value_cost


def _attn_part(q, kv_segments, n_batch, seq_len):
    t = q.shape[0]
    tq = ATTN_Q_TILE
    per_seq = seq_len // tq
    in_specs = [pl.BlockSpec((tq, Q_WIDTH), lambda i: (i, 0))]
    args = [q]
    for k, v_t in kv_segments:
        slab = pl.BlockSpec((1,) + v_t.shape[1:], lambda i: (i // per_seq, 0, 0))
        k_spec = slab if k.ndim == 3 else pl.BlockSpec((v_t.shape[2], KV_WIDTH), lambda i: (i // per_seq, 0))
        in_specs += [k_spec, slab]
        args += [k, v_t]
    stages = functools.partial(_attn_stages, tuple(k.ndim == 3 for k, _ in kv_segments))
    return n_batch * per_seq, _Part(stages, args, in_specs, [pl.BlockSpec((tq, Q_WIDTH), lambda i: (i, 0))],
                                    [jax.ShapeDtypeStruct((t, Q_WIDTH), BF16)], [],
                                    _attn_cost(tq, [v_t.shape[2] for _, v_t in kv_segments]))


def _hgrn_unit_cost():
    blk = HG_BLOCK
    return (2 * _mm_cost(blk, blk, 2 * HG_DK)
            + 2 * _mm_cost(blk, HG_DK, blk) + _mm_cost(blk, blk, HG_DV)
            + (blk // HG_CHUNK) * (_mm_cost(HG_DV, HG_CHUNK, 2 * HG_DK) + _mm_cost(HG_CHUNK, 2 * HG_DK, HG_DV)))


def _hgrn_stages(seq_len, hps, has_s0, has_sfin, ins, outs, scrs):
    lbl_ref, tf_ref, tb_ref, q_ref, ff_ref, fb_ref, v_ref, g_ref, nw_ref = ins[:9]
    s0_ref = ins[9] if has_s0 else None
    o_ref = outs[0]
    sfin_ref = outs[1] if has_sfin else None
    kv_scr, ss_scr, qd_scr, oi_scr, dec_scr = scrs
    n_blk = seq_len // HG_BLOCK
    n_chunk = seq_len // HG_CHUNK
    per_blk = HG_BLOCK // HG_CHUNK

    logits = lbl_ref[...]
    e = jnp.exp(logits - jnp.max(logits, axis=1, keepdims=True))
    lb = e[:, 0, :] / (e[:, 0, :] + e[:, 1, :])

    def chunk_cumsum(t_ref, x):
        hi, lo = _split_bf16(x)
        r = _dot(t_ref[...], jnp.concatenate([hi, lo], axis=1))
        return r[:, 0:HG_DK] + r[:, HG_DK:2 * HG_DK]

    def gate(hf, lbv):
        f = lbv + (1.0 - lbv) * jax.nn.sigmoid(hf)
        return jnp.log(f), 1.0 - f

    def chunk_edge_rows(x, first):
        off = 0 if first else HG_CHUNK - 1
        return [x[c * HG_CHUNK + off:c * HG_CHUNK + off + 1, :] for c in range(per_blk)]

    def spread_rows(rows):
        return jnp.concatenate([jnp.broadcast_to(r, (HG_CHUNK, HG_DK)) for r in rows], axis=0)

    units = [(blk, h) for blk in range(n_blk) for h in range(hps)]
    stage1 = []
    for blk, h in units:
        rows = slice(blk * HG_BLOCK, (blk + 1) * HG_BLOCK)
        cols = slice(h * HG_DK, (h + 1) * HG_DK)
        logf_f, k_f = gate(ff_ref[rows, cols], lb[0:1, cols])
        logf_b, k_b = gate(fb_ref[rows, cols], lb[1:2, cols])
        a_f = chunk_cumsum(tf_ref, logf_f)
        a_b = chunk_cumsum(tb_ref, logf_b)
        stage1.append((rows, cols, k_f, k_b, a_f, a_b))
        yield 2 * _mm_cost(HG_BLOCK, HG_BLOCK, 2 * HG_DK)
    stage2 = []
    for (blk, h), (rows, cols, k_f, k_b, a_f, a_b) in zip(units, stage1):
        q = q_ref[rows, cols] * HG_DK ** -0.5
        vb = v_ref[rows, cols].astype(BF16)
        ea_f = jnp.exp(a_f)
        ea_b = jnp.exp(a_b)
        qd_f = (q * ea_f).astype(BF16)
        qd_b = (q * ea_b).astype(BF16)
        kd_f = k_f * jnp.exp(-a_f)
        kd_b = k_b * jnp.exp(-a_b)
        dec_f = chunk_edge_rows(ea_f, first=False)
        dec_b = chunk_edge_rows(ea_b, first=True)
        ke = jnp.concatenate([kd_f * spread_rows(dec_f), kd_b * spread_rows(dec_b)], axis=1).astype(BF16)
        sc_f = _dot_nt(qd_f, kd_f.astype(BF16))
        sc_b = _dot_nt(qd_b, kd_b.astype(BF16))
        for c in range(per_blk):
            cr = slice(c * HG_CHUNK, (c + 1) * HG_CHUNK)
            kv_scr[h, blk * per_blk + c] = _dot_tn(vb[cr, :], ke[cr, :])
            dec_scr[h, blk * per_blk + c, 0:1, :] = jnp.concatenate([dec_f[c], dec_b[c]], axis=1)
        qd_scr[h, rows, :] = jnp.concatenate([qd_f, qd_b], axis=1)
        stage2.append((rows, cols, vb, sc_f, sc_b))
        yield 2 * _mm_cost(HG_BLOCK, HG_DK, HG_BLOCK) + per_blk * _mm_cost(HG_DV, HG_CHUNK, 2 * HG_DK)
    for rows, cols, vb, sc_f, sc_b in stage2:
        s = sc_f.astype(BF16) * tf_ref[...] + sc_b.astype(BF16) * tb_ref[...]
        oi_scr[rows, cols] = _dot(s, vb)
        yield _mm_cost(HG_BLOCK, HG_BLOCK, HG_DV)

    for h in range(hps):
        if has_s0:
            s_f, s_b = s0_ref[0, 0, h].T, s0_ref[0, 1, h].T
        else:
            s_f = s_b = jnp.zeros((HG_DV, HG_DK), F32)
        for cf in range(n_chunk):
            cb = n_chunk - 1 - cf
            ss_scr[h, cf, :, 0:HG_DK] = s_f.astype(BF16)
            s_f = dec_scr[h, cf, 0:1, 0:HG_DK] * s_f + kv_scr[h, cf, :, 0:HG_DK]
            ss_scr[h, cb, :, HG_DK:2 * HG_DK] = s_b.astype(BF16)
            s_b = dec_scr[h, cb, 0:1, HG_DK:2 * HG_DK] * s_b + kv_scr[h, cb, :, HG_DK:2 * HG_DK]
        if has_sfin:
            sfin_ref[0, 0, h] = s_f.T
            sfin_ref[0, 1, h] = s_b.T

    for blk, h in units:
        cols = slice(h * HG_DK, (h + 1) * HG_DK)
        for c in range(per_blk):
            ci = blk * per_blk + c
            rows = slice(ci * HG_CHUNK, (ci + 1) * HG_CHUNK)
            oi_scr[rows, cols] = oi_scr[rows, cols] + _dot_nt(qd_scr[h, rows, :], ss_scr[h, ci])
        yield per_blk * _mm_cost(HG_CHUNK, 2 * HG_DK, HG_DV)

    for h in range(hps):
        cols = slice(h * HG_DK, (h + 1) * HG_DK)
        o = _rms_rows(oi_scr[:, cols], nw_ref[...])
        g = g_ref[:, cols]
        o_ref[:, cols] = (o * (g * jax.nn.sigmoid(g))).astype(BF16)


def _chunk_masks():
    r = np.arange(HG_BLOCK)
    same = (r[:, None] // HG_CHUNK) == (r[None, :] // HG_CHUNK)
    lower = same & (r[None, :] <= r[:, None])
    upper = same & (r[None, :] >= r[:, None])
    return (jnp.asarray(lower.astype(np.float32), dtype=BF16),
            jnp.asarray(upper.astype(np.float32), dtype=BF16))


def _hgrn_part(zh, lb_logits, norm_w, n_batch, seq_len, hps, s0, want_final):
    t = zh.shape[0]
    n_chunk = seq_len // HG_CHUNK
    width = hps * HG_DK
    n_hb = HG_HEADS // hps
    t_fwd, t_bwd = _chunk_masks()

    def seg_spec(seg):
        return pl.BlockSpec((seq_len, width), lambda i: (i // n_hb, seg * n_hb + i % n_hb))

    state_spec = pl.BlockSpec((1, 2, hps, HG_DK, HG_DV), lambda i: (i // n_hb, 0, i % n_hb, 0, 0))
    mask_spec = pl.BlockSpec((HG_BLOCK, HG_BLOCK), lambda i: (0, 0))
    in_specs = [pl.BlockSpec((2, lb_logits.shape[1], width), lambda i: (0, 0, i % n_hb)),
                mask_spec, mask_spec,
                seg_spec(0), seg_spec(1), seg_spec(2), seg_spec(3), seg_spec(4),
                pl.BlockSpec((1, HG_DV), lambda i: (0, 0))]
    args = [lb_logits, t_fwd, t_bwd, zh, zh, zh, zh, zh, norm_w]
    if s0 is not None:
        in_specs.append(state_spec)
        args.append(s0)
    out_specs = [pl.BlockSpec((seq_len, width), lambda i: (i // n_hb, i % n_hb))]
    out_shape = [jax.ShapeDtypeStruct((t, HG_WIDTH), BF16)]
    if want_final:
        out_specs.append(state_spec)
        out_shape.append(jax.ShapeDtypeStruct((n_batch, 2, HG_HEADS, HG_DK, HG_DV), F32))
    scratch = [pltpu.VMEM((hps, n_chunk, HG_DV, 2 * HG_DK), F32),
               pltpu.VMEM((hps, n_chunk, HG_DV, 2 * HG_DK), BF16),
               pltpu.VMEM((hps, seq_len, 2 * HG_DK), BF16),
               pltpu.VMEM((seq_len, width), F32),
               pltpu.VMEM((hps, n_chunk, SUBLANES, 2 * HG_DK), F32)]
    stages = functools.partial(_hgrn_stages, seq_len, hps, s0 is not None, want_final)
    cost = hps * (seq_len // HG_BLOCK) * _hgrn_unit_cost()
    return n_batch * n_hb, _Part(stages, args, in_specs, out_specs, out_shape, scratch, cost)


def _out_stages(ins, outs, _):
    x_ref, mod_ref, oh_ref, oa_ref, g_ref, who_ref, wao_ref, wout_ref, nfw_ref, wff1_ref, wff2_ref, fnw_ref = ins
    (y_ref,) = outs
    tm = x_ref.shape[0]
    m = mod_ref[0]
    g1 = m[:, 2 * D_MODEL:3 * D_MODEL]
    sh2 = m[:, 3 * D_MODEL:4 * D_MODEL]
    sc2 = m[:, 4 * D_MODEL:5 * D_MODEL]
    g2 = m[:, 5 * D_MODEL:6 * D_MODEL]
    gates = g_ref[...].astype(F32)
    merged = (gates[:, 0:D_MODEL] * _dot(oh_ref[...], who_ref[...])
              + gates[:, D_MODEL:2 * D_MODEL] * _dot(oa_ref[...], wao_ref[...]))
    yield _mm_cost(tm, HG_WIDTH + Q_WIDTH, D_MODEL)
    x1 = x_ref[...] + g1 * _dot(merged.astype(BF16), wout_ref[...])
    yield _mm_cost(tm, D_MODEL, D_MODEL)
    h2 = (_rms_rows(x1, nfw_ref[...]) * (1.0 + sc2) + sh2).astype(BF16)
    acc = jnp.zeros_like(x1)
    for j in range(D_FF // D_MODEL):
        cols = slice(j * D_MODEL, (j + 1) * D_MODEL)
        hj = jnp.maximum(_dot(h2, wff1_ref[:, cols]), 0.0)
        yield _mm_cost(tm, D_MODEL, D_MODEL)
        acc = acc + _dot((hj * hj).astype(BF16), wff2_ref[cols, :])
        if j == D_FF // D_MODEL - 1:
            y_ref[...] = _rms_rows(x1 + g2 * acc, fnw_ref[...])
        yield _mm_cost(tm, D_MODEL, D_MODEL)


def _out_part(x2, mod3, seq_len, oh, oa, gates, w_ho, w_ao, w_out, nfw, w_ff1, w_ff2, fnw):
    t = x2.shape[0]
    tm = TOKEN_TILE
    row = lambda i: (i, 0)
    in_specs = [pl.BlockSpec((tm, D_MODEL), row),
                _mod_spec(mod3, tm, seq_len),
                pl.BlockSpec((tm, HG_WIDTH), row),
                pl.BlockSpec((tm, Q_WIDTH), row),
                pl.BlockSpec((tm, 2 * D_MODEL), row),
                _const_spec((HG_WIDTH, D_MODEL)),
                _const_spec((Q_WIDTH, D_MODEL)),
                _const_spec((D_MODEL, D_MODEL)),
                _const_spec((1, D_MODEL)),
                _const_spec((D_MODEL, D_FF)),
                _const_spec((D_FF, D_MODEL)),
                _const_spec((1, D_MODEL))]
    args = [x2, mod3, oh, oa, gates, w_ho, w_ao, w_out, nfw, w_ff1, w_ff2, fnw]
    cost = _mm_cost(tm, HG_WIDTH + Q_WIDTH, D_MODEL) + _mm_cost(tm, D_MODEL, D_MODEL) + 2 * _mm_cost(tm, D_MODEL, D_FF)
    return t // tm, _Part(_out_stages, args, in_specs, [pl.BlockSpec((tm, D_MODEL), row)],
                          [jax.ShapeDtypeStruct((t, D_MODEL), F32)], [], cost)


def _rope_tables(n_tokens):
    rows = n_tokens // GRID_W
    row = jnp.repeat(jnp.arange(rows, dtype=F32), GRID_W)
    col = jnp.tile(jnp.arange(GRID_W, dtype=F32), rows)
    axis_dim = HEAD_DIM // 2
    freqs = ROPE_THETA ** (-jnp.arange(0, axis_dim, 2, dtype=F32) / axis_dim)
    ang_r = row[:, None] * freqs
    ang_c = col[:, None] * freqs
    cr, sr, cc, sc = jnp.cos(ang_r), jnp.sin(ang_r), jnp.cos(ang_c), jnp.sin(ang_c)
    zero = jnp.zeros_like(sr)
    cos = jnp.concatenate([cr, cr, cc, cc], axis=-1)
    s_up = jnp.concatenate([-sr, zero, -sc, zero], axis=-1)
    s_dn = jnp.concatenate([zero, sr, zero, sc], axis=-1)
    return tuple(jnp.tile(a, (1, N_HEADS)) for a in (cos, s_up, s_dn))


def _to_slab(kv):
    n, length = kv.shape[:2]
    return jnp.transpose(kv, (0, 2, 3, 1)).reshape(n, KV_WIDTH, length)


def _from_slab(slab):
    n, _, length = slab.shape
    return jnp.transpose(slab.reshape(n, N_KV, HEAD_DIM, length), (0, 3, 1, 2))


def _block_ones(width):
    idx = jnp.arange(width) // HEAD_DIM
    return (idx[:, None] == idx[None, :]).astype(BF16)


def _same_steps(*counted_parts):
    steps = {n for n, _ in counted_parts}
    assert len(steps) == 1, steps
    return steps.pop(), [p for _, p in counted_parts]


def kernel(x_prompt, x_sample, cache_k, cache_v, state_hgrn, c, c_ctx, w_ada, b_ada, norm_mix_w, w_in, q_norm_w, k_norm_w, hgrn_lb_logits, hgrn_norm_w, w_hgrn_out, w_attn_out, w_out, norm_ffn_w, w_ff1, w_ff2, final_norm_w):
    n_p, l_p, _ = x_prompt.shape
    n_s, l_s, _ = x_sample.shape
    layer = 0

    cond = jnp.concatenate([c_ctx[None, :], c, jnp.zeros((COND_ROWS - 1 - n_s, D_MODEL), F32)], axis=0)
    mod = _modulation(cond, w_ada[layer], b_ada[layer][None, :])
    mod_p = mod[0:1].reshape(1, 1, N_MOD * D_MODEL)
    mod_s = mod[1:1 + n_s].reshape(n_s, 1, N_MOD * D_MODEL)

    w_in_b = w_in[layer].astype(BF16)
    out_w = (w_hgrn_out[layer].astype(BF16), w_attn_out[layer].astype(BF16), w_out[layer].astype(BF16),
             norm_ffn_w[layer][None, :], w_ff1[layer].astype(BF16), w_ff2[layer].astype(BF16), final_norm_w[None, :])
    in_w = (norm_mix_w[layer][None, :], w_in_b, jnp.tile(q_norm_w[layer], N_HEADS)[None, :],
            jnp.tile(k_norm_w[layer], N_KV)[None, :], _block_ones(Q_WIDTH), _block_ones(KV_WIDTH))
    hnw = hgrn_norm_w[layer][None, :]
    xp2 = x_prompt.reshape(n_p * l_p, D_MODEL)
    xs2 = x_sample.reshape(n_s * l_s, D_MODEL)

    steps, parts = _same_steps(_in_part(xp2, mod_p, l_p, *in_w, None))
    ((zh_p, q_p, k_p, kt_p, vt_p, gates_p),) = _launch("in_ctx", steps, parts)

    steps, parts = _same_steps(
        _in_part(xs2, mod_s, l_s, *in_w, _rope_tables(l_s)),
        _attn_part(q_p, [(k_p, vt_p)], n_p, l_p),
        _hgrn_part(zh_p, hgrn_lb_logits, hnw, n_p, l_p, HG_HEADS, None, True))
    (zh_s, q_s, k_s, vt_s, gates_s), (oa_p,), (oh_p, s_fin) = _launch("in_latent_mix_ctx", steps, parts)

    segs = [(_to_slab(cache_k[:, layer]), _to_slab(cache_v[:, layer])), (k_s, vt_s)]
    steps, parts = _same_steps(
        _out_part(xp2, mod_p, l_p, oh_p, oa_p, gates_p, *out_w),
        _attn_part(q_s, segs, n_s, l_s),
        _hgrn_part(zh_s, hgrn_lb_logits, hnw, n_s, l_s, 1, state_hgrn[:, layer], False))
    (y_p,), (oa_s,), (oh_s,) = _launch("out_ctx_mix_latent", steps, parts)

    steps, parts = _same_steps(_out_part(xs2, mod_s, l_s, oh_s, oa_s, gates_s, *out_w))
    ((y_s,),) = _launch("out_latent", steps, parts)

    new_k = _from_slab(kt_p)[:, None]
    new_v = _from_slab(vt_p)[:, None]
    new_s = s_fin.reshape(n_p, 1, 2, HG_HEADS, HG_DK, HG_DV)
    return (y_p.reshape(n_p, l_p, D_MODEL), y_s.reshape(n_s, l_s, D_MODEL), new_k, new_v, new_s)
```

```python
import functools
from typing import Any, Callable, NamedTuple

import numpy as np
import jax
import jax.numpy as jnp
from jax import lax
from jax.experimental import pallas as pl
from jax.experimental.pallas import tpu as pltpu

F32 = jnp.float32
BF16 = jnp.bfloat16

D_MODEL = 1024
GRID_W = 64
EPS = 1e-6
HG_HEADS = 4
HG_DK = 128
HG_DV = 128
HG_WIDTH = HG_HEADS * HG_DK
HG_CHUNK = 32
N_HEADS = 8
N_KV = 2
HEAD_DIM = 64
Q_WIDTH = N_HEADS * HEAD_DIM
KV_WIDTH = N_KV * HEAD_DIM
ROPE_THETA = 10000.0
D_FF = 4 * D_MODEL
N_MOD = 6
ZH_WIDTH = 5 * HG_WIDTH
D_IN = ZH_WIDTH + Q_WIDTH + 2 * KV_WIDTH + 2 * D_MODEL

V7X_VMEM_LIMIT_BYTES = 56 * 1024 * 1024
SUBLANES = 8
MXU_ROWS = 16
HG_BLOCK = 256
COND_ROWS = 16
ONES_ROWS = 16
LOG2_E = float(np.log2(np.e))
TOKEN_TILE = 256
DOT_COLS = 512
ATTN_Q_TILE = 256
ATTN_LOOKAHEAD = 2
MIX_PACES_IN = (1.0, 1.0, 0.8)
MIX_PACES_OUT = (1.0, 1.0, 1.0)


def _dot(a, b):
    return jnp.dot(a, b, preferred_element_type=F32)


def _dot_nt(a, b):
    return lax.dot_general(a, b, (((1,), (1,)), ((), ())), preferred_element_type=F32)


def _dot_tn(a, b):
    return lax.dot_general(a, b, (((0,), (0,)), ((), ())), preferred_element_type=F32)


def _mm_cost(m, k, n):
    return (m // MXU_ROWS) * -(-k // 256) * -(-n // 256)


def _dot_by_cols(a, w_ref, rows, cols):
    pieces = []
    for c in range(cols[0], cols[1], DOT_COLS):
        end = min(c + DOT_COLS, cols[1])
        pieces.append(_dot(a, w_ref[rows[0]:rows[1], c:end].astype(BF16)))
        yield _mm_cost(a.shape[0], rows[1] - rows[0], end - c)
    return pieces[0] if len(pieces) == 1 else jnp.concatenate(pieces, axis=1)


def _split_bf16(x):
    hi = x.astype(BF16)
    lo = (x - hi.astype(F32)).astype(BF16)
    return hi, lo


def _const_spec(shape):
    nd = len(shape)
    return pl.BlockSpec(shape, lambda *_: (0,) * nd, pipeline_mode=pl.Buffered(1))


class _Part(NamedTuple):
    stages: Callable[..., Any]
    args: list
    in_specs: list
    out_specs: list
    out_shape: list
    scratch: list
    cost: int


def _interleave(gens, totals):
    done = [0] * len(gens)
    live = list(range(len(gens)))
    while live:
        i = min(live, key=lambda j: done[j] / totals[j])
        try:
            done[i] += next(gens[i])
        except StopIteration:
            live.remove(i)


def _launch(name, n_steps, parts, paces=None):
    paces = paces or [1.0] * len(parts)
    n_in = [len(p.args) for p in parts]
    n_out = [len(p.out_shape) for p in parts]
    n_scr = [len(p.scratch) for p in parts]

    def body(*refs):
        refs = list(refs)
        ins = [[refs.pop(0) for _ in range(n)] for n in n_in]
        outs = [[refs.pop(0) for _ in range(n)] for n in n_out]
        scrs = [[refs.pop(0) for _ in range(n)] for n in n_scr]
        _interleave([p.stages(i, o, s) for p, i, o, s in zip(parts, ins, outs, scrs)],
                    [p.cost * pace for p, pace in zip(parts, paces)])

    flat = pl.pallas_call(
        body,
        grid=(n_steps,),
        in_specs=[s for p in parts for s in p.in_specs],
        out_specs=[s for p in parts for s in p.out_specs],
        out_shape=[s for p in parts for s in p.out_shape],
        scratch_shapes=[s for p in parts for s in p.scratch],
        compiler_params=pltpu.CompilerParams(dimension_semantics=("arbitrary",),
                                             vmem_limit_bytes=V7X_VMEM_LIMIT_BYTES),
        name=name,
    )(*[a for p in parts for a in p.args])
    flat = list(flat)
    return [[flat.pop(0) for _ in range(n)] for n in n_out]


class _Mod(NamedTuple):
    rows: Any
    first: int
    shared: bool


def _mod_spec(mod, tile, seq_len):
    if mod.shared:
        return pl.BlockSpec((1, 1, N_MOD * D_MODEL), lambda i: (mod.first, 0, 0))
    assert seq_len % tile == 0
    per_seq = seq_len // tile
    return pl.BlockSpec((1, 1, N_MOD * D_MODEL), lambda i: (mod.first + i // per_seq, 0, 0))


def _mod_kernel(cond_ref, w_ref, b_ref, o_ref):
    c = cond_ref[...]
    x = c * jax.nn.sigmoid(c)
    x_hi, x_lo = _split_bf16(x)
    w = w_ref[...].astype(BF16)
    o_ref[...] = _dot(x_hi, w) + _dot(x_lo, w) + b_ref[...]


def _modulation(cond, w_ada, b_ada):
    n = N_MOD * D_MODEL
    tn = 768
    return pl.pallas_call(
        _mod_kernel,
        grid=(n // tn,),
        in_specs=[pl.BlockSpec((COND_ROWS, D_MODEL), lambda j: (0, 0)),
                  pl.BlockSpec((D_MODEL, tn), lambda j: (0, j)),
                  pl.BlockSpec((1, tn), lambda j: (0, j))],
        out_specs=pl.BlockSpec((COND_ROWS, tn), lambda j: (0, j)),
        out_shape=jax.ShapeDtypeStruct((COND_ROWS, n), F32),
        compiler_params=pltpu.CompilerParams(dimension_semantics=("arbitrary",),
                                             vmem_limit_bytes=V7X_VMEM_LIMIT_BYTES),
        name="adaln_mod",
    )(cond, w_ada, b_ada)


def _rms_rows(x, w):
    return x * lax.rsqrt(jnp.mean(x * x, axis=-1, keepdims=True) + EPS) * w


def _head_sumsq(a, ones_ref):
    return _dot((a * a).astype(BF16), ones_ref[...])


def _head_rms(a, sumsq, w):
    return a * lax.rsqrt(sumsq * (1.0 / HEAD_DIM) + EPS) * w


def _rope(x, cos, s_up, s_dn):
    cols = []
    for j in range(x.shape[1] // 128):
        sl = slice(j * 128, (j + 1) * 128)
        xj = x[:, sl]
        cols.append(xj * cos[:, sl] + pltpu.roll(xj, 112, 1) * s_up[:, sl]
                    + pltpu.roll(xj, 16, 1) * s_dn[:, sl])
    return jnp.concatenate(cols, axis=1) if len(cols) > 1 else cols[0]


def _in_stages(latent, ins, outs, _):
    x_ref, mod_ref, nw_ref, w_ref, qw_ref, kw_ref, oq_ref, ok_ref = ins[:8]
    if latent:
        cos_ref, sup_ref, sdn_ref = ins[8:]
        zh_ref, q_ref, k_ref, vt_ref, g_ref = outs
    else:
        zh_ref, q_ref, k_ref, kt_ref, vt_ref, g_ref = outs
    tm = x_ref.shape[0]
    m = mod_ref[0]
    sh1 = m[:, 0:D_MODEL]
    sc1 = m[:, D_MODEL:2 * D_MODEL]
    h = _rms_rows(x_ref[...], nw_ref[...]) * (1.0 + sc1) + sh1
    hb = h.astype(BF16)
    c0 = ZH_WIDTH
    c1 = c0 + Q_WIDTH
    c2 = c1 + KV_WIDTH
    c3 = c2 + KV_WIDTH
    aq = _dot(hb, w_ref[:, c0:c1].astype(BF16))
    ak = _dot(hb, w_ref[:, c1:c2].astype(BF16))
    av = _dot(hb, w_ref[:, c2:c3].astype(BF16))
    yield _mm_cost(tm, D_MODEL, c3 - c0)
    gl = yield from _dot_by_cols(hb, w_ref, (0, D_MODEL), (c3, D_IN))
    ssq = _head_sumsq(aq, oq_ref)
    ssk = _head_sumsq(ak, ok_ref)
    yield _mm_cost(tm, Q_WIDTH, Q_WIDTH) + _mm_cost(tm, KV_WIDTH, KV_WIDTH)
    half = c0 // 2
    zh_ref[:, 0:half] = yield from _dot_by_cols(hb, w_ref, (0, D_MODEL), (0, half))
    g_ref[...] = jax.nn.sigmoid(gl).astype(BF16)
    qn = _head_rms(aq, ssq, qw_ref[...])
    kn = _head_rms(ak, ssk, kw_ref[...])
    seq = vt_ref.shape[2]
    avt = av.T
    for s in range(vt_ref.shape[0]):
        vt_ref[s] = avt[:, s * seq:(s + 1) * seq].astype(vt_ref.dtype)
    zh_ref[:, half:c0] = yield from _dot_by_cols(hb, w_ref, (0, D_MODEL), (half, c0))
    if latent:
        cos, sup, sdn = cos_ref[...], sup_ref[...], sdn_ref[...]
        qn = _rope(qn, cos, sup, sdn)
        kn = _rope(kn, cos[:, 0:KV_WIDTH], sup[:, 0:KV_WIDTH], sdn[:, 0:KV_WIDTH])
    else:
        knt = kn.T
        for s in range(kt_ref.shape[0]):
            kt_ref[s] = knt[:, s * seq:(s + 1) * seq]
    q_ref[...] = (qn * (HEAD_DIM ** -0.5 * LOG2_E)).astype(BF16)
    k_ref[...] = kn.astype(BF16)


def _in_part(x2, mod, seq_len, norm_w, w_in, qw, kw, ones_q, ones_k, rope):
    t = x2.shape[0]
    tm = TOKEN_TILE
    latent = rope is not None
    per_seq = max(seq_len // tm, 1)
    row = lambda i: (i, 0)
    in_specs = [pl.BlockSpec((tm, D_MODEL), row),
                _mod_spec(mod, tm, seq_len),
                _const_spec((1, D_MODEL)),
                _const_spec((D_MODEL, D_IN)),
                _const_spec((1, Q_WIDTH)),
                _const_spec((1, KV_WIDTH)),
                _const_spec((Q_WIDTH, Q_WIDTH)),
                _const_spec((KV_WIDTH, KV_WIDTH))]
    args = [x2, mod.rows, norm_w, w_in, qw, kw, ones_q, ones_k]
    out_specs = [pl.BlockSpec((tm, ZH_WIDTH), row), pl.BlockSpec((tm, Q_WIDTH), row),
                 pl.BlockSpec((tm, KV_WIDTH), row)]
    out_shape = [jax.ShapeDtypeStruct((t, ZH_WIDTH), F32), jax.ShapeDtypeStruct((t, Q_WIDTH), BF16),
                 jax.ShapeDtypeStruct((t, KV_WIDTH), BF16)]
    if seq_len <= tm:
        assert tm % seq_len == 0
        t_spec = pl.BlockSpec((tm // seq_len, KV_WIDTH, seq_len), lambda i: (i, 0, 0))
    else:
        t_spec = pl.BlockSpec((1, KV_WIDTH, tm), lambda i: (i // per_seq, 0, i % per_seq))
    if latent:
        in_specs += [pl.BlockSpec((tm, Q_WIDTH), lambda i: (i % per_seq, 0))] * 3
        args += list(rope)
        out_specs.append(t_spec)
        out_shape.append(jax.ShapeDtypeStruct((t // seq_len, KV_WIDTH, seq_len), BF16))
    else:
        out_specs += [t_spec, t_spec]
        out_shape += [jax.ShapeDtypeStruct((t // seq_len, KV_WIDTH, seq_len), F32)] * 2
    out_specs.append(pl.BlockSpec((tm, 2 * D_MODEL), row))
    out_shape.append(jax.ShapeDtypeStruct((t, 2 * D_MODEL), BF16))
    cost = (_mm_cost(tm, D_MODEL, D_IN) + _mm_cost(tm, Q_WIDTH, Q_WIDTH) + _mm_cost(tm, KV_WIDTH, KV_WIDTH))
    return t // tm, _Part(functools.partial(_in_stages, latent), args, in_specs, out_specs, out_shape, [], cost)


def _attn_cost(lq, lks):
    per_head = sum(_mm_cost(lk, HEAD_DIM, lq) + _mm_cost(HEAD_DIM + ONES_ROWS, lk, lq) for lk in lks)
    return N_HEADS * per_head


def _attn_stages(k_transposed, lookahead, ins, outs, _):
    n_seg = len(k_transposed)
    q_ref, kv_refs = ins[0], ins[1:]
    (o_ref,) = outs
    q = q_ref[...]
    lq = q.shape[0]
    ks = [(kv_refs[2 * s][0].T if k_transposed[s] else kv_refs[2 * s][...]).astype(BF16) for s in range(n_seg)]
    vts = [kv_refs[2 * s + 1][0].astype(BF16) for s in range(n_seg)]
    vts = [[jnp.concatenate([vt[g * HEAD_DIM:(g + 1) * HEAD_DIM, :],
                             jnp.ones((ONES_ROWS, vt.shape[1]), BF16)], axis=0) for g in range(N_KV)]
           for vt in vts]
    score_cost = sum(_mm_cost(k.shape[0], HEAD_DIM, lq) for k in ks)
    value_cost = sum(_mm_cost(HEAD_DIM + ONES_ROWS, k.shape[0], lq) for k in ks)

    def scores(h):
        g = h // (N_HEADS // N_KV)
        qh = q[:, h * HEAD_DIM:(h + 1) * HEAD_DIM]
        return [_dot_nt(k[:, g * HEAD_DIM:(g + 1) * HEAD_DIM], qh) for k in ks]

    pending = []
    for h in range(lookahead):
        pending.append(scores(h))
        yield score_cost
    outs_t = []
    for h in range(N_HEADS):
        if h + lookahead < N_HEADS:
            pending.append(scores(h + lookahead))
            yield score_cost
        st = pending.pop(0)
        g = h // (N_HEADS // N_KV)
        m = functools.reduce(jnp.maximum, [jnp.max(s, axis=0, keepdims=True) for s in st])
        ot = functools.reduce(jnp.add, [_dot(vt[g], jnp.exp2(s - m).astype(BF16)) for vt, s in zip(vts, st)])
        outs_t.append(ot[0:HEAD_DIM, :] / ot[HEAD_DIM:HEAD_DIM + 1, :])
        if h == N_HEADS - 1:
            o_ref[...] = jnp.concatenate(outs_t, axis=0).T.astype(BF16)
        yield value_cost


def _attn_part(q, kv_segments, n_batch, seq_len, lookahead):
    t = q.shape[0]
    tq = ATTN_Q_TILE
    per_seq = seq_len // tq
    in_specs = [pl.BlockSpec((tq, Q_WIDTH), lambda i: (i, 0))]
    args = [q]
    for k, v_t, layer in kv_segments:
        lk = v_t.shape[-1]
        if v_t.ndim == 4:
            slab = pl.BlockSpec((1, None, KV_WIDTH, lk), lambda i, layer=layer: (i // per_seq, layer, 0, 0))
        else:
            slab = pl.BlockSpec((1, KV_WIDTH, lk), lambda i: (i // per_seq, 0, 0))
        k_spec = slab if k.ndim == v_t.ndim else pl.BlockSpec((lk, KV_WIDTH), lambda i: (i // per_seq, 0))
        in_specs += [k_spec, slab]
        args += [k, v_t]
    stages = functools.partial(_attn_stages, tuple(k.ndim == v_t.ndim for k, v_t, _ in kv_segments), lookahead)
    return n_batch * per_seq, _Part(stages, args, in_specs, [pl.BlockSpec((tq, Q_WIDTH), lambda i: (i, 0))],
                                    [jax.ShapeDtypeStruct((t, Q_WIDTH), BF16)], [],
                                    _attn_cost(tq, [v_t.shape[-1] for _, v_t, _ in kv_segments]))


def _hgrn_unit_cost():
    blk = HG_BLOCK
    return (2 * _mm_cost(blk, blk, 2 * HG_DK)
            + 2 * _mm_cost(blk, HG_DK, blk) + _mm_cost(blk, blk, HG_DV)
            + (blk // HG_CHUNK) * (_mm_cost(HG_DV, HG_CHUNK, 2 * HG_DK) + _mm_cost(HG_CHUNK, 2 * HG_DK, HG_DV)))


def _hgrn_stages(seq_len, hps, has_s0, has_sfin, ins, outs, scrs):
    lbl_ref, tf_ref, tb_ref, q_ref, ff_ref, fb_ref, v_ref, g_ref, nw_ref = ins[:9]
    s0_ref = ins[9] if has_s0 else None
    o_ref = outs[0]
    sfin_ref = outs[1] if has_sfin else None
    kv_scr, ss_scr, qd_scr, oi_scr, dec_scr = scrs
    n_blk = seq_len // HG_BLOCK
    n_chunk = seq_len // HG_CHUNK
    per_blk = HG_BLOCK // HG_CHUNK

    logits = lbl_ref[...]
    e = jnp.exp(logits - jnp.max(logits, axis=1, keepdims=True))
    lb = e[:, 0, :] / (e[:, 0, :] + e[:, 1, :])

    def chunk_cumsum(t_ref, x):
        hi, lo = _split_bf16(x)
        r = _dot(t_ref[...], jnp.concatenate([hi, lo], axis=1))
        return r[:, 0:HG_DK] + r[:, HG_DK:2 * HG_DK]

    def gate(hf, lbv):
        f = lbv + (1.0 - lbv) * jax.nn.sigmoid(hf)
        return jnp.log(f), 1.0 - f

    def chunk_edge_rows(x, first):
        off = 0 if first else HG_CHUNK - 1
        return [x[c * HG_CHUNK + off:c * HG_CHUNK + off + 1, :] for c in range(per_blk)]

    def spread_rows(rows):
        return jnp.concatenate([jnp.broadcast_to(r, (HG_CHUNK, HG_DK)) for r in rows], axis=0)

    units = [(blk, h) for blk in range(n_blk) for h in range(hps)]
    stage1 = []
    for blk, h in units:
        rows = slice(blk * HG_BLOCK, (blk + 1) * HG_BLOCK)
        cols = slice(h * HG_DK, (h + 1) * HG_DK)
        logf_f, k_f = gate(ff_ref[rows, cols], lb[0:1, cols])
        logf_b, k_b = gate(fb_ref[rows, cols], lb[1:2, cols])
        a_f = chunk_cumsum(tf_ref, logf_f)
        a_b = chunk_cumsum(tb_ref, logf_b)
        stage1.append((rows, cols, k_f, k_b, a_f, a_b))
        yield 2 * _mm_cost(HG_BLOCK, HG_BLOCK, 2 * HG_DK)
    stage2 = []
    for (blk, h), (rows, cols, k_f, k_b, a_f, a_b) in zip(units, stage1):
        q = q_ref[rows, cols] * HG_DK ** -0.5
        vb = v_ref[rows, cols].astype(BF16)
        ea_f = jnp.exp(a_f)
        ea_b = jnp.exp(a_b)
        qd_f = (q * ea_f).astype(BF16)
        qd_b = (q * ea_b).astype(BF16)
        kd_f = k_f * jnp.exp(-a_f)
        kd_b = k_b * jnp.exp(-a_b)
        dec_f = chunk_edge_rows(ea_f, first=False)
        dec_b = chunk_edge_rows(ea_b, first=True)
        ke = jnp.concatenate([kd_f * spread_rows(dec_f), kd_b * spread_rows(dec_b)], axis=1).astype(BF16)
        sc_f = _dot_nt(qd_f, kd_f.astype(BF16))
        sc_b = _dot_nt(qd_b, kd_b.astype(BF16))
        for c in range(per_blk):
            cr = slice(c * HG_CHUNK, (c + 1) * HG_CHUNK)
            kv_scr[h, blk * per_blk + c] = _dot_tn(vb[cr, :], ke[cr, :])
            dec_scr[h, blk * per_blk + c, 0:1, :] = jnp.concatenate([dec_f[c], dec_b[c]], axis=1)
        qd_scr[h, rows, :] = jnp.concatenate([qd_f, qd_b], axis=1)
        stage2.append((rows, cols, vb, sc_f, sc_b))
        yield 2 * _mm_cost(HG_BLOCK, HG_DK, HG_BLOCK) + per_blk * _mm_cost(HG_DV, HG_CHUNK, 2 * HG_DK)
    for rows, cols, vb, sc_f, sc_b in stage2:
        s = sc_f.astype(BF16) * tf_ref[...] + sc_b.astype(BF16) * tb_ref[...]
        oi_scr[rows, cols] = _dot(s, vb)
        yield _mm_cost(HG_BLOCK, HG_BLOCK, HG_DV)

    for h in range(hps):
        if has_s0:
            s_f, s_b = s0_ref[0, 0, h].T, s0_ref[0, 1, h].T
        else:
            s_f = s_b = jnp.zeros((HG_DV, HG_DK), F32)
        for cf in range(n_chunk):
            cb = n_chunk - 1 - cf
            ss_scr[h, cf, :, 0:HG_DK] = s_f.astype(BF16)
            s_f = dec_scr[h, cf, 0:1, 0:HG_DK] * s_f + kv_scr[h, cf, :, 0:HG_DK]
            ss_scr[h, cb, :, HG_DK:2 * HG_DK] = s_b.astype(BF16)
            s_b = dec_scr[h, cb, 0:1, HG_DK:2 * HG_DK] * s_b + kv_scr[h, cb, :, HG_DK:2 * HG_DK]
        if has_sfin:
            sfin_ref[0, 0, h] = s_f.T
            sfin_ref[0, 1, h] = s_b.T

    for blk, h in units:
        cols = slice(h * HG_DK, (h + 1) * HG_DK)
        for c in range(per_blk):
            ci = blk * per_blk + c
            rows = slice(ci * HG_CHUNK, (ci + 1) * HG_CHUNK)
            oi_scr[rows, cols] = oi_scr[rows, cols] + _dot_nt(qd_scr[h, rows, :], ss_scr[h, ci])
        yield per_blk * _mm_cost(HG_CHUNK, 2 * HG_DK, HG_DV)

    for h in range(hps):
        cols = slice(h * HG_DK, (h + 1) * HG_DK)
        o = _rms_rows(oi_scr[:, cols], nw_ref[...])
        g = g_ref[:, cols]
        o_ref[:, cols] = (o * (g * jax.nn.sigmoid(g))).astype(BF16)


def _chunk_masks():
    r = np.arange(HG_BLOCK)
    same = (r[:, None] // HG_CHUNK) == (r[None, :] // HG_CHUNK)
    lower = same & (r[None, :] <= r[:, None])
    upper = same & (r[None, :] >= r[:, None])
    return (jnp.asarray(lower.astype(np.float32), dtype=BF16),
            jnp.asarray(upper.astype(np.float32), dtype=BF16))


def _hgrn_part(zh, lb_logits, norm_w, n_batch, seq_len, hps, s0, want_final):
    t = zh.shape[0]
    n_chunk = seq_len // HG_CHUNK
    width = hps * HG_DK
    n_hb = HG_HEADS // hps
    t_fwd, t_bwd = _chunk_masks()

    def seg_spec(seg):
        return pl.BlockSpec((seq_len, width), lambda i: (i // n_hb, seg * n_hb + i % n_hb))

    state_spec = pl.BlockSpec((1, 2, hps, HG_DK, HG_DV), lambda i: (i // n_hb, 0, i % n_hb, 0, 0))
    mask_spec = pl.BlockSpec((HG_BLOCK, HG_BLOCK), lambda i: (0, 0))
    in_specs = [pl.BlockSpec((2, lb_logits.shape[1], width), lambda i: (0, 0, i % n_hb)),
                mask_spec, mask_spec,
                seg_spec(0), seg_spec(1), seg_spec(2), seg_spec(3), seg_spec(4),
                pl.BlockSpec((1, HG_DV), lambda i: (0, 0))]
    args = [lb_logits, t_fwd, t_bwd, zh, zh, zh, zh, zh, norm_w]
    if s0 is not None:
        states, layer = s0
        in_specs.append(pl.BlockSpec((1, None, 2, hps, HG_DK, HG_DV),
                                     lambda i: (i // n_hb, layer, 0, i % n_hb, 0, 0)))
        args.append(states)
    out_specs = [pl.BlockSpec((seq_len, width), lambda i: (i // n_hb, i % n_hb))]
    out_shape = [jax.ShapeDtypeStruct((t, HG_WIDTH), BF16)]
    if want_final:
        out_specs.append(state_spec)
        out_shape.append(jax.ShapeDtypeStruct((n_batch, 2, HG_HEADS, HG_DK, HG_DV), F32))
    scratch = [pltpu.VMEM((hps, n_chunk, HG_DV, 2 * HG_DK), F32),
               pltpu.VMEM((hps, n_chunk, HG_DV, 2 * HG_DK), BF16),
               pltpu.VMEM((hps, seq_len, 2 * HG_DK), BF16),
               pltpu.VMEM((seq_len, width), F32),
               pltpu.VMEM((hps, n_chunk, SUBLANES, 2 * HG_DK), F32)]
    stages = functools.partial(_hgrn_stages, seq_len, hps, s0 is not None, want_final)
    cost = hps * (seq_len // HG_BLOCK) * _hgrn_unit_cost()
    return n_batch * n_hb, _Part(stages, args, in_specs, out_specs, out_shape, scratch, cost)


def _out_stages(ins, outs, _):
    x_ref, mod_ref, oh_ref, oa_ref, g_ref, who_ref, wao_ref, wout_ref, nfw_ref, wff1_ref, wff2_ref, fnw_ref = ins
    (y_ref,) = outs
    tm = x_ref.shape[0]
    m = mod_ref[0]
    g1 = m[:, 2 * D_MODEL:3 * D_MODEL]
    sh2 = m[:, 3 * D_MODEL:4 * D_MODEL]
    sc2 = m[:, 4 * D_MODEL:5 * D_MODEL]
    g2 = m[:, 5 * D_MODEL:6 * D_MODEL]
    gates = g_ref[...].astype(F32)
    from_h = yield from _dot_by_cols(oh_ref[...], who_ref, (0, HG_WIDTH), (0, D_MODEL))
    from_a = yield from _dot_by_cols(oa_ref[...], wao_ref, (0, Q_WIDTH), (0, D_MODEL))
    merged = gates[:, 0:D_MODEL] * from_h + gates[:, D_MODEL:2 * D_MODEL] * from_a
    mixed = yield from _dot_by_cols(merged.astype(BF16), wout_ref, (0, D_MODEL), (0, D_MODEL))
    x1 = x_ref[...] + g1 * mixed
    h2 = (_rms_rows(x1, nfw_ref[...]) * (1.0 + sc2) + sh2).astype(BF16)
    acc = jnp.zeros_like(x1)
    for j in range(D_FF // D_MODEL):
        cols = slice(j * D_MODEL, (j + 1) * D_MODEL)
        hj = yield from _dot_by_cols(h2, wff1_ref, (0, D_MODEL), (cols.start, cols.stop))
        hj = jnp.maximum(hj, 0.0)
        acc = acc + (yield from _dot_by_cols((hj * hj).astype(BF16), wff2_ref, (cols.start, cols.stop), (0, D_MODEL)))
    y_ref[...] = _rms_rows(x1 + g2 * acc, fnw_ref[...])


def _out_part(x2, mod, seq_len, oh, oa, gates, w_ho, w_ao, w_out, nfw, w_ff1, w_ff2, fnw):
    t = x2.shape[0]
    tm = TOKEN_TILE
    row = lambda i: (i, 0)
    in_specs = [pl.BlockSpec((tm, D_MODEL), row),
                _mod_spec(mod, tm, seq_len),
                pl.BlockSpec((tm, HG_WIDTH), row),
                pl.BlockSpec((tm, Q_WIDTH), row),
                pl.BlockSpec((tm, 2 * D_MODEL), row),
                _const_spec((HG_WIDTH, D_MODEL)),
                _const_spec((Q_WIDTH, D_MODEL)),
                _const_spec((D_MODEL, D_MODEL)),
                _const_spec((1, D_MODEL)),
                _const_spec((D_MODEL, D_FF)),
                _const_spec((D_FF, D_MODEL)),
                _const_spec((1, D_MODEL))]
    args = [x2, mod.rows, oh, oa, gates, w_ho, w_ao, w_out, nfw, w_ff1, w_ff2, fnw]
    cost = _mm_cost(tm, HG_WIDTH + Q_WIDTH, D_MODEL) + _mm_cost(tm, D_MODEL, D_MODEL) + 2 * _mm_cost(tm, D_MODEL, D_FF)
    return t // tm, _Part(_out_stages, args, in_specs, [pl.BlockSpec((tm, D_MODEL), row)],
                          [jax.ShapeDtypeStruct((t, D_MODEL), F32)], [], cost)


def _cast_stages(ins, outs, _):
    for src, dst in zip(ins, outs):
        dst[...] = src[...].astype(dst.dtype)
    yield 1


def _cast_part(arrays, n_steps):
    in_specs, out_specs, out_shape = [], [], []
    for a in arrays:
        rows, cols = a.shape
        assert rows % (n_steps * MXU_ROWS) == 0
        spec = pl.BlockSpec((rows // n_steps, cols), lambda i: (i, 0))
        in_specs.append(spec)
        out_specs.append(spec)
        out_shape.append(jax.ShapeDtypeStruct(a.shape, BF16))
    return n_steps, _Part(_cast_stages, list(arrays), in_specs, out_specs, out_shape, [], 1)


def _rope_tables(n_tokens):
    rows = n_tokens // GRID_W
    row = np.repeat(np.arange(rows, dtype=np.float32), GRID_W)
    col = np.tile(np.arange(GRID_W, dtype=np.float32), rows)
    axis_dim = HEAD_DIM // 2
    freqs = (ROPE_THETA ** (-np.arange(0, axis_dim, 2, dtype=np.float32) / axis_dim)).astype(np.float32)
    ang_r = row[:, None] * freqs
    ang_c = col[:, None] * freqs
    cr, sr, cc, sc = np.cos(ang_r), np.sin(ang_r), np.cos(ang_c), np.sin(ang_c)
    zero = np.zeros_like(sr)
    cos = np.concatenate([cr, cr, cc, cc], axis=-1)
    s_up = np.concatenate([-sr, zero, -sc, zero], axis=-1)
    s_dn = np.concatenate([zero, sr, zero, sc], axis=-1)
    return tuple(jnp.asarray(np.tile(a, (1, N_HEADS)), dtype=F32) for a in (cos, s_up, s_dn))


def _to_slab(kv):
    n, depth, length = kv.shape[:3]
    return jnp.transpose(kv, (0, 1, 3, 4, 2)).reshape(n, depth, KV_WIDTH, length)


def _from_slab(slab):
    n, _, length = slab.shape
    return jnp.transpose(slab.reshape(n, N_KV, HEAD_DIM, length), (0, 3, 1, 2))


def _block_ones(width):
    idx = np.arange(width) // HEAD_DIM
    return jnp.asarray((idx[:, None] == idx[None, :]).astype(np.float32), dtype=BF16)


def _same_steps(*counted_parts):
    steps = {n for n, _ in counted_parts}
    assert len(steps) == 1, steps
    return steps.pop(), [p for _, p in counted_parts]


def kernel(x_prompt, x_sample, cache_k, cache_v, state_hgrn, c, c_ctx, w_ada, b_ada, norm_mix_w, w_in, q_norm_w, k_norm_w, hgrn_lb_logits, hgrn_norm_w, w_hgrn_out, w_attn_out, w_out, norm_ffn_w, w_ff1, w_ff2, final_norm_w):
    n_p, l_p, _ = x_prompt.shape
    n_s, l_s, _ = x_sample.shape
    layer = 0

    cond = jnp.concatenate([c_ctx[None, :], c, jnp.zeros((COND_ROWS - 1 - n_s, D_MODEL), F32)], axis=0)
    mod = _modulation(cond, w_ada[layer], b_ada[layer][None, :]).reshape(COND_ROWS, 1, N_MOD * D_MODEL)
    mod_p = _Mod(mod, 0, True)
    mod_s = _Mod(mod, 1, False)

    in_small = (jnp.tile(q_norm_w[layer], N_HEADS)[None, :], jnp.tile(k_norm_w[layer], N_KV)[None, :],
                _block_ones(Q_WIDTH), _block_ones(KV_WIDTH))
    nmw = norm_mix_w[layer][None, :]
    hnw = hgrn_norm_w[layer][None, :]
    xp2 = x_prompt.reshape(n_p * l_p, D_MODEL)
    xs2 = x_sample.reshape(n_s * l_s, D_MODEL)

    n_in, in_p = _in_part(xp2, mod_p, l_p, nmw, w_in[layer], *in_small, None)
    steps, parts = _same_steps((n_in, in_p), _cast_part(
        [w_in[layer], w_hgrn_out[layer], w_attn_out[layer], w_out[layer], w_ff1[layer], w_ff2[layer]], n_in))
    (zh_p, q_p, k_p, kt_p, vt_p, gates_p), (w_in_b, w_ho_b, w_ao_b, w_out_b, w_ff1_b, w_ff2_b) = _launch(
        "in_ctx", steps, parts)
    out_w = (w_ho_b, w_ao_b, w_out_b, norm_ffn_w[layer][None, :], w_ff1_b, w_ff2_b, final_norm_w[None, :])

    steps, parts = _same_steps(
        _attn_part(q_p, [(k_p, vt_p, None)], n_p, l_p, N_HEADS),
        _in_part(xs2, mod_s, l_s, nmw, w_in_b, *in_small, _rope_tables(l_s)),
        _hgrn_part(zh_p, hgrn_lb_logits, hnw, n_p, l_p, HG_HEADS, None, True))
    (oa_p,), (zh_s, q_s, k_s, vt_s, gates_s), (oh_p, s_fin) = _launch("in_latent_mix_ctx", steps, parts,
                                                                      MIX_PACES_IN)

    segs = [(_to_slab(cache_k), _to_slab(cache_v), layer), (k_s, vt_s, None)]
    steps, parts = _same_steps(
        _out_part(xp2, mod_p, l_p, oh_p, oa_p, gates_p, *out_w),
        _attn_part(q_s, segs, n_s, l_s, ATTN_LOOKAHEAD),
        _hgrn_part(zh_s, hgrn_lb_logits, hnw, n_s, l_s, 1, (state_hgrn, layer), False))
    (y_p,), (oa_s,), (oh_s,) = _launch("out_ctx_mix_latent", steps, parts, MIX_PACES_OUT)

    steps, parts = _same_steps(_out_part(xs2, mod_s, l_s, oh_s, oa_s, gates_s, *out_w))
    ((y_s,),) = _launch("out_latent", steps, parts)

    new_k = _from_slab(kt_p)[:, None]
    new_v = _from_slab(vt_p)[:, None]
    new_s = s_fin.reshape(n_p, 1, 2, HG_HEADS, HG_DK, HG_DV)
    return (y_p.reshape(n_p, l_p, D_MODEL), y_s.reshape(n_s, l_s, D_MODEL), new_k, new_v, new_s)
```

```python
import functools
from typing import Any, Callable, NamedTuple

import numpy as np
import jax
import jax.numpy as jnp
from jax import lax
from jax.experimental import pallas as pl
from jax.experimental.pallas import tpu as pltpu

F32 = jnp.float32
BF16 = jnp.bfloat16

D_MODEL = 1024
GRID_W = 64
EPS = 1e-6
HG_HEADS = 4
HG_DK = 128
HG_DV = 128
HG_WIDTH = HG_HEADS * HG_DK
HG_CHUNK = 32
N_HEADS = 8
N_KV = 2
HEAD_DIM = 64
Q_WIDTH = N_HEADS * HEAD_DIM
KV_WIDTH = N_KV * HEAD_DIM
ROPE_THETA = 10000.0
D_FF = 4 * D_MODEL
N_MOD = 6
ZH_WIDTH = 5 * HG_WIDTH
D_IN = ZH_WIDTH + Q_WIDTH + 2 * KV_WIDTH + 2 * D_MODEL

V7X_VMEM_LIMIT_BYTES = 56 * 1024 * 1024
SUBLANES = 8
MXU_ROWS = 16
HG_BLOCK = 256
COND_ROWS = 16
ONES_ROWS = 16
LOG2_E = float(np.log2(np.e))
MIX_TOKEN_TILE = 256
SOLO_TOKEN_TILE = 512
ADALN_COL_TILE = 1536
DOT_COLS = 512
ATTN_Q_TILE = 256
ATTN_LOOKAHEAD = 2
MIX_PACES_IN = (1.0, 1.0, 0.8)
MIX_PACES_OUT = (0.85, 1.0, 1.0)


def _dot(a, b):
    return jnp.dot(a, b, preferred_element_type=F32)


def _dot_nt(a, b):
    return lax.dot_general(a, b, (((1,), (1,)), ((), ())), preferred_element_type=F32)


def _dot_tn(a, b):
    return lax.dot_general(a, b, (((0,), (0,)), ((), ())), preferred_element_type=F32)


def _mm_cost(m, k, n):
    return (m // MXU_ROWS) * -(-k // 256) * -(-n // 256)


def _dot_by_cols(a, w_ref, rows, cols):
    pieces = []
    for c in range(cols[0], cols[1], DOT_COLS):
        end = min(c + DOT_COLS, cols[1])
        pieces.append(_dot(a, w_ref[rows[0]:rows[1], c:end].astype(BF16)))
        yield _mm_cost(a.shape[0], rows[1] - rows[0], end - c)
    return pieces[0] if len(pieces) == 1 else jnp.concatenate(pieces, axis=1)


def _split_bf16(x):
    hi = x.astype(BF16)
    lo = (x - hi.astype(F32)).astype(BF16)
    return hi, lo


def _const_spec(shape):
    nd = len(shape)
    return pl.BlockSpec(shape, lambda *_: (0,) * nd, pipeline_mode=pl.Buffered(1))


class _Part(NamedTuple):
    stages: Callable[..., Any]
    args: list
    in_specs: list
    out_specs: list
    out_shape: list
    scratch: list
    cost: int


def _interleave(gens, totals):
    done = [0] * len(gens)
    live = list(range(len(gens)))
    while live:
        i = min(live, key=lambda j: done[j] / totals[j])
        try:
            done[i] += next(gens[i])
        except StopIteration:
            live.remove(i)


def _launch(name, n_steps, parts, paces=None):
    paces = paces or [1.0] * len(parts)
    n_in = [len(p.args) for p in parts]
    n_out = [len(p.out_shape) for p in parts]
    n_scr = [len(p.scratch) for p in parts]

    def body(*refs):
        refs = list(refs)
        ins = [[refs.pop(0) for _ in range(n)] for n in n_in]
        outs = [[refs.pop(0) for _ in range(n)] for n in n_out]
        scrs = [[refs.pop(0) for _ in range(n)] for n in n_scr]
        _interleave([p.stages(i, o, s) for p, i, o, s in zip(parts, ins, outs, scrs)],
                    [p.cost * pace for p, pace in zip(parts, paces)])

    flat = pl.pallas_call(
        body,
        grid=(n_steps,),
        in_specs=[s for p in parts for s in p.in_specs],
        out_specs=[s for p in parts for s in p.out_specs],
        out_shape=[s for p in parts for s in p.out_shape],
        scratch_shapes=[s for p in parts for s in p.scratch],
        compiler_params=pltpu.CompilerParams(dimension_semantics=("arbitrary",),
                                             vmem_limit_bytes=V7X_VMEM_LIMIT_BYTES),
        name=name,
    )(*[a for p in parts for a in p.args])
    flat = list(flat)
    return [[flat.pop(0) for _ in range(n)] for n in n_out]


class _Mod(NamedTuple):
    rows: Any
    first: int
    shared: bool


def _mod_spec(mod, tile, seq_len):
    if mod.shared:
        return pl.BlockSpec((1, 1, N_MOD * D_MODEL), lambda i: (mod.first, 0, 0))
    assert seq_len % tile == 0
    per_seq = seq_len // tile
    return pl.BlockSpec((1, 1, N_MOD * D_MODEL), lambda i: (mod.first + i // per_seq, 0, 0))


def _mod_kernel(cond_ref, w_ref, b_ref, o_ref):
    c = cond_ref[...]
    x = c * jax.nn.sigmoid(c)
    x_hi, x_lo = _split_bf16(x)
    w = w_ref[...].astype(BF16)
    o_ref[...] = _dot(x_hi, w) + _dot(x_lo, w) + b_ref[...]


def _modulation(cond, w_ada, b_ada):
    n = N_MOD * D_MODEL
    tn = ADALN_COL_TILE
    return pl.pallas_call(
        _mod_kernel,
        grid=(n // tn,),
        in_specs=[pl.BlockSpec((COND_ROWS, D_MODEL), lambda j: (0, 0)),
                  pl.BlockSpec((D_MODEL, tn), lambda j: (0, j)),
                  pl.BlockSpec((1, tn), lambda j: (0, j))],
        out_specs=pl.BlockSpec((COND_ROWS, tn), lambda j: (0, j)),
        out_shape=jax.ShapeDtypeStruct((COND_ROWS, n), F32),
        compiler_params=pltpu.CompilerParams(dimension_semantics=("arbitrary",),
                                             vmem_limit_bytes=V7X_VMEM_LIMIT_BYTES),
        name="adaln_mod",
    )(cond, w_ada, b_ada)


def _rms_rows(x, w):
    return x * lax.rsqrt(jnp.mean(x * x, axis=-1, keepdims=True) + EPS) * w


def _head_sumsq(a, ones_ref):
    return _dot((a * a).astype(BF16), ones_ref[...])


def _head_rms(a, sumsq, w):
    return a * lax.rsqrt(sumsq * (1.0 / HEAD_DIM) + EPS) * w


def _rope(x, cos, s_up, s_dn):
    cols = []
    for j in range(x.shape[1] // 128):
        sl = slice(j * 128, (j + 1) * 128)
        xj = x[:, sl]
        cols.append(xj * cos[:, sl] + pltpu.roll(xj, 112, 1) * s_up[:, sl]
                    + pltpu.roll(xj, 16, 1) * s_dn[:, sl])
    return jnp.concatenate(cols, axis=1) if len(cols) > 1 else cols[0]


def _in_stages(latent, ins, outs, _):
    x_ref, mod_ref, nw_ref, w_ref, qw_ref, kw_ref, oq_ref, ok_ref = ins[:8]
    if latent:
        cos_ref, sup_ref, sdn_ref = ins[8:]
        zh_ref, q_ref, k_ref, vt_ref, g_ref = outs
    else:
        zh_ref, q_ref, k_ref, kt_ref, vt_ref, g_ref = outs
    tm = x_ref.shape[0]
    m = mod_ref[0]
    sh1 = m[:, 0:D_MODEL]
    sc1 = m[:, D_MODEL:2 * D_MODEL]
    h = _rms_rows(x_ref[...], nw_ref[...]) * (1.0 + sc1) + sh1
    hb = h.astype(BF16)
    c0 = ZH_WIDTH
    c1 = c0 + Q_WIDTH
    c2 = c1 + KV_WIDTH
    c3 = c2 + KV_WIDTH
    aq = _dot(hb, w_ref[:, c0:c1].astype(BF16))
    ak = _dot(hb, w_ref[:, c1:c2].astype(BF16))
    av = _dot(hb, w_ref[:, c2:c3].astype(BF16))
    yield _mm_cost(tm, D_MODEL, c3 - c0)
    gl = yield from _dot_by_cols(hb, w_ref, (0, D_MODEL), (c3, D_IN))
    ssq = _head_sumsq(aq, oq_ref)
    ssk = _head_sumsq(ak, ok_ref)
    yield _mm_cost(tm, Q_WIDTH, Q_WIDTH) + _mm_cost(tm, KV_WIDTH, KV_WIDTH)
    half = c0 // 2
    zh_ref[:, 0:half] = yield from _dot_by_cols(hb, w_ref, (0, D_MODEL), (0, half))
    g_ref[...] = jax.nn.sigmoid(gl).astype(BF16)
    qn = _head_rms(aq, ssq, qw_ref[...])
    kn = _head_rms(ak, ssk, kw_ref[...])
    seq = vt_ref.shape[2]
    avt = av.T
    for s in range(vt_ref.shape[0]):
        vt_ref[s] = avt[:, s * seq:(s + 1) * seq].astype(vt_ref.dtype)
    zh_ref[:, half:c0] = yield from _dot_by_cols(hb, w_ref, (0, D_MODEL), (half, c0))
    if latent:
        cos, sup, sdn = cos_ref[...], sup_ref[...], sdn_ref[...]
        qn = _rope(qn, cos, sup, sdn)
        kn = _rope(kn, cos[:, 0:KV_WIDTH], sup[:, 0:KV_WIDTH], sdn[:, 0:KV_WIDTH])
    else:
        knt = kn.T
        for s in range(kt_ref.shape[0]):
            kt_ref[s] = knt[:, s * seq:(s + 1) * seq]
    q_ref[...] = (qn * (HEAD_DIM ** -0.5 * LOG2_E)).astype(BF16)
    k_ref[...] = kn.astype(BF16)


def _in_part(tm, x2, mod, seq_len, norm_w, w_in, qw, kw, ones_q, ones_k, rope):
    t = x2.shape[0]
    latent = rope is not None
    per_seq = max(seq_len // tm, 1)
    row = lambda i: (i, 0)
    in_specs = [pl.BlockSpec((tm, D_MODEL), row),
                _mod_spec(mod, tm, seq_len),
                _const_spec((1, D_MODEL)),
                _const_spec((D_MODEL, D_IN)),
                _const_spec((1, Q_WIDTH)),
                _const_spec((1, KV_WIDTH)),
                _const_spec((Q_WIDTH, Q_WIDTH)),
                _const_spec((KV_WIDTH, KV_WIDTH))]
    args = [x2, mod.rows, norm_w, w_in, qw, kw, ones_q, ones_k]
    out_specs = [pl.BlockSpec((tm, ZH_WIDTH), row), pl.BlockSpec((tm, Q_WIDTH), row),
                 pl.BlockSpec((tm, KV_WIDTH), row)]
    out_shape = [jax.ShapeDtypeStruct((t, ZH_WIDTH), F32), jax.ShapeDtypeStruct((t, Q_WIDTH), BF16),
                 jax.ShapeDtypeStruct((t, KV_WIDTH), BF16)]
    if seq_len <= tm:
        assert tm % seq_len == 0
        t_spec = pl.BlockSpec((tm // seq_len, KV_WIDTH, seq_len), lambda i: (i, 0, 0))
    else:
        t_spec = pl.BlockSpec((1, KV_WIDTH, tm), lambda i: (i // per_seq, 0, i % per_seq))
    if latent:
        in_specs += [pl.BlockSpec((tm, Q_WIDTH), lambda i: (i % per_seq, 0))] * 3
        args += list(rope)
        out_specs.append(t_spec)
        out_shape.append(jax.ShapeDtypeStruct((t // seq_len, KV_WIDTH, seq_len), BF16))
    else:
        out_specs += [t_spec, t_spec]
        out_shape += [jax.ShapeDtypeStruct((t // seq_len, KV_WIDTH, seq_len), F32)] * 2
    out_specs.append(pl.BlockSpec((tm, 2 * D_MODEL), row))
    out_shape.append(jax.ShapeDtypeStruct((t, 2 * D_MODEL), BF16))
    cost = (_mm_cost(tm, D_MODEL, D_IN) + _mm_cost(tm, Q_WIDTH, Q_WIDTH) + _mm_cost(tm, KV_WIDTH, KV_WIDTH))
    return t // tm, _Part(functools.partial(_in_stages, latent), args, in_specs, out_specs, out_shape, [], cost)


def _attn_cost(lq, lks):
    per_head = sum(_mm_cost(lk, HEAD_DIM, lq) + _mm_cost(HEAD_DIM + ONES_ROWS, lk, lq) for lk in lks)
    return N_HEADS * per_head


def _attn_stages(k_transposed, lookahead, ins, outs, _):
    n_seg = len(k_transposed)
    q_ref, kv_refs = ins[0], ins[1:]
    (o_ref,) = outs
    q = q_ref[...]
    lq = q.shape[0]
    ks = [(kv_refs[2 * s][0].T if k_transposed[s] else kv_refs[2 * s][...]).astype(BF16) for s in range(n_seg)]
    vts = [kv_refs[2 * s + 1][0].astype(BF16) for s in range(n_seg)]
    vts = [[jnp.concatenate([vt[g * HEAD_DIM:(g + 1) * HEAD_DIM, :],
                             jnp.ones((ONES_ROWS, vt.shape[1]), BF16)], axis=0) for g in range(N_KV)]
           for vt in vts]
    score_cost = sum(_mm_cost(k.shape[0], HEAD_DIM, lq) for k in ks)
    value_cost = sum(_mm_cost(HEAD_DIM + ONES_ROWS, k.shape[0], lq) for k in ks)

    def scores(h):
        g = h // (N_HEADS // N_KV)
        qh = q[:, h * HEAD_DIM:(h + 1) * HEAD_DIM]
        return [_dot_nt(k[:, g * HEAD_DIM:(g + 1) * HEAD_DIM], qh) for k in ks]

    pending = []
    for h in range(lookahead):
        pending.append(scores(h))
        yield score_cost
    outs_t = []
    for h in range(N_HEADS):
        if h + lookahead < N_HEADS:
            pending.append(scores(h + lookahead))
            yield score_cost
        st = pending.pop(0)
        g = h // (N_HEADS // N_KV)
        m = functools.reduce(jnp.maximum, [jnp.max(s, axis=0, keepdims=True) for s in st])
        ot = functools.reduce(jnp.add, [_dot(vt[g], jnp.exp2(s - m).astype(BF16)) for vt, s in zip(vts, st)])
        outs_t.append(ot[0:HEAD_DIM, :] / ot[HEAD_DIM:HEAD_DIM + 1, :])
        if h == N_HEADS - 1:
            o_ref[...] = jnp.concatenate(outs_t, axis=0).T.astype(BF16)
        yield value_cost


def _attn_part(q, kv_segments, n_batch, seq_len, lookahead):
    t = q.shape[0]
    tq = ATTN_Q_TILE
    per_seq = seq_len // tq
    in_specs = [pl.BlockSpec((tq, Q_WIDTH), lambda i: (i, 0))]
    args = [q]
    for k, v_t, layer in kv_segments:
        lk = v_t.shape[-1]
        if v_t.ndim == 4:
            slab = pl.BlockSpec((1, None, KV_WIDTH, lk), lambda i, layer=layer: (i // per_seq, layer, 0, 0))
        else:
            slab = pl.BlockSpec((1, KV_WIDTH, lk), lambda i: (i // per_seq, 0, 0))
        k_spec = slab if k.ndim == v_t.ndim else pl.BlockSpec((lk, KV_WIDTH), lambda i: (i // per_seq, 0))
        in_specs += [k_spec, slab]
        args += [k, v_t]
    stages = functools.partial(_attn_stages, tuple(k.ndim == v_t.ndim for k, v_t, _ in kv_segments), lookahead)
    return n_batch * per_seq, _Part(stages, args, in_specs, [pl.BlockSpec((tq, Q_WIDTH), lambda i: (i, 0))],
                                    [jax.ShapeDtypeStruct((t, Q_WIDTH), BF16)], [],
                                    _attn_cost(tq, [v_t.shape[-1] for _, v_t, _ in kv_segments]))


def _hgrn_unit_cost():
    blk = HG_BLOCK
    return (2 * _mm_cost(blk, blk, 2 * HG_DK)
            + 2 * _mm_cost(blk, HG_DK, blk) + _mm_cost(blk, blk, HG_DV)
            + (blk // HG_CHUNK) * (_mm_cost(HG_DV, HG_CHUNK, 2 * HG_DK) + _mm_cost(HG_CHUNK, 2 * HG_DK, HG_DV)))


def _hgrn_stages(seq_len, hps, has_s0, has_sfin, ins, outs, scrs):
    lbl_ref, tf_ref, tb_ref, q_ref, ff_ref, fb_ref, v_ref, g_ref, nw_ref = ins[:9]
    s0_ref = ins[9] if has_s0 else None
    o_ref = outs[0]
    sfin_ref = outs[1] if has_sfin else None
    kv_scr, ss_scr, qd_scr, oi_scr, dec_scr = scrs
    n_blk = seq_len // HG_BLOCK
    n_chunk = seq_len // HG_CHUNK
    per_blk = HG_BLOCK // HG_CHUNK

    logits = lbl_ref[...]
    e = jnp.exp(logits - jnp.max(logits, axis=1, keepdims=True))
    lb = e[:, 0, :] / (e[:, 0, :] + e[:, 1, :])

    def chunk_cumsum(t_ref, x):
        hi, lo = _split_bf16(x)
        r = _dot(t_ref[...], jnp.concatenate([hi, lo], axis=1))
        return r[:, 0:HG_DK] + r[:, HG_DK:2 * HG_DK]

    def gate(hf, lbv):
        f = lbv + (1.0 - lbv) * jax.nn.sigmoid(hf)
        return jnp.log(f), 1.0 - f

    def chunk_edge_rows(x, first):
        off = 0 if first else HG_CHUNK - 1
        return [x[c * HG_CHUNK + off:c * HG_CHUNK + off + 1, :] for c in range(per_blk)]

    def spread_rows(rows):
        return jnp.concatenate([jnp.broadcast_to(r, (HG_CHUNK, HG_DK)) for r in rows], axis=0)

    units = [(blk, h) for blk in range(n_blk) for h in range(hps)]
    stage1 = []
    for blk, h in units:
        rows = slice(blk * HG_BLOCK, (blk + 1) * HG_BLOCK)
        cols = slice(h * HG_DK, (h + 1) * HG_DK)
        logf_f, k_f = gate(ff_ref[rows, cols], lb[0:1, cols])
        logf_b, k_b = gate(fb_ref[rows, cols], lb[1:2, cols])
        a_f = chunk_cumsum(tf_ref, logf_f)
        a_b = chunk_cumsum(tb_ref, logf_b)
        stage1.append((rows, cols, k_f, k_b, a_f, a_b))
        yield 2 * _mm_cost(HG_BLOCK, HG_BLOCK, 2 * HG_DK)
    stage2 = []
    for (blk, h), (rows, cols, k_f, k_b, a_f, a_b) in zip(units, stage1):
        q = q_ref[rows, cols] * HG_DK ** -0.5
        vb = v_ref[rows, cols].astype(BF16)
        ea_f = jnp.exp(a_f)
        ea_b = jnp.exp(a_b)
        qd_f = (q * ea_f).astype(BF16)
        qd_b = (q * ea_b).astype(BF16)
        kd_f = k_f * jnp.exp(-a_f)
        kd_b = k_b * jnp.exp(-a_b)
        dec_f = chunk_edge_rows(ea_f, first=False)
        dec_b = chunk_edge_rows(ea_b, first=True)
        ke = jnp.concatenate([kd_f * spread_rows(dec_f), kd_b * spread_rows(dec_b)], axis=1).astype(BF16)
        sc_f = _dot_nt(qd_f, kd_f.astype(BF16))
        sc_b = _dot_nt(qd_b, kd_b.astype(BF16))
        for c in range(per_blk):
            cr = slice(c * HG_CHUNK, (c + 1) * HG_CHUNK)
            kv_scr[h, blk * per_blk + c] = _dot_tn(vb[cr, :], ke[cr, :])
            dec_scr[h, blk * per_blk + c, 0:1, :] = jnp.concatenate([dec_f[c], dec_b[c]], axis=1)
        qd_scr[h, rows, :] = jnp.concatenate([qd_f, qd_b], axis=1)
        stage2.append((rows, cols, vb, sc_f, sc_b))
        yield 2 * _mm_cost(HG_BLOCK, HG_DK, HG_BLOCK) + per_blk * _mm_cost(HG_DV, HG_CHUNK, 2 * HG_DK)
    for rows, cols, vb, sc_f, sc_b in stage2:
        s = sc_f.astype(BF16) * tf_ref[...] + sc_b.astype(BF16) * tb_ref[...]
        oi_scr[rows, cols] = _dot(s, vb)
        yield _mm_cost(HG_BLOCK, HG_BLOCK, HG_DV)

    for h in range(hps):
        if has_s0:
            s_f, s_b = s0_ref[0, 0, h].T, s0_ref[0, 1, h].T
        else:
            s_f = s_b = jnp.zeros((HG_DV, HG_DK), F32)
        for cf in range(n_chunk):
            cb = n_chunk - 1 - cf
            ss_scr[h, cf, :, 0:HG_DK] = s_f.astype(BF16)
            s_f = dec_scr[h, cf, 0:1, 0:HG_DK] * s_f + kv_scr[h, cf, :, 0:HG_DK]
            ss_scr[h, cb, :, HG_DK:2 * HG_DK] = s_b.astype(BF16)
            s_b = dec_scr[h, cb, 0:1, HG_DK:2 * HG_DK] * s_b + kv_scr[h, cb, :, HG_DK:2 * HG_DK]
        if has_sfin:
            sfin_ref[0, 0, h] = s_f.T
            sfin_ref[0, 1, h] = s_b.T

    for blk, h in units:
        cols = slice(h * HG_DK, (h + 1) * HG_DK)
        for c in range(per_blk):
            ci = blk * per_blk + c
            rows = slice(ci * HG_CHUNK, (ci + 1) * HG_CHUNK)
            oi_scr[rows, cols] = oi_scr[rows, cols] + _dot_nt(qd_scr[h, rows, :], ss_scr[h, ci])
        yield per_blk * _mm_cost(HG_CHUNK, 2 * HG_DK, HG_DV)

    for h in range(hps):
        cols = slice(h * HG_DK, (h + 1) * HG_DK)
        o = _rms_rows(oi_scr[:, cols], nw_ref[...])
        g = g_ref[:, cols]
        o_ref[:, cols] = (o * (g * jax.nn.sigmoid(g))).astype(BF16)


def _chunk_masks():
    r = np.arange(HG_BLOCK)
    same = (r[:, None] // HG_CHUNK) == (r[None, :] // HG_CHUNK)
    lower = same & (r[None, :] <= r[:, None])
    upper = same & (r[None, :] >= r[:, None])
    return (jnp.asarray(lower.astype(np.float32), dtype=BF16),
            jnp.asarray(upper.astype(np.float32), dtype=BF16))


def _hgrn_part(zh, lb_logits, norm_w, n_batch, seq_len, hps, s0, want_final):
    t = zh.shape[0]
    n_chunk = seq_len // HG_CHUNK
    width = hps * HG_DK
    n_hb = HG_HEADS // hps
    t_fwd, t_bwd = _chunk_masks()

    def seg_spec(seg):
        return pl.BlockSpec((seq_len, width), lambda i: (i // n_hb, seg * n_hb + i % n_hb))

    state_spec = pl.BlockSpec((1, 2, hps, HG_DK, HG_DV), lambda i: (i // n_hb, 0, i % n_hb, 0, 0))
    mask_spec = pl.BlockSpec((HG_BLOCK, HG_BLOCK), lambda i: (0, 0))
    in_specs = [pl.BlockSpec((2, lb_logits.shape[1], width), lambda i: (0, 0, i % n_hb)),
                mask_spec, mask_spec,
                seg_spec(0), seg_spec(1), seg_spec(2), seg_spec(3), seg_spec(4),
                pl.BlockSpec((1, HG_DV), lambda i: (0, 0))]
    args = [lb_logits, t_fwd, t_bwd, zh, zh, zh, zh, zh, norm_w]
    if s0 is not None:
        states, layer = s0
        in_specs.append(pl.BlockSpec((1, None, 2, hps, HG_DK, HG_DV),
                                     lambda i: (i // n_hb, layer, 0, i % n_hb, 0, 0)))
        args.append(states)
    out_specs = [pl.BlockSpec((seq_len, width), lambda i: (i // n_hb, i % n_hb))]
    out_shape = [jax.ShapeDtypeStruct((t, HG_WIDTH), BF16)]
    if want_final:
        out_specs.append(state_spec)
        out_shape.append(jax.ShapeDtypeStruct((n_batch, 2, HG_HEADS, HG_DK, HG_DV), F32))
    scratch = [pltpu.VMEM((hps, n_chunk, HG_DV, 2 * HG_DK), F32),
               pltpu.VMEM((hps, n_chunk, HG_DV, 2 * HG_DK), BF16),
               pltpu.VMEM((hps, seq_len, 2 * HG_DK), BF16),
               pltpu.VMEM((seq_len, width), F32),
               pltpu.VMEM((hps, n_chunk, SUBLANES, 2 * HG_DK), F32)]
    stages = functools.partial(_hgrn_stages, seq_len, hps, s0 is not None, want_final)
    cost = hps * (seq_len // HG_BLOCK) * _hgrn_unit_cost()
    return n_batch * n_hb, _Part(stages, args, in_specs, out_specs, out_shape, scratch, cost)


def _out_stages(ins, outs, _):
    x_ref, mod_ref, oh_ref, oa_ref, g_ref, who_ref, wao_ref, wout_ref, nfw_ref, wff1_ref, wff2_ref, fnw_ref = ins
    (y_ref,) = outs
    tm = x_ref.shape[0]
    m = mod_ref[0]
    g1 = m[:, 2 * D_MODEL:3 * D_MODEL]
    sh2 = m[:, 3 * D_MODEL:4 * D_MODEL]
    sc2 = m[:, 4 * D_MODEL:5 * D_MODEL]
    g2 = m[:, 5 * D_MODEL:6 * D_MODEL]
    gates = g_ref[...].astype(F32)
    from_h = yield from _dot_by_cols(oh_ref[...], who_ref, (0, HG_WIDTH), (0, D_MODEL))
    from_a = yield from _dot_by_cols(oa_ref[...], wao_ref, (0, Q_WIDTH), (0, D_MODEL))
    merged = gates[:, 0:D_MODEL] * from_h + gates[:, D_MODEL:2 * D_MODEL] * from_a
    mixed = yield from _dot_by_cols(merged.astype(BF16), wout_ref, (0, D_MODEL), (0, D_MODEL))
    x1 = x_ref[...] + g1 * mixed
    h2 = (_rms_rows(x1, nfw_ref[...]) * (1.0 + sc2) + sh2).astype(BF16)
    acc = jnp.zeros_like(x1)
    for j in range(D_FF // D_MODEL):
        cols = slice(j * D_MODEL, (j + 1) * D_MODEL)
        hj = yield from _dot_by_cols(h2, wff1_ref, (0, D_MODEL), (cols.start, cols.stop))
        hj = jnp.maximum(hj, 0.0)
        acc = acc + (yield from _dot_by_cols((hj * hj).astype(BF16), wff2_ref, (cols.start, cols.stop), (0, D_MODEL)))
    y_ref[...] = _rms_rows(x1 + g2 * acc, fnw_ref[...])


def _out_part(tm, x2, mod, seq_len, oh, oa, gates, w_ho, w_ao, w_out, nfw, w_ff1, w_ff2, fnw):
    t = x2.shape[0]
    row = lambda i: (i, 0)
    in_specs = [pl.BlockSpec((tm, D_MODEL), row),
                _mod_spec(mod, tm, seq_len),
                pl.BlockSpec((tm, HG_WIDTH), row),
                pl.BlockSpec((tm, Q_WIDTH), row),
                pl.BlockSpec((tm, 2 * D_MODEL), row),
                _const_spec((HG_WIDTH, D_MODEL)),
                _const_spec((Q_WIDTH, D_MODEL)),
                _const_spec((D_MODEL, D_MODEL)),
                _const_spec((1, D_MODEL)),
                _const_spec((D_MODEL, D_FF)),
                _const_spec((D_FF, D_MODEL)),
                _const_spec((1, D_MODEL))]
    args = [x2, mod.rows, oh, oa, gates, w_ho, w_ao, w_out, nfw, w_ff1, w_ff2, fnw]
    cost = _mm_cost(tm, HG_WIDTH + Q_WIDTH, D_MODEL) + _mm_cost(tm, D_MODEL, D_MODEL) + 2 * _mm_cost(tm, D_MODEL, D_FF)
    return t // tm, _Part(_out_stages, args, in_specs, [pl.BlockSpec((tm, D_MODEL), row)],
                          [jax.ShapeDtypeStruct((t, D_MODEL), F32)], [], cost)


def _cast_stages(ins, outs, _):
    for src, dst in zip(ins, outs):
        dst[...] = src[...].astype(dst.dtype)
    yield 1


def _cast_part(arrays, n_steps):
    in_specs, out_specs, out_shape = [], [], []
    for a in arrays:
        rows, cols = a.shape
        assert rows % (n_steps * MXU_ROWS) == 0
        spec = pl.BlockSpec((rows // n_steps, cols), lambda i: (i, 0))
        in_specs.append(spec)
        out_specs.append(spec)
        out_shape.append(jax.ShapeDtypeStruct(a.shape, BF16))
    return n_steps, _Part(_cast_stages, list(arrays), in_specs, out_specs, out_shape, [], 1)


def _rope_tables(n_tokens):
    rows = n_tokens // GRID_W
    row = np.repeat(np.arange(rows, dtype=np.float32), GRID_W)
    col = np.tile(np.arange(GRID_W, dtype=np.float32), rows)
    axis_dim = HEAD_DIM // 2
    freqs = (ROPE_THETA ** (-np.arange(0, axis_dim, 2, dtype=np.float32) / axis_dim)).astype(np.float32)
    ang_r = row[:, None] * freqs
    ang_c = col[:, None] * freqs
    cr, sr, cc, sc = np.cos(ang_r), np.sin(ang_r), np.cos(ang_c), np.sin(ang_c)
    zero = np.zeros_like(sr)
    cos = np.concatenate([cr, cr, cc, cc], axis=-1)
    s_up = np.concatenate([-sr, zero, -sc, zero], axis=-1)
    s_dn = np.concatenate([zero, sr, zero, sc], axis=-1)
    return tuple(jnp.asarray(np.tile(a, (1, N_HEADS)), dtype=F32) for a in (cos, s_up, s_dn))


def _to_slab(kv):
    n, depth, length = kv.shape[:3]
    return jnp.transpose(kv, (0, 1, 3, 4, 2)).reshape(n, depth, KV_WIDTH, length)


def _from_slab(slab):
    n, _, length = slab.shape
    return jnp.transpose(slab.reshape(n, N_KV, HEAD_DIM, length), (0, 3, 1, 2))


def _block_ones(width):
    idx = np.arange(width) // HEAD_DIM
    return jnp.asarray((idx[:, None] == idx[None, :]).astype(np.float32), dtype=BF16)


def _same_steps(*counted_parts):
    steps = {n for n, _ in counted_parts}
    assert len(steps) == 1, steps
    return steps.pop(), [p for _, p in counted_parts]


def kernel(x_prompt, x_sample, cache_k, cache_v, state_hgrn, c, c_ctx, w_ada, b_ada, norm_mix_w, w_in, q_norm_w, k_norm_w, hgrn_lb_logits, hgrn_norm_w, w_hgrn_out, w_attn_out, w_out, norm_ffn_w, w_ff1, w_ff2, final_norm_w):
    n_p, l_p, _ = x_prompt.shape
    n_s, l_s, _ = x_sample.shape
    layer = 0

    cond = jnp.concatenate([c_ctx[None, :], c, jnp.zeros((COND_ROWS - 1 - n_s, D_MODEL), F32)], axis=0)
    mod = _modulation(cond, w_ada[layer], b_ada[layer][None, :]).reshape(COND_ROWS, 1, N_MOD * D_MODEL)
    mod_p = _Mod(mod, 0, True)
    mod_s = _Mod(mod, 1, False)

    in_small = (jnp.tile(q_norm_w[layer], N_HEADS)[None, :], jnp.tile(k_norm_w[layer], N_KV)[None, :],
                _block_ones(Q_WIDTH), _block_ones(KV_WIDTH))
    nmw = norm_mix_w[layer][None, :]
    hnw = hgrn_norm_w[layer][None, :]
    xp2 = x_prompt.reshape(n_p * l_p, D_MODEL)
    xs2 = x_sample.reshape(n_s * l_s, D_MODEL)

    n_in, in_p = _in_part(MIX_TOKEN_TILE, xp2, mod_p, l_p, nmw, w_in[layer], *in_small, None)
    steps, parts = _same_steps((n_in, in_p), _cast_part(
        [w_in[layer], w_hgrn_out[layer], w_attn_out[layer], w_out[layer], w_ff1[layer], w_ff2[layer]], n_in))
    (zh_p, q_p, k_p, kt_p, vt_p, gates_p), (w_in_b, w_ho_b, w_ao_b, w_out_b, w_ff1_b, w_ff2_b) = _launch(
        "in_ctx", steps, parts)
    out_w = (w_ho_b, w_ao_b, w_out_b, norm_ffn_w[layer][None, :], w_ff1_b, w_ff2_b, final_norm_w[None, :])

    steps, parts = _same_steps(
        _attn_part(q_p, [(k_p, vt_p, None)], n_p, l_p, N_HEADS),
        _in_part(MIX_TOKEN_TILE, xs2, mod_s, l_s, nmw, w_in_b, *in_small, _rope_tables(l_s)),
        _hgrn_part(zh_p, hgrn_lb_logits, hnw, n_p, l_p, HG_HEADS, None, True))
    (oa_p,), (zh_s, q_s, k_s, vt_s, gates_s), (oh_p, s_fin) = _launch("in_latent_mix_ctx", steps, parts,
                                                                      MIX_PACES_IN)

    segs = [(_to_slab(cache_k), _to_slab(cache_v), layer), (k_s, vt_s, None)]
    steps, parts = _same_steps(
        _out_part(MIX_TOKEN_TILE, xp2, mod_p, l_p, oh_p, oa_p, gates_p, *out_w),
        _attn_part(q_s, segs, n_s, l_s, ATTN_LOOKAHEAD),
        _hgrn_part(zh_s, hgrn_lb_logits, hnw, n_s, l_s, 1, (state_hgrn, layer), False))
    (y_p,), (oa_s,), (oh_s,) = _launch("out_ctx_mix_latent", steps, parts, MIX_PACES_OUT)

    steps, parts = _same_steps(_out_part(SOLO_TOKEN_TILE, xs2, mod_s, l_s, oh_s, oa_s, gates_s, *out_w))
    ((y_s,),) = _launch("out_latent", steps, parts)

    new_k = _from_slab(kt_p)[:, None]
    new_v = _from_slab(vt_p)[:, None]
    new_s = s_fin.reshape(n_p, 1, 2, HG_HEADS, HG_DK, HG_DV)
    return (y_p.reshape(n_p, l_p, D_MODEL), y_s.reshape(n_s, l_s, D_MODEL), new_k, new_v, new_s)
```

```python
import functools
from typing import Any, Callable, NamedTuple

import numpy as np
import jax
import jax.numpy as jnp
from jax import lax
from jax.experimental import pallas as pl
from jax.experimental.pallas import tpu as pltpu

F32 = jnp.float32
BF16 = jnp.bfloat16

D_MODEL = 1024
GRID_W = 64
EPS = 1e-6
HG_HEADS = 4
HG_DK = 128
HG_DV = 128
HG_WIDTH = HG_HEADS * HG_DK
HG_CHUNK = 32
N_HEADS = 8
N_KV = 2
HEAD_DIM = 64
Q_WIDTH = N_HEADS * HEAD_DIM
KV_WIDTH = N_KV * HEAD_DIM
ROPE_THETA = 10000.0
D_FF = 4 * D_MODEL
N_MOD = 6
ZH_WIDTH = 5 * HG_WIDTH
D_IN = ZH_WIDTH + Q_WIDTH + 2 * KV_WIDTH + 2 * D_MODEL

V7X_VMEM_LIMIT_BYTES = 56 * 1024 * 1024
SUBLANES = 8
MXU_ROWS = 16
HG_BLOCK = 256
COND_ROWS = 16
ONES_ROWS = 16
LOG2_E = float(np.log2(np.e))
MIX_TOKEN_TILE = 256
SOLO_TOKEN_TILE = 512
ADALN_COL_TILE = 1536
W_CAST_ROWS = 128
DOT_COLS = 512
ATTN_Q_TILE = 256
ATTN_LOOKAHEAD = 2
MIX_PACES_IN = (1.0, 1.0, 0.8)
MIX_PACES_OUT = (0.85, 1.0, 1.0)


def _dot(a, b):
    return jnp.dot(a, b, preferred_element_type=F32)


def _dot_nt(a, b):
    return lax.dot_general(a, b, (((1,), (1,)), ((), ())), preferred_element_type=F32)


def _dot_tn(a, b):
    return lax.dot_general(a, b, (((0,), (0,)), ((), ())), preferred_element_type=F32)


def _mm_cost(m, k, n):
    return (m // MXU_ROWS) * -(-k // 256) * -(-n // 256)


def _dot_by_cols(a, w_ref, rows, cols):
    pieces = []
    for c in range(cols[0], cols[1], DOT_COLS):
        end = min(c + DOT_COLS, cols[1])
        pieces.append(_dot(a, w_ref[rows[0]:rows[1], c:end]))
        yield _mm_cost(a.shape[0], rows[1] - rows[0], end - c)
    return pieces[0] if len(pieces) == 1 else jnp.concatenate(pieces, axis=1)


def _split_bf16(x):
    hi = x.astype(BF16)
    lo = (x - hi.astype(F32)).astype(BF16)
    return hi, lo


def _const_spec(shape):
    nd = len(shape)
    return pl.BlockSpec(shape, lambda *_: (0,) * nd, pipeline_mode=pl.Buffered(1))


class _Part(NamedTuple):
    stages: Callable[..., Any]
    args: list
    in_specs: list
    out_specs: list
    out_shape: list
    scratch: list
    cost: int


def _interleave(gens, totals):
    done = [0] * len(gens)
    live = list(range(len(gens)))
    while live:
        i = min(live, key=lambda j: done[j] / totals[j])
        try:
            done[i] += next(gens[i])
        except StopIteration:
            live.remove(i)


def _launch(name, n_steps, parts, paces=None):
    paces = paces or [1.0] * len(parts)
    n_in = [len(p.args) for p in parts]
    n_out = [len(p.out_shape) for p in parts]
    n_scr = [len(p.scratch) for p in parts]

    def body(*refs):
        refs = list(refs)
        ins = [[refs.pop(0) for _ in range(n)] for n in n_in]
        outs = [[refs.pop(0) for _ in range(n)] for n in n_out]
        scrs = [[refs.pop(0) for _ in range(n)] for n in n_scr]
        _interleave([p.stages(i, o, s) for p, i, o, s in zip(parts, ins, outs, scrs)],
                    [p.cost * pace for p, pace in zip(parts, paces)])

    flat = pl.pallas_call(
        body,
        grid=(n_steps,),
        in_specs=[s for p in parts for s in p.in_specs],
        out_specs=[s for p in parts for s in p.out_specs],
        out_shape=[s for p in parts for s in p.out_shape],
        scratch_shapes=[s for p in parts for s in p.scratch],
        compiler_params=pltpu.CompilerParams(dimension_semantics=("arbitrary",),
                                             vmem_limit_bytes=V7X_VMEM_LIMIT_BYTES),
        name=name,
    )(*[a for p in parts for a in p.args])
    flat = list(flat)
    return [[flat.pop(0) for _ in range(n)] for n in n_out]


class _Mod(NamedTuple):
    rows: Any
    first: int
    shared: bool


def _mod_spec(mod, tile, seq_len, tile_of_step=lambda i: i):
    if mod.shared:
        return pl.BlockSpec((1, 1, N_MOD * D_MODEL), lambda i: (mod.first, 0, 0))
    assert seq_len % tile == 0
    per_seq = seq_len // tile
    return pl.BlockSpec((1, 1, N_MOD * D_MODEL), lambda i: (mod.first + tile_of_step(i) // per_seq, 0, 0))


def _mod_kernel(cond_ref, w_ref, b_ref, o_ref):
    c = cond_ref[...]
    x = c * jax.nn.sigmoid(c)
    x_hi, x_lo = _split_bf16(x)
    w = w_ref[...].astype(BF16)
    o_ref[...] = _dot(x_hi, w) + _dot(x_lo, w) + b_ref[...]


def _modulation(cond, w_ada, b_ada):
    n = N_MOD * D_MODEL
    tn = ADALN_COL_TILE
    return pl.pallas_call(
        _mod_kernel,
        grid=(n // tn,),
        in_specs=[pl.BlockSpec((COND_ROWS, D_MODEL), lambda j: (0, 0)),
                  pl.BlockSpec((D_MODEL, tn), lambda j: (0, j)),
                  pl.BlockSpec((1, tn), lambda j: (0, j))],
        out_specs=pl.BlockSpec((COND_ROWS, tn), lambda j: (0, j)),
        out_shape=jax.ShapeDtypeStruct((COND_ROWS, n), F32),
        compiler_params=pltpu.CompilerParams(dimension_semantics=("arbitrary",),
                                             vmem_limit_bytes=V7X_VMEM_LIMIT_BYTES),
        name="adaln_mod",
    )(cond, w_ada, b_ada)


def _rms_rows(x, w):
    return x * lax.rsqrt(jnp.mean(x * x, axis=-1, keepdims=True) + EPS) * w


def _head_sumsq(a, ones_ref):
    return _dot((a * a).astype(BF16), ones_ref[...])


def _head_rms(a, sumsq, w):
    return a * lax.rsqrt(sumsq * (1.0 / HEAD_DIM) + EPS) * w


def _forget_lower_bound(logits):
    e = jnp.exp(logits - jnp.max(logits, axis=1, keepdims=True))
    return e[:, 0, :] / jnp.sum(e, axis=1)


def _rope(x, cos, s_up, s_dn):
    cols = []
    for j in range(x.shape[1] // 128):
        sl = slice(j * 128, (j + 1) * 128)
        xj = x[:, sl]
        cols.append(xj * cos[:, sl] + pltpu.roll(xj, 112, 1) * s_up[:, sl]
                    + pltpu.roll(xj, 16, 1) * s_dn[:, sl])
    return jnp.concatenate(cols, axis=1) if len(cols) > 1 else cols[0]


def _in_stages(latent, ins, outs, scrs):
    x_ref, mod_ref, nw_ref, w_ref, qw_ref, kw_ref, oq_ref, ok_ref, lbl_ref = ins[:9]
    if scrs:
        (w_bf_ref,) = scrs
        rows = W_CAST_ROWS

        @pl.when(pl.program_id(0) == 0)
        def _():
            def cast_rows(r, carry):
                sl = pl.ds(pl.multiple_of(r * rows, rows), rows)
                w_bf_ref[sl, :] = w_ref[sl, :].astype(BF16)
                return carry

            lax.fori_loop(0, w_ref.shape[0] // rows, cast_rows, 0)

        w_ref = w_bf_ref
    if latent:
        cos_ref, sup_ref, sdn_ref = ins[9:]
        zh_ref, q_ref, k_ref, vt_ref, g_ref = outs
    else:
        zh_ref, q_ref, k_ref, kt_ref, vt_ref, g_ref = outs
    tm = x_ref.shape[0]
    m = mod_ref[0]
    sh1 = m[:, 0:D_MODEL]
    sc1 = m[:, D_MODEL:2 * D_MODEL]
    h = _rms_rows(x_ref[...], nw_ref[...]) * (1.0 + sc1) + sh1
    hb = h.astype(BF16)
    c0 = ZH_WIDTH
    c1 = c0 + Q_WIDTH
    c2 = c1 + KV_WIDTH
    c3 = c2 + KV_WIDTH
    aq = _dot(hb, w_ref[:, c0:c1])
    ak = _dot(hb, w_ref[:, c1:c2])
    av = _dot(hb, w_ref[:, c2:c3])
    yield _mm_cost(tm, D_MODEL, c3 - c0)
    gl = yield from _dot_by_cols(hb, w_ref, (0, D_MODEL), (c3, D_IN))
    ssq = _head_sumsq(aq, oq_ref)
    ssk = _head_sumsq(ak, ok_ref)
    yield _mm_cost(tm, Q_WIDTH, Q_WIDTH) + _mm_cost(tm, KV_WIDTH, KV_WIDTH)
    lb = _forget_lower_bound(lbl_ref[...])

    def hgrn_segment(s):
        z = yield from _dot_by_cols(hb, w_ref, (0, D_MODEL), (s * HG_WIDTH, (s + 1) * HG_WIDTH))
        if s == 0:
            z = z * HG_DK ** -0.5
        elif s in (1, 2):
            lbv = lb[s - 1:s, :]
            z = jnp.log(lbv + (1.0 - lbv) * jax.nn.sigmoid(z))
        elif s == 4:
            z = z * jax.nn.sigmoid(z)
        zh_ref[:, s * HG_WIDTH:(s + 1) * HG_WIDTH] = z

    yield from hgrn_segment(1)
    yield from hgrn_segment(2)
    g_ref[...] = jax.nn.sigmoid(gl).astype(BF16)
    qn = _head_rms(aq, ssq, qw_ref[...])
    kn = _head_rms(ak, ssk, kw_ref[...])
    seq = vt_ref.shape[2]
    avt = av.T
    for s in range(vt_ref.shape[0]):
        vt_ref[s] = avt[:, s * seq:(s + 1) * seq].astype(vt_ref.dtype)
    for s in (4, 0, 3):
        yield from hgrn_segment(s)
    if latent:
        cos, sup, sdn = cos_ref[...], sup_ref[...], sdn_ref[...]
        qn = _rope(qn, cos, sup, sdn)
        kn = _rope(kn, cos[:, 0:KV_WIDTH], sup[:, 0:KV_WIDTH], sdn[:, 0:KV_WIDTH])
    else:
        knt = kn.T
        for s in range(kt_ref.shape[0]):
            kt_ref[s] = knt[:, s * seq:(s + 1) * seq]
    q_ref[...] = (qn * (HEAD_DIM ** -0.5 * LOG2_E)).astype(BF16)
    k_ref[...] = kn.astype(BF16)


def _in_part(tm, x2, mod, seq_len, norm_w, w_in, qw, kw, ones_q, ones_k, lb_logits, rope):
    t = x2.shape[0]
    latent = rope is not None
    per_seq = max(seq_len // tm, 1)
    row = lambda i: (i, 0)
    in_specs = [pl.BlockSpec((tm, D_MODEL), row),
                _mod_spec(mod, tm, seq_len),
                _const_spec((1, D_MODEL)),
                _const_spec((D_MODEL, D_IN)),
                _const_spec((1, Q_WIDTH)),
                _const_spec((1, KV_WIDTH)),
                _const_spec((Q_WIDTH, Q_WIDTH)),
                _const_spec((KV_WIDTH, KV_WIDTH)),
                _const_spec(lb_logits.shape)]
    args = [x2, mod.rows, norm_w, w_in, qw, kw, ones_q, ones_k, lb_logits]
    out_specs = [pl.BlockSpec((tm, ZH_WIDTH), row), pl.BlockSpec((tm, Q_WIDTH), row),
                 pl.BlockSpec((tm, KV_WIDTH), row)]
    out_shape = [jax.ShapeDtypeStruct((t, ZH_WIDTH), F32), jax.ShapeDtypeStruct((t, Q_WIDTH), BF16),
                 jax.ShapeDtypeStruct((t, KV_WIDTH), BF16)]
    if seq_len <= tm:
        assert tm % seq_len == 0
        t_spec = pl.BlockSpec((tm // seq_len, KV_WIDTH, seq_len), lambda i: (i, 0, 0))
    else:
        t_spec = pl.BlockSpec((1, KV_WIDTH, tm), lambda i: (i // per_seq, 0, i % per_seq))
    if latent:
        in_specs += [pl.BlockSpec((tm, Q_WIDTH), lambda i: (i % per_seq, 0))] * 3
        args += list(rope)
        out_specs.append(t_spec)
        out_shape.append(jax.ShapeDtypeStruct((t // seq_len, KV_WIDTH, seq_len), BF16))
    else:
        out_specs += [t_spec, t_spec]
        out_shape += [jax.ShapeDtypeStruct((t // seq_len, KV_WIDTH, seq_len), F32)] * 2
    out_specs.append(pl.BlockSpec((tm, 2 * D_MODEL), row))
    out_shape.append(jax.ShapeDtypeStruct((t, 2 * D_MODEL), BF16))
    cost = (_mm_cost(tm, D_MODEL, D_IN) + _mm_cost(tm, Q_WIDTH, Q_WIDTH) + _mm_cost(tm, KV_WIDTH, KV_WIDTH))
    scratch = [] if w_in.dtype == BF16 else [pltpu.VMEM((D_MODEL, D_IN), BF16)]
    return t // tm, _Part(functools.partial(_in_stages, latent), args, in_specs, out_specs, out_shape, scratch, cost)


def _attn_cost(lq, lks):
    per_head = sum(_mm_cost(lk, HEAD_DIM, lq) + _mm_cost(HEAD_DIM + ONES_ROWS, lk, lq) for lk in lks)
    return N_HEADS * per_head


def _attn_stages(k_transposed, lookahead, ins, outs, _):
    n_seg = len(k_transposed)
    q_ref, kv_refs = ins[0], ins[1:]
    (o_ref,) = outs
    q = q_ref[...]
    lq = q.shape[0]
    ks = [(kv_refs[2 * s][0].T if k_transposed[s] else kv_refs[2 * s][...]).astype(BF16) for s in range(n_seg)]
    vts = [kv_refs[2 * s + 1][0].astype(BF16) for s in range(n_seg)]
    vts = [[jnp.concatenate([vt[g * HEAD_DIM:(g + 1) * HEAD_DIM, :],
                             jnp.ones((ONES_ROWS, vt.shape[1]), BF16)], axis=0) for g in range(N_KV)]
           for vt in vts]
    score_cost = sum(_mm_cost(k.shape[0], HEAD_DIM, lq) for k in ks)
    value_cost = sum(_mm_cost(HEAD_DIM + ONES_ROWS, k.shape[0], lq) for k in ks)

    def scores(h):
        g = h // (N_HEADS // N_KV)
        qh = q[:, h * HEAD_DIM:(h + 1) * HEAD_DIM]
        return [_dot_nt(k[:, g * HEAD_DIM:(g + 1) * HEAD_DIM], qh) for k in ks]

    pending = []
    for h in range(lookahead):
        pending.append(scores(h))
        yield score_cost
    outs_t = []
    for h in range(N_HEADS):
        if h + lookahead < N_HEADS:
            pending.append(scores(h + lookahead))
            yield score_cost
        st = pending.pop(0)
        g = h // (N_HEADS // N_KV)
        m = functools.reduce(jnp.maximum, [jnp.max(s, axis=0, keepdims=True) for s in st])
        ot = functools.reduce(jnp.add, [_dot(vt[g], jnp.exp2(s - m).astype(BF16)) for vt, s in zip(vts, st)])
        outs_t.append(ot[0:HEAD_DIM, :] / ot[HEAD_DIM:HEAD_DIM + 1, :])
        if h == N_HEADS - 1:
            o_ref[...] = jnp.concatenate(outs_t, axis=0).T.astype(BF16)
        yield value_cost


def _attn_part(q, kv_segments, n_batch, seq_len, lookahead):
    t = q.shape[0]
    tq = ATTN_Q_TILE
    per_seq = seq_len // tq
    in_specs = [pl.BlockSpec((tq, Q_WIDTH), lambda i: (i, 0))]
    args = [q]
    for k, v_t, layer in kv_segments:
        lk = v_t.shape[-1]
        if v_t.ndim == 4:
            slab = pl.BlockSpec((1, None, KV_WIDTH, lk), lambda i, layer=layer: (i // per_seq, layer, 0, 0))
        else:
            slab = pl.BlockSpec((1, KV_WIDTH, lk), lambda i: (i // per_seq, 0, 0))
        k_spec = slab if k.ndim == v_t.ndim else pl.BlockSpec((lk, KV_WIDTH), lambda i: (i // per_seq, 0))
        in_specs += [k_spec, slab]
        args += [k, v_t]
    stages = functools.partial(_attn_stages, tuple(k.ndim == v_t.ndim for k, v_t, _ in kv_segments), lookahead)
    return n_batch * per_seq, _Part(stages, args, in_specs, [pl.BlockSpec((tq, Q_WIDTH), lambda i: (i, 0))],
                                    [jax.ShapeDtypeStruct((t, Q_WIDTH), BF16)], [],
                                    _attn_cost(tq, [v_t.shape[-1] for _, v_t, _ in kv_segments]))


def _hgrn_unit_cost():
    blk = HG_BLOCK
    return (2 * _mm_cost(blk, blk, 2 * HG_DK)
            + 2 * _mm_cost(blk, HG_DK, blk) + _mm_cost(blk, blk, HG_DV)
            + (blk // HG_CHUNK) * (_mm_cost(HG_DV, HG_CHUNK, 2 * HG_DK) + _mm_cost(HG_CHUNK, 2 * HG_DK, HG_DV)))


def _hgrn_stages(seq_len, hps, has_s0, has_sfin, ins, outs, scrs):
    tf_ref, tb_ref, q_ref, lf_ref, lb_ref, v_ref, g_ref, nw_ref = ins[:8]
    s0_ref = ins[8] if has_s0 else None
    o_ref = outs[0]
    sfin_ref = outs[1] if has_sfin else None
    kv_scr, ss_scr, qd_scr, oi_scr, dec_scr = scrs
    n_blk = seq_len // HG_BLOCK
    n_chunk = seq_len // HG_CHUNK
    per_blk = HG_BLOCK // HG_CHUNK

    def chunk_cumsum(t_ref, x):
        hi, lo = _split_bf16(x)
        r = _dot(t_ref[...], jnp.concatenate([hi, lo], axis=1))
        return r[:, 0:HG_DK] + r[:, HG_DK:2 * HG_DK]

    def chunk_edge_rows(x, first):
        off = 0 if first else HG_CHUNK - 1
        return [x[c * HG_CHUNK + off:c * HG_CHUNK + off + 1, :] for c in range(per_blk)]

    def spread_rows(rows):
        return jnp.concatenate([jnp.broadcast_to(r, (HG_CHUNK, HG_DK)) for r in rows], axis=0)

    units = [(blk, h) for blk in range(n_blk) for h in range(hps)]
    stage1 = []
    for blk, h in units:
        rows = slice(blk * HG_BLOCK, (blk + 1) * HG_BLOCK)
        cols = slice(h * HG_DK, (h + 1) * HG_DK)
        logf_f = lf_ref[rows, cols]
        logf_b = lb_ref[rows, cols]
        k_f = 1.0 - jnp.exp(logf_f)
        k_b = 1.0 - jnp.exp(logf_b)
        a_f = chunk_cumsum(tf_ref, logf_f)
        a_b = chunk_cumsum(tb_ref, logf_b)
        stage1.append((rows, cols, k_f, k_b, a_f, a_b))
        yield 2 * _mm_cost(HG_BLOCK, HG_BLOCK, 2 * HG_DK)
    stage2 = []
    for (blk, h), (rows, cols, k_f, k_b, a_f, a_b) in zip(units, stage1):
        q = q_ref[rows, cols]
        vb = v_ref[rows, cols].astype(BF16)
        ea_f = jnp.exp(a_f)
        ea_b = jnp.exp(a_b)
        qd_f = (q * ea_f).astype(BF16)
        qd_b = (q * ea_b).astype(BF16)
        kd_f = k_f * jnp.exp(-a_f)
        kd_b = k_b * jnp.exp(-a_b)
        dec_f = chunk_edge_rows(ea_f, first=False)
        dec_b = chunk_edge_rows(ea_b, first=True)
        ke = jnp.concatenate([kd_f * spread_rows(dec_f), kd_b * spread_rows(dec_b)], axis=1).astype(BF16)
        sc_f = _dot_nt(qd_f, kd_f.astype(BF16))
        sc_b = _dot_nt(qd_b, kd_b.astype(BF16))
        for c in range(per_blk):
            cr = slice(c * HG_CHUNK, (c + 1) * HG_CHUNK)
            kv_scr[h, blk * per_blk + c] = _dot_tn(vb[cr, :], ke[cr, :])
            dec_scr[h, blk * per_blk + c, 0:1, :] = jnp.concatenate([dec_f[c], dec_b[c]], axis=1)
        qd_scr[h, rows, :] = jnp.concatenate([qd_f, qd_b], axis=1)
        stage2.append((rows, cols, vb, sc_f, sc_b))
        yield 2 * _mm_cost(HG_BLOCK, HG_DK, HG_BLOCK) + per_blk * _mm_cost(HG_DV, HG_CHUNK, 2 * HG_DK)
    for rows, cols, vb, sc_f, sc_b in stage2:
        s = sc_f.astype(BF16) * tf_ref[...] + sc_b.astype(BF16) * tb_ref[...]
        oi_scr[rows, cols] = _dot(s, vb)
        yield _mm_cost(HG_BLOCK, HG_BLOCK, HG_DV)

    for h in range(hps):
        if has_s0:
            s_f, s_b = s0_ref[0, 0, h].T, s0_ref[0, 1, h].T
        else:
            s_f = s_b = jnp.zeros((HG_DV, HG_DK), F32)
        for cf in range(n_chunk):
            cb = n_chunk - 1 - cf
            ss_scr[h, cf, :, 0:HG_DK] = s_f.astype(BF16)
            s_f = dec_scr[h, cf, 0:1, 0:HG_DK] * s_f + kv_scr[h, cf, :, 0:HG_DK]
            ss_scr[h, cb, :, HG_DK:2 * HG_DK] = s_b.astype(BF16)
            s_b = dec_scr[h, cb, 0:1, HG_DK:2 * HG_DK] * s_b + kv_scr[h, cb, :, HG_DK:2 * HG_DK]
        if has_sfin:
            sfin_ref[0, 0, h] = s_f.T
            sfin_ref[0, 1, h] = s_b.T

    for blk, h in units:
        cols = slice(h * HG_DK, (h + 1) * HG_DK)
        for c in range(per_blk):
            ci = blk * per_blk + c
            rows = slice(ci * HG_CHUNK, (ci + 1) * HG_CHUNK)
            oi_scr[rows, cols] = oi_scr[rows, cols] + _dot_nt(qd_scr[h, rows, :], ss_scr[h, ci])
        yield per_blk * _mm_cost(HG_CHUNK, 2 * HG_DK, HG_DV)

    for h in range(hps):
        cols = slice(h * HG_DK, (h + 1) * HG_DK)
        o = _rms_rows(oi_scr[:, cols], nw_ref[...])
        o_ref[:, cols] = (o * g_ref[:, cols]).astype(BF16)


def _chunk_masks():
    r = np.arange(HG_BLOCK)
    same = (r[:, None] // HG_CHUNK) == (r[None, :] // HG_CHUNK)
    lower = same & (r[None, :] <= r[:, None])
    upper = same & (r[None, :] >= r[:, None])
    return (jnp.asarray(lower.astype(np.float32), dtype=BF16),
            jnp.asarray(upper.astype(np.float32), dtype=BF16))


def _hgrn_part(zh, norm_w, n_batch, seq_len, hps, s0, want_final):
    t = zh.shape[0]
    n_chunk = seq_len // HG_CHUNK
    width = hps * HG_DK
    n_hb = HG_HEADS // hps
    t_fwd, t_bwd = _chunk_masks()

    def seg_spec(seg):
        return pl.BlockSpec((seq_len, width), lambda i: (i // n_hb, seg * n_hb + i % n_hb))

    state_spec = pl.BlockSpec((1, 2, hps, HG_DK, HG_DV), lambda i: (i // n_hb, 0, i % n_hb, 0, 0))
    mask_spec = pl.BlockSpec((HG_BLOCK, HG_BLOCK), lambda i: (0, 0))
    in_specs = [mask_spec, mask_spec,
                seg_spec(0), seg_spec(1), seg_spec(2), seg_spec(3), seg_spec(4),
                pl.BlockSpec((1, HG_DV), lambda i: (0, 0))]
    args = [t_fwd, t_bwd, zh, zh, zh, zh, zh, norm_w]
    if s0 is not None:
        states, layer = s0
        in_specs.append(pl.BlockSpec((1, None, 2, hps, HG_DK, HG_DV),
                                     lambda i: (i // n_hb, layer, 0, i % n_hb, 0, 0)))
        args.append(states)
    out_specs = [pl.BlockSpec((seq_len, width), lambda i: (i // n_hb, i % n_hb))]
    out_shape = [jax.ShapeDtypeStruct((t, HG_WIDTH), BF16)]
    if want_final:
        out_specs.append(state_spec)
        out_shape.append(jax.ShapeDtypeStruct((n_batch, 2, HG_HEADS, HG_DK, HG_DV), F32))
    scratch = [pltpu.VMEM((hps, n_chunk, HG_DV, 2 * HG_DK), F32),
               pltpu.VMEM((hps, n_chunk, HG_DV, 2 * HG_DK), BF16),
               pltpu.VMEM((hps, seq_len, 2 * HG_DK), BF16),
               pltpu.VMEM((seq_len, width), F32),
               pltpu.VMEM((hps, n_chunk, SUBLANES, 2 * HG_DK), F32)]
    stages = functools.partial(_hgrn_stages, seq_len, hps, s0 is not None, want_final)
    cost = hps * (seq_len // HG_BLOCK) * _hgrn_unit_cost()
    return n_batch * n_hb, _Part(stages, args, in_specs, out_specs, out_shape, scratch, cost)


def _out_stages(ins, outs, _):
    x_ref, mod_ref, oh_ref, oa_ref, g_ref, who_ref, wao_ref, wout_ref, nfw_ref, wff1_ref, wff2_ref, fnw_ref = ins
    (y_ref,) = outs
    tm = x_ref.shape[0]
    m = mod_ref[0]
    g1 = m[:, 2 * D_MODEL:3 * D_MODEL]
    sh2 = m[:, 3 * D_MODEL:4 * D_MODEL]
    sc2 = m[:, 4 * D_MODEL:5 * D_MODEL]
    g2 = m[:, 5 * D_MODEL:6 * D_MODEL]
    gates = g_ref[...].astype(F32)
    from_h = yield from _dot_by_cols(oh_ref[...], who_ref, (0, HG_WIDTH), (0, D_MODEL))
    from_a = yield from _dot_by_cols(oa_ref[...], wao_ref, (0, Q_WIDTH), (0, D_MODEL))
    merged = gates[:, 0:D_MODEL] * from_h + gates[:, D_MODEL:2 * D_MODEL] * from_a
    mixed = yield from _dot_by_cols(merged.astype(BF16), wout_ref, (0, D_MODEL), (0, D_MODEL))
    x1 = x_ref[...] + g1 * mixed
    h2 = (_rms_rows(x1, nfw_ref[...]) * (1.0 + sc2) + sh2).astype(BF16)
    acc = jnp.zeros_like(x1)
    for j in range(D_FF // D_MODEL):
        cols = slice(j * D_MODEL, (j + 1) * D_MODEL)
        hj = yield from _dot_by_cols(h2, wff1_ref, (0, D_MODEL), (cols.start, cols.stop))
        hj = jnp.maximum(hj, 0.0)
        acc = acc + (yield from _dot_by_cols((hj * hj).astype(BF16), wff2_ref, (cols.start, cols.stop), (0, D_MODEL)))
    y_ref[...] = _rms_rows(x1 + g2 * acc, fnw_ref[...])


def _out_part(tm, x2, mod, seq_len, oh, oa, gates, w_ho, w_ao, w_out, nfw, w_ff1, w_ff2, fnw):
    t = x2.shape[0]
    row = lambda i: (i, 0)
    in_specs = [pl.BlockSpec((tm, D_MODEL), row),
                _mod_spec(mod, tm, seq_len),
                pl.BlockSpec((tm, HG_WIDTH), row),
                pl.BlockSpec((tm, Q_WIDTH), row),
                pl.BlockSpec((tm, 2 * D_MODEL), row),
                _const_spec((HG_WIDTH, D_MODEL)),
                _const_spec((Q_WIDTH, D_MODEL)),
                _const_spec((D_MODEL, D_MODEL)),
                _const_spec((1, D_MODEL)),
                _const_spec((D_MODEL, D_FF)),
                _const_spec((D_FF, D_MODEL)),
                _const_spec((1, D_MODEL))]
    args = [x2, mod.rows, oh, oa, gates, w_ho, w_ao, w_out, nfw, w_ff1, w_ff2, fnw]
    cost = _mm_cost(tm, HG_WIDTH + Q_WIDTH, D_MODEL) + _mm_cost(tm, D_MODEL, D_MODEL) + 2 * _mm_cost(tm, D_MODEL, D_FF)
    return t // tm, _Part(_out_stages, args, in_specs, [pl.BlockSpec((tm, D_MODEL), row)],
                          [jax.ShapeDtypeStruct((t, D_MODEL), F32)], [], cost)


def _cast_stages(ins, outs, _):
    for src, dst in zip(ins, outs):
        dst[...] = src[...].astype(dst.dtype)
    yield 1


def _cast_part(arrays, n_steps):
    in_specs, out_specs, out_shape = [], [], []
    for a in arrays:
        rows, cols = a.shape
        assert rows % (n_steps * MXU_ROWS) == 0
        spec = pl.BlockSpec((rows // n_steps, cols), lambda i: (i, 0))
        in_specs.append(spec)
        out_specs.append(spec)
        out_shape.append(jax.ShapeDtypeStruct(a.shape, BF16))
    return n_steps, _Part(_cast_stages, list(arrays), in_specs, out_specs, out_shape, [], 1)


def _rope_tables(n_tokens):
    rows = n_tokens // GRID_W
    row = np.repeat(np.arange(rows, dtype=np.float32), GRID_W)
    col = np.tile(np.arange(GRID_W, dtype=np.float32), rows)
    axis_dim = HEAD_DIM // 2
    freqs = (ROPE_THETA ** (-np.arange(0, axis_dim, 2, dtype=np.float32) / axis_dim)).astype(np.float32)
    ang_r = row[:, None] * freqs
    ang_c = col[:, None] * freqs
    cr, sr, cc, sc = np.cos(ang_r), np.sin(ang_r), np.cos(ang_c), np.sin(ang_c)
    zero = np.zeros_like(sr)
    cos = np.concatenate([cr, cr, cc, cc], axis=-1)
    s_up = np.concatenate([-sr, zero, -sc, zero], axis=-1)
    s_dn = np.concatenate([zero, sr, zero, sc], axis=-1)
    return tuple(jnp.asarray(np.tile(a, (1, N_HEADS)), dtype=F32) for a in (cos, s_up, s_dn))


def _to_slab(kv):
    n, depth, length = kv.shape[:3]
    return jnp.transpose(kv, (0, 1, 3, 4, 2)).reshape(n, depth, KV_WIDTH, length)


def _from_slab(slab):
    n, _, length = slab.shape
    return jnp.transpose(slab.reshape(n, N_KV, HEAD_DIM, length), (0, 3, 1, 2))


def _block_ones(width):
    idx = np.arange(width) // HEAD_DIM
    return jnp.asarray((idx[:, None] == idx[None, :]).astype(np.float32), dtype=BF16)


def _same_steps(*counted_parts):
    steps = {n for n, _ in counted_parts}
    assert len(steps) == 1, steps
    return steps.pop(), [p for _, p in counted_parts]


def kernel(x_prompt, x_sample, cache_k, cache_v, state_hgrn, c, c_ctx, w_ada, b_ada, norm_mix_w, w_in, q_norm_w, k_norm_w, hgrn_lb_logits, hgrn_norm_w, w_hgrn_out, w_attn_out, w_out, norm_ffn_w, w_ff1, w_ff2, final_norm_w):
    n_p, l_p, _ = x_prompt.shape
    n_s, l_s, _ = x_sample.shape
    layer = 0

    cond = jnp.concatenate([c_ctx[None, :], c, jnp.zeros((COND_ROWS - 1 - n_s, D_MODEL), F32)], axis=0)
    mod = _modulation(cond, w_ada[layer], b_ada[layer][None, :]).reshape(COND_ROWS, 1, N_MOD * D_MODEL)
    mod_p = _Mod(mod, 0, True)
    mod_s = _Mod(mod, 1, False)

    in_small = (jnp.tile(q_norm_w[layer], N_HEADS)[None, :], jnp.tile(k_norm_w[layer], N_KV)[None, :],
                _block_ones(Q_WIDTH), _block_ones(KV_WIDTH), hgrn_lb_logits)
    nmw = norm_mix_w[layer][None, :]
    hnw = hgrn_norm_w[layer][None, :]
    xp2 = x_prompt.reshape(n_p * l_p, D_MODEL)
    xs2 = x_sample.reshape(n_s * l_s, D_MODEL)

    n_in, in_p = _in_part(MIX_TOKEN_TILE, xp2, mod_p, l_p, nmw, w_in[layer], *in_small, None)
    steps, parts = _same_steps((n_in, in_p), _cast_part(
        [w_in[layer], w_hgrn_out[layer], w_attn_out[layer], w_out[layer], w_ff1[layer], w_ff2[layer]], n_in))
    (zh_p, q_p, k_p, kt_p, vt_p, gates_p), (w_in_b, w_ho_b, w_ao_b, w_out_b, w_ff1_b, w_ff2_b) = _launch(
        "in_ctx", steps, parts)
    out_w = (w_ho_b, w_ao_b, w_out_b, norm_ffn_w[layer][None, :], w_ff1_b, w_ff2_b, final_norm_w[None, :])

    steps, parts = _same_steps(
        _attn_part(q_p, [(k_p, vt_p, None)], n_p, l_p, N_HEADS),
        _in_part(MIX_TOKEN_TILE, xs2, mod_s, l_s, nmw, w_in_b, *in_small, _rope_tables(l_s)),
        _hgrn_part(zh_p, hnw, n_p, l_p, HG_HEADS, None, True))
    (oa_p,), (zh_s, q_s, k_s, vt_s, gates_s), (oh_p, s_fin) = _launch("in_latent_mix_ctx", steps, parts,
                                                                      MIX_PACES_IN)

    segs = [(_to_slab(cache_k), _to_slab(cache_v), layer), (k_s, vt_s, None)]
    steps, parts = _same_steps(
        _out_part(MIX_TOKEN_TILE, xp2, mod_p, l_p, oh_p, oa_p, gates_p, *out_w),
        _attn_part(q_s, segs, n_s, l_s, ATTN_LOOKAHEAD),
        _hgrn_part(zh_s, hnw, n_s, l_s, 1, (state_hgrn, layer), False))
    (y_p,), (oa_s,), (oh_s,) = _launch("out_ctx_mix_latent", steps, parts, MIX_PACES_OUT)

    steps, parts = _same_steps(_out_part(SOLO_TOKEN_TILE, xs2, mod_s, l_s, oh_s, oa_s, gates_s, *out_w))
    ((y_s,),) = _launch("out_latent", steps, parts)

    new_k = _from_slab(kt_p)[:, None]
    new_v = _from_slab(vt_p)[:, None]
    new_s = s_fin.reshape(n_p, 1, 2, HG_HEADS, HG_DK, HG_DV)
    return (y_p.reshape(n_p, l_p, D_MODEL), y_s.reshape(n_s, l_s, D_MODEL), new_k, new_v, new_s)
```

```python
import functools
from typing import Any, Callable, NamedTuple

import numpy as np
import jax
import jax.numpy as jnp
from jax import lax
from jax.experimental import pallas as pl
from jax.experimental.pallas import tpu as pltpu

F32 = jnp.float32
BF16 = jnp.bfloat16

D_MODEL = 1024
GRID_W = 64
EPS = 1e-6
HG_HEADS = 4
HG_DK = 128
HG_DV = 128
HG_WIDTH = HG_HEADS * HG_DK
HG_CHUNK = 32
N_HEADS = 8
N_KV = 2
HEAD_DIM = 64
Q_WIDTH = N_HEADS * HEAD_DIM
KV_WIDTH = N_KV * HEAD_DIM
ROPE_THETA = 10000.0
D_FF = 4 * D_MODEL
N_MOD = 6
ZH_WIDTH = 5 * HG_WIDTH
D_IN = ZH_WIDTH + Q_WIDTH + 2 * KV_WIDTH + 2 * D_MODEL

V7X_VMEM_LIMIT_BYTES = 56 * 1024 * 1024
SUBLANES = 8
MXU_ROWS = 16
HG_BLOCK = 256
COND_ROWS = 16
ONES_ROWS = 16
LOG2_E = float(np.log2(np.e))
MIX_TOKEN_TILE = 256
SOLO_TOKEN_TILE = 512
ADALN_COL_TILE = 1536
W_CAST_ROWS = 128
DOT_COLS = 512
ATTN_Q_TILE = 256
ATTN_LOOKAHEAD = 1
MIX_PACES_IN = (1.0, 1.0, 0.8)
MIX_PACES_OUT = (0.85, 1.0, 1.0)


def _dot(a, b):
    return jnp.dot(a, b, preferred_element_type=F32)


def _dot_nt(a, b):
    return lax.dot_general(a, b, (((1,), (1,)), ((), ())), preferred_element_type=F32)


def _dot_tn(a, b):
    return lax.dot_general(a, b, (((0,), (0,)), ((), ())), preferred_element_type=F32)


def _mm_cost(m, k, n):
    return (m // MXU_ROWS) * -(-k // 256) * -(-n // 256)


def _dot_by_cols(a, w_ref, rows, cols):
    pieces = []
    for c in range(cols[0], cols[1], DOT_COLS):
        end = min(c + DOT_COLS, cols[1])
        pieces.append(_dot(a, w_ref[rows[0]:rows[1], c:end]))
        yield _mm_cost(a.shape[0], rows[1] - rows[0], end - c)
    return pieces[0] if len(pieces) == 1 else jnp.concatenate(pieces, axis=1)


def _split_bf16(x):
    hi = x.astype(BF16)
    lo = (x - hi.astype(F32)).astype(BF16)
    return hi, lo


def _const_spec(shape):
    nd = len(shape)
    return pl.BlockSpec(shape, lambda *_: (0,) * nd, pipeline_mode=pl.Buffered(1))


class _Part(NamedTuple):
    stages: Callable[..., Any]
    args: list
    in_specs: list
    out_specs: list
    out_shape: list
    scratch: list
    cost: int


def _interleave(gens, totals):
    done = [0] * len(gens)
    live = list(range(len(gens)))
    while live:
        i = min(live, key=lambda j: done[j] / totals[j])
        try:
            done[i] += next(gens[i])
        except StopIteration:
            live.remove(i)


def _launch(name, n_steps, parts, paces=None):
    paces = paces or [1.0] * len(parts)
    n_in = [len(p.args) for p in parts]
    n_out = [len(p.out_shape) for p in parts]
    n_scr = [len(p.scratch) for p in parts]

    def body(*refs):
        refs = list(refs)
        ins = [[refs.pop(0) for _ in range(n)] for n in n_in]
        outs = [[refs.pop(0) for _ in range(n)] for n in n_out]
        scrs = [[refs.pop(0) for _ in range(n)] for n in n_scr]
        _interleave([p.stages(i, o, s) for p, i, o, s in zip(parts, ins, outs, scrs)],
                    [p.cost * pace for p, pace in zip(parts, paces)])

    flat = pl.pallas_call(
        body,
        grid=(n_steps,),
        in_specs=[s for p in parts for s in p.in_specs],
        out_specs=[s for p in parts for s in p.out_specs],
        out_shape=[s for p in parts for s in p.out_shape],
        scratch_shapes=[s for p in parts for s in p.scratch],
        compiler_params=pltpu.CompilerParams(dimension_semantics=("arbitrary",),
                                             vmem_limit_bytes=V7X_VMEM_LIMIT_BYTES),
        name=name,
    )(*[a for p in parts for a in p.args])
    flat = list(flat)
    return [[flat.pop(0) for _ in range(n)] for n in n_out]


class _Mod(NamedTuple):
    rows: Any
    first: int
    shared: bool


def _mod_spec(mod, tile, seq_len, tile_of_step=lambda i: i):
    if mod.shared:
        return pl.BlockSpec((1, 1, N_MOD * D_MODEL), lambda i: (mod.first, 0, 0))
    assert seq_len % tile == 0
    per_seq = seq_len // tile
    return pl.BlockSpec((1, 1, N_MOD * D_MODEL), lambda i: (mod.first + tile_of_step(i) // per_seq, 0, 0))


def _mod_kernel(cond_ref, w_ref, b_ref, o_ref):
    c = cond_ref[...]
    x = c * jax.nn.sigmoid(c)
    x_hi, x_lo = _split_bf16(x)
    w = w_ref[...].astype(BF16)
    o_ref[...] = _dot(x_hi, w) + _dot(x_lo, w) + b_ref[...]


def _modulation(cond, w_ada, b_ada):
    n = N_MOD * D_MODEL
    tn = ADALN_COL_TILE
    return pl.pallas_call(
        _mod_kernel,
        grid=(n // tn,),
        in_specs=[pl.BlockSpec((COND_ROWS, D_MODEL), lambda j: (0, 0)),
                  pl.BlockSpec((D_MODEL, tn), lambda j: (0, j)),
                  pl.BlockSpec((1, tn), lambda j: (0, j))],
        out_specs=pl.BlockSpec((COND_ROWS, tn), lambda j: (0, j)),
        out_shape=jax.ShapeDtypeStruct((COND_ROWS, n), F32),
        compiler_params=pltpu.CompilerParams(dimension_semantics=("arbitrary",),
                                             vmem_limit_bytes=V7X_VMEM_LIMIT_BYTES),
        name="adaln_mod",
    )(cond, w_ada, b_ada)


def _rms_rows(x, w):
    return x * lax.rsqrt(jnp.mean(x * x, axis=-1, keepdims=True) + EPS) * w


def _head_sumsq(a, ones_ref):
    return _dot((a * a).astype(BF16), ones_ref[...])


def _head_rms(a, sumsq, w):
    return a * lax.rsqrt(sumsq * (1.0 / HEAD_DIM) + EPS) * w


def _forget_lower_bound(logits):
    e = jnp.exp(logits - jnp.max(logits, axis=1, keepdims=True))
    return e[:, 0, :] / jnp.sum(e, axis=1)


def _rope(x, cos, s_up, s_dn):
    cols = []
    for j in range(x.shape[1] // 128):
        sl = slice(j * 128, (j + 1) * 128)
        xj = x[:, sl]
        cols.append(xj * cos[:, sl] + pltpu.roll(xj, 112, 1) * s_up[:, sl]
                    + pltpu.roll(xj, 16, 1) * s_dn[:, sl])
    return jnp.concatenate(cols, axis=1) if len(cols) > 1 else cols[0]


def _in_stages(latent, ins, outs, scrs):
    x_ref, mod_ref, nw_ref, w_ref, qw_ref, kw_ref, oq_ref, ok_ref, lbl_ref = ins[:9]
    if scrs:
        (w_bf_ref,) = scrs
        rows = W_CAST_ROWS

        @pl.when(pl.program_id(0) == 0)
        def _():
            def cast_rows(r, carry):
                sl = pl.ds(pl.multiple_of(r * rows, rows), rows)
                w_bf_ref[sl, :] = w_ref[sl, :].astype(BF16)
                return carry

            lax.fori_loop(0, w_ref.shape[0] // rows, cast_rows, 0)

        w_ref = w_bf_ref
    if latent:
        cos_ref, sup_ref, sdn_ref = ins[9:]
        zh_ref, q_ref, k_ref, vt_ref, g_ref = outs
    else:
        zh_ref, q_ref, k_ref, kt_ref, vt_ref, g_ref = outs
    tm = x_ref.shape[0]
    m = mod_ref[0]
    sh1 = m[:, 0:D_MODEL]
    sc1 = m[:, D_MODEL:2 * D_MODEL]
    h = _rms_rows(x_ref[...], nw_ref[...]) * (1.0 + sc1) + sh1
    hb = h.astype(BF16)
    c0 = ZH_WIDTH
    c1 = c0 + Q_WIDTH
    c2 = c1 + KV_WIDTH
    c3 = c2 + KV_WIDTH
    aq = _dot(hb, w_ref[:, c0:c1])
    ak = _dot(hb, w_ref[:, c1:c2])
    av = _dot(hb, w_ref[:, c2:c3])
    yield _mm_cost(tm, D_MODEL, c3 - c0)
    gl = yield from _dot_by_cols(hb, w_ref, (0, D_MODEL), (c3, D_IN))
    ssq = _head_sumsq(aq, oq_ref)
    ssk = _head_sumsq(ak, ok_ref)
    yield _mm_cost(tm, Q_WIDTH, Q_WIDTH) + _mm_cost(tm, KV_WIDTH, KV_WIDTH)
    lb = _forget_lower_bound(lbl_ref[...])

    def hgrn_segment(s):
        z = yield from _dot_by_cols(hb, w_ref, (0, D_MODEL), (s * HG_WIDTH, (s + 1) * HG_WIDTH))
        if s == 0:
            z = z * HG_DK ** -0.5
        elif s in (1, 2):
            lbv = lb[s - 1:s, :]
            z = jnp.log(lbv + (1.0 - lbv) * jax.nn.sigmoid(z))
        elif s == 4:
            z = z * jax.nn.sigmoid(z)
        zh_ref[:, s * HG_WIDTH:(s + 1) * HG_WIDTH] = z

    yield from hgrn_segment(1)
    yield from hgrn_segment(2)
    g_ref[...] = jax.nn.sigmoid(gl).astype(BF16)
    qn = _head_rms(aq, ssq, qw_ref[...])
    kn = _head_rms(ak, ssk, kw_ref[...])
    seq = vt_ref.shape[2]
    avt = av.T
    for s in range(vt_ref.shape[0]):
        vt_ref[s] = avt[:, s * seq:(s + 1) * seq].astype(vt_ref.dtype)
    for s in (4, 0, 3):
        yield from hgrn_segment(s)
    if latent:
        cos, sup, sdn = cos_ref[...], sup_ref[...], sdn_ref[...]
        qn = _rope(qn, cos, sup, sdn)
        kn = _rope(kn, cos[:, 0:KV_WIDTH], sup[:, 0:KV_WIDTH], sdn[:, 0:KV_WIDTH])
    else:
        knt = kn.T
        for s in range(kt_ref.shape[0]):
            kt_ref[s] = knt[:, s * seq:(s + 1) * seq]
    q_ref[...] = (qn * (HEAD_DIM ** -0.5 * LOG2_E)).astype(BF16)
    k_ref[...] = kn.astype(BF16)


def _in_part(tm, x2, mod, seq_len, norm_w, w_in, qw, kw, ones_q, ones_k, lb_logits, rope):
    t = x2.shape[0]
    latent = rope is not None
    per_seq = max(seq_len // tm, 1)
    row = lambda i: (i, 0)
    in_specs = [pl.BlockSpec((tm, D_MODEL), row),
                _mod_spec(mod, tm, seq_len),
                _const_spec((1, D_MODEL)),
                _const_spec((D_MODEL, D_IN)),
                _const_spec((1, Q_WIDTH)),
                _const_spec((1, KV_WIDTH)),
                _const_spec((Q_WIDTH, Q_WIDTH)),
                _const_spec((KV_WIDTH, KV_WIDTH)),
                _const_spec(lb_logits.shape)]
    args = [x2, mod.rows, norm_w, w_in, qw, kw, ones_q, ones_k, lb_logits]
    out_specs = [pl.BlockSpec((tm, ZH_WIDTH), row), pl.BlockSpec((tm, Q_WIDTH), row),
                 pl.BlockSpec((tm, KV_WIDTH), row)]
    out_shape = [jax.ShapeDtypeStruct((t, ZH_WIDTH), F32), jax.ShapeDtypeStruct((t, Q_WIDTH), BF16),
                 jax.ShapeDtypeStruct((t, KV_WIDTH), BF16)]
    if seq_len <= tm:
        assert tm % seq_len == 0
        t_spec = pl.BlockSpec((tm // seq_len, KV_WIDTH, seq_len), lambda i: (i, 0, 0))
    else:
        t_spec = pl.BlockSpec((1, KV_WIDTH, tm), lambda i: (i // per_seq, 0, i % per_seq))
    if latent:
        in_specs += [pl.BlockSpec((tm, Q_WIDTH), lambda i: (i % per_seq, 0))] * 3
        args += list(rope)
        out_specs.append(t_spec)
        out_shape.append(jax.ShapeDtypeStruct((t // seq_len, KV_WIDTH, seq_len), BF16))
    else:
        out_specs += [t_spec, t_spec]
        out_shape += [jax.ShapeDtypeStruct((t // seq_len, KV_WIDTH, seq_len), F32)] * 2
    out_specs.append(pl.BlockSpec((tm, 2 * D_MODEL), row))
    out_shape.append(jax.ShapeDtypeStruct((t, 2 * D_MODEL), BF16))
    cost = (_mm_cost(tm, D_MODEL, D_IN) + _mm_cost(tm, Q_WIDTH, Q_WIDTH) + _mm_cost(tm, KV_WIDTH, KV_WIDTH))
    scratch = [] if w_in.dtype == BF16 else [pltpu.VMEM((D_MODEL, D_IN), BF16)]
    return t // tm, _Part(functools.partial(_in_stages, latent), args, in_specs, out_specs, out_shape, scratch, cost)


def _attn_cost(lq, lks):
    per_head = sum(_mm_cost(lk, HEAD_DIM, lq) + _mm_cost(HEAD_DIM + ONES_ROWS, lk, lq) for lk in lks)
    return N_HEADS * per_head


def _attn_stages(k_transposed, lookahead, ins, outs, _):
    n_seg = len(k_transposed)
    q_ref, kv_refs = ins[0], ins[1:]
    (o_ref,) = outs
    q = q_ref[...]
    lq = q.shape[0]
    ks = [(kv_refs[2 * s][0].T if k_transposed[s] else kv_refs[2 * s][...]).astype(BF16) for s in range(n_seg)]
    vts = [kv_refs[2 * s + 1][0].astype(BF16) for s in range(n_seg)]
    vts = [[jnp.concatenate([vt[g * HEAD_DIM:(g + 1) * HEAD_DIM, :],
                             jnp.ones((ONES_ROWS, vt.shape[1]), BF16)], axis=0) for g in range(N_KV)]
           for vt in vts]
    score_cost = sum(_mm_cost(k.shape[0], HEAD_DIM, lq) for k in ks)
    value_cost = sum(_mm_cost(HEAD_DIM + ONES_ROWS, k.shape[0], lq) for k in ks)

    def scores(h):
        g = h // (N_HEADS // N_KV)
        qh = q[:, h * HEAD_DIM:(h + 1) * HEAD_DIM]
        return [_dot_nt(k[:, g * HEAD_DIM:(g + 1) * HEAD_DIM], qh) for k in ks]

    pending = []
    for h in range(lookahead):
        pending.append(scores(h))
        yield score_cost
    outs_t = []
    for h in range(N_HEADS):
        if h + lookahead < N_HEADS:
            pending.append(scores(h + lookahead))
            yield score_cost
        st = pending.pop(0)
        g = h // (N_HEADS // N_KV)
        m = functools.reduce(jnp.maximum, [jnp.max(s, axis=0, keepdims=True) for s in st])
        ot = functools.reduce(jnp.add, [_dot(vt[g], jnp.exp2(s - m).astype(BF16)) for vt, s in zip(vts, st)])
        outs_t.append(ot[0:HEAD_DIM, :] / ot[HEAD_DIM:HEAD_DIM + 1, :])
        if h == N_HEADS - 1:
            o_ref[...] = jnp.concatenate(outs_t, axis=0).T.astype(BF16)
        yield value_cost


def _attn_part(q, kv_segments, n_batch, seq_len, lookahead):
    t = q.shape[0]
    tq = ATTN_Q_TILE
    per_seq = seq_len // tq
    in_specs = [pl.BlockSpec((tq, Q_WIDTH), lambda i: (i, 0))]
    args = [q]
    for k, v_t, layer in kv_segments:
        lk = v_t.shape[-1]
        if v_t.ndim == 4:
            slab = pl.BlockSpec((1, None, KV_WIDTH, lk), lambda i, layer=layer: (i // per_seq, layer, 0, 0))
        else:
            slab = pl.BlockSpec((1, KV_WIDTH, lk), lambda i: (i // per_seq, 0, 0))
        k_spec = slab if k.ndim == v_t.ndim else pl.BlockSpec((lk, KV_WIDTH), lambda i: (i // per_seq, 0))
        in_specs += [k_spec, slab]
        args += [k, v_t]
    stages = functools.partial(_attn_stages, tuple(k.ndim == v_t.ndim for k, v_t, _ in kv_segments), lookahead)
    return n_batch * per_seq, _Part(stages, args, in_specs, [pl.BlockSpec((tq, Q_WIDTH), lambda i: (i, 0))],
                                    [jax.ShapeDtypeStruct((t, Q_WIDTH), BF16)], [],
                                    _attn_cost(tq, [v_t.shape[-1] for _, v_t, _ in kv_segments]))


def _hgrn_unit_cost():
    blk = HG_BLOCK
    return (2 * _mm_cost(blk, blk, 2 * HG_DK)
            + 2 * _mm_cost(blk, HG_DK, blk) + _mm_cost(blk, blk, HG_DV)
            + (blk // HG_CHUNK) * (_mm_cost(HG_DV, HG_CHUNK, 2 * HG_DK) + _mm_cost(HG_CHUNK, 2 * HG_DK, HG_DV)))


def _hgrn_stages(seq_len, hps, has_s0, has_sfin, ins, outs, scrs):
    tf_ref, tb_ref, q_ref, lf_ref, lb_ref, v_ref, g_ref, nw_ref = ins[:8]
    s0_ref = ins[8] if has_s0 else None
    o_ref = outs[0]
    sfin_ref = outs[1] if has_sfin else None
    kv_scr, ss_scr, qd_scr, oi_scr, dec_scr = scrs
    n_blk = seq_len // HG_BLOCK
    n_chunk = seq_len // HG_CHUNK
    per_blk = HG_BLOCK // HG_CHUNK

    def chunk_cumsum(t_ref, x):
        hi, lo = _split_bf16(x)
        r = _dot(t_ref[...], jnp.concatenate([hi, lo], axis=1))
        return r[:, 0:HG_DK] + r[:, HG_DK:2 * HG_DK]

    def chunk_edge_rows(x, first):
        off = 0 if first else HG_CHUNK - 1
        return [x[c * HG_CHUNK + off:c * HG_CHUNK + off + 1, :] for c in range(per_blk)]

    def spread_rows(rows):
        return jnp.concatenate([jnp.broadcast_to(r, (HG_CHUNK, HG_DK)) for r in rows], axis=0)

    units = [(blk, h) for blk in range(n_blk) for h in range(hps)]
    stage1 = []
    for blk, h in units:
        rows = slice(blk * HG_BLOCK, (blk + 1) * HG_BLOCK)
        cols = slice(h * HG_DK, (h + 1) * HG_DK)
        logf_f = lf_ref[rows, cols]
        logf_b = lb_ref[rows, cols]
        k_f = 1.0 - jnp.exp(logf_f)
        k_b = 1.0 - jnp.exp(logf_b)
        a_f = chunk_cumsum(tf_ref, logf_f)
        a_b = chunk_cumsum(tb_ref, logf_b)
        stage1.append((rows, cols, k_f, k_b, a_f, a_b))
        yield 2 * _mm_cost(HG_BLOCK, HG_BLOCK, 2 * HG_DK)
    stage2 = []
    for (blk, h), (rows, cols, k_f, k_b, a_f, a_b) in zip(units, stage1):
        q = q_ref[rows, cols]
        vb = v_ref[rows, cols].astype(BF16)
        ea_f = jnp.exp(a_f)
        ea_b = jnp.exp(a_b)
        qd_f = (q * ea_f).astype(BF16)
        qd_b = (q * ea_b).astype(BF16)
        kd_f = k_f * jnp.exp(-a_f)
        kd_b = k_b * jnp.exp(-a_b)
        dec_f = chunk_edge_rows(ea_f, first=False)
        dec_b = chunk_edge_rows(ea_b, first=True)
        ke = jnp.concatenate([kd_f * spread_rows(dec_f), kd_b * spread_rows(dec_b)], axis=1).astype(BF16)
        sc_f = _dot_nt(qd_f, kd_f.astype(BF16))
        sc_b = _dot_nt(qd_b, kd_b.astype(BF16))
        for c in range(per_blk):
            cr = slice(c * HG_CHUNK, (c + 1) * HG_CHUNK)
            kv_scr[h, blk * per_blk + c] = _dot_tn(vb[cr, :], ke[cr, :])
            dec_scr[h, blk * per_blk + c, 0:1, :] = jnp.concatenate([dec_f[c], dec_b[c]], axis=1)
        qd_scr[h, rows, :] = jnp.concatenate([qd_f, qd_b], axis=1)
        stage2.append((rows, cols, vb, sc_f, sc_b))
        yield 2 * _mm_cost(HG_BLOCK, HG_DK, HG_BLOCK) + per_blk * _mm_cost(HG_DV, HG_CHUNK, 2 * HG_DK)
    for rows, cols, vb, sc_f, sc_b in stage2:
        s = sc_f.astype(BF16) * tf_ref[...] + sc_b.astype(BF16) * tb_ref[...]
        oi_scr[rows, cols] = _dot(s, vb)
        yield _mm_cost(HG_BLOCK, HG_BLOCK, HG_DV)

    for h in range(hps):
        if has_s0:
            s_f, s_b = s0_ref[0, 0, h].T, s0_ref[0, 1, h].T
        else:
            s_f = s_b = jnp.zeros((HG_DV, HG_DK), F32)
        for cf in range(n_chunk):
            cb = n_chunk - 1 - cf
            ss_scr[h, cf, :, 0:HG_DK] = s_f.astype(BF16)
            s_f = dec_scr[h, cf, 0:1, 0:HG_DK] * s_f + kv_scr[h, cf, :, 0:HG_DK]
            ss_scr[h, cb, :, HG_DK:2 * HG_DK] = s_b.astype(BF16)
            s_b = dec_scr[h, cb, 0:1, HG_DK:2 * HG_DK] * s_b + kv_scr[h, cb, :, HG_DK:2 * HG_DK]
        if has_sfin:
            sfin_ref[0, 0, h] = s_f.T
            sfin_ref[0, 1, h] = s_b.T

    for blk, h in units:
        cols = slice(h * HG_DK, (h + 1) * HG_DK)
        for c in range(per_blk):
            ci = blk * per_blk + c
            rows = slice(ci * HG_CHUNK, (ci + 1) * HG_CHUNK)
            oi_scr[rows, cols] = oi_scr[rows, cols] + _dot_nt(qd_scr[h, rows, :], ss_scr[h, ci])
        yield per_blk * _mm_cost(HG_CHUNK, 2 * HG_DK, HG_DV)

    for h in range(hps):
        cols = slice(h * HG_DK, (h + 1) * HG_DK)
        o = _rms_rows(oi_scr[:, cols], nw_ref[...])
        o_ref[:, cols] = (o * g_ref[:, cols]).astype(BF16)


def _chunk_masks():
    r = np.arange(HG_BLOCK)
    same = (r[:, None] // HG_CHUNK) == (r[None, :] // HG_CHUNK)
    lower = same & (r[None, :] <= r[:, None])
    upper = same & (r[None, :] >= r[:, None])
    return (jnp.asarray(lower.astype(np.float32), dtype=BF16),
            jnp.asarray(upper.astype(np.float32), dtype=BF16))


def _hgrn_part(zh, norm_w, n_batch, seq_len, hps, s0, want_final):
    t = zh.shape[0]
    n_chunk = seq_len // HG_CHUNK
    width = hps * HG_DK
    n_hb = HG_HEADS // hps
    t_fwd, t_bwd = _chunk_masks()

    def seg_spec(seg):
        return pl.BlockSpec((seq_len, width), lambda i: (i // n_hb, seg * n_hb + i % n_hb))

    state_spec = pl.BlockSpec((1, 2, hps, HG_DK, HG_DV), lambda i: (i // n_hb, 0, i % n_hb, 0, 0))
    mask_spec = pl.BlockSpec((HG_BLOCK, HG_BLOCK), lambda i: (0, 0))
    in_specs = [mask_spec, mask_spec,
                seg_spec(0), seg_spec(1), seg_spec(2), seg_spec(3), seg_spec(4),
                pl.BlockSpec((1, HG_DV), lambda i: (0, 0))]
    args = [t_fwd, t_bwd, zh, zh, zh, zh, zh, norm_w]
    if s0 is not None:
        states, layer = s0
        in_specs.append(pl.BlockSpec((1, None, 2, hps, HG_DK, HG_DV),
                                     lambda i: (i // n_hb, layer, 0, i % n_hb, 0, 0)))
        args.append(states)
    out_specs = [pl.BlockSpec((seq_len, width), lambda i: (i // n_hb, i % n_hb))]
    out_shape = [jax.ShapeDtypeStruct((t, HG_WIDTH), BF16)]
    if want_final:
        out_specs.append(state_spec)
        out_shape.append(jax.ShapeDtypeStruct((n_batch, 2, HG_HEADS, HG_DK, HG_DV), F32))
    scratch = [pltpu.VMEM((hps, n_chunk, HG_DV, 2 * HG_DK), F32),
               pltpu.VMEM((hps, n_chunk, HG_DV, 2 * HG_DK), BF16),
               pltpu.VMEM((hps, seq_len, 2 * HG_DK), BF16),
               pltpu.VMEM((seq_len, width), F32),
               pltpu.VMEM((hps, n_chunk, SUBLANES, 2 * HG_DK), F32)]
    stages = functools.partial(_hgrn_stages, seq_len, hps, s0 is not None, want_final)
    cost = hps * (seq_len // HG_BLOCK) * _hgrn_unit_cost()
    return n_batch * n_hb, _Part(stages, args, in_specs, out_specs, out_shape, scratch, cost)


def _out_stages(ins, outs, _):
    x_ref, mod_ref, oh_ref, oa_ref, g_ref, who_ref, wao_ref, wout_ref, nfw_ref, wff1_ref, wff2_ref, fnw_ref = ins
    (y_ref,) = outs
    tm = x_ref.shape[0]
    m = mod_ref[0]
    g1 = m[:, 2 * D_MODEL:3 * D_MODEL]
    sh2 = m[:, 3 * D_MODEL:4 * D_MODEL]
    sc2 = m[:, 4 * D_MODEL:5 * D_MODEL]
    g2 = m[:, 5 * D_MODEL:6 * D_MODEL]
    gates = g_ref[...].astype(F32)
    from_h = yield from _dot_by_cols(oh_ref[...], who_ref, (0, HG_WIDTH), (0, D_MODEL))
    from_a = yield from _dot_by_cols(oa_ref[...], wao_ref, (0, Q_WIDTH), (0, D_MODEL))
    merged = gates[:, 0:D_MODEL] * from_h + gates[:, D_MODEL:2 * D_MODEL] * from_a
    mixed = yield from _dot_by_cols(merged.astype(BF16), wout_ref, (0, D_MODEL), (0, D_MODEL))
    x1 = x_ref[...] + g1 * mixed
    h2 = (_rms_rows(x1, nfw_ref[...]) * (1.0 + sc2) + sh2).astype(BF16)
    acc = jnp.zeros_like(x1)
    for j in range(D_FF // D_MODEL):
        cols = slice(j * D_MODEL, (j + 1) * D_MODEL)
        hj = yield from _dot_by_cols(h2, wff1_ref, (0, D_MODEL), (cols.start, cols.stop))
        hj = jnp.maximum(hj, 0.0)
        acc = acc + (yield from _dot_by_cols((hj * hj).astype(BF16), wff2_ref, (cols.start, cols.stop), (0, D_MODEL)))
    y_ref[...] = _rms_rows(x1 + g2 * acc, fnw_ref[...])


def _out_part(tm, x2, mod, seq_len, oh, oa, gates, w_ho, w_ao, w_out, nfw, w_ff1, w_ff2, fnw):
    t = x2.shape[0]
    row = lambda i: (i, 0)
    in_specs = [pl.BlockSpec((tm, D_MODEL), row),
                _mod_spec(mod, tm, seq_len),
                pl.BlockSpec((tm, HG_WIDTH), row),
                pl.BlockSpec((tm, Q_WIDTH), row),
                pl.BlockSpec((tm, 2 * D_MODEL), row),
                _const_spec((HG_WIDTH, D_MODEL)),
                _const_spec((Q_WIDTH, D_MODEL)),
                _const_spec((D_MODEL, D_MODEL)),
                _const_spec((1, D_MODEL)),
                _const_spec((D_MODEL, D_FF)),
                _const_spec((D_FF, D_MODEL)),
                _const_spec((1, D_MODEL))]
    args = [x2, mod.rows, oh, oa, gates, w_ho, w_ao, w_out, nfw, w_ff1, w_ff2, fnw]
    cost = _mm_cost(tm, HG_WIDTH + Q_WIDTH, D_MODEL) + _mm_cost(tm, D_MODEL, D_MODEL) + 2 * _mm_cost(tm, D_MODEL, D_FF)
    return t // tm, _Part(_out_stages, args, in_specs, [pl.BlockSpec((tm, D_MODEL), row)],
                          [jax.ShapeDtypeStruct((t, D_MODEL), F32)], [], cost)


def _cast_stages(ins, outs, _):
    for src, dst in zip(ins, outs):
        dst[...] = src[...].astype(dst.dtype)
    yield 1


def _cast_part(arrays, n_steps):
    in_specs, out_specs, out_shape = [], [], []
    for a in arrays:
        rows, cols = a.shape
        assert rows % (n_steps * MXU_ROWS) == 0
        spec = pl.BlockSpec((rows // n_steps, cols), lambda i: (i, 0))
        in_specs.append(spec)
        out_specs.append(spec)
        out_shape.append(jax.ShapeDtypeStruct(a.shape, BF16))
    return n_steps, _Part(_cast_stages, list(arrays), in_specs, out_specs, out_shape, [], 1)


def _rope_tables(n_tokens):
    rows = n_tokens // GRID_W
    row = np.repeat(np.arange(rows, dtype=np.float32), GRID_W)
    col = np.tile(np.arange(GRID_W, dtype=np.float32), rows)
    axis_dim = HEAD_DIM // 2
    freqs = (ROPE_THETA ** (-np.arange(0, axis_dim, 2, dtype=np.float32) / axis_dim)).astype(np.float32)
    ang_r = row[:, None] * freqs
    ang_c = col[:, None] * freqs
    cr, sr, cc, sc = np.cos(ang_r), np.sin(ang_r), np.cos(ang_c), np.sin(ang_c)
    zero = np.zeros_like(sr)
    cos = np.concatenate([cr, cr, cc, cc], axis=-1)
    s_up = np.concatenate([-sr, zero, -sc, zero], axis=-1)
    s_dn = np.concatenate([zero, sr, zero, sc], axis=-1)
    return tuple(jnp.asarray(np.tile(a, (1, N_HEADS)), dtype=F32) for a in (cos, s_up, s_dn))


def _to_slab(kv):
    n, depth, length = kv.shape[:3]
    return jnp.transpose(kv, (0, 1, 3, 4, 2)).reshape(n, depth, KV_WIDTH, length)


def _from_slab(slab):
    n, _, length = slab.shape
    return jnp.transpose(slab.reshape(n, N_KV, HEAD_DIM, length), (0, 3, 1, 2))


def _block_ones(width):
    idx = np.arange(width) // HEAD_DIM
    return jnp.asarray((idx[:, None] == idx[None, :]).astype(np.float32), dtype=BF16)


def _same_steps(*counted_parts):
    steps = {n for n, _ in counted_parts}
    assert len(steps) == 1, steps
    return steps.pop(), [p for _, p in counted_parts]


def kernel(x_prompt, x_sample, cache_k, cache_v, state_hgrn, c, c_ctx, w_ada, b_ada, norm_mix_w, w_in, q_norm_w, k_norm_w, hgrn_lb_logits, hgrn_norm_w, w_hgrn_out, w_attn_out, w_out, norm_ffn_w, w_ff1, w_ff2, final_norm_w):
    n_p, l_p, _ = x_prompt.shape
    n_s, l_s, _ = x_sample.shape
    layer = 0

    cond = jnp.concatenate([c_ctx[None, :], c, jnp.zeros((COND_ROWS - 1 - n_s, D_MODEL), F32)], axis=0)
    mod = _modulation(cond, w_ada[layer], b_ada[layer][None, :]).reshape(COND_ROWS, 1, N_MOD * D_MODEL)
    mod_p = _Mod(mod, 0, True)
    mod_s = _Mod(mod, 1, False)

    in_small = (jnp.tile(q_norm_w[layer], N_HEADS)[None, :], jnp.tile(k_norm_w[layer], N_KV)[None, :],
                _block_ones(Q_WIDTH), _block_ones(KV_WIDTH), hgrn_lb_logits)
    nmw = norm_mix_w[layer][None, :]
    hnw = hgrn_norm_w[layer][None, :]
    xp2 = x_prompt.reshape(n_p * l_p, D_MODEL)
    xs2 = x_sample.reshape(n_s * l_s, D_MODEL)

    n_in, in_p = _in_part(MIX_TOKEN_TILE, xp2, mod_p, l_p, nmw, w_in[layer], *in_small, None)
    steps, parts = _same_steps((n_in, in_p), _cast_part(
        [w_in[layer], w_hgrn_out[layer], w_attn_out[layer], w_out[layer], w_ff1[layer], w_ff2[layer]], n_in))
    (zh_p, q_p, k_p, kt_p, vt_p, gates_p), (w_in_b, w_ho_b, w_ao_b, w_out_b, w_ff1_b, w_ff2_b) = _launch(
        "in_ctx", steps, parts)
    out_w = (w_ho_b, w_ao_b, w_out_b, norm_ffn_w[layer][None, :], w_ff1_b, w_ff2_b, final_norm_w[None, :])

    steps, parts = _same_steps(
        _attn_part(q_p, [(k_p, vt_p, None)], n_p, l_p, N_HEADS),
        _in_part(MIX_TOKEN_TILE, xs2, mod_s, l_s, nmw, w_in_b, *in_small, _rope_tables(l_s)),
        _hgrn_part(zh_p, hnw, n_p, l_p, HG_HEADS, None, True))
    (oa_p,), (zh_s, q_s, k_s, vt_s, gates_s), (oh_p, s_fin) = _launch("in_latent_mix_ctx", steps, parts,
                                                                      MIX_PACES_IN)

    segs = [(_to_slab(cache_k), _to_slab(cache_v), layer), (k_s, vt_s, None)]
    steps, parts = _same_steps(
        _out_part(MIX_TOKEN_TILE, xp2, mod_p, l_p, oh_p, oa_p, gates_p, *out_w),
        _attn_part(q_s, segs, n_s, l_s, ATTN_LOOKAHEAD),
        _hgrn_part(zh_s, hnw, n_s, l_s, 1, (state_hgrn, layer), False))
    (y_p,), (oa_s,), (oh_s,) = _launch("out_ctx_mix_latent", steps, parts, MIX_PACES_OUT)

    steps, parts = _same_steps(_out_part(SOLO_TOKEN_TILE, xs2, mod_s, l_s, oh_s, oa_s, gates_s, *out_w))
    ((y_s,),) = _launch("out_latent", steps, parts)

    new_k = _from_slab(kt_p)[:, None]
    new_v = _from_slab(vt_p)[:, None]
    new_s = s_fin.reshape(n_p, 1, 2, HG_HEADS, HG_DK, HG_DV)
    return (y_p.reshape(n_p, l_p, D_MODEL), y_s.reshape(n_s, l_s, D_MODEL), new_k, new_v, new_s)
```

```python
import functools
from typing import Any, Callable, NamedTuple

import numpy as np
import jax
import jax.numpy as jnp
from jax import lax
from jax.experimental import pallas as pl
from jax.experimental.pallas import tpu as pltpu

F32 = jnp.float32
BF16 = jnp.bfloat16

D_MODEL = 1024
GRID_W = 64
EPS = 1e-6
HG_HEADS = 4
HG_DK = 128
HG_DV = 128
HG_WIDTH = HG_HEADS * HG_DK
HG_CHUNK = 32
HG_PAIR = 2 * HG_CHUNK
N_HEADS = 8
N_KV = 2
HEAD_DIM = 64
Q_WIDTH = N_HEADS * HEAD_DIM
KV_WIDTH = N_KV * HEAD_DIM
ROPE_THETA = 10000.0
D_FF = 4 * D_MODEL
N_MOD = 6
ZH_WIDTH = 5 * HG_WIDTH
D_IN = ZH_WIDTH + Q_WIDTH + 2 * KV_WIDTH + 2 * D_MODEL

V7X_VMEM_LIMIT_BYTES = 56 * 1024 * 1024
SUBLANES = 8
MXU_ROWS = 16
HG_BLOCK = 256
COND_ROWS = 16
ONES_ROWS = 16
LOG2_E = float(np.log2(np.e))
MIX_TOKEN_TILE = 256
SOLO_TOKEN_TILE = 512
ADALN_ROW_TILE = 256
W_CAST_ROWS = 128
DOT_COLS = 512
ATTN_Q_TILE = 256
ATTN_LOOKAHEAD = 2
MIX_PACES_IN = (1.0, 1.0, 0.8)
MIX_PACES_OUT = (0.85, 1.0, 1.0)


def _dot(a, b):
    return jnp.dot(a, b, preferred_element_type=F32)


def _dot_nt(a, b):
    return lax.dot_general(a, b, (((1,), (1,)), ((), ())), preferred_element_type=F32)


def _dot_tn(a, b):
    return lax.dot_general(a, b, (((0,), (0,)), ((), ())), preferred_element_type=F32)


def _mm_cost(m, k, n):
    return (m // MXU_ROWS) * -(-k // 256) * -(-n // 256)


def _dot_by_cols(a, w_ref, rows, cols):
    pieces = []
    for c in range(cols[0], cols[1], DOT_COLS):
        end = min(c + DOT_COLS, cols[1])
        pieces.append(_dot(a, w_ref[rows[0]:rows[1], c:end]))
        yield _mm_cost(a.shape[0], rows[1] - rows[0], end - c)
    return pieces[0] if len(pieces) == 1 else jnp.concatenate(pieces, axis=1)


def _split_bf16(x):
    hi = x.astype(BF16)
    lo = (x - hi.astype(F32)).astype(BF16)
    return hi, lo


def _const_spec(shape):
    nd = len(shape)
    return pl.BlockSpec(shape, lambda *_: (0,) * nd, pipeline_mode=pl.Buffered(1))


class _Part(NamedTuple):
    stages: Callable[..., Any]
    args: list
    in_specs: list
    out_specs: list
    out_shape: list
    scratch: list
    cost: int


def _interleave(gens, totals):
    done = [0] * len(gens)
    live = list(range(len(gens)))
    while live:
        i = min(live, key=lambda j: done[j] / totals[j])
        try:
            done[i] += next(gens[i])
        except StopIteration:
            live.remove(i)


def _launch(name, n_steps, parts, paces=None):
    paces = paces or [1.0] * len(parts)
    n_in = [len(p.args) for p in parts]
    n_out = [len(p.out_shape) for p in parts]
    n_scr = [len(p.scratch) for p in parts]

    def body(*refs):
        refs = list(refs)
        ins = [[refs.pop(0) for _ in range(n)] for n in n_in]
        outs = [[refs.pop(0) for _ in range(n)] for n in n_out]
        scrs = [[refs.pop(0) for _ in range(n)] for n in n_scr]
        _interleave([p.stages(i, o, s) for p, i, o, s in zip(parts, ins, outs, scrs)],
                    [p.cost * pace for p, pace in zip(parts, paces)])

    flat = pl.pallas_call(
        body,
        grid=(n_steps,),
        in_specs=[s for p in parts for s in p.in_specs],
        out_specs=[s for p in parts for s in p.out_specs],
        out_shape=[s for p in parts for s in p.out_shape],
        scratch_shapes=[s for p in parts for s in p.scratch],
        compiler_params=pltpu.CompilerParams(dimension_semantics=("arbitrary",),
                                             vmem_limit_bytes=V7X_VMEM_LIMIT_BYTES),
        name=name,
    )(*[a for p in parts for a in p.args])
    flat = list(flat)
    return [[flat.pop(0) for _ in range(n)] for n in n_out]


class _Mod(NamedTuple):
    rows: Any
    first: int
    shared: bool


def _mod_spec(mod, tile, seq_len, tile_of_step=lambda i: i):
    if mod.shared:
        return pl.BlockSpec((1, 1, N_MOD * D_MODEL), lambda i: (mod.first, 0, 0))
    assert seq_len % tile == 0
    per_seq = seq_len // tile
    return pl.BlockSpec((1, 1, N_MOD * D_MODEL), lambda i: (mod.first + tile_of_step(i) // per_seq, 0, 0))


def _mod_kernel(cond_ref, w_ref, b_ref, o_ref):
    c = cond_ref[...]
    x = c * jax.nn.sigmoid(c)
    x_hi, x_lo = _split_bf16(x)
    w = w_ref[...].astype(BF16)
    part = _dot(x_hi, w) + _dot(x_lo, w)

    @pl.when(pl.program_id(0) == 0)
    def _():
        o_ref[...] = b_ref[...] + part

    @pl.when(pl.program_id(0) > 0)
    def _():
        o_ref[...] += part


def _modulation(cond, w_ada, b_ada):
    n = N_MOD * D_MODEL
    tk = ADALN_ROW_TILE
    return pl.pallas_call(
        _mod_kernel,
        grid=(D_MODEL // tk,),
        in_specs=[pl.BlockSpec((COND_ROWS, tk), lambda j: (0, j)),
                  pl.BlockSpec((tk, n), lambda j: (j, 0)),
                  pl.BlockSpec((1, n), lambda j: (0, 0))],
        out_specs=pl.BlockSpec((COND_ROWS, n), lambda j: (0, 0)),
        out_shape=jax.ShapeDtypeStruct((COND_ROWS, n), F32),
        compiler_params=pltpu.CompilerParams(dimension_semantics=("arbitrary",),
                                             vmem_limit_bytes=V7X_VMEM_LIMIT_BYTES),
        name="adaln_mod",
    )(cond, w_ada, b_ada)


def _rms_rows(x, w):
    return x * lax.rsqrt(jnp.mean(x * x, axis=-1, keepdims=True) + EPS) * w


def _head_sumsq(a, ones_ref):
    return _dot((a * a).astype(BF16), ones_ref[...])


def _head_rms(a, sumsq, w):
    return a * lax.rsqrt(sumsq * (1.0 / HEAD_DIM) + EPS) * w


def _forget_lower_bound(logits):
    e = jnp.exp(logits - jnp.max(logits, axis=1, keepdims=True))
    return e[:, 0, :] / jnp.sum(e, axis=1)


def _rope(x, cos, s_up, s_dn):
    cols = []
    for j in range(x.shape[1] // 128):
        sl = slice(j * 128, (j + 1) * 128)
        xj = x[:, sl]
        cols.append(xj * cos[:, sl] + pltpu.roll(xj, 112, 1) * s_up[:, sl]
                    + pltpu.roll(xj, 16, 1) * s_dn[:, sl])
    return jnp.concatenate(cols, axis=1) if len(cols) > 1 else cols[0]


def _in_stages(latent, ins, outs, scrs):
    x_ref, mod_ref, nw_ref, w_ref, qw_ref, kw_ref, oq_ref, ok_ref, lbl_ref = ins[:9]
    if scrs:
        (w_bf_ref,) = scrs
        rows = W_CAST_ROWS

        @pl.when(pl.program_id(0) == 0)
        def _():
            def cast_rows(r, carry):
                sl = pl.ds(pl.multiple_of(r * rows, rows), rows)
                w_bf_ref[sl, :] = w_ref[sl, :].astype(BF16)
                return carry

            lax.fori_loop(0, w_ref.shape[0] // rows, cast_rows, 0)

        w_ref = w_bf_ref
    if latent:
        cos_ref, sup_ref, sdn_ref = ins[9:]
        zh_ref, q_ref, k_ref, vt_ref, g_ref = outs
    else:
        zh_ref, q_ref, k_ref, kt_ref, vt_ref, g_ref = outs
    tm = x_ref.shape[0]
    m = mod_ref[0]
    sh1 = m[:, 0:D_MODEL]
    sc1 = m[:, D_MODEL:2 * D_MODEL]
    h = _rms_rows(x_ref[...], nw_ref[...]) * (1.0 + sc1) + sh1
    hb = h.astype(BF16)
    c0 = ZH_WIDTH
    c1 = c0 + Q_WIDTH
    c2 = c1 + KV_WIDTH
    c3 = c2 + KV_WIDTH
    aq = _dot(hb, w_ref[:, c0:c1])
    ak = _dot(hb, w_ref[:, c1:c2])
    av = _dot(hb, w_ref[:, c2:c3])
    yield _mm_cost(tm, D_MODEL, c3 - c0)
    gl = yield from _dot_by_cols(hb, w_ref, (0, D_MODEL), (c3, D_IN))
    ssq = _head_sumsq(aq, oq_ref)
    ssk = _head_sumsq(ak, ok_ref)
    yield _mm_cost(tm, Q_WIDTH, Q_WIDTH) + _mm_cost(tm, KV_WIDTH, KV_WIDTH)
    lb = _forget_lower_bound(lbl_ref[...])

    def hgrn_segment(s):
        z = yield from _dot_by_cols(hb, w_ref, (0, D_MODEL), (s * HG_WIDTH, (s + 1) * HG_WIDTH))
        if s == 0:
            z = z * HG_DK ** -0.5
        elif s in (1, 2):
            lbv = lb[s - 1:s, :]
            z = jnp.log(lbv + (1.0 - lbv) * jax.nn.sigmoid(z))
        elif s == 4:
            z = z * jax.nn.sigmoid(z)
        zh_ref[:, s * HG_WIDTH:(s + 1) * HG_WIDTH] = z

    yield from hgrn_segment(1)
    yield from hgrn_segment(2)
    g_ref[...] = jax.nn.sigmoid(gl).astype(BF16)
    qn = _head_rms(aq, ssq, qw_ref[...])
    kn = _head_rms(ak, ssk, kw_ref[...])
    seq = vt_ref.shape[2]
    avt = av.T
    for s in range(vt_ref.shape[0]):
        vt_ref[s] = avt[:, s * seq:(s + 1) * seq].astype(vt_ref.dtype)
    for s in (4, 0, 3):
        yield from hgrn_segment(s)
    if latent:
        cos, sup, sdn = cos_ref[...], sup_ref[...], sdn_ref[...]
        qn = _rope(qn, cos, sup, sdn)
        kn = _rope(kn, cos[:, 0:KV_WIDTH], sup[:, 0:KV_WIDTH], sdn[:, 0:KV_WIDTH])
    else:
        knt = kn.T
        for s in range(kt_ref.shape[0]):
            kt_ref[s] = knt[:, s * seq:(s + 1) * seq]
    q_ref[...] = (qn * (HEAD_DIM ** -0.5 * LOG2_E)).astype(BF16)
    k_ref[...] = kn.astype(BF16)


def _in_part(tm, x2, mod, seq_len, norm_w, w_in, qw, kw, ones_q, ones_k, lb_logits, rope):
    t = x2.shape[0]
    latent = rope is not None
    per_seq = max(seq_len // tm, 1)
    row = lambda i: (i, 0)
    in_specs = [pl.BlockSpec((tm, D_MODEL), row),
                _mod_spec(mod, tm, seq_len),
                _const_spec((1, D_MODEL)),
                _const_spec((D_MODEL, D_IN)),
                _const_spec((1, Q_WIDTH)),
                _const_spec((1, KV_WIDTH)),
                _const_spec((Q_WIDTH, Q_WIDTH)),
                _const_spec((KV_WIDTH, KV_WIDTH)),
                _const_spec(lb_logits.shape)]
    args = [x2, mod.rows, norm_w, w_in, qw, kw, ones_q, ones_k, lb_logits]
    out_specs = [pl.BlockSpec((tm, ZH_WIDTH), row), pl.BlockSpec((tm, Q_WIDTH), row),
                 pl.BlockSpec((tm, KV_WIDTH), row)]
    out_shape = [jax.ShapeDtypeStruct((t, ZH_WIDTH), F32), jax.ShapeDtypeStruct((t, Q_WIDTH), BF16),
                 jax.ShapeDtypeStruct((t, KV_WIDTH), BF16)]
    if seq_len <= tm:
        assert tm % seq_len == 0
        t_spec = pl.BlockSpec((tm // seq_len, KV_WIDTH, seq_len), lambda i: (i, 0, 0))
    else:
        t_spec = pl.BlockSpec((1, KV_WIDTH, tm), lambda i: (i // per_seq, 0, i % per_seq))
    if latent:
        in_specs += [pl.BlockSpec((tm, Q_WIDTH), lambda i: (i % per_seq, 0))] * 3
        args += list(rope)
        out_specs.append(t_spec)
        out_shape.append(jax.ShapeDtypeStruct((t // seq_len, KV_WIDTH, seq_len), BF16))
    else:
        out_specs += [t_spec, t_spec]
        out_shape += [jax.ShapeDtypeStruct((t // seq_len, KV_WIDTH, seq_len), F32)] * 2
    out_specs.append(pl.BlockSpec((tm, 2 * D_MODEL), row))
    out_shape.append(jax.ShapeDtypeStruct((t, 2 * D_MODEL), BF16))
    cost = (_mm_cost(tm, D_MODEL, D_IN) + _mm_cost(tm, Q_WIDTH, Q_WIDTH) + _mm_cost(tm, KV_WIDTH, KV_WIDTH))
    scratch = [] if w_in.dtype == BF16 else [pltpu.VMEM((D_MODEL, D_IN), BF16)]
    return t // tm, _Part(functools.partial(_in_stages, latent), args, in_specs, out_specs, out_shape, scratch, cost)


def _attn_cost(lq, lks):
    per_head = sum(_mm_cost(lk, HEAD_DIM, lq) + _mm_cost(HEAD_DIM + ONES_ROWS, lk, lq) for lk in lks)
    return N_HEADS * per_head


def _attn_stages(k_transposed, lookahead, ins, outs, _):
    n_seg = len(k_transposed)
    q_ref, kv_refs = ins[0], ins[1:]
    (o_ref,) = outs
    q = q_ref[...]
    lq = q.shape[0]
    ks = [(kv_refs[2 * s][0].T if k_transposed[s] else kv_refs[2 * s][...]).astype(BF16) for s in range(n_seg)]
    vts = [kv_refs[2 * s + 1][0].astype(BF16) for s in range(n_seg)]
    vts = [[jnp.concatenate([vt[g * HEAD_DIM:(g + 1) * HEAD_DIM, :],
                             jnp.ones((ONES_ROWS, vt.shape[1]), BF16)], axis=0) for g in range(N_KV)]
           for vt in vts]
    score_cost = sum(_mm_cost(k.shape[0], HEAD_DIM, lq) for k in ks)
    value_cost = sum(_mm_cost(HEAD_DIM + ONES_ROWS, k.shape[0], lq) for k in ks)

    def scores(h):
        g = h // (N_HEADS // N_KV)
        qh = q[:, h * HEAD_DIM:(h + 1) * HEAD_DIM]
        return [_dot_nt(k[:, g * HEAD_DIM:(g + 1) * HEAD_DIM], qh) for k in ks]

    pending = []
    for h in range(lookahead):
        pending.append(scores(h))
        yield score_cost
    outs_t = []
    for h in range(N_HEADS):
        if h + lookahead < N_HEADS:
            pending.append(scores(h + lookahead))
            yield score_cost
        st = pending.pop(0)
        g = h // (N_HEADS // N_KV)
        m = functools.reduce(jnp.maximum, [jnp.max(s, axis=0, keepdims=True) for s in st])
        ot = functools.reduce(jnp.add, [_dot(vt[g], jnp.exp2(s - m).astype(BF16)) for vt, s in zip(vts, st)])
        outs_t.append(ot[0:HEAD_DIM, :] / ot[HEAD_DIM:HEAD_DIM + 1, :])
        if h == N_HEADS - 1:
            o_ref[...] = jnp.concatenate(outs_t, axis=0).T.astype(BF16)
        yield value_cost


def _attn_part(q, kv_segments, n_batch, seq_len, lookahead):
    t = q.shape[0]
    tq = ATTN_Q_TILE
    per_seq = seq_len // tq
    in_specs = [pl.BlockSpec((tq, Q_WIDTH), lambda i: (i, 0))]
    args = [q]
    for k, v_t, layer in kv_segments:
        lk = v_t.shape[-1]
        if v_t.ndim == 4:
            slab = pl.BlockSpec((1, None, KV_WIDTH, lk), lambda i, layer=layer: (i // per_seq, layer, 0, 0))
        else:
            slab = pl.BlockSpec((1, KV_WIDTH, lk), lambda i: (i // per_seq, 0, 0))
        k_spec = slab if k.ndim == v_t.ndim else pl.BlockSpec((lk, KV_WIDTH), lambda i: (i // per_seq, 0))
        in_specs += [k_spec, slab]
        args += [k, v_t]
    stages = functools.partial(_attn_stages, tuple(k.ndim == v_t.ndim for k, v_t, _ in kv_segments), lookahead)
    return n_batch * per_seq, _Part(stages, args, in_specs, [pl.BlockSpec((tq, Q_WIDTH), lambda i: (i, 0))],
                                    [jax.ShapeDtypeStruct((t, Q_WIDTH), BF16)], [],
                                    _attn_cost(tq, [v_t.shape[-1] for _, v_t, _ in kv_segments]))


def _hgrn_unit_cost():
    blk = HG_BLOCK
    return (2 * _mm_cost(blk, blk, 2 * HG_DK)
            + 2 * _mm_cost(blk, HG_DK, 2 * blk) + _mm_cost(blk, blk, HG_DV)
            + (blk // HG_PAIR) * (_mm_cost(HG_DV, HG_PAIR, 2 * HG_DK) + _mm_cost(HG_PAIR, 2 * HG_DK, HG_DV)))


def _hgrn_stages(seq_len, hps, has_s0, has_sfin, ins, outs, scrs):
    tf_ref, tb_ref, xf_ref, xb_ref, q_ref, lf_ref, lb_ref, v_ref, g_ref, nw_ref = ins[:10]
    s0_ref = ins[10] if has_s0 else None
    o_ref = outs[0]
    sfin_ref = outs[1] if has_sfin else None
    kv_scr, ss_scr, qd_scr, oi_scr, dec_scr = scrs
    n_blk = seq_len // HG_BLOCK
    n_pair = seq_len // HG_PAIR
    per_blk = HG_BLOCK // HG_CHUNK
    pairs_per_blk = HG_BLOCK // HG_PAIR
    one_row = jnp.ones((1, HG_DK), F32)

    def chunk_cumsum(t_ref, x):
        hi, lo = _split_bf16(x)
        r = _dot(t_ref[...], jnp.concatenate([hi, lo], axis=1))
        return r[:, 0:HG_DK] + r[:, HG_DK:2 * HG_DK]

    def chunk_edge_rows(x, first):
        off = 0 if first else HG_CHUNK - 1
        return [x[c * HG_CHUNK + off:c * HG_CHUNK + off + 1, :] for c in range(per_blk)]

    def spread_rows(rows):
        return jnp.concatenate([jnp.broadcast_to(r, (HG_CHUNK, HG_DK)) for r in rows], axis=0)

    units = [(blk, h) for blk in range(n_blk) for h in range(hps)]
    stage1 = []
    for blk, h in units:
        rows = slice(blk * HG_BLOCK, (blk + 1) * HG_BLOCK)
        cols = slice(h * HG_DK, (h + 1) * HG_DK)
        logf_f = lf_ref[rows, cols]
        logf_b = lb_ref[rows, cols]
        k_f = 1.0 - jnp.exp(logf_f)
        k_b = 1.0 - jnp.exp(logf_b)
        a_f = chunk_cumsum(tf_ref, logf_f)
        a_b = chunk_cumsum(tb_ref, logf_b)
        stage1.append((rows, cols, k_f, k_b, a_f, a_b))
        yield 2 * _mm_cost(HG_BLOCK, HG_BLOCK, 2 * HG_DK)
    stage2 = []
    for (blk, h), (rows, cols, k_f, k_b, a_f, a_b) in zip(units, stage1):
        q = q_ref[rows, cols]
        vb = v_ref[rows, cols].astype(BF16)
        ea_f = jnp.exp(a_f)
        ea_b = jnp.exp(a_b)
        qd_f = q * ea_f
        qd_b = q * ea_b
        kd_f = k_f * jnp.exp(-a_f)
        kd_b = k_b * jnp.exp(-a_b)
        dec_f = chunk_edge_rows(ea_f, first=False)
        dec_b = chunk_edge_rows(ea_b, first=True)
        ke_f = kd_f * spread_rows(dec_f)
        ke_b = kd_b * spread_rows(dec_b)
        even = [c % 2 == 0 for c in range(per_blk)]
        qd_pair = jnp.concatenate(
            [qd_f * spread_rows([one_row if even[c] else dec_f[c - 1] for c in range(per_blk)]),
             qd_b * spread_rows([dec_b[c + 1] if even[c] else one_row for c in range(per_blk)])], axis=1)
        ke_pair = jnp.concatenate(
            [ke_f * spread_rows([dec_f[c + 1] if even[c] else one_row for c in range(per_blk)]),
             ke_b * spread_rows([one_row if even[c] else dec_b[c - 1] for c in range(per_blk)])], axis=1).astype(BF16)
        sc_f = _dot_nt(qd_f.astype(BF16), jnp.concatenate([kd_f, ke_f], axis=0).astype(BF16))
        sc_b = _dot_nt(qd_b.astype(BF16), jnp.concatenate([kd_b, ke_b], axis=0).astype(BF16))
        for p in range(pairs_per_blk):
            pr = slice(p * HG_PAIR, (p + 1) * HG_PAIR)
            kv_scr[h, blk * pairs_per_blk + p] = _dot_tn(vb[pr, :], ke_pair[pr, :])
            dec_scr[h, blk * pairs_per_blk + p, 0:1, :] = jnp.concatenate(
                [dec_f[2 * p] * dec_f[2 * p + 1], dec_b[2 * p] * dec_b[2 * p + 1]], axis=1)
        qd_scr[h, rows, :] = qd_pair.astype(BF16)
        stage2.append((rows, cols, vb, sc_f, sc_b))
        yield 2 * _mm_cost(HG_BLOCK, HG_DK, 2 * HG_BLOCK) + pairs_per_blk * _mm_cost(HG_DV, HG_PAIR, 2 * HG_DK)
    for rows, cols, vb, sc_f, sc_b in stage2:
        s = (sc_f[:, 0:HG_BLOCK].astype(BF16) * tf_ref[...] + sc_f[:, HG_BLOCK:].astype(BF16) * xf_ref[...]
             + sc_b[:, 0:HG_BLOCK].astype(BF16) * tb_ref[...] + sc_b[:, HG_BLOCK:].astype(BF16) * xb_ref[...])
        oi_scr[rows, cols] = _dot(s, vb)
        yield _mm_cost(HG_BLOCK, HG_BLOCK, HG_DV)

    for h in range(hps):
        if has_s0:
            s_f, s_b = s0_ref[0, 0, h].T, s0_ref[0, 1, h].T
        else:
            s_f = s_b = jnp.zeros((HG_DV, HG_DK), F32)
        for pf in range(n_pair):
            pb = n_pair - 1 - pf
            ss_scr[h, pf, :, 0:HG_DK] = s_f.astype(BF16)
            s_f = dec_scr[h, pf, 0:1, 0:HG_DK] * s_f + kv_scr[h, pf, :, 0:HG_DK]
            ss_scr[h, pb, :, HG_DK:2 * HG_DK] = s_b.astype(BF16)
            s_b = dec_scr[h, pb, 0:1, HG_DK:2 * HG_DK] * s_b + kv_scr[h, pb, :, HG_DK:2 * HG_DK]
        if has_sfin:
            sfin_ref[0, 0, h] = s_f.T
            sfin_ref[0, 1, h] = s_b.T

    for blk, h in units:
        cols = slice(h * HG_DK, (h + 1) * HG_DK)
        for p in range(pairs_per_blk):
            pi = blk * pairs_per_blk + p
            rows = slice(pi * HG_PAIR, (pi + 1) * HG_PAIR)
            oi_scr[rows, cols] = oi_scr[rows, cols] + _dot_nt(qd_scr[h, rows, :], ss_scr[h, pi])
        yield pairs_per_blk * _mm_cost(HG_PAIR, 2 * HG_DK, HG_DV)

    for h in range(hps):
        cols = slice(h * HG_DK, (h + 1) * HG_DK)
        o = _rms_rows(oi_scr[:, cols], nw_ref[...])
        o_ref[:, cols] = (o * g_ref[:, cols]).astype(BF16)


def _chunk_masks():
    r = np.arange(HG_BLOCK)
    same = (r[:, None] // HG_CHUNK) == (r[None, :] // HG_CHUNK)
    lower = same & (r[None, :] <= r[:, None])
    upper = same & (r[None, :] >= r[:, None])
    same_pair = (r[:, None] // HG_PAIR) == (r[None, :] // HG_PAIR)
    cross_f = same_pair & (r[:, None] // HG_CHUNK == r[None, :] // HG_CHUNK + 1)
    return tuple(jnp.asarray(m.astype(np.float32), dtype=BF16) for m in (lower, upper, cross_f, cross_f.T))


def _hgrn_part(zh, norm_w, n_batch, seq_len, hps, s0, want_final):
    t = zh.shape[0]
    n_pair = seq_len // HG_PAIR
    width = hps * HG_DK
    n_hb = HG_HEADS // hps
    masks = _chunk_masks()

    def seg_spec(seg):
        return pl.BlockSpec((seq_len, width), lambda i: (i // n_hb, seg * n_hb + i % n_hb))

    state_spec = pl.BlockSpec((1, 2, hps, HG_DK, HG_DV), lambda i: (i // n_hb, 0, i % n_hb, 0, 0))
    mask_spec = pl.BlockSpec((HG_BLOCK, HG_BLOCK), lambda i: (0, 0))
    in_specs = [mask_spec] * len(masks) + [seg_spec(s) for s in range(5)] + [pl.BlockSpec((1, HG_DV), lambda i: (0, 0))]
    args = list(masks) + [zh] * 5 + [norm_w]
    if s0 is not None:
        states, layer = s0
        in_specs.append(pl.BlockSpec((1, None, 2, hps, HG_DK, HG_DV),
                                     lambda i: (i // n_hb, layer, 0, i % n_hb, 0, 0)))
        args.append(states)
    out_specs = [pl.BlockSpec((seq_len, width), lambda i: (i // n_hb, i % n_hb))]
    out_shape = [jax.ShapeDtypeStruct((t, HG_WIDTH), BF16)]
    if want_final:
        out_specs.append(state_spec)
        out_shape.append(jax.ShapeDtypeStruct((n_batch, 2, HG_HEADS, HG_DK, HG_DV), F32))
    scratch = [pltpu.VMEM((hps, n_pair, HG_DV, 2 * HG_DK), F32),
               pltpu.VMEM((hps, n_pair, HG_DV, 2 * HG_DK), BF16),
               pltpu.VMEM((hps, seq_len, 2 * HG_DK), BF16),
               pltpu.VMEM((seq_len, width), F32),
               pltpu.VMEM((hps, n_pair, SUBLANES, 2 * HG_DK), F32)]
    stages = functools.partial(_hgrn_stages, seq_len, hps, s0 is not None, want_final)
    cost = hps * (seq_len // HG_BLOCK) * _hgrn_unit_cost()
    return n_batch * n_hb, _Part(stages, args, in_specs, out_specs, out_shape, scratch, cost)


def _out_stages(ins, outs, _):
    x_ref, mod_ref, oh_ref, oa_ref, g_ref, who_ref, wao_ref, wout_ref, nfw_ref, wff1_ref, wff2_ref, fnw_ref = ins
    (y_ref,) = outs
    tm = x_ref.shape[0]
    m = mod_ref[0]
    g1 = m[:, 2 * D_MODEL:3 * D_MODEL]
    sh2 = m[:, 3 * D_MODEL:4 * D_MODEL]
    sc2 = m[:, 4 * D_MODEL:5 * D_MODEL]
    g2 = m[:, 5 * D_MODEL:6 * D_MODEL]
    gates = g_ref[...].astype(F32)
    from_h = yield from _dot_by_cols(oh_ref[...], who_ref, (0, HG_WIDTH), (0, D_MODEL))
    from_a = yield from _dot_by_cols(oa_ref[...], wao_ref, (0, Q_WIDTH), (0, D_MODEL))
    merged = gates[:, 0:D_MODEL] * from_h + gates[:, D_MODEL:2 * D_MODEL] * from_a
    mixed = yield from _dot_by_cols(merged.astype(BF16), wout_ref, (0, D_MODEL), (0, D_MODEL))
    x1 = x_ref[...] + g1 * mixed
    h2 = (_rms_rows(x1, nfw_ref[...]) * (1.0 + sc2) + sh2).astype(BF16)
    acc = jnp.zeros_like(x1)
    for j in range(D_FF // D_MODEL):
        cols = slice(j * D_MODEL, (j + 1) * D_MODEL)
        hj = yield from _dot_by_cols(h2, wff1_ref, (0, D_MODEL), (cols.start, cols.stop))
        hj = jnp.maximum(hj, 0.0)
        acc = acc + (yield from _dot_by_cols((hj * hj).astype(BF16), wff2_ref, (cols.start, cols.stop), (0, D_MODEL)))
    y_ref[...] = _rms_rows(x1 + g2 * acc, fnw_ref[...])


def _out_part(tm, x2, mod, seq_len, oh, oa, gates, w_ho, w_ao, w_out, nfw, w_ff1, w_ff2, fnw):
    t = x2.shape[0]
    row = lambda i: (i, 0)
    in_specs = [pl.BlockSpec((tm, D_MODEL), row),
                _mod_spec(mod, tm, seq_len),
                pl.BlockSpec((tm, HG_WIDTH), row),
                pl.BlockSpec((tm, Q_WIDTH), row),
                pl.BlockSpec((tm, 2 * D_MODEL), row),
                _const_spec((HG_WIDTH, D_MODEL)),
                _const_spec((Q_WIDTH, D_MODEL)),
                _const_spec((D_MODEL, D_MODEL)),
                _const_spec((1, D_MODEL)),
                _const_spec((D_MODEL, D_FF)),
                _const_spec((D_FF, D_MODEL)),
                _const_spec((1, D_MODEL))]
    args = [x2, mod.rows, oh, oa, gates, w_ho, w_ao, w_out, nfw, w_ff1, w_ff2, fnw]
    cost = _mm_cost(tm, HG_WIDTH + Q_WIDTH, D_MODEL) + _mm_cost(tm, D_MODEL, D_MODEL) + 2 * _mm_cost(tm, D_MODEL, D_FF)
    return t // tm, _Part(_out_stages, args, in_specs, [pl.BlockSpec((tm, D_MODEL), row)],
                          [jax.ShapeDtypeStruct((t, D_MODEL), F32)], [], cost)


def _cast_stages(ins, outs, _):
    for src, dst in zip(ins, outs):
        dst[...] = src[...].astype(dst.dtype)
    yield 1


def _cast_part(arrays, n_steps):
    in_specs, out_specs, out_shape = [], [], []
    for a in arrays:
        rows, cols = a.shape
        assert rows % (n_steps * MXU_ROWS) == 0
        spec = pl.BlockSpec((rows // n_steps, cols), lambda i: (i, 0))
        in_specs.append(spec)
        out_specs.append(spec)
        out_shape.append(jax.ShapeDtypeStruct(a.shape, BF16))
    return n_steps, _Part(_cast_stages, list(arrays), in_specs, out_specs, out_shape, [], 1)


def _rope_tables(n_tokens):
    rows = n_tokens // GRID_W
    row = np.repeat(np.arange(rows, dtype=np.float32), GRID_W)
    col = np.tile(np.arange(GRID_W, dtype=np.float32), rows)
    axis_dim = HEAD_DIM // 2
    freqs = (ROPE_THETA ** (-np.arange(0, axis_dim, 2, dtype=np.float32) / axis_dim)).astype(np.float32)
    ang_r = row[:, None] * freqs
    ang_c = col[:, None] * freqs
    cr, sr, cc, sc = np.cos(ang_r), np.sin(ang_r), np.cos(ang_c), np.sin(ang_c)
    zero = np.zeros_like(sr)
    cos = np.concatenate([cr, cr, cc, cc], axis=-1)
    s_up = np.concatenate([-sr, zero, -sc, zero], axis=-1)
    s_dn = np.concatenate([zero, sr, zero, sc], axis=-1)
    return tuple(jnp.asarray(np.tile(a, (1, N_HEADS)), dtype=F32) for a in (cos, s_up, s_dn))


def _to_slab(kv):
    n, depth, length = kv.shape[:3]
    return jnp.transpose(kv, (0, 1, 3, 4, 2)).reshape(n, depth, KV_WIDTH, length)


def _from_slab(slab):
    n, _, length = slab.shape
    return jnp.transpose(slab.reshape(n, N_KV, HEAD_DIM, length), (0, 3, 1, 2))


def _block_ones(width):
    idx = np.arange(width) // HEAD_DIM
    return jnp.asarray((idx[:, None] == idx[None, :]).astype(np.float32), dtype=BF16)


def _same_steps(*counted_parts):
    steps = {n for n, _ in counted_parts}
    assert len(steps) == 1, steps
    return steps.pop(), [p for _, p in counted_parts]


def kernel(x_prompt, x_sample, cache_k, cache_v, state_hgrn, c, c_ctx, w_ada, b_ada, norm_mix_w, w_in, q_norm_w, k_norm_w, hgrn_lb_logits, hgrn_norm_w, w_hgrn_out, w_attn_out, w_out, norm_ffn_w, w_ff1, w_ff2, final_norm_w):
    n_p, l_p, _ = x_prompt.shape
    n_s, l_s, _ = x_sample.shape
    layer = 0

    cond = jnp.concatenate([c_ctx[None, :], c, jnp.zeros((COND_ROWS - 1 - n_s, D_MODEL), F32)], axis=0)
    mod = _modulation(cond, w_ada[layer], b_ada[layer][None, :]).reshape(COND_ROWS, 1, N_MOD * D_MODEL)
    mod_p = _Mod(mod, 0, True)
    mod_s = _Mod(mod, 1, False)

    in_small = (jnp.tile(q_norm_w[layer], N_HEADS)[None, :], jnp.tile(k_norm_w[layer], N_KV)[None, :],
                _block_ones(Q_WIDTH), _block_ones(KV_WIDTH), hgrn_lb_logits)
    nmw = norm_mix_w[layer][None, :]
    hnw = hgrn_norm_w[layer][None, :]
    xp2 = x_prompt.reshape(n_p * l_p, D_MODEL)
    xs2 = x_sample.reshape(n_s * l_s, D_MODEL)

    n_in, in_p = _in_part(MIX_TOKEN_TILE, xp2, mod_p, l_p, nmw, w_in[layer], *in_small, None)
    steps, parts = _same_steps((n_in, in_p), _cast_part(
        [w_in[layer], w_hgrn_out[layer], w_attn_out[layer], w_out[layer], w_ff1[layer], w_ff2[layer]], n_in))
    (zh_p, q_p, k_p, kt_p, vt_p, gates_p), (w_in_b, w_ho_b, w_ao_b, w_out_b, w_ff1_b, w_ff2_b) = _launch(
        "in_ctx", steps, parts)
    out_w = (w_ho_b, w_ao_b, w_out_b, norm_ffn_w[layer][None, :], w_ff1_b, w_ff2_b, final_norm_w[None, :])

    steps, parts = _same_steps(
        _attn_part(q_p, [(k_p, vt_p, None)], n_p, l_p, N_HEADS),
        _in_part(MIX_TOKEN_TILE, xs2, mod_s, l_s, nmw, w_in_b, *in_small, _rope_tables(l_s)),
        _hgrn_part(zh_p, hnw, n_p, l_p, HG_HEADS, None, True))
    (oa_p,), (zh_s, q_s, k_s, vt_s, gates_s), (oh_p, s_fin) = _launch("in_latent_mix_ctx", steps, parts,
                                                                      MIX_PACES_IN)

    segs = [(_to_slab(cache_k), _to_slab(cache_v), layer), (k_s, vt_s, None)]
    steps, parts = _same_steps(
        _out_part(MIX_TOKEN_TILE, xp2, mod_p, l_p, oh_p, oa_p, gates_p, *out_w),
        _attn_part(q_s, segs, n_s, l_s, ATTN_LOOKAHEAD),
        _hgrn_part(zh_s, hnw, n_s, l_s, 1, (state_hgrn, layer), False))
    (y_p,), (oa_s,), (oh_s,) = _launch("out_ctx_mix_latent", steps, parts, MIX_PACES_OUT)

    steps, parts = _same_steps(_out_part(SOLO_TOKEN_TILE, xs2, mod_s, l_s, oh_s, oa_s, gates_s, *out_w))
    ((y_s,),) = _launch("out_latent", steps, parts)

    new_k = _from_slab(kt_p)[:, None]
    new_v = _from_slab(vt_p)[:, None]
    new_s = s_fin.reshape(n_p, 1, 2, HG_HEADS, HG_DK, HG_DV)
    return (y_p.reshape(n_p, l_p, D_MODEL), y_s.reshape(n_s, l_s, D_MODEL), new_k, new_v, new_s)
```

```python
import functools
from typing import Any, Callable, NamedTuple

import numpy as np
import jax
import jax.numpy as jnp
from jax import lax
from jax.experimental import pallas as pl
from jax.experimental.pallas import tpu as pltpu

F32 = jnp.float32
BF16 = jnp.bfloat16

D_MODEL = 1024
GRID_W = 64
EPS = 1e-6
HG_HEADS = 4
HG_DK = 128
HG_DV = 128
HG_WIDTH = HG_HEADS * HG_DK
HG_CHUNK = 32
HG_PAIR = 2 * HG_CHUNK
N_HEADS = 8
N_KV = 2
HEAD_DIM = 64
Q_WIDTH = N_HEADS * HEAD_DIM
KV_WIDTH = N_KV * HEAD_DIM
ROPE_THETA = 10000.0
D_FF = 4 * D_MODEL
N_MOD = 6
ZH_WIDTH = 5 * HG_WIDTH
D_IN = ZH_WIDTH + Q_WIDTH + 2 * KV_WIDTH + 2 * D_MODEL

V7X_VMEM_LIMIT_BYTES = 56 * 1024 * 1024
SUBLANES = 8
MXU_ROWS = 16
HG_BLOCK = 256
COND_ROWS = 16
ONES_ROWS = 16
LOG2_E = float(np.log2(np.e))
MIX_TOKEN_TILE = 256
SOLO_TOKEN_TILE = 512
ADALN_ROW_TILE = 256
W_CAST_ROWS = 128
DOT_COLS = 512
ATTN_Q_TILE = 256
ATTN_LOOKAHEAD = 2
MIX_PACES_IN = (1.0, 1.0, 0.8)
MIX_PACES_OUT = (0.85, 1.0, 1.0)


def _dot(a, b):
    return jnp.dot(a, b, preferred_element_type=F32)


def _dot_nt(a, b):
    return lax.dot_general(a, b, (((1,), (1,)), ((), ())), preferred_element_type=F32)


def _dot_tn(a, b):
    return lax.dot_general(a, b, (((0,), (0,)), ((), ())), preferred_element_type=F32)


def _mm_cost(m, k, n):
    return (m // MXU_ROWS) * -(-k // 256) * -(-n // 256)


def _dot_by_cols(a, w_ref, rows, cols):
    pieces = []
    for c in range(cols[0], cols[1], DOT_COLS):
        end = min(c + DOT_COLS, cols[1])
        pieces.append(_dot(a, w_ref[rows[0]:rows[1], c:end]))
        yield _mm_cost(a.shape[0], rows[1] - rows[0], end - c)
    return pieces[0] if len(pieces) == 1 else jnp.concatenate(pieces, axis=1)


def _split_bf16(x):
    hi = x.astype(BF16)
    lo = (x - hi.astype(F32)).astype(BF16)
    return hi, lo


def _const_spec(shape):
    nd = len(shape)
    return pl.BlockSpec(shape, lambda *_: (0,) * nd, pipeline_mode=pl.Buffered(1))


class _Part(NamedTuple):
    stages: Callable[..., Any]
    args: list
    in_specs: list
    out_specs: list
    out_shape: list
    scratch: list
    cost: int


def _interleave(gens, totals):
    done = [0] * len(gens)
    live = list(range(len(gens)))
    while live:
        i = min(live, key=lambda j: done[j] / totals[j])
        try:
            done[i] += next(gens[i])
        except StopIteration:
            live.remove(i)


def _launch(name, n_steps, parts, paces=None):
    paces = paces or [1.0] * len(parts)
    n_in = [len(p.args) for p in parts]
    n_out = [len(p.out_shape) for p in parts]
    n_scr = [len(p.scratch) for p in parts]

    def body(*refs):
        refs = list(refs)
        ins = [[refs.pop(0) for _ in range(n)] for n in n_in]
        outs = [[refs.pop(0) for _ in range(n)] for n in n_out]
        scrs = [[refs.pop(0) for _ in range(n)] for n in n_scr]
        _interleave([p.stages(i, o, s) for p, i, o, s in zip(parts, ins, outs, scrs)],
                    [p.cost * pace for p, pace in zip(parts, paces)])

    flat = pl.pallas_call(
        body,
        grid=(n_steps,),
        in_specs=[s for p in parts for s in p.in_specs],
        out_specs=[s for p in parts for s in p.out_specs],
        out_shape=[s for p in parts for s in p.out_shape],
        scratch_shapes=[s for p in parts for s in p.scratch],
        compiler_params=pltpu.CompilerParams(dimension_semantics=("arbitrary",),
                                             vmem_limit_bytes=V7X_VMEM_LIMIT_BYTES),
        name=name,
    )(*[a for p in parts for a in p.args])
    flat = list(flat)
    return [[flat.pop(0) for _ in range(n)] for n in n_out]


class _Mod(NamedTuple):
    rows: Any
    first: int
    shared: bool


def _mod_spec(mod, tile, seq_len, tile_of_step=lambda i: i):
    if mod.shared:
        return pl.BlockSpec((1, 1, N_MOD * D_MODEL), lambda i: (mod.first, 0, 0))
    assert seq_len % tile == 0
    per_seq = seq_len // tile
    return pl.BlockSpec((1, 1, N_MOD * D_MODEL), lambda i: (mod.first + tile_of_step(i) // per_seq, 0, 0))


def _mod_kernel(cctx_ref, c_ref, w_ref, b_ref, o_ref, cond_scr):
    n_req = c_ref.shape[0]
    cond_scr[...] = jnp.zeros_like(cond_scr)
    cond_scr[0:1, :] = cctx_ref[...]
    cond_scr[1:1 + n_req, :] = c_ref[...]
    c = cond_scr[...]
    x = c * jax.nn.sigmoid(c)
    x_hi, x_lo = _split_bf16(x)
    w = w_ref[...].astype(BF16)
    part = _dot(x_hi, w) + _dot(x_lo, w)

    @pl.when(pl.program_id(0) == 0)
    def _():
        o_ref[:, 0, :] = b_ref[...] + part

    @pl.when(pl.program_id(0) > 0)
    def _():
        o_ref[:, 0, :] += part


def _modulation(c_ctx, c, w_ada, b_ada):
    n = N_MOD * D_MODEL
    tk = ADALN_ROW_TILE
    n_req = c.shape[0]
    assert 1 + n_req <= COND_ROWS
    return pl.pallas_call(
        _mod_kernel,
        grid=(D_MODEL // tk,),
        in_specs=[pl.BlockSpec((1, tk), lambda j: (0, j)),
                  pl.BlockSpec((n_req, tk), lambda j: (0, j)),
                  pl.BlockSpec((tk, n), lambda j: (j, 0)),
                  pl.BlockSpec((1, n), lambda j: (0, 0))],
        out_specs=pl.BlockSpec((COND_ROWS, 1, n), lambda j: (0, 0, 0)),
        out_shape=jax.ShapeDtypeStruct((COND_ROWS, 1, n), F32),
        scratch_shapes=[pltpu.VMEM((COND_ROWS, tk), F32)],
        compiler_params=pltpu.CompilerParams(dimension_semantics=("arbitrary",),
                                             vmem_limit_bytes=V7X_VMEM_LIMIT_BYTES),
        name="adaln_mod",
    )(c_ctx, c, w_ada, b_ada)


def _rms_rows(x, w):
    return x * lax.rsqrt(jnp.mean(x * x, axis=-1, keepdims=True) + EPS) * w


def _head_sumsq(a, ones_ref):
    return _dot((a * a).astype(BF16), ones_ref[...])


def _head_rms(a, sumsq, w):
    return a * lax.rsqrt(sumsq * (1.0 / HEAD_DIM) + EPS) * w


def _forget_lower_bound(logits):
    e = jnp.exp(logits - jnp.max(logits, axis=1, keepdims=True))
    return e[:, 0, :] / jnp.sum(e, axis=1)


def _rope(x, cos, s_up, s_dn):
    cols = []
    for j in range(x.shape[1] // 128):
        sl = slice(j * 128, (j + 1) * 128)
        xj = x[:, sl]
        cols.append(xj * cos[:, sl] + pltpu.roll(xj, 112, 1) * s_up[:, sl]
                    + pltpu.roll(xj, 16, 1) * s_dn[:, sl])
    return jnp.concatenate(cols, axis=1) if len(cols) > 1 else cols[0]


def _in_stages(latent, ins, outs, scrs):
    x_ref, mod_ref, nw_ref, w_ref, qw_ref, kw_ref, oq_ref, ok_ref, lbl_ref = ins[:9]
    if scrs:
        (w_bf_ref,) = scrs
        rows = W_CAST_ROWS

        @pl.when(pl.program_id(0) == 0)
        def _():
            def cast_rows(r, carry):
                sl = pl.ds(pl.multiple_of(r * rows, rows), rows)
                w_bf_ref[sl, :] = w_ref[sl, :].astype(BF16)
                return carry

            lax.fori_loop(0, w_ref.shape[0] // rows, cast_rows, 0)

        w_ref = w_bf_ref
    if latent:
        cos_ref, sup_ref, sdn_ref = ins[9:]
        zh_ref, q_ref, k_ref, vt_ref, g_ref = outs
    else:
        zh_ref, q_ref, k_ref, kt_ref, vt_ref, g_ref = outs
    tm = x_ref.shape[0]
    m = mod_ref[0]
    sh1 = m[:, 0:D_MODEL]
    sc1 = m[:, D_MODEL:2 * D_MODEL]
    h = _rms_rows(x_ref[...], nw_ref[...]) * (1.0 + sc1) + sh1
    hb = h.astype(BF16)
    c0 = ZH_WIDTH
    c1 = c0 + Q_WIDTH
    c2 = c1 + KV_WIDTH
    c3 = c2 + KV_WIDTH
    aq = _dot(hb, w_ref[:, c0:c1])
    ak = _dot(hb, w_ref[:, c1:c2])
    av = _dot(hb, w_ref[:, c2:c3])
    yield _mm_cost(tm, D_MODEL, c3 - c0)
    gl = yield from _dot_by_cols(hb, w_ref, (0, D_MODEL), (c3, D_IN))
    ssq = _head_sumsq(aq, oq_ref)
    ssk = _head_sumsq(ak, ok_ref)
    yield _mm_cost(tm, Q_WIDTH, Q_WIDTH) + _mm_cost(tm, KV_WIDTH, KV_WIDTH)
    lb = _forget_lower_bound(lbl_ref[...])

    def hgrn_segment(s):
        z = yield from _dot_by_cols(hb, w_ref, (0, D_MODEL), (s * HG_WIDTH, (s + 1) * HG_WIDTH))
        if s == 0:
            z = z * HG_DK ** -0.5
        elif s in (1, 2):
            lbv = lb[s - 1:s, :]
            z = jnp.log(lbv + (1.0 - lbv) * jax.nn.sigmoid(z))
        elif s == 4:
            z = z * jax.nn.sigmoid(z)
        zh_ref[:, s * HG_WIDTH:(s + 1) * HG_WIDTH] = z

    yield from hgrn_segment(1)
    yield from hgrn_segment(2)
    g_ref[...] = jax.nn.sigmoid(gl).astype(BF16)
    qn = _head_rms(aq, ssq, jnp.tile(qw_ref[...], (1, N_HEADS)))
    kn = _head_rms(ak, ssk, jnp.tile(kw_ref[...], (1, N_KV)))
    seq = vt_ref.shape[2]
    avt = av.T
    for s in range(vt_ref.shape[0]):
        vt_ref[s] = avt[:, s * seq:(s + 1) * seq].astype(vt_ref.dtype)
    for s in (4, 0, 3):
        yield from hgrn_segment(s)
    if latent:
        cos, sup, sdn = cos_ref[...], sup_ref[...], sdn_ref[...]
        qn = _rope(qn, cos, sup, sdn)
        kn = _rope(kn, cos[:, 0:KV_WIDTH], sup[:, 0:KV_WIDTH], sdn[:, 0:KV_WIDTH])
    else:
        knt = kn.T
        for s in range(kt_ref.shape[0]):
            kt_ref[s] = knt[:, s * seq:(s + 1) * seq]
    q_ref[...] = (qn * (HEAD_DIM ** -0.5 * LOG2_E)).astype(BF16)
    k_ref[...] = kn.astype(BF16)


def _in_part(tm, x2, mod, seq_len, norm_w, w_in, qw, kw, ones_q, ones_k, lb_logits, rope):
    t = x2.shape[0]
    latent = rope is not None
    per_seq = max(seq_len // tm, 1)
    row = lambda i: (i, 0)
    in_specs = [pl.BlockSpec((tm, D_MODEL), row),
                _mod_spec(mod, tm, seq_len),
                _const_spec((1, D_MODEL)),
                _const_spec((D_MODEL, D_IN)),
                _const_spec((1, HEAD_DIM)),
                _const_spec((1, HEAD_DIM)),
                _const_spec((Q_WIDTH, Q_WIDTH)),
                _const_spec((KV_WIDTH, KV_WIDTH)),
                _const_spec(lb_logits.shape)]
    args = [x2, mod.rows, norm_w, w_in, qw, kw, ones_q, ones_k, lb_logits]
    out_specs = [pl.BlockSpec((tm, ZH_WIDTH), row), pl.BlockSpec((tm, Q_WIDTH), row),
                 pl.BlockSpec((tm, KV_WIDTH), row)]
    out_shape = [jax.ShapeDtypeStruct((t, ZH_WIDTH), F32), jax.ShapeDtypeStruct((t, Q_WIDTH), BF16),
                 jax.ShapeDtypeStruct((t, KV_WIDTH), BF16)]
    if seq_len <= tm:
        assert tm % seq_len == 0
        t_spec = pl.BlockSpec((tm // seq_len, KV_WIDTH, seq_len), lambda i: (i, 0, 0))
    else:
        t_spec = pl.BlockSpec((1, KV_WIDTH, tm), lambda i: (i // per_seq, 0, i % per_seq))
    if latent:
        in_specs += [pl.BlockSpec((tm, Q_WIDTH), lambda i: (i % per_seq, 0))] * 3
        args += list(rope)
        out_specs.append(t_spec)
        out_shape.append(jax.ShapeDtypeStruct((t // seq_len, KV_WIDTH, seq_len), BF16))
    else:
        out_specs += [t_spec, t_spec]
        out_shape += [jax.ShapeDtypeStruct((t // seq_len, KV_WIDTH, seq_len), F32)] * 2
    out_specs.append(pl.BlockSpec((tm, 2 * D_MODEL), row))
    out_shape.append(jax.ShapeDtypeStruct((t, 2 * D_MODEL), BF16))
    cost = (_mm_cost(tm, D_MODEL, D_IN) + _mm_cost(tm, Q_WIDTH, Q_WIDTH) + _mm_cost(tm, KV_WIDTH, KV_WIDTH))
    scratch = [] if w_in.dtype == BF16 else [pltpu.VMEM((D_MODEL, D_IN), BF16)]
    return t // tm, _Part(functools.partial(_in_stages, latent), args, in_specs, out_specs, out_shape, scratch, cost)


def _attn_cost(lq, lks):
    per_head = sum(_mm_cost(lk, HEAD_DIM, lq) + _mm_cost(HEAD_DIM + ONES_ROWS, lk, lq) for lk in lks)
    return N_HEADS * per_head


def _attn_stages(k_transposed, lookahead, ins, outs, _):
    n_seg = len(k_transposed)
    q_ref, kv_refs = ins[0], ins[1:]
    (o_ref,) = outs
    q = q_ref[...]
    lq = q.shape[0]
    ks = [(kv_refs[2 * s][0].T if k_transposed[s] else kv_refs[2 * s][...]).astype(BF16) for s in range(n_seg)]
    vts = [kv_refs[2 * s + 1][0].astype(BF16) for s in range(n_seg)]
    vts = [[jnp.concatenate([vt[g * HEAD_DIM:(g + 1) * HEAD_DIM, :],
                             jnp.ones((ONES_ROWS, vt.shape[1]), BF16)], axis=0) for g in range(N_KV)]
           for vt in vts]
    score_cost = sum(_mm_cost(k.shape[0], HEAD_DIM, lq) for k in ks)
    value_cost = sum(_mm_cost(HEAD_DIM + ONES_ROWS, k.shape[0], lq) for k in ks)

    def scores(h):
        g = h // (N_HEADS // N_KV)
        qh = q[:, h * HEAD_DIM:(h + 1) * HEAD_DIM]
        return [_dot_nt(k[:, g * HEAD_DIM:(g + 1) * HEAD_DIM], qh) for k in ks]

    pending = []
    for h in range(lookahead):
        pending.append(scores(h))
        yield score_cost
    outs_t = []
    for h in range(N_HEADS):
        if h + lookahead < N_HEADS:
            pending.append(scores(h + lookahead))
            yield score_cost
        st = pending.pop(0)
        g = h // (N_HEADS // N_KV)
        m = functools.reduce(jnp.maximum, [jnp.max(s, axis=0, keepdims=True) for s in st])
        ot = functools.reduce(jnp.add, [_dot(vt[g], jnp.exp2(s - m).astype(BF16)) for vt, s in zip(vts, st)])
        outs_t.append(ot[0:HEAD_DIM, :] / ot[HEAD_DIM:HEAD_DIM + 1, :])
        if h == N_HEADS - 1:
            o_ref[...] = jnp.concatenate(outs_t, axis=0).T.astype(BF16)
        yield value_cost


def _attn_part(q, kv_segments, n_batch, seq_len, lookahead):
    t = q.shape[0]
    tq = ATTN_Q_TILE
    per_seq = seq_len // tq
    in_specs = [pl.BlockSpec((tq, Q_WIDTH), lambda i: (i, 0))]
    args = [q]
    for k, v_t, layer in kv_segments:
        lk = v_t.shape[-1]
        if v_t.ndim == 4:
            slab = pl.BlockSpec((1, None, KV_WIDTH, lk), lambda i, layer=layer: (i // per_seq, layer, 0, 0))
        else:
            slab = pl.BlockSpec((1, KV_WIDTH, lk), lambda i: (i // per_seq, 0, 0))
        k_spec = slab if k.ndim == v_t.ndim else pl.BlockSpec((lk, KV_WIDTH), lambda i: (i // per_seq, 0))
        in_specs += [k_spec, slab]
        args += [k, v_t]
    stages = functools.partial(_attn_stages, tuple(k.ndim == v_t.ndim for k, v_t, _ in kv_segments), lookahead)
    return n_batch * per_seq, _Part(stages, args, in_specs, [pl.BlockSpec((tq, Q_WIDTH), lambda i: (i, 0))],
                                    [jax.ShapeDtypeStruct((t, Q_WIDTH), BF16)], [],
                                    _attn_cost(tq, [v_t.shape[-1] for _, v_t, _ in kv_segments]))


def _hgrn_unit_cost():
    blk = HG_BLOCK
    return (2 * _mm_cost(blk, blk, 2 * HG_DK)
            + 2 * _mm_cost(blk, HG_DK, 2 * blk) + _mm_cost(blk, blk, HG_DV)
            + (blk // HG_PAIR) * (_mm_cost(HG_DV, HG_PAIR, 2 * HG_DK) + _mm_cost(HG_PAIR, 2 * HG_DK, HG_DV)))


def _hgrn_stages(seq_len, hps, has_s0, has_sfin, ins, outs, scrs):
    tf_ref, tb_ref, xf_ref, xb_ref, q_ref, lf_ref, lb_ref, v_ref, g_ref, nw_ref = ins[:10]
    s0_ref = ins[10] if has_s0 else None
    o_ref = outs[0]
    sfin_ref = outs[1] if has_sfin else None
    kv_scr, ss_scr, qd_scr, oi_scr, dec_scr = scrs
    n_blk = seq_len // HG_BLOCK
    n_pair = seq_len // HG_PAIR
    per_blk = HG_BLOCK // HG_CHUNK
    pairs_per_blk = HG_BLOCK // HG_PAIR
    one_row = jnp.ones((1, HG_DK), F32)

    def chunk_cumsum(t_ref, x):
        hi, lo = _split_bf16(x)
        r = _dot(t_ref[...], jnp.concatenate([hi, lo], axis=1))
        return r[:, 0:HG_DK] + r[:, HG_DK:2 * HG_DK]

    def chunk_edge_rows(x, first):
        off = 0 if first else HG_CHUNK - 1
        return [x[c * HG_CHUNK + off:c * HG_CHUNK + off + 1, :] for c in range(per_blk)]

    def spread_rows(rows):
        return jnp.concatenate([jnp.broadcast_to(r, (HG_CHUNK, HG_DK)) for r in rows], axis=0)

    units = [(blk, h) for blk in range(n_blk) for h in range(hps)]
    stage1 = []
    for blk, h in units:
        rows = slice(blk * HG_BLOCK, (blk + 1) * HG_BLOCK)
        cols = slice(h * HG_DK, (h + 1) * HG_DK)
        logf_f = lf_ref[rows, cols]
        logf_b = lb_ref[rows, cols]
        k_f = 1.0 - jnp.exp(logf_f)
        k_b = 1.0 - jnp.exp(logf_b)
        a_f = chunk_cumsum(tf_ref, logf_f)
        a_b = chunk_cumsum(tb_ref, logf_b)
        stage1.append((rows, cols, k_f, k_b, a_f, a_b))
        yield 2 * _mm_cost(HG_BLOCK, HG_BLOCK, 2 * HG_DK)
    stage2 = []
    for (blk, h), (rows, cols, k_f, k_b, a_f, a_b) in zip(units, stage1):
        q = q_ref[rows, cols]
        vb = v_ref[rows, cols].astype(BF16)
        ea_f = jnp.exp(a_f)
        ea_b = jnp.exp(a_b)
        qd_f = q * ea_f
        qd_b = q * ea_b
        kd_f = k_f * jnp.exp(-a_f)
        kd_b = k_b * jnp.exp(-a_b)
        dec_f = chunk_edge_rows(ea_f, first=False)
        dec_b = chunk_edge_rows(ea_b, first=True)
        ke_f = kd_f * spread_rows(dec_f)
        ke_b = kd_b * spread_rows(dec_b)
        even = [c % 2 == 0 for c in range(per_blk)]
        qd_pair = jnp.concatenate(
            [qd_f * spread_rows([one_row if even[c] else dec_f[c - 1] for c in range(per_blk)]),
             qd_b * spread_rows([dec_b[c + 1] if even[c] else one_row for c in range(per_blk)])], axis=1)
        ke_pair = jnp.concatenate(
            [ke_f * spread_rows([dec_f[c + 1] if even[c] else one_row for c in range(per_blk)]),
             ke_b * spread_rows([one_row if even[c] else dec_b[c - 1] for c in range(per_blk)])], axis=1).astype(BF16)
        sc_f = _dot_nt(qd_f.astype(BF16), jnp.concatenate([kd_f, ke_f], axis=0).astype(BF16))
        sc_b = _dot_nt(qd_b.astype(BF16), jnp.concatenate([kd_b, ke_b], axis=0).astype(BF16))
        for p in range(pairs_per_blk):
            pr = slice(p * HG_PAIR, (p + 1) * HG_PAIR)
            kv_scr[h, blk * pairs_per_blk + p] = _dot_tn(vb[pr, :], ke_pair[pr, :])
            dec_scr[h, blk * pairs_per_blk + p, 0:1, :] = jnp.concatenate(
                [dec_f[2 * p] * dec_f[2 * p + 1], dec_b[2 * p] * dec_b[2 * p + 1]], axis=1)
        qd_scr[h, rows, :] = qd_pair.astype(BF16)
        stage2.append((rows, cols, vb, sc_f, sc_b))
        yield 2 * _mm_cost(HG_BLOCK, HG_DK, 2 * HG_BLOCK) + pairs_per_blk * _mm_cost(HG_DV, HG_PAIR, 2 * HG_DK)
    for rows, cols, vb, sc_f, sc_b in stage2:
        s = (sc_f[:, 0:HG_BLOCK].astype(BF16) * tf_ref[...] + sc_f[:, HG_BLOCK:].astype(BF16) * xf_ref[...]
             + sc_b[:, 0:HG_BLOCK].astype(BF16) * tb_ref[...] + sc_b[:, HG_BLOCK:].astype(BF16) * xb_ref[...])
        oi_scr[rows, cols] = _dot(s, vb)
        yield _mm_cost(HG_BLOCK, HG_BLOCK, HG_DV)

    for h in range(hps):
        if has_s0:
            s_f, s_b = s0_ref[0, 0, h].T, s0_ref[0, 1, h].T
        else:
            s_f = s_b = jnp.zeros((HG_DV, HG_DK), F32)
        for pf in range(n_pair):
            pb = n_pair - 1 - pf
            ss_scr[h, pf, :, 0:HG_DK] = s_f.astype(BF16)
            s_f = dec_scr[h, pf, 0:1, 0:HG_DK] * s_f + kv_scr[h, pf, :, 0:HG_DK]
            ss_scr[h, pb, :, HG_DK:2 * HG_DK] = s_b.astype(BF16)
            s_b = dec_scr[h, pb, 0:1, HG_DK:2 * HG_DK] * s_b + kv_scr[h, pb, :, HG_DK:2 * HG_DK]
        if has_sfin:
            sfin_ref[0, 0, h] = s_f.T
            sfin_ref[0, 1, h] = s_b.T

    for blk, h in units:
        cols = slice(h * HG_DK, (h + 1) * HG_DK)
        for p in range(pairs_per_blk):
            pi = blk * pairs_per_blk + p
            rows = slice(pi * HG_PAIR, (pi + 1) * HG_PAIR)
            oi_scr[rows, cols] = oi_scr[rows, cols] + _dot_nt(qd_scr[h, rows, :], ss_scr[h, pi])
        yield pairs_per_blk * _mm_cost(HG_PAIR, 2 * HG_DK, HG_DV)

    for h in range(hps):
        cols = slice(h * HG_DK, (h + 1) * HG_DK)
        o = _rms_rows(oi_scr[:, cols], nw_ref[...])
        o_ref[:, cols] = (o * g_ref[:, cols]).astype(BF16)


def _chunk_masks():
    r = np.arange(HG_BLOCK)
    same = (r[:, None] // HG_CHUNK) == (r[None, :] // HG_CHUNK)
    lower = same & (r[None, :] <= r[:, None])
    upper = same & (r[None, :] >= r[:, None])
    same_pair = (r[:, None] // HG_PAIR) == (r[None, :] // HG_PAIR)
    cross_f = same_pair & (r[:, None] // HG_CHUNK == r[None, :] // HG_CHUNK + 1)
    return tuple(jnp.asarray(m.astype(np.float32), dtype=BF16) for m in (lower, upper, cross_f, cross_f.T))


def _hgrn_part(zh, norm_w, n_batch, seq_len, hps, s0, want_final):
    t = zh.shape[0]
    n_pair = seq_len // HG_PAIR
    width = hps * HG_DK
    n_hb = HG_HEADS // hps
    masks = _chunk_masks()

    def seg_spec(seg):
        return pl.BlockSpec((seq_len, width), lambda i: (i // n_hb, seg * n_hb + i % n_hb))

    state_spec = pl.BlockSpec((1, 2, hps, HG_DK, HG_DV), lambda i: (i // n_hb, 0, i % n_hb, 0, 0))
    mask_spec = pl.BlockSpec((HG_BLOCK, HG_BLOCK), lambda i: (0, 0))
    in_specs = [mask_spec] * len(masks) + [seg_spec(s) for s in range(5)] + [pl.BlockSpec((1, HG_DV), lambda i: (0, 0))]
    args = list(masks) + [zh] * 5 + [norm_w]
    if s0 is not None:
        states, layer = s0
        in_specs.append(pl.BlockSpec((1, None, 2, hps, HG_DK, HG_DV),
                                     lambda i: (i // n_hb, layer, 0, i % n_hb, 0, 0)))
        args.append(states)
    out_specs = [pl.BlockSpec((seq_len, width), lambda i: (i // n_hb, i % n_hb))]
    out_shape = [jax.ShapeDtypeStruct((t, HG_WIDTH), BF16)]
    if want_final:
        out_specs.append(state_spec)
        out_shape.append(jax.ShapeDtypeStruct((n_batch, 2, HG_HEADS, HG_DK, HG_DV), F32))
    scratch = [pltpu.VMEM((hps, n_pair, HG_DV, 2 * HG_DK), F32),
               pltpu.VMEM((hps, n_pair, HG_DV, 2 * HG_DK), BF16),
               pltpu.VMEM((hps, seq_len, 2 * HG_DK), BF16),
               pltpu.VMEM((seq_len, width), F32),
               pltpu.VMEM((hps, n_pair, SUBLANES, 2 * HG_DK), F32)]
    stages = functools.partial(_hgrn_stages, seq_len, hps, s0 is not None, want_final)
    cost = hps * (seq_len // HG_BLOCK) * _hgrn_unit_cost()
    return n_batch * n_hb, _Part(stages, args, in_specs, out_specs, out_shape, scratch, cost)


def _out_stages(ins, outs, _):
    x_ref, mod_ref, oh_ref, oa_ref, g_ref, who_ref, wao_ref, wout_ref, nfw_ref, wff1_ref, wff2_ref, fnw_ref = ins
    (y_ref,) = outs
    tm = x_ref.shape[0]
    m = mod_ref[0]
    g1 = m[:, 2 * D_MODEL:3 * D_MODEL]
    sh2 = m[:, 3 * D_MODEL:4 * D_MODEL]
    sc2 = m[:, 4 * D_MODEL:5 * D_MODEL]
    g2 = m[:, 5 * D_MODEL:6 * D_MODEL]
    gates = g_ref[...].astype(F32)
    from_h = yield from _dot_by_cols(oh_ref[...], who_ref, (0, HG_WIDTH), (0, D_MODEL))
    from_a = yield from _dot_by_cols(oa_ref[...], wao_ref, (0, Q_WIDTH), (0, D_MODEL))
    merged = gates[:, 0:D_MODEL] * from_h + gates[:, D_MODEL:2 * D_MODEL] * from_a
    mixed = yield from _dot_by_cols(merged.astype(BF16), wout_ref, (0, D_MODEL), (0, D_MODEL))
    x1 = x_ref[...] + g1 * mixed
    h2 = (_rms_rows(x1, nfw_ref[...]) * (1.0 + sc2) + sh2).astype(BF16)
    acc = jnp.zeros_like(x1)
    for j in range(D_FF // D_MODEL):
        cols = slice(j * D_MODEL, (j + 1) * D_MODEL)
        hj = yield from _dot_by_cols(h2, wff1_ref, (0, D_MODEL), (cols.start, cols.stop))
        hj = jnp.maximum(hj, 0.0)
        acc = acc + (yield from _dot_by_cols((hj * hj).astype(BF16), wff2_ref, (cols.start, cols.stop), (0, D_MODEL)))
    y_ref[...] = _rms_rows(x1 + g2 * acc, fnw_ref[...])


def _out_part(tm, x2, mod, seq_len, oh, oa, gates, w_ho, w_ao, w_out, nfw, w_ff1, w_ff2, fnw):
    t = x2.shape[0]
    row = lambda i: (i, 0)
    in_specs = [pl.BlockSpec((tm, D_MODEL), row),
                _mod_spec(mod, tm, seq_len),
                pl.BlockSpec((tm, HG_WIDTH), row),
                pl.BlockSpec((tm, Q_WIDTH), row),
                pl.BlockSpec((tm, 2 * D_MODEL), row),
                _const_spec((HG_WIDTH, D_MODEL)),
                _const_spec((Q_WIDTH, D_MODEL)),
                _const_spec((D_MODEL, D_MODEL)),
                _const_spec((1, D_MODEL)),
                _const_spec((D_MODEL, D_FF)),
                _const_spec((D_FF, D_MODEL)),
                _const_spec((1, D_MODEL))]
    args = [x2, mod.rows, oh, oa, gates, w_ho, w_ao, w_out, nfw, w_ff1, w_ff2, fnw]
    cost = _mm_cost(tm, HG_WIDTH + Q_WIDTH, D_MODEL) + _mm_cost(tm, D_MODEL, D_MODEL) + 2 * _mm_cost(tm, D_MODEL, D_FF)
    return t // tm, _Part(_out_stages, args, in_specs, [pl.BlockSpec((tm, D_MODEL), row)],
                          [jax.ShapeDtypeStruct((t, D_MODEL), F32)], [], cost)


def _cast_stages(ins, outs, _):
    for src, dst in zip(ins, outs):
        dst[...] = src[...].astype(dst.dtype)
    yield 1


def _cast_part(arrays, n_steps):
    in_specs, out_specs, out_shape = [], [], []
    for a in arrays:
        rows, cols = a.shape
        assert rows % (n_steps * MXU_ROWS) == 0
        spec = pl.BlockSpec((rows // n_steps, cols), lambda i: (i, 0))
        in_specs.append(spec)
        out_specs.append(spec)
        out_shape.append(jax.ShapeDtypeStruct(a.shape, BF16))
    return n_steps, _Part(_cast_stages, list(arrays), in_specs, out_specs, out_shape, [], 1)


def _rope_tables(n_tokens):
    rows = n_tokens // GRID_W
    row = np.repeat(np.arange(rows, dtype=np.float32), GRID_W)
    col = np.tile(np.arange(GRID_W, dtype=np.float32), rows)
    axis_dim = HEAD_DIM // 2
    freqs = (ROPE_THETA ** (-np.arange(0, axis_dim, 2, dtype=np.float32) / axis_dim)).astype(np.float32)
    ang_r = row[:, None] * freqs
    ang_c = col[:, None] * freqs
    cr, sr, cc, sc = np.cos(ang_r), np.sin(ang_r), np.cos(ang_c), np.sin(ang_c)
    zero = np.zeros_like(sr)
    cos = np.concatenate([cr, cr, cc, cc], axis=-1)
    s_up = np.concatenate([-sr, zero, -sc, zero], axis=-1)
    s_dn = np.concatenate([zero, sr, zero, sc], axis=-1)
    return tuple(jnp.asarray(np.tile(a, (1, N_HEADS)), dtype=F32) for a in (cos, s_up, s_dn))


def _to_slab(kv):
    n, depth, length = kv.shape[:3]
    return jnp.transpose(kv, (0, 1, 3, 4, 2)).reshape(n, depth, KV_WIDTH, length)


def _from_slab(slab):
    n, _, length = slab.shape
    return jnp.transpose(slab.reshape(n, N_KV, HEAD_DIM, length), (0, 3, 1, 2))


def _block_ones(width):
    idx = np.arange(width) // HEAD_DIM
    return jnp.asarray((idx[:, None] == idx[None, :]).astype(np.float32), dtype=BF16)


def _same_steps(*counted_parts):
    steps = {n for n, _ in counted_parts}
    assert len(steps) == 1, steps
    return steps.pop(), [p for _, p in counted_parts]


def kernel(x_prompt, x_sample, cache_k, cache_v, state_hgrn, c, c_ctx, w_ada, b_ada, norm_mix_w, w_in, q_norm_w, k_norm_w, hgrn_lb_logits, hgrn_norm_w, w_hgrn_out, w_attn_out, w_out, norm_ffn_w, w_ff1, w_ff2, final_norm_w):
    n_p, l_p, _ = x_prompt.shape
    n_s, l_s, _ = x_sample.shape
    layer = 0

    mod = _modulation(c_ctx[None, :], c, w_ada[layer], b_ada[layer][None, :])
    mod_p = _Mod(mod, 0, True)
    mod_s = _Mod(mod, 1, False)

    in_small = (q_norm_w[layer][None, :], k_norm_w[layer][None, :],
                _block_ones(Q_WIDTH), _block_ones(KV_WIDTH), hgrn_lb_logits)
    nmw = norm_mix_w[layer][None, :]
    hnw = hgrn_norm_w[layer][None, :]
    xp2 = x_prompt.reshape(n_p * l_p, D_MODEL)
    xs2 = x_sample.reshape(n_s * l_s, D_MODEL)

    n_in, in_p = _in_part(MIX_TOKEN_TILE, xp2, mod_p, l_p, nmw, w_in[layer], *in_small, None)
    steps, parts = _same_steps((n_in, in_p), _cast_part(
        [w_in[layer], w_hgrn_out[layer], w_attn_out[layer], w_out[layer], w_ff1[layer], w_ff2[layer]], n_in))
    (zh_p, q_p, k_p, kt_p, vt_p, gates_p), (w_in_b, w_ho_b, w_ao_b, w_out_b, w_ff1_b, w_ff2_b) = _launch(
        "in_ctx", steps, parts)
    out_w = (w_ho_b, w_ao_b, w_out_b, norm_ffn_w[layer][None, :], w_ff1_b, w_ff2_b, final_norm_w[None, :])

    steps, parts = _same_steps(
        _attn_part(q_p, [(k_p, vt_p, None)], n_p, l_p, N_HEADS),
        _in_part(MIX_TOKEN_TILE, xs2, mod_s, l_s, nmw, w_in_b, *in_small, _rope_tables(l_s)),
        _hgrn_part(zh_p, hnw, n_p, l_p, HG_HEADS, None, True))
    (oa_p,), (zh_s, q_s, k_s, vt_s, gates_s), (oh_p, s_fin) = _launch("in_latent_mix_ctx", steps, parts,
                                                                      MIX_PACES_IN)

    segs = [(_to_slab(cache_k), _to_slab(cache_v), layer), (k_s, vt_s, None)]
    steps, parts = _same_steps(
        _out_part(MIX_TOKEN_TILE, xp2, mod_p, l_p, oh_p, oa_p, gates_p, *out_w),
        _attn_part(q_s, segs, n_s, l_s, ATTN_LOOKAHEAD),
        _hgrn_part(zh_s, hnw, n_s, l_s, 1, (state_hgrn, layer), False))
    (y_p,), (oa_s,), (oh_s,) = _launch("out_ctx_mix_latent", steps, parts, MIX_PACES_OUT)

    steps, parts = _same_steps(_out_part(SOLO_TOKEN_TILE, xs2, mod_s, l_s, oh_s, oa_s, gates_s, *out_w))
    ((y_s,),) = _launch("out_latent", steps, parts)

    new_k = _from_slab(kt_p)[:, None]
    new_v = _from_slab(vt_p)[:, None]
    new_s = s_fin.reshape(n_p, 1, 2, HG_HEADS, HG_DK, HG_DV)
    return (y_p.reshape(n_p, l_p, D_MODEL), y_s.reshape(n_s, l_s, D_MODEL), new_k, new_v, new_s)
```

```python
import functools
from typing import Any, Callable, NamedTuple

import numpy as np
import jax
import jax.numpy as jnp
from jax import lax
from jax.experimental import pallas as pl
from jax.experimental.pallas import tpu as pltpu

F32 = jnp.float32
BF16 = jnp.bfloat16

D_MODEL = 1024
GRID_W = 64
EPS = 1e-6
HG_HEADS = 4
HG_DK = 128
HG_DV = 128
HG_WIDTH = HG_HEADS * HG_DK
HG_CHUNK = 32
HG_PAIR = 2 * HG_CHUNK
N_HEADS = 8
N_KV = 2
HEAD_DIM = 64
Q_WIDTH = N_HEADS * HEAD_DIM
KV_WIDTH = N_KV * HEAD_DIM
ROPE_THETA = 10000.0
D_FF = 4 * D_MODEL
N_MOD = 6
ZH_WIDTH = 5 * HG_WIDTH
D_IN = ZH_WIDTH + Q_WIDTH + 2 * KV_WIDTH + 2 * D_MODEL

V7X_VMEM_LIMIT_BYTES = 56 * 1024 * 1024
SUBLANES = 8
LANES = 128
MXU_ROWS = 16
MXU_TILE = 256
ROPE_PAIR_LANES = HEAD_DIM // 4
HG_BLOCK = 256
COND_ROWS = 16
ONES_ROWS = 16
LOG2_E = float(np.log2(np.e))
MIX_TOKEN_TILE = 256
SOLO_TOKEN_TILE = 512
ADALN_ROW_TILE = 256
W_CAST_ROWS = 128
DOT_COLS = 512
ATTN_Q_TILE = 256
ATTN_LOOKAHEAD = 2
MIX_PACES_IN = (1.0, 1.0, 0.9)
MIX_PACES_OUT = (0.95, 1.0, 0.9)


def _dot(a, b):
    return jnp.dot(a, b, preferred_element_type=F32)


def _dot_nt(a, b):
    return lax.dot_general(a, b, (((1,), (1,)), ((), ())), preferred_element_type=F32)


def _dot_tn(a, b):
    return lax.dot_general(a, b, (((0,), (0,)), ((), ())), preferred_element_type=F32)


def _mm_cost(m, k, n):
    return (m // MXU_ROWS) * -(-k // MXU_TILE) * -(-n // MXU_TILE)


def _dot_by_cols(a, w_ref, rows, cols):
    pieces = []
    for c in range(cols[0], cols[1], DOT_COLS):
        end = min(c + DOT_COLS, cols[1])
        pieces.append(_dot(a, w_ref[rows[0]:rows[1], c:end]))
        yield _mm_cost(a.shape[0], rows[1] - rows[0], end - c)
    return pieces[0] if len(pieces) == 1 else jnp.concatenate(pieces, axis=1)


def _split_bf16(x):
    hi = x.astype(BF16)
    lo = (x - hi.astype(F32)).astype(BF16)
    return hi, lo


def _const_spec(shape):
    nd = len(shape)
    return pl.BlockSpec(shape, lambda *_: (0,) * nd, pipeline_mode=pl.Buffered(1))


class _Part(NamedTuple):
    stages: Callable[..., Any]
    args: list
    in_specs: list
    out_specs: list
    out_shape: list
    scratch: list
    cost: int


def _interleave(gens, totals):
    done = [0] * len(gens)
    live = list(range(len(gens)))
    while live:
        i = min(live, key=lambda j: done[j] / totals[j])
        try:
            done[i] += next(gens[i])
        except StopIteration:
            live.remove(i)


def _launch(name, n_steps, parts, paces=None):
    paces = paces or [1.0] * len(parts)
    n_in = [len(p.args) for p in parts]
    n_out = [len(p.out_shape) for p in parts]
    n_scr = [len(p.scratch) for p in parts]

    def body(*refs):
        refs = list(refs)
        ins = [[refs.pop(0) for _ in range(n)] for n in n_in]
        outs = [[refs.pop(0) for _ in range(n)] for n in n_out]
        scrs = [[refs.pop(0) for _ in range(n)] for n in n_scr]
        _interleave([p.stages(i, o, s) for p, i, o, s in zip(parts, ins, outs, scrs)],
                    [p.cost * pace for p, pace in zip(parts, paces)])

    flat = pl.pallas_call(
        body,
        grid=(n_steps,),
        in_specs=[s for p in parts for s in p.in_specs],
        out_specs=[s for p in parts for s in p.out_specs],
        out_shape=[s for p in parts for s in p.out_shape],
        scratch_shapes=[s for p in parts for s in p.scratch],
        compiler_params=pltpu.CompilerParams(dimension_semantics=("arbitrary",),
                                             vmem_limit_bytes=V7X_VMEM_LIMIT_BYTES),
        name=name,
    )(*[a for p in parts for a in p.args])
    flat = list(flat)
    return [[flat.pop(0) for _ in range(n)] for n in n_out]


class _Mod(NamedTuple):
    rows: Any
    first: int
    shared: bool


def _mod_spec(mod, tile, seq_len, tile_of_step=lambda i: i):
    if mod.shared:
        return pl.BlockSpec((1, 1, N_MOD * D_MODEL), lambda i: (mod.first, 0, 0))
    assert seq_len % tile == 0
    per_seq = seq_len // tile
    return pl.BlockSpec((1, 1, N_MOD * D_MODEL), lambda i: (mod.first + tile_of_step(i) // per_seq, 0, 0))


def _mod_kernel(cctx_ref, c_ref, w_ref, b_ref, o_ref, cond_scr):
    n_req = c_ref.shape[0]
    cond_scr[...] = jnp.zeros_like(cond_scr)
    cond_scr[0:1, :] = cctx_ref[...]
    cond_scr[1:1 + n_req, :] = c_ref[...]
    c = cond_scr[...]
    x = c * jax.nn.sigmoid(c)
    x_hi, x_lo = _split_bf16(x)
    w = w_ref[...].astype(BF16)
    part = _dot(x_hi, w) + _dot(x_lo, w)

    @pl.when(pl.program_id(0) == 0)
    def _():
        o_ref[:, 0, :] = b_ref[...] + part

    @pl.when(pl.program_id(0) > 0)
    def _():
        o_ref[:, 0, :] += part


def _modulation(c_ctx, c, w_ada, b_ada):
    n = N_MOD * D_MODEL
    tk = ADALN_ROW_TILE
    n_req = c.shape[0]
    assert 1 + n_req <= COND_ROWS
    return pl.pallas_call(
        _mod_kernel,
        grid=(D_MODEL // tk,),
        in_specs=[pl.BlockSpec((1, tk), lambda j: (0, j)),
                  pl.BlockSpec((n_req, tk), lambda j: (0, j)),
                  pl.BlockSpec((tk, n), lambda j: (j, 0)),
                  pl.BlockSpec((1, n), lambda j: (0, 0))],
        out_specs=pl.BlockSpec((COND_ROWS, 1, n), lambda j: (0, 0, 0)),
        out_shape=jax.ShapeDtypeStruct((COND_ROWS, 1, n), F32),
        scratch_shapes=[pltpu.VMEM((COND_ROWS, tk), F32)],
        compiler_params=pltpu.CompilerParams(dimension_semantics=("arbitrary",),
                                             vmem_limit_bytes=V7X_VMEM_LIMIT_BYTES),
        name="adaln_mod",
    )(c_ctx, c, w_ada, b_ada)


def _rms_rows(x, w):
    return x * lax.rsqrt(jnp.mean(x * x, axis=-1, keepdims=True) + EPS) * w


def _head_sumsq(a, ones_ref):
    return _dot((a * a).astype(BF16), ones_ref[...])


def _head_rms(a, sumsq, w):
    return a * lax.rsqrt(sumsq * (1.0 / HEAD_DIM) + EPS) * w


def _forget_lower_bound(logits):
    e = jnp.exp(logits - jnp.max(logits, axis=1, keepdims=True))
    return e[:, 0, :] / jnp.sum(e, axis=1)


def _rope(x, cos, s_up, s_dn):
    cols = []
    for j in range(x.shape[1] // LANES):
        sl = slice(j * LANES, (j + 1) * LANES)
        xj = x[:, sl]
        cols.append(xj * cos[:, sl] + pltpu.roll(xj, LANES - ROPE_PAIR_LANES, 1) * s_up[:, sl]
                    + pltpu.roll(xj, ROPE_PAIR_LANES, 1) * s_dn[:, sl])
    return jnp.concatenate(cols, axis=1) if len(cols) > 1 else cols[0]


def _in_stages(latent, ins, outs, scrs):
    x_ref, mod_ref, nw_ref, w_ref, qw_ref, kw_ref, oq_ref, ok_ref, lbl_ref = ins[:9]
    if scrs:
        (w_bf_ref,) = scrs
        rows = W_CAST_ROWS

        @pl.when(pl.program_id(0) == 0)
        def _():
            def cast_rows(r, carry):
                sl = pl.ds(pl.multiple_of(r * rows, rows), rows)
                w_bf_ref[sl, :] = w_ref[sl, :].astype(BF16)
                return carry

            lax.fori_loop(0, w_ref.shape[0] // rows, cast_rows, 0)

        w_ref = w_bf_ref
    if latent:
        cos_ref, sup_ref, sdn_ref = ins[9:]
        zh_ref, q_ref, k_ref, vt_ref, g_ref = outs
    else:
        zh_ref, q_ref, k_ref, kt_ref, vt_ref, g_ref = outs
    tm = x_ref.shape[0]
    m = mod_ref[0]
    sh1 = m[:, 0:D_MODEL]
    sc1 = m[:, D_MODEL:2 * D_MODEL]
    h = _rms_rows(x_ref[...], nw_ref[...]) * (1.0 + sc1) + sh1
    hb = h.astype(BF16)
    c0 = ZH_WIDTH
    c1 = c0 + Q_WIDTH
    c2 = c1 + KV_WIDTH
    c3 = c2 + KV_WIDTH
    aq = _dot(hb, w_ref[:, c0:c1])
    ak = _dot(hb, w_ref[:, c1:c2])
    av = _dot(hb, w_ref[:, c2:c3])
    yield _mm_cost(tm, D_MODEL, c3 - c0)
    gl = yield from _dot_by_cols(hb, w_ref, (0, D_MODEL), (c3, D_IN))
    ssq = _head_sumsq(aq, oq_ref)
    ssk = _head_sumsq(ak, ok_ref)
    yield _mm_cost(tm, Q_WIDTH, Q_WIDTH) + _mm_cost(tm, KV_WIDTH, KV_WIDTH)
    lb = _forget_lower_bound(lbl_ref[...])

    def hgrn_segment(s):
        z = yield from _dot_by_cols(hb, w_ref, (0, D_MODEL), (s * HG_WIDTH, (s + 1) * HG_WIDTH))
        if s == 0:
            z = z * HG_DK ** -0.5
        elif s in (1, 2):
            lbv = lb[s - 1:s, :]
            z = jnp.log(lbv + (1.0 - lbv) * jax.nn.sigmoid(z))
        elif s == 4:
            z = z * jax.nn.sigmoid(z)
        zh_ref[:, s * HG_WIDTH:(s + 1) * HG_WIDTH] = z

    yield from hgrn_segment(1)
    yield from hgrn_segment(2)
    g_ref[...] = jax.nn.sigmoid(gl).astype(BF16)
    qn = _head_rms(aq, ssq, jnp.tile(qw_ref[...], (1, N_HEADS)))
    kn = _head_rms(ak, ssk, jnp.tile(kw_ref[...], (1, N_KV)))
    seq = vt_ref.shape[2]
    avt = av.T
    for s in range(vt_ref.shape[0]):
        vt_ref[s] = avt[:, s * seq:(s + 1) * seq].astype(vt_ref.dtype)
    for s in (4, 0, 3):
        yield from hgrn_segment(s)
    if latent:
        cos, sup, sdn = cos_ref[...], sup_ref[...], sdn_ref[...]
        qn = _rope(qn, cos, sup, sdn)
        kn = _rope(kn, cos[:, 0:KV_WIDTH], sup[:, 0:KV_WIDTH], sdn[:, 0:KV_WIDTH])
    else:
        knt = kn.T
        for s in range(kt_ref.shape[0]):
            kt_ref[s] = knt[:, s * seq:(s + 1) * seq]
    q_ref[...] = (qn * (HEAD_DIM ** -0.5 * LOG2_E)).astype(BF16)
    k_ref[...] = kn.astype(BF16)


def _in_part(tm, x2, mod, seq_len, norm_w, w_in, qw, kw, ones_q, ones_k, lb_logits, rope):
    t = x2.shape[0]
    latent = rope is not None
    per_seq = max(seq_len // tm, 1)
    row = lambda i: (i, 0)
    in_specs = [pl.BlockSpec((tm, D_MODEL), row),
                _mod_spec(mod, tm, seq_len),
                _const_spec((1, D_MODEL)),
                _const_spec((D_MODEL, D_IN)),
                _const_spec((1, HEAD_DIM)),
                _const_spec((1, HEAD_DIM)),
                _const_spec((Q_WIDTH, Q_WIDTH)),
                _const_spec((KV_WIDTH, KV_WIDTH)),
                _const_spec(lb_logits.shape)]
    args = [x2, mod.rows, norm_w, w_in, qw, kw, ones_q, ones_k, lb_logits]
    out_specs = [pl.BlockSpec((tm, ZH_WIDTH), row), pl.BlockSpec((tm, Q_WIDTH), row),
                 pl.BlockSpec((tm, KV_WIDTH), row)]
    out_shape = [jax.ShapeDtypeStruct((t, ZH_WIDTH), F32), jax.ShapeDtypeStruct((t, Q_WIDTH), BF16),
                 jax.ShapeDtypeStruct((t, KV_WIDTH), BF16)]
    if seq_len <= tm:
        assert tm % seq_len == 0
        t_spec = pl.BlockSpec((tm // seq_len, KV_WIDTH, seq_len), lambda i: (i, 0, 0))
    else:
        t_spec = pl.BlockSpec((1, KV_WIDTH, tm), lambda i: (i // per_seq, 0, i % per_seq))
    if latent:
        in_specs += [pl.BlockSpec((tm, Q_WIDTH), lambda i: (i % per_seq, 0))] * 3
        args += list(rope)
        out_specs.append(t_spec)
        out_shape.append(jax.ShapeDtypeStruct((t // seq_len, KV_WIDTH, seq_len), BF16))
    else:
        out_specs += [t_spec, t_spec]
        out_shape += [jax.ShapeDtypeStruct((t // seq_len, KV_WIDTH, seq_len), F32)] * 2
    out_specs.append(pl.BlockSpec((tm, 2 * D_MODEL), row))
    out_shape.append(jax.ShapeDtypeStruct((t, 2 * D_MODEL), BF16))
    cost = (_mm_cost(tm, D_MODEL, D_IN) + _mm_cost(tm, Q_WIDTH, Q_WIDTH) + _mm_cost(tm, KV_WIDTH, KV_WIDTH))
    scratch = [] if w_in.dtype == BF16 else [pltpu.VMEM((D_MODEL, D_IN), BF16)]
    return t // tm, _Part(functools.partial(_in_stages, latent), args, in_specs, out_specs, out_shape, scratch, cost)


def _attn_cost(lq, lks):
    per_head = sum(_mm_cost(lk, HEAD_DIM, lq) + _mm_cost(HEAD_DIM + ONES_ROWS, lk, lq) for lk in lks)
    return N_HEADS * per_head


def _attn_stages(k_transposed, lookahead, ins, outs, _):
    n_seg = len(k_transposed)
    q_ref, kv_refs = ins[0], ins[1:]
    (o_ref,) = outs
    q = q_ref[...]
    lq = q.shape[0]
    ks = [(kv_refs[2 * s][0].T if k_transposed[s] else kv_refs[2 * s][...]).astype(BF16) for s in range(n_seg)]
    vts = [kv_refs[2 * s + 1][0].astype(BF16) for s in range(n_seg)]
    vts = [[jnp.concatenate([vt[g * HEAD_DIM:(g + 1) * HEAD_DIM, :],
                             jnp.ones((ONES_ROWS, vt.shape[1]), BF16)], axis=0) for g in range(N_KV)]
           for vt in vts]
    score_cost = sum(_mm_cost(k.shape[0], HEAD_DIM, lq) for k in ks)
    value_cost = sum(_mm_cost(HEAD_DIM + ONES_ROWS, k.shape[0], lq) for k in ks)

    def scores(h):
        g = h // (N_HEADS // N_KV)
        qh = q[:, h * HEAD_DIM:(h + 1) * HEAD_DIM]
        return [_dot_nt(k[:, g * HEAD_DIM:(g + 1) * HEAD_DIM], qh) for k in ks]

    pending = []
    for h in range(lookahead):
        pending.append(scores(h))
        yield score_cost
    outs_t = []
    for h in range(N_HEADS):
        if h + lookahead < N_HEADS:
            pending.append(scores(h + lookahead))
            yield score_cost
        st = pending.pop(0)
        g = h // (N_HEADS // N_KV)
        m = functools.reduce(jnp.maximum, [jnp.max(s, axis=0, keepdims=True) for s in st])
        ot = functools.reduce(jnp.add, [_dot(vt[g], jnp.exp2(s - m).astype(BF16)) for vt, s in zip(vts, st)])
        outs_t.append(ot[0:HEAD_DIM, :] / ot[HEAD_DIM:HEAD_DIM + 1, :])
        if h == N_HEADS - 1:
            o_ref[...] = jnp.concatenate(outs_t, axis=0).T.astype(BF16)
        yield value_cost


def _attn_part(q, kv_segments, n_batch, seq_len, lookahead):
    t = q.shape[0]
    tq = ATTN_Q_TILE
    per_seq = seq_len // tq
    in_specs = [pl.BlockSpec((tq, Q_WIDTH), lambda i: (i, 0))]
    args = [q]
    for k, v_t, layer in kv_segments:
        lk = v_t.shape[-1]
        if v_t.ndim == 4:
            slab = pl.BlockSpec((1, None, KV_WIDTH, lk), lambda i, layer=layer: (i // per_seq, layer, 0, 0))
        else:
            slab = pl.BlockSpec((1, KV_WIDTH, lk), lambda i: (i // per_seq, 0, 0))
        k_spec = slab if k.ndim == v_t.ndim else pl.BlockSpec((lk, KV_WIDTH), lambda i: (i // per_seq, 0))
        in_specs += [k_spec, slab]
        args += [k, v_t]
    stages = functools.partial(_attn_stages, tuple(k.ndim == v_t.ndim for k, v_t, _ in kv_segments), lookahead)
    return n_batch * per_seq, _Part(stages, args, in_specs, [pl.BlockSpec((tq, Q_WIDTH), lambda i: (i, 0))],
                                    [jax.ShapeDtypeStruct((t, Q_WIDTH), BF16)], [],
                                    _attn_cost(tq, [v_t.shape[-1] for _, v_t, _ in kv_segments]))


def _hgrn_unit_cost():
    blk = HG_BLOCK
    return (2 * _mm_cost(blk, blk, 2 * HG_DK)
            + 2 * _mm_cost(blk, HG_DK, 2 * blk) + _mm_cost(blk, blk, HG_DV)
            + (blk // HG_PAIR) * (_mm_cost(HG_DV, HG_PAIR, 2 * HG_DK) + _mm_cost(HG_PAIR, 2 * HG_DK, HG_DV)))


def _hgrn_stages(seq_len, hps, has_s0, has_sfin, ins, outs, scrs):
    tf_ref, tb_ref, xf_ref, xb_ref, q_ref, lf_ref, lb_ref, v_ref, g_ref, nw_ref = ins[:10]
    s0_ref = ins[10] if has_s0 else None
    o_ref = outs[0]
    sfin_ref = outs[1] if has_sfin else None
    kv_scr, ss_scr, qd_scr, oi_scr, dec_scr = scrs
    n_blk = seq_len // HG_BLOCK
    n_pair = seq_len // HG_PAIR
    per_blk = HG_BLOCK // HG_CHUNK
    pairs_per_blk = HG_BLOCK // HG_PAIR
    one_row = jnp.ones((1, HG_DK), F32)

    def chunk_cumsum(t_ref, x):
        hi, lo = _split_bf16(x)
        r = _dot(t_ref[...], jnp.concatenate([hi, lo], axis=1))
        return r[:, 0:HG_DK] + r[:, HG_DK:2 * HG_DK]

    def chunk_edge_rows(x, first):
        off = 0 if first else HG_CHUNK - 1
        return [x[c * HG_CHUNK + off:c * HG_CHUNK + off + 1, :] for c in range(per_blk)]

    def spread_rows(rows):
        return jnp.concatenate([jnp.broadcast_to(r, (HG_CHUNK, HG_DK)) for r in rows], axis=0)

    units = [(blk, h) for blk in range(n_blk) for h in range(hps)]
    stage1 = []
    for blk, h in units:
        rows = slice(blk * HG_BLOCK, (blk + 1) * HG_BLOCK)
        cols = slice(h * HG_DK, (h + 1) * HG_DK)
        logf_f = lf_ref[rows, cols]
        logf_b = lb_ref[rows, cols]
        k_f = 1.0 - jnp.exp(logf_f)
        k_b = 1.0 - jnp.exp(logf_b)
        a_f = chunk_cumsum(tf_ref, logf_f)
        a_b = chunk_cumsum(tb_ref, logf_b)
        stage1.append((rows, cols, k_f, k_b, a_f, a_b))
        yield 2 * _mm_cost(HG_BLOCK, HG_BLOCK, 2 * HG_DK)
    stage2 = []
    for (blk, h), (rows, cols, k_f, k_b, a_f, a_b) in zip(units, stage1):
        q = q_ref[rows, cols]
        vb = v_ref[rows, cols].astype(BF16)
        ea_f = jnp.exp(a_f)
        ea_b = jnp.exp(a_b)
        qd_f = q * ea_f
        qd_b = q * ea_b
        kd_f = k_f * jnp.exp(-a_f)
        kd_b = k_b * jnp.exp(-a_b)
        dec_f = chunk_edge_rows(ea_f, first=False)
        dec_b = chunk_edge_rows(ea_b, first=True)
        ke_f = kd_f * spread_rows(dec_f)
        ke_b = kd_b * spread_rows(dec_b)
        even = [c % 2 == 0 for c in range(per_blk)]
        qd_pair = jnp.concatenate(
            [qd_f * spread_rows([one_row if even[c] else dec_f[c - 1] for c in range(per_blk)]),
             qd_b * spread_rows([dec_b[c + 1] if even[c] else one_row for c in range(per_blk)])], axis=1)
        ke_pair = jnp.concatenate(
            [ke_f * spread_rows([dec_f[c + 1] if even[c] else one_row for c in range(per_blk)]),
             ke_b * spread_rows([one_row if even[c] else dec_b[c - 1] for c in range(per_blk)])], axis=1).astype(BF16)
        sc_f = _dot_nt(qd_f.astype(BF16), jnp.concatenate([kd_f, ke_f], axis=0).astype(BF16))
        sc_b = _dot_nt(qd_b.astype(BF16), jnp.concatenate([kd_b, ke_b], axis=0).astype(BF16))
        for p in range(pairs_per_blk):
            pr = slice(p * HG_PAIR, (p + 1) * HG_PAIR)
            kv_scr[h, blk * pairs_per_blk + p] = _dot_tn(vb[pr, :], ke_pair[pr, :])
            dec_scr[h, blk * pairs_per_blk + p, 0:1, :] = jnp.concatenate(
                [dec_f[2 * p] * dec_f[2 * p + 1], dec_b[2 * p] * dec_b[2 * p + 1]], axis=1)
        qd_scr[h, rows, :] = qd_pair.astype(BF16)
        stage2.append((rows, cols, vb, sc_f, sc_b))
        yield 2 * _mm_cost(HG_BLOCK, HG_DK, 2 * HG_BLOCK) + pairs_per_blk * _mm_cost(HG_DV, HG_PAIR, 2 * HG_DK)
    for rows, cols, vb, sc_f, sc_b in stage2:
        s = (sc_f[:, 0:HG_BLOCK].astype(BF16) * tf_ref[...] + sc_f[:, HG_BLOCK:].astype(BF16) * xf_ref[...]
             + sc_b[:, 0:HG_BLOCK].astype(BF16) * tb_ref[...] + sc_b[:, HG_BLOCK:].astype(BF16) * xb_ref[...])
        oi_scr[rows, cols] = _dot(s, vb)
        yield _mm_cost(HG_BLOCK, HG_BLOCK, HG_DV)

    for h in range(hps):
        if has_s0:
            s_f, s_b = s0_ref[0, 0, h].T, s0_ref[0, 1, h].T
        else:
            s_f = s_b = jnp.zeros((HG_DV, HG_DK), F32)
        for pf in range(n_pair):
            pb = n_pair - 1 - pf
            ss_scr[h, pf, :, 0:HG_DK] = s_f.astype(BF16)
            s_f = dec_scr[h, pf, 0:1, 0:HG_DK] * s_f + kv_scr[h, pf, :, 0:HG_DK]
            ss_scr[h, pb, :, HG_DK:2 * HG_DK] = s_b.astype(BF16)
            s_b = dec_scr[h, pb, 0:1, HG_DK:2 * HG_DK] * s_b + kv_scr[h, pb, :, HG_DK:2 * HG_DK]
        if has_sfin:
            sfin_ref[0, 0, h] = s_f.T
            sfin_ref[0, 1, h] = s_b.T

    for blk, h in units:
        cols = slice(h * HG_DK, (h + 1) * HG_DK)
        for p in range(pairs_per_blk):
            pi = blk * pairs_per_blk + p
            rows = slice(pi * HG_PAIR, (pi + 1) * HG_PAIR)
            oi_scr[rows, cols] = oi_scr[rows, cols] + _dot_nt(qd_scr[h, rows, :], ss_scr[h, pi])
        yield pairs_per_blk * _mm_cost(HG_PAIR, 2 * HG_DK, HG_DV)

    for h in range(hps):
        cols = slice(h * HG_DK, (h + 1) * HG_DK)
        o = _rms_rows(oi_scr[:, cols], nw_ref[...])
        o_ref[:, cols] = (o * g_ref[:, cols]).astype(BF16)


def _chunk_masks():
    r = np.arange(HG_BLOCK)
    same = (r[:, None] // HG_CHUNK) == (r[None, :] // HG_CHUNK)
    lower = same & (r[None, :] <= r[:, None])
    upper = same & (r[None, :] >= r[:, None])
    same_pair = (r[:, None] // HG_PAIR) == (r[None, :] // HG_PAIR)
    cross_f = same_pair & (r[:, None] // HG_CHUNK == r[None, :] // HG_CHUNK + 1)
    return tuple(jnp.asarray(m.astype(np.float32), dtype=BF16) for m in (lower, upper, cross_f, cross_f.T))


def _hgrn_part(zh, norm_w, n_batch, seq_len, hps, s0, want_final):
    t = zh.shape[0]
    n_pair = seq_len // HG_PAIR
    width = hps * HG_DK
    n_hb = HG_HEADS // hps
    masks = _chunk_masks()

    def seg_spec(seg):
        return pl.BlockSpec((seq_len, width), lambda i: (i // n_hb, seg * n_hb + i % n_hb))

    state_spec = pl.BlockSpec((1, 2, hps, HG_DK, HG_DV), lambda i: (i // n_hb, 0, i % n_hb, 0, 0))
    mask_spec = pl.BlockSpec((HG_BLOCK, HG_BLOCK), lambda i: (0, 0))
    in_specs = [mask_spec] * len(masks) + [seg_spec(s) for s in range(5)] + [pl.BlockSpec((1, HG_DV), lambda i: (0, 0))]
    args = list(masks) + [zh] * 5 + [norm_w]
    if s0 is not None:
        states, layer = s0
        in_specs.append(pl.BlockSpec((1, None, 2, hps, HG_DK, HG_DV),
                                     lambda i: (i // n_hb, layer, 0, i % n_hb, 0, 0)))
        args.append(states)
    out_specs = [pl.BlockSpec((seq_len, width), lambda i: (i // n_hb, i % n_hb))]
    out_shape = [jax.ShapeDtypeStruct((t, HG_WIDTH), BF16)]
    if want_final:
        out_specs.append(state_spec)
        out_shape.append(jax.ShapeDtypeStruct((n_batch, 2, HG_HEADS, HG_DK, HG_DV), F32))
    scratch = [pltpu.VMEM((hps, n_pair, HG_DV, 2 * HG_DK), F32),
               pltpu.VMEM((hps, n_pair, HG_DV, 2 * HG_DK), BF16),
               pltpu.VMEM((hps, seq_len, 2 * HG_DK), BF16),
               pltpu.VMEM((seq_len, width), F32),
               pltpu.VMEM((hps, n_pair, SUBLANES, 2 * HG_DK), F32)]
    stages = functools.partial(_hgrn_stages, seq_len, hps, s0 is not None, want_final)
    cost = hps * (seq_len // HG_BLOCK) * _hgrn_unit_cost()
    return n_batch * n_hb, _Part(stages, args, in_specs, out_specs, out_shape, scratch, cost)


def _out_stages(ins, outs, _):
    x_ref, mod_ref, oh_ref, oa_ref, g_ref, who_ref, wao_ref, wout_ref, nfw_ref, wff1_ref, wff2_ref, fnw_ref = ins
    (y_ref,) = outs
    tm = x_ref.shape[0]
    m = mod_ref[0]
    g1 = m[:, 2 * D_MODEL:3 * D_MODEL]
    sh2 = m[:, 3 * D_MODEL:4 * D_MODEL]
    sc2 = m[:, 4 * D_MODEL:5 * D_MODEL]
    g2 = m[:, 5 * D_MODEL:6 * D_MODEL]
    gates = g_ref[...].astype(F32)
    from_h = yield from _dot_by_cols(oh_ref[...], who_ref, (0, HG_WIDTH), (0, D_MODEL))
    from_a = yield from _dot_by_cols(oa_ref[...], wao_ref, (0, Q_WIDTH), (0, D_MODEL))
    merged = gates[:, 0:D_MODEL] * from_h + gates[:, D_MODEL:2 * D_MODEL] * from_a
    mixed = yield from _dot_by_cols(merged.astype(BF16), wout_ref, (0, D_MODEL), (0, D_MODEL))
    x1 = x_ref[...] + g1 * mixed
    h2 = (_rms_rows(x1, nfw_ref[...]) * (1.0 + sc2) + sh2).astype(BF16)
    acc = jnp.zeros_like(x1)
    for j in range(D_FF // D_MODEL):
        cols = slice(j * D_MODEL, (j + 1) * D_MODEL)
        hj = yield from _dot_by_cols(h2, wff1_ref, (0, D_MODEL), (cols.start, cols.stop))
        hj = jnp.maximum(hj, 0.0)
        acc = acc + (yield from _dot_by_cols((hj * hj).astype(BF16), wff2_ref, (cols.start, cols.stop), (0, D_MODEL)))
    y_ref[...] = _rms_rows(x1 + g2 * acc, fnw_ref[...])


def _out_part(tm, x2, mod, seq_len, oh, oa, gates, w_ho, w_ao, w_out, nfw, w_ff1, w_ff2, fnw):
    t = x2.shape[0]
    row = lambda i: (i, 0)
    in_specs = [pl.BlockSpec((tm, D_MODEL), row),
                _mod_spec(mod, tm, seq_len),
                pl.BlockSpec((tm, HG_WIDTH), row),
                pl.BlockSpec((tm, Q_WIDTH), row),
                pl.BlockSpec((tm, 2 * D_MODEL), row),
                _const_spec((HG_WIDTH, D_MODEL)),
                _const_spec((Q_WIDTH, D_MODEL)),
                _const_spec((D_MODEL, D_MODEL)),
                _const_spec((1, D_MODEL)),
                _const_spec((D_MODEL, D_FF)),
                _const_spec((D_FF, D_MODEL)),
                _const_spec((1, D_MODEL))]
    args = [x2, mod.rows, oh, oa, gates, w_ho, w_ao, w_out, nfw, w_ff1, w_ff2, fnw]
    cost = _mm_cost(tm, HG_WIDTH + Q_WIDTH, D_MODEL) + _mm_cost(tm, D_MODEL, D_MODEL) + 2 * _mm_cost(tm, D_MODEL, D_FF)
    return t // tm, _Part(_out_stages, args, in_specs, [pl.BlockSpec((tm, D_MODEL), row)],
                          [jax.ShapeDtypeStruct((t, D_MODEL), F32)], [], cost)


def _cast_stages(ins, outs, _):
    for src, dst in zip(ins, outs):
        dst[...] = src[...].astype(dst.dtype)
    yield 1


def _cast_part(arrays, n_steps):
    in_specs, out_specs, out_shape = [], [], []
    for a in arrays:
        rows, cols = a.shape
        assert rows % (n_steps * MXU_ROWS) == 0
        spec = pl.BlockSpec((rows // n_steps, cols), lambda i: (i, 0))
        in_specs.append(spec)
        out_specs.append(spec)
        out_shape.append(jax.ShapeDtypeStruct(a.shape, BF16))
    return n_steps, _Part(_cast_stages, list(arrays), in_specs, out_specs, out_shape, [], 1)


def _rope_tables(n_tokens):
    rows = n_tokens // GRID_W
    row = np.repeat(np.arange(rows, dtype=np.float32), GRID_W)
    col = np.tile(np.arange(GRID_W, dtype=np.float32), rows)
    axis_dim = HEAD_DIM // 2
    freqs = (ROPE_THETA ** (-np.arange(0, axis_dim, 2, dtype=np.float32) / axis_dim)).astype(np.float32)
    ang_r = row[:, None] * freqs
    ang_c = col[:, None] * freqs
    cr, sr, cc, sc = np.cos(ang_r), np.sin(ang_r), np.cos(ang_c), np.sin(ang_c)
    zero = np.zeros_like(sr)
    cos = np.concatenate([cr, cr, cc, cc], axis=-1)
    s_up = np.concatenate([-sr, zero, -sc, zero], axis=-1)
    s_dn = np.concatenate([zero, sr, zero, sc], axis=-1)
    return tuple(jnp.asarray(np.tile(a, (1, N_HEADS)), dtype=F32) for a in (cos, s_up, s_dn))


def _to_slab(kv):
    n, depth, length = kv.shape[:3]
    return jnp.transpose(kv, (0, 1, 3, 4, 2)).reshape(n, depth, KV_WIDTH, length)


def _from_slab(slab):
    n, _, length = slab.shape
    return jnp.transpose(slab.reshape(n, N_KV, HEAD_DIM, length), (0, 3, 1, 2))


def _block_ones(width):
    idx = np.arange(width) // HEAD_DIM
    return jnp.asarray((idx[:, None] == idx[None, :]).astype(np.float32), dtype=BF16)


def _same_steps(*counted_parts):
    steps = {n for n, _ in counted_parts}
    assert len(steps) == 1, steps
    return steps.pop(), [p for _, p in counted_parts]


def kernel(x_prompt, x_sample, cache_k, cache_v, state_hgrn, c, c_ctx, w_ada, b_ada, norm_mix_w, w_in, q_norm_w, k_norm_w, hgrn_lb_logits, hgrn_norm_w, w_hgrn_out, w_attn_out, w_out, norm_ffn_w, w_ff1, w_ff2, final_norm_w):
    n_p, l_p, _ = x_prompt.shape
    n_s, l_s, _ = x_sample.shape
    layer = 0

    mod = _modulation(c_ctx[None, :], c, w_ada[layer], b_ada[layer][None, :])
    mod_p = _Mod(mod, 0, True)
    mod_s = _Mod(mod, 1, False)

    in_small = (q_norm_w[layer][None, :], k_norm_w[layer][None, :],
                _block_ones(Q_WIDTH), _block_ones(KV_WIDTH), hgrn_lb_logits)
    nmw = norm_mix_w[layer][None, :]
    hnw = hgrn_norm_w[layer][None, :]
    xp2 = x_prompt.reshape(n_p * l_p, D_MODEL)
    xs2 = x_sample.reshape(n_s * l_s, D_MODEL)

    n_in, in_p = _in_part(MIX_TOKEN_TILE, xp2, mod_p, l_p, nmw, w_in[layer], *in_small, None)
    steps, parts = _same_steps((n_in, in_p), _cast_part(
        [w_in[layer], w_hgrn_out[layer], w_attn_out[layer], w_out[layer], w_ff1[layer], w_ff2[layer]], n_in))
    (zh_p, q_p, k_p, kt_p, vt_p, gates_p), (w_in_b, w_ho_b, w_ao_b, w_out_b, w_ff1_b, w_ff2_b) = _launch(
        "in_ctx", steps, parts)
    out_w = (w_ho_b, w_ao_b, w_out_b, norm_ffn_w[layer][None, :], w_ff1_b, w_ff2_b, final_norm_w[None, :])

    steps, parts = _same_steps(
        _attn_part(q_p, [(k_p, vt_p, None)], n_p, l_p, N_HEADS),
        _in_part(MIX_TOKEN_TILE, xs2, mod_s, l_s, nmw, w_in_b, *in_small, _rope_tables(l_s)),
        _hgrn_part(zh_p, hnw, n_p, l_p, HG_HEADS, None, True))
    (oa_p,), (zh_s, q_s, k_s, vt_s, gates_s), (oh_p, s_fin) = _launch("in_latent_mix_ctx", steps, parts,
                                                                      MIX_PACES_IN)

    segs = [(_to_slab(cache_k), _to_slab(cache_v), layer), (k_s, vt_s, None)]
    steps, parts = _same_steps(
        _out_part(MIX_TOKEN_TILE, xp2, mod_p, l_p, oh_p, oa_p, gates_p, *out_w),
        _attn_part(q_s, segs, n_s, l_s, ATTN_LOOKAHEAD),
        _hgrn_part(zh_s, hnw, n_s, l_s, 1, (state_hgrn, layer), False))
    (y_p,), (oa_s,), (oh_s,) = _launch("out_ctx_mix_latent", steps, parts, MIX_PACES_OUT)

    steps, parts = _same_steps(_out_part(SOLO_TOKEN_TILE, xs2, mod_s, l_s, oh_s, oa_s, gates_s, *out_w))
    ((y_s,),) = _launch("out_latent", steps, parts)

    new_k = _from_slab(kt_p)[:, None]
    new_v = _from_slab(vt_p)[:, None]
    new_s = s_fin.reshape(n_p, 1, 2, HG_HEADS, HG_DK, HG_DV)
    return (y_p.reshape(n_p, l_p, D_MODEL), y_s.reshape(n_s, l_s, D_MODEL), new_k, new_v, new_s)
```

```python
import functools
from typing import Any, Callable, NamedTuple

import numpy as np
import jax
import jax.numpy as jnp
from jax import lax
from jax.experimental import pallas as pl
from jax.experimental.pallas import tpu as pltpu

F32 = jnp.float32
BF16 = jnp.bfloat16

D_MODEL = 1024
GRID_W = 64
EPS = 1e-6
HG_HEADS = 4
HG_DK = 128
HG_DV = 128
HG_WIDTH = HG_HEADS * HG_DK
HG_CHUNK = 32
HG_PAIR = 2 * HG_CHUNK
N_HEADS = 8
N_KV = 2
HEAD_DIM = 64
Q_WIDTH = N_HEADS * HEAD_DIM
KV_WIDTH = N_KV * HEAD_DIM
ROPE_THETA = 10000.0
D_FF = 4 * D_MODEL
N_MOD = 6
ZH_WIDTH = 5 * HG_WIDTH
D_IN = ZH_WIDTH + Q_WIDTH + 2 * KV_WIDTH + 2 * D_MODEL

V7X_VMEM_LIMIT_BYTES = 56 * 1024 * 1024
SUBLANES = 8
LANES = 128
MXU_ROWS = 16
MXU_TILE = 256
ROPE_PAIR_LANES = HEAD_DIM // 4
HG_BLOCK = 256
COND_ROWS = 16
ONES_ROWS = 16
LOG2_E = float(np.log2(np.e))
MIX_TOKEN_TILE = 256
SOLO_TOKEN_TILE = 512
ADALN_ROW_TILE = 256
W_CAST_ROWS = 128
DOT_COLS = 512
ATTN_Q_TILE = 256
ATTN_LOOKAHEAD = 2
MIX_PACES_IN = (1.0, 1.0, 0.7)
MIX_PACES_OUT = (1.0, 1.0, 0.8)


def _dot(a, b):
    return jnp.dot(a, b, preferred_element_type=F32)


def _dot_nt(a, b):
    return lax.dot_general(a, b, (((1,), (1,)), ((), ())), preferred_element_type=F32)


def _dot_tn(a, b):
    return lax.dot_general(a, b, (((0,), (0,)), ((), ())), preferred_element_type=F32)


def _mm_cost(m, k, n):
    return (m // MXU_ROWS) * -(-k // MXU_TILE) * -(-n // MXU_TILE)


def _dot_by_cols(a, w_ref, rows, cols):
    pieces = []
    for c in range(cols[0], cols[1], DOT_COLS):
        end = min(c + DOT_COLS, cols[1])
        pieces.append(_dot(a, w_ref[rows[0]:rows[1], c:end]))
        yield _mm_cost(a.shape[0], rows[1] - rows[0], end - c)
    return pieces[0] if len(pieces) == 1 else jnp.concatenate(pieces, axis=1)


def _split_bf16(x):
    hi = x.astype(BF16)
    lo = (x - hi.astype(F32)).astype(BF16)
    return hi, lo


def _const_spec(shape):
    nd = len(shape)
    return pl.BlockSpec(shape, lambda *_: (0,) * nd, pipeline_mode=pl.Buffered(1))


class _Part(NamedTuple):
    stages: Callable[..., Any]
    args: list
    in_specs: list
    out_specs: list
    out_shape: list
    scratch: list
    cost: int


def _interleave(gens, totals):
    done = [0] * len(gens)
    live = list(range(len(gens)))
    while live:
        i = min(live, key=lambda j: done[j] / totals[j])
        try:
            done[i] += next(gens[i])
        except StopIteration:
            live.remove(i)


def _launch(name, n_steps, parts, paces=None):
    paces = paces or [1.0] * len(parts)
    n_in = [len(p.args) for p in parts]
    n_out = [len(p.out_shape) for p in parts]
    n_scr = [len(p.scratch) for p in parts]

    def body(*refs):
        refs = list(refs)
        ins = [[refs.pop(0) for _ in range(n)] for n in n_in]
        outs = [[refs.pop(0) for _ in range(n)] for n in n_out]
        scrs = [[refs.pop(0) for _ in range(n)] for n in n_scr]
        _interleave([p.stages(i, o, s) for p, i, o, s in zip(parts, ins, outs, scrs)],
                    [p.cost * pace for p, pace in zip(parts, paces)])

    flat = pl.pallas_call(
        body,
        grid=(n_steps,),
        in_specs=[s for p in parts for s in p.in_specs],
        out_specs=[s for p in parts for s in p.out_specs],
        out_shape=[s for p in parts for s in p.out_shape],
        scratch_shapes=[s for p in parts for s in p.scratch],
        compiler_params=pltpu.CompilerParams(dimension_semantics=("arbitrary",),
                                             vmem_limit_bytes=V7X_VMEM_LIMIT_BYTES),
        name=name,
    )(*[a for p in parts for a in p.args])
    flat = list(flat)
    return [[flat.pop(0) for _ in range(n)] for n in n_out]


class _Mod(NamedTuple):
    rows: Any
    first: int
    shared: bool


def _mod_spec(mod, tile, seq_len, tile_of_step=lambda i: i):
    if mod.shared:
        return pl.BlockSpec((1, 1, N_MOD * D_MODEL), lambda i: (mod.first, 0, 0))
    assert seq_len % tile == 0
    per_seq = seq_len // tile
    return pl.BlockSpec((1, 1, N_MOD * D_MODEL), lambda i: (mod.first + tile_of_step(i) // per_seq, 0, 0))


def _mod_kernel(cctx_ref, c_ref, w_ref, b_ref, o_ref, cond_scr):
    n_req = c_ref.shape[0]
    cond_scr[...] = jnp.zeros_like(cond_scr)
    cond_scr[0:1, :] = cctx_ref[...]
    cond_scr[1:1 + n_req, :] = c_ref[...]
    c = cond_scr[...]
    x = c * jax.nn.sigmoid(c)
    x_hi, x_lo = _split_bf16(x)
    w = w_ref[...].astype(BF16)
    part = _dot(x_hi, w) + _dot(x_lo, w)

    @pl.when(pl.program_id(0) == 0)
    def _():
        o_ref[:, 0, :] = b_ref[...] + part

    @pl.when(pl.program_id(0) > 0)
    def _():
        o_ref[:, 0, :] += part


def _modulation(c_ctx, c, w_ada, b_ada):
    n = N_MOD * D_MODEL
    tk = ADALN_ROW_TILE
    n_req = c.shape[0]
    assert 1 + n_req <= COND_ROWS
    return pl.pallas_call(
        _mod_kernel,
        grid=(D_MODEL // tk,),
        in_specs=[pl.BlockSpec((1, tk), lambda j: (0, j)),
                  pl.BlockSpec((n_req, tk), lambda j: (0, j)),
                  pl.BlockSpec((tk, n), lambda j: (j, 0)),
                  pl.BlockSpec((1, n), lambda j: (0, 0))],
        out_specs=pl.BlockSpec((COND_ROWS, 1, n), lambda j: (0, 0, 0)),
        out_shape=jax.ShapeDtypeStruct((COND_ROWS, 1, n), F32),
        scratch_shapes=[pltpu.VMEM((COND_ROWS, tk), F32)],
        compiler_params=pltpu.CompilerParams(dimension_semantics=("arbitrary",),
                                             vmem_limit_bytes=V7X_VMEM_LIMIT_BYTES),
        name="adaln_mod",
    )(c_ctx, c, w_ada, b_ada)


def _rms_rows(x, w):
    return x * lax.rsqrt(jnp.mean(x * x, axis=-1, keepdims=True) + EPS) * w


def _head_sumsq(a, ones_ref):
    return _dot((a * a).astype(BF16), ones_ref[...])


def _head_rms(a, sumsq, w):
    return a * lax.rsqrt(sumsq * (1.0 / HEAD_DIM) + EPS) * w


def _forget_lower_bound(logits):
    e = jnp.exp(logits - jnp.max(logits, axis=1, keepdims=True))
    return e[:, 0, :] / jnp.sum(e, axis=1)


def _rope(x, cos, s_up, s_dn):
    cols = []
    for j in range(x.shape[1] // LANES):
        sl = slice(j * LANES, (j + 1) * LANES)
        xj = x[:, sl]
        cols.append(xj * cos[:, sl] + pltpu.roll(xj, LANES - ROPE_PAIR_LANES, 1) * s_up[:, sl]
                    + pltpu.roll(xj, ROPE_PAIR_LANES, 1) * s_dn[:, sl])
    return jnp.concatenate(cols, axis=1) if len(cols) > 1 else cols[0]


def _in_stages(latent, ins, outs, scrs):
    x_ref, mod_ref, nw_ref, w_ref, qw_ref, kw_ref, oq_ref, ok_ref, lbl_ref = ins[:9]
    if scrs:
        (w_bf_ref,) = scrs
        rows = W_CAST_ROWS

        @pl.when(pl.program_id(0) == 0)
        def _():
            def cast_rows(r, carry):
                sl = pl.ds(pl.multiple_of(r * rows, rows), rows)
                w_bf_ref[sl, :] = w_ref[sl, :].astype(BF16)
                return carry

            lax.fori_loop(0, w_ref.shape[0] // rows, cast_rows, 0)

        w_ref = w_bf_ref
    if latent:
        cos_ref, sup_ref, sdn_ref = ins[9:]
        zh_ref, q_ref, k_ref, vt_ref, g_ref = outs
    else:
        zh_ref, q_ref, k_ref, kt_ref, vt_ref, g_ref = outs
    tm = x_ref.shape[0]
    m = mod_ref[0]
    sh1 = m[:, 0:D_MODEL]
    sc1 = m[:, D_MODEL:2 * D_MODEL]
    h = _rms_rows(x_ref[...], nw_ref[...]) * (1.0 + sc1) + sh1
    hb = h.astype(BF16)
    c0 = ZH_WIDTH
    c1 = c0 + Q_WIDTH
    c2 = c1 + KV_WIDTH
    c3 = c2 + KV_WIDTH
    aq = _dot(hb, w_ref[:, c0:c1])
    ak = _dot(hb, w_ref[:, c1:c2])
    av = _dot(hb, w_ref[:, c2:c3])
    yield _mm_cost(tm, D_MODEL, c3 - c0)
    gl = yield from _dot_by_cols(hb, w_ref, (0, D_MODEL), (c3, D_IN))
    ssq = _head_sumsq(aq, oq_ref)
    ssk = _head_sumsq(ak, ok_ref)
    yield _mm_cost(tm, Q_WIDTH, Q_WIDTH) + _mm_cost(tm, KV_WIDTH, KV_WIDTH)
    lb = _forget_lower_bound(lbl_ref[...])

    def hgrn_segment(s):
        z = yield from _dot_by_cols(hb, w_ref, (0, D_MODEL), (s * HG_WIDTH, (s + 1) * HG_WIDTH))
        if s == 0:
            z = z * HG_DK ** -0.5
        elif s in (1, 2):
            lbv = lb[s - 1:s, :]
            z = jnp.log(lbv + (1.0 - lbv) * jax.nn.sigmoid(z))
        elif s == 4:
            z = z * jax.nn.sigmoid(z)
        zh_ref[:, s * HG_WIDTH:(s + 1) * HG_WIDTH] = z

    yield from hgrn_segment(1)
    yield from hgrn_segment(2)
    g_ref[...] = jax.nn.sigmoid(gl).astype(BF16)
    qn = _head_rms(aq, ssq, jnp.tile(qw_ref[...], (1, N_HEADS)))
    kn = _head_rms(ak, ssk, jnp.tile(kw_ref[...], (1, N_KV)))
    seq = vt_ref.shape[2]
    avt = av.T
    for s in range(vt_ref.shape[0]):
        vt_ref[s] = avt[:, s * seq:(s + 1) * seq].astype(vt_ref.dtype)
    for s in (4, 0, 3):
        yield from hgrn_segment(s)
    if latent:
        cos, sup, sdn = cos_ref[...], sup_ref[...], sdn_ref[...]
        qn = _rope(qn, cos, sup, sdn)
        kn = _rope(kn, cos[:, 0:KV_WIDTH], sup[:, 0:KV_WIDTH], sdn[:, 0:KV_WIDTH])
    else:
        knt = kn.T
        for s in range(kt_ref.shape[0]):
            kt_ref[s] = knt[:, s * seq:(s + 1) * seq]
    q_ref[...] = (qn * (HEAD_DIM ** -0.5 * LOG2_E)).astype(BF16)
    k_ref[...] = kn.astype(BF16)


def _in_part(tm, x2, mod, seq_len, norm_w, w_in, qw, kw, ones_q, ones_k, lb_logits, rope):
    t = x2.shape[0]
    latent = rope is not None
    per_seq = max(seq_len // tm, 1)
    row = lambda i: (i, 0)
    in_specs = [pl.BlockSpec((tm, D_MODEL), row),
                _mod_spec(mod, tm, seq_len),
                _const_spec((1, D_MODEL)),
                _const_spec((D_MODEL, D_IN)),
                _const_spec((1, HEAD_DIM)),
                _const_spec((1, HEAD_DIM)),
                _const_spec((Q_WIDTH, Q_WIDTH)),
                _const_spec((KV_WIDTH, KV_WIDTH)),
                _const_spec(lb_logits.shape)]
    args = [x2, mod.rows, norm_w, w_in, qw, kw, ones_q, ones_k, lb_logits]
    out_specs = [pl.BlockSpec((tm, ZH_WIDTH), row), pl.BlockSpec((tm, Q_WIDTH), row),
                 pl.BlockSpec((tm, KV_WIDTH), row)]
    out_shape = [jax.ShapeDtypeStruct((t, ZH_WIDTH), F32), jax.ShapeDtypeStruct((t, Q_WIDTH), BF16),
                 jax.ShapeDtypeStruct((t, KV_WIDTH), BF16)]
    if seq_len <= tm:
        assert tm % seq_len == 0
        t_spec = pl.BlockSpec((tm // seq_len, KV_WIDTH, seq_len), lambda i: (i, 0, 0))
    else:
        t_spec = pl.BlockSpec((1, KV_WIDTH, tm), lambda i: (i // per_seq, 0, i % per_seq))
    if latent:
        in_specs += [pl.BlockSpec((tm, Q_WIDTH), lambda i: (i % per_seq, 0))] * 3
        args += list(rope)
        out_specs.append(t_spec)
        out_shape.append(jax.ShapeDtypeStruct((t // seq_len, KV_WIDTH, seq_len), BF16))
    else:
        out_specs += [t_spec, t_spec]
        out_shape += [jax.ShapeDtypeStruct((t // seq_len, KV_WIDTH, seq_len), F32)] * 2
    out_specs.append(pl.BlockSpec((tm, 2 * D_MODEL), row))
    out_shape.append(jax.ShapeDtypeStruct((t, 2 * D_MODEL), BF16))
    cost = (_mm_cost(tm, D_MODEL, D_IN) + _mm_cost(tm, Q_WIDTH, Q_WIDTH) + _mm_cost(tm, KV_WIDTH, KV_WIDTH))
    scratch = [] if w_in.dtype == BF16 else [pltpu.VMEM((D_MODEL, D_IN), BF16)]
    return t // tm, _Part(functools.partial(_in_stages, latent), args, in_specs, out_specs, out_shape, scratch, cost)


def _attn_cost(lq, lks):
    per_head = sum(_mm_cost(lk, HEAD_DIM, lq) + _mm_cost(HEAD_DIM + ONES_ROWS, lk, lq) for lk in lks)
    return N_HEADS * per_head


def _attn_stages(k_transposed, lookahead, ins, outs, _):
    n_seg = len(k_transposed)
    q_ref, kv_refs = ins[0], ins[1:]
    (o_ref,) = outs
    q = q_ref[...]
    lq = q.shape[0]
    ks = [(kv_refs[2 * s][0].T if k_transposed[s] else kv_refs[2 * s][...]).astype(BF16) for s in range(n_seg)]
    vts = [kv_refs[2 * s + 1][0].astype(BF16) for s in range(n_seg)]
    vts = [[jnp.concatenate([vt[g * HEAD_DIM:(g + 1) * HEAD_DIM, :],
                             jnp.ones((ONES_ROWS, vt.shape[1]), BF16)], axis=0) for g in range(N_KV)]
           for vt in vts]
    score_cost = sum(_mm_cost(k.shape[0], HEAD_DIM, lq) for k in ks)
    value_cost = sum(_mm_cost(HEAD_DIM + ONES_ROWS, k.shape[0], lq) for k in ks)

    def scores(h):
        g = h // (N_HEADS // N_KV)
        qh = q[:, h * HEAD_DIM:(h + 1) * HEAD_DIM]
        return [_dot_nt(k[:, g * HEAD_DIM:(g + 1) * HEAD_DIM], qh) for k in ks]

    pending = []
    for h in range(lookahead):
        pending.append(scores(h))
        yield score_cost
    outs_t = []
    for h in range(N_HEADS):
        if h + lookahead < N_HEADS:
            pending.append(scores(h + lookahead))
            yield score_cost
        st = pending.pop(0)
        g = h // (N_HEADS // N_KV)
        m = functools.reduce(jnp.maximum, [jnp.max(s, axis=0, keepdims=True) for s in st])
        ot = functools.reduce(jnp.add, [_dot(vt[g], jnp.exp2(s - m).astype(BF16)) for vt, s in zip(vts, st)])
        outs_t.append(ot[0:HEAD_DIM, :] / ot[HEAD_DIM:HEAD_DIM + 1, :])
        if h == N_HEADS - 1:
            o_ref[...] = jnp.concatenate(outs_t, axis=0).T.astype(BF16)
        yield value_cost


def _attn_part(q, kv_segments, n_batch, seq_len, lookahead):
    t = q.shape[0]
    tq = ATTN_Q_TILE
    per_seq = seq_len // tq
    in_specs = [pl.BlockSpec((tq, Q_WIDTH), lambda i: (i, 0))]
    args = [q]
    for k, v_t, layer in kv_segments:
        lk = v_t.shape[-1]
        if v_t.ndim == 4:
            slab = pl.BlockSpec((1, None, KV_WIDTH, lk), lambda i, layer=layer: (i // per_seq, layer, 0, 0))
        else:
            slab = pl.BlockSpec((1, KV_WIDTH, lk), lambda i: (i // per_seq, 0, 0))
        k_spec = slab if k.ndim == v_t.ndim else pl.BlockSpec((lk, KV_WIDTH), lambda i: (i // per_seq, 0))
        in_specs += [k_spec, slab]
        args += [k, v_t]
    stages = functools.partial(_attn_stages, tuple(k.ndim == v_t.ndim for k, v_t, _ in kv_segments), lookahead)
    return n_batch * per_seq, _Part(stages, args, in_specs, [pl.BlockSpec((tq, Q_WIDTH), lambda i: (i, 0))],
                                    [jax.ShapeDtypeStruct((t, Q_WIDTH), BF16)], [],
                                    _attn_cost(tq, [v_t.shape[-1] for _, v_t, _ in kv_segments]))


def _hgrn_unit_cost():
    blk = HG_BLOCK
    return (2 * _mm_cost(blk, blk, 2 * HG_DK)
            + 2 * _mm_cost(blk, HG_DK, 2 * blk) + _mm_cost(blk, blk, HG_DV)
            + (blk // HG_PAIR) * (_mm_cost(HG_DV, HG_PAIR, 2 * HG_DK) + _mm_cost(HG_PAIR, 2 * HG_DK, HG_DV)))


def _hgrn_stages(seq_len, hps, has_s0, has_sfin, ins, outs, scrs):
    tf_ref, tb_ref, xf_ref, xb_ref, q_ref, lf_ref, lb_ref, v_ref, g_ref, nw_ref = ins[:10]
    s0_ref = ins[10] if has_s0 else None
    o_ref = outs[0]
    sfin_ref = outs[1] if has_sfin else None
    kv_scr, ss_scr, qd_scr, oi_scr, dec_scr = scrs
    n_blk = seq_len // HG_BLOCK
    n_pair = seq_len // HG_PAIR
    per_blk = HG_BLOCK // HG_CHUNK
    pairs_per_blk = HG_BLOCK // HG_PAIR
    one_row = jnp.ones((1, HG_DK), F32)

    def chunk_cumsum(t_ref, x):
        hi, lo = _split_bf16(x)
        r = _dot(t_ref[...], jnp.concatenate([hi, lo], axis=1))
        return r[:, 0:HG_DK] + r[:, HG_DK:2 * HG_DK]

    def chunk_edge_rows(x, first):
        off = 0 if first else HG_CHUNK - 1
        return [x[c * HG_CHUNK + off:c * HG_CHUNK + off + 1, :] for c in range(per_blk)]

    def spread_rows(rows):
        return jnp.concatenate([jnp.broadcast_to(r, (HG_CHUNK, HG_DK)) for r in rows], axis=0)

    units = [(blk, h) for blk in range(n_blk) for h in range(hps)]
    stage1 = []
    for blk, h in units:
        rows = slice(blk * HG_BLOCK, (blk + 1) * HG_BLOCK)
        cols = slice(h * HG_DK, (h + 1) * HG_DK)
        logf_f = lf_ref[rows, cols]
        logf_b = lb_ref[rows, cols]
        k_f = 1.0 - jnp.exp(logf_f)
        k_b = 1.0 - jnp.exp(logf_b)
        a_f = chunk_cumsum(tf_ref, logf_f)
        a_b = chunk_cumsum(tb_ref, logf_b)
        stage1.append((rows, cols, k_f, k_b, a_f, a_b))
        yield 2 * _mm_cost(HG_BLOCK, HG_BLOCK, 2 * HG_DK)
    stage2 = []
    for (blk, h), (rows, cols, k_f, k_b, a_f, a_b) in zip(units, stage1):
        q = q_ref[rows, cols]
        vb = v_ref[rows, cols].astype(BF16)
        ea_f = jnp.exp(a_f)
        ea_b = jnp.exp(a_b)
        qd_f = q * ea_f
        qd_b = q * ea_b
        kd_f = k_f * jnp.exp(-a_f)
        kd_b = k_b * jnp.exp(-a_b)
        dec_f = chunk_edge_rows(ea_f, first=False)
        dec_b = chunk_edge_rows(ea_b, first=True)
        ke_f = kd_f * spread_rows(dec_f)
        ke_b = kd_b * spread_rows(dec_b)
        even = [c % 2 == 0 for c in range(per_blk)]
        qd_pair = jnp.concatenate(
            [qd_f * spread_rows([one_row if even[c] else dec_f[c - 1] for c in range(per_blk)]),
             qd_b * spread_rows([dec_b[c + 1] if even[c] else one_row for c in range(per_blk)])], axis=1)
        ke_pair = jnp.concatenate(
            [ke_f * spread_rows([dec_f[c + 1] if even[c] else one_row for c in range(per_blk)]),
             ke_b * spread_rows([one_row if even[c] else dec_b[c - 1] for c in range(per_blk)])], axis=1).astype(BF16)
        sc_f = _dot_nt(qd_f.astype(BF16), jnp.concatenate([kd_f, ke_f], axis=0).astype(BF16))
        sc_b = _dot_nt(qd_b.astype(BF16), jnp.concatenate([kd_b, ke_b], axis=0).astype(BF16))
        for p in range(pairs_per_blk):
            pr = slice(p * HG_PAIR, (p + 1) * HG_PAIR)
            kv_scr[h, blk * pairs_per_blk + p] = _dot_tn(vb[pr, :], ke_pair[pr, :])
            dec_scr[h, blk * pairs_per_blk + p, 0:1, :] = jnp.concatenate(
                [dec_f[2 * p] * dec_f[2 * p + 1], dec_b[2 * p] * dec_b[2 * p + 1]], axis=1)
        qd_scr[h, rows, :] = qd_pair.astype(BF16)
        stage2.append((rows, cols, vb, sc_f, sc_b))
        yield 2 * _mm_cost(HG_BLOCK, HG_DK, 2 * HG_BLOCK) + pairs_per_blk * _mm_cost(HG_DV, HG_PAIR, 2 * HG_DK)
    for rows, cols, vb, sc_f, sc_b in stage2:
        s = (sc_f[:, 0:HG_BLOCK].astype(BF16) * tf_ref[...] + sc_f[:, HG_BLOCK:].astype(BF16) * xf_ref[...]
             + sc_b[:, 0:HG_BLOCK].astype(BF16) * tb_ref[...] + sc_b[:, HG_BLOCK:].astype(BF16) * xb_ref[...])
        oi_scr[rows, cols] = _dot(s, vb)
        yield _mm_cost(HG_BLOCK, HG_BLOCK, HG_DV)

    for h in range(hps):
        if has_s0:
            s_f, s_b = s0_ref[0, 0, h].T, s0_ref[0, 1, h].T
        else:
            s_f = s_b = jnp.zeros((HG_DV, HG_DK), F32)
        for pf in range(n_pair):
            pb = n_pair - 1 - pf
            ss_scr[h, pf, :, 0:HG_DK] = s_f.astype(BF16)
            s_f = dec_scr[h, pf, 0:1, 0:HG_DK] * s_f + kv_scr[h, pf, :, 0:HG_DK]
            ss_scr[h, pb, :, HG_DK:2 * HG_DK] = s_b.astype(BF16)
            s_b = dec_scr[h, pb, 0:1, HG_DK:2 * HG_DK] * s_b + kv_scr[h, pb, :, HG_DK:2 * HG_DK]
        if has_sfin:
            sfin_ref[0, 0, h] = s_f.T
            sfin_ref[0, 1, h] = s_b.T

    for blk, h in units:
        cols = slice(h * HG_DK, (h + 1) * HG_DK)
        for p in range(pairs_per_blk):
            pi = blk * pairs_per_blk + p
            rows = slice(pi * HG_PAIR, (pi + 1) * HG_PAIR)
            oi_scr[rows, cols] = oi_scr[rows, cols] + _dot_nt(qd_scr[h, rows, :], ss_scr[h, pi])
        yield pairs_per_blk * _mm_cost(HG_PAIR, 2 * HG_DK, HG_DV)

    for h in range(hps):
        cols = slice(h * HG_DK, (h + 1) * HG_DK)
        o = _rms_rows(oi_scr[:, cols], nw_ref[...])
        o_ref[:, cols] = (o * g_ref[:, cols]).astype(BF16)


def _chunk_masks():
    r = np.arange(HG_BLOCK)
    same = (r[:, None] // HG_CHUNK) == (r[None, :] // HG_CHUNK)
    lower = same & (r[None, :] <= r[:, None])
    upper = same & (r[None, :] >= r[:, None])
    same_pair = (r[:, None] // HG_PAIR) == (r[None, :] // HG_PAIR)
    cross_f = same_pair & (r[:, None] // HG_CHUNK == r[None, :] // HG_CHUNK + 1)
    return tuple(jnp.asarray(m.astype(np.float32), dtype=BF16) for m in (lower, upper, cross_f, cross_f.T))


def _hgrn_part(zh, norm_w, n_batch, seq_len, hps, s0, want_final):
    t = zh.shape[0]
    n_pair = seq_len // HG_PAIR
    width = hps * HG_DK
    n_hb = HG_HEADS // hps
    masks = _chunk_masks()

    def seg_spec(seg):
        return pl.BlockSpec((seq_len, width), lambda i: (i // n_hb, seg * n_hb + i % n_hb))

    state_spec = pl.BlockSpec((1, 2, hps, HG_DK, HG_DV), lambda i: (i // n_hb, 0, i % n_hb, 0, 0))
    mask_spec = pl.BlockSpec((HG_BLOCK, HG_BLOCK), lambda i: (0, 0))
    in_specs = [mask_spec] * len(masks) + [seg_spec(s) for s in range(5)] + [pl.BlockSpec((1, HG_DV), lambda i: (0, 0))]
    args = list(masks) + [zh] * 5 + [norm_w]
    if s0 is not None:
        states, layer = s0
        in_specs.append(pl.BlockSpec((1, None, 2, hps, HG_DK, HG_DV),
                                     lambda i: (i // n_hb, layer, 0, i % n_hb, 0, 0)))
        args.append(states)
    out_specs = [pl.BlockSpec((seq_len, width), lambda i: (i // n_hb, i % n_hb))]
    out_shape = [jax.ShapeDtypeStruct((t, HG_WIDTH), BF16)]
    if want_final:
        out_specs.append(state_spec)
        out_shape.append(jax.ShapeDtypeStruct((n_batch, 2, HG_HEADS, HG_DK, HG_DV), F32))
    scratch = [pltpu.VMEM((hps, n_pair, HG_DV, 2 * HG_DK), F32),
               pltpu.VMEM((hps, n_pair, HG_DV, 2 * HG_DK), BF16),
               pltpu.VMEM((hps, seq_len, 2 * HG_DK), BF16),
               pltpu.VMEM((seq_len, width), F32),
               pltpu.VMEM((hps, n_pair, SUBLANES, 2 * HG_DK), F32)]
    stages = functools.partial(_hgrn_stages, seq_len, hps, s0 is not None, want_final)
    cost = hps * (seq_len // HG_BLOCK) * _hgrn_unit_cost()
    return n_batch * n_hb, _Part(stages, args, in_specs, out_specs, out_shape, scratch, cost)


def _out_stages(ins, outs, _):
    x_ref, mod_ref, oh_ref, oa_ref, g_ref, who_ref, wao_ref, wout_ref, nfw_ref, wff1_ref, wff2_ref, fnw_ref = ins
    (y_ref,) = outs
    tm = x_ref.shape[0]
    m = mod_ref[0]
    g1 = m[:, 2 * D_MODEL:3 * D_MODEL]
    sh2 = m[:, 3 * D_MODEL:4 * D_MODEL]
    sc2 = m[:, 4 * D_MODEL:5 * D_MODEL]
    g2 = m[:, 5 * D_MODEL:6 * D_MODEL]
    gates = g_ref[...].astype(F32)
    from_h = yield from _dot_by_cols(oh_ref[...], who_ref, (0, HG_WIDTH), (0, D_MODEL))
    from_a = yield from _dot_by_cols(oa_ref[...], wao_ref, (0, Q_WIDTH), (0, D_MODEL))
    merged = gates[:, 0:D_MODEL] * from_h + gates[:, D_MODEL:2 * D_MODEL] * from_a
    mixed = yield from _dot_by_cols(merged.astype(BF16), wout_ref, (0, D_MODEL), (0, D_MODEL))
    x1 = x_ref[...] + g1 * mixed
    h2 = (_rms_rows(x1, nfw_ref[...]) * (1.0 + sc2) + sh2).astype(BF16)
    acc = jnp.zeros_like(x1)
    for j in range(D_FF // D_MODEL):
        cols = slice(j * D_MODEL, (j + 1) * D_MODEL)
        hj = yield from _dot_by_cols(h2, wff1_ref, (0, D_MODEL), (cols.start, cols.stop))
        hj = jnp.maximum(hj, 0.0)
        acc = acc + (yield from _dot_by_cols((hj * hj).astype(BF16), wff2_ref, (cols.start, cols.stop), (0, D_MODEL)))
    y_ref[...] = _rms_rows(x1 + g2 * acc, fnw_ref[...])


def _out_part(tm, x2, mod, seq_len, oh, oa, gates, w_ho, w_ao, w_out, nfw, w_ff1, w_ff2, fnw):
    t = x2.shape[0]
    row = lambda i: (i, 0)
    in_specs = [pl.BlockSpec((tm, D_MODEL), row),
                _mod_spec(mod, tm, seq_len),
                pl.BlockSpec((tm, HG_WIDTH), row),
                pl.BlockSpec((tm, Q_WIDTH), row),
                pl.BlockSpec((tm, 2 * D_MODEL), row),
                _const_spec((HG_WIDTH, D_MODEL)),
                _const_spec((Q_WIDTH, D_MODEL)),
                _const_spec((D_MODEL, D_MODEL)),
                _const_spec((1, D_MODEL)),
                _const_spec((D_MODEL, D_FF)),
                _const_spec((D_FF, D_MODEL)),
                _const_spec((1, D_MODEL))]
    args = [x2, mod.rows, oh, oa, gates, w_ho, w_ao, w_out, nfw, w_ff1, w_ff2, fnw]
    cost = _mm_cost(tm, HG_WIDTH + Q_WIDTH, D_MODEL) + _mm_cost(tm, D_MODEL, D_MODEL) + 2 * _mm_cost(tm, D_MODEL, D_FF)
    return t // tm, _Part(_out_stages, args, in_specs, [pl.BlockSpec((tm, D_MODEL), row)],
                          [jax.ShapeDtypeStruct((t, D_MODEL), F32)], [], cost)


def _cast_stages(ins, outs, _):
    for src, dst in zip(ins, outs):
        dst[...] = src[...].astype(dst.dtype)
    yield 1


def _cast_part(arrays, n_steps):
    in_specs, out_specs, out_shape = [], [], []
    for a in arrays:
        rows, cols = a.shape
        assert rows % (n_steps * MXU_ROWS) == 0
        spec = pl.BlockSpec((rows // n_steps, cols), lambda i: (i, 0))
        in_specs.append(spec)
        out_specs.append(spec)
        out_shape.append(jax.ShapeDtypeStruct(a.shape, BF16))
    return n_steps, _Part(_cast_stages, list(arrays), in_specs, out_specs, out_shape, [], 1)


def _rope_tables(n_tokens):
    rows = n_tokens // GRID_W
    row = np.repeat(np.arange(rows, dtype=np.float32), GRID_W)
    col = np.tile(np.arange(GRID_W, dtype=np.float32), rows)
    axis_dim = HEAD_DIM // 2
    freqs = (ROPE_THETA ** (-np.arange(0, axis_dim, 2, dtype=np.float32) / axis_dim)).astype(np.float32)
    ang_r = row[:, None] * freqs
    ang_c = col[:, None] * freqs
    cr, sr, cc, sc = np.cos(ang_r), np.sin(ang_r), np.cos(ang_c), np.sin(ang_c)
    zero = np.zeros_like(sr)
    cos = np.concatenate([cr, cr, cc, cc], axis=-1)
    s_up = np.concatenate([-sr, zero, -sc, zero], axis=-1)
    s_dn = np.concatenate([zero, sr, zero, sc], axis=-1)
    return tuple(jnp.asarray(np.tile(a, (1, N_HEADS)), dtype=F32) for a in (cos, s_up, s_dn))


def _to_slab(kv):
    n, depth, length = kv.shape[:3]
    return jnp.transpose(kv, (0, 1, 3, 4, 2)).reshape(n, depth, KV_WIDTH, length)


def _from_slab(slab):
    n, _, length = slab.shape
    return jnp.transpose(slab.reshape(n, N_KV, HEAD_DIM, length), (0, 3, 1, 2))


def _block_ones(width):
    idx = np.arange(width) // HEAD_DIM
    return jnp.asarray((idx[:, None] == idx[None, :]).astype(np.float32), dtype=BF16)


def _same_steps(*counted_parts):
    steps = {n for n, _ in counted_parts}
    assert len(steps) == 1, steps
    return steps.pop(), [p for _, p in counted_parts]


def kernel(x_prompt, x_sample, cache_k, cache_v, state_hgrn, c, c_ctx, w_ada, b_ada, norm_mix_w, w_in, q_norm_w, k_norm_w, hgrn_lb_logits, hgrn_norm_w, w_hgrn_out, w_attn_out, w_out, norm_ffn_w, w_ff1, w_ff2, final_norm_w):
    n_p, l_p, _ = x_prompt.shape
    n_s, l_s, _ = x_sample.shape
    layer = 0

    mod = _modulation(c_ctx[None, :], c, w_ada[layer], b_ada[layer][None, :])
    mod_p = _Mod(mod, 0, True)
    mod_s = _Mod(mod, 1, False)

    in_small = (q_norm_w[layer][None, :], k_norm_w[layer][None, :],
                _block_ones(Q_WIDTH), _block_ones(KV_WIDTH), hgrn_lb_logits)
    nmw = norm_mix_w[layer][None, :]
    hnw = hgrn_norm_w[layer][None, :]
    xp2 = x_prompt.reshape(n_p * l_p, D_MODEL)
    xs2 = x_sample.reshape(n_s * l_s, D_MODEL)

    n_in, in_p = _in_part(MIX_TOKEN_TILE, xp2, mod_p, l_p, nmw, w_in[layer], *in_small, None)
    steps, parts = _same_steps((n_in, in_p), _cast_part(
        [w_in[layer], w_hgrn_out[layer], w_attn_out[layer], w_out[layer], w_ff1[layer], w_ff2[layer]], n_in))
    (zh_p, q_p, k_p, kt_p, vt_p, gates_p), (w_in_b, w_ho_b, w_ao_b, w_out_b, w_ff1_b, w_ff2_b) = _launch(
        "in_ctx", steps, parts)
    out_w = (w_ho_b, w_ao_b, w_out_b, norm_ffn_w[layer][None, :], w_ff1_b, w_ff2_b, final_norm_w[None, :])

    steps, parts = _same_steps(
        _attn_part(q_p, [(k_p, vt_p, None)], n_p, l_p, N_HEADS),
        _in_part(MIX_TOKEN_TILE, xs2, mod_s, l_s, nmw, w_in_b, *in_small, _rope_tables(l_s)),
        _hgrn_part(zh_p, hnw, n_p, l_p, HG_HEADS, None, True))
    (oa_p,), (zh_s, q_s, k_s, vt_s, gates_s), (oh_p, s_fin) = _launch("in_latent_mix_ctx", steps, parts,
                                                                      MIX_PACES_IN)

    segs = [(_to_slab(cache_k), _to_slab(cache_v), layer), (k_s, vt_s, None)]
    steps, parts = _same_steps(
        _out_part(MIX_TOKEN_TILE, xp2, mod_p, l_p, oh_p, oa_p, gates_p, *out_w),
        _attn_part(q_s, segs, n_s, l_s, ATTN_LOOKAHEAD),
        _hgrn_part(zh_s, hnw, n_s, l_s, 1, (state_hgrn, layer), False))
    (y_p,), (oa_s,), (oh_s,) = _launch("out_ctx_mix_latent", steps, parts, MIX_PACES_OUT)

    steps, parts = _same_steps(_out_part(SOLO_TOKEN_TILE, xs2, mod_s, l_s, oh_s, oa_s, gates_s, *out_w))
    ((y_s,),) = _launch("out_latent", steps, parts)

    new_k = _from_slab(kt_p)[:, None]
    new_v = _from_slab(vt_p)[:, None]
    new_s = s_fin.reshape(n_p, 1, 2, HG_HEADS, HG_DK, HG_DV)
    return (y_p.reshape(n_p, l_p, D_MODEL), y_s.reshape(n_s, l_s, D_MODEL), new_k, new_v, new_s)
```

```python
import functools
from typing import Any, Callable, NamedTuple

import numpy as np
import jax
import jax.numpy as jnp
from jax import lax
from jax.experimental import pallas as pl
from jax.experimental.pallas import tpu as pltpu

F32 = jnp.float32
BF16 = jnp.bfloat16

D_MODEL = 1024
GRID_W = 64
EPS = 1e-6
HG_HEADS = 4
HG_DK = 128
HG_DV = 128
HG_WIDTH = HG_HEADS * HG_DK
HG_CHUNK = 32
HG_PAIR = 2 * HG_CHUNK
N_HEADS = 8
N_KV = 2
HEAD_DIM = 64
Q_WIDTH = N_HEADS * HEAD_DIM
KV_WIDTH = N_KV * HEAD_DIM
ROPE_THETA = 10000.0
D_FF = 4 * D_MODEL
N_MOD = 6
ZH_WIDTH = 5 * HG_WIDTH
D_IN = ZH_WIDTH + Q_WIDTH + 2 * KV_WIDTH + 2 * D_MODEL

V7X_VMEM_LIMIT_BYTES = 56 * 1024 * 1024
SUBLANES = 8
LANES = 128
MXU_ROWS = 16
MXU_TILE = 256
ROPE_PAIR_LANES = HEAD_DIM // 4
HG_BLOCK = 256
COND_ROWS = 16
ONES_ROWS = 16
LOG2_E = float(np.log2(np.e))
MIX_TOKEN_TILE = 256
SOLO_TOKEN_TILE = 512
ADALN_ROW_TILE = 256
W_CAST_ROWS = 128
DOT_COLS = 512
ATTN_Q_TILE = 256
ATTN_LOOKAHEAD = 2
MIX_PACES_IN = (1.0, 0.95, 0.8)
MIX_PACES_OUT = (0.9, 1.0, 0.85)


def _dot(a, b):
    return jnp.dot(a, b, preferred_element_type=F32)


def _dot_nt(a, b):
    return lax.dot_general(a, b, (((1,), (1,)), ((), ())), preferred_element_type=F32)


def _dot_tn(a, b):
    return lax.dot_general(a, b, (((0,), (0,)), ((), ())), preferred_element_type=F32)


def _mm_cost(m, k, n):
    return (m // MXU_ROWS) * -(-k // MXU_TILE) * -(-n // MXU_TILE)


def _dot_by_cols(a, w_ref, rows, cols):
    pieces = []
    for c in range(cols[0], cols[1], DOT_COLS):
        end = min(c + DOT_COLS, cols[1])
        pieces.append(_dot(a, w_ref[rows[0]:rows[1], c:end]))
        yield _mm_cost(a.shape[0], rows[1] - rows[0], end - c)
    return pieces[0] if len(pieces) == 1 else jnp.concatenate(pieces, axis=1)


def _split_bf16(x):
    hi = x.astype(BF16)
    lo = (x - hi.astype(F32)).astype(BF16)
    return hi, lo


def _const_spec(shape):
    nd = len(shape)
    return pl.BlockSpec(shape, lambda *_: (0,) * nd, pipeline_mode=pl.Buffered(1))


class _Part(NamedTuple):
    stages: Callable[..., Any]
    args: list
    in_specs: list
    out_specs: list
    out_shape: list
    scratch: list
    cost: int


def _interleave(gens, totals):
    done = [0] * len(gens)
    live = list(range(len(gens)))
    while live:
        i = min(live, key=lambda j: done[j] / totals[j])
        try:
            done[i] += next(gens[i])
        except StopIteration:
            live.remove(i)


def _launch(name, n_steps, parts, paces=None):
    paces = paces or [1.0] * len(parts)
    n_in = [len(p.args) for p in parts]
    n_out = [len(p.out_shape) for p in parts]
    n_scr = [len(p.scratch) for p in parts]

    def body(*refs):
        refs = list(refs)
        ins = [[refs.pop(0) for _ in range(n)] for n in n_in]
        outs = [[refs.pop(0) for _ in range(n)] for n in n_out]
        scrs = [[refs.pop(0) for _ in range(n)] for n in n_scr]
        _interleave([p.stages(i, o, s) for p, i, o, s in zip(parts, ins, outs, scrs)],
                    [p.cost * pace for p, pace in zip(parts, paces)])

    flat = pl.pallas_call(
        body,
        grid=(n_steps,),
        in_specs=[s for p in parts for s in p.in_specs],
        out_specs=[s for p in parts for s in p.out_specs],
        out_shape=[s for p in parts for s in p.out_shape],
        scratch_shapes=[s for p in parts for s in p.scratch],
        compiler_params=pltpu.CompilerParams(dimension_semantics=("arbitrary",),
                                             vmem_limit_bytes=V7X_VMEM_LIMIT_BYTES),
        name=name,
    )(*[a for p in parts for a in p.args])
    flat = list(flat)
    return [[flat.pop(0) for _ in range(n)] for n in n_out]


class _Mod(NamedTuple):
    rows: Any
    first: int
    shared: bool


def _mod_spec(mod, tile, seq_len, tile_of_step=lambda i: i):
    if mod.shared:
        return pl.BlockSpec((1, 1, N_MOD * D_MODEL), lambda i: (mod.first, 0, 0))
    assert seq_len % tile == 0
    per_seq = seq_len // tile
    return pl.BlockSpec((1, 1, N_MOD * D_MODEL), lambda i: (mod.first + tile_of_step(i) // per_seq, 0, 0))


def _mod_kernel(cctx_ref, c_ref, w_ref, b_ref, o_ref, cond_scr):
    n_req = c_ref.shape[0]
    cond_scr[...] = jnp.zeros_like(cond_scr)
    cond_scr[0:1, :] = cctx_ref[...]
    cond_scr[1:1 + n_req, :] = c_ref[...]
    c = cond_scr[...]
    x = c * jax.nn.sigmoid(c)
    x_hi, x_lo = _split_bf16(x)
    w = w_ref[...].astype(BF16)
    part = _dot(x_hi, w) + _dot(x_lo, w)

    @pl.when(pl.program_id(0) == 0)
    def _():
        o_ref[:, 0, :] = b_ref[...] + part

    @pl.when(pl.program_id(0) > 0)
    def _():
        o_ref[:, 0, :] += part


def _modulation(c_ctx, c, w_ada, b_ada):
    n = N_MOD * D_MODEL
    tk = ADALN_ROW_TILE
    n_req = c.shape[0]
    assert 1 + n_req <= COND_ROWS
    return pl.pallas_call(
        _mod_kernel,
        grid=(D_MODEL // tk,),
        in_specs=[pl.BlockSpec((1, tk), lambda j: (0, j)),
                  pl.BlockSpec((n_req, tk), lambda j: (0, j)),
                  pl.BlockSpec((tk, n), lambda j: (j, 0)),
                  pl.BlockSpec((1, n), lambda j: (0, 0))],
        out_specs=pl.BlockSpec((COND_ROWS, 1, n), lambda j: (0, 0, 0)),
        out_shape=jax.ShapeDtypeStruct((COND_ROWS, 1, n), F32),
        scratch_shapes=[pltpu.VMEM((COND_ROWS, tk), F32)],
        compiler_params=pltpu.CompilerParams(dimension_semantics=("arbitrary",),
                                             vmem_limit_bytes=V7X_VMEM_LIMIT_BYTES),
        name="adaln_mod",
    )(c_ctx, c, w_ada, b_ada)


def _rms_rows(x, w):
    return x * lax.rsqrt(jnp.mean(x * x, axis=-1, keepdims=True) + EPS) * w


def _head_sumsq(a, ones_ref):
    return _dot((a * a).astype(BF16), ones_ref[...])


def _head_rms(a, sumsq, w):
    return a * lax.rsqrt(sumsq * (1.0 / HEAD_DIM) + EPS) * w


def _forget_lower_bound(logits):
    e = jnp.exp(logits - jnp.max(logits, axis=1, keepdims=True))
    return e[:, 0, :] / jnp.sum(e, axis=1)


def _rope(x, cos, s_up, s_dn):
    cols = []
    for j in range(x.shape[1] // LANES):
        sl = slice(j * LANES, (j + 1) * LANES)
        xj = x[:, sl]
        cols.append(xj * cos[:, sl] + pltpu.roll(xj, LANES - ROPE_PAIR_LANES, 1) * s_up[:, sl]
                    + pltpu.roll(xj, ROPE_PAIR_LANES, 1) * s_dn[:, sl])
    return jnp.concatenate(cols, axis=1) if len(cols) > 1 else cols[0]


def _in_stages(latent, ins, outs, scrs):
    x_ref, mod_ref, nw_ref, w_ref, qw_ref, kw_ref, oq_ref, ok_ref, lbl_ref = ins[:9]
    if scrs:
        (w_bf_ref,) = scrs
        rows = W_CAST_ROWS

        @pl.when(pl.program_id(0) == 0)
        def _():
            def cast_rows(r, carry):
                sl = pl.ds(pl.multiple_of(r * rows, rows), rows)
                w_bf_ref[sl, :] = w_ref[sl, :].astype(BF16)
                return carry

            lax.fori_loop(0, w_ref.shape[0] // rows, cast_rows, 0)

        w_ref = w_bf_ref
    if latent:
        cos_ref, sup_ref, sdn_ref = ins[9:]
        zh_ref, q_ref, k_ref, vt_ref, g_ref = outs
    else:
        zh_ref, q_ref, k_ref, kt_ref, vt_ref, g_ref = outs
    tm = x_ref.shape[0]
    m = mod_ref[0]
    sh1 = m[:, 0:D_MODEL]
    sc1 = m[:, D_MODEL:2 * D_MODEL]
    h = _rms_rows(x_ref[...], nw_ref[...]) * (1.0 + sc1) + sh1
    hb = h.astype(BF16)
    c0 = ZH_WIDTH
    c1 = c0 + Q_WIDTH
    c2 = c1 + KV_WIDTH
    c3 = c2 + KV_WIDTH
    aq = _dot(hb, w_ref[:, c0:c1])
    ak = _dot(hb, w_ref[:, c1:c2])
    av = _dot(hb, w_ref[:, c2:c3])
    yield _mm_cost(tm, D_MODEL, c3 - c0)
    gl = yield from _dot_by_cols(hb, w_ref, (0, D_MODEL), (c3, D_IN))
    ssq = _head_sumsq(aq, oq_ref)
    ssk = _head_sumsq(ak, ok_ref)
    yield _mm_cost(tm, Q_WIDTH, Q_WIDTH) + _mm_cost(tm, KV_WIDTH, KV_WIDTH)
    lb = _forget_lower_bound(lbl_ref[...])

    def hgrn_segment(s):
        z = yield from _dot_by_cols(hb, w_ref, (0, D_MODEL), (s * HG_WIDTH, (s + 1) * HG_WIDTH))
        if s == 0:
            z = z * HG_DK ** -0.5
        elif s in (1, 2):
            lbv = lb[s - 1:s, :]
            z = jnp.log(lbv + (1.0 - lbv) * jax.nn.sigmoid(z))
        elif s == 4:
            z = z * jax.nn.sigmoid(z)
        zh_ref[:, s * HG_WIDTH:(s + 1) * HG_WIDTH] = z

    yield from hgrn_segment(1)
    yield from hgrn_segment(2)
    g_ref[...] = jax.nn.sigmoid(gl).astype(BF16)
    qn = _head_rms(aq, ssq, jnp.tile(qw_ref[...], (1, N_HEADS)))
    kn = _head_rms(ak, ssk, jnp.tile(kw_ref[...], (1, N_KV)))
    seq = vt_ref.shape[2]
    avt = av.T
    for s in range(vt_ref.shape[0]):
        vt_ref[s] = avt[:, s * seq:(s + 1) * seq].astype(vt_ref.dtype)
    for s in (4, 0, 3):
        yield from hgrn_segment(s)
    if latent:
        cos, sup, sdn = cos_ref[...], sup_ref[...], sdn_ref[...]
        qn = _rope(qn, cos, sup, sdn)
        kn = _rope(kn, cos[:, 0:KV_WIDTH], sup[:, 0:KV_WIDTH], sdn[:, 0:KV_WIDTH])
    else:
        knt = kn.T
        for s in range(kt_ref.shape[0]):
            kt_ref[s] = knt[:, s * seq:(s + 1) * seq]
    q_ref[...] = (qn * (HEAD_DIM ** -0.5 * LOG2_E)).astype(BF16)
    k_ref[...] = kn.astype(BF16)


def _in_part(tm, x2, mod, seq_len, norm_w, w_in, qw, kw, ones_q, ones_k, lb_logits, rope):
    t = x2.shape[0]
    latent = rope is not None
    per_seq = max(seq_len // tm, 1)
    row = lambda i: (i, 0)
    in_specs = [pl.BlockSpec((tm, D_MODEL), row),
                _mod_spec(mod, tm, seq_len),
                _const_spec((1, D_MODEL)),
                _const_spec((D_MODEL, D_IN)),
                _const_spec((1, HEAD_DIM)),
                _const_spec((1, HEAD_DIM)),
                _const_spec((Q_WIDTH, Q_WIDTH)),
                _const_spec((KV_WIDTH, KV_WIDTH)),
                _const_spec(lb_logits.shape)]
    args = [x2, mod.rows, norm_w, w_in, qw, kw, ones_q, ones_k, lb_logits]
    out_specs = [pl.BlockSpec((tm, ZH_WIDTH), row), pl.BlockSpec((tm, Q_WIDTH), row),
                 pl.BlockSpec((tm, KV_WIDTH), row)]
    out_shape = [jax.ShapeDtypeStruct((t, ZH_WIDTH), F32), jax.ShapeDtypeStruct((t, Q_WIDTH), BF16),
                 jax.ShapeDtypeStruct((t, KV_WIDTH), BF16)]
    if seq_len <= tm:
        assert tm % seq_len == 0
        t_spec = pl.BlockSpec((tm // seq_len, KV_WIDTH, seq_len), lambda i: (i, 0, 0))
    else:
        t_spec = pl.BlockSpec((1, KV_WIDTH, tm), lambda i: (i // per_seq, 0, i % per_seq))
    if latent:
        in_specs += [pl.BlockSpec((tm, Q_WIDTH), lambda i: (i % per_seq, 0))] * 3
        args += list(rope)
        out_specs.append(t_spec)
        out_shape.append(jax.ShapeDtypeStruct((t // seq_len, KV_WIDTH, seq_len), BF16))
    else:
        out_specs += [t_spec, t_spec]
        out_shape += [jax.ShapeDtypeStruct((t // seq_len, KV_WIDTH, seq_len), F32)] * 2
    out_specs.append(pl.BlockSpec((tm, 2 * D_MODEL), row))
    out_shape.append(jax.ShapeDtypeStruct((t, 2 * D_MODEL), BF16))
    cost = (_mm_cost(tm, D_MODEL, D_IN) + _mm_cost(tm, Q_WIDTH, Q_WIDTH) + _mm_cost(tm, KV_WIDTH, KV_WIDTH))
    scratch = [] if w_in.dtype == BF16 else [pltpu.VMEM((D_MODEL, D_IN), BF16)]
    return t // tm, _Part(functools.partial(_in_stages, latent), args, in_specs, out_specs, out_shape, scratch, cost)


def _attn_cost(lq, lks):
    per_head = sum(_mm_cost(lk, HEAD_DIM, lq) + _mm_cost(HEAD_DIM + ONES_ROWS, lk, lq) for lk in lks)
    return N_HEADS * per_head


def _attn_stages(k_transposed, lookahead, ins, outs, _):
    n_seg = len(k_transposed)
    q_ref, kv_refs = ins[0], ins[1:]
    (o_ref,) = outs
    q = q_ref[...]
    lq = q.shape[0]
    ks = [(kv_refs[2 * s][0].T if k_transposed[s] else kv_refs[2 * s][...]).astype(BF16) for s in range(n_seg)]
    vts = [kv_refs[2 * s + 1][0].astype(BF16) for s in range(n_seg)]
    vts = [[jnp.concatenate([vt[g * HEAD_DIM:(g + 1) * HEAD_DIM, :],
                             jnp.ones((ONES_ROWS, vt.shape[1]), BF16)], axis=0) for g in range(N_KV)]
           for vt in vts]
    score_cost = sum(_mm_cost(k.shape[0], HEAD_DIM, lq) for k in ks)
    value_cost = sum(_mm_cost(HEAD_DIM + ONES_ROWS, k.shape[0], lq) for k in ks)

    def scores(h):
        g = h // (N_HEADS // N_KV)
        qh = q[:, h * HEAD_DIM:(h + 1) * HEAD_DIM]
        return [_dot_nt(k[:, g * HEAD_DIM:(g + 1) * HEAD_DIM], qh) for k in ks]

    pending = []
    for h in range(lookahead):
        pending.append(scores(h))
        yield score_cost
    outs_t = []
    for h in range(N_HEADS):
        if h + lookahead < N_HEADS:
            pending.append(scores(h + lookahead))
            yield score_cost
        st = pending.pop(0)
        g = h // (N_HEADS // N_KV)
        m = functools.reduce(jnp.maximum, [jnp.max(s, axis=0, keepdims=True) for s in st])
        ot = functools.reduce(jnp.add, [_dot(vt[g], jnp.exp2(s - m).astype(BF16)) for vt, s in zip(vts, st)])
        outs_t.append(ot[0:HEAD_DIM, :] / ot[HEAD_DIM:HEAD_DIM + 1, :])
        if h == N_HEADS - 1:
            o_ref[...] = jnp.concatenate(outs_t, axis=0).T.astype(BF16)
        yield value_cost


def _attn_part(q, kv_segments, n_batch, seq_len, lookahead):
    t = q.shape[0]
    tq = ATTN_Q_TILE
    per_seq = seq_len // tq
    in_specs = [pl.BlockSpec((tq, Q_WIDTH), lambda i: (i, 0))]
    args = [q]
    for k, v_t, layer in kv_segments:
        lk = v_t.shape[-1]
        if v_t.ndim == 4:
            slab = pl.BlockSpec((1, None, KV_WIDTH, lk), lambda i, layer=layer: (i // per_seq, layer, 0, 0))
        else:
            slab = pl.BlockSpec((1, KV_WIDTH, lk), lambda i: (i // per_seq, 0, 0))
        k_spec = slab if k.ndim == v_t.ndim else pl.BlockSpec((lk, KV_WIDTH), lambda i: (i // per_seq, 0))
        in_specs += [k_spec, slab]
        args += [k, v_t]
    stages = functools.partial(_attn_stages, tuple(k.ndim == v_t.ndim for k, v_t, _ in kv_segments), lookahead)
    return n_batch * per_seq, _Part(stages, args, in_specs, [pl.BlockSpec((tq, Q_WIDTH), lambda i: (i, 0))],
                                    [jax.ShapeDtypeStruct((t, Q_WIDTH), BF16)], [],
                                    _attn_cost(tq, [v_t.shape[-1] for _, v_t, _ in kv_segments]))


def _hgrn_unit_cost():
    blk = HG_BLOCK
    return (2 * _mm_cost(blk, blk, 2 * HG_DK)
            + 2 * _mm_cost(blk, HG_DK, 2 * blk) + _mm_cost(blk, blk, HG_DV)
            + (blk // HG_PAIR) * (_mm_cost(HG_DV, HG_PAIR, 2 * HG_DK) + _mm_cost(HG_PAIR, 2 * HG_DK, HG_DV)))


def _hgrn_stages(seq_len, hps, has_s0, has_sfin, ins, outs, scrs):
    tf_ref, tb_ref, xf_ref, xb_ref, q_ref, lf_ref, lb_ref, v_ref, g_ref, nw_ref = ins[:10]
    s0_ref = ins[10] if has_s0 else None
    o_ref = outs[0]
    sfin_ref = outs[1] if has_sfin else None
    kv_scr, ss_scr, qd_scr, oi_scr, dec_scr = scrs
    n_blk = seq_len // HG_BLOCK
    n_pair = seq_len // HG_PAIR
    per_blk = HG_BLOCK // HG_CHUNK
    pairs_per_blk = HG_BLOCK // HG_PAIR
    one_row = jnp.ones((1, HG_DK), F32)

    def chunk_cumsum(t_ref, x):
        hi, lo = _split_bf16(x)
        r = _dot(t_ref[...], jnp.concatenate([hi, lo], axis=1))
        return r[:, 0:HG_DK] + r[:, HG_DK:2 * HG_DK]

    def chunk_edge_rows(x, first):
        off = 0 if first else HG_CHUNK - 1
        return [x[c * HG_CHUNK + off:c * HG_CHUNK + off + 1, :] for c in range(per_blk)]

    def spread_rows(rows):
        return jnp.concatenate([jnp.broadcast_to(r, (HG_CHUNK, HG_DK)) for r in rows], axis=0)

    units = [(blk, h) for blk in range(n_blk) for h in range(hps)]
    stage1 = []
    for blk, h in units:
        rows = slice(blk * HG_BLOCK, (blk + 1) * HG_BLOCK)
        cols = slice(h * HG_DK, (h + 1) * HG_DK)
        logf_f = lf_ref[rows, cols]
        logf_b = lb_ref[rows, cols]
        k_f = 1.0 - jnp.exp(logf_f)
        k_b = 1.0 - jnp.exp(logf_b)
        a_f = chunk_cumsum(tf_ref, logf_f)
        a_b = chunk_cumsum(tb_ref, logf_b)
        stage1.append((rows, cols, k_f, k_b, a_f, a_b))
        yield 2 * _mm_cost(HG_BLOCK, HG_BLOCK, 2 * HG_DK)
    stage2 = []
    for (blk, h), (rows, cols, k_f, k_b, a_f, a_b) in zip(units, stage1):
        q = q_ref[rows, cols]
        vb = v_ref[rows, cols].astype(BF16)
        ea_f = jnp.exp(a_f)
        ea_b = jnp.exp(a_b)
        qd_f = q * ea_f
        qd_b = q * ea_b
        kd_f = k_f * jnp.exp(-a_f)
        kd_b = k_b * jnp.exp(-a_b)
        dec_f = chunk_edge_rows(ea_f, first=False)
        dec_b = chunk_edge_rows(ea_b, first=True)
        ke_f = kd_f * spread_rows(dec_f)
        ke_b = kd_b * spread_rows(dec_b)
        even = [c % 2 == 0 for c in range(per_blk)]
        qd_pair = jnp.concatenate(
            [qd_f * spread_rows([one_row if even[c] else dec_f[c - 1] for c in range(per_blk)]),
             qd_b * spread_rows([dec_b[c + 1] if even[c] else one_row for c in range(per_blk)])], axis=1)
        ke_pair = jnp.concatenate(
            [ke_f * spread_rows([dec_f[c + 1] if even[c] else one_row for c in range(per_blk)]),
             ke_b * spread_rows([one_row if even[c] else dec_b[c - 1] for c in range(per_blk)])], axis=1).astype(BF16)
        sc_f = _dot_nt(qd_f.astype(BF16), jnp.concatenate([kd_f, ke_f], axis=0).astype(BF16))
        sc_b = _dot_nt(qd_b.astype(BF16), jnp.concatenate([kd_b, ke_b], axis=0).astype(BF16))
        for p in range(pairs_per_blk):
            pr = slice(p * HG_PAIR, (p + 1) * HG_PAIR)
            kv_scr[h, blk * pairs_per_blk + p] = _dot_tn(vb[pr, :], ke_pair[pr, :])
            dec_scr[h, blk * pairs_per_blk + p, 0:1, :] = jnp.concatenate(
                [dec_f[2 * p] * dec_f[2 * p + 1], dec_b[2 * p] * dec_b[2 * p + 1]], axis=1)
        qd_scr[h, rows, :] = qd_pair.astype(BF16)
        stage2.append((rows, cols, vb, sc_f, sc_b))
        yield 2 * _mm_cost(HG_BLOCK, HG_DK, 2 * HG_BLOCK) + pairs_per_blk * _mm_cost(HG_DV, HG_PAIR, 2 * HG_DK)
    for rows, cols, vb, sc_f, sc_b in stage2:
        s = (sc_f[:, 0:HG_BLOCK].astype(BF16) * tf_ref[...] + sc_f[:, HG_BLOCK:].astype(BF16) * xf_ref[...]
             + sc_b[:, 0:HG_BLOCK].astype(BF16) * tb_ref[...] + sc_b[:, HG_BLOCK:].astype(BF16) * xb_ref[...])
        oi_scr[rows, cols] = _dot(s, vb)
        yield _mm_cost(HG_BLOCK, HG_BLOCK, HG_DV)

    for h in range(hps):
        if has_s0:
            s_f, s_b = s0_ref[0, 0, h].T, s0_ref[0, 1, h].T
        else:
            s_f = s_b = jnp.zeros((HG_DV, HG_DK), F32)
        for pf in range(n_pair):
            pb = n_pair - 1 - pf
            ss_scr[h, pf, :, 0:HG_DK] = s_f.astype(BF16)
            s_f = dec_scr[h, pf, 0:1, 0:HG_DK] * s_f + kv_scr[h, pf, :, 0:HG_DK]
            ss_scr[h, pb, :, HG_DK:2 * HG_DK] = s_b.astype(BF16)
            s_b = dec_scr[h, pb, 0:1, HG_DK:2 * HG_DK] * s_b + kv_scr[h, pb, :, HG_DK:2 * HG_DK]
        if has_sfin:
            sfin_ref[0, 0, h] = s_f.T
            sfin_ref[0, 1, h] = s_b.T

    for blk, h in units:
        cols = slice(h * HG_DK, (h + 1) * HG_DK)
        for p in range(pairs_per_blk):
            pi = blk * pairs_per_blk + p
            rows = slice(pi * HG_PAIR, (pi + 1) * HG_PAIR)
            oi_scr[rows, cols] = oi_scr[rows, cols] + _dot_nt(qd_scr[h, rows, :], ss_scr[h, pi])
        yield pairs_per_blk * _mm_cost(HG_PAIR, 2 * HG_DK, HG_DV)

    for h in range(hps):
        cols = slice(h * HG_DK, (h + 1) * HG_DK)
        o = _rms_rows(oi_scr[:, cols], nw_ref[...])
        o_ref[:, cols] = (o * g_ref[:, cols]).astype(BF16)


def _chunk_masks():
    r = np.arange(HG_BLOCK)
    same = (r[:, None] // HG_CHUNK) == (r[None, :] // HG_CHUNK)
    lower = same & (r[None, :] <= r[:, None])
    upper = same & (r[None, :] >= r[:, None])
    same_pair = (r[:, None] // HG_PAIR) == (r[None, :] // HG_PAIR)
    cross_f = same_pair & (r[:, None] // HG_CHUNK == r[None, :] // HG_CHUNK + 1)
    return tuple(jnp.asarray(m.astype(np.float32), dtype=BF16) for m in (lower, upper, cross_f, cross_f.T))


def _hgrn_part(zh, norm_w, n_batch, seq_len, hps, s0, want_final):
    t = zh.shape[0]
    n_pair = seq_len // HG_PAIR
    width = hps * HG_DK
    n_hb = HG_HEADS // hps
    masks = _chunk_masks()

    def seg_spec(seg):
        return pl.BlockSpec((seq_len, width), lambda i: (i // n_hb, seg * n_hb + i % n_hb))

    state_spec = pl.BlockSpec((1, 2, hps, HG_DK, HG_DV), lambda i: (i // n_hb, 0, i % n_hb, 0, 0))
    mask_spec = pl.BlockSpec((HG_BLOCK, HG_BLOCK), lambda i: (0, 0))
    in_specs = [mask_spec] * len(masks) + [seg_spec(s) for s in range(5)] + [pl.BlockSpec((1, HG_DV), lambda i: (0, 0))]
    args = list(masks) + [zh] * 5 + [norm_w]
    if s0 is not None:
        states, layer = s0
        in_specs.append(pl.BlockSpec((1, None, 2, hps, HG_DK, HG_DV),
                                     lambda i: (i // n_hb, layer, 0, i % n_hb, 0, 0)))
        args.append(states)
    out_specs = [pl.BlockSpec((seq_len, width), lambda i: (i // n_hb, i % n_hb))]
    out_shape = [jax.ShapeDtypeStruct((t, HG_WIDTH), BF16)]
    if want_final:
        out_specs.append(state_spec)
        out_shape.append(jax.ShapeDtypeStruct((n_batch, 2, HG_HEADS, HG_DK, HG_DV), F32))
    scratch = [pltpu.VMEM((hps, n_pair, HG_DV, 2 * HG_DK), F32),
               pltpu.VMEM((hps, n_pair, HG_DV, 2 * HG_DK), BF16),
               pltpu.VMEM((hps, seq_len, 2 * HG_DK), BF16),
               pltpu.VMEM((seq_len, width), F32),
               pltpu.VMEM((hps, n_pair, SUBLANES, 2 * HG_DK), F32)]
    stages = functools.partial(_hgrn_stages, seq_len, hps, s0 is not None, want_final)
    cost = hps * (seq_len // HG_BLOCK) * _hgrn_unit_cost()
    return n_batch * n_hb, _Part(stages, args, in_specs, out_specs, out_shape, scratch, cost)


def _out_stages(ins, outs, _):
    x_ref, mod_ref, oh_ref, oa_ref, g_ref, who_ref, wao_ref, wout_ref, nfw_ref, wff1_ref, wff2_ref, fnw_ref = ins
    (y_ref,) = outs
    tm = x_ref.shape[0]
    m = mod_ref[0]
    g1 = m[:, 2 * D_MODEL:3 * D_MODEL]
    sh2 = m[:, 3 * D_MODEL:4 * D_MODEL]
    sc2 = m[:, 4 * D_MODEL:5 * D_MODEL]
    g2 = m[:, 5 * D_MODEL:6 * D_MODEL]
    gates = g_ref[...].astype(F32)
    from_h = yield from _dot_by_cols(oh_ref[...], who_ref, (0, HG_WIDTH), (0, D_MODEL))
    from_a = yield from _dot_by_cols(oa_ref[...], wao_ref, (0, Q_WIDTH), (0, D_MODEL))
    merged = gates[:, 0:D_MODEL] * from_h + gates[:, D_MODEL:2 * D_MODEL] * from_a
    mixed = yield from _dot_by_cols(merged.astype(BF16), wout_ref, (0, D_MODEL), (0, D_MODEL))
    x1 = x_ref[...] + g1 * mixed
    h2 = (_rms_rows(x1, nfw_ref[...]) * (1.0 + sc2) + sh2).astype(BF16)
    acc = jnp.zeros_like(x1)
    for j in range(D_FF // D_MODEL):
        cols = slice(j * D_MODEL, (j + 1) * D_MODEL)
        hj = yield from _dot_by_cols(h2, wff1_ref, (0, D_MODEL), (cols.start, cols.stop))
        hj = jnp.maximum(hj, 0.0)
        acc = acc + (yield from _dot_by_cols((hj * hj).astype(BF16), wff2_ref, (cols.start, cols.stop), (0, D_MODEL)))
    y_ref[...] = _rms_rows(x1 + g2 * acc, fnw_ref[...])


def _out_part(tm, x2, mod, seq_len, oh, oa, gates, w_ho, w_ao, w_out, nfw, w_ff1, w_ff2, fnw):
    t = x2.shape[0]
    row = lambda i: (i, 0)
    in_specs = [pl.BlockSpec((tm, D_MODEL), row),
                _mod_spec(mod, tm, seq_len),
                pl.BlockSpec((tm, HG_WIDTH), row),
                pl.BlockSpec((tm, Q_WIDTH), row),
                pl.BlockSpec((tm, 2 * D_MODEL), row),
                _const_spec((HG_WIDTH, D_MODEL)),
                _const_spec((Q_WIDTH, D_MODEL)),
                _const_spec((D_MODEL, D_MODEL)),
                _const_spec((1, D_MODEL)),
                _const_spec((D_MODEL, D_FF)),
                _const_spec((D_FF, D_MODEL)),
                _const_spec((1, D_MODEL))]
    args = [x2, mod.rows, oh, oa, gates, w_ho, w_ao, w_out, nfw, w_ff1, w_ff2, fnw]
    cost = _mm_cost(tm, HG_WIDTH + Q_WIDTH, D_MODEL) + _mm_cost(tm, D_MODEL, D_MODEL) + 2 * _mm_cost(tm, D_MODEL, D_FF)
    return t // tm, _Part(_out_stages, args, in_specs, [pl.BlockSpec((tm, D_MODEL), row)],
                          [jax.ShapeDtypeStruct((t, D_MODEL), F32)], [], cost)


def _cast_stages(ins, outs, _):
    for src, dst in zip(ins, outs):
        dst[...] = src[...].astype(dst.dtype)
    yield 1


def _cast_part(arrays, n_steps):
    in_specs, out_specs, out_shape = [], [], []
    for a in arrays:
        rows, cols = a.shape
        assert rows % (n_steps * MXU_ROWS) == 0
        spec = pl.BlockSpec((rows // n_steps, cols), lambda i: (i, 0))
        in_specs.append(spec)
        out_specs.append(spec)
        out_shape.append(jax.ShapeDtypeStruct(a.shape, BF16))
    return n_steps, _Part(_cast_stages, list(arrays), in_specs, out_specs, out_shape, [], 1)


def _rope_tables(n_tokens):
    rows = n_tokens // GRID_W
    row = np.repeat(np.arange(rows, dtype=np.float32), GRID_W)
    col = np.tile(np.arange(GRID_W, dtype=np.float32), rows)
    axis_dim = HEAD_DIM // 2
    freqs = (ROPE_THETA ** (-np.arange(0, axis_dim, 2, dtype=np.float32) / axis_dim)).astype(np.float32)
    ang_r = row[:, None] * freqs
    ang_c = col[:, None] * freqs
    cr, sr, cc, sc = np.cos(ang_r), np.sin(ang_r), np.cos(ang_c), np.sin(ang_c)
    zero = np.zeros_like(sr)
    cos = np.concatenate([cr, cr, cc, cc], axis=-1)
    s_up = np.concatenate([-sr, zero, -sc, zero], axis=-1)
    s_dn = np.concatenate([zero, sr, zero, sc], axis=-1)
    return tuple(jnp.asarray(np.tile(a, (1, N_HEADS)), dtype=F32) for a in (cos, s_up, s_dn))


def _to_slab(kv):
    n, depth, length = kv.shape[:3]
    return jnp.transpose(kv, (0, 1, 3, 4, 2)).reshape(n, depth, KV_WIDTH, length)


def _from_slab(slab):
    n, _, length = slab.shape
    return jnp.transpose(slab.reshape(n, N_KV, HEAD_DIM, length), (0, 3, 1, 2))


def _block_ones(width):
    idx = np.arange(width) // HEAD_DIM
    return jnp.asarray((idx[:, None] == idx[None, :]).astype(np.float32), dtype=BF16)


def _same_steps(*counted_parts):
    steps = {n for n, _ in counted_parts}
    assert len(steps) == 1, steps
    return steps.pop(), [p for _, p in counted_parts]


def kernel(x_prompt, x_sample, cache_k, cache_v, state_hgrn, c, c_ctx, w_ada, b_ada, norm_mix_w, w_in, q_norm_w, k_norm_w, hgrn_lb_logits, hgrn_norm_w, w_hgrn_out, w_attn_out, w_out, norm_ffn_w, w_ff1, w_ff2, final_norm_w):
    n_p, l_p, _ = x_prompt.shape
    n_s, l_s, _ = x_sample.shape
    layer = 0

    mod = _modulation(c_ctx[None, :], c, w_ada[layer], b_ada[layer][None, :])
    mod_p = _Mod(mod, 0, True)
    mod_s = _Mod(mod, 1, False)

    in_small = (q_norm_w[layer][None, :], k_norm_w[layer][None, :],
                _block_ones(Q_WIDTH), _block_ones(KV_WIDTH), hgrn_lb_logits)
    nmw = norm_mix_w[layer][None, :]
    hnw = hgrn_norm_w[layer][None, :]
    xp2 = x_prompt.reshape(n_p * l_p, D_MODEL)
    xs2 = x_sample.reshape(n_s * l_s, D_MODEL)

    n_in, in_p = _in_part(MIX_TOKEN_TILE, xp2, mod_p, l_p, nmw, w_in[layer], *in_small, None)
    steps, parts = _same_steps((n_in, in_p), _cast_part(
        [w_in[layer], w_hgrn_out[layer], w_attn_out[layer], w_out[layer], w_ff1[layer], w_ff2[layer]], n_in))
    (zh_p, q_p, k_p, kt_p, vt_p, gates_p), (w_in_b, w_ho_b, w_ao_b, w_out_b, w_ff1_b, w_ff2_b) = _launch(
        "in_ctx", steps, parts)
    out_w = (w_ho_b, w_ao_b, w_out_b, norm_ffn_w[layer][None, :], w_ff1_b, w_ff2_b, final_norm_w[None, :])

    steps, parts = _same_steps(
        _attn_part(q_p, [(k_p, vt_p, None)], n_p, l_p, N_HEADS),
        _in_part(MIX_TOKEN_TILE, xs2, mod_s, l_s, nmw, w_in_b, *in_small, _rope_tables(l_s)),
        _hgrn_part(zh_p, hnw, n_p, l_p, HG_HEADS, None, True))
    (oa_p,), (zh_s, q_s, k_s, vt_s, gates_s), (oh_p, s_fin) = _launch("in_latent_mix_ctx", steps, parts,
                                                                      MIX_PACES_IN)

    segs = [(_to_slab(cache_k), _to_slab(cache_v), layer), (k_s, vt_s, None)]
    steps, parts = _same_steps(
        _out_part(MIX_TOKEN_TILE, xp2, mod_p, l_p, oh_p, oa_p, gates_p, *out_w),
        _attn_part(q_s, segs, n_s, l_s, ATTN_LOOKAHEAD),
        _hgrn_part(zh_s, hnw, n_s, l_s, 1, (state_hgrn, layer), False))
    (y_p,), (oa_s,), (oh_s,) = _launch("out_ctx_mix_latent", steps, parts, MIX_PACES_OUT)

    steps, parts = _same_steps(_out_part(SOLO_TOKEN_TILE, xs2, mod_s, l_s, oh_s, oa_s, gates_s, *out_w))
    ((y_s,),) = _launch("out_latent", steps, parts)

    new_k = _from_slab(kt_p)[:, None]
    new_v = _from_slab(vt_p)[:, None]
    new_s = s_fin.reshape(n_p, 1, 2, HG_HEADS, HG_DK, HG_DV)
    return (y_p.reshape(n_p, l_p, D_MODEL), y_s.reshape(n_s, l_s, D_MODEL), new_k, new_v, new_s)
```

```python
import functools
from typing import Any, Callable, NamedTuple

import numpy as np
import jax
import jax.numpy as jnp
from jax import lax
from jax.experimental import pallas as pl
from jax.experimental.pallas import tpu as pltpu

F32 = jnp.float32
BF16 = jnp.bfloat16

D_MODEL = 1024
GRID_W = 64
EPS = 1e-6
HG_HEADS = 4
HG_DK = 128
HG_DV = 128
HG_WIDTH = HG_HEADS * HG_DK
HG_CHUNK = 32
HG_PAIR = 2 * HG_CHUNK
N_HEADS = 8
N_KV = 2
HEAD_DIM = 64
Q_WIDTH = N_HEADS * HEAD_DIM
KV_WIDTH = N_KV * HEAD_DIM
ROPE_THETA = 10000.0
D_FF = 4 * D_MODEL
N_MOD = 6
ZH_WIDTH = 5 * HG_WIDTH
D_IN = ZH_WIDTH + Q_WIDTH + 2 * KV_WIDTH + 2 * D_MODEL

V7X_VMEM_LIMIT_BYTES = 56 * 1024 * 1024
SUBLANES = 8
LANES = 128
MXU_ROWS = 16
MXU_TILE = 256
ROPE_PAIR_LANES = HEAD_DIM // 4
HG_BLOCK = 256
COND_ROWS = 16
ONES_ROWS = 16
LOG2_E = float(np.log2(np.e))
MIX_TOKEN_TILE = 256
SOLO_TOKEN_TILE = 512
ADALN_ROW_TILE = 256
W_CAST_ROWS = 128
DOT_COLS = 512
ATTN_Q_TILE = 256
ATTN_LOOKAHEAD = 2
MIX_PACES_IN = (1.0, 1.0, 0.8)
MIX_PACES_OUT = (1.0, 1.0, 0.9)


def _dot(a, b):
    return jnp.dot(a, b, preferred_element_type=F32)


def _dot_nt(a, b):
    return lax.dot_general(a, b, (((1,), (1,)), ((), ())), preferred_element_type=F32)


def _dot_tn(a, b):
    return lax.dot_general(a, b, (((0,), (0,)), ((), ())), preferred_element_type=F32)


def _mm_cost(m, k, n):
    return (m // MXU_ROWS) * -(-k // MXU_TILE) * -(-n // MXU_TILE)


def _dot_by_cols(a, w_ref, rows, cols):
    pieces = []
    for c in range(cols[0], cols[1], DOT_COLS):
        end = min(c + DOT_COLS, cols[1])
        pieces.append(_dot(a, w_ref[rows[0]:rows[1], c:end]))
        yield _mm_cost(a.shape[0], rows[1] - rows[0], end - c)
    return pieces[0] if len(pieces) == 1 else jnp.concatenate(pieces, axis=1)


def _split_bf16(x):
    hi = x.astype(BF16)
    lo = (x - hi.astype(F32)).astype(BF16)
    return hi, lo


def _const_spec(shape):
    nd = len(shape)
    return pl.BlockSpec(shape, lambda *_: (0,) * nd, pipeline_mode=pl.Buffered(1))


class _Part(NamedTuple):
    stages: Callable[..., Any]
    args: list
    in_specs: list
    out_specs: list
    out_shape: list
    scratch: list
    cost: int


def _interleave(gens, totals):
    done = [0] * len(gens)
    live = list(range(len(gens)))
    while live:
        i = min(live, key=lambda j: done[j] / totals[j])
        try:
            done[i] += next(gens[i])
        except StopIteration:
            live.remove(i)


def _launch(name, n_steps, parts, paces=None):
    paces = paces or [1.0] * len(parts)
    n_in = [len(p.args) for p in parts]
    n_out = [len(p.out_shape) for p in parts]
    n_scr = [len(p.scratch) for p in parts]

    def body(*refs):
        refs = list(refs)
        ins = [[refs.pop(0) for _ in range(n)] for n in n_in]
        outs = [[refs.pop(0) for _ in range(n)] for n in n_out]
        scrs = [[refs.pop(0) for _ in range(n)] for n in n_scr]
        _interleave([p.stages(i, o, s) for p, i, o, s in zip(parts, ins, outs, scrs)],
                    [p.cost * pace for p, pace in zip(parts, paces)])

    flat = pl.pallas_call(
        body,
        grid=(n_steps,),
        in_specs=[s for p in parts for s in p.in_specs],
        out_specs=[s for p in parts for s in p.out_specs],
        out_shape=[s for p in parts for s in p.out_shape],
        scratch_shapes=[s for p in parts for s in p.scratch],
        compiler_params=pltpu.CompilerParams(dimension_semantics=("arbitrary",),
                                             vmem_limit_bytes=V7X_VMEM_LIMIT_BYTES),
        name=name,
    )(*[a for p in parts for a in p.args])
    flat = list(flat)
    return [[flat.pop(0) for _ in range(n)] for n in n_out]


class _Mod(NamedTuple):
    rows: Any
    first: int
    shared: bool


def _mod_spec(mod, tile, seq_len, tile_of_step=lambda i: i):
    if mod.shared:
        return pl.BlockSpec((1, 1, N_MOD * D_MODEL), lambda i: (mod.first, 0, 0))
    assert seq_len % tile == 0
    per_seq = seq_len // tile
    return pl.BlockSpec((1, 1, N_MOD * D_MODEL), lambda i: (mod.first + tile_of_step(i) // per_seq, 0, 0))


def _mod_kernel(cctx_ref, c_ref, w_ref, b_ref, o_ref, cond_scr):
    n_req = c_ref.shape[0]
    cond_scr[...] = jnp.zeros_like(cond_scr)
    cond_scr[0:1, :] = cctx_ref[...]
    cond_scr[1:1 + n_req, :] = c_ref[...]
    c = cond_scr[...]
    x = c * jax.nn.sigmoid(c)
    x_hi, x_lo = _split_bf16(x)
    w = w_ref[...].astype(BF16)
    part = _dot(x_hi, w) + _dot(x_lo, w)

    @pl.when(pl.program_id(0) == 0)
    def _():
        o_ref[:, 0, :] = b_ref[...] + part

    @pl.when(pl.program_id(0) > 0)
    def _():
        o_ref[:, 0, :] += part


def _modulation(c_ctx, c, w_ada, b_ada):
    n = N_MOD * D_MODEL
    tk = ADALN_ROW_TILE
    n_req = c.shape[0]
    assert 1 + n_req <= COND_ROWS
    return pl.pallas_call(
        _mod_kernel,
        grid=(D_MODEL // tk,),
        in_specs=[pl.BlockSpec((1, tk), lambda j: (0, j)),
                  pl.BlockSpec((n_req, tk), lambda j: (0, j)),
                  pl.BlockSpec((tk, n), lambda j: (j, 0)),
                  pl.BlockSpec((1, n), lambda j: (0, 0))],
        out_specs=pl.BlockSpec((COND_ROWS, 1, n), lambda j: (0, 0, 0)),
        out_shape=jax.ShapeDtypeStruct((COND_ROWS, 1, n), F32),
        scratch_shapes=[pltpu.VMEM((COND_ROWS, tk), F32)],
        compiler_params=pltpu.CompilerParams(dimension_semantics=("arbitrary",),
                                             vmem_limit_bytes=V7X_VMEM_LIMIT_BYTES),
        name="adaln_mod",
    )(c_ctx, c, w_ada, b_ada)


def _rms_rows(x, w):
    return x * lax.rsqrt(jnp.mean(x * x, axis=-1, keepdims=True) + EPS) * w


def _head_sumsq(a, ones_ref):
    return _dot((a * a).astype(BF16), ones_ref[...])


def _head_rms(a, sumsq, w):
    return a * lax.rsqrt(sumsq * (1.0 / HEAD_DIM) + EPS) * w


def _forget_lower_bound(logits):
    e = jnp.exp(logits - jnp.max(logits, axis=1, keepdims=True))
    return e[:, 0, :] / jnp.sum(e, axis=1)


def _rope(x, cos, s_up, s_dn):
    cols = []
    for j in range(x.shape[1] // LANES):
        sl = slice(j * LANES, (j + 1) * LANES)
        xj = x[:, sl]
        cols.append(xj * cos[:, sl] + pltpu.roll(xj, LANES - ROPE_PAIR_LANES, 1) * s_up[:, sl]
                    + pltpu.roll(xj, ROPE_PAIR_LANES, 1) * s_dn[:, sl])
    return jnp.concatenate(cols, axis=1) if len(cols) > 1 else cols[0]


def _in_stages(latent, ins, outs, scrs):
    x_ref, mod_ref, nw_ref, w_ref, qw_ref, kw_ref, oq_ref, ok_ref, lbl_ref = ins[:9]
    if scrs:
        (w_bf_ref,) = scrs
        rows = W_CAST_ROWS

        @pl.when(pl.program_id(0) == 0)
        def _():
            def cast_rows(r, carry):
                sl = pl.ds(pl.multiple_of(r * rows, rows), rows)
                w_bf_ref[sl, :] = w_ref[sl, :].astype(BF16)
                return carry

            lax.fori_loop(0, w_ref.shape[0] // rows, cast_rows, 0)

        w_ref = w_bf_ref
    if latent:
        cos_ref, sup_ref, sdn_ref = ins[9:]
        zh_ref, q_ref, k_ref, vt_ref, g_ref = outs
    else:
        zh_ref, q_ref, k_ref, kt_ref, vt_ref, g_ref = outs
    tm = x_ref.shape[0]
    m = mod_ref[0]
    sh1 = m[:, 0:D_MODEL]
    sc1 = m[:, D_MODEL:2 * D_MODEL]
    h = _rms_rows(x_ref[...], nw_ref[...]) * (1.0 + sc1) + sh1
    hb = h.astype(BF16)
    c0 = ZH_WIDTH
    c1 = c0 + Q_WIDTH
    c2 = c1 + KV_WIDTH
    c3 = c2 + KV_WIDTH
    aq = _dot(hb, w_ref[:, c0:c1])
    ak = _dot(hb, w_ref[:, c1:c2])
    av = _dot(hb, w_ref[:, c2:c3])
    yield _mm_cost(tm, D_MODEL, c3 - c0)
    gl = yield from _dot_by_cols(hb, w_ref, (0, D_MODEL), (c3, D_IN))
    ssq = _head_sumsq(aq, oq_ref)
    ssk = _head_sumsq(ak, ok_ref)
    yield _mm_cost(tm, Q_WIDTH, Q_WIDTH) + _mm_cost(tm, KV_WIDTH, KV_WIDTH)
    lb = _forget_lower_bound(lbl_ref[...])

    def hgrn_segment(s):
        z = yield from _dot_by_cols(hb, w_ref, (0, D_MODEL), (s * HG_WIDTH, (s + 1) * HG_WIDTH))
        if s == 0:
            z = z * HG_DK ** -0.5
        elif s in (1, 2):
            lbv = lb[s - 1:s, :]
            z = jnp.log(lbv + (1.0 - lbv) * jax.nn.sigmoid(z))
        elif s == 4:
            z = z * jax.nn.sigmoid(z)
        zh_ref[:, s * HG_WIDTH:(s + 1) * HG_WIDTH] = z

    yield from hgrn_segment(1)
    yield from hgrn_segment(2)
    g_ref[...] = jax.nn.sigmoid(gl).astype(BF16)
    qn = _head_rms(aq, ssq, jnp.tile(qw_ref[...], (1, N_HEADS)))
    kn = _head_rms(ak, ssk, jnp.tile(kw_ref[...], (1, N_KV)))
    seq = vt_ref.shape[2]
    avt = av.T
    for s in range(vt_ref.shape[0]):
        vt_ref[s] = avt[:, s * seq:(s + 1) * seq].astype(vt_ref.dtype)
    for s in (4, 0, 3):
        yield from hgrn_segment(s)
    if latent:
        cos, sup, sdn = cos_ref[...], sup_ref[...], sdn_ref[...]
        qn = _rope(qn, cos, sup, sdn)
        kn = _rope(kn, cos[:, 0:KV_WIDTH], sup[:, 0:KV_WIDTH], sdn[:, 0:KV_WIDTH])
    else:
        knt = kn.T
        for s in range(kt_ref.shape[0]):
            kt_ref[s] = knt[:, s * seq:(s + 1) * seq]
    q_ref[...] = (qn * (HEAD_DIM ** -0.5 * LOG2_E)).astype(BF16)
    k_ref[...] = kn.astype(BF16)


def _in_part(tm, x2, mod, seq_len, norm_w, w_in, qw, kw, ones_q, ones_k, lb_logits, rope):
    t = x2.shape[0]
    latent = rope is not None
    per_seq = max(seq_len // tm, 1)
    row = lambda i: (i, 0)
    in_specs = [pl.BlockSpec((tm, D_MODEL), row),
                _mod_spec(mod, tm, seq_len),
                _const_spec((1, D_MODEL)),
                _const_spec((D_MODEL, D_IN)),
                _const_spec((1, HEAD_DIM)),
                _const_spec((1, HEAD_DIM)),
                _const_spec((Q_WIDTH, Q_WIDTH)),
                _const_spec((KV_WIDTH, KV_WIDTH)),
                _const_spec(lb_logits.shape)]
    args = [x2, mod.rows, norm_w, w_in, qw, kw, ones_q, ones_k, lb_logits]
    out_specs = [pl.BlockSpec((tm, ZH_WIDTH), row), pl.BlockSpec((tm, Q_WIDTH), row),
                 pl.BlockSpec((tm, KV_WIDTH), row)]
    out_shape = [jax.ShapeDtypeStruct((t, ZH_WIDTH), F32), jax.ShapeDtypeStruct((t, Q_WIDTH), BF16),
                 jax.ShapeDtypeStruct((t, KV_WIDTH), BF16)]
    if seq_len <= tm:
        assert tm % seq_len == 0
        t_spec = pl.BlockSpec((tm // seq_len, KV_WIDTH, seq_len), lambda i: (i, 0, 0))
    else:
        t_spec = pl.BlockSpec((1, KV_WIDTH, tm), lambda i: (i // per_seq, 0, i % per_seq))
    if latent:
        in_specs += [pl.BlockSpec((tm, Q_WIDTH), lambda i: (i % per_seq, 0))] * 3
        args += list(rope)
        out_specs.append(t_spec)
        out_shape.append(jax.ShapeDtypeStruct((t // seq_len, KV_WIDTH, seq_len), BF16))
    else:
        out_specs += [t_spec, t_spec]
        out_shape += [jax.ShapeDtypeStruct((t // seq_len, KV_WIDTH, seq_len), F32)] * 2
    out_specs.append(pl.BlockSpec((tm, 2 * D_MODEL), row))
    out_shape.append(jax.ShapeDtypeStruct((t, 2 * D_MODEL), BF16))
    cost = (_mm_cost(tm, D_MODEL, D_IN) + _mm_cost(tm, Q_WIDTH, Q_WIDTH) + _mm_cost(tm, KV_WIDTH, KV_WIDTH))
    scratch = [] if w_in.dtype == BF16 else [pltpu.VMEM((D_MODEL, D_IN), BF16)]
    return t // tm, _Part(functools.partial(_in_stages, latent), args, in_specs, out_specs, out_shape, scratch, cost)


def _attn_cost(lq, lks):
    per_head = sum(_mm_cost(lk, HEAD_DIM, lq) + _mm_cost(HEAD_DIM + ONES_ROWS, lk, lq) for lk in lks)
    return N_HEADS * per_head


def _attn_stages(k_transposed, lookahead, ins, outs, _):
    n_seg = len(k_transposed)
    q_ref, kv_refs = ins[0], ins[1:]
    (o_ref,) = outs
    q = q_ref[...]
    lq = q.shape[0]
    ks = [(kv_refs[2 * s][0].T if k_transposed[s] else kv_refs[2 * s][...]).astype(BF16) for s in range(n_seg)]
    vts = [kv_refs[2 * s + 1][0].astype(BF16) for s in range(n_seg)]
    vts = [[jnp.concatenate([vt[g * HEAD_DIM:(g + 1) * HEAD_DIM, :],
                             jnp.ones((ONES_ROWS, vt.shape[1]), BF16)], axis=0) for g in range(N_KV)]
           for vt in vts]
    score_cost = sum(_mm_cost(k.shape[0], HEAD_DIM, lq) for k in ks)
    value_cost = sum(_mm_cost(HEAD_DIM + ONES_ROWS, k.shape[0], lq) for k in ks)

    def scores(h):
        g = h // (N_HEADS // N_KV)
        qh = q[:, h * HEAD_DIM:(h + 1) * HEAD_DIM]
        return [_dot_nt(k[:, g * HEAD_DIM:(g + 1) * HEAD_DIM], qh) for k in ks]

    pending = []
    for h in range(lookahead):
        pending.append(scores(h))
        yield score_cost
    outs_t = []
    for h in range(N_HEADS):
        if h + lookahead < N_HEADS:
            pending.append(scores(h + lookahead))
            yield score_cost
        st = pending.pop(0)
        g = h // (N_HEADS // N_KV)
        m = functools.reduce(jnp.maximum, [jnp.max(s, axis=0, keepdims=True) for s in st])
        ot = functools.reduce(jnp.add, [_dot(vt[g], jnp.exp2(s - m).astype(BF16)) for vt, s in zip(vts, st)])
        outs_t.append(ot[0:HEAD_DIM, :] / ot[HEAD_DIM:HEAD_DIM + 1, :])
        if h == N_HEADS - 1:
            o_ref[...] = jnp.concatenate(outs_t, axis=0).T.astype(BF16)
        yield value_cost


def _attn_part(q, kv_segments, n_batch, seq_len, lookahead):
    t = q.shape[0]
    tq = ATTN_Q_TILE
    per_seq = seq_len // tq
    in_specs = [pl.BlockSpec((tq, Q_WIDTH), lambda i: (i, 0))]
    args = [q]
    for k, v_t, layer in kv_segments:
        lk = v_t.shape[-1]
        if v_t.ndim == 4:
            slab = pl.BlockSpec((1, None, KV_WIDTH, lk), lambda i, layer=layer: (i // per_seq, layer, 0, 0))
        else:
            slab = pl.BlockSpec((1, KV_WIDTH, lk), lambda i: (i // per_seq, 0, 0))
        k_spec = slab if k.ndim == v_t.ndim else pl.BlockSpec((lk, KV_WIDTH), lambda i: (i // per_seq, 0))
        in_specs += [k_spec, slab]
        args += [k, v_t]
    stages = functools.partial(_attn_stages, tuple(k.ndim == v_t.ndim for k, v_t, _ in kv_segments), lookahead)
    return n_batch * per_seq, _Part(stages, args, in_specs, [pl.BlockSpec((tq, Q_WIDTH), lambda i: (i, 0))],
                                    [jax.ShapeDtypeStruct((t, Q_WIDTH), BF16)], [],
                                    _attn_cost(tq, [v_t.shape[-1] for _, v_t, _ in kv_segments]))


def _hgrn_unit_cost():
    blk = HG_BLOCK
    return (2 * _mm_cost(blk, blk, 2 * HG_DK)
            + 2 * _mm_cost(blk, HG_DK, 2 * blk) + _mm_cost(blk, blk, HG_DV)
            + (blk // HG_PAIR) * (_mm_cost(HG_DV, HG_PAIR, 2 * HG_DK) + _mm_cost(HG_PAIR, 2 * HG_DK, HG_DV)))


def _hgrn_stages(seq_len, hps, has_s0, has_sfin, ins, outs, scrs):
    tf_ref, tb_ref, xf_ref, xb_ref, q_ref, lf_ref, lb_ref, v_ref, g_ref, nw_ref = ins[:10]
    s0_ref = ins[10] if has_s0 else None
    o_ref = outs[0]
    sfin_ref = outs[1] if has_sfin else None
    kv_scr, ss_scr, qd_scr, oi_scr, dec_scr = scrs
    n_blk = seq_len // HG_BLOCK
    n_pair = seq_len // HG_PAIR
    per_blk = HG_BLOCK // HG_CHUNK
    pairs_per_blk = HG_BLOCK // HG_PAIR
    one_row = jnp.ones((1, HG_DK), F32)

    def chunk_cumsum(t_ref, x):
        hi, lo = _split_bf16(x)
        r = _dot(t_ref[...], jnp.concatenate([hi, lo], axis=1))
        return r[:, 0:HG_DK] + r[:, HG_DK:2 * HG_DK]

    def chunk_edge_rows(x, first):
        off = 0 if first else HG_CHUNK - 1
        return [x[c * HG_CHUNK + off:c * HG_CHUNK + off + 1, :] for c in range(per_blk)]

    def spread_rows(rows):
        return jnp.concatenate([jnp.broadcast_to(r, (HG_CHUNK, HG_DK)) for r in rows], axis=0)

    units = [(blk, h) for blk in range(n_blk) for h in range(hps)]
    stage1 = []
    for blk, h in units:
        rows = slice(blk * HG_BLOCK, (blk + 1) * HG_BLOCK)
        cols = slice(h * HG_DK, (h + 1) * HG_DK)
        logf_f = lf_ref[rows, cols]
        logf_b = lb_ref[rows, cols]
        k_f = 1.0 - jnp.exp(logf_f)
        k_b = 1.0 - jnp.exp(logf_b)
        a_f = chunk_cumsum(tf_ref, logf_f)
        a_b = chunk_cumsum(tb_ref, logf_b)
        stage1.append((rows, cols, k_f, k_b, a_f, a_b))
        yield 2 * _mm_cost(HG_BLOCK, HG_BLOCK, 2 * HG_DK)
    stage2 = []
    for (blk, h), (rows, cols, k_f, k_b, a_f, a_b) in zip(units, stage1):
        q = q_ref[rows, cols]
        vb = v_ref[rows, cols].astype(BF16)
        ea_f = jnp.exp(a_f)
        ea_b = jnp.exp(a_b)
        qd_f = q * ea_f
        qd_b = q * ea_b
        kd_f = k_f * jnp.exp(-a_f)
        kd_b = k_b * jnp.exp(-a_b)
        dec_f = chunk_edge_rows(ea_f, first=False)
        dec_b = chunk_edge_rows(ea_b, first=True)
        ke_f = kd_f * spread_rows(dec_f)
        ke_b = kd_b * spread_rows(dec_b)
        even = [c % 2 == 0 for c in range(per_blk)]
        qd_pair = jnp.concatenate(
            [qd_f * spread_rows([one_row if even[c] else dec_f[c - 1] for c in range(per_blk)]),
             qd_b * spread_rows([dec_b[c + 1] if even[c] else one_row for c in range(per_blk)])], axis=1)
        ke_pair = jnp.concatenate(
            [ke_f * spread_rows([dec_f[c + 1] if even[c] else one_row for c in range(per_blk)]),
             ke_b * spread_rows([one_row if even[c] else dec_b[c - 1] for c in range(per_blk)])], axis=1).astype(BF16)
        sc_f = _dot_nt(qd_f.astype(BF16), jnp.concatenate([kd_f, ke_f], axis=0).astype(BF16))
        sc_b = _dot_nt(qd_b.astype(BF16), jnp.concatenate([kd_b, ke_b], axis=0).astype(BF16))
        for p in range(pairs_per_blk):
            pr = slice(p * HG_PAIR, (p + 1) * HG_PAIR)
            kv_scr[h, blk * pairs_per_blk + p] = _dot_tn(vb[pr, :], ke_pair[pr, :])
            dec_scr[h, blk * pairs_per_blk + p, 0:1, :] = jnp.concatenate(
                [dec_f[2 * p] * dec_f[2 * p + 1], dec_b[2 * p] * dec_b[2 * p + 1]], axis=1)
        qd_scr[h, rows, :] = qd_pair.astype(BF16)
        stage2.append((rows, cols, vb, sc_f, sc_b))
        yield 2 * _mm_cost(HG_BLOCK, HG_DK, 2 * HG_BLOCK) + pairs_per_blk * _mm_cost(HG_DV, HG_PAIR, 2 * HG_DK)
    for rows, cols, vb, sc_f, sc_b in stage2:
        s = (sc_f[:, 0:HG_BLOCK].astype(BF16) * tf_ref[...] + sc_f[:, HG_BLOCK:].astype(BF16) * xf_ref[...]
             + sc_b[:, 0:HG_BLOCK].astype(BF16) * tb_ref[...] + sc_b[:, HG_BLOCK:].astype(BF16) * xb_ref[...])
        oi_scr[rows, cols] = _dot(s, vb)
        yield _mm_cost(HG_BLOCK, HG_BLOCK, HG_DV)

    for h in range(hps):
        if has_s0:
            s_f, s_b = s0_ref[0, 0, h].T, s0_ref[0, 1, h].T
        else:
            s_f = s_b = jnp.zeros((HG_DV, HG_DK), F32)
        for pf in range(n_pair):
            pb = n_pair - 1 - pf
            ss_scr[h, pf, :, 0:HG_DK] = s_f.astype(BF16)
            s_f = dec_scr[h, pf, 0:1, 0:HG_DK] * s_f + kv_scr[h, pf, :, 0:HG_DK]
            ss_scr[h, pb, :, HG_DK:2 * HG_DK] = s_b.astype(BF16)
            s_b = dec_scr[h, pb, 0:1, HG_DK:2 * HG_DK] * s_b + kv_scr[h, pb, :, HG_DK:2 * HG_DK]
        if has_sfin:
            sfin_ref[0, 0, h] = s_f.T
            sfin_ref[0, 1, h] = s_b.T

    for blk, h in units:
        cols = slice(h * HG_DK, (h + 1) * HG_DK)
        for p in range(pairs_per_blk):
            pi = blk * pairs_per_blk + p
            rows = slice(pi * HG_PAIR, (pi + 1) * HG_PAIR)
            oi_scr[rows, cols] = oi_scr[rows, cols] + _dot_nt(qd_scr[h, rows, :], ss_scr[h, pi])
        yield pairs_per_blk * _mm_cost(HG_PAIR, 2 * HG_DK, HG_DV)

    for h in range(hps):
        cols = slice(h * HG_DK, (h + 1) * HG_DK)
        o = _rms_rows(oi_scr[:, cols], nw_ref[...])
        o_ref[:, cols] = (o * g_ref[:, cols]).astype(BF16)


def _chunk_masks():
    r = np.arange(HG_BLOCK)
    same = (r[:, None] // HG_CHUNK) == (r[None, :] // HG_CHUNK)
    lower = same & (r[None, :] <= r[:, None])
    upper = same & (r[None, :] >= r[:, None])
    same_pair = (r[:, None] // HG_PAIR) == (r[None, :] // HG_PAIR)
    cross_f = same_pair & (r[:, None] // HG_CHUNK == r[None, :] // HG_CHUNK + 1)
    return tuple(jnp.asarray(m.astype(np.float32), dtype=BF16) for m in (lower, upper, cross_f, cross_f.T))


def _hgrn_part(zh, norm_w, n_batch, seq_len, hps, s0, want_final):
    t = zh.shape[0]
    n_pair = seq_len // HG_PAIR
    width = hps * HG_DK
    n_hb = HG_HEADS // hps
    masks = _chunk_masks()

    def seg_spec(seg):
        return pl.BlockSpec((seq_len, width), lambda i: (i // n_hb, seg * n_hb + i % n_hb))

    state_spec = pl.BlockSpec((1, 2, hps, HG_DK, HG_DV), lambda i: (i // n_hb, 0, i % n_hb, 0, 0))
    mask_spec = pl.BlockSpec((HG_BLOCK, HG_BLOCK), lambda i: (0, 0))
    in_specs = [mask_spec] * len(masks) + [seg_spec(s) for s in range(5)] + [pl.BlockSpec((1, HG_DV), lambda i: (0, 0))]
    args = list(masks) + [zh] * 5 + [norm_w]
    if s0 is not None:
        states, layer = s0
        in_specs.append(pl.BlockSpec((1, None, 2, hps, HG_DK, HG_DV),
                                     lambda i: (i // n_hb, layer, 0, i % n_hb, 0, 0)))
        args.append(states)
    out_specs = [pl.BlockSpec((seq_len, width), lambda i: (i // n_hb, i % n_hb))]
    out_shape = [jax.ShapeDtypeStruct((t, HG_WIDTH), BF16)]
    if want_final:
        out_specs.append(state_spec)
        out_shape.append(jax.ShapeDtypeStruct((n_batch, 2, HG_HEADS, HG_DK, HG_DV), F32))
    scratch = [pltpu.VMEM((hps, n_pair, HG_DV, 2 * HG_DK), F32),
               pltpu.VMEM((hps, n_pair, HG_DV, 2 * HG_DK), BF16),
               pltpu.VMEM((hps, seq_len, 2 * HG_DK), BF16),
               pltpu.VMEM((seq_len, width), F32),
               pltpu.VMEM((hps, n_pair, SUBLANES, 2 * HG_DK), F32)]
    stages = functools.partial(_hgrn_stages, seq_len, hps, s0 is not None, want_final)
    cost = hps * (seq_len // HG_BLOCK) * _hgrn_unit_cost()
    return n_batch * n_hb, _Part(stages, args, in_specs, out_specs, out_shape, scratch, cost)


def _out_stages(ins, outs, _):
    x_ref, mod_ref, oh_ref, oa_ref, g_ref, who_ref, wao_ref, wout_ref, nfw_ref, wff1_ref, wff2_ref, fnw_ref = ins
    (y_ref,) = outs
    tm = x_ref.shape[0]
    m = mod_ref[0]
    g1 = m[:, 2 * D_MODEL:3 * D_MODEL]
    sh2 = m[:, 3 * D_MODEL:4 * D_MODEL]
    sc2 = m[:, 4 * D_MODEL:5 * D_MODEL]
    g2 = m[:, 5 * D_MODEL:6 * D_MODEL]
    gates = g_ref[...].astype(F32)
    from_h = yield from _dot_by_cols(oh_ref[...], who_ref, (0, HG_WIDTH), (0, D_MODEL))
    from_a = yield from _dot_by_cols(oa_ref[...], wao_ref, (0, Q_WIDTH), (0, D_MODEL))
    merged = gates[:, 0:D_MODEL] * from_h + gates[:, D_MODEL:2 * D_MODEL] * from_a
    mixed = yield from _dot_by_cols(merged.astype(BF16), wout_ref, (0, D_MODEL), (0, D_MODEL))
    x1 = x_ref[...] + g1 * mixed
    h2 = (_rms_rows(x1, nfw_ref[...]) * (1.0 + sc2) + sh2).astype(BF16)
    acc = jnp.zeros_like(x1)
    for j in range(D_FF // D_MODEL):
        cols = slice(j * D_MODEL, (j + 1) * D_MODEL)
        hj = yield from _dot_by_cols(h2, wff1_ref, (0, D_MODEL), (cols.start, cols.stop))
        hj = jnp.maximum(hj, 0.0)
        acc = acc + (yield from _dot_by_cols((hj * hj).astype(BF16), wff2_ref, (cols.start, cols.stop), (0, D_MODEL)))
    y_ref[...] = _rms_rows(x1 + g2 * acc, fnw_ref[...])


def _out_part(tm, x2, mod, seq_len, oh, oa, gates, w_ho, w_ao, w_out, nfw, w_ff1, w_ff2, fnw):
    t = x2.shape[0]
    row = lambda i: (i, 0)
    in_specs = [pl.BlockSpec((tm, D_MODEL), row),
                _mod_spec(mod, tm, seq_len),
                pl.BlockSpec((tm, HG_WIDTH), row),
                pl.BlockSpec((tm, Q_WIDTH), row),
                pl.BlockSpec((tm, 2 * D_MODEL), row),
                _const_spec((HG_WIDTH, D_MODEL)),
                _const_spec((Q_WIDTH, D_MODEL)),
                _const_spec((D_MODEL, D_MODEL)),
                _const_spec((1, D_MODEL)),
                _const_spec((D_MODEL, D_FF)),
                _const_spec((D_FF, D_MODEL)),
                _const_spec((1, D_MODEL))]
    args = [x2, mod.rows, oh, oa, gates, w_ho, w_ao, w_out, nfw, w_ff1, w_ff2, fnw]
    cost = _mm_cost(tm, HG_WIDTH + Q_WIDTH, D_MODEL) + _mm_cost(tm, D_MODEL, D_MODEL) + 2 * _mm_cost(tm, D_MODEL, D_FF)
    return t // tm, _Part(_out_stages, args, in_specs, [pl.BlockSpec((tm, D_MODEL), row)],
                          [jax.ShapeDtypeStruct((t, D_MODEL), F32)], [], cost)


def _cast_stages(ins, outs, _):
    for src, dst in zip(ins, outs):
        dst[...] = src[...].astype(dst.dtype)
    yield 1


def _cast_part(arrays, n_steps):
    in_specs, out_specs, out_shape = [], [], []
    for a in arrays:
        rows, cols = a.shape
        assert rows % (n_steps * MXU_ROWS) == 0
        spec = pl.BlockSpec((rows // n_steps, cols), lambda i: (i, 0))
        in_specs.append(spec)
        out_specs.append(spec)
        out_shape.append(jax.ShapeDtypeStruct(a.shape, BF16))
    return n_steps, _Part(_cast_stages, list(arrays), in_specs, out_specs, out_shape, [], 1)


def _rope_tables(n_tokens):
    rows = n_tokens // GRID_W
    row = np.repeat(np.arange(rows, dtype=np.float32), GRID_W)
    col = np.tile(np.arange(GRID_W, dtype=np.float32), rows)
    axis_dim = HEAD_DIM // 2
    freqs = (ROPE_THETA ** (-np.arange(0, axis_dim, 2, dtype=np.float32) / axis_dim)).astype(np.float32)
    ang_r = row[:, None] * freqs
    ang_c = col[:, None] * freqs
    cr, sr, cc, sc = np.cos(ang_r), np.sin(ang_r), np.cos(ang_c), np.sin(ang_c)
    zero = np.zeros_like(sr)
    cos = np.concatenate([cr, cr, cc, cc], axis=-1)
    s_up = np.concatenate([-sr, zero, -sc, zero], axis=-1)
    s_dn = np.concatenate([zero, sr, zero, sc], axis=-1)
    return tuple(jnp.asarray(np.tile(a, (1, N_HEADS)), dtype=F32) for a in (cos, s_up, s_dn))


def _to_slab(kv):
    n, depth, length = kv.shape[:3]
    return jnp.transpose(kv, (0, 1, 3, 4, 2)).reshape(n, depth, KV_WIDTH, length)


def _from_slab(slab):
    n, _, length = slab.shape
    return jnp.transpose(slab.reshape(n, N_KV, HEAD_DIM, length), (0, 3, 1, 2))


def _block_ones(width):
    idx = np.arange(width) // HEAD_DIM
    return jnp.asarray((idx[:, None] == idx[None, :]).astype(np.float32), dtype=BF16)


def _same_steps(*counted_parts):
    steps = {n for n, _ in counted_parts}
    assert len(steps) == 1, steps
    return steps.pop(), [p for _, p in counted_parts]


def kernel(x_prompt, x_sample, cache_k, cache_v, state_hgrn, c, c_ctx, w_ada, b_ada, norm_mix_w, w_in, q_norm_w, k_norm_w, hgrn_lb_logits, hgrn_norm_w, w_hgrn_out, w_attn_out, w_out, norm_ffn_w, w_ff1, w_ff2, final_norm_w):
    n_p, l_p, _ = x_prompt.shape
    n_s, l_s, _ = x_sample.shape
    layer = 0

    mod = _modulation(c_ctx[None, :], c, w_ada[layer], b_ada[layer][None, :])
    mod_p = _Mod(mod, 0, True)
    mod_s = _Mod(mod, 1, False)

    in_small = (q_norm_w[layer][None, :], k_norm_w[layer][None, :],
                _block_ones(Q_WIDTH), _block_ones(KV_WIDTH), hgrn_lb_logits)
    nmw = norm_mix_w[layer][None, :]
    hnw = hgrn_norm_w[layer][None, :]
    xp2 = x_prompt.reshape(n_p * l_p, D_MODEL)
    xs2 = x_sample.reshape(n_s * l_s, D_MODEL)

    n_in, in_p = _in_part(MIX_TOKEN_TILE, xp2, mod_p, l_p, nmw, w_in[layer], *in_small, None)
    steps, parts = _same_steps((n_in, in_p), _cast_part(
        [w_in[layer], w_hgrn_out[layer], w_attn_out[layer], w_out[layer], w_ff1[layer], w_ff2[layer]], n_in))
    (zh_p, q_p, k_p, kt_p, vt_p, gates_p), (w_in_b, w_ho_b, w_ao_b, w_out_b, w_ff1_b, w_ff2_b) = _launch(
        "in_ctx", steps, parts)
    out_w = (w_ho_b, w_ao_b, w_out_b, norm_ffn_w[layer][None, :], w_ff1_b, w_ff2_b, final_norm_w[None, :])

    steps, parts = _same_steps(
        _attn_part(q_p, [(k_p, vt_p, None)], n_p, l_p, N_HEADS),
        _in_part(MIX_TOKEN_TILE, xs2, mod_s, l_s, nmw, w_in_b, *in_small, _rope_tables(l_s)),
        _hgrn_part(zh_p, hnw, n_p, l_p, HG_HEADS, None, True))
    (oa_p,), (zh_s, q_s, k_s, vt_s, gates_s), (oh_p, s_fin) = _launch("in_latent_mix_ctx", steps, parts,
                                                                      MIX_PACES_IN)

    segs = [(_to_slab(cache_k), _to_slab(cache_v), layer), (k_s, vt_s, None)]
    steps, parts = _same_steps(
        _out_part(MIX_TOKEN_TILE, xp2, mod_p, l_p, oh_p, oa_p, gates_p, *out_w),
        _attn_part(q_s, segs, n_s, l_s, ATTN_LOOKAHEAD),
        _hgrn_part(zh_s, hnw, n_s, l_s, 1, (state_hgrn, layer), False))
    (y_p,), (oa_s,), (oh_s,) = _launch("out_ctx_mix_latent", steps, parts, MIX_PACES_OUT)

    steps, parts = _same_steps(_out_part(SOLO_TOKEN_TILE, xs2, mod_s, l_s, oh_s, oa_s, gates_s, *out_w))
    ((y_s,),) = _launch("out_latent", steps, parts)

    new_k = _from_slab(kt_p)[:, None]
    new_v = _from_slab(vt_p)[:, None]
    new_s = s_fin.reshape(n_p, 1, 2, HG_HEADS, HG_DK, HG_DV)
    return (y_p.reshape(n_p, l_p, D_MODEL), y_s.reshape(n_s, l_s, D_MODEL), new_k, new_v, new_s)
```

```python
import functools
from typing import Any, Callable, NamedTuple

import numpy as np
import jax
import jax.numpy as jnp
from jax import lax
from jax.experimental import pallas as pl
from jax.experimental.pallas import tpu as pltpu

F32 = jnp.float32
BF16 = jnp.bfloat16

D_MODEL = 1024
GRID_W = 64
EPS = 1e-6
HG_HEADS = 4
HG_DK = 128
HG_DV = 128
HG_WIDTH = HG_HEADS * HG_DK
HG_CHUNK = 32
HG_PAIR = 2 * HG_CHUNK
N_HEADS = 8
N_KV = 2
HEAD_DIM = 64
Q_WIDTH = N_HEADS * HEAD_DIM
KV_WIDTH = N_KV * HEAD_DIM
ROPE_THETA = 10000.0
D_FF = 4 * D_MODEL
N_MOD = 6
ZH_WIDTH = 5 * HG_WIDTH
D_IN = ZH_WIDTH + Q_WIDTH + 2 * KV_WIDTH + 2 * D_MODEL

V7X_VMEM_LIMIT_BYTES = 56 * 1024 * 1024
SUBLANES = 8
LANES = 128
MXU_ROWS = 16
MXU_TILE = 256
ROPE_PAIR_LANES = HEAD_DIM // 4
HG_BLOCK = 256
COND_ROWS = 16
ONES_ROWS = 16
LOG2_E = float(np.log2(np.e))
MIX_TOKEN_TILE = 256
SOLO_TOKEN_TILE = 512
ADALN_ROW_TILE = 256
W_CAST_ROWS = 128
DOT_COLS = 512
ATTN_Q_TILE = 256
ATTN_LOOKAHEAD = 2
MIX_PACES_IN = (1.0, 1.0, 0.8)
MIX_PACES_OUT = (0.9, 1.0, 1.0)


def _dot(a, b):
    return jnp.dot(a, b, preferred_element_type=F32)


def _dot_nt(a, b):
    return lax.dot_general(a, b, (((1,), (1,)), ((), ())), preferred_element_type=F32)


def _dot_tn(a, b):
    return lax.dot_general(a, b, (((0,), (0,)), ((), ())), preferred_element_type=F32)


def _mm_cost(m, k, n):
    return (m // MXU_ROWS) * -(-k // MXU_TILE) * -(-n // MXU_TILE)


def _dot_by_cols(a, w_ref, rows, cols):
    pieces = []
    for c in range(cols[0], cols[1], DOT_COLS):
        end = min(c + DOT_COLS, cols[1])
        pieces.append(_dot(a, w_ref[rows[0]:rows[1], c:end]))
        yield _mm_cost(a.shape[0], rows[1] - rows[0], end - c)
    return pieces[0] if len(pieces) == 1 else jnp.concatenate(pieces, axis=1)


def _split_bf16(x):
    hi = x.astype(BF16)
    lo = (x - hi.astype(F32)).astype(BF16)
    return hi, lo


def _const_spec(shape):
    nd = len(shape)
    return pl.BlockSpec(shape, lambda *_: (0,) * nd, pipeline_mode=pl.Buffered(1))


class _Part(NamedTuple):
    stages: Callable[..., Any]
    args: list
    in_specs: list
    out_specs: list
    out_shape: list
    scratch: list
    cost: int


def _interleave(gens, totals):
    done = [0] * len(gens)
    live = list(range(len(gens)))
    while live:
        i = min(live, key=lambda j: done[j] / totals[j])
        try:
            done[i] += next(gens[i])
        except StopIteration:
            live.remove(i)


def _launch(name, n_steps, parts, paces=None):
    paces = paces or [1.0] * len(parts)
    n_in = [len(p.args) for p in parts]
    n_out = [len(p.out_shape) for p in parts]
    n_scr = [len(p.scratch) for p in parts]

    def body(*refs):
        refs = list(refs)
        ins = [[refs.pop(0) for _ in range(n)] for n in n_in]
        outs = [[refs.pop(0) for _ in range(n)] for n in n_out]
        scrs = [[refs.pop(0) for _ in range(n)] for n in n_scr]
        _interleave([p.stages(i, o, s) for p, i, o, s in zip(parts, ins, outs, scrs)],
                    [p.cost * pace for p, pace in zip(parts, paces)])

    flat = pl.pallas_call(
        body,
        grid=(n_steps,),
        in_specs=[s for p in parts for s in p.in_specs],
        out_specs=[s for p in parts for s in p.out_specs],
        out_shape=[s for p in parts for s in p.out_shape],
        scratch_shapes=[s for p in parts for s in p.scratch],
        compiler_params=pltpu.CompilerParams(dimension_semantics=("arbitrary",),
                                             vmem_limit_bytes=V7X_VMEM_LIMIT_BYTES),
        name=name,
    )(*[a for p in parts for a in p.args])
    flat = list(flat)
    return [[flat.pop(0) for _ in range(n)] for n in n_out]


class _Mod(NamedTuple):
    rows: Any
    first: int
    shared: bool


def _mod_spec(mod, tile, seq_len, tile_of_step=lambda i: i):
    if mod.shared:
        return pl.BlockSpec((1, 1, N_MOD * D_MODEL), lambda i: (mod.first, 0, 0))
    assert seq_len % tile == 0
    per_seq = seq_len // tile
    return pl.BlockSpec((1, 1, N_MOD * D_MODEL), lambda i: (mod.first + tile_of_step(i) // per_seq, 0, 0))


def _mod_kernel(cctx_ref, c_ref, w_ref, b_ref, o_ref, cond_scr):
    n_req = c_ref.shape[0]
    cond_scr[...] = jnp.zeros_like(cond_scr)
    cond_scr[0:1, :] = cctx_ref[...]
    cond_scr[1:1 + n_req, :] = c_ref[...]
    c = cond_scr[...]
    x = c * jax.nn.sigmoid(c)
    x_hi, x_lo = _split_bf16(x)
    w = w_ref[...].astype(BF16)
    part = _dot(x_hi, w) + _dot(x_lo, w)

    @pl.when(pl.program_id(0) == 0)
    def _():
        o_ref[:, 0, :] = b_ref[...] + part

    @pl.when(pl.program_id(0) > 0)
    def _():
        o_ref[:, 0, :] += part


def _modulation(c_ctx, c, w_ada, b_ada):
    n = N_MOD * D_MODEL
    tk = ADALN_ROW_TILE
    n_req = c.shape[0]
    assert 1 + n_req <= COND_ROWS
    return pl.pallas_call(
        _mod_kernel,
        grid=(D_MODEL // tk,),
        in_specs=[pl.BlockSpec((1, tk), lambda j: (0, j)),
                  pl.BlockSpec((n_req, tk), lambda j: (0, j)),
                  pl.BlockSpec((tk, n), lambda j: (j, 0)),
                  pl.BlockSpec((1, n), lambda j: (0, 0))],
        out_specs=pl.BlockSpec((COND_ROWS, 1, n), lambda j: (0, 0, 0)),
        out_shape=jax.ShapeDtypeStruct((COND_ROWS, 1, n), F32),
        scratch_shapes=[pltpu.VMEM((COND_ROWS, tk), F32)],
        compiler_params=pltpu.CompilerParams(dimension_semantics=("arbitrary",),
                                             vmem_limit_bytes=V7X_VMEM_LIMIT_BYTES),
        name="adaln_mod",
    )(c_ctx, c, w_ada, b_ada)


def _rms_rows(x, w):
    return x * lax.rsqrt(jnp.mean(x * x, axis=-1, keepdims=True) + EPS) * w


def _head_sumsq(a, ones_ref):
    return _dot((a * a).astype(BF16), ones_ref[...])


def _head_rms(a, sumsq, w):
    return a * lax.rsqrt(sumsq * (1.0 / HEAD_DIM) + EPS) * w


def _forget_lower_bound(logits):
    e = jnp.exp(logits - jnp.max(logits, axis=1, keepdims=True))
    return e[:, 0, :] / jnp.sum(e, axis=1)


def _rope(x, cos, s_up, s_dn):
    cols = []
    for j in range(x.shape[1] // LANES):
        sl = slice(j * LANES, (j + 1) * LANES)
        xj = x[:, sl]
        cols.append(xj * cos[:, sl] + pltpu.roll(xj, LANES - ROPE_PAIR_LANES, 1) * s_up[:, sl]
                    + pltpu.roll(xj, ROPE_PAIR_LANES, 1) * s_dn[:, sl])
    return jnp.concatenate(cols, axis=1) if len(cols) > 1 else cols[0]


def _in_stages(latent, ins, outs, scrs):
    x_ref, mod_ref, nw_ref, w_ref, qw_ref, kw_ref, oq_ref, ok_ref, lbl_ref = ins[:9]
    if scrs:
        (w_bf_ref,) = scrs
        rows = W_CAST_ROWS

        @pl.when(pl.program_id(0) == 0)
        def _():
            def cast_rows(r, carry):
                sl = pl.ds(pl.multiple_of(r * rows, rows), rows)
                w_bf_ref[sl, :] = w_ref[sl, :].astype(BF16)
                return carry

            lax.fori_loop(0, w_ref.shape[0] // rows, cast_rows, 0)

        w_ref = w_bf_ref
    if latent:
        cos_ref, sup_ref, sdn_ref = ins[9:]
        zh_ref, q_ref, k_ref, vt_ref, g_ref = outs
    else:
        zh_ref, q_ref, k_ref, kt_ref, vt_ref, g_ref = outs
    tm = x_ref.shape[0]
    m = mod_ref[0]
    sh1 = m[:, 0:D_MODEL]
    sc1 = m[:, D_MODEL:2 * D_MODEL]
    h = _rms_rows(x_ref[...], nw_ref[...]) * (1.0 + sc1) + sh1
    hb = h.astype(BF16)
    c0 = ZH_WIDTH
    c1 = c0 + Q_WIDTH
    c2 = c1 + KV_WIDTH
    c3 = c2 + KV_WIDTH
    aq = _dot(hb, w_ref[:, c0:c1])
    ak = _dot(hb, w_ref[:, c1:c2])
    av = _dot(hb, w_ref[:, c2:c3])
    yield _mm_cost(tm, D_MODEL, c3 - c0)
    gl = yield from _dot_by_cols(hb, w_ref, (0, D_MODEL), (c3, D_IN))
    ssq = _head_sumsq(aq, oq_ref)
    ssk = _head_sumsq(ak, ok_ref)
    yield _mm_cost(tm, Q_WIDTH, Q_WIDTH) + _mm_cost(tm, KV_WIDTH, KV_WIDTH)
    lb = _forget_lower_bound(lbl_ref[...])

    def hgrn_segment(s):
        z = yield from _dot_by_cols(hb, w_ref, (0, D_MODEL), (s * HG_WIDTH, (s + 1) * HG_WIDTH))
        if s == 0:
            z = z * HG_DK ** -0.5
        elif s in (1, 2):
            lbv = lb[s - 1:s, :]
            z = jnp.log(lbv + (1.0 - lbv) * jax.nn.sigmoid(z))
        elif s == 4:
            z = z * jax.nn.sigmoid(z)
        zh_ref[:, s * HG_WIDTH:(s + 1) * HG_WIDTH] = z

    yield from hgrn_segment(1)
    yield from hgrn_segment(2)
    g_ref[...] = jax.nn.sigmoid(gl).astype(BF16)
    qn = _head_rms(aq, ssq, jnp.tile(qw_ref[...], (1, N_HEADS)))
    kn = _head_rms(ak, ssk, jnp.tile(kw_ref[...], (1, N_KV)))
    seq = vt_ref.shape[2]
    avt = av.T
    for s in range(vt_ref.shape[0]):
        vt_ref[s] = avt[:, s * seq:(s + 1) * seq].astype(vt_ref.dtype)
    for s in (4, 0, 3):
        yield from hgrn_segment(s)
    if latent:
        cos, sup, sdn = cos_ref[...], sup_ref[...], sdn_ref[...]
        qn = _rope(qn, cos, sup, sdn)
        kn = _rope(kn, cos[:, 0:KV_WIDTH], sup[:, 0:KV_WIDTH], sdn[:, 0:KV_WIDTH])
    else:
        knt = kn.T
        for s in range(kt_ref.shape[0]):
            kt_ref[s] = knt[:, s * seq:(s + 1) * seq]
    q_ref[...] = (qn * (HEAD_DIM ** -0.5 * LOG2_E)).astype(BF16)
    k_ref[...] = kn.astype(BF16)


def _in_part(tm, x2, mod, seq_len, norm_w, w_in, qw, kw, ones_q, ones_k, lb_logits, rope):
    t = x2.shape[0]
    latent = rope is not None
    per_seq = max(seq_len // tm, 1)
    row = lambda i: (i, 0)
    in_specs = [pl.BlockSpec((tm, D_MODEL), row),
                _mod_spec(mod, tm, seq_len),
                _const_spec((1, D_MODEL)),
                _const_spec((D_MODEL, D_IN)),
                _const_spec((1, HEAD_DIM)),
                _const_spec((1, HEAD_DIM)),
                _const_spec((Q_WIDTH, Q_WIDTH)),
                _const_spec((KV_WIDTH, KV_WIDTH)),
                _const_spec(lb_logits.shape)]
    args = [x2, mod.rows, norm_w, w_in, qw, kw, ones_q, ones_k, lb_logits]
    out_specs = [pl.BlockSpec((tm, ZH_WIDTH), row), pl.BlockSpec((tm, Q_WIDTH), row),
                 pl.BlockSpec((tm, KV_WIDTH), row)]
    out_shape = [jax.ShapeDtypeStruct((t, ZH_WIDTH), F32), jax.ShapeDtypeStruct((t, Q_WIDTH), BF16),
                 jax.ShapeDtypeStruct((t, KV_WIDTH), BF16)]
    if seq_len <= tm:
        assert tm % seq_len == 0
        t_spec = pl.BlockSpec((tm // seq_len, KV_WIDTH, seq_len), lambda i: (i, 0, 0))
    else:
        t_spec = pl.BlockSpec((1, KV_WIDTH, tm), lambda i: (i // per_seq, 0, i % per_seq))
    if latent:
        in_specs += [pl.BlockSpec((tm, Q_WIDTH), lambda i: (i % per_seq, 0))] * 3
        args += list(rope)
        out_specs.append(t_spec)
        out_shape.append(jax.ShapeDtypeStruct((t // seq_len, KV_WIDTH, seq_len), BF16))
    else:
        out_specs += [t_spec, t_spec]
        out_shape += [jax.ShapeDtypeStruct((t // seq_len, KV_WIDTH, seq_len), F32)] * 2
    out_specs.append(pl.BlockSpec((tm, 2 * D_MODEL), row))
    out_shape.append(jax.ShapeDtypeStruct((t, 2 * D_MODEL), BF16))
    cost = (_mm_cost(tm, D_MODEL, D_IN) + _mm_cost(tm, Q_WIDTH, Q_WIDTH) + _mm_cost(tm, KV_WIDTH, KV_WIDTH))
    scratch = [] if w_in.dtype == BF16 else [pltpu.VMEM((D_MODEL, D_IN), BF16)]
    return t // tm, _Part(functools.partial(_in_stages, latent), args, in_specs, out_specs, out_shape, scratch, cost)


def _attn_cost(lq, lks):
    per_head = sum(_mm_cost(lk, HEAD_DIM, lq) + _mm_cost(HEAD_DIM + ONES_ROWS, lk, lq) for lk in lks)
    return N_HEADS * per_head


def _attn_stages(k_transposed, lookahead, ins, outs, _):
    n_seg = len(k_transposed)
    q_ref, kv_refs = ins[0], ins[1:]
    (o_ref,) = outs
    q = q_ref[...]
    lq = q.shape[0]
    ks = [(kv_refs[2 * s][0].T if k_transposed[s] else kv_refs[2 * s][...]).astype(BF16) for s in range(n_seg)]
    vts = [kv_refs[2 * s + 1][0].astype(BF16) for s in range(n_seg)]
    vts = [[jnp.concatenate([vt[g * HEAD_DIM:(g + 1) * HEAD_DIM, :],
                             jnp.ones((ONES_ROWS, vt.shape[1]), BF16)], axis=0) for g in range(N_KV)]
           for vt in vts]
    score_cost = sum(_mm_cost(k.shape[0], HEAD_DIM, lq) for k in ks)
    value_cost = sum(_mm_cost(HEAD_DIM + ONES_ROWS, k.shape[0], lq) for k in ks)

    def scores(h):
        g = h // (N_HEADS // N_KV)
        qh = q[:, h * HEAD_DIM:(h + 1) * HEAD_DIM]
        return [_dot_nt(k[:, g * HEAD_DIM:(g + 1) * HEAD_DIM], qh) for k in ks]

    pending = []
    for h in range(lookahead):
        pending.append(scores(h))
        yield score_cost
    outs_t = []
    for h in range(N_HEADS):
        if h + lookahead < N_HEADS:
            pending.append(scores(h + lookahead))
            yield score_cost
        st = pending.pop(0)
        g = h // (N_HEADS // N_KV)
        m = functools.reduce(jnp.maximum, [jnp.max(s, axis=0, keepdims=True) for s in st])
        ot = functools.reduce(jnp.add, [_dot(vt[g], jnp.exp2(s - m).astype(BF16)) for vt, s in zip(vts, st)])
        outs_t.append(ot[0:HEAD_DIM, :] / ot[HEAD_DIM:HEAD_DIM + 1, :])
        if h == N_HEADS - 1:
            o_ref[...] = jnp.concatenate(outs_t, axis=0).T.astype(BF16)
        yield value_cost


def _attn_part(q, kv_segments, n_batch, seq_len, lookahead):
    t = q.shape[0]
    tq = ATTN_Q_TILE
    per_seq = seq_len // tq
    in_specs = [pl.BlockSpec((tq, Q_WIDTH), lambda i: (i, 0))]
    args = [q]
    for k, v_t, layer in kv_segments:
        lk = v_t.shape[-1]
        if v_t.ndim == 4:
            slab = pl.BlockSpec((1, None, KV_WIDTH, lk), lambda i, layer=layer: (i // per_seq, layer, 0, 0))
        else:
            slab = pl.BlockSpec((1, KV_WIDTH, lk), lambda i: (i // per_seq, 0, 0))
        k_spec = slab if k.ndim == v_t.ndim else pl.BlockSpec((lk, KV_WIDTH), lambda i: (i // per_seq, 0))
        in_specs += [k_spec, slab]
        args += [k, v_t]
    stages = functools.partial(_attn_stages, tuple(k.ndim == v_t.ndim for k, v_t, _ in kv_segments), lookahead)
    return n_batch * per_seq, _Part(stages, args, in_specs, [pl.BlockSpec((tq, Q_WIDTH), lambda i: (i, 0))],
                                    [jax.ShapeDtypeStruct((t, Q_WIDTH), BF16)], [],
                                    _attn_cost(tq, [v_t.shape[-1] for _, v_t, _ in kv_segments]))


def _hgrn_unit_cost():
    blk = HG_BLOCK
    return (2 * _mm_cost(blk, blk, 2 * HG_DK)
            + 2 * _mm_cost(blk, HG_DK, 2 * blk) + _mm_cost(blk, blk, HG_DV)
            + (blk // HG_PAIR) * (_mm_cost(HG_DV, HG_PAIR, 2 * HG_DK) + _mm_cost(HG_PAIR, 2 * HG_DK, HG_DV)))


def _hgrn_stages(seq_len, hps, has_s0, has_sfin, ins, outs, scrs):
    tf_ref, tb_ref, xf_ref, xb_ref, q_ref, lf_ref, lb_ref, v_ref, g_ref, nw_ref = ins[:10]
    s0_ref = ins[10] if has_s0 else None
    o_ref = outs[0]
    sfin_ref = outs[1] if has_sfin else None
    kv_scr, ss_scr, qd_scr, oi_scr, dec_scr = scrs
    n_blk = seq_len // HG_BLOCK
    n_pair = seq_len // HG_PAIR
    per_blk = HG_BLOCK // HG_CHUNK
    pairs_per_blk = HG_BLOCK // HG_PAIR
    one_row = jnp.ones((1, HG_DK), F32)

    def chunk_cumsum(t_ref, x):
        hi, lo = _split_bf16(x)
        r = _dot(t_ref[...], jnp.concatenate([hi, lo], axis=1))
        return r[:, 0:HG_DK] + r[:, HG_DK:2 * HG_DK]

    def chunk_edge_rows(x, first):
        off = 0 if first else HG_CHUNK - 1
        return [x[c * HG_CHUNK + off:c * HG_CHUNK + off + 1, :] for c in range(per_blk)]

    def spread_rows(rows):
        return jnp.concatenate([jnp.broadcast_to(r, (HG_CHUNK, HG_DK)) for r in rows], axis=0)

    units = [(blk, h) for blk in range(n_blk) for h in range(hps)]
    stage1 = []
    for blk, h in units:
        rows = slice(blk * HG_BLOCK, (blk + 1) * HG_BLOCK)
        cols = slice(h * HG_DK, (h + 1) * HG_DK)
        logf_f = lf_ref[rows, cols]
        logf_b = lb_ref[rows, cols]
        k_f = 1.0 - jnp.exp(logf_f)
        k_b = 1.0 - jnp.exp(logf_b)
        a_f = chunk_cumsum(tf_ref, logf_f)
        a_b = chunk_cumsum(tb_ref, logf_b)
        stage1.append((rows, cols, k_f, k_b, a_f, a_b))
        yield 2 * _mm_cost(HG_BLOCK, HG_BLOCK, 2 * HG_DK)
    stage2 = []
    for (blk, h), (rows, cols, k_f, k_b, a_f, a_b) in zip(units, stage1):
        q = q_ref[rows, cols]
        vb = v_ref[rows, cols].astype(BF16)
        ea_f = jnp.exp(a_f)
        ea_b = jnp.exp(a_b)
        qd_f = q * ea_f
        qd_b = q * ea_b
        kd_f = k_f * jnp.exp(-a_f)
        kd_b = k_b * jnp.exp(-a_b)
        dec_f = chunk_edge_rows(ea_f, first=False)
        dec_b = chunk_edge_rows(ea_b, first=True)
        ke_f = kd_f * spread_rows(dec_f)
        ke_b = kd_b * spread_rows(dec_b)
        even = [c % 2 == 0 for c in range(per_blk)]
        qd_pair = jnp.concatenate(
            [qd_f * spread_rows([one_row if even[c] else dec_f[c - 1] for c in range(per_blk)]),
             qd_b * spread_rows([dec_b[c + 1] if even[c] else one_row for c in range(per_blk)])], axis=1)
        ke_pair = jnp.concatenate(
            [ke_f * spread_rows([dec_f[c + 1] if even[c] else one_row for c in range(per_blk)]),
             ke_b * spread_rows([one_row if even[c] else dec_b[c - 1] for c in range(per_blk)])], axis=1).astype(BF16)
        sc_f = _dot_nt(qd_f.astype(BF16), jnp.concatenate([kd_f, ke_f], axis=0).astype(BF16))
        sc_b = _dot_nt(qd_b.astype(BF16), jnp.concatenate([kd_b, ke_b], axis=0).astype(BF16))
        for p in range(pairs_per_blk):
            pr = slice(p * HG_PAIR, (p + 1) * HG_PAIR)
            kv_scr[h, blk * pairs_per_blk + p] = _dot_tn(vb[pr, :], ke_pair[pr, :])
            dec_scr[h, blk * pairs_per_blk + p, 0:1, :] = jnp.concatenate(
                [dec_f[2 * p] * dec_f[2 * p + 1], dec_b[2 * p] * dec_b[2 * p + 1]], axis=1)
        qd_scr[h, rows, :] = qd_pair.astype(BF16)
        stage2.append((rows, cols, vb, sc_f, sc_b))
        yield 2 * _mm_cost(HG_BLOCK, HG_DK, 2 * HG_BLOCK) + pairs_per_blk * _mm_cost(HG_DV, HG_PAIR, 2 * HG_DK)
    for rows, cols, vb, sc_f, sc_b in stage2:
        s = (sc_f[:, 0:HG_BLOCK].astype(BF16) * tf_ref[...] + sc_f[:, HG_BLOCK:].astype(BF16) * xf_ref[...]
             + sc_b[:, 0:HG_BLOCK].astype(BF16) * tb_ref[...] + sc_b[:, HG_BLOCK:].astype(BF16) * xb_ref[...])
        oi_scr[rows, cols] = _dot(s, vb)
        yield _mm_cost(HG_BLOCK, HG_BLOCK, HG_DV)

    for h in range(hps):
        if has_s0:
            s_f, s_b = s0_ref[0, 0, h].T, s0_ref[0, 1, h].T
        else:
            s_f = s_b = jnp.zeros((HG_DV, HG_DK), F32)
        for pf in range(n_pair):
            pb = n_pair - 1 - pf
            ss_scr[h, pf, :, 0:HG_DK] = s_f.astype(BF16)
            s_f = dec_scr[h, pf, 0:1, 0:HG_DK] * s_f + kv_scr[h, pf, :, 0:HG_DK]
            ss_scr[h, pb, :, HG_DK:2 * HG_DK] = s_b.astype(BF16)
            s_b = dec_scr[h, pb, 0:1, HG_DK:2 * HG_DK] * s_b + kv_scr[h, pb, :, HG_DK:2 * HG_DK]
        if has_sfin:
            sfin_ref[0, 0, h] = s_f.T
            sfin_ref[0, 1, h] = s_b.T

    for blk, h in units:
        cols = slice(h * HG_DK, (h + 1) * HG_DK)
        for p in range(pairs_per_blk):
            pi = blk * pairs_per_blk + p
            rows = slice(pi * HG_PAIR, (pi + 1) * HG_PAIR)
            oi_scr[rows, cols] = oi_scr[rows, cols] + _dot_nt(qd_scr[h, rows, :], ss_scr[h, pi])
        yield pairs_per_blk * _mm_cost(HG_PAIR, 2 * HG_DK, HG_DV)

    for h in range(hps):
        cols = slice(h * HG_DK, (h + 1) * HG_DK)
        o = _rms_rows(oi_scr[:, cols], nw_ref[...])
        o_ref[:, cols] = (o * g_ref[:, cols]).astype(BF16)


def _chunk_masks():
    r = np.arange(HG_BLOCK)
    same = (r[:, None] // HG_CHUNK) == (r[None, :] // HG_CHUNK)
    lower = same & (r[None, :] <= r[:, None])
    upper = same & (r[None, :] >= r[:, None])
    same_pair = (r[:, None] // HG_PAIR) == (r[None, :] // HG_PAIR)
    cross_f = same_pair & (r[:, None] // HG_CHUNK == r[None, :] // HG_CHUNK + 1)
    return tuple(jnp.asarray(m.astype(np.float32), dtype=BF16) for m in (lower, upper, cross_f, cross_f.T))


def _hgrn_part(zh, norm_w, n_batch, seq_len, hps, s0, want_final):
    t = zh.shape[0]
    n_pair = seq_len // HG_PAIR
    width = hps * HG_DK
    n_hb = HG_HEADS // hps
    masks = _chunk_masks()

    def seg_spec(seg):
        return pl.BlockSpec((seq_len, width), lambda i: (i // n_hb, seg * n_hb + i % n_hb))

    state_spec = pl.BlockSpec((1, 2, hps, HG_DK, HG_DV), lambda i: (i // n_hb, 0, i % n_hb, 0, 0))
    mask_spec = pl.BlockSpec((HG_BLOCK, HG_BLOCK), lambda i: (0, 0))
    in_specs = [mask_spec] * len(masks) + [seg_spec(s) for s in range(5)] + [pl.BlockSpec((1, HG_DV), lambda i: (0, 0))]
    args = list(masks) + [zh] * 5 + [norm_w]
    if s0 is not None:
        states, layer = s0
        in_specs.append(pl.BlockSpec((1, None, 2, hps, HG_DK, HG_DV),
                                     lambda i: (i // n_hb, layer, 0, i % n_hb, 0, 0)))
        args.append(states)
    out_specs = [pl.BlockSpec((seq_len, width), lambda i: (i // n_hb, i % n_hb))]
    out_shape = [jax.ShapeDtypeStruct((t, HG_WIDTH), BF16)]
    if want_final:
        out_specs.append(state_spec)
        out_shape.append(jax.ShapeDtypeStruct((n_batch, 2, HG_HEADS, HG_DK, HG_DV), F32))
    scratch = [pltpu.VMEM((hps, n_pair, HG_DV, 2 * HG_DK), F32),
               pltpu.VMEM((hps, n_pair, HG_DV, 2 * HG_DK), BF16),
               pltpu.VMEM((hps, seq_len, 2 * HG_DK), BF16),
               pltpu.VMEM((seq_len, width), F32),
               pltpu.VMEM((hps, n_pair, SUBLANES, 2 * HG_DK), F32)]
    stages = functools.partial(_hgrn_stages, seq_len, hps, s0 is not None, want_final)
    cost = hps * (seq_len // HG_BLOCK) * _hgrn_unit_cost()
    return n_batch * n_hb, _Part(stages, args, in_specs, out_specs, out_shape, scratch, cost)


def _out_stages(ins, outs, _):
    x_ref, mod_ref, oh_ref, oa_ref, g_ref, who_ref, wao_ref, wout_ref, nfw_ref, wff1_ref, wff2_ref, fnw_ref = ins
    (y_ref,) = outs
    tm = x_ref.shape[0]
    m = mod_ref[0]
    g1 = m[:, 2 * D_MODEL:3 * D_MODEL]
    sh2 = m[:, 3 * D_MODEL:4 * D_MODEL]
    sc2 = m[:, 4 * D_MODEL:5 * D_MODEL]
    g2 = m[:, 5 * D_MODEL:6 * D_MODEL]
    gates = g_ref[...].astype(F32)
    from_h = yield from _dot_by_cols(oh_ref[...], who_ref, (0, HG_WIDTH), (0, D_MODEL))
    from_a = yield from _dot_by_cols(oa_ref[...], wao_ref, (0, Q_WIDTH), (0, D_MODEL))
    merged = gates[:, 0:D_MODEL] * from_h + gates[:, D_MODEL:2 * D_MODEL] * from_a
    mixed = yield from _dot_by_cols(merged.astype(BF16), wout_ref, (0, D_MODEL), (0, D_MODEL))
    x1 = x_ref[...] + g1 * mixed
    h2 = (_rms_rows(x1, nfw_ref[...]) * (1.0 + sc2) + sh2).astype(BF16)
    acc = jnp.zeros_like(x1)
    for j in range(D_FF // D_MODEL):
        cols = slice(j * D_MODEL, (j + 1) * D_MODEL)
        hj = yield from _dot_by_cols(h2, wff1_ref, (0, D_MODEL), (cols.start, cols.stop))
        hj = jnp.maximum(hj, 0.0)
        acc = acc + (yield from _dot_by_cols((hj * hj).astype(BF16), wff2_ref, (cols.start, cols.stop), (0, D_MODEL)))
    y_ref[...] = _rms_rows(x1 + g2 * acc, fnw_ref[...])


def _out_part(tm, x2, mod, seq_len, oh, oa, gates, w_ho, w_ao, w_out, nfw, w_ff1, w_ff2, fnw):
    t = x2.shape[0]
    row = lambda i: (i, 0)
    in_specs = [pl.BlockSpec((tm, D_MODEL), row),
                _mod_spec(mod, tm, seq_len),
                pl.BlockSpec((tm, HG_WIDTH), row),
                pl.BlockSpec((tm, Q_WIDTH), row),
                pl.BlockSpec((tm, 2 * D_MODEL), row),
                _const_spec((HG_WIDTH, D_MODEL)),
                _const_spec((Q_WIDTH, D_MODEL)),
                _const_spec((D_MODEL, D_MODEL)),
                _const_spec((1, D_MODEL)),
                _const_spec((D_MODEL, D_FF)),
                _const_spec((D_FF, D_MODEL)),
                _const_spec((1, D_MODEL))]
    args = [x2, mod.rows, oh, oa, gates, w_ho, w_ao, w_out, nfw, w_ff1, w_ff2, fnw]
    cost = _mm_cost(tm, HG_WIDTH + Q_WIDTH, D_MODEL) + _mm_cost(tm, D_MODEL, D_MODEL) + 2 * _mm_cost(tm, D_MODEL, D_FF)
    return t // tm, _Part(_out_stages, args, in_specs, [pl.BlockSpec((tm, D_MODEL), row)],
                          [jax.ShapeDtypeStruct((t, D_MODEL), F32)], [], cost)


def _cast_stages(ins, outs, _):
    for src, dst in zip(ins, outs):
        dst[...] = src[...].astype(dst.dtype)
    yield 1


def _cast_part(arrays, n_steps):
    in_specs, out_specs, out_shape = [], [], []
    for a in arrays:
        rows, cols = a.shape
        assert rows % (n_steps * MXU_ROWS) == 0
        spec = pl.BlockSpec((rows // n_steps, cols), lambda i: (i, 0))
        in_specs.append(spec)
        out_specs.append(spec)
        out_shape.append(jax.ShapeDtypeStruct(a.shape, BF16))
    return n_steps, _Part(_cast_stages, list(arrays), in_specs, out_specs, out_shape, [], 1)


def _rope_tables(n_tokens):
    rows = n_tokens // GRID_W
    row = np.repeat(np.arange(rows, dtype=np.float32), GRID_W)
    col = np.tile(np.arange(GRID_W, dtype=np.float32), rows)
    axis_dim = HEAD_DIM // 2
    freqs = (ROPE_THETA ** (-np.arange(0, axis_dim, 2, dtype=np.float32) / axis_dim)).astype(np.float32)
    ang_r = row[:, None] * freqs
    ang_c = col[:, None] * freqs
    cr, sr, cc, sc = np.cos(ang_r), np.sin(ang_r), np.cos(ang_c), np.sin(ang_c)
    zero = np.zeros_like(sr)
    cos = np.concatenate([cr, cr, cc, cc], axis=-1)
    s_up = np.concatenate([-sr, zero, -sc, zero], axis=-1)
    s_dn = np.concatenate([zero, sr, zero, sc], axis=-1)
    return tuple(jnp.asarray(np.tile(a, (1, N_HEADS)), dtype=F32) for a in (cos, s_up, s_dn))


def _to_slab(kv):
    n, depth, length = kv.shape[:3]
    return jnp.transpose(kv, (0, 1, 3, 4, 2)).reshape(n, depth, KV_WIDTH, length)


def _from_slab(slab):
    n, _, length = slab.shape
    return jnp.transpose(slab.reshape(n, N_KV, HEAD_DIM, length), (0, 3, 1, 2))


def _block_ones(width):
    idx = np.arange(width) // HEAD_DIM
    return jnp.asarray((idx[:, None] == idx[None, :]).astype(np.float32), dtype=BF16)


def _same_steps(*counted_parts):
    steps = {n for n, _ in counted_parts}
    assert len(steps) == 1, steps
    return steps.pop(), [p for _, p in counted_parts]


def kernel(x_prompt, x_sample, cache_k, cache_v, state_hgrn, c, c_ctx, w_ada, b_ada, norm_mix_w, w_in, q_norm_w, k_norm_w, hgrn_lb_logits, hgrn_norm_w, w_hgrn_out, w_attn_out, w_out, norm_ffn_w, w_ff1, w_ff2, final_norm_w):
    n_p, l_p, _ = x_prompt.shape
    n_s, l_s, _ = x_sample.shape
    layer = 0

    mod = _modulation(c_ctx[None, :], c, w_ada[layer], b_ada[layer][None, :])
    mod_p = _Mod(mod, 0, True)
    mod_s = _Mod(mod, 1, False)

    in_small = (q_norm_w[layer][None, :], k_norm_w[layer][None, :],
                _block_ones(Q_WIDTH), _block_ones(KV_WIDTH), hgrn_lb_logits)
    nmw = norm_mix_w[layer][None, :]
    hnw = hgrn_norm_w[layer][None, :]
    xp2 = x_prompt.reshape(n_p * l_p, D_MODEL)
    xs2 = x_sample.reshape(n_s * l_s, D_MODEL)

    n_in, in_p = _in_part(MIX_TOKEN_TILE, xp2, mod_p, l_p, nmw, w_in[layer], *in_small, None)
    steps, parts = _same_steps((n_in, in_p), _cast_part(
        [w_in[layer], w_hgrn_out[layer], w_attn_out[layer], w_out[layer], w_ff1[layer], w_ff2[layer]], n_in))
    (zh_p, q_p, k_p, kt_p, vt_p, gates_p), (w_in_b, w_ho_b, w_ao_b, w_out_b, w_ff1_b, w_ff2_b) = _launch(
        "in_ctx", steps, parts)
    out_w = (w_ho_b, w_ao_b, w_out_b, norm_ffn_w[layer][None, :], w_ff1_b, w_ff2_b, final_norm_w[None, :])

    steps, parts = _same_steps(
        _attn_part(q_p, [(k_p, vt_p, None)], n_p, l_p, N_HEADS),
        _in_part(MIX_TOKEN_TILE, xs2, mod_s, l_s, nmw, w_in_b, *in_small, _rope_tables(l_s)),
        _hgrn_part(zh_p, hnw, n_p, l_p, HG_HEADS, None, True))
    (oa_p,), (zh_s, q_s, k_s, vt_s, gates_s), (oh_p, s_fin) = _launch("in_latent_mix_ctx", steps, parts,
                                                                      MIX_PACES_IN)

    segs = [(_to_slab(cache_k), _to_slab(cache_v), layer), (k_s, vt_s, None)]
    steps, parts = _same_steps(
        _hgrn_part(zh_s, hnw, n_s, l_s, 1, (state_hgrn, layer), False),
        _out_part(MIX_TOKEN_TILE, xp2, mod_p, l_p, oh_p, oa_p, gates_p, *out_w),
        _attn_part(q_s, segs, n_s, l_s, ATTN_LOOKAHEAD))
    (oh_s,), (y_p,), (oa_s,) = _launch("out_ctx_mix_latent", steps, parts, MIX_PACES_OUT)

    steps, parts = _same_steps(_out_part(SOLO_TOKEN_TILE, xs2, mod_s, l_s, oh_s, oa_s, gates_s, *out_w))
    ((y_s,),) = _launch("out_latent", steps, parts)

    new_k = _from_slab(kt_p)[:, None]
    new_v = _from_slab(vt_p)[:, None]
    new_s = s_fin.reshape(n_p, 1, 2, HG_HEADS, HG_DK, HG_DV)
    return (y_p.reshape(n_p, l_p, D_MODEL), y_s.reshape(n_s, l_s, D_MODEL), new_k, new_v, new_s)
```

```python
import functools
from typing import Any, Callable, NamedTuple

import numpy as np
import jax
import jax.numpy as jnp
from jax import lax
from jax.experimental import pallas as pl
from jax.experimental.pallas import tpu as pltpu

F32 = jnp.float32
BF16 = jnp.bfloat16

D_MODEL = 1024
GRID_W = 64
EPS = 1e-6
HG_HEADS = 4
HG_DK = 128
HG_DV = 128
HG_WIDTH = HG_HEADS * HG_DK
HG_CHUNK = 32
HG_PAIR = 2 * HG_CHUNK
N_HEADS = 8
N_KV = 2
HEAD_DIM = 64
Q_WIDTH = N_HEADS * HEAD_DIM
KV_WIDTH = N_KV * HEAD_DIM
ROPE_THETA = 10000.0
D_FF = 4 * D_MODEL
N_MOD = 6
ZH_WIDTH = 5 * HG_WIDTH
D_IN = ZH_WIDTH + Q_WIDTH + 2 * KV_WIDTH + 2 * D_MODEL

V7X_VMEM_LIMIT_BYTES = 56 * 1024 * 1024
SUBLANES = 8
LANES = 128
MXU_ROWS = 16
MXU_TILE = 256
ROPE_PAIR_LANES = HEAD_DIM // 4
HG_BLOCK = 256
COND_ROWS = 16
ONES_ROWS = 16
LOG2_E = float(np.log2(np.e))
MIX_TOKEN_TILE = 256
SOLO_TOKEN_TILE = 512
ADALN_ROW_TILE = 256
W_CAST_ROWS = 128
DOT_COLS = 512
ATTN_Q_TILE = 256
ATTN_LOOKAHEAD = 2
MIX_PACES_IN = (1.0, 1.0, 0.8)
MIX_PACES_OUT = (1.0, 1.0, 0.9)


def _dot(a, b):
    return jnp.dot(a, b, preferred_element_type=F32)


def _dot_nt(a, b):
    return lax.dot_general(a, b, (((1,), (1,)), ((), ())), preferred_element_type=F32)


def _dot_tn(a, b):
    return lax.dot_general(a, b, (((0,), (0,)), ((), ())), preferred_element_type=F32)


def _mm_cost(m, k, n):
    return (m // MXU_ROWS) * -(-k // MXU_TILE) * -(-n // MXU_TILE)


def _dot_by_cols(a, w_ref, rows, cols, post=lambda z, c0, c1: z):
    pieces = []
    for c in range(cols[0], cols[1], DOT_COLS):
        end = min(c + DOT_COLS, cols[1])
        pieces.append(post(_dot(a, w_ref[rows[0]:rows[1], c:end]), c, end))
        yield _mm_cost(a.shape[0], rows[1] - rows[0], end - c)
    return pieces[0] if len(pieces) == 1 else jnp.concatenate(pieces, axis=1)


def _split_bf16(x):
    hi = x.astype(BF16)
    lo = (x - hi.astype(F32)).astype(BF16)
    return hi, lo


def _const_spec(shape):
    nd = len(shape)
    return pl.BlockSpec(shape, lambda *_: (0,) * nd, pipeline_mode=pl.Buffered(1))


class _Part(NamedTuple):
    stages: Callable[..., Any]
    args: list
    in_specs: list
    out_specs: list
    out_shape: list
    scratch: list
    cost: int


def _interleave(gens, totals):
    done = [0] * len(gens)
    live = list(range(len(gens)))
    while live:
        i = min(live, key=lambda j: done[j] / totals[j])
        try:
            done[i] += next(gens[i])
        except StopIteration:
            live.remove(i)


def _launch(name, n_steps, parts, paces=None):
    paces = paces or [1.0] * len(parts)
    n_in = [len(p.args) for p in parts]
    n_out = [len(p.out_shape) for p in parts]
    n_scr = [len(p.scratch) for p in parts]

    def body(*refs):
        refs = list(refs)
        ins = [[refs.pop(0) for _ in range(n)] for n in n_in]
        outs = [[refs.pop(0) for _ in range(n)] for n in n_out]
        scrs = [[refs.pop(0) for _ in range(n)] for n in n_scr]
        _interleave([p.stages(i, o, s) for p, i, o, s in zip(parts, ins, outs, scrs)],
                    [p.cost * pace for p, pace in zip(parts, paces)])

    flat = pl.pallas_call(
        body,
        grid=(n_steps,),
        in_specs=[s for p in parts for s in p.in_specs],
        out_specs=[s for p in parts for s in p.out_specs],
        out_shape=[s for p in parts for s in p.out_shape],
        scratch_shapes=[s for p in parts for s in p.scratch],
        compiler_params=pltpu.CompilerParams(dimension_semantics=("arbitrary",),
                                             vmem_limit_bytes=V7X_VMEM_LIMIT_BYTES),
        name=name,
    )(*[a for p in parts for a in p.args])
    flat = list(flat)
    return [[flat.pop(0) for _ in range(n)] for n in n_out]


class _Mod(NamedTuple):
    rows: Any
    first: int
    shared: bool


def _mod_spec(mod, tile, seq_len, tile_of_step=lambda i: i):
    if mod.shared:
        return pl.BlockSpec((1, 1, N_MOD * D_MODEL), lambda i: (mod.first, 0, 0))
    assert seq_len % tile == 0
    per_seq = seq_len // tile
    return pl.BlockSpec((1, 1, N_MOD * D_MODEL), lambda i: (mod.first + tile_of_step(i) // per_seq, 0, 0))


def _mod_kernel(cctx_ref, c_ref, w_ref, b_ref, o_ref, cond_scr):
    n_req = c_ref.shape[0]
    cond_scr[...] = jnp.zeros_like(cond_scr)
    cond_scr[0:1, :] = cctx_ref[...]
    cond_scr[1:1 + n_req, :] = c_ref[...]
    c = cond_scr[...]
    x = c * jax.nn.sigmoid(c)
    x_hi, x_lo = _split_bf16(x)
    w = w_ref[...].astype(BF16)
    part = _dot(x_hi, w) + _dot(x_lo, w)

    @pl.when(pl.program_id(0) == 0)
    def _():
        o_ref[:, 0, :] = b_ref[...] + part

    @pl.when(pl.program_id(0) > 0)
    def _():
        o_ref[:, 0, :] += part


def _modulation(c_ctx, c, w_ada, b_ada):
    n = N_MOD * D_MODEL
    tk = ADALN_ROW_TILE
    n_req = c.shape[0]
    assert 1 + n_req <= COND_ROWS
    return pl.pallas_call(
        _mod_kernel,
        grid=(D_MODEL // tk,),
        in_specs=[pl.BlockSpec((1, tk), lambda j: (0, j)),
                  pl.BlockSpec((n_req, tk), lambda j: (0, j)),
                  pl.BlockSpec((tk, n), lambda j: (j, 0)),
                  pl.BlockSpec((1, n), lambda j: (0, 0))],
        out_specs=pl.BlockSpec((COND_ROWS, 1, n), lambda j: (0, 0, 0)),
        out_shape=jax.ShapeDtypeStruct((COND_ROWS, 1, n), F32),
        scratch_shapes=[pltpu.VMEM((COND_ROWS, tk), F32)],
        compiler_params=pltpu.CompilerParams(dimension_semantics=("arbitrary",),
                                             vmem_limit_bytes=V7X_VMEM_LIMIT_BYTES),
        name="adaln_mod",
    )(c_ctx, c, w_ada, b_ada)


def _rms_rows(x, w):
    return x * lax.rsqrt(jnp.mean(x * x, axis=-1, keepdims=True) + EPS) * w


def _head_sumsq(a, ones_ref):
    return _dot((a * a).astype(BF16), ones_ref[...])


def _head_rms(a, sumsq, w):
    return a * lax.rsqrt(sumsq * (1.0 / HEAD_DIM) + EPS) * w


def _forget_lower_bound(logits):
    e = jnp.exp(logits - jnp.max(logits, axis=1, keepdims=True))
    return e[:, 0, :] / jnp.sum(e, axis=1)


def _rope(x, cos, s_up, s_dn):
    cols = []
    for j in range(x.shape[1] // LANES):
        sl = slice(j * LANES, (j + 1) * LANES)
        xj = x[:, sl]
        cols.append(xj * cos[:, sl] + pltpu.roll(xj, LANES - ROPE_PAIR_LANES, 1) * s_up[:, sl]
                    + pltpu.roll(xj, ROPE_PAIR_LANES, 1) * s_dn[:, sl])
    return jnp.concatenate(cols, axis=1) if len(cols) > 1 else cols[0]


def _in_stages(latent, shared_mod, ins, outs, scrs):
    x_ref, mod_ref, nw_ref, w_ref, qw_ref, kw_ref, oq_ref, ok_ref, lbl_ref = ins[:9]
    scrs = list(scrs)
    m = mod_ref[0]
    sh1 = m[:, 0:D_MODEL]
    sc1 = m[:, D_MODEL:2 * D_MODEL]
    f32_weights = w_ref.dtype != BF16
    if f32_weights:
        w_bf_ref = scrs.pop(0)
    if shared_mod:
        bias_scr = scrs.pop(0)
    if f32_weights or shared_mod:
        @pl.when(pl.program_id(0) == 0)
        def _():
            w_src = w_ref
            if f32_weights:
                rows = W_CAST_ROWS

                def cast_rows(r, carry):
                    sl = pl.ds(pl.multiple_of(r * rows, rows), rows)
                    w_bf_ref[sl, :] = w_ref[sl, :].astype(BF16)
                    return carry

                lax.fori_loop(0, w_ref.shape[0] // rows, cast_rows, 0)
                w_src = w_bf_ref
            if shared_mod:
                hi, lo = _split_bf16(jnp.broadcast_to(sh1, (MXU_ROWS, D_MODEL)))
                bias_scr[...] = (_dot(hi, w_src[...]) + _dot(lo, w_src[...]))[0:SUBLANES, :]

    if f32_weights:
        w_ref = w_bf_ref
    if latent:
        cos_ref, sup_ref, sdn_ref = ins[9:]
        zh_ref, q_ref, k_ref, vt_ref, g_ref = outs
    else:
        zh_ref, q_ref, k_ref, kt_ref, vt_ref, g_ref = outs
    tm = x_ref.shape[0]
    x = x_ref[...]
    inv_rms = lax.rsqrt(jnp.mean(x * x, axis=-1, keepdims=True) + EPS)
    gain = nw_ref[...] * (1.0 + sc1)
    if shared_mod:
        hb = (x * gain).astype(BF16)

        def post(z, lo_col, hi_col):
            return z * inv_rms + bias_scr[0:1, lo_col:hi_col]
    else:
        hb = (x * inv_rms * gain + sh1).astype(BF16)

        def post(z, lo_col, hi_col):
            return z
    c0 = ZH_WIDTH
    c1 = c0 + Q_WIDTH
    c2 = c1 + KV_WIDTH
    c3 = c2 + KV_WIDTH
    aq = post(_dot(hb, w_ref[:, c0:c1]), c0, c1)
    ak = post(_dot(hb, w_ref[:, c1:c2]), c1, c2)
    av = post(_dot(hb, w_ref[:, c2:c3]), c2, c3)
    yield _mm_cost(tm, D_MODEL, c3 - c0)
    gl = yield from _dot_by_cols(hb, w_ref, (0, D_MODEL), (c3, D_IN), post)
    ssq = _head_sumsq(aq, oq_ref)
    ssk = _head_sumsq(ak, ok_ref)
    yield _mm_cost(tm, Q_WIDTH, Q_WIDTH) + _mm_cost(tm, KV_WIDTH, KV_WIDTH)
    lb = _forget_lower_bound(lbl_ref[...])

    def hgrn_segment(s):
        z = yield from _dot_by_cols(hb, w_ref, (0, D_MODEL), (s * HG_WIDTH, (s + 1) * HG_WIDTH), post)
        if s == 0:
            z = z * HG_DK ** -0.5
        elif s in (1, 2):
            lbv = lb[s - 1:s, :]
            z = jnp.log(lbv + (1.0 - lbv) * jax.nn.sigmoid(z))
        elif s == 4:
            z = z * jax.nn.sigmoid(z)
        zh_ref[:, s * HG_WIDTH:(s + 1) * HG_WIDTH] = z

    yield from hgrn_segment(1)
    yield from hgrn_segment(2)
    g_ref[...] = jax.nn.sigmoid(gl).astype(BF16)
    qn = _head_rms(aq, ssq, jnp.tile(qw_ref[...], (1, N_HEADS)))
    kn = _head_rms(ak, ssk, jnp.tile(kw_ref[...], (1, N_KV)))
    seq = vt_ref.shape[2]
    avt = av.T
    for s in range(vt_ref.shape[0]):
        vt_ref[s] = avt[:, s * seq:(s + 1) * seq].astype(vt_ref.dtype)
    for s in (4, 0, 3):
        yield from hgrn_segment(s)
    if latent:
        cos, sup, sdn = cos_ref[...], sup_ref[...], sdn_ref[...]
        qn = _rope(qn, cos, sup, sdn)
        kn = _rope(kn, cos[:, 0:KV_WIDTH], sup[:, 0:KV_WIDTH], sdn[:, 0:KV_WIDTH])
    else:
        knt = kn.T
        for s in range(kt_ref.shape[0]):
            kt_ref[s] = knt[:, s * seq:(s + 1) * seq]
    q_ref[...] = (qn * (HEAD_DIM ** -0.5 * LOG2_E)).astype(BF16)
    k_ref[...] = kn.astype(BF16)


def _in_part(tm, x2, mod, seq_len, norm_w, w_in, qw, kw, ones_q, ones_k, lb_logits, rope):
    t = x2.shape[0]
    latent = rope is not None
    per_seq = max(seq_len // tm, 1)
    row = lambda i: (i, 0)
    in_specs = [pl.BlockSpec((tm, D_MODEL), row),
                _mod_spec(mod, tm, seq_len),
                _const_spec((1, D_MODEL)),
                _const_spec((D_MODEL, D_IN)),
                _const_spec((1, HEAD_DIM)),
                _const_spec((1, HEAD_DIM)),
                _const_spec((Q_WIDTH, Q_WIDTH)),
                _const_spec((KV_WIDTH, KV_WIDTH)),
                _const_spec(lb_logits.shape)]
    args = [x2, mod.rows, norm_w, w_in, qw, kw, ones_q, ones_k, lb_logits]
    out_specs = [pl.BlockSpec((tm, ZH_WIDTH), row), pl.BlockSpec((tm, Q_WIDTH), row),
                 pl.BlockSpec((tm, KV_WIDTH), row)]
    out_shape = [jax.ShapeDtypeStruct((t, ZH_WIDTH), F32), jax.ShapeDtypeStruct((t, Q_WIDTH), BF16),
                 jax.ShapeDtypeStruct((t, KV_WIDTH), BF16)]
    if seq_len <= tm:
        assert tm % seq_len == 0
        t_spec = pl.BlockSpec((tm // seq_len, KV_WIDTH, seq_len), lambda i: (i, 0, 0))
    else:
        t_spec = pl.BlockSpec((1, KV_WIDTH, tm), lambda i: (i // per_seq, 0, i % per_seq))
    if latent:
        in_specs += [pl.BlockSpec((tm, Q_WIDTH), lambda i: (i % per_seq, 0))] * 3
        args += list(rope)
        out_specs.append(t_spec)
        out_shape.append(jax.ShapeDtypeStruct((t // seq_len, KV_WIDTH, seq_len), BF16))
    else:
        out_specs += [t_spec, t_spec]
        out_shape += [jax.ShapeDtypeStruct((t // seq_len, KV_WIDTH, seq_len), F32)] * 2
    out_specs.append(pl.BlockSpec((tm, 2 * D_MODEL), row))
    out_shape.append(jax.ShapeDtypeStruct((t, 2 * D_MODEL), BF16))
    cost = (_mm_cost(tm, D_MODEL, D_IN) + _mm_cost(tm, Q_WIDTH, Q_WIDTH) + _mm_cost(tm, KV_WIDTH, KV_WIDTH))
    scratch = [] if w_in.dtype == BF16 else [pltpu.VMEM((D_MODEL, D_IN), BF16)]
    if mod.shared:
        scratch.append(pltpu.VMEM((SUBLANES, D_IN), F32))
    stages = functools.partial(_in_stages, latent, mod.shared)
    return t // tm, _Part(stages, args, in_specs, out_specs, out_shape, scratch, cost)


def _attn_cost(lq, lks):
    per_head = sum(_mm_cost(lk, HEAD_DIM, lq) + _mm_cost(HEAD_DIM + ONES_ROWS, lk, lq) for lk in lks)
    return N_HEADS * per_head


def _attn_stages(k_transposed, lookahead, ins, outs, _):
    n_seg = len(k_transposed)
    q_ref, kv_refs = ins[0], ins[1:]
    (o_ref,) = outs
    q = q_ref[...]
    lq = q.shape[0]
    ks = [(kv_refs[2 * s][0].T if k_transposed[s] else kv_refs[2 * s][...]).astype(BF16) for s in range(n_seg)]
    vts = [kv_refs[2 * s + 1][0].astype(BF16) for s in range(n_seg)]
    vts = [[jnp.concatenate([vt[g * HEAD_DIM:(g + 1) * HEAD_DIM, :],
                             jnp.ones((ONES_ROWS, vt.shape[1]), BF16)], axis=0) for g in range(N_KV)]
           for vt in vts]
    score_cost = sum(_mm_cost(k.shape[0], HEAD_DIM, lq) for k in ks)
    value_cost = sum(_mm_cost(HEAD_DIM + ONES_ROWS, k.shape[0], lq) for k in ks)

    def scores(h):
        g = h // (N_HEADS // N_KV)
        qh = q[:, h * HEAD_DIM:(h + 1) * HEAD_DIM]
        return [_dot_nt(k[:, g * HEAD_DIM:(g + 1) * HEAD_DIM], qh) for k in ks]

    pending = []
    for h in range(lookahead):
        pending.append(scores(h))
        yield score_cost
    outs_t = []
    for h in range(N_HEADS):
        if h + lookahead < N_HEADS:
            pending.append(scores(h + lookahead))
            yield score_cost
        st = pending.pop(0)
        g = h // (N_HEADS // N_KV)
        m = functools.reduce(jnp.maximum, [jnp.max(s, axis=0, keepdims=True) for s in st])
        ot = functools.reduce(jnp.add, [_dot(vt[g], jnp.exp2(s - m).astype(BF16)) for vt, s in zip(vts, st)])
        outs_t.append(ot[0:HEAD_DIM, :] / ot[HEAD_DIM:HEAD_DIM + 1, :])
        if h == N_HEADS - 1:
            o_ref[...] = jnp.concatenate(outs_t, axis=0).T.astype(BF16)
        yield value_cost


def _attn_part(q, kv_segments, n_batch, seq_len, lookahead):
    t = q.shape[0]
    tq = ATTN_Q_TILE
    per_seq = seq_len // tq
    in_specs = [pl.BlockSpec((tq, Q_WIDTH), lambda i: (i, 0))]
    args = [q]
    for k, v_t, layer in kv_segments:
        lk = v_t.shape[-1]
        if v_t.ndim == 4:
            slab = pl.BlockSpec((1, None, KV_WIDTH, lk), lambda i, layer=layer: (i // per_seq, layer, 0, 0))
        else:
            slab = pl.BlockSpec((1, KV_WIDTH, lk), lambda i: (i // per_seq, 0, 0))
        k_spec = slab if k.ndim == v_t.ndim else pl.BlockSpec((lk, KV_WIDTH), lambda i: (i // per_seq, 0))
        in_specs += [k_spec, slab]
        args += [k, v_t]
    stages = functools.partial(_attn_stages, tuple(k.ndim == v_t.ndim for k, v_t, _ in kv_segments), lookahead)
    return n_batch * per_seq, _Part(stages, args, in_specs, [pl.BlockSpec((tq, Q_WIDTH), lambda i: (i, 0))],
                                    [jax.ShapeDtypeStruct((t, Q_WIDTH), BF16)], [],
                                    _attn_cost(tq, [v_t.shape[-1] for _, v_t, _ in kv_segments]))


def _hgrn_unit_cost():
    blk = HG_BLOCK
    return (2 * _mm_cost(blk, blk, 2 * HG_DK)
            + 2 * _mm_cost(blk, HG_DK, 2 * blk) + _mm_cost(blk, blk, HG_DV)
            + (blk // HG_PAIR) * (_mm_cost(HG_DV, HG_PAIR, 2 * HG_DK) + _mm_cost(HG_PAIR, 2 * HG_DK, HG_DV)))


def _hgrn_stages(seq_len, hps, has_s0, has_sfin, ins, outs, scrs):
    tf_ref, tb_ref, xf_ref, xb_ref, q_ref, lf_ref, lb_ref, v_ref, g_ref, nw_ref = ins[:10]
    s0_ref = ins[10] if has_s0 else None
    o_ref = outs[0]
    sfin_ref = outs[1] if has_sfin else None
    kv_scr, ss_scr, qd_scr, oi_scr, dec_scr = scrs
    n_blk = seq_len // HG_BLOCK
    n_pair = seq_len // HG_PAIR
    per_blk = HG_BLOCK // HG_CHUNK
    pairs_per_blk = HG_BLOCK // HG_PAIR
    one_row = jnp.ones((1, HG_DK), F32)

    def chunk_cumsum(t_ref, x):
        hi, lo = _split_bf16(x)
        r = _dot(t_ref[...], jnp.concatenate([hi, lo], axis=1))
        return r[:, 0:HG_DK] + r[:, HG_DK:2 * HG_DK]

    def chunk_edge_rows(x, first):
        off = 0 if first else HG_CHUNK - 1
        return [x[c * HG_CHUNK + off:c * HG_CHUNK + off + 1, :] for c in range(per_blk)]

    def spread_rows(rows):
        return jnp.concatenate([jnp.broadcast_to(r, (HG_CHUNK, HG_DK)) for r in rows], axis=0)

    units = [(blk, h) for blk in range(n_blk) for h in range(hps)]
    stage1 = []
    for blk, h in units:
        rows = slice(blk * HG_BLOCK, (blk + 1) * HG_BLOCK)
        cols = slice(h * HG_DK, (h + 1) * HG_DK)
        logf_f = lf_ref[rows, cols]
        logf_b = lb_ref[rows, cols]
        k_f = 1.0 - jnp.exp(logf_f)
        k_b = 1.0 - jnp.exp(logf_b)
        a_f = chunk_cumsum(tf_ref, logf_f)
        a_b = chunk_cumsum(tb_ref, logf_b)
        stage1.append((rows, cols, k_f, k_b, a_f, a_b))
        yield 2 * _mm_cost(HG_BLOCK, HG_BLOCK, 2 * HG_DK)
    stage2 = []
    for (blk, h), (rows, cols, k_f, k_b, a_f, a_b) in zip(units, stage1):
        q = q_ref[rows, cols]
        vb = v_ref[rows, cols].astype(BF16)
        ea_f = jnp.exp(a_f)
        ea_b = jnp.exp(a_b)
        qd_f = q * ea_f
        qd_b = q * ea_b
        kd_f = k_f * jnp.exp(-a_f)
        kd_b = k_b * jnp.exp(-a_b)
        dec_f = chunk_edge_rows(ea_f, first=False)
        dec_b = chunk_edge_rows(ea_b, first=True)
        ke_f = kd_f * spread_rows(dec_f)
        ke_b = kd_b * spread_rows(dec_b)
        even = [c % 2 == 0 for c in range(per_blk)]
        qd_pair = jnp.concatenate(
            [qd_f * spread_rows([one_row if even[c] else dec_f[c - 1] for c in range(per_blk)]),
             qd_b * spread_rows([dec_b[c + 1] if even[c] else one_row for c in range(per_blk)])], axis=1)
        ke_pair = jnp.concatenate(
            [ke_f * spread_rows([dec_f[c + 1] if even[c] else one_row for c in range(per_blk)]),
             ke_b * spread_rows([one_row if even[c] else dec_b[c - 1] for c in range(per_blk)])], axis=1).astype(BF16)
        sc_f = _dot_nt(qd_f.astype(BF16), jnp.concatenate([kd_f, ke_f], axis=0).astype(BF16))
        sc_b = _dot_nt(qd_b.astype(BF16), jnp.concatenate([kd_b, ke_b], axis=0).astype(BF16))
        for p in range(pairs_per_blk):
            pr = slice(p * HG_PAIR, (p + 1) * HG_PAIR)
            kv_scr[h, blk * pairs_per_blk + p] = _dot_tn(vb[pr, :], ke_pair[pr, :])
            dec_scr[h, blk * pairs_per_blk + p, 0:1, :] = jnp.concatenate(
                [dec_f[2 * p] * dec_f[2 * p + 1], dec_b[2 * p] * dec_b[2 * p + 1]], axis=1)
        qd_scr[h, rows, :] = qd_pair.astype(BF16)
        stage2.append((rows, cols, vb, sc_f, sc_b))
        yield 2 * _mm_cost(HG_BLOCK, HG_DK, 2 * HG_BLOCK) + pairs_per_blk * _mm_cost(HG_DV, HG_PAIR, 2 * HG_DK)
    for rows, cols, vb, sc_f, sc_b in stage2:
        s = (sc_f[:, 0:HG_BLOCK].astype(BF16) * tf_ref[...] + sc_f[:, HG_BLOCK:].astype(BF16) * xf_ref[...]
             + sc_b[:, 0:HG_BLOCK].astype(BF16) * tb_ref[...] + sc_b[:, HG_BLOCK:].astype(BF16) * xb_ref[...])
        oi_scr[rows, cols] = _dot(s, vb)
        yield _mm_cost(HG_BLOCK, HG_BLOCK, HG_DV)

    for h in range(hps):
        if has_s0:
            s_f, s_b = s0_ref[0, 0, h].T, s0_ref[0, 1, h].T
        else:
            s_f = s_b = jnp.zeros((HG_DV, HG_DK), F32)
        for pf in range(n_pair):
            pb = n_pair - 1 - pf
            ss_scr[h, pf, :, 0:HG_DK] = s_f.astype(BF16)
            s_f = dec_scr[h, pf, 0:1, 0:HG_DK] * s_f + kv_scr[h, pf, :, 0:HG_DK]
            ss_scr[h, pb, :, HG_DK:2 * HG_DK] = s_b.astype(BF16)
            s_b = dec_scr[h, pb, 0:1, HG_DK:2 * HG_DK] * s_b + kv_scr[h, pb, :, HG_DK:2 * HG_DK]
        if has_sfin:
            sfin_ref[0, 0, h] = s_f.T
            sfin_ref[0, 1, h] = s_b.T

    for blk, h in units:
        cols = slice(h * HG_DK, (h + 1) * HG_DK)
        for p in range(pairs_per_blk):
            pi = blk * pairs_per_blk + p
            rows = slice(pi * HG_PAIR, (pi + 1) * HG_PAIR)
            oi_scr[rows, cols] = oi_scr[rows, cols] + _dot_nt(qd_scr[h, rows, :], ss_scr[h, pi])
        yield pairs_per_blk * _mm_cost(HG_PAIR, 2 * HG_DK, HG_DV)

    for h in range(hps):
        cols = slice(h * HG_DK, (h + 1) * HG_DK)
        o = _rms_rows(oi_scr[:, cols], nw_ref[...])
        o_ref[:, cols] = (o * g_ref[:, cols]).astype(BF16)


def _chunk_masks():
    r = np.arange(HG_BLOCK)
    same = (r[:, None] // HG_CHUNK) == (r[None, :] // HG_CHUNK)
    lower = same & (r[None, :] <= r[:, None])
    upper = same & (r[None, :] >= r[:, None])
    same_pair = (r[:, None] // HG_PAIR) == (r[None, :] // HG_PAIR)
    cross_f = same_pair & (r[:, None] // HG_CHUNK == r[None, :] // HG_CHUNK + 1)
    return tuple(jnp.asarray(m.astype(np.float32), dtype=BF16) for m in (lower, upper, cross_f, cross_f.T))


def _hgrn_part(zh, norm_w, n_batch, seq_len, hps, s0, want_final):
    t = zh.shape[0]
    n_pair = seq_len // HG_PAIR
    width = hps * HG_DK
    n_hb = HG_HEADS // hps
    masks = _chunk_masks()

    def seg_spec(seg):
        return pl.BlockSpec((seq_len, width), lambda i: (i // n_hb, seg * n_hb + i % n_hb))

    state_spec = pl.BlockSpec((1, 2, hps, HG_DK, HG_DV), lambda i: (i // n_hb, 0, i % n_hb, 0, 0))
    mask_spec = pl.BlockSpec((HG_BLOCK, HG_BLOCK), lambda i: (0, 0))
    in_specs = [mask_spec] * len(masks) + [seg_spec(s) for s in range(5)] + [pl.BlockSpec((1, HG_DV), lambda i: (0, 0))]
    args = list(masks) + [zh] * 5 + [norm_w]
    if s0 is not None:
        states, layer = s0
        in_specs.append(pl.BlockSpec((1, None, 2, hps, HG_DK, HG_DV),
                                     lambda i: (i // n_hb, layer, 0, i % n_hb, 0, 0)))
        args.append(states)
    out_specs = [pl.BlockSpec((seq_len, width), lambda i: (i // n_hb, i % n_hb))]
    out_shape = [jax.ShapeDtypeStruct((t, HG_WIDTH), BF16)]
    if want_final:
        out_specs.append(state_spec)
        out_shape.append(jax.ShapeDtypeStruct((n_batch, 2, HG_HEADS, HG_DK, HG_DV), F32))
    scratch = [pltpu.VMEM((hps, n_pair, HG_DV, 2 * HG_DK), F32),
               pltpu.VMEM((hps, n_pair, HG_DV, 2 * HG_DK), BF16),
               pltpu.VMEM((hps, seq_len, 2 * HG_DK), BF16),
               pltpu.VMEM((seq_len, width), F32),
               pltpu.VMEM((hps, n_pair, SUBLANES, 2 * HG_DK), F32)]
    stages = functools.partial(_hgrn_stages, seq_len, hps, s0 is not None, want_final)
    cost = hps * (seq_len // HG_BLOCK) * _hgrn_unit_cost()
    return n_batch * n_hb, _Part(stages, args, in_specs, out_specs, out_shape, scratch, cost)


def _out_stages(ins, outs, _):
    x_ref, mod_ref, oh_ref, oa_ref, g_ref, who_ref, wao_ref, wout_ref, nfw_ref, wff1_ref, wff2_ref, fnw_ref = ins
    (y_ref,) = outs
    tm = x_ref.shape[0]
    m = mod_ref[0]
    g1 = m[:, 2 * D_MODEL:3 * D_MODEL]
    sh2 = m[:, 3 * D_MODEL:4 * D_MODEL]
    sc2 = m[:, 4 * D_MODEL:5 * D_MODEL]
    g2 = m[:, 5 * D_MODEL:6 * D_MODEL]
    gates = g_ref[...].astype(F32)
    from_h = yield from _dot_by_cols(oh_ref[...], who_ref, (0, HG_WIDTH), (0, D_MODEL))
    from_a = yield from _dot_by_cols(oa_ref[...], wao_ref, (0, Q_WIDTH), (0, D_MODEL))
    merged = gates[:, 0:D_MODEL] * from_h + gates[:, D_MODEL:2 * D_MODEL] * from_a
    mixed = yield from _dot_by_cols(merged.astype(BF16), wout_ref, (0, D_MODEL), (0, D_MODEL))
    x1 = x_ref[...] + g1 * mixed
    h2 = (_rms_rows(x1, nfw_ref[...]) * (1.0 + sc2) + sh2).astype(BF16)
    acc = jnp.zeros_like(x1)
    for j in range(D_FF // D_MODEL):
        cols = slice(j * D_MODEL, (j + 1) * D_MODEL)
        hj = yield from _dot_by_cols(h2, wff1_ref, (0, D_MODEL), (cols.start, cols.stop))
        hj = jnp.maximum(hj, 0.0)
        acc = acc + (yield from _dot_by_cols((hj * hj).astype(BF16), wff2_ref, (cols.start, cols.stop), (0, D_MODEL)))
    y_ref[...] = _rms_rows(x1 + g2 * acc, fnw_ref[...])


def _out_part(tm, x2, mod, seq_len, oh, oa, gates, w_ho, w_ao, w_out, nfw, w_ff1, w_ff2, fnw):
    t = x2.shape[0]
    row = lambda i: (i, 0)
    in_specs = [pl.BlockSpec((tm, D_MODEL), row),
                _mod_spec(mod, tm, seq_len),
                pl.BlockSpec((tm, HG_WIDTH), row),
                pl.BlockSpec((tm, Q_WIDTH), row),
                pl.BlockSpec((tm, 2 * D_MODEL), row),
                _const_spec((HG_WIDTH, D_MODEL)),
                _const_spec((Q_WIDTH, D_MODEL)),
                _const_spec((D_MODEL, D_MODEL)),
                _const_spec((1, D_MODEL)),
                _const_spec((D_MODEL, D_FF)),
                _const_spec((D_FF, D_MODEL)),
                _const_spec((1, D_MODEL))]
    args = [x2, mod.rows, oh, oa, gates, w_ho, w_ao, w_out, nfw, w_ff1, w_ff2, fnw]
    cost = _mm_cost(tm, HG_WIDTH + Q_WIDTH, D_MODEL) + _mm_cost(tm, D_MODEL, D_MODEL) + 2 * _mm_cost(tm, D_MODEL, D_FF)
    return t // tm, _Part(_out_stages, args, in_specs, [pl.BlockSpec((tm, D_MODEL), row)],
                          [jax.ShapeDtypeStruct((t, D_MODEL), F32)], [], cost)


def _cast_stages(ins, outs, _):
    for src, dst in zip(ins, outs):
        dst[...] = src[...].astype(dst.dtype)
    yield 1


def _cast_part(arrays, n_steps):
    in_specs, out_specs, out_shape = [], [], []
    for a in arrays:
        rows, cols = a.shape
        assert rows % (n_steps * MXU_ROWS) == 0
        spec = pl.BlockSpec((rows // n_steps, cols), lambda i: (i, 0))
        in_specs.append(spec)
        out_specs.append(spec)
        out_shape.append(jax.ShapeDtypeStruct(a.shape, BF16))
    return n_steps, _Part(_cast_stages, list(arrays), in_specs, out_specs, out_shape, [], 1)


def _rope_tables(n_tokens):
    rows = n_tokens // GRID_W
    row = np.repeat(np.arange(rows, dtype=np.float32), GRID_W)
    col = np.tile(np.arange(GRID_W, dtype=np.float32), rows)
    axis_dim = HEAD_DIM // 2
    freqs = (ROPE_THETA ** (-np.arange(0, axis_dim, 2, dtype=np.float32) / axis_dim)).astype(np.float32)
    ang_r = row[:, None] * freqs
    ang_c = col[:, None] * freqs
    cr, sr, cc, sc = np.cos(ang_r), np.sin(ang_r), np.cos(ang_c), np.sin(ang_c)
    zero = np.zeros_like(sr)
    cos = np.concatenate([cr, cr, cc, cc], axis=-1)
    s_up = np.concatenate([-sr, zero, -sc, zero], axis=-1)
    s_dn = np.concatenate([zero, sr, zero, sc], axis=-1)
    return tuple(jnp.asarray(np.tile(a, (1, N_HEADS)), dtype=F32) for a in (cos, s_up, s_dn))


def _to_slab(kv):
    n, depth, length = kv.shape[:3]
    return jnp.transpose(kv, (0, 1, 3, 4, 2)).reshape(n, depth, KV_WIDTH, length)


def _from_slab(slab):
    n, _, length = slab.shape
    return jnp.transpose(slab.reshape(n, N_KV, HEAD_DIM, length), (0, 3, 1, 2))


def _block_ones(width):
    idx = np.arange(width) // HEAD_DIM
    return jnp.asarray((idx[:, None] == idx[None, :]).astype(np.float32), dtype=BF16)


def _same_steps(*counted_parts):
    steps = {n for n, _ in counted_parts}
    assert len(steps) == 1, steps
    return steps.pop(), [p for _, p in counted_parts]


def kernel(x_prompt, x_sample, cache_k, cache_v, state_hgrn, c, c_ctx, w_ada, b_ada, norm_mix_w, w_in, q_norm_w, k_norm_w, hgrn_lb_logits, hgrn_norm_w, w_hgrn_out, w_attn_out, w_out, norm_ffn_w, w_ff1, w_ff2, final_norm_w):
    n_p, l_p, _ = x_prompt.shape
    n_s, l_s, _ = x_sample.shape
    layer = 0

    mod = _modulation(c_ctx[None, :], c, w_ada[layer], b_ada[layer][None, :])
    mod_p = _Mod(mod, 0, True)
    mod_s = _Mod(mod, 1, False)

    in_small = (q_norm_w[layer][None, :], k_norm_w[layer][None, :],
                _block_ones(Q_WIDTH), _block_ones(KV_WIDTH), hgrn_lb_logits)
    nmw = norm_mix_w[layer][None, :]
    hnw = hgrn_norm_w[layer][None, :]
    xp2 = x_prompt.reshape(n_p * l_p, D_MODEL)
    xs2 = x_sample.reshape(n_s * l_s, D_MODEL)

    n_in, in_p = _in_part(MIX_TOKEN_TILE, xp2, mod_p, l_p, nmw, w_in[layer], *in_small, None)
    steps, parts = _same_steps((n_in, in_p), _cast_part(
        [w_in[layer], w_hgrn_out[layer], w_attn_out[layer], w_out[layer], w_ff1[layer], w_ff2[layer]], n_in))
    (zh_p, q_p, k_p, kt_p, vt_p, gates_p), (w_in_b, w_ho_b, w_ao_b, w_out_b, w_ff1_b, w_ff2_b) = _launch(
        "in_ctx", steps, parts)
    out_w = (w_ho_b, w_ao_b, w_out_b, norm_ffn_w[layer][None, :], w_ff1_b, w_ff2_b, final_norm_w[None, :])

    steps, parts = _same_steps(
        _attn_part(q_p, [(k_p, vt_p, None)], n_p, l_p, N_HEADS),
        _in_part(MIX_TOKEN_TILE, xs2, mod_s, l_s, nmw, w_in_b, *in_small, _rope_tables(l_s)),
        _hgrn_part(zh_p, hnw, n_p, l_p, HG_HEADS, None, True))
    (oa_p,), (zh_s, q_s, k_s, vt_s, gates_s), (oh_p, s_fin) = _launch("in_latent_mix_ctx", steps, parts,
                                                                      MIX_PACES_IN)

    segs = [(_to_slab(cache_k), _to_slab(cache_v), layer), (k_s, vt_s, None)]
    steps, parts = _same_steps(
        _out_part(MIX_TOKEN_TILE, xp2, mod_p, l_p, oh_p, oa_p, gates_p, *out_w),
        _attn_part(q_s, segs, n_s, l_s, ATTN_LOOKAHEAD),
        _hgrn_part(zh_s, hnw, n_s, l_s, 1, (state_hgrn, layer), False))
    (y_p,), (oa_s,), (oh_s,) = _launch("out_ctx_mix_latent", steps, parts, MIX_PACES_OUT)

    steps, parts = _same_steps(_out_part(SOLO_TOKEN_TILE, xs2, mod_s, l_s, oh_s, oa_s, gates_s, *out_w))
    ((y_s,),) = _launch("out_latent", steps, parts)

    new_k = _from_slab(kt_p)[:, None]
    new_v = _from_slab(vt_p)[:, None]
    new_s = s_fin.reshape(n_p, 1, 2, HG_HEADS, HG_DK, HG_DV)
    return (y_p.reshape(n_p, l_p, D_MODEL), y_s.reshape(n_s, l_s, D_MODEL), new_k, new_v, new_s)
```

```python
import functools
from typing import Any, Callable, NamedTuple

import numpy as np
import jax
import jax.numpy as jnp
from jax import lax
from jax.experimental import pallas as pl
from jax.experimental.pallas import tpu as pltpu

F32 = jnp.float32
BF16 = jnp.bfloat16

D_MODEL = 1024
GRID_W = 64
EPS = 1e-6
HG_HEADS = 4
HG_DK = 128
HG_DV = 128
HG_WIDTH = HG_HEADS * HG_DK
HG_CHUNK = 32
HG_PAIR = 2 * HG_CHUNK
N_HEADS = 8
N_KV = 2
HEAD_DIM = 64
Q_WIDTH = N_HEADS * HEAD_DIM
KV_WIDTH = N_KV * HEAD_DIM
ROPE_THETA = 10000.0
D_FF = 4 * D_MODEL
N_MOD = 6
ZH_WIDTH = 5 * HG_WIDTH
D_IN = ZH_WIDTH + Q_WIDTH + 2 * KV_WIDTH + 2 * D_MODEL

V7X_VMEM_LIMIT_BYTES = 56 * 1024 * 1024
SUBLANES = 8
LANES = 128
MXU_ROWS = 16
MXU_TILE = 256
ROPE_PAIR_LANES = HEAD_DIM // 4
HG_BLOCK = 256
COND_ROWS = 16
ONES_ROWS = 16
LOG2_E = float(np.log2(np.e))
MIX_TOKEN_TILE = 256
SOLO_TOKEN_TILE = 512
ADALN_ROW_TILE = 256
W_CAST_ROWS = 128
DOT_COLS = 512
ATTN_Q_TILE = 256
ATTN_LOOKAHEAD = 2
MIX_PACES_IN = (1.0, 1.0, 0.8)
MIX_PACES_OUT = (1.0, 1.0, 0.9)


def _dot(a, b):
    return jnp.dot(a, b, preferred_element_type=F32)


def _dot_nt(a, b):
    return lax.dot_general(a, b, (((1,), (1,)), ((), ())), preferred_element_type=F32)


def _dot_tn(a, b):
    return lax.dot_general(a, b, (((0,), (0,)), ((), ())), preferred_element_type=F32)


def _mm_cost(m, k, n):
    return (m // MXU_ROWS) * -(-k // MXU_TILE) * -(-n // MXU_TILE)


def _dot_by_cols(a, w_ref, rows, cols):
    pieces = []
    for c in range(cols[0], cols[1], DOT_COLS):
        end = min(c + DOT_COLS, cols[1])
        pieces.append(_dot(a, w_ref[rows[0]:rows[1], c:end]))
        yield _mm_cost(a.shape[0], rows[1] - rows[0], end - c)
    return pieces[0] if len(pieces) == 1 else jnp.concatenate(pieces, axis=1)


def _split_bf16(x):
    hi = x.astype(BF16)
    lo = (x - hi.astype(F32)).astype(BF16)
    return hi, lo


def _const_spec(shape):
    nd = len(shape)
    return pl.BlockSpec(shape, lambda *_: (0,) * nd, pipeline_mode=pl.Buffered(1))


class _Part(NamedTuple):
    stages: Callable[..., Any]
    args: list
    in_specs: list
    out_specs: list
    out_shape: list
    scratch: list
    cost: int


def _interleave(gens, totals):
    done = [0] * len(gens)
    live = list(range(len(gens)))
    while live:
        i = min(live, key=lambda j: done[j] / totals[j])
        try:
            done[i] += next(gens[i])
        except StopIteration:
            live.remove(i)


def _launch(name, n_steps, parts, paces=None):
    paces = paces or [1.0] * len(parts)
    n_in = [len(p.args) for p in parts]
    n_out = [len(p.out_shape) for p in parts]
    n_scr = [len(p.scratch) for p in parts]

    def body(*refs):
        refs = list(refs)
        ins = [[refs.pop(0) for _ in range(n)] for n in n_in]
        outs = [[refs.pop(0) for _ in range(n)] for n in n_out]
        scrs = [[refs.pop(0) for _ in range(n)] for n in n_scr]
        _interleave([p.stages(i, o, s) for p, i, o, s in zip(parts, ins, outs, scrs)],
                    [p.cost * pace for p, pace in zip(parts, paces)])

    flat = pl.pallas_call(
        body,
        grid=(n_steps,),
        in_specs=[s for p in parts for s in p.in_specs],
        out_specs=[s for p in parts for s in p.out_specs],
        out_shape=[s for p in parts for s in p.out_shape],
        scratch_shapes=[s for p in parts for s in p.scratch],
        compiler_params=pltpu.CompilerParams(dimension_semantics=("arbitrary",),
                                             vmem_limit_bytes=V7X_VMEM_LIMIT_BYTES),
        name=name,
    )(*[a for p in parts for a in p.args])
    flat = list(flat)
    return [[flat.pop(0) for _ in range(n)] for n in n_out]


class _Mod(NamedTuple):
    rows: Any
    first: int
    shared: bool


def _mod_spec(mod, tile, seq_len, tile_of_step=lambda i: i):
    if mod.shared:
        return pl.BlockSpec((1, 1, N_MOD * D_MODEL), lambda i: (mod.first, 0, 0))
    assert seq_len % tile == 0
    per_seq = seq_len // tile
    return pl.BlockSpec((1, 1, N_MOD * D_MODEL), lambda i: (mod.first + tile_of_step(i) // per_seq, 0, 0))


def _mod_kernel(cctx_ref, c_ref, w_ref, b_ref, o_ref, cond_scr):
    n_req = c_ref.shape[0]
    cond_scr[...] = jnp.zeros_like(cond_scr)
    cond_scr[0:1, :] = cctx_ref[...]
    cond_scr[1:1 + n_req, :] = c_ref[...]
    c = cond_scr[...]
    x = c * jax.nn.sigmoid(c)
    x_hi, x_lo = _split_bf16(x)
    w = w_ref[...].astype(BF16)
    part = _dot(x_hi, w) + _dot(x_lo, w)

    @pl.when(pl.program_id(0) == 0)
    def _():
        o_ref[:, 0, :] = b_ref[...] + part

    @pl.when(pl.program_id(0) > 0)
    def _():
        o_ref[:, 0, :] += part


def _modulation(c_ctx, c, w_ada, b_ada):
    n = N_MOD * D_MODEL
    tk = ADALN_ROW_TILE
    n_req = c.shape[0]
    assert 1 + n_req <= COND_ROWS
    return pl.pallas_call(
        _mod_kernel,
        grid=(D_MODEL // tk,),
        in_specs=[pl.BlockSpec((1, tk), lambda j: (0, j)),
                  pl.BlockSpec((n_req, tk), lambda j: (0, j)),
                  pl.BlockSpec((tk, n), lambda j: (j, 0)),
                  pl.BlockSpec((1, n), lambda j: (0, 0))],
        out_specs=pl.BlockSpec((COND_ROWS, 1, n), lambda j: (0, 0, 0)),
        out_shape=jax.ShapeDtypeStruct((COND_ROWS, 1, n), F32),
        scratch_shapes=[pltpu.VMEM((COND_ROWS, tk), F32)],
        compiler_params=pltpu.CompilerParams(dimension_semantics=("arbitrary",),
                                             vmem_limit_bytes=V7X_VMEM_LIMIT_BYTES),
        name="adaln_mod",
    )(c_ctx, c, w_ada, b_ada)


def _rms_rows(x, w):
    return x * lax.rsqrt(jnp.mean(x * x, axis=-1, keepdims=True) + EPS) * w


def _head_sumsq(a, ones_ref):
    return _dot((a * a).astype(BF16), ones_ref[...])


def _head_rms(a, sumsq, w):
    return a * lax.rsqrt(sumsq * (1.0 / HEAD_DIM) + EPS) * w


def _forget_lower_bound(logits):
    e = jnp.exp(logits - jnp.max(logits, axis=1, keepdims=True))
    return e[:, 0, :] / jnp.sum(e, axis=1)


def _rope(x, cos, s_up, s_dn):
    cols = []
    for j in range(x.shape[1] // LANES):
        sl = slice(j * LANES, (j + 1) * LANES)
        xj = x[:, sl]
        cols.append(xj * cos[:, sl] + pltpu.roll(xj, LANES - ROPE_PAIR_LANES, 1) * s_up[:, sl]
                    + pltpu.roll(xj, ROPE_PAIR_LANES, 1) * s_dn[:, sl])
    return jnp.concatenate(cols, axis=1) if len(cols) > 1 else cols[0]


def _in_stages(latent, ins, outs, scrs):
    x_ref, mod_ref, nw_ref, w_ref, qw_ref, kw_ref, oq_ref, ok_ref, lbl_ref = ins[:9]
    if scrs:
        (w_bf_ref,) = scrs
        rows = W_CAST_ROWS

        @pl.when(pl.program_id(0) == 0)
        def _():
            def cast_rows(r, carry):
                sl = pl.ds(pl.multiple_of(r * rows, rows), rows)
                w_bf_ref[sl, :] = w_ref[sl, :].astype(BF16)
                return carry

            lax.fori_loop(0, w_ref.shape[0] // rows, cast_rows, 0)

        w_ref = w_bf_ref
    if latent:
        cos_ref, sup_ref, sdn_ref = ins[9:]
        zh_ref, q_ref, k_ref, vt_ref, g_ref = outs
    else:
        zh_ref, q_ref, k_ref, kt_ref, vt_ref, g_ref = outs
    tm = x_ref.shape[0]
    m = mod_ref[0]
    sh1 = m[:, 0:D_MODEL]
    sc1 = m[:, D_MODEL:2 * D_MODEL]
    h = _rms_rows(x_ref[...], nw_ref[...]) * (1.0 + sc1) + sh1
    hb = h.astype(BF16)
    c0 = ZH_WIDTH
    c1 = c0 + Q_WIDTH
    c2 = c1 + KV_WIDTH
    c3 = c2 + KV_WIDTH
    aq = _dot(hb, w_ref[:, c0:c1])
    ak = _dot(hb, w_ref[:, c1:c2])
    av = _dot(hb, w_ref[:, c2:c3])
    yield _mm_cost(tm, D_MODEL, c3 - c0)
    gl = yield from _dot_by_cols(hb, w_ref, (0, D_MODEL), (c3, D_IN))
    ssq = _head_sumsq(aq, oq_ref)
    ssk = _head_sumsq(ak, ok_ref)
    yield _mm_cost(tm, Q_WIDTH, Q_WIDTH) + _mm_cost(tm, KV_WIDTH, KV_WIDTH)
    lb = _forget_lower_bound(lbl_ref[...])

    def hgrn_segment(s):
        z = yield from _dot_by_cols(hb, w_ref, (0, D_MODEL), (s * HG_WIDTH, (s + 1) * HG_WIDTH))
        if s == 0:
            z = z * HG_DK ** -0.5
        elif s in (1, 2):
            lbv = lb[s - 1:s, :]
            z = jnp.log(lbv + (1.0 - lbv) * jax.nn.sigmoid(z))
        elif s == 4:
            z = z * jax.nn.sigmoid(z)
        zh_ref[:, s * HG_WIDTH:(s + 1) * HG_WIDTH] = z

    yield from hgrn_segment(1)
    yield from hgrn_segment(2)
    g_ref[...] = jax.nn.sigmoid(gl).astype(BF16)
    qn = _head_rms(aq, ssq, jnp.tile(qw_ref[...], (1, N_HEADS)))
    kn = _head_rms(ak, ssk, jnp.tile(kw_ref[...], (1, N_KV)))
    seq = vt_ref.shape[2]
    avt = av.T
    for s in range(vt_ref.shape[0]):
        vt_ref[s] = avt[:, s * seq:(s + 1) * seq].astype(vt_ref.dtype)
    for s in (4, 0, 3):
        yield from hgrn_segment(s)
    if latent:
        cos, sup, sdn = cos_ref[...], sup_ref[...], sdn_ref[...]
        qn = _rope(qn, cos, sup, sdn)
        kn = _rope(kn, cos[:, 0:KV_WIDTH], sup[:, 0:KV_WIDTH], sdn[:, 0:KV_WIDTH])
    else:
        knt = kn.T
        for s in range(kt_ref.shape[0]):
            kt_ref[s] = knt[:, s * seq:(s + 1) * seq]
    q_ref[...] = (qn * (HEAD_DIM ** -0.5 * LOG2_E)).astype(BF16)
    k_ref[...] = kn.astype(BF16)


def _in_part(tm, x2, mod, seq_len, norm_w, w_in, qw, kw, ones_q, ones_k, lb_logits, rope):
    t = x2.shape[0]
    latent = rope is not None
    per_seq = max(seq_len // tm, 1)
    row = lambda i: (i, 0)
    in_specs = [pl.BlockSpec((tm, D_MODEL), row),
                _mod_spec(mod, tm, seq_len),
                _const_spec((1, D_MODEL)),
                _const_spec((D_MODEL, D_IN)),
                _const_spec((1, HEAD_DIM)),
                _const_spec((1, HEAD_DIM)),
                _const_spec((Q_WIDTH, Q_WIDTH)),
                _const_spec((KV_WIDTH, KV_WIDTH)),
                _const_spec(lb_logits.shape)]
    args = [x2, mod.rows, norm_w, w_in, qw, kw, ones_q, ones_k, lb_logits]
    out_specs = [pl.BlockSpec((tm, ZH_WIDTH), row), pl.BlockSpec((tm, Q_WIDTH), row),
                 pl.BlockSpec((tm, KV_WIDTH), row)]
    out_shape = [jax.ShapeDtypeStruct((t, ZH_WIDTH), F32), jax.ShapeDtypeStruct((t, Q_WIDTH), BF16),
                 jax.ShapeDtypeStruct((t, KV_WIDTH), BF16)]
    if seq_len <= tm:
        assert tm % seq_len == 0
        t_spec = pl.BlockSpec((tm // seq_len, KV_WIDTH, seq_len), lambda i: (i, 0, 0))
    else:
        t_spec = pl.BlockSpec((1, KV_WIDTH, tm), lambda i: (i // per_seq, 0, i % per_seq))
    if latent:
        in_specs += [pl.BlockSpec((tm, Q_WIDTH), lambda i: (i % per_seq, 0))] * 3
        args += list(rope)
        out_specs.append(t_spec)
        out_shape.append(jax.ShapeDtypeStruct((t // seq_len, KV_WIDTH, seq_len), BF16))
    else:
        out_specs += [t_spec, t_spec]
        out_shape += [jax.ShapeDtypeStruct((t // seq_len, KV_WIDTH, seq_len), F32)] * 2
    out_specs.append(pl.BlockSpec((tm, 2 * D_MODEL), row))
    out_shape.append(jax.ShapeDtypeStruct((t, 2 * D_MODEL), BF16))
    cost = (_mm_cost(tm, D_MODEL, D_IN) + _mm_cost(tm, Q_WIDTH, Q_WIDTH) + _mm_cost(tm, KV_WIDTH, KV_WIDTH))
    scratch = [] if w_in.dtype == BF16 else [pltpu.VMEM((D_MODEL, D_IN), BF16)]
    return t // tm, _Part(functools.partial(_in_stages, latent), args, in_specs, out_specs, out_shape, scratch, cost)


def _attn_cost(lq, lks):
    per_head = sum(_mm_cost(lk, HEAD_DIM, lq) + _mm_cost(HEAD_DIM + ONES_ROWS, lk, lq) for lk in lks)
    return N_HEADS * per_head


def _attn_stages(k_transposed, lookahead, ins, outs, _):
    n_seg = len(k_transposed)
    q_ref, kv_refs = ins[0], ins[1:]
    (o_ref,) = outs
    q = q_ref[...]
    lq = q.shape[0]
    ks = [(kv_refs[2 * s][0].T if k_transposed[s] else kv_refs[2 * s][...]).astype(BF16) for s in range(n_seg)]
    vts = [kv_refs[2 * s + 1][0].astype(BF16) for s in range(n_seg)]
    vts = [[jnp.concatenate([vt[g * HEAD_DIM:(g + 1) * HEAD_DIM, :],
                             jnp.ones((ONES_ROWS, vt.shape[1]), BF16)], axis=0) for g in range(N_KV)]
           for vt in vts]
    score_cost = sum(_mm_cost(k.shape[0], HEAD_DIM, lq) for k in ks)
    value_cost = sum(_mm_cost(HEAD_DIM + ONES_ROWS, k.shape[0], lq) for k in ks)

    def scores(h):
        g = h // (N_HEADS // N_KV)
        qh = q[:, h * HEAD_DIM:(h + 1) * HEAD_DIM]
        return [_dot_nt(k[:, g * HEAD_DIM:(g + 1) * HEAD_DIM], qh) for k in ks]

    pending = []
    for h in range(lookahead):
        pending.append(scores(h))
        yield score_cost
    outs_t = []
    for h in range(N_HEADS):
        if h + lookahead < N_HEADS:
            pending.append(scores(h + lookahead))
            yield score_cost
        st = pending.pop(0)
        g = h // (N_HEADS // N_KV)
        m = functools.reduce(jnp.maximum, [jnp.max(s, axis=0, keepdims=True) for s in st])
        ot = functools.reduce(jnp.add, [_dot(vt[g], jnp.exp2(s - m).astype(BF16)) for vt, s in zip(vts, st)])
        outs_t.append(ot[0:HEAD_DIM, :] / ot[HEAD_DIM:HEAD_DIM + 1, :])
        if h == N_HEADS - 1:
            o_ref[...] = jnp.concatenate(outs_t, axis=0).T.astype(BF16)
        yield value_cost


def _attn_part(q, kv_segments, n_batch, seq_len, lookahead):
    t = q.shape[0]
    tq = ATTN_Q_TILE
    per_seq = seq_len // tq
    in_specs = [pl.BlockSpec((tq, Q_WIDTH), lambda i: (i, 0))]
    args = [q]
    for k, v_t, layer in kv_segments:
        lk = v_t.shape[-1]
        if v_t.ndim == 4:
            slab = pl.BlockSpec((1, None, KV_WIDTH, lk), lambda i, layer=layer: (i // per_seq, layer, 0, 0))
        else:
            slab = pl.BlockSpec((1, KV_WIDTH, lk), lambda i: (i // per_seq, 0, 0))
        k_spec = slab if k.ndim == v_t.ndim else pl.BlockSpec((lk, KV_WIDTH), lambda i: (i // per_seq, 0))
        in_specs += [k_spec, slab]
        args += [k, v_t]
    stages = functools.partial(_attn_stages, tuple(k.ndim == v_t.ndim for k, v_t, _ in kv_segments), lookahead)
    return n_batch * per_seq, _Part(stages, args, in_specs, [pl.BlockSpec((tq, Q_WIDTH), lambda i: (i, 0))],
                                    [jax.ShapeDtypeStruct((t, Q_WIDTH), BF16)], [],
                                    _attn_cost(tq, [v_t.shape[-1] for _, v_t, _ in kv_segments]))


def _hgrn_unit_cost():
    blk = HG_BLOCK
    return (2 * _mm_cost(blk, blk, 2 * HG_DK)
            + 2 * _mm_cost(blk, HG_DK, 2 * blk) + _mm_cost(blk, blk, HG_DV)
            + (blk // HG_PAIR) * (_mm_cost(HG_DV, HG_PAIR, 2 * HG_DK) + _mm_cost(HG_PAIR, 2 * HG_DK, HG_DV)))


def _hgrn_stages(seq_len, hps, has_s0, has_sfin, ins, outs, scrs):
    tf_ref, tb_ref, xf_ref, xb_ref, q_ref, lf_ref, lb_ref, v_ref, g_ref, nw_ref = ins[:10]
    s0_ref = ins[10] if has_s0 else None
    o_ref = outs[0]
    sfin_ref = outs[1] if has_sfin else None
    kv_scr, ss_scr, qd_scr, oi_scr, dec_scr = scrs
    n_blk = seq_len // HG_BLOCK
    n_pair = seq_len // HG_PAIR
    per_blk = HG_BLOCK // HG_CHUNK
    pairs_per_blk = HG_BLOCK // HG_PAIR
    one_row = jnp.ones((1, HG_DK), F32)

    def chunk_cumsum(t_ref, x):
        hi, lo = _split_bf16(x)
        r = _dot(t_ref[...], jnp.concatenate([hi, lo], axis=1))
        return r[:, 0:HG_DK] + r[:, HG_DK:2 * HG_DK]

    def chunk_edge_rows(x, first):
        off = 0 if first else HG_CHUNK - 1
        return [x[c * HG_CHUNK + off:c * HG_CHUNK + off + 1, :] for c in range(per_blk)]

    def spread_rows(rows):
        return jnp.concatenate([jnp.broadcast_to(r, (HG_CHUNK, HG_DK)) for r in rows], axis=0)

    units = [(blk, h) for blk in range(n_blk) for h in range(hps)]

    def stage_one(blk, h):
        rows = slice(blk * HG_BLOCK, (blk + 1) * HG_BLOCK)
        cols = slice(h * HG_DK, (h + 1) * HG_DK)
        logf_f = lf_ref[rows, cols]
        logf_b = lb_ref[rows, cols]
        k_f = 1.0 - jnp.exp(logf_f)
        k_b = 1.0 - jnp.exp(logf_b)
        a_f = chunk_cumsum(tf_ref, logf_f)
        a_b = chunk_cumsum(tb_ref, logf_b)
        return rows, cols, k_f, k_b, a_f, a_b

    def stage_two(blk, h, rows, cols, k_f, k_b, a_f, a_b):
        q = q_ref[rows, cols]
        vb = v_ref[rows, cols].astype(BF16)
        ea_f = jnp.exp(a_f)
        ea_b = jnp.exp(a_b)
        qd_f = q * ea_f
        qd_b = q * ea_b
        kd_f = k_f * jnp.exp(-a_f)
        kd_b = k_b * jnp.exp(-a_b)
        dec_f = chunk_edge_rows(ea_f, first=False)
        dec_b = chunk_edge_rows(ea_b, first=True)
        ke_f = kd_f * spread_rows(dec_f)
        ke_b = kd_b * spread_rows(dec_b)
        even = [c % 2 == 0 for c in range(per_blk)]
        qd_pair = jnp.concatenate(
            [qd_f * spread_rows([one_row if even[c] else dec_f[c - 1] for c in range(per_blk)]),
             qd_b * spread_rows([dec_b[c + 1] if even[c] else one_row for c in range(per_blk)])], axis=1)
        ke_pair = jnp.concatenate(
            [ke_f * spread_rows([dec_f[c + 1] if even[c] else one_row for c in range(per_blk)]),
             ke_b * spread_rows([one_row if even[c] else dec_b[c - 1] for c in range(per_blk)])], axis=1).astype(BF16)
        sc_f = _dot_nt(qd_f.astype(BF16), jnp.concatenate([kd_f, ke_f], axis=0).astype(BF16))
        sc_b = _dot_nt(qd_b.astype(BF16), jnp.concatenate([kd_b, ke_b], axis=0).astype(BF16))
        for p in range(pairs_per_blk):
            pr = slice(p * HG_PAIR, (p + 1) * HG_PAIR)
            kv_scr[h, blk * pairs_per_blk + p] = _dot_tn(vb[pr, :], ke_pair[pr, :])
            dec_scr[h, blk * pairs_per_blk + p, 0:1, :] = jnp.concatenate(
                [dec_f[2 * p] * dec_f[2 * p + 1], dec_b[2 * p] * dec_b[2 * p + 1]], axis=1)
        qd_scr[h, rows, :] = qd_pair.astype(BF16)
        return rows, cols, vb, sc_f, sc_b

    def stage_three(rows, cols, vb, sc_f, sc_b):
        s = (sc_f[:, 0:HG_BLOCK].astype(BF16) * tf_ref[...] + sc_f[:, HG_BLOCK:].astype(BF16) * xf_ref[...]
             + sc_b[:, 0:HG_BLOCK].astype(BF16) * tb_ref[...] + sc_b[:, HG_BLOCK:].astype(BF16) * xb_ref[...])
        oi_scr[rows, cols] = _dot(s, vb)

    after_one, after_two = {}, {}
    for t in range(len(units) + 2):
        if t < len(units):
            after_one[t] = stage_one(*units[t])
            yield 2 * _mm_cost(HG_BLOCK, HG_BLOCK, 2 * HG_DK)
        if 0 <= t - 1 < len(units):
            after_two[t - 1] = stage_two(*units[t - 1], *after_one.pop(t - 1))
            yield 2 * _mm_cost(HG_BLOCK, HG_DK, 2 * HG_BLOCK) + pairs_per_blk * _mm_cost(HG_DV, HG_PAIR, 2 * HG_DK)
        if 0 <= t - 2 < len(units):
            stage_three(*after_two.pop(t - 2))
            yield _mm_cost(HG_BLOCK, HG_BLOCK, HG_DV)

    for h in range(hps):
        if has_s0:
            s_f, s_b = s0_ref[0, 0, h].T, s0_ref[0, 1, h].T
        else:
            s_f = s_b = jnp.zeros((HG_DV, HG_DK), F32)
        for pf in range(n_pair):
            pb = n_pair - 1 - pf
            ss_scr[h, pf, :, 0:HG_DK] = s_f.astype(BF16)
            s_f = dec_scr[h, pf, 0:1, 0:HG_DK] * s_f + kv_scr[h, pf, :, 0:HG_DK]
            ss_scr[h, pb, :, HG_DK:2 * HG_DK] = s_b.astype(BF16)
            s_b = dec_scr[h, pb, 0:1, HG_DK:2 * HG_DK] * s_b + kv_scr[h, pb, :, HG_DK:2 * HG_DK]
        if has_sfin:
            sfin_ref[0, 0, h] = s_f.T
            sfin_ref[0, 1, h] = s_b.T

    for blk, h in units:
        cols = slice(h * HG_DK, (h + 1) * HG_DK)
        for p in range(pairs_per_blk):
            pi = blk * pairs_per_blk + p
            rows = slice(pi * HG_PAIR, (pi + 1) * HG_PAIR)
            oi_scr[rows, cols] = oi_scr[rows, cols] + _dot_nt(qd_scr[h, rows, :], ss_scr[h, pi])
        yield pairs_per_blk * _mm_cost(HG_PAIR, 2 * HG_DK, HG_DV)

    for h in range(hps):
        cols = slice(h * HG_DK, (h + 1) * HG_DK)
        o = _rms_rows(oi_scr[:, cols], nw_ref[...])
        o_ref[:, cols] = (o * g_ref[:, cols]).astype(BF16)


def _chunk_masks():
    r = np.arange(HG_BLOCK)
    same = (r[:, None] // HG_CHUNK) == (r[None, :] // HG_CHUNK)
    lower = same & (r[None, :] <= r[:, None])
    upper = same & (r[None, :] >= r[:, None])
    same_pair = (r[:, None] // HG_PAIR) == (r[None, :] // HG_PAIR)
    cross_f = same_pair & (r[:, None] // HG_CHUNK == r[None, :] // HG_CHUNK + 1)
    return tuple(jnp.asarray(m.astype(np.float32), dtype=BF16) for m in (lower, upper, cross_f, cross_f.T))


def _hgrn_part(zh, norm_w, n_batch, seq_len, hps, s0, want_final):
    t = zh.shape[0]
    n_pair = seq_len // HG_PAIR
    width = hps * HG_DK
    n_hb = HG_HEADS // hps
    masks = _chunk_masks()

    def seg_spec(seg):
        return pl.BlockSpec((seq_len, width), lambda i: (i // n_hb, seg * n_hb + i % n_hb))

    state_spec = pl.BlockSpec((1, 2, hps, HG_DK, HG_DV), lambda i: (i // n_hb, 0, i % n_hb, 0, 0))
    mask_spec = pl.BlockSpec((HG_BLOCK, HG_BLOCK), lambda i: (0, 0))
    in_specs = [mask_spec] * len(masks) + [seg_spec(s) for s in range(5)] + [pl.BlockSpec((1, HG_DV), lambda i: (0, 0))]
    args = list(masks) + [zh] * 5 + [norm_w]
    if s0 is not None:
        states, layer = s0
        in_specs.append(pl.BlockSpec((1, None, 2, hps, HG_DK, HG_DV),
                                     lambda i: (i // n_hb, layer, 0, i % n_hb, 0, 0)))
        args.append(states)
    out_specs = [pl.BlockSpec((seq_len, width), lambda i: (i // n_hb, i % n_hb))]
    out_shape = [jax.ShapeDtypeStruct((t, HG_WIDTH), BF16)]
    if want_final:
        out_specs.append(state_spec)
        out_shape.append(jax.ShapeDtypeStruct((n_batch, 2, HG_HEADS, HG_DK, HG_DV), F32))
    scratch = [pltpu.VMEM((hps, n_pair, HG_DV, 2 * HG_DK), F32),
               pltpu.VMEM((hps, n_pair, HG_DV, 2 * HG_DK), BF16),
               pltpu.VMEM((hps, seq_len, 2 * HG_DK), BF16),
               pltpu.VMEM((seq_len, width), F32),
               pltpu.VMEM((hps, n_pair, SUBLANES, 2 * HG_DK), F32)]
    stages = functools.partial(_hgrn_stages, seq_len, hps, s0 is not None, want_final)
    cost = hps * (seq_len // HG_BLOCK) * _hgrn_unit_cost()
    return n_batch * n_hb, _Part(stages, args, in_specs, out_specs, out_shape, scratch, cost)


def _out_stages(ins, outs, _):
    x_ref, mod_ref, oh_ref, oa_ref, g_ref, who_ref, wao_ref, wout_ref, nfw_ref, wff1_ref, wff2_ref, fnw_ref = ins
    (y_ref,) = outs
    tm = x_ref.shape[0]
    m = mod_ref[0]
    g1 = m[:, 2 * D_MODEL:3 * D_MODEL]
    sh2 = m[:, 3 * D_MODEL:4 * D_MODEL]
    sc2 = m[:, 4 * D_MODEL:5 * D_MODEL]
    g2 = m[:, 5 * D_MODEL:6 * D_MODEL]
    gates = g_ref[...].astype(F32)
    from_h = yield from _dot_by_cols(oh_ref[...], who_ref, (0, HG_WIDTH), (0, D_MODEL))
    from_a = yield from _dot_by_cols(oa_ref[...], wao_ref, (0, Q_WIDTH), (0, D_MODEL))
    merged = gates[:, 0:D_MODEL] * from_h + gates[:, D_MODEL:2 * D_MODEL] * from_a
    mixed = yield from _dot_by_cols(merged.astype(BF16), wout_ref, (0, D_MODEL), (0, D_MODEL))
    x1 = x_ref[...] + g1 * mixed
    h2 = (_rms_rows(x1, nfw_ref[...]) * (1.0 + sc2) + sh2).astype(BF16)
    acc = jnp.zeros_like(x1)
    for j in range(D_FF // D_MODEL):
        cols = slice(j * D_MODEL, (j + 1) * D_MODEL)
        hj = yield from _dot_by_cols(h2, wff1_ref, (0, D_MODEL), (cols.start, cols.stop))
        hj = jnp.maximum(hj, 0.0)
        acc = acc + (yield from _dot_by_cols((hj * hj).astype(BF16), wff2_ref, (cols.start, cols.stop), (0, D_MODEL)))
    y_ref[...] = _rms_rows(x1 + g2 * acc, fnw_ref[...])


def _out_part(tm, x2, mod, seq_len, oh, oa, gates, w_ho, w_ao, w_out, nfw, w_ff1, w_ff2, fnw):
    t = x2.shape[0]
    row = lambda i: (i, 0)
    in_specs = [pl.BlockSpec((tm, D_MODEL), row),
                _mod_spec(mod, tm, seq_len),
                pl.BlockSpec((tm, HG_WIDTH), row),
                pl.BlockSpec((tm, Q_WIDTH), row),
                pl.BlockSpec((tm, 2 * D_MODEL), row),
                _const_spec((HG_WIDTH, D_MODEL)),
                _const_spec((Q_WIDTH, D_MODEL)),
                _const_spec((D_MODEL, D_MODEL)),
                _const_spec((1, D_MODEL)),
                _const_spec((D_MODEL, D_FF)),
                _const_spec((D_FF, D_MODEL)),
                _const_spec((1, D_MODEL))]
    args = [x2, mod.rows, oh, oa, gates, w_ho, w_ao, w_out, nfw, w_ff1, w_ff2, fnw]
    cost = _mm_cost(tm, HG_WIDTH + Q_WIDTH, D_MODEL) + _mm_cost(tm, D_MODEL, D_MODEL) + 2 * _mm_cost(tm, D_MODEL, D_FF)
    return t // tm, _Part(_out_stages, args, in_specs, [pl.BlockSpec((tm, D_MODEL), row)],
                          [jax.ShapeDtypeStruct((t, D_MODEL), F32)], [], cost)


def _cast_stages(ins, outs, _):
    for src, dst in zip(ins, outs):
        dst[...] = src[...].astype(dst.dtype)
    yield 1


def _cast_part(arrays, n_steps):
    in_specs, out_specs, out_shape = [], [], []
    for a in arrays:
        rows, cols = a.shape
        assert rows % (n_steps * MXU_ROWS) == 0
        spec = pl.BlockSpec((rows // n_steps, cols), lambda i: (i, 0))
        in_specs.append(spec)
        out_specs.append(spec)
        out_shape.append(jax.ShapeDtypeStruct(a.shape, BF16))
    return n_steps, _Part(_cast_stages, list(arrays), in_specs, out_specs, out_shape, [], 1)


def _rope_tables(n_tokens):
    rows = n_tokens // GRID_W
    row = np.repeat(np.arange(rows, dtype=np.float32), GRID_W)
    col = np.tile(np.arange(GRID_W, dtype=np.float32), rows)
    axis_dim = HEAD_DIM // 2
    freqs = (ROPE_THETA ** (-np.arange(0, axis_dim, 2, dtype=np.float32) / axis_dim)).astype(np.float32)
    ang_r = row[:, None] * freqs
    ang_c = col[:, None] * freqs
    cr, sr, cc, sc = np.cos(ang_r), np.sin(ang_r), np.cos(ang_c), np.sin(ang_c)
    zero = np.zeros_like(sr)
    cos = np.concatenate([cr, cr, cc, cc], axis=-1)
    s_up = np.concatenate([-sr, zero, -sc, zero], axis=-1)
    s_dn = np.concatenate([zero, sr, zero, sc], axis=-1)
    return tuple(jnp.asarray(np.tile(a, (1, N_HEADS)), dtype=F32) for a in (cos, s_up, s_dn))


def _to_slab(kv):
    n, depth, length = kv.shape[:3]
    return jnp.transpose(kv, (0, 1, 3, 4, 2)).reshape(n, depth, KV_WIDTH, length)


def _from_slab(slab):
    n, _, length = slab.shape
    return jnp.transpose(slab.reshape(n, N_KV, HEAD_DIM, length), (0, 3, 1, 2))


def _block_ones(width):
    idx = np.arange(width) // HEAD_DIM
    return jnp.asarray((idx[:, None] == idx[None, :]).astype(np.float32), dtype=BF16)


def _same_steps(*counted_parts):
    steps = {n for n, _ in counted_parts}
    assert len(steps) == 1, steps
    return steps.pop(), [p for _, p in counted_parts]


def kernel(x_prompt, x_sample, cache_k, cache_v, state_hgrn, c, c_ctx, w_ada, b_ada, norm_mix_w, w_in, q_norm_w, k_norm_w, hgrn_lb_logits, hgrn_norm_w, w_hgrn_out, w_attn_out, w_out, norm_ffn_w, w_ff1, w_ff2, final_norm_w):
    n_p, l_p, _ = x_prompt.shape
    n_s, l_s, _ = x_sample.shape
    layer = 0

    mod = _modulation(c_ctx[None, :], c, w_ada[layer], b_ada[layer][None, :])
    mod_p = _Mod(mod, 0, True)
    mod_s = _Mod(mod, 1, False)

    in_small = (q_norm_w[layer][None, :], k_norm_w[layer][None, :],
                _block_ones(Q_WIDTH), _block_ones(KV_WIDTH), hgrn_lb_logits)
    nmw = norm_mix_w[layer][None, :]
    hnw = hgrn_norm_w[layer][None, :]
    xp2 = x_prompt.reshape(n_p * l_p, D_MODEL)
    xs2 = x_sample.reshape(n_s * l_s, D_MODEL)

    n_in, in_p = _in_part(MIX_TOKEN_TILE, xp2, mod_p, l_p, nmw, w_in[layer], *in_small, None)
    steps, parts = _same_steps((n_in, in_p), _cast_part(
        [w_in[layer], w_hgrn_out[layer], w_attn_out[layer], w_out[layer], w_ff1[layer], w_ff2[layer]], n_in))
    (zh_p, q_p, k_p, kt_p, vt_p, gates_p), (w_in_b, w_ho_b, w_ao_b, w_out_b, w_ff1_b, w_ff2_b) = _launch(
        "in_ctx", steps, parts)
    out_w = (w_ho_b, w_ao_b, w_out_b, norm_ffn_w[layer][None, :], w_ff1_b, w_ff2_b, final_norm_w[None, :])

    steps, parts = _same_steps(
        _attn_part(q_p, [(k_p, vt_p, None)], n_p, l_p, N_HEADS),
        _in_part(MIX_TOKEN_TILE, xs2, mod_s, l_s, nmw, w_in_b, *in_small, _rope_tables(l_s)),
        _hgrn_part(zh_p, hnw, n_p, l_p, HG_HEADS, None, True))
    (oa_p,), (zh_s, q_s, k_s, vt_s, gates_s), (oh_p, s_fin) = _launch("in_latent_mix_ctx", steps, parts,
                                                                      MIX_PACES_IN)

    segs = [(_to_slab(cache_k), _to_slab(cache_v), layer), (k_s, vt_s, None)]
    steps, parts = _same_steps(
        _out_part(MIX_TOKEN_TILE, xp2, mod_p, l_p, oh_p, oa_p, gates_p, *out_w),
        _attn_part(q_s, segs, n_s, l_s, ATTN_LOOKAHEAD),
        _hgrn_part(zh_s, hnw, n_s, l_s, 1, (state_hgrn, layer), False))
    (y_p,), (oa_s,), (oh_s,) = _launch("out_ctx_mix_latent", steps, parts, MIX_PACES_OUT)

    steps, parts = _same_steps(_out_part(SOLO_TOKEN_TILE, xs2, mod_s, l_s, oh_s, oa_s, gates_s, *out_w))
    ((y_s,),) = _launch("out_latent", steps, parts)

    new_k = _from_slab(kt_p)[:, None]
    new_v = _from_slab(vt_p)[:, None]
    new_s = s_fin.reshape(n_p, 1, 2, HG_HEADS, HG_DK, HG_DV)
    return (y_p.reshape(n_p, l_p, D_MODEL), y_s.reshape(n_s, l_s, D_MODEL), new_k, new_v, new_s)
```

```python
import functools
from typing import Any, Callable, NamedTuple

import numpy as np
import jax
import jax.numpy as jnp
from jax import lax
from jax.experimental import pallas as pl
from jax.experimental.pallas import tpu as pltpu

F32 = jnp.float32
BF16 = jnp.bfloat16

D_MODEL = 1024
GRID_W = 64
EPS = 1e-6
HG_HEADS = 4
HG_DK = 128
HG_DV = 128
HG_WIDTH = HG_HEADS * HG_DK
HG_CHUNK = 32
HG_PAIR = 2 * HG_CHUNK
N_HEADS = 8
N_KV = 2
HEAD_DIM = 64
Q_WIDTH = N_HEADS * HEAD_DIM
KV_WIDTH = N_KV * HEAD_DIM
ROPE_THETA = 10000.0
D_FF = 4 * D_MODEL
N_MOD = 6
ZH_WIDTH = 5 * HG_WIDTH
D_IN = ZH_WIDTH + Q_WIDTH + 2 * KV_WIDTH + 2 * D_MODEL

V7X_VMEM_LIMIT_BYTES = 56 * 1024 * 1024
SUBLANES = 8
LANES = 128
MXU_ROWS = 16
MXU_TILE = 256
ROPE_PAIR_LANES = HEAD_DIM // 4
HG_BLOCK = 256
COND_ROWS = 16
ONES_ROWS = 16
LOG2_E = float(np.log2(np.e))
MIX_TOKEN_TILE = 256
SOLO_TOKEN_TILE = 512
ADALN_ROW_TILE = 256
W_CAST_ROWS = 128
DOT_COLS = 512
ATTN_Q_TILE = 256
ATTN_LOOKAHEAD = 2
MIX_PACES_IN = (1.0, 1.0, 0.8)
MIX_PACES_OUT = (1.0, 1.0, 0.9)


def _dot(a, b):
    return jnp.dot(a, b, preferred_element_type=F32)


def _dot_nt(a, b):
    return lax.dot_general(a, b, (((1,), (1,)), ((), ())), preferred_element_type=F32)


def _dot_tn(a, b):
    return lax.dot_general(a, b, (((0,), (0,)), ((), ())), preferred_element_type=F32)


def _mm_cost(m, k, n):
    return (m // MXU_ROWS) * -(-k // MXU_TILE) * -(-n // MXU_TILE)


def _dot_by_cols(a, w_ref, rows, cols):
    pieces = []
    for c in range(cols[0], cols[1], DOT_COLS):
        end = min(c + DOT_COLS, cols[1])
        pieces.append(_dot(a, w_ref[rows[0]:rows[1], c:end]))
        yield _mm_cost(a.shape[0], rows[1] - rows[0], end - c)
    return pieces[0] if len(pieces) == 1 else jnp.concatenate(pieces, axis=1)


def _split_bf16(x):
    hi = x.astype(BF16)
    lo = (x - hi.astype(F32)).astype(BF16)
    return hi, lo


def _const_spec(shape):
    nd = len(shape)
    return pl.BlockSpec(shape, lambda *_: (0,) * nd, pipeline_mode=pl.Buffered(1))


class _Part(NamedTuple):
    stages: Callable[..., Any]
    args: list
    in_specs: list
    out_specs: list
    out_shape: list
    scratch: list
    cost: int


def _interleave(gens, totals):
    done = [0] * len(gens)
    live = list(range(len(gens)))
    while live:
        i = min(live, key=lambda j: done[j] / totals[j])
        try:
            done[i] += next(gens[i])
        except StopIteration:
            live.remove(i)


def _launch(name, n_steps, parts, paces=None):
    paces = paces or [1.0] * len(parts)
    n_in = [len(p.args) for p in parts]
    n_out = [len(p.out_shape) for p in parts]
    n_scr = [len(p.scratch) for p in parts]

    def body(*refs):
        refs = list(refs)
        ins = [[refs.pop(0) for _ in range(n)] for n in n_in]
        outs = [[refs.pop(0) for _ in range(n)] for n in n_out]
        scrs = [[refs.pop(0) for _ in range(n)] for n in n_scr]
        _interleave([p.stages(i, o, s) for p, i, o, s in zip(parts, ins, outs, scrs)],
                    [p.cost * pace for p, pace in zip(parts, paces)])

    flat = pl.pallas_call(
        body,
        grid=(n_steps,),
        in_specs=[s for p in parts for s in p.in_specs],
        out_specs=[s for p in parts for s in p.out_specs],
        out_shape=[s for p in parts for s in p.out_shape],
        scratch_shapes=[s for p in parts for s in p.scratch],
        compiler_params=pltpu.CompilerParams(dimension_semantics=("arbitrary",),
                                             vmem_limit_bytes=V7X_VMEM_LIMIT_BYTES),
        name=name,
    )(*[a for p in parts for a in p.args])
    flat = list(flat)
    return [[flat.pop(0) for _ in range(n)] for n in n_out]


class _Mod(NamedTuple):
    rows: Any
    first: int
    shared: bool


def _mod_spec(mod, tile, seq_len, tile_of_step=lambda i: i):
    if mod.shared:
        return pl.BlockSpec((1, 1, N_MOD * D_MODEL), lambda i: (mod.first, 0, 0))
    assert seq_len % tile == 0
    per_seq = seq_len // tile
    return pl.BlockSpec((1, 1, N_MOD * D_MODEL), lambda i: (mod.first + tile_of_step(i) // per_seq, 0, 0))


def _mod_kernel(cctx_ref, c_ref, w_ref, b_ref, o_ref, cond_scr):
    n_req = c_ref.shape[0]
    cond_scr[...] = jnp.zeros_like(cond_scr)
    cond_scr[0:1, :] = cctx_ref[...]
    cond_scr[1:1 + n_req, :] = c_ref[...]
    c = cond_scr[...]
    x = c * jax.nn.sigmoid(c)
    x_hi, x_lo = _split_bf16(x)
    w = w_ref[...].astype(BF16)
    part = _dot(x_hi, w) + _dot(x_lo, w)

    @pl.when(pl.program_id(0) == 0)
    def _():
        o_ref[:, 0, :] = b_ref[...] + part

    @pl.when(pl.program_id(0) > 0)
    def _():
        o_ref[:, 0, :] += part


def _modulation(c_ctx, c, w_ada, b_ada):
    n = N_MOD * D_MODEL
    tk = ADALN_ROW_TILE
    n_req = c.shape[0]
    assert 1 + n_req <= COND_ROWS
    return pl.pallas_call(
        _mod_kernel,
        grid=(D_MODEL // tk,),
        in_specs=[pl.BlockSpec((1, tk), lambda j: (0, j)),
                  pl.BlockSpec((n_req, tk), lambda j: (0, j)),
                  pl.BlockSpec((tk, n), lambda j: (j, 0)),
                  pl.BlockSpec((1, n), lambda j: (0, 0))],
        out_specs=pl.BlockSpec((COND_ROWS, 1, n), lambda j: (0, 0, 0)),
        out_shape=jax.ShapeDtypeStruct((COND_ROWS, 1, n), F32),
        scratch_shapes=[pltpu.VMEM((COND_ROWS, tk), F32)],
        compiler_params=pltpu.CompilerParams(dimension_semantics=("arbitrary",),
                                             vmem_limit_bytes=V7X_VMEM_LIMIT_BYTES),
        name="adaln_mod",
    )(c_ctx, c, w_ada, b_ada)


def _rms_rows(x, w):
    return x * lax.rsqrt(jnp.mean(x * x, axis=-1, keepdims=True) + EPS) * w


def _head_sumsq(a, ones_ref):
    return _dot((a * a).astype(BF16), ones_ref[...])


def _head_rms(a, sumsq, w):
    return a * lax.rsqrt(sumsq * (1.0 / HEAD_DIM) + EPS) * w


def _forget_lower_bound(logits):
    e = jnp.exp(logits - jnp.max(logits, axis=1, keepdims=True))
    return e[:, 0, :] / jnp.sum(e, axis=1)


def _rope(x, cos, s_up, s_dn):
    cols = []
    for j in range(x.shape[1] // LANES):
        sl = slice(j * LANES, (j + 1) * LANES)
        xj = x[:, sl]
        cols.append(xj * cos[:, sl] + pltpu.roll(xj, LANES - ROPE_PAIR_LANES, 1) * s_up[:, sl]
                    + pltpu.roll(xj, ROPE_PAIR_LANES, 1) * s_dn[:, sl])
    return jnp.concatenate(cols, axis=1) if len(cols) > 1 else cols[0]


def _in_stages(latent, ins, outs, scrs):
    x_ref, mod_ref, nw_ref, w_ref, qw_ref, kw_ref, oq_ref, ok_ref, lbl_ref = ins[:9]
    if scrs:
        (w_bf_ref,) = scrs
        rows = W_CAST_ROWS

        @pl.when(pl.program_id(0) == 0)
        def _():
            def cast_rows(r, carry):
                sl = pl.ds(pl.multiple_of(r * rows, rows), rows)
                w_bf_ref[sl, :] = w_ref[sl, :].astype(BF16)
                return carry

            lax.fori_loop(0, w_ref.shape[0] // rows, cast_rows, 0)

        w_ref = w_bf_ref
    if latent:
        cos_ref, sup_ref, sdn_ref = ins[9:]
        zh_ref, q_ref, k_ref, vt_ref, g_ref = outs
    else:
        zh_ref, q_ref, k_ref, kt_ref, vt_ref, g_ref = outs
    tm = x_ref.shape[0]
    m = mod_ref[0]
    sh1 = m[:, 0:D_MODEL]
    sc1 = m[:, D_MODEL:2 * D_MODEL]
    h = _rms_rows(x_ref[...], nw_ref[...]) * (1.0 + sc1) + sh1
    hb = h.astype(BF16)
    c0 = ZH_WIDTH
    c1 = c0 + Q_WIDTH
    c2 = c1 + KV_WIDTH
    c3 = c2 + KV_WIDTH
    aq = _dot(hb, w_ref[:, c0:c1])
    ak = _dot(hb, w_ref[:, c1:c2])
    av = _dot(hb, w_ref[:, c2:c3])
    yield _mm_cost(tm, D_MODEL, c3 - c0)
    gl = yield from _dot_by_cols(hb, w_ref, (0, D_MODEL), (c3, D_IN))
    ssq = _head_sumsq(aq, oq_ref)
    ssk = _head_sumsq(ak, ok_ref)
    yield _mm_cost(tm, Q_WIDTH, Q_WIDTH) + _mm_cost(tm, KV_WIDTH, KV_WIDTH)
    lb = _forget_lower_bound(lbl_ref[...])

    def hgrn_segment(s):
        z = yield from _dot_by_cols(hb, w_ref, (0, D_MODEL), (s * HG_WIDTH, (s + 1) * HG_WIDTH))
        if s == 0:
            z = z * HG_DK ** -0.5
        elif s in (1, 2):
            lbv = lb[s - 1:s, :]
            z = jnp.log(lbv + (1.0 - lbv) * jax.nn.sigmoid(z))
        elif s == 4:
            z = z * jax.nn.sigmoid(z)
        zh_ref[:, s * HG_WIDTH:(s + 1) * HG_WIDTH] = z

    yield from hgrn_segment(1)
    yield from hgrn_segment(2)
    g_ref[...] = jax.nn.sigmoid(gl).astype(BF16)
    qn = _head_rms(aq, ssq, jnp.tile(qw_ref[...], (1, N_HEADS)))
    kn = _head_rms(ak, ssk, jnp.tile(kw_ref[...], (1, N_KV)))
    seq = vt_ref.shape[2]
    avt = av.T
    for s in range(vt_ref.shape[0]):
        vt_ref[s] = avt[:, s * seq:(s + 1) * seq].astype(vt_ref.dtype)
    for s in (4, 0, 3):
        yield from hgrn_segment(s)
    if latent:
        cos, sup, sdn = cos_ref[...], sup_ref[...], sdn_ref[...]
        qn = _rope(qn, cos, sup, sdn)
        kn = _rope(kn, cos[:, 0:KV_WIDTH], sup[:, 0:KV_WIDTH], sdn[:, 0:KV_WIDTH])
    else:
        knt = kn.T
        for s in range(kt_ref.shape[0]):
            kt_ref[s] = knt[:, s * seq:(s + 1) * seq]
    q_ref[...] = (qn * (HEAD_DIM ** -0.5 * LOG2_E)).astype(BF16)
    k_ref[...] = kn.astype(BF16)


def _in_part(tm, x2, mod, seq_len, norm_w, w_in, qw, kw, ones_q, ones_k, lb_logits, rope):
    t = x2.shape[0]
    latent = rope is not None
    per_seq = max(seq_len // tm, 1)
    row = lambda i: (i, 0)
    in_specs = [pl.BlockSpec((tm, D_MODEL), row),
                _mod_spec(mod, tm, seq_len),
                _const_spec((1, D_MODEL)),
                _const_spec((D_MODEL, D_IN)),
                _const_spec((1, HEAD_DIM)),
                _const_spec((1, HEAD_DIM)),
                _const_spec((Q_WIDTH, Q_WIDTH)),
                _const_spec((KV_WIDTH, KV_WIDTH)),
                _const_spec(lb_logits.shape)]
    args = [x2, mod.rows, norm_w, w_in, qw, kw, ones_q, ones_k, lb_logits]
    out_specs = [pl.BlockSpec((tm, ZH_WIDTH), row), pl.BlockSpec((tm, Q_WIDTH), row),
                 pl.BlockSpec((tm, KV_WIDTH), row)]
    out_shape = [jax.ShapeDtypeStruct((t, ZH_WIDTH), F32), jax.ShapeDtypeStruct((t, Q_WIDTH), BF16),
                 jax.ShapeDtypeStruct((t, KV_WIDTH), BF16)]
    if seq_len <= tm:
        assert tm % seq_len == 0
        t_spec = pl.BlockSpec((tm // seq_len, KV_WIDTH, seq_len), lambda i: (i, 0, 0))
    else:
        t_spec = pl.BlockSpec((1, KV_WIDTH, tm), lambda i: (i // per_seq, 0, i % per_seq))
    if latent:
        in_specs += [pl.BlockSpec((tm, Q_WIDTH), lambda i: (i % per_seq, 0))] * 3
        args += list(rope)
        out_specs.append(t_spec)
        out_shape.append(jax.ShapeDtypeStruct((t // seq_len, KV_WIDTH, seq_len), BF16))
    else:
        out_specs += [t_spec, t_spec]
        out_shape += [jax.ShapeDtypeStruct((t // seq_len, KV_WIDTH, seq_len), F32)] * 2
    out_specs.append(pl.BlockSpec((tm, 2 * D_MODEL), row))
    out_shape.append(jax.ShapeDtypeStruct((t, 2 * D_MODEL), BF16))
    cost = (_mm_cost(tm, D_MODEL, D_IN) + _mm_cost(tm, Q_WIDTH, Q_WIDTH) + _mm_cost(tm, KV_WIDTH, KV_WIDTH))
    scratch = [] if w_in.dtype == BF16 else [pltpu.VMEM((D_MODEL, D_IN), BF16)]
    return t // tm, _Part(functools.partial(_in_stages, latent), args, in_specs, out_specs, out_shape, scratch, cost)


def _attn_cost(lq, lks):
    per_head = sum(_mm_cost(lk, HEAD_DIM, lq) + _mm_cost(HEAD_DIM + ONES_ROWS, lk, lq) for lk in lks)
    return N_HEADS * per_head


def _attn_stages(k_transposed, lookahead, ins, outs, _):
    n_seg = len(k_transposed)
    q_ref, kv_refs = ins[0], ins[1:]
    (o_ref,) = outs
    q = q_ref[...]
    lq = q.shape[0]
    ks = [(kv_refs[2 * s][0].T if k_transposed[s] else kv_refs[2 * s][...]).astype(BF16) for s in range(n_seg)]
    vts = [kv_refs[2 * s + 1][0].astype(BF16) for s in range(n_seg)]
    vts = [[jnp.concatenate([vt[g * HEAD_DIM:(g + 1) * HEAD_DIM, :],
                             jnp.ones((ONES_ROWS, vt.shape[1]), BF16)], axis=0) for g in range(N_KV)]
           for vt in vts]
    score_cost = sum(_mm_cost(k.shape[0], HEAD_DIM, lq) for k in ks)
    value_cost = sum(_mm_cost(HEAD_DIM + ONES_ROWS, k.shape[0], lq) for k in ks)

    def scores(h):
        g = h // (N_HEADS // N_KV)
        qh = q[:, h * HEAD_DIM:(h + 1) * HEAD_DIM]
        return [_dot_nt(k[:, g * HEAD_DIM:(g + 1) * HEAD_DIM], qh) for k in ks]

    pending = []
    for h in range(lookahead):
        pending.append(scores(h))
        yield score_cost
    outs_t = []
    for h in range(N_HEADS):
        if h + lookahead < N_HEADS:
            pending.append(scores(h + lookahead))
            yield score_cost
        st = pending.pop(0)
        g = h // (N_HEADS // N_KV)
        m = functools.reduce(jnp.maximum, [jnp.max(s, axis=0, keepdims=True) for s in st])
        ot = functools.reduce(jnp.add, [_dot(vt[g], jnp.exp2(s - m).astype(BF16)) for vt, s in zip(vts, st)])
        outs_t.append(ot[0:HEAD_DIM, :] / ot[HEAD_DIM:HEAD_DIM + 1, :])
        if h == N_HEADS - 1:
            o_ref[...] = jnp.concatenate(outs_t, axis=0).T.astype(BF16)
        yield value_cost


def _attn_part(q, kv_segments, n_batch, seq_len, lookahead):
    t = q.shape[0]
    tq = ATTN_Q_TILE
    per_seq = seq_len // tq
    in_specs = [pl.BlockSpec((tq, Q_WIDTH), lambda i: (i, 0))]
    args = [q]
    for k, v_t, layer in kv_segments:
        lk = v_t.shape[-1]
        if v_t.ndim == 4:
            slab = pl.BlockSpec((1, None, KV_WIDTH, lk), lambda i, layer=layer: (i // per_seq, layer, 0, 0))
        else:
            slab = pl.BlockSpec((1, KV_WIDTH, lk), lambda i: (i // per_seq, 0, 0))
        k_spec = slab if k.ndim == v_t.ndim else pl.BlockSpec((lk, KV_WIDTH), lambda i: (i // per_seq, 0))
        in_specs += [k_spec, slab]
        args += [k, v_t]
    stages = functools.partial(_attn_stages, tuple(k.ndim == v_t.ndim for k, v_t, _ in kv_segments), lookahead)
    return n_batch * per_seq, _Part(stages, args, in_specs, [pl.BlockSpec((tq, Q_WIDTH), lambda i: (i, 0))],
                                    [jax.ShapeDtypeStruct((t, Q_WIDTH), BF16)], [],
                                    _attn_cost(tq, [v_t.shape[-1] for _, v_t, _ in kv_segments]))


def _hgrn_unit_cost():
    blk = HG_BLOCK
    return (2 * _mm_cost(blk, blk, 2 * HG_DK)
            + 2 * _mm_cost(blk, HG_DK, 2 * blk) + _mm_cost(blk, blk, HG_DV)
            + (blk // HG_PAIR) * (_mm_cost(HG_DV, HG_PAIR, 2 * HG_DK) + _mm_cost(HG_PAIR, 2 * HG_DK, HG_DV)))


def _hgrn_stages(seq_len, hps, has_s0, has_sfin, ins, outs, scrs):
    tf_ref, tb_ref, xf_ref, xb_ref, q_ref, lf_ref, lb_ref, v_ref, g_ref, nw_ref = ins[:10]
    s0_ref = ins[10] if has_s0 else None
    o_ref = outs[0]
    sfin_ref = outs[1] if has_sfin else None
    kv_scr, ss_scr, qd_scr, oi_scr, dec_scr = scrs
    n_blk = seq_len // HG_BLOCK
    n_pair = seq_len // HG_PAIR
    per_blk = HG_BLOCK // HG_CHUNK
    pairs_per_blk = HG_BLOCK // HG_PAIR
    one_row = jnp.ones((1, HG_DK), F32)

    def chunk_cumsum(t_ref, x):
        hi, lo = _split_bf16(x)
        r = _dot(t_ref[...], jnp.concatenate([hi, lo], axis=1))
        return r[:, 0:HG_DK] + r[:, HG_DK:2 * HG_DK]

    def chunk_edge_rows(x, first):
        off = 0 if first else HG_CHUNK - 1
        return [x[c * HG_CHUNK + off:c * HG_CHUNK + off + 1, :] for c in range(per_blk)]

    def spread_rows(rows):
        return jnp.concatenate([jnp.broadcast_to(r, (HG_CHUNK, HG_DK)) for r in rows], axis=0)

    units = [(blk, h) for blk in range(n_blk) for h in range(hps)]
    stage1 = []
    for blk, h in units:
        rows = slice(blk * HG_BLOCK, (blk + 1) * HG_BLOCK)
        cols = slice(h * HG_DK, (h + 1) * HG_DK)
        logf_f = lf_ref[rows, cols]
        logf_b = lb_ref[rows, cols]
        k_f = 1.0 - jnp.exp(logf_f)
        k_b = 1.0 - jnp.exp(logf_b)
        a_f = chunk_cumsum(tf_ref, logf_f)
        a_b = chunk_cumsum(tb_ref, logf_b)
        stage1.append((rows, cols, k_f, k_b, a_f, a_b))
        yield 2 * _mm_cost(HG_BLOCK, HG_BLOCK, 2 * HG_DK)
    stage2 = []
    for (blk, h), (rows, cols, k_f, k_b, a_f, a_b) in zip(units, stage1):
        q = q_ref[rows, cols]
        vb = v_ref[rows, cols].astype(BF16)
        ea_f = jnp.exp(a_f)
        ea_b = jnp.exp(a_b)
        qd_f = q * ea_f
        qd_b = q * ea_b
        kd_f = k_f * jnp.exp(-a_f)
        kd_b = k_b * jnp.exp(-a_b)
        dec_f = chunk_edge_rows(ea_f, first=False)
        dec_b = chunk_edge_rows(ea_b, first=True)
        ke_f = kd_f * spread_rows(dec_f)
        ke_b = kd_b * spread_rows(dec_b)
        even = [c % 2 == 0 for c in range(per_blk)]
        qd_pair = jnp.concatenate(
            [qd_f * spread_rows([one_row if even[c] else dec_f[c - 1] for c in range(per_blk)]),
             qd_b * spread_rows([dec_b[c + 1] if even[c] else one_row for c in range(per_blk)])], axis=1)
        ke_pair = jnp.concatenate(
            [ke_f * spread_rows([dec_f[c + 1] if even[c] else one_row for c in range(per_blk)]),
             ke_b * spread_rows([one_row if even[c] else dec_b[c - 1] for c in range(per_blk)])], axis=1).astype(BF16)
        sc_f = _dot_nt(qd_f.astype(BF16), jnp.concatenate([kd_f, ke_f], axis=0).astype(BF16))
        sc_b = _dot_nt(qd_b.astype(BF16), jnp.concatenate([kd_b, ke_b], axis=0).astype(BF16))
        for p in range(pairs_per_blk):
            pr = slice(p * HG_PAIR, (p + 1) * HG_PAIR)
            kv_scr[h, blk * pairs_per_blk + p] = _dot_tn(vb[pr, :], ke_pair[pr, :])
            dec_scr[h, blk * pairs_per_blk + p, 0:1, :] = jnp.concatenate(
                [dec_f[2 * p] * dec_f[2 * p + 1], dec_b[2 * p] * dec_b[2 * p + 1]], axis=1)
        qd_scr[h, rows, :] = qd_pair.astype(BF16)
        stage2.append((rows, cols, vb, sc_f, sc_b))
        yield 2 * _mm_cost(HG_BLOCK, HG_DK, 2 * HG_BLOCK) + pairs_per_blk * _mm_cost(HG_DV, HG_PAIR, 2 * HG_DK)
    for rows, cols, vb, sc_f, sc_b in stage2:
        s = (sc_f[:, 0:HG_BLOCK].astype(BF16) * tf_ref[...] + sc_f[:, HG_BLOCK:].astype(BF16) * xf_ref[...]
             + sc_b[:, 0:HG_BLOCK].astype(BF16) * tb_ref[...] + sc_b[:, HG_BLOCK:].astype(BF16) * xb_ref[...])
        oi_scr[rows, cols] = _dot(s, vb)
        yield _mm_cost(HG_BLOCK, HG_BLOCK, HG_DV)

    for h in range(hps):
        if has_s0:
            s_f, s_b = s0_ref[0, 0, h].T, s0_ref[0, 1, h].T
        else:
            s_f = s_b = jnp.zeros((HG_DV, HG_DK), F32)
        for pf in range(n_pair):
            pb = n_pair - 1 - pf
            ss_scr[h, pf, :, 0:HG_DK] = s_f.astype(BF16)
            s_f = dec_scr[h, pf, 0:1, 0:HG_DK] * s_f + kv_scr[h, pf, :, 0:HG_DK]
            ss_scr[h, pb, :, HG_DK:2 * HG_DK] = s_b.astype(BF16)
            s_b = dec_scr[h, pb, 0:1, HG_DK:2 * HG_DK] * s_b + kv_scr[h, pb, :, HG_DK:2 * HG_DK]
        if has_sfin:
            sfin_ref[0, 0, h] = s_f.T
            sfin_ref[0, 1, h] = s_b.T

    for blk, h in units:
        cols = slice(h * HG_DK, (h + 1) * HG_DK)
        for p in range(pairs_per_blk):
            pi = blk * pairs_per_blk + p
            rows = slice(pi * HG_PAIR, (pi + 1) * HG_PAIR)
            oi_scr[rows, cols] = oi_scr[rows, cols] + _dot_nt(qd_scr[h, rows, :], ss_scr[h, pi])
        yield pairs_per_blk * _mm_cost(HG_PAIR, 2 * HG_DK, HG_DV)

    for h in range(hps):
        cols = slice(h * HG_DK, (h + 1) * HG_DK)
        o = _rms_rows(oi_scr[:, cols], nw_ref[...])
        o_ref[:, cols] = (o * g_ref[:, cols]).astype(BF16)


def _chunk_masks():
    r = np.arange(HG_BLOCK)
    same = (r[:, None] // HG_CHUNK) == (r[None, :] // HG_CHUNK)
    lower = same & (r[None, :] <= r[:, None])
    upper = same & (r[None, :] >= r[:, None])
    same_pair = (r[:, None] // HG_PAIR) == (r[None, :] // HG_PAIR)
    cross_f = same_pair & (r[:, None] // HG_CHUNK == r[None, :] // HG_CHUNK + 1)
    return tuple(jnp.asarray(m.astype(np.float32), dtype=BF16) for m in (lower, upper, cross_f, cross_f.T))


def _hgrn_part(zh, norm_w, n_batch, seq_len, hps, s0, want_final):
    t = zh.shape[0]
    n_pair = seq_len // HG_PAIR
    width = hps * HG_DK
    n_hb = HG_HEADS // hps
    masks = _chunk_masks()

    def seg_spec(seg):
        return pl.BlockSpec((seq_len, width), lambda i: (i // n_hb, seg * n_hb + i % n_hb))

    state_spec = pl.BlockSpec((1, 2, hps, HG_DK, HG_DV), lambda i: (i // n_hb, 0, i % n_hb, 0, 0))
    mask_spec = pl.BlockSpec((HG_BLOCK, HG_BLOCK), lambda i: (0, 0))
    in_specs = [mask_spec] * len(masks) + [seg_spec(s) for s in range(5)] + [pl.BlockSpec((1, HG_DV), lambda i: (0, 0))]
    args = list(masks) + [zh] * 5 + [norm_w]
    if s0 is not None:
        states, layer = s0
        in_specs.append(pl.BlockSpec((1, None, 2, hps, HG_DK, HG_DV),
                                     lambda i: (i // n_hb, layer, 0, i % n_hb, 0, 0)))
        args.append(states)
    out_specs = [pl.BlockSpec((seq_len, width), lambda i: (i // n_hb, i % n_hb))]
    out_shape = [jax.ShapeDtypeStruct((t, HG_WIDTH), BF16)]
    if want_final:
        out_specs.append(state_spec)
        out_shape.append(jax.ShapeDtypeStruct((n_batch, 2, HG_HEADS, HG_DK, HG_DV), F32))
    scratch = [pltpu.VMEM((hps, n_pair, HG_DV, 2 * HG_DK), F32),
               pltpu.VMEM((hps, n_pair, HG_DV, 2 * HG_DK), BF16),
               pltpu.VMEM((hps, seq_len, 2 * HG_DK), BF16),
               pltpu.VMEM((seq_len, width), F32),
               pltpu.VMEM((hps, n_pair, SUBLANES, 2 * HG_DK), F32)]
    stages = functools.partial(_hgrn_stages, seq_len, hps, s0 is not None, want_final)
    cost = hps * (seq_len // HG_BLOCK) * _hgrn_unit_cost()
    return n_batch * n_hb, _Part(stages, args, in_specs, out_specs, out_shape, scratch, cost)


def _out_stages(ins, outs, _):
    x_ref, mod_ref, oh_ref, oa_ref, g_ref, who_ref, wao_ref, wout_ref, nfw_ref, wff1_ref, wff2_ref, fnw_ref = ins
    (y_ref,) = outs
    tm = x_ref.shape[0]
    m = mod_ref[0]
    g1 = m[:, 2 * D_MODEL:3 * D_MODEL]
    sh2 = m[:, 3 * D_MODEL:4 * D_MODEL]
    sc2 = m[:, 4 * D_MODEL:5 * D_MODEL]
    g2 = m[:, 5 * D_MODEL:6 * D_MODEL]
    gates = g_ref[...].astype(F32)
    from_h = yield from _dot_by_cols(oh_ref[...], who_ref, (0, HG_WIDTH), (0, D_MODEL))
    from_a = yield from _dot_by_cols(oa_ref[...], wao_ref, (0, Q_WIDTH), (0, D_MODEL))
    merged = gates[:, 0:D_MODEL] * from_h + gates[:, D_MODEL:2 * D_MODEL] * from_a
    mixed = yield from _dot_by_cols(merged.astype(BF16), wout_ref, (0, D_MODEL), (0, D_MODEL))
    x1 = x_ref[...] + g1 * mixed
    h2 = (_rms_rows(x1, nfw_ref[...]) * (1.0 + sc2) + sh2).astype(BF16)
    acc = jnp.zeros_like(x1)
    for j in range(D_FF // D_MODEL):
        cols = slice(j * D_MODEL, (j + 1) * D_MODEL)
        hj = yield from _dot_by_cols(h2, wff1_ref, (0, D_MODEL), (cols.start, cols.stop))
        hj = jnp.maximum(hj, 0.0)
        acc = acc + (yield from _dot_by_cols((hj * hj).astype(BF16), wff2_ref, (cols.start, cols.stop), (0, D_MODEL)))
    y_ref[...] = _rms_rows(x1 + g2 * acc, fnw_ref[...])


def _out_part(tm, x2, mod, seq_len, oh, oa, gates, w_ho, w_ao, w_out, nfw, w_ff1, w_ff2, fnw):
    t = x2.shape[0]
    row = lambda i: (i, 0)
    in_specs = [pl.BlockSpec((tm, D_MODEL), row),
                _mod_spec(mod, tm, seq_len),
                pl.BlockSpec((tm, HG_WIDTH), row),
                pl.BlockSpec((tm, Q_WIDTH), row),
                pl.BlockSpec((tm, 2 * D_MODEL), row),
                _const_spec((HG_WIDTH, D_MODEL)),
                _const_spec((Q_WIDTH, D_MODEL)),
                _const_spec((D_MODEL, D_MODEL)),
                _const_spec((1, D_MODEL)),
                _const_spec((D_MODEL, D_FF)),
                _const_spec((D_FF, D_MODEL)),
                _const_spec((1, D_MODEL))]
    args = [x2, mod.rows, oh, oa, gates, w_ho, w_ao, w_out, nfw, w_ff1, w_ff2, fnw]
    cost = _mm_cost(tm, HG_WIDTH + Q_WIDTH, D_MODEL) + _mm_cost(tm, D_MODEL, D_MODEL) + 2 * _mm_cost(tm, D_MODEL, D_FF)
    return t // tm, _Part(_out_stages, args, in_specs, [pl.BlockSpec((tm, D_MODEL), row)],
                          [jax.ShapeDtypeStruct((t, D_MODEL), F32)], [], cost)


def _cast_stages(ins, outs, _):
    for src, dst in zip(ins, outs):
        dst[...] = src[...].astype(dst.dtype)
    yield 1


def _cast_part(arrays, n_steps):
    in_specs, out_specs, out_shape = [], [], []
    for a in arrays:
        rows, cols = a.shape
        assert rows % (n_steps * MXU_ROWS) == 0
        spec = pl.BlockSpec((rows // n_steps, cols), lambda i: (i, 0))
        in_specs.append(spec)
        out_specs.append(spec)
        out_shape.append(jax.ShapeDtypeStruct(a.shape, BF16))
    return n_steps, _Part(_cast_stages, list(arrays), in_specs, out_specs, out_shape, [], 1)


def _rope_tables(n_tokens):
    rows = n_tokens // GRID_W
    row = np.repeat(np.arange(rows, dtype=np.float32), GRID_W)
    col = np.tile(np.arange(GRID_W, dtype=np.float32), rows)
    axis_dim = HEAD_DIM // 2
    freqs = (ROPE_THETA ** (-np.arange(0, axis_dim, 2, dtype=np.float32) / axis_dim)).astype(np.float32)
    ang_r = row[:, None] * freqs
    ang_c = col[:, None] * freqs
    cr, sr, cc, sc = np.cos(ang_r), np.sin(ang_r), np.cos(ang_c), np.sin(ang_c)
    zero = np.zeros_like(sr)
    cos = np.concatenate([cr, cr, cc, cc], axis=-1)
    s_up = np.concatenate([-sr, zero, -sc, zero], axis=-1)
    s_dn = np.concatenate([zero, sr, zero, sc], axis=-1)
    return tuple(jnp.asarray(np.tile(a, (1, N_HEADS)), dtype=F32) for a in (cos, s_up, s_dn))


def _to_slab(kv):
    n, depth, length = kv.shape[:3]
    return jnp.transpose(kv, (0, 1, 3, 4, 2)).reshape(n, depth, KV_WIDTH, length)


def _from_slab(slab):
    n, _, length = slab.shape
    return jnp.transpose(slab.reshape(n, N_KV, HEAD_DIM, length), (0, 3, 1, 2))


def _block_ones(width):
    idx = np.arange(width) // HEAD_DIM
    return jnp.asarray((idx[:, None] == idx[None, :]).astype(np.float32), dtype=BF16)


def _same_steps(*counted_parts):
    steps = {n for n, _ in counted_parts}
    assert len(steps) == 1, steps
    return steps.pop(), [p for _, p in counted_parts]


def kernel(x_prompt, x_sample, cache_k, cache_v, state_hgrn, c, c_ctx, w_ada, b_ada, norm_mix_w, w_in, q_norm_w, k_norm_w, hgrn_lb_logits, hgrn_norm_w, w_hgrn_out, w_attn_out, w_out, norm_ffn_w, w_ff1, w_ff2, final_norm_w):
    n_p, l_p, _ = x_prompt.shape
    n_s, l_s, _ = x_sample.shape
    layer = 0

    mod = _modulation(c_ctx[None, :], c, w_ada[layer], b_ada[layer][None, :])
    mod_p = _Mod(mod, 0, True)
    mod_s = _Mod(mod, 1, False)

    in_small = (q_norm_w[layer][None, :], k_norm_w[layer][None, :],
                _block_ones(Q_WIDTH), _block_ones(KV_WIDTH), hgrn_lb_logits)
    nmw = norm_mix_w[layer][None, :]
    hnw = hgrn_norm_w[layer][None, :]
    xp2 = x_prompt.reshape(n_p * l_p, D_MODEL)
    xs2 = x_sample.reshape(n_s * l_s, D_MODEL)

    n_in, in_p = _in_part(MIX_TOKEN_TILE, xp2, mod_p, l_p, nmw, w_in[layer], *in_small, None)
    steps, parts = _same_steps((n_in, in_p), _cast_part([w_in[layer]], n_in))
    (zh_p, q_p, k_p, kt_p, vt_p, gates_p), (w_in_b,) = _launch("in_ctx", steps, parts)

    n_mix, attn_p = _attn_part(q_p, [(k_p, vt_p, None)], n_p, l_p, N_HEADS)
    steps, parts = _same_steps(
        (n_mix, attn_p),
        _in_part(MIX_TOKEN_TILE, xs2, mod_s, l_s, nmw, w_in_b, *in_small, _rope_tables(l_s)),
        _hgrn_part(zh_p, hnw, n_p, l_p, HG_HEADS, None, True),
        _cast_part([w_hgrn_out[layer], w_attn_out[layer], w_out[layer], w_ff1[layer], w_ff2[layer]], n_mix))
    (oa_p,), (zh_s, q_s, k_s, vt_s, gates_s), (oh_p, s_fin), (w_ho_b, w_ao_b, w_out_b, w_ff1_b, w_ff2_b) = _launch(
        "in_latent_mix_ctx", steps, parts, MIX_PACES_IN + (1.0,))
    out_w = (w_ho_b, w_ao_b, w_out_b, norm_ffn_w[layer][None, :], w_ff1_b, w_ff2_b, final_norm_w[None, :])

    segs = [(_to_slab(cache_k), _to_slab(cache_v), layer), (k_s, vt_s, None)]
    steps, parts = _same_steps(
        _out_part(MIX_TOKEN_TILE, xp2, mod_p, l_p, oh_p, oa_p, gates_p, *out_w),
        _attn_part(q_s, segs, n_s, l_s, ATTN_LOOKAHEAD),
        _hgrn_part(zh_s, hnw, n_s, l_s, 1, (state_hgrn, layer), False))
    (y_p,), (oa_s,), (oh_s,) = _launch("out_ctx_mix_latent", steps, parts, MIX_PACES_OUT)

    steps, parts = _same_steps(_out_part(SOLO_TOKEN_TILE, xs2, mod_s, l_s, oh_s, oa_s, gates_s, *out_w))
    ((y_s,),) = _launch("out_latent", steps, parts)

    new_k = _from_slab(kt_p)[:, None]
    new_v = _from_slab(vt_p)[:, None]
    new_s = s_fin.reshape(n_p, 1, 2, HG_HEADS, HG_DK, HG_DV)
    return (y_p.reshape(n_p, l_p, D_MODEL), y_s.reshape(n_s, l_s, D_MODEL), new_k, new_v, new_s)
```

```python
import functools
from typing import Any, Callable, NamedTuple

import numpy as np
import jax
import jax.numpy as jnp
from jax import lax
from jax.experimental import pallas as pl
from jax.experimental.pallas import tpu as pltpu

F32 = jnp.float32
BF16 = jnp.bfloat16

D_MODEL = 1024
GRID_W = 64
EPS = 1e-6
HG_HEADS = 4
HG_DK = 128
HG_DV = 128
HG_WIDTH = HG_HEADS * HG_DK
HG_CHUNK = 32
HG_PAIR = 2 * HG_CHUNK
N_HEADS = 8
N_KV = 2
HEAD_DIM = 64
Q_WIDTH = N_HEADS * HEAD_DIM
KV_WIDTH = N_KV * HEAD_DIM
ROPE_THETA = 10000.0
D_FF = 4 * D_MODEL
N_MOD = 6
ZH_WIDTH = 5 * HG_WIDTH
D_IN = ZH_WIDTH + Q_WIDTH + 2 * KV_WIDTH + 2 * D_MODEL

V7X_VMEM_LIMIT_BYTES = 56 * 1024 * 1024
SUBLANES = 8
LANES = 128
MXU_ROWS = 16
MXU_TILE = 256
ROPE_PAIR_LANES = HEAD_DIM // 4
HG_BLOCK = 256
COND_ROWS = 16
ONES_ROWS = 16
LOG2_E = float(np.log2(np.e))
MIX_TOKEN_TILE = 256
SOLO_TOKEN_TILE = 512
ADALN_ROW_TILE = 256
W_CAST_ROWS = 128
DOT_COLS = 512
ATTN_Q_TILE = 256
ATTN_LOOKAHEAD = 3
MIX_PACES_IN = (1.0, 1.0, 0.8)
MIX_PACES_OUT = (1.0, 1.0, 0.9)


def _dot(a, b):
    return jnp.dot(a, b, preferred_element_type=F32)


def _dot_nt(a, b):
    return lax.dot_general(a, b, (((1,), (1,)), ((), ())), preferred_element_type=F32)


def _dot_tn(a, b):
    return lax.dot_general(a, b, (((0,), (0,)), ((), ())), preferred_element_type=F32)


def _mm_cost(m, k, n):
    return (m // MXU_ROWS) * -(-k // MXU_TILE) * -(-n // MXU_TILE)


def _dot_by_cols(a, w_ref, rows, cols):
    pieces = []
    for c in range(cols[0], cols[1], DOT_COLS):
        end = min(c + DOT_COLS, cols[1])
        pieces.append(_dot(a, w_ref[rows[0]:rows[1], c:end]))
        yield _mm_cost(a.shape[0], rows[1] - rows[0], end - c)
    return pieces[0] if len(pieces) == 1 else jnp.concatenate(pieces, axis=1)


def _split_bf16(x):
    hi = x.astype(BF16)
    lo = (x - hi.astype(F32)).astype(BF16)
    return hi, lo


def _const_spec(shape):
    nd = len(shape)
    return pl.BlockSpec(shape, lambda *_: (0,) * nd, pipeline_mode=pl.Buffered(1))


class _Part(NamedTuple):
    stages: Callable[..., Any]
    args: list
    in_specs: list
    out_specs: list
    out_shape: list
    scratch: list
    cost: int


def _interleave(gens, totals):
    done = [0] * len(gens)
    live = list(range(len(gens)))
    while live:
        i = min(live, key=lambda j: done[j] / totals[j])
        try:
            done[i] += next(gens[i])
        except StopIteration:
            live.remove(i)


def _launch(name, n_steps, parts, paces=None):
    paces = paces or [1.0] * len(parts)
    n_in = [len(p.args) for p in parts]
    n_out = [len(p.out_shape) for p in parts]
    n_scr = [len(p.scratch) for p in parts]

    def body(*refs):
        refs = list(refs)
        ins = [[refs.pop(0) for _ in range(n)] for n in n_in]
        outs = [[refs.pop(0) for _ in range(n)] for n in n_out]
        scrs = [[refs.pop(0) for _ in range(n)] for n in n_scr]
        _interleave([p.stages(i, o, s) for p, i, o, s in zip(parts, ins, outs, scrs)],
                    [p.cost * pace for p, pace in zip(parts, paces)])

    flat = pl.pallas_call(
        body,
        grid=(n_steps,),
        in_specs=[s for p in parts for s in p.in_specs],
        out_specs=[s for p in parts for s in p.out_specs],
        out_shape=[s for p in parts for s in p.out_shape],
        scratch_shapes=[s for p in parts for s in p.scratch],
        compiler_params=pltpu.CompilerParams(dimension_semantics=("arbitrary",),
                                             vmem_limit_bytes=V7X_VMEM_LIMIT_BYTES),
        name=name,
    )(*[a for p in parts for a in p.args])
    flat = list(flat)
    return [[flat.pop(0) for _ in range(n)] for n in n_out]


class _Mod(NamedTuple):
    rows: Any
    first: int
    shared: bool


def _mod_spec(mod, tile, seq_len, tile_of_step=lambda i: i):
    if mod.shared:
        return pl.BlockSpec((1, 1, N_MOD * D_MODEL), lambda i: (mod.first, 0, 0))
    assert seq_len % tile == 0
    per_seq = seq_len // tile
    return pl.BlockSpec((1, 1, N_MOD * D_MODEL), lambda i: (mod.first + tile_of_step(i) // per_seq, 0, 0))


def _mod_kernel(cctx_ref, c_ref, w_ref, b_ref, o_ref, cond_scr):
    n_req = c_ref.shape[0]
    cond_scr[...] = jnp.zeros_like(cond_scr)
    cond_scr[0:1, :] = cctx_ref[...]
    cond_scr[1:1 + n_req, :] = c_ref[...]
    c = cond_scr[...]
    x = c * jax.nn.sigmoid(c)
    x_hi, x_lo = _split_bf16(x)
    w = w_ref[...].astype(BF16)
    part = _dot(x_hi, w) + _dot(x_lo, w)

    @pl.when(pl.program_id(0) == 0)
    def _():
        o_ref[:, 0, :] = b_ref[...] + part

    @pl.when(pl.program_id(0) > 0)
    def _():
        o_ref[:, 0, :] += part


def _modulation(c_ctx, c, w_ada, b_ada):
    n = N_MOD * D_MODEL
    tk = ADALN_ROW_TILE
    n_req = c.shape[0]
    assert 1 + n_req <= COND_ROWS
    return pl.pallas_call(
        _mod_kernel,
        grid=(D_MODEL // tk,),
        in_specs=[pl.BlockSpec((1, tk), lambda j: (0, j)),
                  pl.BlockSpec((n_req, tk), lambda j: (0, j)),
                  pl.BlockSpec((tk, n), lambda j: (j, 0)),
                  pl.BlockSpec((1, n), lambda j: (0, 0))],
        out_specs=pl.BlockSpec((COND_ROWS, 1, n), lambda j: (0, 0, 0)),
        out_shape=jax.ShapeDtypeStruct((COND_ROWS, 1, n), F32),
        scratch_shapes=[pltpu.VMEM((COND_ROWS, tk), F32)],
        compiler_params=pltpu.CompilerParams(dimension_semantics=("arbitrary",),
                                             vmem_limit_bytes=V7X_VMEM_LIMIT_BYTES),
        name="adaln_mod",
    )(c_ctx, c, w_ada, b_ada)


def _rms_rows(x, w):
    return x * lax.rsqrt(jnp.mean(x * x, axis=-1, keepdims=True) + EPS) * w


def _head_sumsq(a, ones_ref):
    return _dot((a * a).astype(BF16), ones_ref[...])


def _head_rms(a, sumsq, w):
    return a * lax.rsqrt(sumsq * (1.0 / HEAD_DIM) + EPS) * w


def _forget_lower_bound(logits):
    e = jnp.exp(logits - jnp.max(logits, axis=1, keepdims=True))
    return e[:, 0, :] / jnp.sum(e, axis=1)


def _rope(x, cos, s_up, s_dn):
    cols = []
    for j in range(x.shape[1] // LANES):
        sl = slice(j * LANES, (j + 1) * LANES)
        xj = x[:, sl]
        cols.append(xj * cos[:, sl] + pltpu.roll(xj, LANES - ROPE_PAIR_LANES, 1) * s_up[:, sl]
                    + pltpu.roll(xj, ROPE_PAIR_LANES, 1) * s_dn[:, sl])
    return jnp.concatenate(cols, axis=1) if len(cols) > 1 else cols[0]


def _in_stages(latent, ins, outs, scrs):
    x_ref, mod_ref, nw_ref, w_ref, qw_ref, kw_ref, oq_ref, ok_ref, lbl_ref = ins[:9]
    if scrs:
        (w_bf_ref,) = scrs
        rows = W_CAST_ROWS

        @pl.when(pl.program_id(0) == 0)
        def _():
            def cast_rows(r, carry):
                sl = pl.ds(pl.multiple_of(r * rows, rows), rows)
                w_bf_ref[sl, :] = w_ref[sl, :].astype(BF16)
                return carry

            lax.fori_loop(0, w_ref.shape[0] // rows, cast_rows, 0)

        w_ref = w_bf_ref
    if latent:
        cos_ref, sup_ref, sdn_ref = ins[9:]
        zh_ref, q_ref, k_ref, vt_ref, g_ref = outs
    else:
        zh_ref, q_ref, k_ref, kt_ref, vt_ref, g_ref = outs
    tm = x_ref.shape[0]
    m = mod_ref[0]
    sh1 = m[:, 0:D_MODEL]
    sc1 = m[:, D_MODEL:2 * D_MODEL]
    h = _rms_rows(x_ref[...], nw_ref[...]) * (1.0 + sc1) + sh1
    hb = h.astype(BF16)
    c0 = ZH_WIDTH
    c1 = c0 + Q_WIDTH
    c2 = c1 + KV_WIDTH
    c3 = c2 + KV_WIDTH
    aq = _dot(hb, w_ref[:, c0:c1])
    ak = _dot(hb, w_ref[:, c1:c2])
    av = _dot(hb, w_ref[:, c2:c3])
    yield _mm_cost(tm, D_MODEL, c3 - c0)
    gl = yield from _dot_by_cols(hb, w_ref, (0, D_MODEL), (c3, D_IN))
    ssq = _head_sumsq(aq, oq_ref)
    ssk = _head_sumsq(ak, ok_ref)
    yield _mm_cost(tm, Q_WIDTH, Q_WIDTH) + _mm_cost(tm, KV_WIDTH, KV_WIDTH)
    lb = _forget_lower_bound(lbl_ref[...])

    def hgrn_segment(s):
        z = yield from _dot_by_cols(hb, w_ref, (0, D_MODEL), (s * HG_WIDTH, (s + 1) * HG_WIDTH))
        if s == 0:
            z = z * HG_DK ** -0.5
        elif s in (1, 2):
            lbv = lb[s - 1:s, :]
            z = jnp.log(lbv + (1.0 - lbv) * jax.nn.sigmoid(z))
        elif s == 4:
            z = z * jax.nn.sigmoid(z)
        zh_ref[:, s * HG_WIDTH:(s + 1) * HG_WIDTH] = z

    yield from hgrn_segment(1)
    yield from hgrn_segment(2)
    g_ref[...] = jax.nn.sigmoid(gl).astype(BF16)
    qn = _head_rms(aq, ssq, jnp.tile(qw_ref[...], (1, N_HEADS)))
    kn = _head_rms(ak, ssk, jnp.tile(kw_ref[...], (1, N_KV)))
    seq = vt_ref.shape[2]
    avt = av.T
    for s in range(vt_ref.shape[0]):
        vt_ref[s] = avt[:, s * seq:(s + 1) * seq].astype(vt_ref.dtype)
    for s in (4, 0, 3):
        yield from hgrn_segment(s)
    if latent:
        cos, sup, sdn = cos_ref[...], sup_ref[...], sdn_ref[...]
        qn = _rope(qn, cos, sup, sdn)
        kn = _rope(kn, cos[:, 0:KV_WIDTH], sup[:, 0:KV_WIDTH], sdn[:, 0:KV_WIDTH])
    else:
        knt = kn.T
        for s in range(kt_ref.shape[0]):
            kt_ref[s] = knt[:, s * seq:(s + 1) * seq]
    q_ref[...] = (qn * (HEAD_DIM ** -0.5 * LOG2_E)).astype(BF16)
    k_ref[...] = kn.astype(BF16)


def _in_part(tm, x2, mod, seq_len, norm_w, w_in, qw, kw, ones_q, ones_k, lb_logits, rope):
    t = x2.shape[0]
    latent = rope is not None
    per_seq = max(seq_len // tm, 1)
    row = lambda i: (i, 0)
    in_specs = [pl.BlockSpec((tm, D_MODEL), row),
                _mod_spec(mod, tm, seq_len),
                _const_spec((1, D_MODEL)),
                _const_spec((D_MODEL, D_IN)),
                _const_spec((1, HEAD_DIM)),
                _const_spec((1, HEAD_DIM)),
                _const_spec((Q_WIDTH, Q_WIDTH)),
                _const_spec((KV_WIDTH, KV_WIDTH)),
                _const_spec(lb_logits.shape)]
    args = [x2, mod.rows, norm_w, w_in, qw, kw, ones_q, ones_k, lb_logits]
    out_specs = [pl.BlockSpec((tm, ZH_WIDTH), row), pl.BlockSpec((tm, Q_WIDTH), row),
                 pl.BlockSpec((tm, KV_WIDTH), row)]
    out_shape = [jax.ShapeDtypeStruct((t, ZH_WIDTH), F32), jax.ShapeDtypeStruct((t, Q_WIDTH), BF16),
                 jax.ShapeDtypeStruct((t, KV_WIDTH), BF16)]
    if seq_len <= tm:
        assert tm % seq_len == 0
        t_spec = pl.BlockSpec((tm // seq_len, KV_WIDTH, seq_len), lambda i: (i, 0, 0))
    else:
        t_spec = pl.BlockSpec((1, KV_WIDTH, tm), lambda i: (i // per_seq, 0, i % per_seq))
    if latent:
        in_specs += [pl.BlockSpec((tm, Q_WIDTH), lambda i: (i % per_seq, 0))] * 3
        args += list(rope)
        out_specs.append(t_spec)
        out_shape.append(jax.ShapeDtypeStruct((t // seq_len, KV_WIDTH, seq_len), BF16))
    else:
        out_specs += [t_spec, t_spec]
        out_shape += [jax.ShapeDtypeStruct((t // seq_len, KV_WIDTH, seq_len), F32)] * 2
    out_specs.append(pl.BlockSpec((tm, 2 * D_MODEL), row))
    out_shape.append(jax.ShapeDtypeStruct((t, 2 * D_MODEL), BF16))
    cost = (_mm_cost(tm, D_MODEL, D_IN) + _mm_cost(tm, Q_WIDTH, Q_WIDTH) + _mm_cost(tm, KV_WIDTH, KV_WIDTH))
    scratch = [] if w_in.dtype == BF16 else [pltpu.VMEM((D_MODEL, D_IN), BF16)]
    return t // tm, _Part(functools.partial(_in_stages, latent), args, in_specs, out_specs, out_shape, scratch, cost)


def _attn_cost(lq, lks):
    per_head = sum(_mm_cost(lk, HEAD_DIM, lq) + _mm_cost(HEAD_DIM + ONES_ROWS, lk, lq) for lk in lks)
    return N_HEADS * per_head


def _attn_stages(k_transposed, lookahead, ins, outs, _):
    n_seg = len(k_transposed)
    q_ref, kv_refs = ins[0], ins[1:]
    (o_ref,) = outs
    q = q_ref[...]
    lq = q.shape[0]
    ks = [(kv_refs[2 * s][0].T if k_transposed[s] else kv_refs[2 * s][...]).astype(BF16) for s in range(n_seg)]
    vts = [kv_refs[2 * s + 1][0].astype(BF16) for s in range(n_seg)]
    vts = [[jnp.concatenate([vt[g * HEAD_DIM:(g + 1) * HEAD_DIM, :],
                             jnp.ones((ONES_ROWS, vt.shape[1]), BF16)], axis=0) for g in range(N_KV)]
           for vt in vts]
    score_cost = sum(_mm_cost(k.shape[0], HEAD_DIM, lq) for k in ks)
    value_cost = sum(_mm_cost(HEAD_DIM + ONES_ROWS, k.shape[0], lq) for k in ks)

    def scores(h):
        g = h // (N_HEADS // N_KV)
        qh = q[:, h * HEAD_DIM:(h + 1) * HEAD_DIM]
        return [_dot_nt(k[:, g * HEAD_DIM:(g + 1) * HEAD_DIM], qh) for k in ks]

    pending = []
    for h in range(lookahead):
        pending.append(scores(h))
        yield score_cost
    outs_t = []
    for h in range(N_HEADS):
        if h + lookahead < N_HEADS:
            pending.append(scores(h + lookahead))
            yield score_cost
        st = pending.pop(0)
        g = h // (N_HEADS // N_KV)
        m = functools.reduce(jnp.maximum, [jnp.max(s, axis=0, keepdims=True) for s in st])
        ot = functools.reduce(jnp.add, [_dot(vt[g], jnp.exp2(s - m).astype(BF16)) for vt, s in zip(vts, st)])
        outs_t.append(ot[0:HEAD_DIM, :] / ot[HEAD_DIM:HEAD_DIM + 1, :])
        if h == N_HEADS - 1:
            o_ref[...] = jnp.concatenate(outs_t, axis=0).T.astype(BF16)
        yield value_cost


def _attn_part(q, kv_segments, n_batch, seq_len, lookahead):
    t = q.shape[0]
    tq = ATTN_Q_TILE
    per_seq = seq_len // tq
    in_specs = [pl.BlockSpec((tq, Q_WIDTH), lambda i: (i, 0))]
    args = [q]
    for k, v_t, layer in kv_segments:
        lk = v_t.shape[-1]
        if v_t.ndim == 4:
            slab = pl.BlockSpec((1, None, KV_WIDTH, lk), lambda i, layer=layer: (i // per_seq, layer, 0, 0))
        else:
            slab = pl.BlockSpec((1, KV_WIDTH, lk), lambda i: (i // per_seq, 0, 0))
        k_spec = slab if k.ndim == v_t.ndim else pl.BlockSpec((lk, KV_WIDTH), lambda i: (i // per_seq, 0))
        in_specs += [k_spec, slab]
        args += [k, v_t]
    stages = functools.partial(_attn_stages, tuple(k.ndim == v_t.ndim for k, v_t, _ in kv_segments), lookahead)
    return n_batch * per_seq, _Part(stages, args, in_specs, [pl.BlockSpec((tq, Q_WIDTH), lambda i: (i, 0))],
                                    [jax.ShapeDtypeStruct((t, Q_WIDTH), BF16)], [],
                                    _attn_cost(tq, [v_t.shape[-1] for _, v_t, _ in kv_segments]))


def _hgrn_unit_cost():
    blk = HG_BLOCK
    return (2 * _mm_cost(blk, blk, 2 * HG_DK)
            + 2 * _mm_cost(blk, HG_DK, 2 * blk) + _mm_cost(blk, blk, HG_DV)
            + (blk // HG_PAIR) * (_mm_cost(HG_DV, HG_PAIR, 2 * HG_DK) + _mm_cost(HG_PAIR, 2 * HG_DK, HG_DV)))


def _hgrn_stages(seq_len, hps, has_s0, has_sfin, ins, outs, scrs):
    tf_ref, tb_ref, xf_ref, xb_ref, q_ref, lf_ref, lb_ref, v_ref, g_ref, nw_ref = ins[:10]
    s0_ref = ins[10] if has_s0 else None
    o_ref = outs[0]
    sfin_ref = outs[1] if has_sfin else None
    kv_scr, ss_scr, qd_scr, oi_scr, dec_scr = scrs
    n_blk = seq_len // HG_BLOCK
    n_pair = seq_len // HG_PAIR
    per_blk = HG_BLOCK // HG_CHUNK
    pairs_per_blk = HG_BLOCK // HG_PAIR
    one_row = jnp.ones((1, HG_DK), F32)

    def chunk_cumsum(t_ref, x):
        hi, lo = _split_bf16(x)
        r = _dot(t_ref[...], jnp.concatenate([hi, lo], axis=1))
        return r[:, 0:HG_DK] + r[:, HG_DK:2 * HG_DK]

    def chunk_edge_rows(x, first):
        off = 0 if first else HG_CHUNK - 1
        return [x[c * HG_CHUNK + off:c * HG_CHUNK + off + 1, :] for c in range(per_blk)]

    def spread_rows(rows):
        return jnp.concatenate([jnp.broadcast_to(r, (HG_CHUNK, HG_DK)) for r in rows], axis=0)

    units = [(blk, h) for blk in range(n_blk) for h in range(hps)]
    stage1 = []
    for blk, h in units:
        rows = slice(blk * HG_BLOCK, (blk + 1) * HG_BLOCK)
        cols = slice(h * HG_DK, (h + 1) * HG_DK)
        logf_f = lf_ref[rows, cols]
        logf_b = lb_ref[rows, cols]
        k_f = 1.0 - jnp.exp(logf_f)
        k_b = 1.0 - jnp.exp(logf_b)
        a_f = chunk_cumsum(tf_ref, logf_f)
        a_b = chunk_cumsum(tb_ref, logf_b)
        stage1.append((rows, cols, k_f, k_b, a_f, a_b))
        yield 2 * _mm_cost(HG_BLOCK, HG_BLOCK, 2 * HG_DK)
    stage2 = []
    for (blk, h), (rows, cols, k_f, k_b, a_f, a_b) in zip(units, stage1):
        q = q_ref[rows, cols]
        vb = v_ref[rows, cols].astype(BF16)
        ea_f = jnp.exp(a_f)
        ea_b = jnp.exp(a_b)
        qd_f = q * ea_f
        qd_b = q * ea_b
        kd_f = k_f * jnp.exp(-a_f)
        kd_b = k_b * jnp.exp(-a_b)
        dec_f = chunk_edge_rows(ea_f, first=False)
        dec_b = chunk_edge_rows(ea_b, first=True)
        ke_f = kd_f * spread_rows(dec_f)
        ke_b = kd_b * spread_rows(dec_b)
        even = [c % 2 == 0 for c in range(per_blk)]
        qd_pair = jnp.concatenate(
            [qd_f * spread_rows([one_row if even[c] else dec_f[c - 1] for c in range(per_blk)]),
             qd_b * spread_rows([dec_b[c + 1] if even[c] else one_row for c in range(per_blk)])], axis=1)
        ke_pair = jnp.concatenate(
            [ke_f * spread_rows([dec_f[c + 1] if even[c] else one_row for c in range(per_blk)]),
             ke_b * spread_rows([one_row if even[c] else dec_b[c - 1] for c in range(per_blk)])], axis=1).astype(BF16)
        sc_f = _dot_nt(qd_f.astype(BF16), jnp.concatenate([kd_f, ke_f], axis=0).astype(BF16))
        sc_b = _dot_nt(qd_b.astype(BF16), jnp.concatenate([kd_b, ke_b], axis=0).astype(BF16))
        for p in range(pairs_per_blk):
            pr = slice(p * HG_PAIR, (p + 1) * HG_PAIR)
            kv_scr[h, blk * pairs_per_blk + p] = _dot_tn(vb[pr, :], ke_pair[pr, :])
            dec_scr[h, blk * pairs_per_blk + p, 0:1, :] = jnp.concatenate(
                [dec_f[2 * p] * dec_f[2 * p + 1], dec_b[2 * p] * dec_b[2 * p + 1]], axis=1)
        qd_scr[h, rows, :] = qd_pair.astype(BF16)
        stage2.append((rows, cols, vb, sc_f, sc_b))
        yield 2 * _mm_cost(HG_BLOCK, HG_DK, 2 * HG_BLOCK) + pairs_per_blk * _mm_cost(HG_DV, HG_PAIR, 2 * HG_DK)
    for rows, cols, vb, sc_f, sc_b in stage2:
        s = (sc_f[:, 0:HG_BLOCK].astype(BF16) * tf_ref[...] + sc_f[:, HG_BLOCK:].astype(BF16) * xf_ref[...]
             + sc_b[:, 0:HG_BLOCK].astype(BF16) * tb_ref[...] + sc_b[:, HG_BLOCK:].astype(BF16) * xb_ref[...])
        oi_scr[rows, cols] = _dot(s, vb)
        yield _mm_cost(HG_BLOCK, HG_BLOCK, HG_DV)

    for h in range(hps):
        if has_s0:
            s_f, s_b = s0_ref[0, 0, h].T, s0_ref[0, 1, h].T
        else:
            s_f = s_b = jnp.zeros((HG_DV, HG_DK), F32)
        for pf in range(n_pair):
            pb = n_pair - 1 - pf
            ss_scr[h, pf, :, 0:HG_DK] = s_f.astype(BF16)
            s_f = dec_scr[h, pf, 0:1, 0:HG_DK] * s_f + kv_scr[h, pf, :, 0:HG_DK]
            ss_scr[h, pb, :, HG_DK:2 * HG_DK] = s_b.astype(BF16)
            s_b = dec_scr[h, pb, 0:1, HG_DK:2 * HG_DK] * s_b + kv_scr[h, pb, :, HG_DK:2 * HG_DK]
        if has_sfin:
            sfin_ref[0, 0, h] = s_f.T
            sfin_ref[0, 1, h] = s_b.T

    for blk, h in units:
        cols = slice(h * HG_DK, (h + 1) * HG_DK)
        for p in range(pairs_per_blk):
            pi = blk * pairs_per_blk + p
            rows = slice(pi * HG_PAIR, (pi + 1) * HG_PAIR)
            oi_scr[rows, cols] = oi_scr[rows, cols] + _dot_nt(qd_scr[h, rows, :], ss_scr[h, pi])
        yield pairs_per_blk * _mm_cost(HG_PAIR, 2 * HG_DK, HG_DV)

    for h in range(hps):
        cols = slice(h * HG_DK, (h + 1) * HG_DK)
        o = _rms_rows(oi_scr[:, cols], nw_ref[...])
        o_ref[:, cols] = (o * g_ref[:, cols]).astype(BF16)


def _chunk_masks():
    r = np.arange(HG_BLOCK)
    same = (r[:, None] // HG_CHUNK) == (r[None, :] // HG_CHUNK)
    lower = same & (r[None, :] <= r[:, None])
    upper = same & (r[None, :] >= r[:, None])
    same_pair = (r[:, None] // HG_PAIR) == (r[None, :] // HG_PAIR)
    cross_f = same_pair & (r[:, None] // HG_CHUNK == r[None, :] // HG_CHUNK + 1)
    return tuple(jnp.asarray(m.astype(np.float32), dtype=BF16) for m in (lower, upper, cross_f, cross_f.T))


def _hgrn_part(zh, norm_w, n_batch, seq_len, hps, s0, want_final):
    t = zh.shape[0]
    n_pair = seq_len // HG_PAIR
    width = hps * HG_DK
    n_hb = HG_HEADS // hps
    masks = _chunk_masks()

    def seg_spec(seg):
        return pl.BlockSpec((seq_len, width), lambda i: (i // n_hb, seg * n_hb + i % n_hb))

    state_spec = pl.BlockSpec((1, 2, hps, HG_DK, HG_DV), lambda i: (i // n_hb, 0, i % n_hb, 0, 0))
    mask_spec = pl.BlockSpec((HG_BLOCK, HG_BLOCK), lambda i: (0, 0))
    in_specs = [mask_spec] * len(masks) + [seg_spec(s) for s in range(5)] + [pl.BlockSpec((1, HG_DV), lambda i: (0, 0))]
    args = list(masks) + [zh] * 5 + [norm_w]
    if s0 is not None:
        states, layer = s0
        in_specs.append(pl.BlockSpec((1, None, 2, hps, HG_DK, HG_DV),
                                     lambda i: (i // n_hb, layer, 0, i % n_hb, 0, 0)))
        args.append(states)
    out_specs = [pl.BlockSpec((seq_len, width), lambda i: (i // n_hb, i % n_hb))]
    out_shape = [jax.ShapeDtypeStruct((t, HG_WIDTH), BF16)]
    if want_final:
        out_specs.append(state_spec)
        out_shape.append(jax.ShapeDtypeStruct((n_batch, 2, HG_HEADS, HG_DK, HG_DV), F32))
    scratch = [pltpu.VMEM((hps, n_pair, HG_DV, 2 * HG_DK), F32),
               pltpu.VMEM((hps, n_pair, HG_DV, 2 * HG_DK), BF16),
               pltpu.VMEM((hps, seq_len, 2 * HG_DK), BF16),
               pltpu.VMEM((seq_len, width), F32),
               pltpu.VMEM((hps, n_pair, SUBLANES, 2 * HG_DK), F32)]
    stages = functools.partial(_hgrn_stages, seq_len, hps, s0 is not None, want_final)
    cost = hps * (seq_len // HG_BLOCK) * _hgrn_unit_cost()
    return n_batch * n_hb, _Part(stages, args, in_specs, out_specs, out_shape, scratch, cost)


def _out_stages(ins, outs, _):
    x_ref, mod_ref, oh_ref, oa_ref, g_ref, who_ref, wao_ref, wout_ref, nfw_ref, wff1_ref, wff2_ref, fnw_ref = ins
    (y_ref,) = outs
    tm = x_ref.shape[0]
    m = mod_ref[0]
    g1 = m[:, 2 * D_MODEL:3 * D_MODEL]
    sh2 = m[:, 3 * D_MODEL:4 * D_MODEL]
    sc2 = m[:, 4 * D_MODEL:5 * D_MODEL]
    g2 = m[:, 5 * D_MODEL:6 * D_MODEL]
    gates = g_ref[...].astype(F32)
    from_h = yield from _dot_by_cols(oh_ref[...], who_ref, (0, HG_WIDTH), (0, D_MODEL))
    from_a = yield from _dot_by_cols(oa_ref[...], wao_ref, (0, Q_WIDTH), (0, D_MODEL))
    merged = gates[:, 0:D_MODEL] * from_h + gates[:, D_MODEL:2 * D_MODEL] * from_a
    mixed = yield from _dot_by_cols(merged.astype(BF16), wout_ref, (0, D_MODEL), (0, D_MODEL))
    x1 = x_ref[...] + g1 * mixed
    h2 = (_rms_rows(x1, nfw_ref[...]) * (1.0 + sc2) + sh2).astype(BF16)
    acc = jnp.zeros_like(x1)
    for j in range(D_FF // D_MODEL):
        cols = slice(j * D_MODEL, (j + 1) * D_MODEL)
        hj = yield from _dot_by_cols(h2, wff1_ref, (0, D_MODEL), (cols.start, cols.stop))
        hj = jnp.maximum(hj, 0.0)
        acc = acc + (yield from _dot_by_cols((hj * hj).astype(BF16), wff2_ref, (cols.start, cols.stop), (0, D_MODEL)))
    y_ref[...] = _rms_rows(x1 + g2 * acc, fnw_ref[...])


def _out_part(tm, x2, mod, seq_len, oh, oa, gates, w_ho, w_ao, w_out, nfw, w_ff1, w_ff2, fnw):
    t = x2.shape[0]
    row = lambda i: (i, 0)
    in_specs = [pl.BlockSpec((tm, D_MODEL), row),
                _mod_spec(mod, tm, seq_len),
                pl.BlockSpec((tm, HG_WIDTH), row),
                pl.BlockSpec((tm, Q_WIDTH), row),
                pl.BlockSpec((tm, 2 * D_MODEL), row),
                _const_spec((HG_WIDTH, D_MODEL)),
                _const_spec((Q_WIDTH, D_MODEL)),
                _const_spec((D_MODEL, D_MODEL)),
                _const_spec((1, D_MODEL)),
                _const_spec((D_MODEL, D_FF)),
                _const_spec((D_FF, D_MODEL)),
                _const_spec((1, D_MODEL))]
    args = [x2, mod.rows, oh, oa, gates, w_ho, w_ao, w_out, nfw, w_ff1, w_ff2, fnw]
    cost = _mm_cost(tm, HG_WIDTH + Q_WIDTH, D_MODEL) + _mm_cost(tm, D_MODEL, D_MODEL) + 2 * _mm_cost(tm, D_MODEL, D_FF)
    return t // tm, _Part(_out_stages, args, in_specs, [pl.BlockSpec((tm, D_MODEL), row)],
                          [jax.ShapeDtypeStruct((t, D_MODEL), F32)], [], cost)


def _cast_stages(ins, outs, _):
    for src, dst in zip(ins, outs):
        dst[...] = src[...].astype(dst.dtype)
    yield 1


def _cast_part(arrays, n_steps):
    in_specs, out_specs, out_shape = [], [], []
    for a in arrays:
        rows, cols = a.shape
        assert rows % (n_steps * MXU_ROWS) == 0
        spec = pl.BlockSpec((rows // n_steps, cols), lambda i: (i, 0))
        in_specs.append(spec)
        out_specs.append(spec)
        out_shape.append(jax.ShapeDtypeStruct(a.shape, BF16))
    return n_steps, _Part(_cast_stages, list(arrays), in_specs, out_specs, out_shape, [], 1)


def _rope_tables(n_tokens):
    rows = n_tokens // GRID_W
    row = np.repeat(np.arange(rows, dtype=np.float32), GRID_W)
    col = np.tile(np.arange(GRID_W, dtype=np.float32), rows)
    axis_dim = HEAD_DIM // 2
    freqs = (ROPE_THETA ** (-np.arange(0, axis_dim, 2, dtype=np.float32) / axis_dim)).astype(np.float32)
    ang_r = row[:, None] * freqs
    ang_c = col[:, None] * freqs
    cr, sr, cc, sc = np.cos(ang_r), np.sin(ang_r), np.cos(ang_c), np.sin(ang_c)
    zero = np.zeros_like(sr)
    cos = np.concatenate([cr, cr, cc, cc], axis=-1)
    s_up = np.concatenate([-sr, zero, -sc, zero], axis=-1)
    s_dn = np.concatenate([zero, sr, zero, sc], axis=-1)
    return tuple(jnp.asarray(np.tile(a, (1, N_HEADS)), dtype=F32) for a in (cos, s_up, s_dn))


def _to_slab(kv):
    n, depth, length = kv.shape[:3]
    return jnp.transpose(kv, (0, 1, 3, 4, 2)).reshape(n, depth, KV_WIDTH, length)


def _from_slab(slab):
    n, _, length = slab.shape
    return jnp.transpose(slab.reshape(n, N_KV, HEAD_DIM, length), (0, 3, 1, 2))


def _block_ones(width):
    idx = np.arange(width) // HEAD_DIM
    return jnp.asarray((idx[:, None] == idx[None, :]).astype(np.float32), dtype=BF16)


def _same_steps(*counted_parts):
    steps = {n for n, _ in counted_parts}
    assert len(steps) == 1, steps
    return steps.pop(), [p for _, p in counted_parts]


def kernel(x_prompt, x_sample, cache_k, cache_v, state_hgrn, c, c_ctx, w_ada, b_ada, norm_mix_w, w_in, q_norm_w, k_norm_w, hgrn_lb_logits, hgrn_norm_w, w_hgrn_out, w_attn_out, w_out, norm_ffn_w, w_ff1, w_ff2, final_norm_w):
    n_p, l_p, _ = x_prompt.shape
    n_s, l_s, _ = x_sample.shape
    layer = 0

    mod = _modulation(c_ctx[None, :], c, w_ada[layer], b_ada[layer][None, :])
    mod_p = _Mod(mod, 0, True)
    mod_s = _Mod(mod, 1, False)

    in_small = (q_norm_w[layer][None, :], k_norm_w[layer][None, :],
                _block_ones(Q_WIDTH), _block_ones(KV_WIDTH), hgrn_lb_logits)
    nmw = norm_mix_w[layer][None, :]
    hnw = hgrn_norm_w[layer][None, :]
    xp2 = x_prompt.reshape(n_p * l_p, D_MODEL)
    xs2 = x_sample.reshape(n_s * l_s, D_MODEL)

    n_in, in_p = _in_part(MIX_TOKEN_TILE, xp2, mod_p, l_p, nmw, w_in[layer], *in_small, None)
    steps, parts = _same_steps((n_in, in_p), _cast_part(
        [w_in[layer], w_hgrn_out[layer], w_attn_out[layer], w_out[layer], w_ff1[layer], w_ff2[layer]], n_in))
    (zh_p, q_p, k_p, kt_p, vt_p, gates_p), (w_in_b, w_ho_b, w_ao_b, w_out_b, w_ff1_b, w_ff2_b) = _launch(
        "in_ctx", steps, parts)
    out_w = (w_ho_b, w_ao_b, w_out_b, norm_ffn_w[layer][None, :], w_ff1_b, w_ff2_b, final_norm_w[None, :])

    steps, parts = _same_steps(
        _attn_part(q_p, [(k_p, vt_p, None)], n_p, l_p, N_HEADS),
        _in_part(MIX_TOKEN_TILE, xs2, mod_s, l_s, nmw, w_in_b, *in_small, _rope_tables(l_s)),
        _hgrn_part(zh_p, hnw, n_p, l_p, HG_HEADS, None, True))
    (oa_p,), (zh_s, q_s, k_s, vt_s, gates_s), (oh_p, s_fin) = _launch("in_latent_mix_ctx", steps, parts,
                                                                      MIX_PACES_IN)

    segs = [(_to_slab(cache_k), _to_slab(cache_v), layer), (k_s, vt_s, None)]
    steps, parts = _same_steps(
        _out_part(MIX_TOKEN_TILE, xp2, mod_p, l_p, oh_p, oa_p, gates_p, *out_w),
        _attn_part(q_s, segs, n_s, l_s, ATTN_LOOKAHEAD),
        _hgrn_part(zh_s, hnw, n_s, l_s, 1, (state_hgrn, layer), False))
    (y_p,), (oa_s,), (oh_s,) = _launch("out_ctx_mix_latent", steps, parts, MIX_PACES_OUT)

    steps, parts = _same_steps(_out_part(SOLO_TOKEN_TILE, xs2, mod_s, l_s, oh_s, oa_s, gates_s, *out_w))
    ((y_s,),) = _launch("out_latent", steps, parts)

    new_k = _from_slab(kt_p)[:, None]
    new_v = _from_slab(vt_p)[:, None]
    new_s = s_fin.reshape(n_p, 1, 2, HG_HEADS, HG_DK, HG_DV)
    return (y_p.reshape(n_p, l_p, D_MODEL), y_s.reshape(n_s, l_s, D_MODEL), new_k, new_v, new_s)
```

```python
import functools
from typing import Any, Callable, NamedTuple

import numpy as np
import jax
import jax.numpy as jnp
from jax import lax
from jax.experimental import pallas as pl
from jax.experimental.pallas import tpu as pltpu

F32 = jnp.float32
BF16 = jnp.bfloat16

D_MODEL = 1024
GRID_W = 64
EPS = 1e-6
HG_HEADS = 4
HG_DK = 128
HG_DV = 128
HG_WIDTH = HG_HEADS * HG_DK
HG_CHUNK = 32
HG_PAIR = 2 * HG_CHUNK
N_HEADS = 8
N_KV = 2
HEAD_DIM = 64
Q_WIDTH = N_HEADS * HEAD_DIM
KV_WIDTH = N_KV * HEAD_DIM
ROPE_THETA = 10000.0
D_FF = 4 * D_MODEL
N_MOD = 6
ZH_WIDTH = 5 * HG_WIDTH
D_IN = ZH_WIDTH + Q_WIDTH + 2 * KV_WIDTH + 2 * D_MODEL

V7X_VMEM_LIMIT_BYTES = 56 * 1024 * 1024
SUBLANES = 8
LANES = 128
MXU_ROWS = 16
MXU_TILE = 256
ROPE_PAIR_LANES = HEAD_DIM // 4
HG_BLOCK = 256
COND_ROWS = 16
ONES_ROWS = 16
LOG2_E = float(np.log2(np.e))
MIX_TOKEN_TILE = 256
SOLO_TOKEN_TILE = 512
ADALN_ROW_TILE = 256
W_CAST_ROWS = 128
DOT_COLS = 512
ATTN_Q_TILE = 256
ATTN_LOOKAHEAD = 2
MIX_PACES_IN = (1.0, 1.0, 0.8)
MIX_PACES_OUT = (1.0, 1.0, 0.9)


def _dot(a, b):
    return jnp.dot(a, b, preferred_element_type=F32)


def _dot_nt(a, b):
    return lax.dot_general(a, b, (((1,), (1,)), ((), ())), preferred_element_type=F32)


def _dot_tn(a, b):
    return lax.dot_general(a, b, (((0,), (0,)), ((), ())), preferred_element_type=F32)


def _mm_cost(m, k, n):
    return (m // MXU_ROWS) * -(-k // MXU_TILE) * -(-n // MXU_TILE)


def _dot_by_cols(a, w_ref, rows, cols):
    pieces = []
    for c in range(cols[0], cols[1], DOT_COLS):
        end = min(c + DOT_COLS, cols[1])
        pieces.append(_dot(a, w_ref[rows[0]:rows[1], c:end]))
        yield _mm_cost(a.shape[0], rows[1] - rows[0], end - c)
    return pieces[0] if len(pieces) == 1 else jnp.concatenate(pieces, axis=1)


def _split_bf16(x):
    hi = x.astype(BF16)
    lo = (x - hi.astype(F32)).astype(BF16)
    return hi, lo


def _const_spec(shape):
    nd = len(shape)
    return pl.BlockSpec(shape, lambda *_: (0,) * nd, pipeline_mode=pl.Buffered(1))


class _Part(NamedTuple):
    stages: Callable[..., Any]
    args: list
    in_specs: list
    out_specs: list
    out_shape: list
    scratch: list
    cost: int


def _interleave(gens, totals):
    done = [0] * len(gens)
    live = list(range(len(gens)))
    while live:
        i = min(live, key=lambda j: done[j] / totals[j])
        try:
            done[i] += next(gens[i])
        except StopIteration:
            live.remove(i)


def _launch(name, n_steps, parts, paces=None):
    paces = paces or [1.0] * len(parts)
    n_in = [len(p.args) for p in parts]
    n_out = [len(p.out_shape) for p in parts]
    n_scr = [len(p.scratch) for p in parts]

    def body(*refs):
        refs = list(refs)
        ins = [[refs.pop(0) for _ in range(n)] for n in n_in]
        outs = [[refs.pop(0) for _ in range(n)] for n in n_out]
        scrs = [[refs.pop(0) for _ in range(n)] for n in n_scr]
        _interleave([p.stages(i, o, s) for p, i, o, s in zip(parts, ins, outs, scrs)],
                    [p.cost * pace for p, pace in zip(parts, paces)])

    flat = pl.pallas_call(
        body,
        grid=(n_steps,),
        in_specs=[s for p in parts for s in p.in_specs],
        out_specs=[s for p in parts for s in p.out_specs],
        out_shape=[s for p in parts for s in p.out_shape],
        scratch_shapes=[s for p in parts for s in p.scratch],
        compiler_params=pltpu.CompilerParams(dimension_semantics=("arbitrary",),
                                             vmem_limit_bytes=V7X_VMEM_LIMIT_BYTES),
        name=name,
    )(*[a for p in parts for a in p.args])
    flat = list(flat)
    return [[flat.pop(0) for _ in range(n)] for n in n_out]


class _Mod(NamedTuple):
    rows: Any
    first: int
    shared: bool


def _mod_spec(mod, tile, seq_len, tile_of_step=lambda i: i):
    if mod.shared:
        return pl.BlockSpec((1, 1, N_MOD * D_MODEL), lambda i: (mod.first, 0, 0))
    assert seq_len % tile == 0
    per_seq = seq_len // tile
    return pl.BlockSpec((1, 1, N_MOD * D_MODEL), lambda i: (mod.first + tile_of_step(i) // per_seq, 0, 0))


def _mod_kernel(cctx_ref, c_ref, w_ref, b_ref, o_ref, cond_scr):
    n_req = c_ref.shape[0]
    cond_scr[...] = jnp.zeros_like(cond_scr)
    cond_scr[0:1, :] = cctx_ref[...]
    cond_scr[1:1 + n_req, :] = c_ref[...]
    c = cond_scr[...]
    x = c * jax.nn.sigmoid(c)
    x_hi, x_lo = _split_bf16(x)
    w = w_ref[...].astype(BF16)
    part = _dot(x_hi, w) + _dot(x_lo, w)

    @pl.when(pl.program_id(0) == 0)
    def _():
        o_ref[:, 0, :] = b_ref[...] + part

    @pl.when(pl.program_id(0) > 0)
    def _():
        o_ref[:, 0, :] += part


def _modulation(c_ctx, c, w_ada, b_ada):
    n = N_MOD * D_MODEL
    tk = ADALN_ROW_TILE
    n_req = c.shape[0]
    assert 1 + n_req <= COND_ROWS
    return pl.pallas_call(
        _mod_kernel,
        grid=(D_MODEL // tk,),
        in_specs=[pl.BlockSpec((1, tk), lambda j: (0, j)),
                  pl.BlockSpec((n_req, tk), lambda j: (0, j)),
                  pl.BlockSpec((tk, n), lambda j: (j, 0)),
                  pl.BlockSpec((1, n), lambda j: (0, 0))],
        out_specs=pl.BlockSpec((COND_ROWS, 1, n), lambda j: (0, 0, 0)),
        out_shape=jax.ShapeDtypeStruct((COND_ROWS, 1, n), F32),
        scratch_shapes=[pltpu.VMEM((COND_ROWS, tk), F32)],
        compiler_params=pltpu.CompilerParams(dimension_semantics=("arbitrary",),
                                             vmem_limit_bytes=V7X_VMEM_LIMIT_BYTES),
        name="adaln_mod",
    )(c_ctx, c, w_ada, b_ada)


def _rms_rows(x, w):
    return x * lax.rsqrt(jnp.mean(x * x, axis=-1, keepdims=True) + EPS) * w


def _head_sumsq(a, ones_ref):
    return _dot((a * a).astype(BF16), ones_ref[...])


def _head_rms(a, sumsq, w):
    return a * lax.rsqrt(sumsq * (1.0 / HEAD_DIM) + EPS) * w


def _forget_lower_bound(logits):
    e = jnp.exp(logits - jnp.max(logits, axis=1, keepdims=True))
    return e[:, 0, :] / jnp.sum(e, axis=1)


def _rope(x, cos, s_up, s_dn):
    cols = []
    for j in range(x.shape[1] // LANES):
        sl = slice(j * LANES, (j + 1) * LANES)
        xj = x[:, sl]
        cols.append(xj * cos[:, sl] + pltpu.roll(xj, LANES - ROPE_PAIR_LANES, 1) * s_up[:, sl]
                    + pltpu.roll(xj, ROPE_PAIR_LANES, 1) * s_dn[:, sl])
    return jnp.concatenate(cols, axis=1) if len(cols) > 1 else cols[0]


def _in_stages(latent, ins, outs, scrs):
    x_ref, mod_ref, nw_ref, w_ref, qw_ref, kw_ref, oq_ref, ok_ref, lbl_ref = ins[:9]
    if scrs:
        (w_bf_ref,) = scrs
        rows = W_CAST_ROWS

        @pl.when(pl.program_id(0) == 0)
        def _():
            def cast_rows(r, carry):
                sl = pl.ds(pl.multiple_of(r * rows, rows), rows)
                w_bf_ref[sl, :] = w_ref[sl, :].astype(BF16)
                return carry

            lax.fori_loop(0, w_ref.shape[0] // rows, cast_rows, 0)

        w_ref = w_bf_ref
    if latent:
        cos_ref, sup_ref, sdn_ref = ins[9:]
        zh_ref, q_ref, k_ref, vt_ref, g_ref = outs
    else:
        zh_ref, q_ref, k_ref, kt_ref, vt_ref, g_ref = outs
    tm = x_ref.shape[0]
    m = mod_ref[0]
    sh1 = m[:, 0:D_MODEL]
    sc1 = m[:, D_MODEL:2 * D_MODEL]
    h = _rms_rows(x_ref[...], nw_ref[...]) * (1.0 + sc1) + sh1
    hb = h.astype(BF16)
    c0 = ZH_WIDTH
    c1 = c0 + Q_WIDTH
    c2 = c1 + KV_WIDTH
    c3 = c2 + KV_WIDTH
    aq = _dot(hb, w_ref[:, c0:c1])
    ak = _dot(hb, w_ref[:, c1:c2])
    av = _dot(hb, w_ref[:, c2:c3])
    yield _mm_cost(tm, D_MODEL, c3 - c0)
    gl = yield from _dot_by_cols(hb, w_ref, (0, D_MODEL), (c3, D_IN))
    ssq = _head_sumsq(aq, oq_ref)
    ssk = _head_sumsq(ak, ok_ref)
    yield _mm_cost(tm, Q_WIDTH, Q_WIDTH) + _mm_cost(tm, KV_WIDTH, KV_WIDTH)
    lb = _forget_lower_bound(lbl_ref[...])

    def hgrn_segment(s):
        z = yield from _dot_by_cols(hb, w_ref, (0, D_MODEL), (s * HG_WIDTH, (s + 1) * HG_WIDTH))
        if s == 0:
            z = z * HG_DK ** -0.5
        elif s in (1, 2):
            lbv = lb[s - 1:s, :]
            z = jnp.log(lbv + (1.0 - lbv) * jax.nn.sigmoid(z))
        elif s == 4:
            z = z * jax.nn.sigmoid(z)
        zh_ref[:, s * HG_WIDTH:(s + 1) * HG_WIDTH] = z

    yield from hgrn_segment(1)
    yield from hgrn_segment(2)
    g_ref[...] = jax.nn.sigmoid(gl).astype(BF16)
    qn = _head_rms(aq, ssq, jnp.tile(qw_ref[...], (1, N_HEADS)))
    kn = _head_rms(ak, ssk, jnp.tile(kw_ref[...], (1, N_KV)))
    seq = vt_ref.shape[2]
    avt = av.T
    for s in range(vt_ref.shape[0]):
        vt_ref[s] = avt[:, s * seq:(s + 1) * seq].astype(vt_ref.dtype)
    for s in (4, 0, 3):
        yield from hgrn_segment(s)
    if latent:
        cos, sup, sdn = cos_ref[...], sup_ref[...], sdn_ref[...]
        qn = _rope(qn, cos, sup, sdn)
        kn = _rope(kn, cos[:, 0:KV_WIDTH], sup[:, 0:KV_WIDTH], sdn[:, 0:KV_WIDTH])
    else:
        knt = kn.T
        for s in range(kt_ref.shape[0]):
            kt_ref[s] = knt[:, s * seq:(s + 1) * seq]
    q_ref[...] = (qn * (HEAD_DIM ** -0.5 * LOG2_E)).astype(BF16)
    k_ref[...] = kn.astype(BF16)


def _in_part(tm, x2, mod, seq_len, norm_w, w_in, qw, kw, ones_q, ones_k, lb_logits, rope):
    t = x2.shape[0]
    latent = rope is not None
    per_seq = max(seq_len // tm, 1)
    row = lambda i: (i, 0)
    in_specs = [pl.BlockSpec((tm, D_MODEL), row),
                _mod_spec(mod, tm, seq_len),
                _const_spec((1, D_MODEL)),
                _const_spec((D_MODEL, D_IN)),
                _const_spec((1, HEAD_DIM)),
                _const_spec((1, HEAD_DIM)),
                _const_spec((Q_WIDTH, Q_WIDTH)),
                _const_spec((KV_WIDTH, KV_WIDTH)),
                _const_spec(lb_logits.shape)]
    args = [x2, mod.rows, norm_w, w_in, qw, kw, ones_q, ones_k, lb_logits]
    out_specs = [pl.BlockSpec((tm, ZH_WIDTH), row), pl.BlockSpec((tm, Q_WIDTH), row),
                 pl.BlockSpec((tm, KV_WIDTH), row)]
    out_shape = [jax.ShapeDtypeStruct((t, ZH_WIDTH), F32), jax.ShapeDtypeStruct((t, Q_WIDTH), BF16),
                 jax.ShapeDtypeStruct((t, KV_WIDTH), BF16)]
    if seq_len <= tm:
        assert tm % seq_len == 0
        t_spec = pl.BlockSpec((tm // seq_len, KV_WIDTH, seq_len), lambda i: (i, 0, 0))
    else:
        t_spec = pl.BlockSpec((1, KV_WIDTH, tm), lambda i: (i // per_seq, 0, i % per_seq))
    if latent:
        in_specs += [pl.BlockSpec((tm, Q_WIDTH), lambda i: (i % per_seq, 0))] * 3
        args += list(rope)
        out_specs.append(t_spec)
        out_shape.append(jax.ShapeDtypeStruct((t // seq_len, KV_WIDTH, seq_len), BF16))
    else:
        out_specs += [t_spec, t_spec]
        out_shape += [jax.ShapeDtypeStruct((t // seq_len, KV_WIDTH, seq_len), F32)] * 2
    out_specs.append(pl.BlockSpec((tm, 2 * D_MODEL), row))
    out_shape.append(jax.ShapeDtypeStruct((t, 2 * D_MODEL), BF16))
    cost = (_mm_cost(tm, D_MODEL, D_IN) + _mm_cost(tm, Q_WIDTH, Q_WIDTH) + _mm_cost(tm, KV_WIDTH, KV_WIDTH))
    scratch = [] if w_in.dtype == BF16 else [pltpu.VMEM((D_MODEL, D_IN), BF16)]
    return t // tm, _Part(functools.partial(_in_stages, latent), args, in_specs, out_specs, out_shape, scratch, cost)


def _attn_cost(lq, lks):
    per_head = sum(_mm_cost(lk, HEAD_DIM, lq) + _mm_cost(HEAD_DIM + ONES_ROWS, lk, lq) for lk in lks)
    return N_HEADS * per_head


def _attn_stages(k_transposed, lookahead, ins, outs, _):
    n_seg = len(k_transposed)
    q_ref, kv_refs = ins[0], ins[1:]
    (o_ref,) = outs
    q = q_ref[...]
    lq = q.shape[0]
    ks = [(kv_refs[2 * s][0].T if k_transposed[s] else kv_refs[2 * s][...]).astype(BF16) for s in range(n_seg)]
    vts = [kv_refs[2 * s + 1][0].astype(BF16) for s in range(n_seg)]
    vts = [[jnp.concatenate([vt[g * HEAD_DIM:(g + 1) * HEAD_DIM, :],
                             jnp.ones((ONES_ROWS, vt.shape[1]), BF16)], axis=0) for g in range(N_KV)]
           for vt in vts]
    score_cost = sum(_mm_cost(k.shape[0], HEAD_DIM, lq) for k in ks)
    value_cost = sum(_mm_cost(HEAD_DIM + ONES_ROWS, k.shape[0], lq) for k in ks)

    def scores(h):
        g = h // (N_HEADS // N_KV)
        qh = q[:, h * HEAD_DIM:(h + 1) * HEAD_DIM]
        return [_dot_nt(k[:, g * HEAD_DIM:(g + 1) * HEAD_DIM], qh) for k in ks]

    pending = []
    for h in range(lookahead):
        pending.append(scores(h))
        yield score_cost
    outs_t = []
    for h in range(N_HEADS):
        if h + lookahead < N_HEADS:
            pending.append(scores(h + lookahead))
            yield score_cost
        st = pending.pop(0)
        g = h // (N_HEADS // N_KV)
        m = functools.reduce(jnp.maximum, [jnp.max(s, axis=0, keepdims=True) for s in st])
        ot = functools.reduce(jnp.add, [_dot(vt[g], jnp.exp2(s - m).astype(BF16)) for vt, s in zip(vts, st)])
        outs_t.append(ot[0:HEAD_DIM, :] / ot[HEAD_DIM:HEAD_DIM + 1, :])
        if h == N_HEADS - 1:
            o_ref[...] = jnp.concatenate(outs_t, axis=0).T.astype(BF16)
        yield value_cost


def _attn_part(q, kv_segments, n_batch, seq_len, lookahead):
    t = q.shape[0]
    tq = ATTN_Q_TILE
    per_seq = seq_len // tq
    in_specs = [pl.BlockSpec((tq, Q_WIDTH), lambda i: (i, 0))]
    args = [q]
    for k, v_t, layer in kv_segments:
        lk = v_t.shape[-1]
        if v_t.ndim == 4:
            slab = pl.BlockSpec((1, None, KV_WIDTH, lk), lambda i, layer=layer: (i // per_seq, layer, 0, 0))
        else:
            slab = pl.BlockSpec((1, KV_WIDTH, lk), lambda i: (i // per_seq, 0, 0))
        k_spec = slab if k.ndim == v_t.ndim else pl.BlockSpec((lk, KV_WIDTH), lambda i: (i // per_seq, 0))
        in_specs += [k_spec, slab]
        args += [k, v_t]
    stages = functools.partial(_attn_stages, tuple(k.ndim == v_t.ndim for k, v_t, _ in kv_segments), lookahead)
    return n_batch * per_seq, _Part(stages, args, in_specs, [pl.BlockSpec((tq, Q_WIDTH), lambda i: (i, 0))],
                                    [jax.ShapeDtypeStruct((t, Q_WIDTH), BF16)], [],
                                    _attn_cost(tq, [v_t.shape[-1] for _, v_t, _ in kv_segments]))


def _hgrn_unit_cost():
    blk = HG_BLOCK
    return (2 * _mm_cost(blk, blk, 2 * HG_DK)
            + 2 * _mm_cost(blk, HG_DK, 2 * blk) + _mm_cost(blk, blk, HG_DV)
            + (blk // HG_PAIR) * (_mm_cost(HG_DV, HG_PAIR, 2 * HG_DK) + _mm_cost(HG_PAIR, 2 * HG_DK, HG_DV)))


def _hgrn_stages(seq_len, hps, has_s0, has_sfin, ins, outs, scrs):
    tf_ref, tb_ref, xf_ref, xb_ref, q_ref, lf_ref, lb_ref, v_ref, g_ref, nw_ref = ins[:10]
    s0_ref = ins[10] if has_s0 else None
    o_ref = outs[0]
    sfin_ref = outs[1] if has_sfin else None
    kv_scr, ss_scr, qd_scr, oi_scr, dec_scr = scrs
    n_blk = seq_len // HG_BLOCK
    n_pair = seq_len // HG_PAIR
    per_blk = HG_BLOCK // HG_CHUNK
    pairs_per_blk = HG_BLOCK // HG_PAIR
    one_row = jnp.ones((1, HG_DK), F32)

    def chunk_cumsum(t_ref, x):
        hi, lo = _split_bf16(x)
        r = _dot(t_ref[...], jnp.concatenate([hi, lo], axis=1))
        return r[:, 0:HG_DK] + r[:, HG_DK:2 * HG_DK]

    def chunk_edge_rows(x, first):
        off = 0 if first else HG_CHUNK - 1
        return [x[c * HG_CHUNK + off:c * HG_CHUNK + off + 1, :] for c in range(per_blk)]

    def spread_rows(rows):
        return jnp.concatenate([jnp.broadcast_to(r, (HG_CHUNK, HG_DK)) for r in rows], axis=0)

    units = [(blk, h) for blk in range(n_blk) for h in range(hps)]
    stage1 = []
    for blk, h in units:
        rows = slice(blk * HG_BLOCK, (blk + 1) * HG_BLOCK)
        cols = slice(h * HG_DK, (h + 1) * HG_DK)
        logf_f = lf_ref[rows, cols]
        logf_b = lb_ref[rows, cols]
        k_f = 1.0 - jnp.exp(logf_f)
        k_b = 1.0 - jnp.exp(logf_b)
        a_f = chunk_cumsum(tf_ref, logf_f)
        a_b = chunk_cumsum(tb_ref, logf_b)
        stage1.append((rows, cols, k_f, k_b, a_f, a_b))
        yield 2 * _mm_cost(HG_BLOCK, HG_BLOCK, 2 * HG_DK)
    stage2 = []
    for (blk, h), (rows, cols, k_f, k_b, a_f, a_b) in zip(units, stage1):
        q = q_ref[rows, cols]
        vb = v_ref[rows, cols].astype(BF16)
        ea_f = jnp.exp(a_f)
        ea_b = jnp.exp(a_b)
        qd_f = q * ea_f
        qd_b = q * ea_b
        kd_f = k_f * jnp.exp(-a_f)
        kd_b = k_b * jnp.exp(-a_b)
        dec_f = chunk_edge_rows(ea_f, first=False)
        dec_b = chunk_edge_rows(ea_b, first=True)
        ke_f = kd_f * spread_rows(dec_f)
        ke_b = kd_b * spread_rows(dec_b)
        even = [c % 2 == 0 for c in range(per_blk)]
        qd_pair = jnp.concatenate(
            [qd_f * spread_rows([one_row if even[c] else dec_f[c - 1] for c in range(per_blk)]),
             qd_b * spread_rows([dec_b[c + 1] if even[c] else one_row for c in range(per_blk)])], axis=1)
        ke_pair = jnp.concatenate(
            [ke_f * spread_rows([dec_f[c + 1] if even[c] else one_row for c in range(per_blk)]),
             ke_b * spread_rows([one_row if even[c] else dec_b[c - 1] for c in range(per_blk)])], axis=1).astype(BF16)
        sc_f = _dot_nt(qd_f.astype(BF16), jnp.concatenate([kd_f, ke_f], axis=0).astype(BF16))
        sc_b = _dot_nt(qd_b.astype(BF16), jnp.concatenate([kd_b, ke_b], axis=0).astype(BF16))
        for p in range(pairs_per_blk):
            pr = slice(p * HG_PAIR, (p + 1) * HG_PAIR)
            kv_scr[h, blk * pairs_per_blk + p] = _dot_tn(vb[pr, :], ke_pair[pr, :])
            dec_scr[h, blk * pairs_per_blk + p, 0:1, :] = jnp.concatenate(
                [dec_f[2 * p] * dec_f[2 * p + 1], dec_b[2 * p] * dec_b[2 * p + 1]], axis=1)
        qd_scr[h, rows, :] = qd_pair.astype(BF16)
        stage2.append((rows, cols, vb, sc_f, sc_b))
        yield 2 * _mm_cost(HG_BLOCK, HG_DK, 2 * HG_BLOCK) + pairs_per_blk * _mm_cost(HG_DV, HG_PAIR, 2 * HG_DK)
    for rows, cols, vb, sc_f, sc_b in stage2:
        s = (sc_f[:, 0:HG_BLOCK].astype(BF16) * tf_ref[...] + sc_f[:, HG_BLOCK:].astype(BF16) * xf_ref[...]
             + sc_b[:, 0:HG_BLOCK].astype(BF16) * tb_ref[...] + sc_b[:, HG_BLOCK:].astype(BF16) * xb_ref[...])
        oi_scr[rows, cols] = _dot(s, vb)
        yield _mm_cost(HG_BLOCK, HG_BLOCK, HG_DV)

    for h in range(hps):
        if has_s0:
            s_f, s_b = s0_ref[0, 0, h].T, s0_ref[0, 1, h].T
        else:
            s_f = s_b = jnp.zeros((HG_DV, HG_DK), F32)
        for pf in range(n_pair):
            pb = n_pair - 1 - pf
            ss_scr[h, pf, :, 0:HG_DK] = s_f.astype(BF16)
            s_f = dec_scr[h, pf, 0:1, 0:HG_DK] * s_f + kv_scr[h, pf, :, 0:HG_DK]
            ss_scr[h, pb, :, HG_DK:2 * HG_DK] = s_b.astype(BF16)
            s_b = dec_scr[h, pb, 0:1, HG_DK:2 * HG_DK] * s_b + kv_scr[h, pb, :, HG_DK:2 * HG_DK]
        if has_sfin:
            sfin_ref[0, 0, h] = s_f.T
            sfin_ref[0, 1, h] = s_b.T

    for blk, h in units:
        cols = slice(h * HG_DK, (h + 1) * HG_DK)
        for p in range(pairs_per_blk):
            pi = blk * pairs_per_blk + p
            rows = slice(pi * HG_PAIR, (pi + 1) * HG_PAIR)
            oi_scr[rows, cols] = oi_scr[rows, cols] + _dot_nt(qd_scr[h, rows, :], ss_scr[h, pi])
        yield pairs_per_blk * _mm_cost(HG_PAIR, 2 * HG_DK, HG_DV)

    for h in range(hps):
        cols = slice(h * HG_DK, (h + 1) * HG_DK)
        o = _rms_rows(oi_scr[:, cols], nw_ref[...])
        o_ref[:, cols] = (o * g_ref[:, cols]).astype(BF16)


def _chunk_masks():
    r = np.arange(HG_BLOCK)
    same = (r[:, None] // HG_CHUNK) == (r[None, :] // HG_CHUNK)
    lower = same & (r[None, :] <= r[:, None])
    upper = same & (r[None, :] >= r[:, None])
    same_pair = (r[:, None] // HG_PAIR) == (r[None, :] // HG_PAIR)
    cross_f = same_pair & (r[:, None] // HG_CHUNK == r[None, :] // HG_CHUNK + 1)
    return tuple(jnp.asarray(m.astype(np.float32), dtype=BF16) for m in (lower, upper, cross_f, cross_f.T))


def _hgrn_part(zh, norm_w, n_batch, seq_len, hps, s0, want_final):
    t = zh.shape[0]
    n_pair = seq_len // HG_PAIR
    width = hps * HG_DK
    n_hb = HG_HEADS // hps
    masks = _chunk_masks()

    def seg_spec(seg):
        return pl.BlockSpec((seq_len, width), lambda i: (i // n_hb, seg * n_hb + i % n_hb))

    state_spec = pl.BlockSpec((1, 2, hps, HG_DK, HG_DV), lambda i: (i // n_hb, 0, i % n_hb, 0, 0))
    mask_spec = pl.BlockSpec((HG_BLOCK, HG_BLOCK), lambda i: (0, 0))
    in_specs = [mask_spec] * len(masks) + [seg_spec(s) for s in range(5)] + [pl.BlockSpec((1, HG_DV), lambda i: (0, 0))]
    args = list(masks) + [zh] * 5 + [norm_w]
    if s0 is not None:
        states, layer = s0
        in_specs.append(pl.BlockSpec((1, None, 2, hps, HG_DK, HG_DV),
                                     lambda i: (i // n_hb, layer, 0, i % n_hb, 0, 0)))
        args.append(states)
    out_specs = [pl.BlockSpec((seq_len, width), lambda i: (i // n_hb, i % n_hb))]
    out_shape = [jax.ShapeDtypeStruct((t, HG_WIDTH), BF16)]
    if want_final:
        out_specs.append(state_spec)
        out_shape.append(jax.ShapeDtypeStruct((n_batch, 2, HG_HEADS, HG_DK, HG_DV), F32))
    scratch = [pltpu.VMEM((hps, n_pair, HG_DV, 2 * HG_DK), F32),
               pltpu.VMEM((hps, n_pair, HG_DV, 2 * HG_DK), BF16),
               pltpu.VMEM((hps, seq_len, 2 * HG_DK), BF16),
               pltpu.VMEM((seq_len, width), F32),
               pltpu.VMEM((hps, n_pair, SUBLANES, 2 * HG_DK), F32)]
    stages = functools.partial(_hgrn_stages, seq_len, hps, s0 is not None, want_final)
    cost = hps * (seq_len // HG_BLOCK) * _hgrn_unit_cost()
    return n_batch * n_hb, _Part(stages, args, in_specs, out_specs, out_shape, scratch, cost)


def _out_stages(ins, outs, _):
    x_ref, mod_ref, oh_ref, oa_ref, g_ref, who_ref, wao_ref, wout_ref, nfw_ref, wff1_ref, wff2_ref, fnw_ref = ins
    (y_ref,) = outs
    tm = x_ref.shape[0]
    m = mod_ref[0]
    g1 = m[:, 2 * D_MODEL:3 * D_MODEL]
    sh2 = m[:, 3 * D_MODEL:4 * D_MODEL]
    sc2 = m[:, 4 * D_MODEL:5 * D_MODEL]
    g2 = m[:, 5 * D_MODEL:6 * D_MODEL]
    gates = g_ref[...].astype(F32)
    from_h = yield from _dot_by_cols(oh_ref[...], who_ref, (0, HG_WIDTH), (0, D_MODEL))
    from_a = yield from _dot_by_cols(oa_ref[...], wao_ref, (0, Q_WIDTH), (0, D_MODEL))
    merged = gates[:, 0:D_MODEL] * from_h + gates[:, D_MODEL:2 * D_MODEL] * from_a
    mixed = yield from _dot_by_cols(merged.astype(BF16), wout_ref, (0, D_MODEL), (0, D_MODEL))
    x1 = x_ref[...] + g1 * mixed
    h2 = (_rms_rows(x1, nfw_ref[...]) * (1.0 + sc2) + sh2).astype(BF16)
    hidden = []
    for j in range(D_FF // D_MODEL):
        hj = yield from _dot_by_cols(h2, wff1_ref, (0, D_MODEL), (j * D_MODEL, (j + 1) * D_MODEL))
        hj = jnp.maximum(hj, 0.0)
        hidden.append((hj * hj).astype(BF16))
    ffn = yield from _dot_by_cols(jnp.concatenate(hidden, axis=1), wff2_ref, (0, D_FF), (0, D_MODEL))
    y_ref[...] = _rms_rows(x1 + g2 * ffn, fnw_ref[...])


def _out_part(tm, x2, mod, seq_len, oh, oa, gates, w_ho, w_ao, w_out, nfw, w_ff1, w_ff2, fnw):
    t = x2.shape[0]
    row = lambda i: (i, 0)
    in_specs = [pl.BlockSpec((tm, D_MODEL), row),
                _mod_spec(mod, tm, seq_len),
                pl.BlockSpec((tm, HG_WIDTH), row),
                pl.BlockSpec((tm, Q_WIDTH), row),
                pl.BlockSpec((tm, 2 * D_MODEL), row),
                _const_spec((HG_WIDTH, D_MODEL)),
                _const_spec((Q_WIDTH, D_MODEL)),
                _const_spec((D_MODEL, D_MODEL)),
                _const_spec((1, D_MODEL)),
                _const_spec((D_MODEL, D_FF)),
                _const_spec((D_FF, D_MODEL)),
                _const_spec((1, D_MODEL))]
    args = [x2, mod.rows, oh, oa, gates, w_ho, w_ao, w_out, nfw, w_ff1, w_ff2, fnw]
    cost = _mm_cost(tm, HG_WIDTH + Q_WIDTH, D_MODEL) + _mm_cost(tm, D_MODEL, D_MODEL) + 2 * _mm_cost(tm, D_MODEL, D_FF)
    return t // tm, _Part(_out_stages, args, in_specs, [pl.BlockSpec((tm, D_MODEL), row)],
                          [jax.ShapeDtypeStruct((t, D_MODEL), F32)], [], cost)


def _cast_stages(ins, outs, _):
    for src, dst in zip(ins, outs):
        dst[...] = src[...].astype(dst.dtype)
    yield 1


def _cast_part(arrays, n_steps):
    in_specs, out_specs, out_shape = [], [], []
    for a in arrays:
        rows, cols = a.shape
        assert rows % (n_steps * MXU_ROWS) == 0
        spec = pl.BlockSpec((rows // n_steps, cols), lambda i: (i, 0))
        in_specs.append(spec)
        out_specs.append(spec)
        out_shape.append(jax.ShapeDtypeStruct(a.shape, BF16))
    return n_steps, _Part(_cast_stages, list(arrays), in_specs, out_specs, out_shape, [], 1)


def _rope_tables(n_tokens):
    rows = n_tokens // GRID_W
    row = np.repeat(np.arange(rows, dtype=np.float32), GRID_W)
    col = np.tile(np.arange(GRID_W, dtype=np.float32), rows)
    axis_dim = HEAD_DIM // 2
    freqs = (ROPE_THETA ** (-np.arange(0, axis_dim, 2, dtype=np.float32) / axis_dim)).astype(np.float32)
    ang_r = row[:, None] * freqs
    ang_c = col[:, None] * freqs
    cr, sr, cc, sc = np.cos(ang_r), np.sin(ang_r), np.cos(ang_c), np.sin(ang_c)
    zero = np.zeros_like(sr)
    cos = np.concatenate([cr, cr, cc, cc], axis=-1)
    s_up = np.concatenate([-sr, zero, -sc, zero], axis=-1)
    s_dn = np.concatenate([zero, sr, zero, sc], axis=-1)
    return tuple(jnp.asarray(np.tile(a, (1, N_HEADS)), dtype=F32) for a in (cos, s_up, s_dn))


def _to_slab(kv):
    n, depth, length = kv.shape[:3]
    return jnp.transpose(kv, (0, 1, 3, 4, 2)).reshape(n, depth, KV_WIDTH, length)


def _from_slab(slab):
    n, _, length = slab.shape
    return jnp.transpose(slab.reshape(n, N_KV, HEAD_DIM, length), (0, 3, 1, 2))


def _block_ones(width):
    idx = np.arange(width) // HEAD_DIM
    return jnp.asarray((idx[:, None] == idx[None, :]).astype(np.float32), dtype=BF16)


def _same_steps(*counted_parts):
    steps = {n for n, _ in counted_parts}
    assert len(steps) == 1, steps
    return steps.pop(), [p for _, p in counted_parts]


def kernel(x_prompt, x_sample, cache_k, cache_v, state_hgrn, c, c_ctx, w_ada, b_ada, norm_mix_w, w_in, q_norm_w, k_norm_w, hgrn_lb_logits, hgrn_norm_w, w_hgrn_out, w_attn_out, w_out, norm_ffn_w, w_ff1, w_ff2, final_norm_w):
    n_p, l_p, _ = x_prompt.shape
    n_s, l_s, _ = x_sample.shape
    layer = 0

    mod = _modulation(c_ctx[None, :], c, w_ada[layer], b_ada[layer][None, :])
    mod_p = _Mod(mod, 0, True)
    mod_s = _Mod(mod, 1, False)

    in_small = (q_norm_w[layer][None, :], k_norm_w[layer][None, :],
                _block_ones(Q_WIDTH), _block_ones(KV_WIDTH), hgrn_lb_logits)
    nmw = norm_mix_w[layer][None, :]
    hnw = hgrn_norm_w[layer][None, :]
    xp2 = x_prompt.reshape(n_p * l_p, D_MODEL)
    xs2 = x_sample.reshape(n_s * l_s, D_MODEL)

    n_in, in_p = _in_part(MIX_TOKEN_TILE, xp2, mod_p, l_p, nmw, w_in[layer], *in_small, None)
    steps, parts = _same_steps((n_in, in_p), _cast_part(
        [w_in[layer], w_hgrn_out[layer], w_attn_out[layer], w_out[layer], w_ff1[layer], w_ff2[layer]], n_in))
    (zh_p, q_p, k_p, kt_p, vt_p, gates_p), (w_in_b, w_ho_b, w_ao_b, w_out_b, w_ff1_b, w_ff2_b) = _launch(
        "in_ctx", steps, parts)
    out_w = (w_ho_b, w_ao_b, w_out_b, norm_ffn_w[layer][None, :], w_ff1_b, w_ff2_b, final_norm_w[None, :])

    steps, parts = _same_steps(
        _attn_part(q_p, [(k_p, vt_p, None)], n_p, l_p, N_HEADS),
        _in_part(MIX_TOKEN_TILE, xs2, mod_s, l_s, nmw, w_in_b, *in_small, _rope_tables(l_s)),
        _hgrn_part(zh_p, hnw, n_p, l_p, HG_HEADS, None, True))
    (oa_p,), (zh_s, q_s, k_s, vt_s, gates_s), (oh_p, s_fin) = _launch("in_latent_mix_ctx", steps, parts,
                                                                      MIX_PACES_IN)

    segs = [(_to_slab(cache_k), _to_slab(cache_v), layer), (k_s, vt_s, None)]
    steps, parts = _same_steps(
        _out_part(MIX_TOKEN_TILE, xp2, mod_p, l_p, oh_p, oa_p, gates_p, *out_w),
        _attn_part(q_s, segs, n_s, l_s, ATTN_LOOKAHEAD),
        _hgrn_part(zh_s, hnw, n_s, l_s, 1, (state_hgrn, layer), False))
    (y_p,), (oa_s,), (oh_s,) = _launch("out_ctx_mix_latent", steps, parts, MIX_PACES_OUT)

    steps, parts = _same_steps(_out_part(SOLO_TOKEN_TILE, xs2, mod_s, l_s, oh_s, oa_s, gates_s, *out_w))
    ((y_s,),) = _launch("out_latent", steps, parts)

    new_k = _from_slab(kt_p)[:, None]
    new_v = _from_slab(vt_p)[:, None]
    new_s = s_fin.reshape(n_p, 1, 2, HG_HEADS, HG_DK, HG_DV)
    return (y_p.reshape(n_p, l_p, D_MODEL), y_s.reshape(n_s, l_s, D_MODEL), new_k, new_v, new_s)
```

```python
import functools
from typing import Any, Callable, NamedTuple

import numpy as np
import jax
import jax.numpy as jnp
from jax import lax
from jax.experimental import pallas as pl
from jax.experimental.pallas import tpu as pltpu

F32 = jnp.float32
BF16 = jnp.bfloat16

D_MODEL = 1024
GRID_W = 64
EPS = 1e-6
HG_HEADS = 4
HG_DK = 128
HG_DV = 128
HG_WIDTH = HG_HEADS * HG_DK
HG_CHUNK = 32
HG_PAIR = 2 * HG_CHUNK
N_HEADS = 8
N_KV = 2
HEAD_DIM = 64
Q_WIDTH = N_HEADS * HEAD_DIM
KV_WIDTH = N_KV * HEAD_DIM
ROPE_THETA = 10000.0
D_FF = 4 * D_MODEL
N_MOD = 6
ZH_WIDTH = 5 * HG_WIDTH
D_IN = ZH_WIDTH + Q_WIDTH + 2 * KV_WIDTH + 2 * D_MODEL

V7X_VMEM_LIMIT_BYTES = 56 * 1024 * 1024
SUBLANES = 8
LANES = 128
MXU_ROWS = 16
MXU_TILE = 256
ROPE_PAIR_LANES = HEAD_DIM // 4
HG_BLOCK = 256
COND_ROWS = 16
ONES_ROWS = 16
LOG2_E = float(np.log2(np.e))
MIX_TOKEN_TILE = 256
SOLO_TOKEN_TILE = 512
ADALN_ROW_TILE = 256
W_CAST_ROWS = 128
DOT_COLS = 512
ATTN_Q_TILE = 256
ATTN_LOOKAHEAD = 2
MIX_PACES_IN = (1.0, 1.0, 0.8)
MIX_PACES_OUT = (1.0, 0.95, 0.9)


def _dot(a, b):
    return jnp.dot(a, b, preferred_element_type=F32)


def _dot_nt(a, b):
    return lax.dot_general(a, b, (((1,), (1,)), ((), ())), preferred_element_type=F32)


def _dot_tn(a, b):
    return lax.dot_general(a, b, (((0,), (0,)), ((), ())), preferred_element_type=F32)


def _mm_cost(m, k, n):
    return (m // MXU_ROWS) * -(-k // MXU_TILE) * -(-n // MXU_TILE)


def _dot_by_cols(a, w_ref, rows, cols):
    pieces = []
    for c in range(cols[0], cols[1], DOT_COLS):
        end = min(c + DOT_COLS, cols[1])
        pieces.append(_dot(a, w_ref[rows[0]:rows[1], c:end]))
        yield _mm_cost(a.shape[0], rows[1] - rows[0], end - c)
    return pieces[0] if len(pieces) == 1 else jnp.concatenate(pieces, axis=1)


def _split_bf16(x):
    hi = x.astype(BF16)
    lo = (x - hi.astype(F32)).astype(BF16)
    return hi, lo


def _const_spec(shape):
    nd = len(shape)
    return pl.BlockSpec(shape, lambda *_: (0,) * nd, pipeline_mode=pl.Buffered(1))


class _Part(NamedTuple):
    stages: Callable[..., Any]
    args: list
    in_specs: list
    out_specs: list
    out_shape: list
    scratch: list
    cost: int


def _interleave(gens, totals):
    done = [0] * len(gens)
    live = list(range(len(gens)))
    while live:
        i = min(live, key=lambda j: done[j] / totals[j])
        try:
            done[i] += next(gens[i])
        except StopIteration:
            live.remove(i)


def _launch(name, n_steps, parts, paces=None):
    paces = paces or [1.0] * len(parts)
    n_in = [len(p.args) for p in parts]
    n_out = [len(p.out_shape) for p in parts]
    n_scr = [len(p.scratch) for p in parts]

    def body(*refs):
        refs = list(refs)
        ins = [[refs.pop(0) for _ in range(n)] for n in n_in]
        outs = [[refs.pop(0) for _ in range(n)] for n in n_out]
        scrs = [[refs.pop(0) for _ in range(n)] for n in n_scr]
        _interleave([p.stages(i, o, s) for p, i, o, s in zip(parts, ins, outs, scrs)],
                    [p.cost * pace for p, pace in zip(parts, paces)])

    flat = pl.pallas_call(
        body,
        grid=(n_steps,),
        in_specs=[s for p in parts for s in p.in_specs],
        out_specs=[s for p in parts for s in p.out_specs],
        out_shape=[s for p in parts for s in p.out_shape],
        scratch_shapes=[s for p in parts for s in p.scratch],
        compiler_params=pltpu.CompilerParams(dimension_semantics=("arbitrary",),
                                             vmem_limit_bytes=V7X_VMEM_LIMIT_BYTES),
        name=name,
    )(*[a for p in parts for a in p.args])
    flat = list(flat)
    return [[flat.pop(0) for _ in range(n)] for n in n_out]


class _Mod(NamedTuple):
    rows: Any
    first: int
    shared: bool


def _mod_spec(mod, tile, seq_len, tile_of_step=lambda i: i):
    if mod.shared:
        return pl.BlockSpec((1, 1, N_MOD * D_MODEL), lambda i: (mod.first, 0, 0))
    assert seq_len % tile == 0
    per_seq = seq_len // tile
    return pl.BlockSpec((1, 1, N_MOD * D_MODEL), lambda i: (mod.first + tile_of_step(i) // per_seq, 0, 0))


def _mod_kernel(cctx_ref, c_ref, w_ref, b_ref, o_ref, cond_scr):
    n_req = c_ref.shape[0]
    cond_scr[...] = jnp.zeros_like(cond_scr)
    cond_scr[0:1, :] = cctx_ref[...]
    cond_scr[1:1 + n_req, :] = c_ref[...]
    c = cond_scr[...]
    x = c * jax.nn.sigmoid(c)
    x_hi, x_lo = _split_bf16(x)
    w = w_ref[...].astype(BF16)
    part = _dot(x_hi, w) + _dot(x_lo, w)

    @pl.when(pl.program_id(0) == 0)
    def _():
        o_ref[:, 0, :] = b_ref[...] + part

    @pl.when(pl.program_id(0) > 0)
    def _():
        o_ref[:, 0, :] += part


def _modulation(c_ctx, c, w_ada, b_ada):
    n = N_MOD * D_MODEL
    tk = ADALN_ROW_TILE
    n_req = c.shape[0]
    assert 1 + n_req <= COND_ROWS
    return pl.pallas_call(
        _mod_kernel,
        grid=(D_MODEL // tk,),
        in_specs=[pl.BlockSpec((1, tk), lambda j: (0, j)),
                  pl.BlockSpec((n_req, tk), lambda j: (0, j)),
                  pl.BlockSpec((tk, n), lambda j: (j, 0)),
                  pl.BlockSpec((1, n), lambda j: (0, 0))],
        out_specs=pl.BlockSpec((COND_ROWS, 1, n), lambda j: (0, 0, 0)),
        out_shape=jax.ShapeDtypeStruct((COND_ROWS, 1, n), F32),
        scratch_shapes=[pltpu.VMEM((COND_ROWS, tk), F32)],
        compiler_params=pltpu.CompilerParams(dimension_semantics=("arbitrary",),
                                             vmem_limit_bytes=V7X_VMEM_LIMIT_BYTES),
        name="adaln_mod",
    )(c_ctx, c, w_ada, b_ada)


def _rms_rows(x, w):
    return x * lax.rsqrt(jnp.mean(x * x, axis=-1, keepdims=True) + EPS) * w


def _head_sumsq(a, ones_ref):
    return _dot((a * a).astype(BF16), ones_ref[...])


def _head_rms(a, sumsq, w):
    return a * lax.rsqrt(sumsq * (1.0 / HEAD_DIM) + EPS) * w


def _forget_lower_bound(logits):
    e = jnp.exp(logits - jnp.max(logits, axis=1, keepdims=True))
    return e[:, 0, :] / jnp.sum(e, axis=1)


def _rope(x, cos, s_up, s_dn):
    cols = []
    for j in range(x.shape[1] // LANES):
        sl = slice(j * LANES, (j + 1) * LANES)
        xj = x[:, sl]
        cols.append(xj * cos[:, sl] + pltpu.roll(xj, LANES - ROPE_PAIR_LANES, 1) * s_up[:, sl]
                    + pltpu.roll(xj, ROPE_PAIR_LANES, 1) * s_dn[:, sl])
    return jnp.concatenate(cols, axis=1) if len(cols) > 1 else cols[0]


def _in_stages(latent, ins, outs, scrs):
    x_ref, mod_ref, nw_ref, w_ref, qw_ref, kw_ref, oq_ref, ok_ref, lbl_ref = ins[:9]
    if scrs:
        (w_bf_ref,) = scrs
        rows = W_CAST_ROWS

        @pl.when(pl.program_id(0) == 0)
        def _():
            def cast_rows(r, carry):
                sl = pl.ds(pl.multiple_of(r * rows, rows), rows)
                w_bf_ref[sl, :] = w_ref[sl, :].astype(BF16)
                return carry

            lax.fori_loop(0, w_ref.shape[0] // rows, cast_rows, 0)

        w_ref = w_bf_ref
    if latent:
        cos_ref, sup_ref, sdn_ref = ins[9:]
        zh_ref, q_ref, k_ref, vt_ref, g_ref = outs
    else:
        zh_ref, q_ref, k_ref, kt_ref, vt_ref, g_ref = outs
    tm = x_ref.shape[0]
    m = mod_ref[0]
    sh1 = m[:, 0:D_MODEL]
    sc1 = m[:, D_MODEL:2 * D_MODEL]
    h = _rms_rows(x_ref[...], nw_ref[...]) * (1.0 + sc1) + sh1
    hb = h.astype(BF16)
    c0 = ZH_WIDTH
    c1 = c0 + Q_WIDTH
    c2 = c1 + KV_WIDTH
    c3 = c2 + KV_WIDTH
    aq = _dot(hb, w_ref[:, c0:c1])
    ak = _dot(hb, w_ref[:, c1:c2])
    av = _dot(hb, w_ref[:, c2:c3])
    yield _mm_cost(tm, D_MODEL, c3 - c0)
    gl = yield from _dot_by_cols(hb, w_ref, (0, D_MODEL), (c3, D_IN))
    ssq = _head_sumsq(aq, oq_ref)
    ssk = _head_sumsq(ak, ok_ref)
    yield _mm_cost(tm, Q_WIDTH, Q_WIDTH) + _mm_cost(tm, KV_WIDTH, KV_WIDTH)
    lb = _forget_lower_bound(lbl_ref[...])

    def hgrn_segment(s):
        z = yield from _dot_by_cols(hb, w_ref, (0, D_MODEL), (s * HG_WIDTH, (s + 1) * HG_WIDTH))
        if s == 0:
            z = z * HG_DK ** -0.5
        elif s in (1, 2):
            lbv = lb[s - 1:s, :]
            z = jnp.log(lbv + (1.0 - lbv) * jax.nn.sigmoid(z))
        elif s == 4:
            z = z * jax.nn.sigmoid(z)
        zh_ref[:, s * HG_WIDTH:(s + 1) * HG_WIDTH] = z

    yield from hgrn_segment(1)
    yield from hgrn_segment(2)
    g_ref[...] = jax.nn.sigmoid(gl).astype(BF16)
    qn = _head_rms(aq, ssq, jnp.tile(qw_ref[...], (1, N_HEADS)))
    kn = _head_rms(ak, ssk, jnp.tile(kw_ref[...], (1, N_KV)))
    seq = vt_ref.shape[2]
    avt = av.T
    for s in range(vt_ref.shape[0]):
        vt_ref[s] = avt[:, s * seq:(s + 1) * seq].astype(vt_ref.dtype)
    for s in (4, 0, 3):
        yield from hgrn_segment(s)
    if latent:
        cos, sup, sdn = cos_ref[...], sup_ref[...], sdn_ref[...]
        qn = _rope(qn, cos, sup, sdn)
        kn = _rope(kn, cos[:, 0:KV_WIDTH], sup[:, 0:KV_WIDTH], sdn[:, 0:KV_WIDTH])
    else:
        knt = kn.T
        for s in range(kt_ref.shape[0]):
            kt_ref[s] = knt[:, s * seq:(s + 1) * seq]
    q_ref[...] = (qn * (HEAD_DIM ** -0.5 * LOG2_E)).astype(BF16)
    k_ref[...] = kn.astype(BF16)


def _in_part(tm, x2, mod, seq_len, norm_w, w_in, qw, kw, ones_q, ones_k, lb_logits, rope):
    t = x2.shape[0]
    latent = rope is not None
    per_seq = max(seq_len // tm, 1)
    row = lambda i: (i, 0)
    in_specs = [pl.BlockSpec((tm, D_MODEL), row),
                _mod_spec(mod, tm, seq_len),
                _const_spec((1, D_MODEL)),
                _const_spec((D_MODEL, D_IN)),
                _const_spec((1, HEAD_DIM)),
                _const_spec((1, HEAD_DIM)),
                _const_spec((Q_WIDTH, Q_WIDTH)),
                _const_spec((KV_WIDTH, KV_WIDTH)),
                _const_spec(lb_logits.shape)]
    args = [x2, mod.rows, norm_w, w_in, qw, kw, ones_q, ones_k, lb_logits]
    out_specs = [pl.BlockSpec((tm, ZH_WIDTH), row), pl.BlockSpec((tm, Q_WIDTH), row),
                 pl.BlockSpec((tm, KV_WIDTH), row)]
    out_shape = [jax.ShapeDtypeStruct((t, ZH_WIDTH), F32), jax.ShapeDtypeStruct((t, Q_WIDTH), BF16),
                 jax.ShapeDtypeStruct((t, KV_WIDTH), BF16)]
    if seq_len <= tm:
        assert tm % seq_len == 0
        t_spec = pl.BlockSpec((tm // seq_len, KV_WIDTH, seq_len), lambda i: (i, 0, 0))
    else:
        t_spec = pl.BlockSpec((1, KV_WIDTH, tm), lambda i: (i // per_seq, 0, i % per_seq))
    if latent:
        in_specs += [pl.BlockSpec((tm, Q_WIDTH), lambda i: (i % per_seq, 0))] * 3
        args += list(rope)
        out_specs.append(t_spec)
        out_shape.append(jax.ShapeDtypeStruct((t // seq_len, KV_WIDTH, seq_len), BF16))
    else:
        out_specs += [t_spec, t_spec]
        out_shape += [jax.ShapeDtypeStruct((t // seq_len, KV_WIDTH, seq_len), F32)] * 2
    out_specs.append(pl.BlockSpec((tm, 2 * D_MODEL), row))
    out_shape.append(jax.ShapeDtypeStruct((t, 2 * D_MODEL), BF16))
    cost = (_mm_cost(tm, D_MODEL, D_IN) + _mm_cost(tm, Q_WIDTH, Q_WIDTH) + _mm_cost(tm, KV_WIDTH, KV_WIDTH))
    scratch = [] if w_in.dtype == BF16 else [pltpu.VMEM((D_MODEL, D_IN), BF16)]
    return t // tm, _Part(functools.partial(_in_stages, latent), args, in_specs, out_specs, out_shape, scratch, cost)


def _attn_cost(lq, lks):
    per_head = sum(_mm_cost(lk, HEAD_DIM, lq) + _mm_cost(HEAD_DIM + ONES_ROWS, lk, lq) for lk in lks)
    return N_HEADS * per_head


def _attn_stages(k_transposed, lookahead, ins, outs, _):
    n_seg = len(k_transposed)
    q_ref, kv_refs = ins[0], ins[1:]
    (o_ref,) = outs
    q = q_ref[...]
    lq = q.shape[0]
    ks = [(kv_refs[2 * s][0].T if k_transposed[s] else kv_refs[2 * s][...]).astype(BF16) for s in range(n_seg)]
    vts = [kv_refs[2 * s + 1][0].astype(BF16) for s in range(n_seg)]
    vts = [[jnp.concatenate([vt[g * HEAD_DIM:(g + 1) * HEAD_DIM, :],
                             jnp.ones((ONES_ROWS, vt.shape[1]), BF16)], axis=0) for g in range(N_KV)]
           for vt in vts]
    score_cost = sum(_mm_cost(k.shape[0], HEAD_DIM, lq) for k in ks)
    value_cost = sum(_mm_cost(HEAD_DIM + ONES_ROWS, k.shape[0], lq) for k in ks)

    def scores(h):
        g = h // (N_HEADS // N_KV)
        qh = q[:, h * HEAD_DIM:(h + 1) * HEAD_DIM]
        return [_dot_nt(k[:, g * HEAD_DIM:(g + 1) * HEAD_DIM], qh) for k in ks]

    pending = []
    for h in range(lookahead):
        pending.append(scores(h))
        yield score_cost
    outs_t = []
    for h in range(N_HEADS):
        if h + lookahead < N_HEADS:
            pending.append(scores(h + lookahead))
            yield score_cost
        st = pending.pop(0)
        g = h // (N_HEADS // N_KV)
        m = functools.reduce(jnp.maximum, [jnp.max(s, axis=0, keepdims=True) for s in st])
        ot = functools.reduce(jnp.add, [_dot(vt[g], jnp.exp2(s - m).astype(BF16)) for vt, s in zip(vts, st)])
        outs_t.append(ot[0:HEAD_DIM, :] / ot[HEAD_DIM:HEAD_DIM + 1, :])
        if h == N_HEADS - 1:
            o_ref[...] = jnp.concatenate(outs_t, axis=0).T.astype(BF16)
        yield value_cost


def _attn_part(q, kv_segments, n_batch, seq_len, lookahead):
    t = q.shape[0]
    tq = ATTN_Q_TILE
    per_seq = seq_len // tq
    in_specs = [pl.BlockSpec((tq, Q_WIDTH), lambda i: (i, 0))]
    args = [q]
    for k, v_t, layer in kv_segments:
        lk = v_t.shape[-1]
        if v_t.ndim == 4:
            slab = pl.BlockSpec((1, None, KV_WIDTH, lk), lambda i, layer=layer: (i // per_seq, layer, 0, 0))
        else:
            slab = pl.BlockSpec((1, KV_WIDTH, lk), lambda i: (i // per_seq, 0, 0))
        k_spec = slab if k.ndim == v_t.ndim else pl.BlockSpec((lk, KV_WIDTH), lambda i: (i // per_seq, 0))
        in_specs += [k_spec, slab]
        args += [k, v_t]
    stages = functools.partial(_attn_stages, tuple(k.ndim == v_t.ndim for k, v_t, _ in kv_segments), lookahead)
    return n_batch * per_seq, _Part(stages, args, in_specs, [pl.BlockSpec((tq, Q_WIDTH), lambda i: (i, 0))],
                                    [jax.ShapeDtypeStruct((t, Q_WIDTH), BF16)], [],
                                    _attn_cost(tq, [v_t.shape[-1] for _, v_t, _ in kv_segments]))


def _hgrn_unit_cost():
    blk = HG_BLOCK
    return (2 * _mm_cost(blk, blk, 2 * HG_DK)
            + 2 * _mm_cost(blk, HG_DK, 2 * blk) + _mm_cost(blk, blk, HG_DV)
            + (blk // HG_PAIR) * (_mm_cost(HG_DV, HG_PAIR, 2 * HG_DK) + _mm_cost(HG_PAIR, 2 * HG_DK, HG_DV)))


def _hgrn_stages(seq_len, hps, has_s0, has_sfin, ins, outs, scrs):
    tf_ref, tb_ref, xf_ref, xb_ref, q_ref, lf_ref, lb_ref, v_ref, g_ref, nw_ref = ins[:10]
    s0_ref = ins[10] if has_s0 else None
    o_ref = outs[0]
    sfin_ref = outs[1] if has_sfin else None
    kv_scr, ss_scr, qd_scr, oi_scr, dec_scr = scrs
    n_blk = seq_len // HG_BLOCK
    n_pair = seq_len // HG_PAIR
    per_blk = HG_BLOCK // HG_CHUNK
    pairs_per_blk = HG_BLOCK // HG_PAIR
    one_row = jnp.ones((1, HG_DK), F32)

    def chunk_cumsum(t_ref, x):
        hi, lo = _split_bf16(x)
        r = _dot(t_ref[...], jnp.concatenate([hi, lo], axis=1))
        return r[:, 0:HG_DK] + r[:, HG_DK:2 * HG_DK]

    def chunk_edge_rows(x, first):
        off = 0 if first else HG_CHUNK - 1
        return [x[c * HG_CHUNK + off:c * HG_CHUNK + off + 1, :] for c in range(per_blk)]

    def spread_rows(rows):
        return jnp.concatenate([jnp.broadcast_to(r, (HG_CHUNK, HG_DK)) for r in rows], axis=0)

    units = [(blk, h) for blk in range(n_blk) for h in range(hps)]
    stage1 = []
    for blk, h in units:
        rows = slice(blk * HG_BLOCK, (blk + 1) * HG_BLOCK)
        cols = slice(h * HG_DK, (h + 1) * HG_DK)
        logf_f = lf_ref[rows, cols]
        logf_b = lb_ref[rows, cols]
        k_f = 1.0 - jnp.exp(logf_f)
        k_b = 1.0 - jnp.exp(logf_b)
        a_f = chunk_cumsum(tf_ref, logf_f)
        a_b = chunk_cumsum(tb_ref, logf_b)
        stage1.append((rows, cols, k_f, k_b, a_f, a_b))
        yield 2 * _mm_cost(HG_BLOCK, HG_BLOCK, 2 * HG_DK)
    stage2 = []
    for (blk, h), (rows, cols, k_f, k_b, a_f, a_b) in zip(units, stage1):
        q = q_ref[rows, cols]
        vb = v_ref[rows, cols].astype(BF16)
        ea_f = jnp.exp(a_f)
        ea_b = jnp.exp(a_b)
        qd_f = q * ea_f
        qd_b = q * ea_b
        kd_f = k_f * jnp.exp(-a_f)
        kd_b = k_b * jnp.exp(-a_b)
        dec_f = chunk_edge_rows(ea_f, first=False)
        dec_b = chunk_edge_rows(ea_b, first=True)
        ke_f = kd_f * spread_rows(dec_f)
        ke_b = kd_b * spread_rows(dec_b)
        even = [c % 2 == 0 for c in range(per_blk)]
        qd_pair = jnp.concatenate(
            [qd_f * spread_rows([one_row if even[c] else dec_f[c - 1] for c in range(per_blk)]),
             qd_b * spread_rows([dec_b[c + 1] if even[c] else one_row for c in range(per_blk)])], axis=1)
        ke_pair = jnp.concatenate(
            [ke_f * spread_rows([dec_f[c + 1] if even[c] else one_row for c in range(per_blk)]),
             ke_b * spread_rows([one_row if even[c] else dec_b[c - 1] for c in range(per_blk)])], axis=1).astype(BF16)
        sc_f = _dot_nt(qd_f.astype(BF16), jnp.concatenate([kd_f, ke_f], axis=0).astype(BF16))
        sc_b = _dot_nt(qd_b.astype(BF16), jnp.concatenate([kd_b, ke_b], axis=0).astype(BF16))
        for p in range(pairs_per_blk):
            pr = slice(p * HG_PAIR, (p + 1) * HG_PAIR)
            kv_scr[h, blk * pairs_per_blk + p] = _dot_tn(vb[pr, :], ke_pair[pr, :])
            dec_scr[h, blk * pairs_per_blk + p, 0:1, :] = jnp.concatenate(
                [dec_f[2 * p] * dec_f[2 * p + 1], dec_b[2 * p] * dec_b[2 * p + 1]], axis=1)
        qd_scr[h, rows, :] = qd_pair.astype(BF16)
        stage2.append((rows, cols, vb, sc_f, sc_b))
        yield 2 * _mm_cost(HG_BLOCK, HG_DK, 2 * HG_BLOCK) + pairs_per_blk * _mm_cost(HG_DV, HG_PAIR, 2 * HG_DK)
    for rows, cols, vb, sc_f, sc_b in stage2:
        s = (sc_f[:, 0:HG_BLOCK].astype(BF16) * tf_ref[...] + sc_f[:, HG_BLOCK:].astype(BF16) * xf_ref[...]
             + sc_b[:, 0:HG_BLOCK].astype(BF16) * tb_ref[...] + sc_b[:, HG_BLOCK:].astype(BF16) * xb_ref[...])
        oi_scr[rows, cols] = _dot(s, vb)
        yield _mm_cost(HG_BLOCK, HG_BLOCK, HG_DV)

    for h in range(hps):
        if has_s0:
            s_f, s_b = s0_ref[0, 0, h].T, s0_ref[0, 1, h].T
        else:
            s_f = s_b = jnp.zeros((HG_DV, HG_DK), F32)
        for pf in range(n_pair):
            pb = n_pair - 1 - pf
            ss_scr[h, pf, :, 0:HG_DK] = s_f.astype(BF16)
            s_f = dec_scr[h, pf, 0:1, 0:HG_DK] * s_f + kv_scr[h, pf, :, 0:HG_DK]
            ss_scr[h, pb, :, HG_DK:2 * HG_DK] = s_b.astype(BF16)
            s_b = dec_scr[h, pb, 0:1, HG_DK:2 * HG_DK] * s_b + kv_scr[h, pb, :, HG_DK:2 * HG_DK]
        if has_sfin:
            sfin_ref[0, 0, h] = s_f.T
            sfin_ref[0, 1, h] = s_b.T

    for blk, h in units:
        cols = slice(h * HG_DK, (h + 1) * HG_DK)
        for p in range(pairs_per_blk):
            pi = blk * pairs_per_blk + p
            rows = slice(pi * HG_PAIR, (pi + 1) * HG_PAIR)
            oi_scr[rows, cols] = oi_scr[rows, cols] + _dot_nt(qd_scr[h, rows, :], ss_scr[h, pi])
        yield pairs_per_blk * _mm_cost(HG_PAIR, 2 * HG_DK, HG_DV)

    for h in range(hps):
        cols = slice(h * HG_DK, (h + 1) * HG_DK)
        o = _rms_rows(oi_scr[:, cols], nw_ref[...])
        o_ref[:, cols] = (o * g_ref[:, cols]).astype(BF16)


def _chunk_masks():
    r = np.arange(HG_BLOCK)
    same = (r[:, None] // HG_CHUNK) == (r[None, :] // HG_CHUNK)
    lower = same & (r[None, :] <= r[:, None])
    upper = same & (r[None, :] >= r[:, None])
    same_pair = (r[:, None] // HG_PAIR) == (r[None, :] // HG_PAIR)
    cross_f = same_pair & (r[:, None] // HG_CHUNK == r[None, :] // HG_CHUNK + 1)
    return tuple(jnp.asarray(m.astype(np.float32), dtype=BF16) for m in (lower, upper, cross_f, cross_f.T))


def _hgrn_part(zh, norm_w, n_batch, seq_len, hps, s0, want_final):
    t = zh.shape[0]
    n_pair = seq_len // HG_PAIR
    width = hps * HG_DK
    n_hb = HG_HEADS // hps
    masks = _chunk_masks()

    def seg_spec(seg):
        return pl.BlockSpec((seq_len, width), lambda i: (i // n_hb, seg * n_hb + i % n_hb))

    state_spec = pl.BlockSpec((1, 2, hps, HG_DK, HG_DV), lambda i: (i // n_hb, 0, i % n_hb, 0, 0))
    mask_spec = pl.BlockSpec((HG_BLOCK, HG_BLOCK), lambda i: (0, 0))
    in_specs = [mask_spec] * len(masks) + [seg_spec(s) for s in range(5)] + [pl.BlockSpec((1, HG_DV), lambda i: (0, 0))]
    args = list(masks) + [zh] * 5 + [norm_w]
    if s0 is not None:
        states, layer = s0
        in_specs.append(pl.BlockSpec((1, None, 2, hps, HG_DK, HG_DV),
                                     lambda i: (i // n_hb, layer, 0, i % n_hb, 0, 0)))
        args.append(states)
    out_specs = [pl.BlockSpec((seq_len, width), lambda i: (i // n_hb, i % n_hb))]
    out_shape = [jax.ShapeDtypeStruct((t, HG_WIDTH), BF16)]
    if want_final:
        out_specs.append(state_spec)
        out_shape.append(jax.ShapeDtypeStruct((n_batch, 2, HG_HEADS, HG_DK, HG_DV), F32))
    scratch = [pltpu.VMEM((hps, n_pair, HG_DV, 2 * HG_DK), F32),
               pltpu.VMEM((hps, n_pair, HG_DV, 2 * HG_DK), BF16),
               pltpu.VMEM((hps, seq_len, 2 * HG_DK), BF16),
               pltpu.VMEM((seq_len, width), F32),
               pltpu.VMEM((hps, n_pair, SUBLANES, 2 * HG_DK), F32)]
    stages = functools.partial(_hgrn_stages, seq_len, hps, s0 is not None, want_final)
    cost = hps * (seq_len // HG_BLOCK) * _hgrn_unit_cost()
    return n_batch * n_hb, _Part(stages, args, in_specs, out_specs, out_shape, scratch, cost)


def _out_stages(ins, outs, _):
    x_ref, mod_ref, oh_ref, oa_ref, g_ref, who_ref, wao_ref, wout_ref, nfw_ref, wff1_ref, wff2_ref, fnw_ref = ins
    (y_ref,) = outs
    tm = x_ref.shape[0]
    m = mod_ref[0]
    g1 = m[:, 2 * D_MODEL:3 * D_MODEL]
    sh2 = m[:, 3 * D_MODEL:4 * D_MODEL]
    sc2 = m[:, 4 * D_MODEL:5 * D_MODEL]
    g2 = m[:, 5 * D_MODEL:6 * D_MODEL]
    gates = g_ref[...].astype(F32)
    from_h = yield from _dot_by_cols(oh_ref[...], who_ref, (0, HG_WIDTH), (0, D_MODEL))
    from_a = yield from _dot_by_cols(oa_ref[...], wao_ref, (0, Q_WIDTH), (0, D_MODEL))
    merged = gates[:, 0:D_MODEL] * from_h + gates[:, D_MODEL:2 * D_MODEL] * from_a
    mixed = yield from _dot_by_cols(merged.astype(BF16), wout_ref, (0, D_MODEL), (0, D_MODEL))
    x1 = x_ref[...] + g1 * mixed
    h2 = (_rms_rows(x1, nfw_ref[...]) * (1.0 + sc2) + sh2).astype(BF16)
    acc = jnp.zeros_like(x1)
    for j in range(D_FF // D_MODEL):
        cols = slice(j * D_MODEL, (j + 1) * D_MODEL)
        hj = yield from _dot_by_cols(h2, wff1_ref, (0, D_MODEL), (cols.start, cols.stop))
        hj = jnp.maximum(hj, 0.0)
        acc = acc + (yield from _dot_by_cols((hj * hj).astype(BF16), wff2_ref, (cols.start, cols.stop), (0, D_MODEL)))
    y_ref[...] = _rms_rows(x1 + g2 * acc, fnw_ref[...])


def _out_part(tm, x2, mod, seq_len, oh, oa, gates, w_ho, w_ao, w_out, nfw, w_ff1, w_ff2, fnw):
    t = x2.shape[0]
    row = lambda i: (i, 0)
    in_specs = [pl.BlockSpec((tm, D_MODEL), row),
                _mod_spec(mod, tm, seq_len),
                pl.BlockSpec((tm, HG_WIDTH), row),
                pl.BlockSpec((tm, Q_WIDTH), row),
                pl.BlockSpec((tm, 2 * D_MODEL), row),
                _const_spec((HG_WIDTH, D_MODEL)),
                _const_spec((Q_WIDTH, D_MODEL)),
                _const_spec((D_MODEL, D_MODEL)),
                _const_spec((1, D_MODEL)),
                _const_spec((D_MODEL, D_FF)),
                _const_spec((D_FF, D_MODEL)),
                _const_spec((1, D_MODEL))]
    args = [x2, mod.rows, oh, oa, gates, w_ho, w_ao, w_out, nfw, w_ff1, w_ff2, fnw]
    cost = _mm_cost(tm, HG_WIDTH + Q_WIDTH, D_MODEL) + _mm_cost(tm, D_MODEL, D_MODEL) + 2 * _mm_cost(tm, D_MODEL, D_FF)
    return t // tm, _Part(_out_stages, args, in_specs, [pl.BlockSpec((tm, D_MODEL), row)],
                          [jax.ShapeDtypeStruct((t, D_MODEL), F32)], [], cost)


def _cast_stages(ins, outs, _):
    for src, dst in zip(ins, outs):
        dst[...] = src[...].astype(dst.dtype)
    yield 1


def _cast_part(arrays, n_steps):
    in_specs, out_specs, out_shape = [], [], []
    for a in arrays:
        rows, cols = a.shape
        assert rows % (n_steps * MXU_ROWS) == 0
        spec = pl.BlockSpec((rows // n_steps, cols), lambda i: (i, 0))
        in_specs.append(spec)
        out_specs.append(spec)
        out_shape.append(jax.ShapeDtypeStruct(a.shape, BF16))
    return n_steps, _Part(_cast_stages, list(arrays), in_specs, out_specs, out_shape, [], 1)


def _rope_tables(n_tokens):
    rows = n_tokens // GRID_W
    row = np.repeat(np.arange(rows, dtype=np.float32), GRID_W)
    col = np.tile(np.arange(GRID_W, dtype=np.float32), rows)
    axis_dim = HEAD_DIM // 2
    freqs = (ROPE_THETA ** (-np.arange(0, axis_dim, 2, dtype=np.float32) / axis_dim)).astype(np.float32)
    ang_r = row[:, None] * freqs
    ang_c = col[:, None] * freqs
    cr, sr, cc, sc = np.cos(ang_r), np.sin(ang_r), np.cos(ang_c), np.sin(ang_c)
    zero = np.zeros_like(sr)
    cos = np.concatenate([cr, cr, cc, cc], axis=-1)
    s_up = np.concatenate([-sr, zero, -sc, zero], axis=-1)
    s_dn = np.concatenate([zero, sr, zero, sc], axis=-1)
    return tuple(jnp.asarray(np.tile(a, (1, N_HEADS)), dtype=F32) for a in (cos, s_up, s_dn))


def _to_slab(kv):
    n, depth, length = kv.shape[:3]
    return jnp.transpose(kv, (0, 1, 3, 4, 2)).reshape(n, depth, KV_WIDTH, length)


def _from_slab(slab):
    n, _, length = slab.shape
    return jnp.transpose(slab.reshape(n, N_KV, HEAD_DIM, length), (0, 3, 1, 2))


def _block_ones(width):
    idx = np.arange(width) // HEAD_DIM
    return jnp.asarray((idx[:, None] == idx[None, :]).astype(np.float32), dtype=BF16)


def _same_steps(*counted_parts):
    steps = {n for n, _ in counted_parts}
    assert len(steps) == 1, steps
    return steps.pop(), [p for _, p in counted_parts]


def kernel(x_prompt, x_sample, cache_k, cache_v, state_hgrn, c, c_ctx, w_ada, b_ada, norm_mix_w, w_in, q_norm_w, k_norm_w, hgrn_lb_logits, hgrn_norm_w, w_hgrn_out, w_attn_out, w_out, norm_ffn_w, w_ff1, w_ff2, final_norm_w):
    n_p, l_p, _ = x_prompt.shape
    n_s, l_s, _ = x_sample.shape
    layer = 0

    mod = _modulation(c_ctx[None, :], c, w_ada[layer], b_ada[layer][None, :])
    mod_p = _Mod(mod, 0, True)
    mod_s = _Mod(mod, 1, False)

    in_small = (q_norm_w[layer][None, :], k_norm_w[layer][None, :],
                _block_ones(Q_WIDTH), _block_ones(KV_WIDTH), hgrn_lb_logits)
    nmw = norm_mix_w[layer][None, :]
    hnw = hgrn_norm_w[layer][None, :]
    xp2 = x_prompt.reshape(n_p * l_p, D_MODEL)
    xs2 = x_sample.reshape(n_s * l_s, D_MODEL)

    n_in, in_p = _in_part(MIX_TOKEN_TILE, xp2, mod_p, l_p, nmw, w_in[layer], *in_small, None)
    steps, parts = _same_steps((n_in, in_p), _cast_part(
        [w_in[layer], w_hgrn_out[layer], w_attn_out[layer], w_out[layer], w_ff1[layer], w_ff2[layer]], n_in))
    (zh_p, q_p, k_p, kt_p, vt_p, gates_p), (w_in_b, w_ho_b, w_ao_b, w_out_b, w_ff1_b, w_ff2_b) = _launch(
        "in_ctx", steps, parts)
    out_w = (w_ho_b, w_ao_b, w_out_b, norm_ffn_w[layer][None, :], w_ff1_b, w_ff2_b, final_norm_w[None, :])

    steps, parts = _same_steps(
        _attn_part(q_p, [(k_p, vt_p, None)], n_p, l_p, N_HEADS),
        _in_part(MIX_TOKEN_TILE, xs2, mod_s, l_s, nmw, w_in_b, *in_small, _rope_tables(l_s)),
        _hgrn_part(zh_p, hnw, n_p, l_p, HG_HEADS, None, True))
    (oa_p,), (zh_s, q_s, k_s, vt_s, gates_s), (oh_p, s_fin) = _launch("in_latent_mix_ctx", steps, parts,
                                                                      MIX_PACES_IN)

    segs = [(_to_slab(cache_k), _to_slab(cache_v), layer), (k_s, vt_s, None)]
    steps, parts = _same_steps(
        _out_part(MIX_TOKEN_TILE, xp2, mod_p, l_p, oh_p, oa_p, gates_p, *out_w),
        _attn_part(q_s, segs, n_s, l_s, ATTN_LOOKAHEAD),
        _hgrn_part(zh_s, hnw, n_s, l_s, 1, (state_hgrn, layer), False))
    (y_p,), (oa_s,), (oh_s,) = _launch("out_ctx_mix_latent", steps, parts, MIX_PACES_OUT)

    steps, parts = _same_steps(_out_part(SOLO_TOKEN_TILE, xs2, mod_s, l_s, oh_s, oa_s, gates_s, *out_w))
    ((y_s,),) = _launch("out_latent", steps, parts)

    new_k = _from_slab(kt_p)[:, None]
    new_v = _from_slab(vt_p)[:, None]
    new_s = s_fin.reshape(n_p, 1, 2, HG_HEADS, HG_DK, HG_DV)
    return (y_p.reshape(n_p, l_p, D_MODEL), y_s.reshape(n_s, l_s, D_MODEL), new_k, new_v, new_s)
```

```python
import functools
from typing import Any, Callable, NamedTuple

import numpy as np
import jax
import jax.numpy as jnp
from jax import lax
from jax.experimental import pallas as pl
from jax.experimental.pallas import tpu as pltpu

F32 = jnp.float32
BF16 = jnp.bfloat16

D_MODEL = 1024
GRID_W = 64
EPS = 1e-6
HG_HEADS = 4
HG_DK = 128
HG_DV = 128
HG_WIDTH = HG_HEADS * HG_DK
HG_CHUNK = 32
HG_PAIR = 2 * HG_CHUNK
N_HEADS = 8
N_KV = 2
HEAD_DIM = 64
Q_WIDTH = N_HEADS * HEAD_DIM
KV_WIDTH = N_KV * HEAD_DIM
ROPE_THETA = 10000.0
D_FF = 4 * D_MODEL
N_MOD = 6
ZH_WIDTH = 5 * HG_WIDTH
D_IN = ZH_WIDTH + Q_WIDTH + 2 * KV_WIDTH + 2 * D_MODEL

V7X_VMEM_LIMIT_BYTES = 56 * 1024 * 1024
SUBLANES = 8
LANES = 128
MXU_ROWS = 16
MXU_TILE = 256
ROPE_PAIR_LANES = HEAD_DIM // 4
HG_BLOCK = 256
COND_ROWS = 16
ONES_ROWS = 16
LOG2_E = float(np.log2(np.e))
MIX_TOKEN_TILE = 256
SOLO_TOKEN_TILE = 512
ADALN_ROW_TILE = 256
W_CAST_ROWS = 128
DOT_COLS = 512
ATTN_Q_TILE = 256
ATTN_LOOKAHEAD = 2
MIX_PACES_IN = (1.0, 1.0, 0.8)
MIX_PACES_OUT = (1.0, 1.0, 0.95)


def _dot(a, b):
    return jnp.dot(a, b, preferred_element_type=F32)


def _dot_nt(a, b):
    return lax.dot_general(a, b, (((1,), (1,)), ((), ())), preferred_element_type=F32)


def _dot_tn(a, b):
    return lax.dot_general(a, b, (((0,), (0,)), ((), ())), preferred_element_type=F32)


def _mm_cost(m, k, n):
    return (m // MXU_ROWS) * -(-k // MXU_TILE) * -(-n // MXU_TILE)


def _dot_by_cols(a, w_ref, rows, cols):
    pieces = []
    for c in range(cols[0], cols[1], DOT_COLS):
        end = min(c + DOT_COLS, cols[1])
        pieces.append(_dot(a, w_ref[rows[0]:rows[1], c:end]))
        yield _mm_cost(a.shape[0], rows[1] - rows[0], end - c)
    return pieces[0] if len(pieces) == 1 else jnp.concatenate(pieces, axis=1)


def _split_bf16(x):
    hi = x.astype(BF16)
    lo = (x - hi.astype(F32)).astype(BF16)
    return hi, lo


def _const_spec(shape):
    nd = len(shape)
    return pl.BlockSpec(shape, lambda *_: (0,) * nd, pipeline_mode=pl.Buffered(1))


class _Part(NamedTuple):
    stages: Callable[..., Any]
    args: list
    in_specs: list
    out_specs: list
    out_shape: list
    scratch: list
    cost: int


def _interleave(gens, totals):
    done = [0] * len(gens)
    live = list(range(len(gens)))
    while live:
        i = min(live, key=lambda j: done[j] / totals[j])
        try:
            done[i] += next(gens[i])
        except StopIteration:
            live.remove(i)


def _launch(name, n_steps, parts, paces=None):
    paces = paces or [1.0] * len(parts)
    n_in = [len(p.args) for p in parts]
    n_out = [len(p.out_shape) for p in parts]
    n_scr = [len(p.scratch) for p in parts]

    def body(*refs):
        refs = list(refs)
        ins = [[refs.pop(0) for _ in range(n)] for n in n_in]
        outs = [[refs.pop(0) for _ in range(n)] for n in n_out]
        scrs = [[refs.pop(0) for _ in range(n)] for n in n_scr]
        _interleave([p.stages(i, o, s) for p, i, o, s in zip(parts, ins, outs, scrs)],
                    [p.cost * pace for p, pace in zip(parts, paces)])

    flat = pl.pallas_call(
        body,
        grid=(n_steps,),
        in_specs=[s for p in parts for s in p.in_specs],
        out_specs=[s for p in parts for s in p.out_specs],
        out_shape=[s for p in parts for s in p.out_shape],
        scratch_shapes=[s for p in parts for s in p.scratch],
        compiler_params=pltpu.CompilerParams(dimension_semantics=("arbitrary",),
                                             vmem_limit_bytes=V7X_VMEM_LIMIT_BYTES),
        name=name,
    )(*[a for p in parts for a in p.args])
    flat = list(flat)
    return [[flat.pop(0) for _ in range(n)] for n in n_out]


class _Mod(NamedTuple):
    rows: Any
    first: int
    shared: bool


def _mod_spec(mod, tile, seq_len, tile_of_step=lambda i: i):
    if mod.shared:
        return pl.BlockSpec((1, 1, N_MOD * D_MODEL), lambda i: (mod.first, 0, 0))
    assert seq_len % tile == 0
    per_seq = seq_len // tile
    return pl.BlockSpec((1, 1, N_MOD * D_MODEL), lambda i: (mod.first + tile_of_step(i) // per_seq, 0, 0))


def _mod_kernel(cctx_ref, c_ref, w_ref, b_ref, o_ref, cond_scr):
    n_req = c_ref.shape[0]
    cond_scr[...] = jnp.zeros_like(cond_scr)
    cond_scr[0:1, :] = cctx_ref[...]
    cond_scr[1:1 + n_req, :] = c_ref[...]
    c = cond_scr[...]
    x = c * jax.nn.sigmoid(c)
    x_hi, x_lo = _split_bf16(x)
    w = w_ref[...].astype(BF16)
    part = _dot(x_hi, w) + _dot(x_lo, w)

    @pl.when(pl.program_id(0) == 0)
    def _():
        o_ref[:, 0, :] = b_ref[...] + part

    @pl.when(pl.program_id(0) > 0)
    def _():
        o_ref[:, 0, :] += part


def _modulation(c_ctx, c, w_ada, b_ada):
    n = N_MOD * D_MODEL
    tk = ADALN_ROW_TILE
    n_req = c.shape[0]
    assert 1 + n_req <= COND_ROWS
    return pl.pallas_call(
        _mod_kernel,
        grid=(D_MODEL // tk,),
        in_specs=[pl.BlockSpec((1, tk), lambda j: (0, j)),
                  pl.BlockSpec((n_req, tk), lambda j: (0, j)),
                  pl.BlockSpec((tk, n), lambda j: (j, 0)),
                  pl.BlockSpec((1, n), lambda j: (0, 0))],
        out_specs=pl.BlockSpec((COND_ROWS, 1, n), lambda j: (0, 0, 0)),
        out_shape=jax.ShapeDtypeStruct((COND_ROWS, 1, n), F32),
        scratch_shapes=[pltpu.VMEM((COND_ROWS, tk), F32)],
        compiler_params=pltpu.CompilerParams(dimension_semantics=("arbitrary",),
                                             vmem_limit_bytes=V7X_VMEM_LIMIT_BYTES),
        name="adaln_mod",
    )(c_ctx, c, w_ada, b_ada)


def _rms_rows(x, w):
    return x * lax.rsqrt(jnp.mean(x * x, axis=-1, keepdims=True) + EPS) * w


def _head_sumsq(a, ones_ref):
    return _dot((a * a).astype(BF16), ones_ref[...])


def _head_rms(a, sumsq, w):
    return a * lax.rsqrt(sumsq * (1.0 / HEAD_DIM) + EPS) * w


def _forget_lower_bound(logits):
    e = jnp.exp(logits - jnp.max(logits, axis=1, keepdims=True))
    return e[:, 0, :] / jnp.sum(e, axis=1)


def _rope(x, cos, s_up, s_dn):
    cols = []
    for j in range(x.shape[1] // LANES):
        sl = slice(j * LANES, (j + 1) * LANES)
        xj = x[:, sl]
        cols.append(xj * cos[:, sl] + pltpu.roll(xj, LANES - ROPE_PAIR_LANES, 1) * s_up[:, sl]
                    + pltpu.roll(xj, ROPE_PAIR_LANES, 1) * s_dn[:, sl])
    return jnp.concatenate(cols, axis=1) if len(cols) > 1 else cols[0]


def _in_stages(latent, ins, outs, scrs):
    x_ref, mod_ref, nw_ref, w_ref, qw_ref, kw_ref, oq_ref, ok_ref, lbl_ref = ins[:9]
    if scrs:
        (w_bf_ref,) = scrs
        rows = W_CAST_ROWS

        @pl.when(pl.program_id(0) == 0)
        def _():
            def cast_rows(r, carry):
                sl = pl.ds(pl.multiple_of(r * rows, rows), rows)
                w_bf_ref[sl, :] = w_ref[sl, :].astype(BF16)
                return carry

            lax.fori_loop(0, w_ref.shape[0] // rows, cast_rows, 0)

        w_ref = w_bf_ref
    if latent:
        cos_ref, sup_ref, sdn_ref = ins[9:]
        zh_ref, q_ref, k_ref, vt_ref, g_ref = outs
    else:
        zh_ref, q_ref, k_ref, kt_ref, vt_ref, g_ref = outs
    tm = x_ref.shape[0]
    m = mod_ref[0]
    sh1 = m[:, 0:D_MODEL]
    sc1 = m[:, D_MODEL:2 * D_MODEL]
    h = _rms_rows(x_ref[...], nw_ref[...]) * (1.0 + sc1) + sh1
    hb = h.astype(BF16)
    c0 = ZH_WIDTH
    c1 = c0 + Q_WIDTH
    c2 = c1 + KV_WIDTH
    c3 = c2 + KV_WIDTH
    aq = _dot(hb, w_ref[:, c0:c1])
    ak = _dot(hb, w_ref[:, c1:c2])
    av = _dot(hb, w_ref[:, c2:c3])
    yield _mm_cost(tm, D_MODEL, c3 - c0)
    gl = yield from _dot_by_cols(hb, w_ref, (0, D_MODEL), (c3, D_IN))
    ssq = _head_sumsq(aq, oq_ref)
    ssk = _head_sumsq(ak, ok_ref)
    yield _mm_cost(tm, Q_WIDTH, Q_WIDTH) + _mm_cost(tm, KV_WIDTH, KV_WIDTH)
    lb = _forget_lower_bound(lbl_ref[...])

    def hgrn_segment(s):
        z = yield from _dot_by_cols(hb, w_ref, (0, D_MODEL), (s * HG_WIDTH, (s + 1) * HG_WIDTH))
        if s == 0:
            z = z * HG_DK ** -0.5
        elif s in (1, 2):
            lbv = lb[s - 1:s, :]
            z = jnp.log(lbv + (1.0 - lbv) * jax.nn.sigmoid(z))
        elif s == 4:
            z = z * jax.nn.sigmoid(z)
        zh_ref[:, s * HG_WIDTH:(s + 1) * HG_WIDTH] = z

    yield from hgrn_segment(1)
    yield from hgrn_segment(2)
    g_ref[...] = jax.nn.sigmoid(gl).astype(BF16)
    qn = _head_rms(aq, ssq, jnp.tile(qw_ref[...], (1, N_HEADS)))
    kn = _head_rms(ak, ssk, jnp.tile(kw_ref[...], (1, N_KV)))
    seq = vt_ref.shape[2]
    avt = av.T
    for s in range(vt_ref.shape[0]):
        vt_ref[s] = avt[:, s * seq:(s + 1) * seq].astype(vt_ref.dtype)
    for s in (4, 0, 3):
        yield from hgrn_segment(s)
    if latent:
        cos, sup, sdn = cos_ref[...], sup_ref[...], sdn_ref[...]
        qn = _rope(qn, cos, sup, sdn)
        kn = _rope(kn, cos[:, 0:KV_WIDTH], sup[:, 0:KV_WIDTH], sdn[:, 0:KV_WIDTH])
    else:
        knt = kn.T
        for s in range(kt_ref.shape[0]):
            kt_ref[s] = knt[:, s * seq:(s + 1) * seq]
    q_ref[...] = (qn * (HEAD_DIM ** -0.5 * LOG2_E)).astype(BF16)
    k_ref[...] = kn.astype(BF16)


def _in_part(tm, x2, mod, seq_len, norm_w, w_in, qw, kw, ones_q, ones_k, lb_logits, rope):
    t = x2.shape[0]
    latent = rope is not None
    per_seq = max(seq_len // tm, 1)
    row = lambda i: (i, 0)
    in_specs = [pl.BlockSpec((tm, D_MODEL), row),
                _mod_spec(mod, tm, seq_len),
                _const_spec((1, D_MODEL)),
                _const_spec((D_MODEL, D_IN)),
                _const_spec((1, HEAD_DIM)),
                _const_spec((1, HEAD_DIM)),
                _const_spec((Q_WIDTH, Q_WIDTH)),
                _const_spec((KV_WIDTH, KV_WIDTH)),
                _const_spec(lb_logits.shape)]
    args = [x2, mod.rows, norm_w, w_in, qw, kw, ones_q, ones_k, lb_logits]
    out_specs = [pl.BlockSpec((tm, ZH_WIDTH), row), pl.BlockSpec((tm, Q_WIDTH), row),
                 pl.BlockSpec((tm, KV_WIDTH), row)]
    out_shape = [jax.ShapeDtypeStruct((t, ZH_WIDTH), F32), jax.ShapeDtypeStruct((t, Q_WIDTH), BF16),
                 jax.ShapeDtypeStruct((t, KV_WIDTH), BF16)]
    if seq_len <= tm:
        assert tm % seq_len == 0
        t_spec = pl.BlockSpec((tm // seq_len, KV_WIDTH, seq_len), lambda i: (i, 0, 0))
    else:
        t_spec = pl.BlockSpec((1, KV_WIDTH, tm), lambda i: (i // per_seq, 0, i % per_seq))
    if latent:
        in_specs += [pl.BlockSpec((tm, Q_WIDTH), lambda i: (i % per_seq, 0))] * 3
        args += list(rope)
        out_specs.append(t_spec)
        out_shape.append(jax.ShapeDtypeStruct((t // seq_len, KV_WIDTH, seq_len), BF16))
    else:
        out_specs += [t_spec, t_spec]
        out_shape += [jax.ShapeDtypeStruct((t // seq_len, KV_WIDTH, seq_len), F32)] * 2
    out_specs.append(pl.BlockSpec((tm, 2 * D_MODEL), row))
    out_shape.append(jax.ShapeDtypeStruct((t, 2 * D_MODEL), BF16))
    cost = (_mm_cost(tm, D_MODEL, D_IN) + _mm_cost(tm, Q_WIDTH, Q_WIDTH) + _mm_cost(tm, KV_WIDTH, KV_WIDTH))
    scratch = [] if w_in.dtype == BF16 else [pltpu.VMEM((D_MODEL, D_IN), BF16)]
    return t // tm, _Part(functools.partial(_in_stages, latent), args, in_specs, out_specs, out_shape, scratch, cost)


def _attn_cost(lq, lks):
    per_head = sum(_mm_cost(lk, HEAD_DIM, lq) + _mm_cost(HEAD_DIM + ONES_ROWS, lk, lq) for lk in lks)
    return N_HEADS * per_head


def _attn_stages(k_transposed, lookahead, ins, outs, _):
    n_seg = len(k_transposed)
    q_ref, kv_refs = ins[0], ins[1:]
    (o_ref,) = outs
    q = q_ref[...]
    lq = q.shape[0]
    ks = [(kv_refs[2 * s][0].T if k_transposed[s] else kv_refs[2 * s][...]).astype(BF16) for s in range(n_seg)]
    vts = [kv_refs[2 * s + 1][0].astype(BF16) for s in range(n_seg)]
    vts = [[jnp.concatenate([vt[g * HEAD_DIM:(g + 1) * HEAD_DIM, :],
                             jnp.ones((ONES_ROWS, vt.shape[1]), BF16)], axis=0) for g in range(N_KV)]
           for vt in vts]
    score_cost = sum(_mm_cost(k.shape[0], HEAD_DIM, lq) for k in ks)
    value_cost = sum(_mm_cost(HEAD_DIM + ONES_ROWS, k.shape[0], lq) for k in ks)

    def scores(h):
        g = h // (N_HEADS // N_KV)
        qh = q[:, h * HEAD_DIM:(h + 1) * HEAD_DIM]
        return [_dot_nt(k[:, g * HEAD_DIM:(g + 1) * HEAD_DIM], qh) for k in ks]

    pending = []
    for h in range(lookahead):
        pending.append(scores(h))
        yield score_cost
    outs_t = []
    for h in range(N_HEADS):
        if h + lookahead < N_HEADS:
            pending.append(scores(h + lookahead))
            yield score_cost
        st = pending.pop(0)
        g = h // (N_HEADS // N_KV)
        m = functools.reduce(jnp.maximum, [jnp.max(s, axis=0, keepdims=True) for s in st])
        ot = functools.reduce(jnp.add, [_dot(vt[g], jnp.exp2(s - m).astype(BF16)) for vt, s in zip(vts, st)])
        outs_t.append(ot[0:HEAD_DIM, :] / ot[HEAD_DIM:HEAD_DIM + 1, :])
        if h == N_HEADS - 1:
            o_ref[...] = jnp.concatenate(outs_t, axis=0).T.astype(BF16)
        yield value_cost


def _attn_part(q, kv_segments, n_batch, seq_len, lookahead):
    t = q.shape[0]
    tq = ATTN_Q_TILE
    per_seq = seq_len // tq
    in_specs = [pl.BlockSpec((tq, Q_WIDTH), lambda i: (i, 0))]
    args = [q]
    for k, v_t, layer in kv_segments:
        lk = v_t.shape[-1]
        if v_t.ndim == 4:
            slab = pl.BlockSpec((1, None, KV_WIDTH, lk), lambda i, layer=layer: (i // per_seq, layer, 0, 0))
        else:
            slab = pl.BlockSpec((1, KV_WIDTH, lk), lambda i: (i // per_seq, 0, 0))
        k_spec = slab if k.ndim == v_t.ndim else pl.BlockSpec((lk, KV_WIDTH), lambda i: (i // per_seq, 0))
        in_specs += [k_spec, slab]
        args += [k, v_t]
    stages = functools.partial(_attn_stages, tuple(k.ndim == v_t.ndim for k, v_t, _ in kv_segments), lookahead)
    return n_batch * per_seq, _Part(stages, args, in_specs, [pl.BlockSpec((tq, Q_WIDTH), lambda i: (i, 0))],
                                    [jax.ShapeDtypeStruct((t, Q_WIDTH), BF16)], [],
                                    _attn_cost(tq, [v_t.shape[-1] for _, v_t, _ in kv_segments]))


def _hgrn_unit_cost():
    blk = HG_BLOCK
    return (2 * _mm_cost(blk, blk, 2 * HG_DK)
            + 2 * _mm_cost(blk, HG_DK, 2 * blk) + _mm_cost(blk, blk, HG_DV)
            + (blk // HG_PAIR) * (_mm_cost(HG_DV, HG_PAIR, 2 * HG_DK) + _mm_cost(HG_PAIR, 2 * HG_DK, HG_DV)))


def _hgrn_stages(seq_len, hps, has_s0, has_sfin, ins, outs, scrs):
    tf_ref, tb_ref, xf_ref, xb_ref, q_ref, lf_ref, lb_ref, v_ref, g_ref, nw_ref = ins[:10]
    s0_ref = ins[10] if has_s0 else None
    o_ref = outs[0]
    sfin_ref = outs[1] if has_sfin else None
    kv_scr, ss_scr, qd_scr, oi_scr, dec_scr = scrs
    n_blk = seq_len // HG_BLOCK
    n_pair = seq_len // HG_PAIR
    per_blk = HG_BLOCK // HG_CHUNK
    pairs_per_blk = HG_BLOCK // HG_PAIR
    one_row = jnp.ones((1, HG_DK), F32)

    def chunk_cumsum(t_ref, x):
        hi, lo = _split_bf16(x)
        r = _dot(t_ref[...], jnp.concatenate([hi, lo], axis=1))
        return r[:, 0:HG_DK] + r[:, HG_DK:2 * HG_DK]

    def chunk_edge_rows(x, first):
        off = 0 if first else HG_CHUNK - 1
        return [x[c * HG_CHUNK + off:c * HG_CHUNK + off + 1, :] for c in range(per_blk)]

    def spread_rows(rows):
        return jnp.concatenate([jnp.broadcast_to(r, (HG_CHUNK, HG_DK)) for r in rows], axis=0)

    units = [(blk, h) for blk in range(n_blk) for h in range(hps)]
    stage1 = []
    for blk, h in units:
        rows = slice(blk * HG_BLOCK, (blk + 1) * HG_BLOCK)
        cols = slice(h * HG_DK, (h + 1) * HG_DK)
        logf_f = lf_ref[rows, cols]
        logf_b = lb_ref[rows, cols]
        k_f = 1.0 - jnp.exp(logf_f)
        k_b = 1.0 - jnp.exp(logf_b)
        a_f = chunk_cumsum(tf_ref, logf_f)
        a_b = chunk_cumsum(tb_ref, logf_b)
        stage1.append((rows, cols, k_f, k_b, a_f, a_b))
        yield 2 * _mm_cost(HG_BLOCK, HG_BLOCK, 2 * HG_DK)
    stage2 = []
    for (blk, h), (rows, cols, k_f, k_b, a_f, a_b) in zip(units, stage1):
        q = q_ref[rows, cols]
        vb = v_ref[rows, cols].astype(BF16)
        ea_f = jnp.exp(a_f)
        ea_b = jnp.exp(a_b)
        qd_f = q * ea_f
        qd_b = q * ea_b
        kd_f = k_f * jnp.exp(-a_f)
        kd_b = k_b * jnp.exp(-a_b)
        dec_f = chunk_edge_rows(ea_f, first=False)
        dec_b = chunk_edge_rows(ea_b, first=True)
        ke_f = kd_f * spread_rows(dec_f)
        ke_b = kd_b * spread_rows(dec_b)
        even = [c % 2 == 0 for c in range(per_blk)]
        qd_pair = jnp.concatenate(
            [qd_f * spread_rows([one_row if even[c] else dec_f[c - 1] for c in range(per_blk)]),
             qd_b * spread_rows([dec_b[c + 1] if even[c] else one_row for c in range(per_blk)])], axis=1)
        ke_pair = jnp.concatenate(
            [ke_f * spread_rows([dec_f[c + 1] if even[c] else one_row for c in range(per_blk)]),
             ke_b * spread_rows([one_row if even[c] else dec_b[c - 1] for c in range(per_blk)])], axis=1).astype(BF16)
        sc_f = _dot_nt(qd_f.astype(BF16), jnp.concatenate([kd_f, ke_f], axis=0).astype(BF16))
        sc_b = _dot_nt(qd_b.astype(BF16), jnp.concatenate([kd_b, ke_b], axis=0).astype(BF16))
        for p in range(pairs_per_blk):
            pr = slice(p * HG_PAIR, (p + 1) * HG_PAIR)
            kv_scr[h, blk * pairs_per_blk + p] = _dot_tn(vb[pr, :], ke_pair[pr, :])
            dec_scr[h, blk * pairs_per_blk + p, 0:1, :] = jnp.concatenate(
                [dec_f[2 * p] * dec_f[2 * p + 1], dec_b[2 * p] * dec_b[2 * p + 1]], axis=1)
        qd_scr[h, rows, :] = qd_pair.astype(BF16)
        stage2.append((rows, cols, vb, sc_f, sc_b))
        yield 2 * _mm_cost(HG_BLOCK, HG_DK, 2 * HG_BLOCK) + pairs_per_blk * _mm_cost(HG_DV, HG_PAIR, 2 * HG_DK)
    for rows, cols, vb, sc_f, sc_b in stage2:
        s = (sc_f[:, 0:HG_BLOCK].astype(BF16) * tf_ref[...] + sc_f[:, HG_BLOCK:].astype(BF16) * xf_ref[...]
             + sc_b[:, 0:HG_BLOCK].astype(BF16) * tb_ref[...] + sc_b[:, HG_BLOCK:].astype(BF16) * xb_ref[...])
        oi_scr[rows, cols] = _dot(s, vb)
        yield _mm_cost(HG_BLOCK, HG_BLOCK, HG_DV)

    for h in range(hps):
        if has_s0:
            s_f, s_b = s0_ref[0, 0, h].T, s0_ref[0, 1, h].T
        else:
            s_f = s_b = jnp.zeros((HG_DV, HG_DK), F32)
        for pf in range(n_pair):
            pb = n_pair - 1 - pf
            ss_scr[h, pf, :, 0:HG_DK] = s_f.astype(BF16)
            s_f = dec_scr[h, pf, 0:1, 0:HG_DK] * s_f + kv_scr[h, pf, :, 0:HG_DK]
            ss_scr[h, pb, :, HG_DK:2 * HG_DK] = s_b.astype(BF16)
            s_b = dec_scr[h, pb, 0:1, HG_DK:2 * HG_DK] * s_b + kv_scr[h, pb, :, HG_DK:2 * HG_DK]
        if has_sfin:
            sfin_ref[0, 0, h] = s_f.T
            sfin_ref[0, 1, h] = s_b.T

    for blk, h in units:
        cols = slice(h * HG_DK, (h + 1) * HG_DK)
        for p in range(pairs_per_blk):
            pi = blk * pairs_per_blk + p
            rows = slice(pi * HG_PAIR, (pi + 1) * HG_PAIR)
            oi_scr[rows, cols] = oi_scr[rows, cols] + _dot_nt(qd_scr[h, rows, :], ss_scr[h, pi])
        yield pairs_per_blk * _mm_cost(HG_PAIR, 2 * HG_DK, HG_DV)

    for h in range(hps):
        cols = slice(h * HG_DK, (h + 1) * HG_DK)
        o = _rms_rows(oi_scr[:, cols], nw_ref[...])
        o_ref[:, cols] = (o * g_ref[:, cols]).astype(BF16)


def _chunk_masks():
    r = np.arange(HG_BLOCK)
    same = (r[:, None] // HG_CHUNK) == (r[None, :] // HG_CHUNK)
    lower = same & (r[None, :] <= r[:, None])
    upper = same & (r[None, :] >= r[:, None])
    same_pair = (r[:, None] // HG_PAIR) == (r[None, :] // HG_PAIR)
    cross_f = same_pair & (r[:, None] // HG_CHUNK == r[None, :] // HG_CHUNK + 1)
    return tuple(jnp.asarray(m.astype(np.float32), dtype=BF16) for m in (lower, upper, cross_f, cross_f.T))


def _hgrn_part(zh, norm_w, n_batch, seq_len, hps, s0, want_final):
    t = zh.shape[0]
    n_pair = seq_len // HG_PAIR
    width = hps * HG_DK
    n_hb = HG_HEADS // hps
    masks = _chunk_masks()

    def seg_spec(seg):
        return pl.BlockSpec((seq_len, width), lambda i: (i // n_hb, seg * n_hb + i % n_hb))

    state_spec = pl.BlockSpec((1, 2, hps, HG_DK, HG_DV), lambda i: (i // n_hb, 0, i % n_hb, 0, 0))
    mask_spec = pl.BlockSpec((HG_BLOCK, HG_BLOCK), lambda i: (0, 0))
    in_specs = [mask_spec] * len(masks) + [seg_spec(s) for s in range(5)] + [pl.BlockSpec((1, HG_DV), lambda i: (0, 0))]
    args = list(masks) + [zh] * 5 + [norm_w]
    if s0 is not None:
        states, layer = s0
        in_specs.append(pl.BlockSpec((1, None, 2, hps, HG_DK, HG_DV),
                                     lambda i: (i // n_hb, layer, 0, i % n_hb, 0, 0)))
        args.append(states)
    out_specs = [pl.BlockSpec((seq_len, width), lambda i: (i // n_hb, i % n_hb))]
    out_shape = [jax.ShapeDtypeStruct((t, HG_WIDTH), BF16)]
    if want_final:
        out_specs.append(state_spec)
        out_shape.append(jax.ShapeDtypeStruct((n_batch, 2, HG_HEADS, HG_DK, HG_DV), F32))
    scratch = [pltpu.VMEM((hps, n_pair, HG_DV, 2 * HG_DK), F32),
               pltpu.VMEM((hps, n_pair, HG_DV, 2 * HG_DK), BF16),
               pltpu.VMEM((hps, seq_len, 2 * HG_DK), BF16),
               pltpu.VMEM((seq_len, width), F32),
               pltpu.VMEM((hps, n_pair, SUBLANES, 2 * HG_DK), F32)]
    stages = functools.partial(_hgrn_stages, seq_len, hps, s0 is not None, want_final)
    cost = hps * (seq_len // HG_BLOCK) * _hgrn_unit_cost()
    return n_batch * n_hb, _Part(stages, args, in_specs, out_specs, out_shape, scratch, cost)


def _out_stages(ins, outs, _):
    x_ref, mod_ref, oh_ref, oa_ref, g_ref, who_ref, wao_ref, wout_ref, nfw_ref, wff1_ref, wff2_ref, fnw_ref = ins
    (y_ref,) = outs
    tm = x_ref.shape[0]
    m = mod_ref[0]
    g1 = m[:, 2 * D_MODEL:3 * D_MODEL]
    sh2 = m[:, 3 * D_MODEL:4 * D_MODEL]
    sc2 = m[:, 4 * D_MODEL:5 * D_MODEL]
    g2 = m[:, 5 * D_MODEL:6 * D_MODEL]
    gates = g_ref[...].astype(F32)
    from_h = yield from _dot_by_cols(oh_ref[...], who_ref, (0, HG_WIDTH), (0, D_MODEL))
    from_a = yield from _dot_by_cols(oa_ref[...], wao_ref, (0, Q_WIDTH), (0, D_MODEL))
    merged = gates[:, 0:D_MODEL] * from_h + gates[:, D_MODEL:2 * D_MODEL] * from_a
    mixed = yield from _dot_by_cols(merged.astype(BF16), wout_ref, (0, D_MODEL), (0, D_MODEL))
    x1 = x_ref[...] + g1 * mixed
    h2 = (_rms_rows(x1, nfw_ref[...]) * (1.0 + sc2) + sh2).astype(BF16)
    acc = jnp.zeros_like(x1)
    for j in range(D_FF // D_MODEL):
        cols = slice(j * D_MODEL, (j + 1) * D_MODEL)
        hj = yield from _dot_by_cols(h2, wff1_ref, (0, D_MODEL), (cols.start, cols.stop))
        hj = jnp.maximum(hj, 0.0)
        acc = acc + (yield from _dot_by_cols((hj * hj).astype(BF16), wff2_ref, (cols.start, cols.stop), (0, D_MODEL)))
    y_ref[...] = _rms_rows(x1 + g2 * acc, fnw_ref[...])


def _out_part(tm, x2, mod, seq_len, oh, oa, gates, w_ho, w_ao, w_out, nfw, w_ff1, w_ff2, fnw):
    t = x2.shape[0]
    row = lambda i: (i, 0)
    in_specs = [pl.BlockSpec((tm, D_MODEL), row),
                _mod_spec(mod, tm, seq_len),
                pl.BlockSpec((tm, HG_WIDTH), row),
                pl.BlockSpec((tm, Q_WIDTH), row),
                pl.BlockSpec((tm, 2 * D_MODEL), row),
                _const_spec((HG_WIDTH, D_MODEL)),
                _const_spec((Q_WIDTH, D_MODEL)),
                _const_spec((D_MODEL, D_MODEL)),
                _const_spec((1, D_MODEL)),
                _const_spec((D_MODEL, D_FF)),
                _const_spec((D_FF, D_MODEL)),
                _const_spec((1, D_MODEL))]
    args = [x2, mod.rows, oh, oa, gates, w_ho, w_ao, w_out, nfw, w_ff1, w_ff2, fnw]
    cost = _mm_cost(tm, HG_WIDTH + Q_WIDTH, D_MODEL) + _mm_cost(tm, D_MODEL, D_MODEL) + 2 * _mm_cost(tm, D_MODEL, D_FF)
    return t // tm, _Part(_out_stages, args, in_specs, [pl.BlockSpec((tm, D_MODEL), row)],
                          [jax.ShapeDtypeStruct((t, D_MODEL), F32)], [], cost)


def _cast_stages(ins, outs, _):
    for src, dst in zip(ins, outs):
        dst[...] = src[...].astype(dst.dtype)
    yield 1


def _cast_part(arrays, n_steps):
    in_specs, out_specs, out_shape = [], [], []
    for a in arrays:
        rows, cols = a.shape
        assert rows % (n_steps * MXU_ROWS) == 0
        spec = pl.BlockSpec((rows // n_steps, cols), lambda i: (i, 0))
        in_specs.append(spec)
        out_specs.append(spec)
        out_shape.append(jax.ShapeDtypeStruct(a.shape, BF16))
    return n_steps, _Part(_cast_stages, list(arrays), in_specs, out_specs, out_shape, [], 1)


def _rope_tables(n_tokens):
    rows = n_tokens // GRID_W
    row = np.repeat(np.arange(rows, dtype=np.float32), GRID_W)
    col = np.tile(np.arange(GRID_W, dtype=np.float32), rows)
    axis_dim = HEAD_DIM // 2
    freqs = (ROPE_THETA ** (-np.arange(0, axis_dim, 2, dtype=np.float32) / axis_dim)).astype(np.float32)
    ang_r = row[:, None] * freqs
    ang_c = col[:, None] * freqs
    cr, sr, cc, sc = np.cos(ang_r), np.sin(ang_r), np.cos(ang_c), np.sin(ang_c)
    zero = np.zeros_like(sr)
    cos = np.concatenate([cr, cr, cc, cc], axis=-1)
    s_up = np.concatenate([-sr, zero, -sc, zero], axis=-1)
    s_dn = np.concatenate([zero, sr, zero, sc], axis=-1)
    return tuple(jnp.asarray(np.tile(a, (1, N_HEADS)), dtype=F32) for a in (cos, s_up, s_dn))


def _to_slab(kv):
    n, depth, length = kv.shape[:3]
    return jnp.transpose(kv, (0, 1, 3, 4, 2)).reshape(n, depth, KV_WIDTH, length)


def _from_slab(slab):
    n, _, length = slab.shape
    return jnp.transpose(slab.reshape(n, N_KV, HEAD_DIM, length), (0, 3, 1, 2))


def _block_ones(width):
    idx = np.arange(width) // HEAD_DIM
    return jnp.asarray((idx[:, None] == idx[None, :]).astype(np.float32), dtype=BF16)


def _same_steps(*counted_parts):
    steps = {n for n, _ in counted_parts}
    assert len(steps) == 1, steps
    return steps.pop(), [p for _, p in counted_parts]


def kernel(x_prompt, x_sample, cache_k, cache_v, state_hgrn, c, c_ctx, w_ada, b_ada, norm_mix_w, w_in, q_norm_w, k_norm_w, hgrn_lb_logits, hgrn_norm_w, w_hgrn_out, w_attn_out, w_out, norm_ffn_w, w_ff1, w_ff2, final_norm_w):
    n_p, l_p, _ = x_prompt.shape
    n_s, l_s, _ = x_sample.shape
    layer = 0

    mod = _modulation(c_ctx[None, :], c, w_ada[layer], b_ada[layer][None, :])
    mod_p = _Mod(mod, 0, True)
    mod_s = _Mod(mod, 1, False)

    in_small = (q_norm_w[layer][None, :], k_norm_w[layer][None, :],
                _block_ones(Q_WIDTH), _block_ones(KV_WIDTH), hgrn_lb_logits)
    nmw = norm_mix_w[layer][None, :]
    hnw = hgrn_norm_w[layer][None, :]
    xp2 = x_prompt.reshape(n_p * l_p, D_MODEL)
    xs2 = x_sample.reshape(n_s * l_s, D_MODEL)

    n_in, in_p = _in_part(MIX_TOKEN_TILE, xp2, mod_p, l_p, nmw, w_in[layer], *in_small, None)
    steps, parts = _same_steps((n_in, in_p), _cast_part(
        [w_in[layer], w_hgrn_out[layer], w_attn_out[layer], w_out[layer], w_ff1[layer], w_ff2[layer]], n_in))
    (zh_p, q_p, k_p, kt_p, vt_p, gates_p), (w_in_b, w_ho_b, w_ao_b, w_out_b, w_ff1_b, w_ff2_b) = _launch(
        "in_ctx", steps, parts)
    out_w = (w_ho_b, w_ao_b, w_out_b, norm_ffn_w[layer][None, :], w_ff1_b, w_ff2_b, final_norm_w[None, :])

    steps, parts = _same_steps(
        _attn_part(q_p, [(k_p, vt_p, None)], n_p, l_p, N_HEADS // 2),
        _in_part(MIX_TOKEN_TILE, xs2, mod_s, l_s, nmw, w_in_b, *in_small, _rope_tables(l_s)),
        _hgrn_part(zh_p, hnw, n_p, l_p, HG_HEADS, None, True))
    (oa_p,), (zh_s, q_s, k_s, vt_s, gates_s), (oh_p, s_fin) = _launch("in_latent_mix_ctx", steps, parts,
                                                                      MIX_PACES_IN)

    segs = [(_to_slab(cache_k), _to_slab(cache_v), layer), (k_s, vt_s, None)]
    steps, parts = _same_steps(
        _out_part(MIX_TOKEN_TILE, xp2, mod_p, l_p, oh_p, oa_p, gates_p, *out_w),
        _attn_part(q_s, segs, n_s, l_s, ATTN_LOOKAHEAD),
        _hgrn_part(zh_s, hnw, n_s, l_s, 1, (state_hgrn, layer), False))
    (y_p,), (oa_s,), (oh_s,) = _launch("out_ctx_mix_latent", steps, parts, MIX_PACES_OUT)

    steps, parts = _same_steps(_out_part(SOLO_TOKEN_TILE, xs2, mod_s, l_s, oh_s, oa_s, gates_s, *out_w))
    ((y_s,),) = _launch("out_latent", steps, parts)

    new_k = _from_slab(kt_p)[:, None]
    new_v = _from_slab(vt_p)[:, None]
    new_s = s_fin.reshape(n_p, 1, 2, HG_HEADS, HG_DK, HG_DV)
    return (y_p.reshape(n_p, l_p, D_MODEL), y_s.reshape(n_s, l_s, D_MODEL), new_k, new_v, new_s)
```

```python
import functools
from typing import Any, Callable, NamedTuple

import numpy as np
import jax
import jax.numpy as jnp
from jax import lax
from jax.experimental import pallas as pl
from jax.experimental.pallas import tpu as pltpu

F32 = jnp.float32
BF16 = jnp.bfloat16

D_MODEL = 1024
GRID_W = 64
EPS = 1e-6
HG_HEADS = 4
HG_DK = 128
HG_DV = 128
HG_WIDTH = HG_HEADS * HG_DK
HG_CHUNK = 32
HG_PAIR = 2 * HG_CHUNK
N_HEADS = 8
N_KV = 2
HEAD_DIM = 64
Q_WIDTH = N_HEADS * HEAD_DIM
KV_WIDTH = N_KV * HEAD_DIM
ROPE_THETA = 10000.0
D_FF = 4 * D_MODEL
N_MOD = 6
ZH_WIDTH = 5 * HG_WIDTH
D_IN = ZH_WIDTH + Q_WIDTH + 2 * KV_WIDTH + 2 * D_MODEL

V7X_VMEM_LIMIT_BYTES = 56 * 1024 * 1024
SUBLANES = 8
LANES = 128
MXU_ROWS = 16
MXU_TILE = 256
ROPE_PAIR_LANES = HEAD_DIM // 4
HG_BLOCK = 256
COND_ROWS = 16
ONES_ROWS = 16
LOG2_E = float(np.log2(np.e))
MIX_TOKEN_TILE = 256
SOLO_TOKEN_TILE = 512
ADALN_ROW_TILE = 256
W_CAST_ROWS = 128
DOT_COLS = 512
ATTN_Q_TILE = 256
ATTN_LOOKAHEAD = 2
MIX_PACES_IN = (1.0, 1.0, 0.75)
MIX_PACES_OUT = (0.97, 1.0, 0.9)


def _dot(a, b):
    return jnp.dot(a, b, preferred_element_type=F32)


def _dot_nt(a, b):
    return lax.dot_general(a, b, (((1,), (1,)), ((), ())), preferred_element_type=F32)


def _dot_tn(a, b):
    return lax.dot_general(a, b, (((0,), (0,)), ((), ())), preferred_element_type=F32)


def _mm_cost(m, k, n):
    return (m // MXU_ROWS) * -(-k // MXU_TILE) * -(-n // MXU_TILE)


def _dot_by_cols(a, w_ref, rows, cols):
    pieces = []
    for c in range(cols[0], cols[1], DOT_COLS):
        end = min(c + DOT_COLS, cols[1])
        pieces.append(_dot(a, w_ref[rows[0]:rows[1], c:end]))
        yield _mm_cost(a.shape[0], rows[1] - rows[0], end - c)
    return pieces[0] if len(pieces) == 1 else jnp.concatenate(pieces, axis=1)


def _split_bf16(x):
    hi = x.astype(BF16)
    lo = (x - hi.astype(F32)).astype(BF16)
    return hi, lo


def _const_spec(shape):
    nd = len(shape)
    return pl.BlockSpec(shape, lambda *_: (0,) * nd, pipeline_mode=pl.Buffered(1))


class _Part(NamedTuple):
    stages: Callable[..., Any]
    args: list
    in_specs: list
    out_specs: list
    out_shape: list
    scratch: list
    cost: int


def _interleave(gens, totals):
    done = [0] * len(gens)
    live = list(range(len(gens)))
    while live:
        i = min(live, key=lambda j: done[j] / totals[j])
        try:
            done[i] += next(gens[i])
        except StopIteration:
            live.remove(i)


def _launch(name, n_steps, parts, paces=None):
    paces = paces or [1.0] * len(parts)
    n_in = [len(p.args) for p in parts]
    n_out = [len(p.out_shape) for p in parts]
    n_scr = [len(p.scratch) for p in parts]

    def body(*refs):
        refs = list(refs)
        ins = [[refs.pop(0) for _ in range(n)] for n in n_in]
        outs = [[refs.pop(0) for _ in range(n)] for n in n_out]
        scrs = [[refs.pop(0) for _ in range(n)] for n in n_scr]
        _interleave([p.stages(i, o, s) for p, i, o, s in zip(parts, ins, outs, scrs)],
                    [p.cost * pace for p, pace in zip(parts, paces)])

    flat = pl.pallas_call(
        body,
        grid=(n_steps,),
        in_specs=[s for p in parts for s in p.in_specs],
        out_specs=[s for p in parts for s in p.out_specs],
        out_shape=[s for p in parts for s in p.out_shape],
        scratch_shapes=[s for p in parts for s in p.scratch],
        compiler_params=pltpu.CompilerParams(dimension_semantics=("arbitrary",),
                                             vmem_limit_bytes=V7X_VMEM_LIMIT_BYTES),
        name=name,
    )(*[a for p in parts for a in p.args])
    flat = list(flat)
    return [[flat.pop(0) for _ in range(n)] for n in n_out]


class _Mod(NamedTuple):
    rows: Any
    first: int
    shared: bool


def _mod_spec(mod, tile, seq_len, tile_of_step=lambda i: i):
    if mod.shared:
        return pl.BlockSpec((1, 1, N_MOD * D_MODEL), lambda i: (mod.first, 0, 0))
    assert seq_len % tile == 0
    per_seq = seq_len // tile
    return pl.BlockSpec((1, 1, N_MOD * D_MODEL), lambda i: (mod.first + tile_of_step(i) // per_seq, 0, 0))


def _mod_kernel(cctx_ref, c_ref, w_ref, b_ref, o_ref, cond_scr):
    n_req = c_ref.shape[0]
    cond_scr[...] = jnp.zeros_like(cond_scr)
    cond_scr[0:1, :] = cctx_ref[...]
    cond_scr[1:1 + n_req, :] = c_ref[...]
    c = cond_scr[...]
    x = c * jax.nn.sigmoid(c)
    x_hi, x_lo = _split_bf16(x)
    w = w_ref[...].astype(BF16)
    part = _dot(x_hi, w) + _dot(x_lo, w)

    @pl.when(pl.program_id(0) == 0)
    def _():
        o_ref[:, 0, :] = b_ref[...] + part

    @pl.when(pl.program_id(0) > 0)
    def _():
        o_ref[:, 0, :] += part


def _modulation(c_ctx, c, w_ada, b_ada):
    n = N_MOD * D_MODEL
    tk = ADALN_ROW_TILE
    n_req = c.shape[0]
    assert 1 + n_req <= COND_ROWS
    return pl.pallas_call(
        _mod_kernel,
        grid=(D_MODEL // tk,),
        in_specs=[pl.BlockSpec((1, tk), lambda j: (0, j)),
                  pl.BlockSpec((n_req, tk), lambda j: (0, j)),
                  pl.BlockSpec((tk, n), lambda j: (j, 0)),
                  pl.BlockSpec((1, n), lambda j: (0, 0))],
        out_specs=pl.BlockSpec((COND_ROWS, 1, n), lambda j: (0, 0, 0)),
        out_shape=jax.ShapeDtypeStruct((COND_ROWS, 1, n), F32),
        scratch_shapes=[pltpu.VMEM((COND_ROWS, tk), F32)],
        compiler_params=pltpu.CompilerParams(dimension_semantics=("arbitrary",),
                                             vmem_limit_bytes=V7X_VMEM_LIMIT_BYTES),
        name="adaln_mod",
    )(c_ctx, c, w_ada, b_ada)


def _rms_rows(x, w):
    return x * lax.rsqrt(jnp.mean(x * x, axis=-1, keepdims=True) + EPS) * w


def _head_sumsq(a, ones_ref):
    return _dot((a * a).astype(BF16), ones_ref[...])


def _head_rms(a, sumsq, w):
    return a * lax.rsqrt(sumsq * (1.0 / HEAD_DIM) + EPS) * w


def _forget_lower_bound(logits):
    e = jnp.exp(logits - jnp.max(logits, axis=1, keepdims=True))
    return e[:, 0, :] / jnp.sum(e, axis=1)


def _rope(x, cos, s_up, s_dn):
    cols = []
    for j in range(x.shape[1] // LANES):
        sl = slice(j * LANES, (j + 1) * LANES)
        xj = x[:, sl]
        cols.append(xj * cos[:, sl] + pltpu.roll(xj, LANES - ROPE_PAIR_LANES, 1) * s_up[:, sl]
                    + pltpu.roll(xj, ROPE_PAIR_LANES, 1) * s_dn[:, sl])
    return jnp.concatenate(cols, axis=1) if len(cols) > 1 else cols[0]


def _in_stages(latent, ins, outs, scrs):
    x_ref, mod_ref, nw_ref, w_ref, qw_ref, kw_ref, oq_ref, ok_ref, lbl_ref = ins[:9]
    if scrs:
        (w_bf_ref,) = scrs
        rows = W_CAST_ROWS

        @pl.when(pl.program_id(0) == 0)
        def _():
            def cast_rows(r, carry):
                sl = pl.ds(pl.multiple_of(r * rows, rows), rows)
                w_bf_ref[sl, :] = w_ref[sl, :].astype(BF16)
                return carry

            lax.fori_loop(0, w_ref.shape[0] // rows, cast_rows, 0)

        w_ref = w_bf_ref
    if latent:
        cos_ref, sup_ref, sdn_ref = ins[9:]
        zh_ref, q_ref, k_ref, vt_ref, g_ref = outs
    else:
        zh_ref, q_ref, k_ref, kt_ref, vt_ref, g_ref = outs
    tm = x_ref.shape[0]
    m = mod_ref[0]
    sh1 = m[:, 0:D_MODEL]
    sc1 = m[:, D_MODEL:2 * D_MODEL]
    h = _rms_rows(x_ref[...], nw_ref[...]) * (1.0 + sc1) + sh1
    hb = h.astype(BF16)
    c0 = ZH_WIDTH
    c1 = c0 + Q_WIDTH
    c2 = c1 + KV_WIDTH
    c3 = c2 + KV_WIDTH
    aq = _dot(hb, w_ref[:, c0:c1])
    ak = _dot(hb, w_ref[:, c1:c2])
    av = _dot(hb, w_ref[:, c2:c3])
    yield _mm_cost(tm, D_MODEL, c3 - c0)
    gl = yield from _dot_by_cols(hb, w_ref, (0, D_MODEL), (c3, D_IN))
    ssq = _head_sumsq(aq, oq_ref)
    ssk = _head_sumsq(ak, ok_ref)
    yield _mm_cost(tm, Q_WIDTH, Q_WIDTH) + _mm_cost(tm, KV_WIDTH, KV_WIDTH)
    lb = _forget_lower_bound(lbl_ref[...])

    def hgrn_segment(s):
        z = yield from _dot_by_cols(hb, w_ref, (0, D_MODEL), (s * HG_WIDTH, (s + 1) * HG_WIDTH))
        if s == 0:
            z = z * HG_DK ** -0.5
        elif s in (1, 2):
            lbv = lb[s - 1:s, :]
            z = jnp.log(lbv + (1.0 - lbv) * jax.nn.sigmoid(z))
        elif s == 4:
            z = z * jax.nn.sigmoid(z)
        zh_ref[:, s * HG_WIDTH:(s + 1) * HG_WIDTH] = z

    yield from hgrn_segment(1)
    yield from hgrn_segment(2)
    g_ref[...] = jax.nn.sigmoid(gl).astype(BF16)
    qn = _head_rms(aq, ssq, jnp.tile(qw_ref[...], (1, N_HEADS)))
    kn = _head_rms(ak, ssk, jnp.tile(kw_ref[...], (1, N_KV)))
    seq = vt_ref.shape[2]
    avt = av.T
    for s in range(vt_ref.shape[0]):
        vt_ref[s] = avt[:, s * seq:(s + 1) * seq].astype(vt_ref.dtype)
    for s in (4, 0, 3):
        yield from hgrn_segment(s)
    if latent:
        cos, sup, sdn = cos_ref[...], sup_ref[...], sdn_ref[...]
        qn = _rope(qn, cos, sup, sdn)
        kn = _rope(kn, cos[:, 0:KV_WIDTH], sup[:, 0:KV_WIDTH], sdn[:, 0:KV_WIDTH])
    else:
        knt = kn.T
        for s in range(kt_ref.shape[0]):
            kt_ref[s] = knt[:, s * seq:(s + 1) * seq]
    q_ref[...] = (qn * (HEAD_DIM ** -0.5 * LOG2_E)).astype(BF16)
    k_ref[...] = kn.astype(BF16)


def _in_part(tm, x2, mod, seq_len, norm_w, w_in, qw, kw, ones_q, ones_k, lb_logits, rope):
    t = x2.shape[0]
    latent = rope is not None
    per_seq = max(seq_len // tm, 1)
    row = lambda i: (i, 0)
    in_specs = [pl.BlockSpec((tm, D_MODEL), row),
                _mod_spec(mod, tm, seq_len),
                _const_spec((1, D_MODEL)),
                _const_spec((D_MODEL, D_IN)),
                _const_spec((1, HEAD_DIM)),
                _const_spec((1, HEAD_DIM)),
                _const_spec((Q_WIDTH, Q_WIDTH)),
                _const_spec((KV_WIDTH, KV_WIDTH)),
                _const_spec(lb_logits.shape)]
    args = [x2, mod.rows, norm_w, w_in, qw, kw, ones_q, ones_k, lb_logits]
    out_specs = [pl.BlockSpec((tm, ZH_WIDTH), row), pl.BlockSpec((tm, Q_WIDTH), row),
                 pl.BlockSpec((tm, KV_WIDTH), row)]
    out_shape = [jax.ShapeDtypeStruct((t, ZH_WIDTH), F32), jax.ShapeDtypeStruct((t, Q_WIDTH), BF16),
                 jax.ShapeDtypeStruct((t, KV_WIDTH), BF16)]
    if seq_len <= tm:
        assert tm % seq_len == 0
        t_spec = pl.BlockSpec((tm // seq_len, KV_WIDTH, seq_len), lambda i: (i, 0, 0))
    else:
        t_spec = pl.BlockSpec((1, KV_WIDTH, tm), lambda i: (i // per_seq, 0, i % per_seq))
    if latent:
        in_specs += [pl.BlockSpec((tm, Q_WIDTH), lambda i: (i % per_seq, 0))] * 3
        args += list(rope)
        out_specs.append(t_spec)
        out_shape.append(jax.ShapeDtypeStruct((t // seq_len, KV_WIDTH, seq_len), BF16))
    else:
        out_specs += [t_spec, t_spec]
        out_shape += [jax.ShapeDtypeStruct((t // seq_len, KV_WIDTH, seq_len), F32)] * 2
    out_specs.append(pl.BlockSpec((tm, 2 * D_MODEL), row))
    out_shape.append(jax.ShapeDtypeStruct((t, 2 * D_MODEL), BF16))
    cost = (_mm_cost(tm, D_MODEL, D_IN) + _mm_cost(tm, Q_WIDTH, Q_WIDTH) + _mm_cost(tm, KV_WIDTH, KV_WIDTH))
    scratch = [] if w_in.dtype == BF16 else [pltpu.VMEM((D_MODEL, D_IN), BF16)]
    return t // tm, _Part(functools.partial(_in_stages, latent), args, in_specs, out_specs, out_shape, scratch, cost)


def _attn_cost(lq, lks):
    per_head = sum(_mm_cost(lk, HEAD_DIM, lq) + _mm_cost(HEAD_DIM + ONES_ROWS, lk, lq) for lk in lks)
    return N_HEADS * per_head


def _attn_stages(k_transposed, lookahead, ins, outs, _):
    n_seg = len(k_transposed)
    q_ref, kv_refs = ins[0], ins[1:]
    (o_ref,) = outs
    q = q_ref[...]
    lq = q.shape[0]
    ks = [(kv_refs[2 * s][0].T if k_transposed[s] else kv_refs[2 * s][...]).astype(BF16) for s in range(n_seg)]
    vts = [kv_refs[2 * s + 1][0].astype(BF16) for s in range(n_seg)]
    vts = [[jnp.concatenate([vt[g * HEAD_DIM:(g + 1) * HEAD_DIM, :],
                             jnp.ones((ONES_ROWS, vt.shape[1]), BF16)], axis=0) for g in range(N_KV)]
           for vt in vts]
    score_cost = sum(_mm_cost(k.shape[0], HEAD_DIM, lq) for k in ks)
    value_cost = sum(_mm_cost(HEAD_DIM + ONES_ROWS, k.shape[0], lq) for k in ks)

    def scores(h):
        g = h // (N_HEADS // N_KV)
        qh = q[:, h * HEAD_DIM:(h + 1) * HEAD_DIM]
        return [_dot_nt(k[:, g * HEAD_DIM:(g + 1) * HEAD_DIM], qh) for k in ks]

    pending = []
    for h in range(lookahead):
        pending.append(scores(h))
        yield score_cost
    outs_t = []
    for h in range(N_HEADS):
        if h + lookahead < N_HEADS:
            pending.append(scores(h + lookahead))
            yield score_cost
        st = pending.pop(0)
        g = h // (N_HEADS // N_KV)
        m = functools.reduce(jnp.maximum, [jnp.max(s, axis=0, keepdims=True) for s in st])
        ot = functools.reduce(jnp.add, [_dot(vt[g], jnp.exp2(s - m).astype(BF16)) for vt, s in zip(vts, st)])
        outs_t.append(ot[0:HEAD_DIM, :] / ot[HEAD_DIM:HEAD_DIM + 1, :])
        if h == N_HEADS - 1:
            o_ref[...] = jnp.concatenate(outs_t, axis=0).T.astype(BF16)
        yield value_cost


def _attn_part(q, kv_segments, n_batch, seq_len, lookahead):
    t = q.shape[0]
    tq = ATTN_Q_TILE
    per_seq = seq_len // tq
    in_specs = [pl.BlockSpec((tq, Q_WIDTH), lambda i: (i, 0))]
    args = [q]
    for k, v_t, layer in kv_segments:
        lk = v_t.shape[-1]
        if v_t.ndim == 4:
            slab = pl.BlockSpec((1, None, KV_WIDTH, lk), lambda i, layer=layer: (i // per_seq, layer, 0, 0))
        else:
            slab = pl.BlockSpec((1, KV_WIDTH, lk), lambda i: (i // per_seq, 0, 0))
        k_spec = slab if k.ndim == v_t.ndim else pl.BlockSpec((lk, KV_WIDTH), lambda i: (i // per_seq, 0))
        in_specs += [k_spec, slab]
        args += [k, v_t]
    stages = functools.partial(_attn_stages, tuple(k.ndim == v_t.ndim for k, v_t, _ in kv_segments), lookahead)
    return n_batch * per_seq, _Part(stages, args, in_specs, [pl.BlockSpec((tq, Q_WIDTH), lambda i: (i, 0))],
                                    [jax.ShapeDtypeStruct((t, Q_WIDTH), BF16)], [],
                                    _attn_cost(tq, [v_t.shape[-1] for _, v_t, _ in kv_segments]))


def _hgrn_unit_cost():
    blk = HG_BLOCK
    return (2 * _mm_cost(blk, blk, 2 * HG_DK)
            + 2 * _mm_cost(blk, HG_DK, 2 * blk) + _mm_cost(blk, blk, HG_DV)
            + (blk // HG_PAIR) * (_mm_cost(HG_DV, HG_PAIR, 2 * HG_DK) + _mm_cost(HG_PAIR, 2 * HG_DK, HG_DV)))


def _hgrn_stages(seq_len, hps, has_s0, has_sfin, ins, outs, scrs):
    tf_ref, tb_ref, xf_ref, xb_ref, q_ref, lf_ref, lb_ref, v_ref, g_ref, nw_ref = ins[:10]
    s0_ref = ins[10] if has_s0 else None
    o_ref = outs[0]
    sfin_ref = outs[1] if has_sfin else None
    kv_scr, ss_scr, qd_scr, oi_scr, dec_scr = scrs
    n_blk = seq_len // HG_BLOCK
    n_pair = seq_len // HG_PAIR
    per_blk = HG_BLOCK // HG_CHUNK
    pairs_per_blk = HG_BLOCK // HG_PAIR
    one_row = jnp.ones((1, HG_DK), F32)

    def chunk_cumsum(t_ref, x):
        hi, lo = _split_bf16(x)
        r = _dot(t_ref[...], jnp.concatenate([hi, lo], axis=1))
        return r[:, 0:HG_DK] + r[:, HG_DK:2 * HG_DK]

    def chunk_edge_rows(x, first):
        off = 0 if first else HG_CHUNK - 1
        return [x[c * HG_CHUNK + off:c * HG_CHUNK + off + 1, :] for c in range(per_blk)]

    def spread_rows(rows):
        return jnp.concatenate([jnp.broadcast_to(r, (HG_CHUNK, HG_DK)) for r in rows], axis=0)

    units = [(blk, h) for blk in range(n_blk) for h in range(hps)]
    stage1 = []
    for blk, h in units:
        rows = slice(blk * HG_BLOCK, (blk + 1) * HG_BLOCK)
        cols = slice(h * HG_DK, (h + 1) * HG_DK)
        logf_f = lf_ref[rows, cols]
        logf_b = lb_ref[rows, cols]
        k_f = 1.0 - jnp.exp(logf_f)
        k_b = 1.0 - jnp.exp(logf_b)
        a_f = chunk_cumsum(tf_ref, logf_f)
        a_b = chunk_cumsum(tb_ref, logf_b)
        stage1.append((rows, cols, k_f, k_b, a_f, a_b))
        yield 2 * _mm_cost(HG_BLOCK, HG_BLOCK, 2 * HG_DK)
    stage2 = []
    for (blk, h), (rows, cols, k_f, k_b, a_f, a_b) in zip(units, stage1):
        q = q_ref[rows, cols]
        vb = v_ref[rows, cols].astype(BF16)
        ea_f = jnp.exp(a_f)
        ea_b = jnp.exp(a_b)
        qd_f = q * ea_f
        qd_b = q * ea_b
        kd_f = k_f * jnp.exp(-a_f)
        kd_b = k_b * jnp.exp(-a_b)
        dec_f = chunk_edge_rows(ea_f, first=False)
        dec_b = chunk_edge_rows(ea_b, first=True)
        ke_f = kd_f * spread_rows(dec_f)
        ke_b = kd_b * spread_rows(dec_b)
        even = [c % 2 == 0 for c in range(per_blk)]
        qd_pair = jnp.concatenate(
            [qd_f * spread_rows([one_row if even[c] else dec_f[c - 1] for c in range(per_blk)]),
             qd_b * spread_rows([dec_b[c + 1] if even[c] else one_row for c in range(per_blk)])], axis=1)
        ke_pair = jnp.concatenate(
            [ke_f * spread_rows([dec_f[c + 1] if even[c] else one_row for c in range(per_blk)]),
             ke_b * spread_rows([one_row if even[c] else dec_b[c - 1] for c in range(per_blk)])], axis=1).astype(BF16)
        sc_f = _dot_nt(qd_f.astype(BF16), jnp.concatenate([kd_f, ke_f], axis=0).astype(BF16))
        sc_b = _dot_nt(qd_b.astype(BF16), jnp.concatenate([kd_b, ke_b], axis=0).astype(BF16))
        for p in range(pairs_per_blk):
            pr = slice(p * HG_PAIR, (p + 1) * HG_PAIR)
            kv_scr[h, blk * pairs_per_blk + p] = _dot_tn(vb[pr, :], ke_pair[pr, :])
            dec_scr[h, blk * pairs_per_blk + p, 0:1, :] = jnp.concatenate(
                [dec_f[2 * p] * dec_f[2 * p + 1], dec_b[2 * p] * dec_b[2 * p + 1]], axis=1)
        qd_scr[h, rows, :] = qd_pair.astype(BF16)
        stage2.append((rows, cols, vb, sc_f, sc_b))
        yield 2 * _mm_cost(HG_BLOCK, HG_DK, 2 * HG_BLOCK) + pairs_per_blk * _mm_cost(HG_DV, HG_PAIR, 2 * HG_DK)
    for rows, cols, vb, sc_f, sc_b in stage2:
        s = (sc_f[:, 0:HG_BLOCK].astype(BF16) * tf_ref[...] + sc_f[:, HG_BLOCK:].astype(BF16) * xf_ref[...]
             + sc_b[:, 0:HG_BLOCK].astype(BF16) * tb_ref[...] + sc_b[:, HG_BLOCK:].astype(BF16) * xb_ref[...])
        oi_scr[rows, cols] = _dot(s, vb)
        yield _mm_cost(HG_BLOCK, HG_BLOCK, HG_DV)

    for h in range(hps):
        if has_s0:
            s_f, s_b = s0_ref[0, 0, h].T, s0_ref[0, 1, h].T
        else:
            s_f = s_b = jnp.zeros((HG_DV, HG_DK), F32)
        for pf in range(n_pair):
            pb = n_pair - 1 - pf
            ss_scr[h, pf, :, 0:HG_DK] = s_f.astype(BF16)
            s_f = dec_scr[h, pf, 0:1, 0:HG_DK] * s_f + kv_scr[h, pf, :, 0:HG_DK]
            ss_scr[h, pb, :, HG_DK:2 * HG_DK] = s_b.astype(BF16)
            s_b = dec_scr[h, pb, 0:1, HG_DK:2 * HG_DK] * s_b + kv_scr[h, pb, :, HG_DK:2 * HG_DK]
        if has_sfin:
            sfin_ref[0, 0, h] = s_f.T
            sfin_ref[0, 1, h] = s_b.T

    for blk, h in units:
        cols = slice(h * HG_DK, (h + 1) * HG_DK)
        for p in range(pairs_per_blk):
            pi = blk * pairs_per_blk + p
            rows = slice(pi * HG_PAIR, (pi + 1) * HG_PAIR)
            oi_scr[rows, cols] = oi_scr[rows, cols] + _dot_nt(qd_scr[h, rows, :], ss_scr[h, pi])
        yield pairs_per_blk * _mm_cost(HG_PAIR, 2 * HG_DK, HG_DV)

    for h in range(hps):
        cols = slice(h * HG_DK, (h + 1) * HG_DK)
        o = _rms_rows(oi_scr[:, cols], nw_ref[...])
        o_ref[:, cols] = (o * g_ref[:, cols]).astype(BF16)


def _chunk_masks():
    r = np.arange(HG_BLOCK)
    same = (r[:, None] // HG_CHUNK) == (r[None, :] // HG_CHUNK)
    lower = same & (r[None, :] <= r[:, None])
    upper = same & (r[None, :] >= r[:, None])
    same_pair = (r[:, None] // HG_PAIR) == (r[None, :] // HG_PAIR)
    cross_f = same_pair & (r[:, None] // HG_CHUNK == r[None, :] // HG_CHUNK + 1)
    return tuple(jnp.asarray(m.astype(np.float32), dtype=BF16) for m in (lower, upper, cross_f, cross_f.T))


def _hgrn_part(zh, norm_w, n_batch, seq_len, hps, s0, want_final):
    t = zh.shape[0]
    n_pair = seq_len // HG_PAIR
    width = hps * HG_DK
    n_hb = HG_HEADS // hps
    masks = _chunk_masks()

    def seg_spec(seg):
        return pl.BlockSpec((seq_len, width), lambda i: (i // n_hb, seg * n_hb + i % n_hb))

    state_spec = pl.BlockSpec((1, 2, hps, HG_DK, HG_DV), lambda i: (i // n_hb, 0, i % n_hb, 0, 0))
    mask_spec = pl.BlockSpec((HG_BLOCK, HG_BLOCK), lambda i: (0, 0))
    in_specs = [mask_spec] * len(masks) + [seg_spec(s) for s in range(5)] + [pl.BlockSpec((1, HG_DV), lambda i: (0, 0))]
    args = list(masks) + [zh] * 5 + [norm_w]
    if s0 is not None:
        states, layer = s0
        in_specs.append(pl.BlockSpec((1, None, 2, hps, HG_DK, HG_DV),
                                     lambda i: (i // n_hb, layer, 0, i % n_hb, 0, 0)))
        args.append(states)
    out_specs = [pl.BlockSpec((seq_len, width), lambda i: (i // n_hb, i % n_hb))]
    out_shape = [jax.ShapeDtypeStruct((t, HG_WIDTH), BF16)]
    if want_final:
        out_specs.append(state_spec)
        out_shape.append(jax.ShapeDtypeStruct((n_batch, 2, HG_HEADS, HG_DK, HG_DV), F32))
    scratch = [pltpu.VMEM((hps, n_pair, HG_DV, 2 * HG_DK), F32),
               pltpu.VMEM((hps, n_pair, HG_DV, 2 * HG_DK), BF16),
               pltpu.VMEM((hps, seq_len, 2 * HG_DK), BF16),
               pltpu.VMEM((seq_len, width), F32),
               pltpu.VMEM((hps, n_pair, SUBLANES, 2 * HG_DK), F32)]
    stages = functools.partial(_hgrn_stages, seq_len, hps, s0 is not None, want_final)
    cost = hps * (seq_len // HG_BLOCK) * _hgrn_unit_cost()
    return n_batch * n_hb, _Part(stages, args, in_specs, out_specs, out_shape, scratch, cost)


def _out_stages(ins, outs, _):
    x_ref, mod_ref, oh_ref, oa_ref, g_ref, who_ref, wao_ref, wout_ref, nfw_ref, wff1_ref, wff2_ref, fnw_ref = ins
    (y_ref,) = outs
    tm = x_ref.shape[0]
    m = mod_ref[0]
    g1 = m[:, 2 * D_MODEL:3 * D_MODEL]
    sh2 = m[:, 3 * D_MODEL:4 * D_MODEL]
    sc2 = m[:, 4 * D_MODEL:5 * D_MODEL]
    g2 = m[:, 5 * D_MODEL:6 * D_MODEL]
    gates = g_ref[...].astype(F32)
    from_h = yield from _dot_by_cols(oh_ref[...], who_ref, (0, HG_WIDTH), (0, D_MODEL))
    from_a = yield from _dot_by_cols(oa_ref[...], wao_ref, (0, Q_WIDTH), (0, D_MODEL))
    merged = gates[:, 0:D_MODEL] * from_h + gates[:, D_MODEL:2 * D_MODEL] * from_a
    mixed = yield from _dot_by_cols(merged.astype(BF16), wout_ref, (0, D_MODEL), (0, D_MODEL))
    x1 = x_ref[...] + g1 * mixed
    h2 = (_rms_rows(x1, nfw_ref[...]) * (1.0 + sc2) + sh2).astype(BF16)
    acc = jnp.zeros_like(x1)
    for j in range(D_FF // D_MODEL):
        cols = slice(j * D_MODEL, (j + 1) * D_MODEL)
        hj = yield from _dot_by_cols(h2, wff1_ref, (0, D_MODEL), (cols.start, cols.stop))
        hj = jnp.maximum(hj, 0.0)
        acc = acc + (yield from _dot_by_cols((hj * hj).astype(BF16), wff2_ref, (cols.start, cols.stop), (0, D_MODEL)))
    y_ref[...] = _rms_rows(x1 + g2 * acc, fnw_ref[...])


def _out_part(tm, x2, mod, seq_len, oh, oa, gates, w_ho, w_ao, w_out, nfw, w_ff1, w_ff2, fnw):
    t = x2.shape[0]
    row = lambda i: (i, 0)
    in_specs = [pl.BlockSpec((tm, D_MODEL), row),
                _mod_spec(mod, tm, seq_len),
                pl.BlockSpec((tm, HG_WIDTH), row),
                pl.BlockSpec((tm, Q_WIDTH), row),
                pl.BlockSpec((tm, 2 * D_MODEL), row),
                _const_spec((HG_WIDTH, D_MODEL)),
                _const_spec((Q_WIDTH, D_MODEL)),
                _const_spec((D_MODEL, D_MODEL)),
                _const_spec((1, D_MODEL)),
                _const_spec((D_MODEL, D_FF)),
                _const_spec((D_FF, D_MODEL)),
                _const_spec((1, D_MODEL))]
    args = [x2, mod.rows, oh, oa, gates, w_ho, w_ao, w_out, nfw, w_ff1, w_ff2, fnw]
    cost = _mm_cost(tm, HG_WIDTH + Q_WIDTH, D_MODEL) + _mm_cost(tm, D_MODEL, D_MODEL) + 2 * _mm_cost(tm, D_MODEL, D_FF)
    return t // tm, _Part(_out_stages, args, in_specs, [pl.BlockSpec((tm, D_MODEL), row)],
                          [jax.ShapeDtypeStruct((t, D_MODEL), F32)], [], cost)


def _cast_stages(ins, outs, _):
    for src, dst in zip(ins, outs):
        dst[...] = src[...].astype(dst.dtype)
    yield 1


def _cast_part(arrays, n_steps):
    in_specs, out_specs, out_shape = [], [], []
    for a in arrays:
        rows, cols = a.shape
        assert rows % (n_steps * MXU_ROWS) == 0
        spec = pl.BlockSpec((rows // n_steps, cols), lambda i: (i, 0))
        in_specs.append(spec)
        out_specs.append(spec)
        out_shape.append(jax.ShapeDtypeStruct(a.shape, BF16))
    return n_steps, _Part(_cast_stages, list(arrays), in_specs, out_specs, out_shape, [], 1)


def _rope_tables(n_tokens):
    rows = n_tokens // GRID_W
    row = np.repeat(np.arange(rows, dtype=np.float32), GRID_W)
    col = np.tile(np.arange(GRID_W, dtype=np.float32), rows)
    axis_dim = HEAD_DIM // 2
    freqs = (ROPE_THETA ** (-np.arange(0, axis_dim, 2, dtype=np.float32) / axis_dim)).astype(np.float32)
    ang_r = row[:, None] * freqs
    ang_c = col[:, None] * freqs
    cr, sr, cc, sc = np.cos(ang_r), np.sin(ang_r), np.cos(ang_c), np.sin(ang_c)
    zero = np.zeros_like(sr)
    cos = np.concatenate([cr, cr, cc, cc], axis=-1)
    s_up = np.concatenate([-sr, zero, -sc, zero], axis=-1)
    s_dn = np.concatenate([zero, sr, zero, sc], axis=-1)
    return tuple(jnp.asarray(np.tile(a, (1, N_HEADS)), dtype=F32) for a in (cos, s_up, s_dn))


def _to_slab(kv):
    n, depth, length = kv.shape[:3]
    return jnp.transpose(kv, (0, 1, 3, 4, 2)).reshape(n, depth, KV_WIDTH, length)


def _from_slab(slab):
    n, _, length = slab.shape
    return jnp.transpose(slab.reshape(n, N_KV, HEAD_DIM, length), (0, 3, 1, 2))


def _block_ones(width):
    idx = np.arange(width) // HEAD_DIM
    return jnp.asarray((idx[:, None] == idx[None, :]).astype(np.float32), dtype=BF16)


def _same_steps(*counted_parts):
    steps = {n for n, _ in counted_parts}
    assert len(steps) == 1, steps
    return steps.pop(), [p for _, p in counted_parts]


def kernel(x_prompt, x_sample, cache_k, cache_v, state_hgrn, c, c_ctx, w_ada, b_ada, norm_mix_w, w_in, q_norm_w, k_norm_w, hgrn_lb_logits, hgrn_norm_w, w_hgrn_out, w_attn_out, w_out, norm_ffn_w, w_ff1, w_ff2, final_norm_w):
    n_p, l_p, _ = x_prompt.shape
    n_s, l_s, _ = x_sample.shape
    layer = 0

    mod = _modulation(c_ctx[None, :], c, w_ada[layer], b_ada[layer][None, :])
    mod_p = _Mod(mod, 0, True)
    mod_s = _Mod(mod, 1, False)

    in_small = (q_norm_w[layer][None, :], k_norm_w[layer][None, :],
                _block_ones(Q_WIDTH), _block_ones(KV_WIDTH), hgrn_lb_logits)
    nmw = norm_mix_w[layer][None, :]
    hnw = hgrn_norm_w[layer][None, :]
    xp2 = x_prompt.reshape(n_p * l_p, D_MODEL)
    xs2 = x_sample.reshape(n_s * l_s, D_MODEL)

    n_in, in_p = _in_part(MIX_TOKEN_TILE, xp2, mod_p, l_p, nmw, w_in[layer], *in_small, None)
    steps, parts = _same_steps((n_in, in_p), _cast_part(
        [w_in[layer], w_hgrn_out[layer], w_attn_out[layer], w_out[layer], w_ff1[layer], w_ff2[layer]], n_in))
    (zh_p, q_p, k_p, kt_p, vt_p, gates_p), (w_in_b, w_ho_b, w_ao_b, w_out_b, w_ff1_b, w_ff2_b) = _launch(
        "in_ctx", steps, parts)
    out_w = (w_ho_b, w_ao_b, w_out_b, norm_ffn_w[layer][None, :], w_ff1_b, w_ff2_b, final_norm_w[None, :])

    steps, parts = _same_steps(
        _attn_part(q_p, [(k_p, vt_p, None)], n_p, l_p, N_HEADS),
        _in_part(MIX_TOKEN_TILE, xs2, mod_s, l_s, nmw, w_in_b, *in_small, _rope_tables(l_s)),
        _hgrn_part(zh_p, hnw, n_p, l_p, HG_HEADS, None, True))
    (oa_p,), (zh_s, q_s, k_s, vt_s, gates_s), (oh_p, s_fin) = _launch("in_latent_mix_ctx", steps, parts,
                                                                      MIX_PACES_IN)

    segs = [(_to_slab(cache_k), _to_slab(cache_v), layer), (k_s, vt_s, None)]
    steps, parts = _same_steps(
        _out_part(MIX_TOKEN_TILE, xp2, mod_p, l_p, oh_p, oa_p, gates_p, *out_w),
        _attn_part(q_s, segs, n_s, l_s, ATTN_LOOKAHEAD),
        _hgrn_part(zh_s, hnw, n_s, l_s, 1, (state_hgrn, layer), False))
    (y_p,), (oa_s,), (oh_s,) = _launch("out_ctx_mix_latent", steps, parts, MIX_PACES_OUT)

    steps, parts = _same_steps(_out_part(SOLO_TOKEN_TILE, xs2, mod_s, l_s, oh_s, oa_s, gates_s, *out_w))
    ((y_s,),) = _launch("out_latent", steps, parts)

    new_k = _from_slab(kt_p)[:, None]
    new_v = _from_slab(vt_p)[:, None]
    new_s = s_fin.reshape(n_p, 1, 2, HG_HEADS, HG_DK, HG_DV)
    return (y_p.reshape(n_p, l_p, D_MODEL), y_s.reshape(n_s, l_s, D_MODEL), new_k, new_v, new_s)
```

```python
import functools
from typing import Any, Callable, NamedTuple

import numpy as np
import jax
import jax.numpy as jnp
from jax import lax
from jax.experimental import pallas as pl
from jax.experimental.pallas import tpu as pltpu

F32 = jnp.float32
BF16 = jnp.bfloat16

D_MODEL = 1024
GRID_W = 64
EPS = 1e-6
HG_HEADS = 4
HG_DK = 128
HG_DV = 128
HG_WIDTH = HG_HEADS * HG_DK
HG_CHUNK = 32
HG_PAIR = 2 * HG_CHUNK
N_HEADS = 8
N_KV = 2
HEAD_DIM = 64
Q_WIDTH = N_HEADS * HEAD_DIM
KV_WIDTH = N_KV * HEAD_DIM
ROPE_THETA = 10000.0
D_FF = 4 * D_MODEL
N_MOD = 6
ZH_WIDTH = 5 * HG_WIDTH
D_IN = ZH_WIDTH + Q_WIDTH + 2 * KV_WIDTH + 2 * D_MODEL

V7X_VMEM_LIMIT_BYTES = 56 * 1024 * 1024
SUBLANES = 8
LANES = 128
MXU_ROWS = 16
MXU_TILE = 256
ROPE_PAIR_LANES = HEAD_DIM // 4
HG_BLOCK = 256
COND_ROWS = 16
ONES_ROWS = 16
LOG2_E = float(np.log2(np.e))
MIX_TOKEN_TILE = 256
SOLO_TOKEN_TILE = 512
ADALN_ROW_TILE = 256
W_CAST_ROWS = 128
DOT_COLS = 512
ATTN_Q_TILE = 256
ATTN_LOOKAHEAD = 2
MIX_PACES_IN = (1.0, 1.0, 0.75)
MIX_PACES_OUT = (1.0, 1.0, 0.9)


def _dot(a, b):
    return jnp.dot(a, b, preferred_element_type=F32)


def _dot_nt(a, b):
    return lax.dot_general(a, b, (((1,), (1,)), ((), ())), preferred_element_type=F32)


def _dot_tn(a, b):
    return lax.dot_general(a, b, (((0,), (0,)), ((), ())), preferred_element_type=F32)


def _mm_cost(m, k, n):
    return (m // MXU_ROWS) * -(-k // MXU_TILE) * -(-n // MXU_TILE)


def _dot_by_cols(a, w_ref, rows, cols):
    pieces = []
    for c in range(cols[0], cols[1], DOT_COLS):
        end = min(c + DOT_COLS, cols[1])
        pieces.append(_dot(a, w_ref[rows[0]:rows[1], c:end]))
        yield _mm_cost(a.shape[0], rows[1] - rows[0], end - c)
    return pieces[0] if len(pieces) == 1 else jnp.concatenate(pieces, axis=1)


def _split_bf16(x):
    hi = x.astype(BF16)
    lo = (x - hi.astype(F32)).astype(BF16)
    return hi, lo


def _const_spec(shape):
    nd = len(shape)
    return pl.BlockSpec(shape, lambda *_: (0,) * nd, pipeline_mode=pl.Buffered(1))


class _Part(NamedTuple):
    stages: Callable[..., Any]
    args: list
    in_specs: list
    out_specs: list
    out_shape: list
    scratch: list
    cost: int


def _interleave(gens, totals):
    done = [0] * len(gens)
    live = list(range(len(gens)))
    while live:
        i = min(live, key=lambda j: done[j] / totals[j])
        try:
            done[i] += next(gens[i])
        except StopIteration:
            live.remove(i)


def _launch(name, n_steps, parts, paces=None):
    paces = paces or [1.0] * len(parts)
    n_in = [len(p.args) for p in parts]
    n_out = [len(p.out_shape) for p in parts]
    n_scr = [len(p.scratch) for p in parts]

    def body(*refs):
        refs = list(refs)
        ins = [[refs.pop(0) for _ in range(n)] for n in n_in]
        outs = [[refs.pop(0) for _ in range(n)] for n in n_out]
        scrs = [[refs.pop(0) for _ in range(n)] for n in n_scr]
        _interleave([p.stages(i, o, s) for p, i, o, s in zip(parts, ins, outs, scrs)],
                    [p.cost * pace for p, pace in zip(parts, paces)])

    flat = pl.pallas_call(
        body,
        grid=(n_steps,),
        in_specs=[s for p in parts for s in p.in_specs],
        out_specs=[s for p in parts for s in p.out_specs],
        out_shape=[s for p in parts for s in p.out_shape],
        scratch_shapes=[s for p in parts for s in p.scratch],
        compiler_params=pltpu.CompilerParams(dimension_semantics=("arbitrary",),
                                             vmem_limit_bytes=V7X_VMEM_LIMIT_BYTES),
        name=name,
    )(*[a for p in parts for a in p.args])
    flat = list(flat)
    return [[flat.pop(0) for _ in range(n)] for n in n_out]


class _Mod(NamedTuple):
    rows: Any
    first: int
    shared: bool


def _mod_spec(mod, tile, seq_len, tile_of_step=lambda i: i):
    if mod.shared:
        return pl.BlockSpec((1, 1, N_MOD * D_MODEL), lambda i: (mod.first, 0, 0))
    assert seq_len % tile == 0
    per_seq = seq_len // tile
    return pl.BlockSpec((1, 1, N_MOD * D_MODEL), lambda i: (mod.first + tile_of_step(i) // per_seq, 0, 0))


def _mod_kernel(cctx_ref, c_ref, w_ref, b_ref, o_ref, cond_scr):
    n_req = c_ref.shape[0]
    cond_scr[...] = jnp.zeros_like(cond_scr)
    cond_scr[0:1, :] = cctx_ref[...]
    cond_scr[1:1 + n_req, :] = c_ref[...]
    c = cond_scr[...]
    x = c * jax.nn.sigmoid(c)
    x_hi, x_lo = _split_bf16(x)
    w = w_ref[...].astype(BF16)
    part = _dot(x_hi, w) + _dot(x_lo, w)

    @pl.when(pl.program_id(0) == 0)
    def _():
        o_ref[:, 0, :] = b_ref[...] + part

    @pl.when(pl.program_id(0) > 0)
    def _():
        o_ref[:, 0, :] += part


def _modulation(c_ctx, c, w_ada, b_ada):
    n = N_MOD * D_MODEL
    tk = ADALN_ROW_TILE
    n_req = c.shape[0]
    assert 1 + n_req <= COND_ROWS
    return pl.pallas_call(
        _mod_kernel,
        grid=(D_MODEL // tk,),
        in_specs=[pl.BlockSpec((1, tk), lambda j: (0, j)),
                  pl.BlockSpec((n_req, tk), lambda j: (0, j)),
                  pl.BlockSpec((tk, n), lambda j: (j, 0)),
                  pl.BlockSpec((1, n), lambda j: (0, 0))],
        out_specs=pl.BlockSpec((COND_ROWS, 1, n), lambda j: (0, 0, 0)),
        out_shape=jax.ShapeDtypeStruct((COND_ROWS, 1, n), F32),
        scratch_shapes=[pltpu.VMEM((COND_ROWS, tk), F32)],
        compiler_params=pltpu.CompilerParams(dimension_semantics=("arbitrary",),
                                             vmem_limit_bytes=V7X_VMEM_LIMIT_BYTES),
        name="adaln_mod",
    )(c_ctx, c, w_ada, b_ada)


def _rms_rows(x, w):
    return x * lax.rsqrt(jnp.mean(x * x, axis=-1, keepdims=True) + EPS) * w


def _head_sumsq(a, ones_ref):
    return _dot((a * a).astype(BF16), ones_ref[...])


def _head_rms(a, sumsq, w):
    return a * lax.rsqrt(sumsq * (1.0 / HEAD_DIM) + EPS) * w


def _forget_lower_bound(logits):
    e = jnp.exp(logits - jnp.max(logits, axis=1, keepdims=True))
    return e[:, 0, :] / jnp.sum(e, axis=1)


def _rope(x, cos, s_up, s_dn):
    cols = []
    for j in range(x.shape[1] // LANES):
        sl = slice(j * LANES, (j + 1) * LANES)
        xj = x[:, sl]
        cols.append(xj * cos[:, sl] + pltpu.roll(xj, LANES - ROPE_PAIR_LANES, 1) * s_up[:, sl]
                    + pltpu.roll(xj, ROPE_PAIR_LANES, 1) * s_dn[:, sl])
    return jnp.concatenate(cols, axis=1) if len(cols) > 1 else cols[0]


def _in_stages(latent, ins, outs, scrs):
    x_ref, mod_ref, nw_ref, w_ref, qw_ref, kw_ref, oq_ref, ok_ref, lbl_ref = ins[:9]
    if scrs:
        (w_bf_ref,) = scrs
        rows = W_CAST_ROWS

        @pl.when(pl.program_id(0) == 0)
        def _():
            def cast_rows(r, carry):
                sl = pl.ds(pl.multiple_of(r * rows, rows), rows)
                w_bf_ref[sl, :] = w_ref[sl, :].astype(BF16)
                return carry

            lax.fori_loop(0, w_ref.shape[0] // rows, cast_rows, 0)

        w_ref = w_bf_ref
    if latent:
        cos_ref, sup_ref, sdn_ref = ins[9:]
        zh_ref, q_ref, k_ref, vt_ref, g_ref = outs
    else:
        zh_ref, q_ref, k_ref, kt_ref, vt_ref, g_ref = outs
    tm = x_ref.shape[0]
    m = mod_ref[0]
    sh1 = m[:, 0:D_MODEL]
    sc1 = m[:, D_MODEL:2 * D_MODEL]
    h = _rms_rows(x_ref[...], nw_ref[...]) * (1.0 + sc1) + sh1
    hb = h.astype(BF16)
    c0 = ZH_WIDTH
    c1 = c0 + Q_WIDTH
    c2 = c1 + KV_WIDTH
    c3 = c2 + KV_WIDTH
    aq = _dot(hb, w_ref[:, c0:c1])
    ak = _dot(hb, w_ref[:, c1:c2])
    av = _dot(hb, w_ref[:, c2:c3])
    yield _mm_cost(tm, D_MODEL, c3 - c0)
    gl = yield from _dot_by_cols(hb, w_ref, (0, D_MODEL), (c3, D_IN))
    ssq = _head_sumsq(aq, oq_ref)
    ssk = _head_sumsq(ak, ok_ref)
    yield _mm_cost(tm, Q_WIDTH, Q_WIDTH) + _mm_cost(tm, KV_WIDTH, KV_WIDTH)
    lb = _forget_lower_bound(lbl_ref[...])

    def hgrn_segment(s):
        z = yield from _dot_by_cols(hb, w_ref, (0, D_MODEL), (s * HG_WIDTH, (s + 1) * HG_WIDTH))
        if s == 0:
            z = z * HG_DK ** -0.5
        elif s in (1, 2):
            lbv = lb[s - 1:s, :]
            z = jnp.log(lbv + (1.0 - lbv) * jax.nn.sigmoid(z))
        elif s == 4:
            z = z * jax.nn.sigmoid(z)
        zh_ref[:, s * HG_WIDTH:(s + 1) * HG_WIDTH] = z

    yield from hgrn_segment(1)
    yield from hgrn_segment(2)
    g_ref[...] = jax.nn.sigmoid(gl).astype(BF16)
    qn = _head_rms(aq, ssq, jnp.tile(qw_ref[...], (1, N_HEADS)))
    kn = _head_rms(ak, ssk, jnp.tile(kw_ref[...], (1, N_KV)))
    seq = vt_ref.shape[2]
    avt = av.T
    for s in range(vt_ref.shape[0]):
        vt_ref[s] = avt[:, s * seq:(s + 1) * seq].astype(vt_ref.dtype)
    for s in (4, 0, 3):
        yield from hgrn_segment(s)
    if latent:
        cos, sup, sdn = cos_ref[...], sup_ref[...], sdn_ref[...]
        qn = _rope(qn, cos, sup, sdn)
        kn = _rope(kn, cos[:, 0:KV_WIDTH], sup[:, 0:KV_WIDTH], sdn[:, 0:KV_WIDTH])
    else:
        knt = kn.T
        for s in range(kt_ref.shape[0]):
            kt_ref[s] = knt[:, s * seq:(s + 1) * seq]
    q_ref[...] = (qn * (HEAD_DIM ** -0.5 * LOG2_E)).astype(BF16)
    k_ref[...] = kn.astype(BF16)


def _in_part(tm, x2, mod, seq_len, norm_w, w_in, qw, kw, ones_q, ones_k, lb_logits, rope):
    t = x2.shape[0]
    latent = rope is not None
    per_seq = max(seq_len // tm, 1)
    row = lambda i: (i, 0)
    in_specs = [pl.BlockSpec((tm, D_MODEL), row),
                _mod_spec(mod, tm, seq_len),
                _const_spec((1, D_MODEL)),
                _const_spec((D_MODEL, D_IN)),
                _const_spec((1, HEAD_DIM)),
                _const_spec((1, HEAD_DIM)),
                _const_spec((Q_WIDTH, Q_WIDTH)),
                _const_spec((KV_WIDTH, KV_WIDTH)),
                _const_spec(lb_logits.shape)]
    args = [x2, mod.rows, norm_w, w_in, qw, kw, ones_q, ones_k, lb_logits]
    out_specs = [pl.BlockSpec((tm, ZH_WIDTH), row), pl.BlockSpec((tm, Q_WIDTH), row),
                 pl.BlockSpec((tm, KV_WIDTH), row)]
    out_shape = [jax.ShapeDtypeStruct((t, ZH_WIDTH), F32), jax.ShapeDtypeStruct((t, Q_WIDTH), BF16),
                 jax.ShapeDtypeStruct((t, KV_WIDTH), BF16)]
    if seq_len <= tm:
        assert tm % seq_len == 0
        t_spec = pl.BlockSpec((tm // seq_len, KV_WIDTH, seq_len), lambda i: (i, 0, 0))
    else:
        t_spec = pl.BlockSpec((1, KV_WIDTH, tm), lambda i: (i // per_seq, 0, i % per_seq))
    if latent:
        in_specs += [pl.BlockSpec((tm, Q_WIDTH), lambda i: (i % per_seq, 0))] * 3
        args += list(rope)
        out_specs.append(t_spec)
        out_shape.append(jax.ShapeDtypeStruct((t // seq_len, KV_WIDTH, seq_len), BF16))
    else:
        out_specs += [t_spec, t_spec]
        out_shape += [jax.ShapeDtypeStruct((t // seq_len, KV_WIDTH, seq_len), F32)] * 2
    out_specs.append(pl.BlockSpec((tm, 2 * D_MODEL), row))
    out_shape.append(jax.ShapeDtypeStruct((t, 2 * D_MODEL), BF16))
    cost = (_mm_cost(tm, D_MODEL, D_IN) + _mm_cost(tm, Q_WIDTH, Q_WIDTH) + _mm_cost(tm, KV_WIDTH, KV_WIDTH))
    scratch = [] if w_in.dtype == BF16 else [pltpu.VMEM((D_MODEL, D_IN), BF16)]
    return t // tm, _Part(functools.partial(_in_stages, latent), args, in_specs, out_specs, out_shape, scratch, cost)


def _attn_cost(lq, lks):
    per_head = sum(_mm_cost(lk, HEAD_DIM, lq) + _mm_cost(HEAD_DIM + ONES_ROWS, lk, lq) for lk in lks)
    return N_HEADS * per_head


def _attn_stages(k_transposed, lookahead, ins, outs, _):
    n_seg = len(k_transposed)
    q_ref, kv_refs = ins[0], ins[1:]
    (o_ref,) = outs
    q = q_ref[...]
    lq = q.shape[0]
    ks = [(kv_refs[2 * s][0].T if k_transposed[s] else kv_refs[2 * s][...]).astype(BF16) for s in range(n_seg)]
    vts = [kv_refs[2 * s + 1][0].astype(BF16) for s in range(n_seg)]
    vts = [[jnp.concatenate([vt[g * HEAD_DIM:(g + 1) * HEAD_DIM, :],
                             jnp.ones((ONES_ROWS, vt.shape[1]), BF16)], axis=0) for g in range(N_KV)]
           for vt in vts]
    score_cost = sum(_mm_cost(k.shape[0], HEAD_DIM, lq) for k in ks)
    value_cost = sum(_mm_cost(HEAD_DIM + ONES_ROWS, k.shape[0], lq) for k in ks)

    def scores(h):
        g = h // (N_HEADS // N_KV)
        qh = q[:, h * HEAD_DIM:(h + 1) * HEAD_DIM]
        return [_dot_nt(k[:, g * HEAD_DIM:(g + 1) * HEAD_DIM], qh) for k in ks]

    pending = []
    for h in range(lookahead):
        pending.append(scores(h))
        yield score_cost
    outs_t = []
    for h in range(N_HEADS):
        if h + lookahead < N_HEADS:
            pending.append(scores(h + lookahead))
            yield score_cost
        st = pending.pop(0)
        g = h // (N_HEADS // N_KV)
        m = functools.reduce(jnp.maximum, [jnp.max(s, axis=0, keepdims=True) for s in st])
        ot = functools.reduce(jnp.add, [_dot(vt[g], jnp.exp2(s - m).astype(BF16)) for vt, s in zip(vts, st)])
        outs_t.append(ot[0:HEAD_DIM, :] / ot[HEAD_DIM:HEAD_DIM + 1, :])
        if h == N_HEADS - 1:
            o_ref[...] = jnp.concatenate(outs_t, axis=0).T.astype(BF16)
        yield value_cost


def _attn_part(q, kv_segments, n_batch, seq_len, lookahead):
    t = q.shape[0]
    tq = ATTN_Q_TILE
    per_seq = seq_len // tq
    in_specs = [pl.BlockSpec((tq, Q_WIDTH), lambda i: (i, 0))]
    args = [q]
    for k, v_t, layer in kv_segments:
        lk = v_t.shape[-1]
        if v_t.ndim == 4:
            slab = pl.BlockSpec((1, None, KV_WIDTH, lk), lambda i, layer=layer: (i // per_seq, layer, 0, 0))
        else:
            slab = pl.BlockSpec((1, KV_WIDTH, lk), lambda i: (i // per_seq, 0, 0))
        k_spec = slab if k.ndim == v_t.ndim else pl.BlockSpec((lk, KV_WIDTH), lambda i: (i // per_seq, 0))
        in_specs += [k_spec, slab]
        args += [k, v_t]
    stages = functools.partial(_attn_stages, tuple(k.ndim == v_t.ndim for k, v_t, _ in kv_segments), lookahead)
    return n_batch * per_seq, _Part(stages, args, in_specs, [pl.BlockSpec((tq, Q_WIDTH), lambda i: (i, 0))],
                                    [jax.ShapeDtypeStruct((t, Q_WIDTH), BF16)], [],
                                    _attn_cost(tq, [v_t.shape[-1] for _, v_t, _ in kv_segments]))


def _hgrn_unit_cost():
    blk = HG_BLOCK
    return (2 * _mm_cost(blk, blk, 2 * HG_DK)
            + 2 * _mm_cost(blk, HG_DK, 2 * blk) + _mm_cost(blk, blk, HG_DV)
            + (blk // HG_PAIR) * (_mm_cost(HG_DV, HG_PAIR, 2 * HG_DK) + _mm_cost(HG_PAIR, 2 * HG_DK, HG_DV)))


def _hgrn_stages(seq_len, hps, has_s0, has_sfin, ins, outs, scrs):
    tf_ref, tb_ref, xf_ref, xb_ref, q_ref, lf_ref, lb_ref, v_ref, g_ref, nw_ref = ins[:10]
    s0_ref = ins[10] if has_s0 else None
    o_ref = outs[0]
    sfin_ref = outs[1] if has_sfin else None
    kv_scr, ss_scr, qd_scr, oi_scr, dec_scr = scrs
    n_blk = seq_len // HG_BLOCK
    n_pair = seq_len // HG_PAIR
    per_blk = HG_BLOCK // HG_CHUNK
    pairs_per_blk = HG_BLOCK // HG_PAIR
    one_row = jnp.ones((1, HG_DK), F32)

    def chunk_cumsum(t_ref, x):
        hi, lo = _split_bf16(x)
        r = _dot(t_ref[...], jnp.concatenate([hi, lo], axis=1))
        return r[:, 0:HG_DK] + r[:, HG_DK:2 * HG_DK]

    def chunk_edge_rows(x, first):
        off = 0 if first else HG_CHUNK - 1
        return [x[c * HG_CHUNK + off:c * HG_CHUNK + off + 1, :] for c in range(per_blk)]

    def spread_rows(rows):
        return jnp.concatenate([jnp.broadcast_to(r, (HG_CHUNK, HG_DK)) for r in rows], axis=0)

    units = [(blk, h) for blk in range(n_blk) for h in range(hps)]
    stage1 = []
    for blk, h in units:
        rows = slice(blk * HG_BLOCK, (blk + 1) * HG_BLOCK)
        cols = slice(h * HG_DK, (h + 1) * HG_DK)
        logf_f = lf_ref[rows, cols]
        logf_b = lb_ref[rows, cols]
        k_f = 1.0 - jnp.exp(logf_f)
        k_b = 1.0 - jnp.exp(logf_b)
        a_f = chunk_cumsum(tf_ref, logf_f)
        a_b = chunk_cumsum(tb_ref, logf_b)
        stage1.append((rows, cols, k_f, k_b, a_f, a_b))
        yield 2 * _mm_cost(HG_BLOCK, HG_BLOCK, 2 * HG_DK)
    stage2 = []
    for (blk, h), (rows, cols, k_f, k_b, a_f, a_b) in zip(units, stage1):
        q = q_ref[rows, cols]
        vb = v_ref[rows, cols].astype(BF16)
        ea_f = jnp.exp(a_f)
        ea_b = jnp.exp(a_b)
        qd_f = q * ea_f
        qd_b = q * ea_b
        kd_f = k_f * jnp.exp(-a_f)
        kd_b = k_b * jnp.exp(-a_b)
        dec_f = chunk_edge_rows(ea_f, first=False)
        dec_b = chunk_edge_rows(ea_b, first=True)
        ke_f = kd_f * spread_rows(dec_f)
        ke_b = kd_b * spread_rows(dec_b)
        even = [c % 2 == 0 for c in range(per_blk)]
        qd_pair = jnp.concatenate(
            [qd_f * spread_rows([one_row if even[c] else dec_f[c - 1] for c in range(per_blk)]),
             qd_b * spread_rows([dec_b[c + 1] if even[c] else one_row for c in range(per_blk)])], axis=1)
        ke_pair = jnp.concatenate(
            [ke_f * spread_rows([dec_f[c + 1] if even[c] else one_row for c in range(per_blk)]),
             ke_b * spread_rows([one_row if even[c] else dec_b[c - 1] for c in range(per_blk)])], axis=1).astype(BF16)
        sc_f = _dot_nt(qd_f.astype(BF16), jnp.concatenate([kd_f, ke_f], axis=0).astype(BF16))
        sc_b = _dot_nt(qd_b.astype(BF16), jnp.concatenate([kd_b, ke_b], axis=0).astype(BF16))
        for p in range(pairs_per_blk):
            pr = slice(p * HG_PAIR, (p + 1) * HG_PAIR)
            kv_scr[h, blk * pairs_per_blk + p] = _dot_tn(vb[pr, :], ke_pair[pr, :])
            dec_scr[h, blk * pairs_per_blk + p, 0:1, :] = jnp.concatenate(
                [dec_f[2 * p] * dec_f[2 * p + 1], dec_b[2 * p] * dec_b[2 * p + 1]], axis=1)
        qd_scr[h, rows, :] = qd_pair.astype(BF16)
        stage2.append((rows, cols, vb, sc_f, sc_b))
        yield 2 * _mm_cost(HG_BLOCK, HG_DK, 2 * HG_BLOCK) + pairs_per_blk * _mm_cost(HG_DV, HG_PAIR, 2 * HG_DK)
    for rows, cols, vb, sc_f, sc_b in stage2:
        s = (sc_f[:, 0:HG_BLOCK].astype(BF16) * tf_ref[...] + sc_f[:, HG_BLOCK:].astype(BF16) * xf_ref[...]
             + sc_b[:, 0:HG_BLOCK].astype(BF16) * tb_ref[...] + sc_b[:, HG_BLOCK:].astype(BF16) * xb_ref[...])
        oi_scr[rows, cols] = _dot(s, vb)
        yield _mm_cost(HG_BLOCK, HG_BLOCK, HG_DV)

    for h in range(hps):
        if has_s0:
            s_f, s_b = s0_ref[0, 0, h].T, s0_ref[0, 1, h].T
        else:
            s_f = s_b = jnp.zeros((HG_DV, HG_DK), F32)
        for pf in range(n_pair):
            pb = n_pair - 1 - pf
            ss_scr[h, pf, :, 0:HG_DK] = s_f.astype(BF16)
            s_f = dec_scr[h, pf, 0:1, 0:HG_DK] * s_f + kv_scr[h, pf, :, 0:HG_DK]
            ss_scr[h, pb, :, HG_DK:2 * HG_DK] = s_b.astype(BF16)
            s_b = dec_scr[h, pb, 0:1, HG_DK:2 * HG_DK] * s_b + kv_scr[h, pb, :, HG_DK:2 * HG_DK]
        if has_sfin:
            sfin_ref[0, 0, h] = s_f.T
            sfin_ref[0, 1, h] = s_b.T

    for blk, h in units:
        cols = slice(h * HG_DK, (h + 1) * HG_DK)
        for p in range(pairs_per_blk):
            pi = blk * pairs_per_blk + p
            rows = slice(pi * HG_PAIR, (pi + 1) * HG_PAIR)
            oi_scr[rows, cols] = oi_scr[rows, cols] + _dot_nt(qd_scr[h, rows, :], ss_scr[h, pi])
        yield pairs_per_blk * _mm_cost(HG_PAIR, 2 * HG_DK, HG_DV)

    for h in range(hps):
        cols = slice(h * HG_DK, (h + 1) * HG_DK)
        o = _rms_rows(oi_scr[:, cols], nw_ref[...])
        o_ref[:, cols] = (o * g_ref[:, cols]).astype(BF16)


def _chunk_masks():
    r = np.arange(HG_BLOCK)
    same = (r[:, None] // HG_CHUNK) == (r[None, :] // HG_CHUNK)
    lower = same & (r[None, :] <= r[:, None])
    upper = same & (r[None, :] >= r[:, None])
    same_pair = (r[:, None] // HG_PAIR) == (r[None, :] // HG_PAIR)
    cross_f = same_pair & (r[:, None] // HG_CHUNK == r[None, :] // HG_CHUNK + 1)
    return tuple(jnp.asarray(m.astype(np.float32), dtype=BF16) for m in (lower, upper, cross_f, cross_f.T))


def _hgrn_part(zh, norm_w, n_batch, seq_len, hps, s0, want_final):
    t = zh.shape[0]
    n_pair = seq_len // HG_PAIR
    width = hps * HG_DK
    n_hb = HG_HEADS // hps
    masks = _chunk_masks()

    def seg_spec(seg):
        return pl.BlockSpec((seq_len, width), lambda i: (i // n_hb, seg * n_hb + i % n_hb))

    state_spec = pl.BlockSpec((1, 2, hps, HG_DK, HG_DV), lambda i: (i // n_hb, 0, i % n_hb, 0, 0))
    mask_spec = pl.BlockSpec((HG_BLOCK, HG_BLOCK), lambda i: (0, 0))
    in_specs = [mask_spec] * len(masks) + [seg_spec(s) for s in range(5)] + [pl.BlockSpec((1, HG_DV), lambda i: (0, 0))]
    args = list(masks) + [zh] * 5 + [norm_w]
    if s0 is not None:
        states, layer = s0
        in_specs.append(pl.BlockSpec((1, None, 2, hps, HG_DK, HG_DV),
                                     lambda i: (i // n_hb, layer, 0, i % n_hb, 0, 0)))
        args.append(states)
    out_specs = [pl.BlockSpec((seq_len, width), lambda i: (i // n_hb, i % n_hb))]
    out_shape = [jax.ShapeDtypeStruct((t, HG_WIDTH), BF16)]
    if want_final:
        out_specs.append(state_spec)
        out_shape.append(jax.ShapeDtypeStruct((n_batch, 2, HG_HEADS, HG_DK, HG_DV), F32))
    scratch = [pltpu.VMEM((hps, n_pair, HG_DV, 2 * HG_DK), F32),
               pltpu.VMEM((hps, n_pair, HG_DV, 2 * HG_DK), BF16),
               pltpu.VMEM((hps, seq_len, 2 * HG_DK), BF16),
               pltpu.VMEM((seq_len, width), F32),
               pltpu.VMEM((hps, n_pair, SUBLANES, 2 * HG_DK), F32)]
    stages = functools.partial(_hgrn_stages, seq_len, hps, s0 is not None, want_final)
    cost = hps * (seq_len // HG_BLOCK) * _hgrn_unit_cost()
    return n_batch * n_hb, _Part(stages, args, in_specs, out_specs, out_shape, scratch, cost)


def _out_stages(ins, outs, _):
    x_ref, mod_ref, oh_ref, oa_ref, g_ref, who_ref, wao_ref, wout_ref, nfw_ref, wff1_ref, wff2_ref, fnw_ref = ins
    (y_ref,) = outs
    tm = x_ref.shape[0]
    m = mod_ref[0]
    g1 = m[:, 2 * D_MODEL:3 * D_MODEL]
    sh2 = m[:, 3 * D_MODEL:4 * D_MODEL]
    sc2 = m[:, 4 * D_MODEL:5 * D_MODEL]
    g2 = m[:, 5 * D_MODEL:6 * D_MODEL]
    gates = g_ref[...].astype(F32)
    from_h = yield from _dot_by_cols(oh_ref[...], who_ref, (0, HG_WIDTH), (0, D_MODEL))
    from_a = yield from _dot_by_cols(oa_ref[...], wao_ref, (0, Q_WIDTH), (0, D_MODEL))
    merged = gates[:, 0:D_MODEL] * from_h + gates[:, D_MODEL:2 * D_MODEL] * from_a
    mixed = yield from _dot_by_cols(merged.astype(BF16), wout_ref, (0, D_MODEL), (0, D_MODEL))
    x1 = x_ref[...] + g1 * mixed
    h2 = (_rms_rows(x1, nfw_ref[...]) * (1.0 + sc2) + sh2).astype(BF16)
    acc = jnp.zeros_like(x1)
    for j in range(D_FF // D_MODEL):
        cols = slice(j * D_MODEL, (j + 1) * D_MODEL)
        hj = yield from _dot_by_cols(h2, wff1_ref, (0, D_MODEL), (cols.start, cols.stop))
        hj = jnp.maximum(hj, 0.0)
        acc = acc + (yield from _dot_by_cols((hj * hj).astype(BF16), wff2_ref, (cols.start, cols.stop), (0, D_MODEL)))
    y_ref[...] = _rms_rows(x1 + g2 * acc, fnw_ref[...])


def _out_part(tm, x2, mod, seq_len, oh, oa, gates, w_ho, w_ao, w_out, nfw, w_ff1, w_ff2, fnw):
    t = x2.shape[0]
    row = lambda i: (i, 0)
    in_specs = [pl.BlockSpec((tm, D_MODEL), row),
                _mod_spec(mod, tm, seq_len),
                pl.BlockSpec((tm, HG_WIDTH), row),
                pl.BlockSpec((tm, Q_WIDTH), row),
                pl.BlockSpec((tm, 2 * D_MODEL), row),
                _const_spec((HG_WIDTH, D_MODEL)),
                _const_spec((Q_WIDTH, D_MODEL)),
                _const_spec((D_MODEL, D_MODEL)),
                _const_spec((1, D_MODEL)),
                _const_spec((D_MODEL, D_FF)),
                _const_spec((D_FF, D_MODEL)),
                _const_spec((1, D_MODEL))]
    args = [x2, mod.rows, oh, oa, gates, w_ho, w_ao, w_out, nfw, w_ff1, w_ff2, fnw]
    cost = _mm_cost(tm, HG_WIDTH + Q_WIDTH, D_MODEL) + _mm_cost(tm, D_MODEL, D_MODEL) + 2 * _mm_cost(tm, D_MODEL, D_FF)
    return t // tm, _Part(_out_stages, args, in_specs, [pl.BlockSpec((tm, D_MODEL), row)],
                          [jax.ShapeDtypeStruct((t, D_MODEL), F32)], [], cost)


def _cast_stages(ins, outs, _):
    for src, dst in zip(ins, outs):
        dst[...] = src[...].astype(dst.dtype)
    yield 1


def _cast_part(arrays, n_steps):
    in_specs, out_specs, out_shape = [], [], []
    for a in arrays:
        rows, cols = a.shape
        assert rows % (n_steps * MXU_ROWS) == 0
        spec = pl.BlockSpec((rows // n_steps, cols), lambda i: (i, 0))
        in_specs.append(spec)
        out_specs.append(spec)
        out_shape.append(jax.ShapeDtypeStruct(a.shape, BF16))
    return n_steps, _Part(_cast_stages, list(arrays), in_specs, out_specs, out_shape, [], 1)


def _rope_tables(n_tokens):
    rows = n_tokens // GRID_W
    row = np.repeat(np.arange(rows, dtype=np.float32), GRID_W)
    col = np.tile(np.arange(GRID_W, dtype=np.float32), rows)
    axis_dim = HEAD_DIM // 2
    freqs = (ROPE_THETA ** (-np.arange(0, axis_dim, 2, dtype=np.float32) / axis_dim)).astype(np.float32)
    ang_r = row[:, None] * freqs
    ang_c = col[:, None] * freqs
    cr, sr, cc, sc = np.cos(ang_r), np.sin(ang_r), np.cos(ang_c), np.sin(ang_c)
    zero = np.zeros_like(sr)
    cos = np.concatenate([cr, cr, cc, cc], axis=-1)
    s_up = np.concatenate([-sr, zero, -sc, zero], axis=-1)
    s_dn = np.concatenate([zero, sr, zero, sc], axis=-1)
    return tuple(jnp.asarray(np.tile(a, (1, N_HEADS)), dtype=F32) for a in (cos, s_up, s_dn))


def _to_slab(kv):
    n, depth, length = kv.shape[:3]
    return jnp.transpose(kv, (0, 1, 3, 4, 2)).reshape(n, depth, KV_WIDTH, length)


def _from_slab(slab):
    n, _, length = slab.shape
    return jnp.transpose(slab.reshape(n, N_KV, HEAD_DIM, length), (0, 3, 1, 2))


def _block_ones(width):
    idx = np.arange(width) // HEAD_DIM
    return jnp.asarray((idx[:, None] == idx[None, :]).astype(np.float32), dtype=BF16)


def _same_steps(*counted_parts):
    steps = {n for n, _ in counted_parts}
    assert len(steps) == 1, steps
    return steps.pop(), [p for _, p in counted_parts]


def kernel(x_prompt, x_sample, cache_k, cache_v, state_hgrn, c, c_ctx, w_ada, b_ada, norm_mix_w, w_in, q_norm_w, k_norm_w, hgrn_lb_logits, hgrn_norm_w, w_hgrn_out, w_attn_out, w_out, norm_ffn_w, w_ff1, w_ff2, final_norm_w):
    n_p, l_p, _ = x_prompt.shape
    n_s, l_s, _ = x_sample.shape
    layer = 0

    mod = _modulation(c_ctx[None, :], c, w_ada[layer], b_ada[layer][None, :])
    mod_p = _Mod(mod, 0, True)
    mod_s = _Mod(mod, 1, False)

    in_small = (q_norm_w[layer][None, :], k_norm_w[layer][None, :],
                _block_ones(Q_WIDTH), _block_ones(KV_WIDTH), hgrn_lb_logits)
    nmw = norm_mix_w[layer][None, :]
    hnw = hgrn_norm_w[layer][None, :]
    xp2 = x_prompt.reshape(n_p * l_p, D_MODEL)
    xs2 = x_sample.reshape(n_s * l_s, D_MODEL)

    n_in, in_p = _in_part(MIX_TOKEN_TILE, xp2, mod_p, l_p, nmw, w_in[layer], *in_small, None)
    steps, parts = _same_steps((n_in, in_p), _cast_part(
        [w_in[layer], w_hgrn_out[layer], w_attn_out[layer], w_out[layer], w_ff1[layer], w_ff2[layer]], n_in))
    (zh_p, q_p, k_p, kt_p, vt_p, gates_p), (w_in_b, w_ho_b, w_ao_b, w_out_b, w_ff1_b, w_ff2_b) = _launch(
        "in_ctx", steps, parts)
    out_w = (w_ho_b, w_ao_b, w_out_b, norm_ffn_w[layer][None, :], w_ff1_b, w_ff2_b, final_norm_w[None, :])

    steps, parts = _same_steps(
        _attn_part(q_p, [(k_p, vt_p, None)], n_p, l_p, N_HEADS),
        _in_part(MIX_TOKEN_TILE, xs2, mod_s, l_s, nmw, w_in_b, *in_small, _rope_tables(l_s)),
        _hgrn_part(zh_p, hnw, n_p, l_p, HG_HEADS, None, True))
    (oa_p,), (zh_s, q_s, k_s, vt_s, gates_s), (oh_p, s_fin) = _launch("in_latent_mix_ctx", steps, parts,
                                                                      MIX_PACES_IN)

    segs = [(_to_slab(cache_k), _to_slab(cache_v), layer), (k_s, vt_s, None)]
    steps, parts = _same_steps(
        _out_part(MIX_TOKEN_TILE, xp2, mod_p, l_p, oh_p, oa_p, gates_p, *out_w),
        _attn_part(q_s, segs, n_s, l_s, ATTN_LOOKAHEAD),
        _hgrn_part(zh_s, hnw, n_s, l_s, 1, (state_hgrn, layer), False))
    (y_p,), (oa_s,), (oh_s,) = _launch("out_ctx_mix_latent", steps, parts, MIX_PACES_OUT)

    steps, parts = _same_steps(_out_part(SOLO_TOKEN_TILE, xs2, mod_s, l_s, oh_s, oa_s, gates_s, *out_w))
    ((y_s,),) = _launch("out_latent", steps, parts)

    new_k = _from_slab(kt_p)[:, None]
    new_v = _from_slab(vt_p)[:, None]
    new_s = s_fin.reshape(n_p, 1, 2, HG_HEADS, HG_DK, HG_DV)
    return (y_p.reshape(n_p, l_p, D_MODEL), y_s.reshape(n_s, l_s, D_MODEL), new_k, new_v, new_s)
```

```python
import functools
from typing import Any, Callable, NamedTuple

import numpy as np
import jax
import jax.numpy as jnp
from jax import lax
from jax.experimental import pallas as pl
from jax.experimental.pallas import tpu as pltpu

F32 = jnp.float32
BF16 = jnp.bfloat16

D_MODEL = 1024
GRID_W = 64
EPS = 1e-6
HG_HEADS = 4
HG_DK = 128
HG_DV = 128
HG_WIDTH = HG_HEADS * HG_DK
HG_CHUNK = 32
HG_PAIR = 2 * HG_CHUNK
N_HEADS = 8
N_KV = 2
HEAD_DIM = 64
Q_WIDTH = N_HEADS * HEAD_DIM
KV_WIDTH = N_KV * HEAD_DIM
ROPE_THETA = 10000.0
D_FF = 4 * D_MODEL
N_MOD = 6
ZH_WIDTH = 5 * HG_WIDTH
D_IN = ZH_WIDTH + Q_WIDTH + 2 * KV_WIDTH + 2 * D_MODEL

V7X_VMEM_LIMIT_BYTES = 56 * 1024 * 1024
SUBLANES = 8
LANES = 128
MXU_ROWS = 16
MXU_TILE = 256
ROPE_PAIR_LANES = HEAD_DIM // 4
HG_BLOCK = 256
COND_ROWS = 16
ONES_ROWS = 16
LOG2_E = float(np.log2(np.e))
MIX_TOKEN_TILE = 256
SOLO_TOKEN_TILE = 512
ADALN_ROW_TILE = 256
W_CAST_ROWS = 128
DOT_COLS = 512
ATTN_Q_TILE = 256
ATTN_LOOKAHEAD = 2
MIX_PACES_IN = (1.0, 1.0, 0.75)
MIX_PACES_OUT = (0.95, 1.0, 0.9)


def _dot(a, b):
    return jnp.dot(a, b, preferred_element_type=F32)


def _dot_nt(a, b):
    return lax.dot_general(a, b, (((1,), (1,)), ((), ())), preferred_element_type=F32)


def _dot_tn(a, b):
    return lax.dot_general(a, b, (((0,), (0,)), ((), ())), preferred_element_type=F32)


def _mm_cost(m, k, n):
    return (m // MXU_ROWS) * -(-k // MXU_TILE) * -(-n // MXU_TILE)


def _dot_by_cols(a, w_ref, rows, cols):
    pieces = []
    for c in range(cols[0], cols[1], DOT_COLS):
        end = min(c + DOT_COLS, cols[1])
        pieces.append(_dot(a, w_ref[rows[0]:rows[1], c:end]))
        yield _mm_cost(a.shape[0], rows[1] - rows[0], end - c)
    return pieces[0] if len(pieces) == 1 else jnp.concatenate(pieces, axis=1)


def _split_bf16(x):
    hi = x.astype(BF16)
    lo = (x - hi.astype(F32)).astype(BF16)
    return hi, lo


def _const_spec(shape):
    nd = len(shape)
    return pl.BlockSpec(shape, lambda *_: (0,) * nd, pipeline_mode=pl.Buffered(1))


class _Part(NamedTuple):
    stages: Callable[..., Any]
    args: list
    in_specs: list
    out_specs: list
    out_shape: list
    scratch: list
    cost: int


def _interleave(gens, totals):
    done = [0] * len(gens)
    live = list(range(len(gens)))
    while live:
        i = min(live, key=lambda j: done[j] / totals[j])
        try:
            done[i] += next(gens[i])
        except StopIteration:
            live.remove(i)


def _launch(name, n_steps, parts, paces=None):
    paces = paces or [1.0] * len(parts)
    n_in = [len(p.args) for p in parts]
    n_out = [len(p.out_shape) for p in parts]
    n_scr = [len(p.scratch) for p in parts]

    def body(*refs):
        refs = list(refs)
        ins = [[refs.pop(0) for _ in range(n)] for n in n_in]
        outs = [[refs.pop(0) for _ in range(n)] for n in n_out]
        scrs = [[refs.pop(0) for _ in range(n)] for n in n_scr]
        _interleave([p.stages(i, o, s) for p, i, o, s in zip(parts, ins, outs, scrs)],
                    [p.cost * pace for p, pace in zip(parts, paces)])

    flat = pl.pallas_call(
        body,
        grid=(n_steps,),
        in_specs=[s for p in parts for s in p.in_specs],
        out_specs=[s for p in parts for s in p.out_specs],
        out_shape=[s for p in parts for s in p.out_shape],
        scratch_shapes=[s for p in parts for s in p.scratch],
        compiler_params=pltpu.CompilerParams(dimension_semantics=("arbitrary",),
                                             vmem_limit_bytes=V7X_VMEM_LIMIT_BYTES),
        name=name,
    )(*[a for p in parts for a in p.args])
    flat = list(flat)
    return [[flat.pop(0) for _ in range(n)] for n in n_out]


class _Mod(NamedTuple):
    rows: Any
    first: int
    shared: bool


def _mod_spec(mod, tile, seq_len, tile_of_step=lambda i: i):
    if mod.shared:
        return pl.BlockSpec((1, 1, N_MOD * D_MODEL), lambda i: (mod.first, 0, 0))
    assert seq_len % tile == 0
    per_seq = seq_len // tile
    return pl.BlockSpec((1, 1, N_MOD * D_MODEL), lambda i: (mod.first + tile_of_step(i) // per_seq, 0, 0))


def _mod_kernel(cctx_ref, c_ref, w_ref, b_ref, o_ref, cond_scr):
    n_req = c_ref.shape[0]
    cond_scr[...] = jnp.zeros_like(cond_scr)
    cond_scr[0:1, :] = cctx_ref[...]
    cond_scr[1:1 + n_req, :] = c_ref[...]
    c = cond_scr[...]
    x = c * jax.nn.sigmoid(c)
    x_hi, x_lo = _split_bf16(x)
    w = w_ref[...].astype(BF16)
    part = _dot(x_hi, w) + _dot(x_lo, w)

    @pl.when(pl.program_id(0) == 0)
    def _():
        o_ref[:, 0, :] = b_ref[...] + part

    @pl.when(pl.program_id(0) > 0)
    def _():
        o_ref[:, 0, :] += part


def _modulation(c_ctx, c, w_ada, b_ada):
    n = N_MOD * D_MODEL
    tk = ADALN_ROW_TILE
    n_req = c.shape[0]
    assert 1 + n_req <= COND_ROWS
    return pl.pallas_call(
        _mod_kernel,
        grid=(D_MODEL // tk,),
        in_specs=[pl.BlockSpec((1, tk), lambda j: (0, j)),
                  pl.BlockSpec((n_req, tk), lambda j: (0, j)),
                  pl.BlockSpec((tk, n), lambda j: (j, 0)),
                  pl.BlockSpec((1, n), lambda j: (0, 0))],
        out_specs=pl.BlockSpec((COND_ROWS, 1, n), lambda j: (0, 0, 0)),
        out_shape=jax.ShapeDtypeStruct((COND_ROWS, 1, n), F32),
        scratch_shapes=[pltpu.VMEM((COND_ROWS, tk), F32)],
        compiler_params=pltpu.CompilerParams(dimension_semantics=("arbitrary",),
                                             vmem_limit_bytes=V7X_VMEM_LIMIT_BYTES),
        name="adaln_mod",
    )(c_ctx, c, w_ada, b_ada)


def _rms_rows(x, w):
    return x * lax.rsqrt(jnp.mean(x * x, axis=-1, keepdims=True) + EPS) * w


def _head_sumsq(a, ones_ref):
    return _dot((a * a).astype(BF16), ones_ref[...])


def _head_rms(a, sumsq, w):
    return a * lax.rsqrt(sumsq * (1.0 / HEAD_DIM) + EPS) * w


def _forget_lower_bound(logits):
    e = jnp.exp(logits - jnp.max(logits, axis=1, keepdims=True))
    return e[:, 0, :] / jnp.sum(e, axis=1)


def _rope(x, cos, s_up, s_dn):
    cols = []
    for j in range(x.shape[1] // LANES):
        sl = slice(j * LANES, (j + 1) * LANES)
        xj = x[:, sl]
        cols.append(xj * cos[:, sl] + pltpu.roll(xj, LANES - ROPE_PAIR_LANES, 1) * s_up[:, sl]
                    + pltpu.roll(xj, ROPE_PAIR_LANES, 1) * s_dn[:, sl])
    return jnp.concatenate(cols, axis=1) if len(cols) > 1 else cols[0]


def _in_stages(latent, ins, outs, scrs):
    x_ref, mod_ref, nw_ref, w_ref, qw_ref, kw_ref, oq_ref, ok_ref, lbl_ref = ins[:9]
    if scrs:
        (w_bf_ref,) = scrs
        rows = W_CAST_ROWS

        @pl.when(pl.program_id(0) == 0)
        def _():
            def cast_rows(r, carry):
                sl = pl.ds(pl.multiple_of(r * rows, rows), rows)
                w_bf_ref[sl, :] = w_ref[sl, :].astype(BF16)
                return carry

            lax.fori_loop(0, w_ref.shape[0] // rows, cast_rows, 0)

        w_ref = w_bf_ref
    if latent:
        cos_ref, sup_ref, sdn_ref = ins[9:]
        zh_ref, q_ref, k_ref, vt_ref, g_ref = outs
    else:
        zh_ref, q_ref, k_ref, kt_ref, vt_ref, g_ref = outs
    tm = x_ref.shape[0]
    m = mod_ref[0]
    sh1 = m[:, 0:D_MODEL]
    sc1 = m[:, D_MODEL:2 * D_MODEL]
    h = _rms_rows(x_ref[...], nw_ref[...]) * (1.0 + sc1) + sh1
    hb = h.astype(BF16)
    c0 = ZH_WIDTH
    c1 = c0 + Q_WIDTH
    c2 = c1 + KV_WIDTH
    c3 = c2 + KV_WIDTH
    aq = _dot(hb, w_ref[:, c0:c1])
    ak = _dot(hb, w_ref[:, c1:c2])
    av = _dot(hb, w_ref[:, c2:c3])
    yield _mm_cost(tm, D_MODEL, c3 - c0)
    gl = yield from _dot_by_cols(hb, w_ref, (0, D_MODEL), (c3, D_IN))
    ssq = _head_sumsq(aq, oq_ref)
    ssk = _head_sumsq(ak, ok_ref)
    yield _mm_cost(tm, Q_WIDTH, Q_WIDTH) + _mm_cost(tm, KV_WIDTH, KV_WIDTH)
    lb = _forget_lower_bound(lbl_ref[...])

    def hgrn_segment(s):
        z = yield from _dot_by_cols(hb, w_ref, (0, D_MODEL), (s * HG_WIDTH, (s + 1) * HG_WIDTH))
        if s == 0:
            z = z * HG_DK ** -0.5
        elif s in (1, 2):
            lbv = lb[s - 1:s, :]
            z = jnp.log(lbv + (1.0 - lbv) * jax.nn.sigmoid(z))
        elif s == 4:
            z = z * jax.nn.sigmoid(z)
        zh_ref[:, s * HG_WIDTH:(s + 1) * HG_WIDTH] = z

    yield from hgrn_segment(1)
    yield from hgrn_segment(2)
    g_ref[...] = jax.nn.sigmoid(gl).astype(BF16)
    qn = _head_rms(aq, ssq, jnp.tile(qw_ref[...], (1, N_HEADS)))
    kn = _head_rms(ak, ssk, jnp.tile(kw_ref[...], (1, N_KV)))
    seq = vt_ref.shape[2]
    avt = av.T
    for s in range(vt_ref.shape[0]):
        vt_ref[s] = avt[:, s * seq:(s + 1) * seq].astype(vt_ref.dtype)
    for s in (4, 0, 3):
        yield from hgrn_segment(s)
    if latent:
        cos, sup, sdn = cos_ref[...], sup_ref[...], sdn_ref[...]
        qn = _rope(qn, cos, sup, sdn)
        kn = _rope(kn, cos[:, 0:KV_WIDTH], sup[:, 0:KV_WIDTH], sdn[:, 0:KV_WIDTH])
    else:
        knt = kn.T
        for s in range(kt_ref.shape[0]):
            kt_ref[s] = knt[:, s * seq:(s + 1) * seq]
    q_ref[...] = (qn * (HEAD_DIM ** -0.5 * LOG2_E)).astype(BF16)
    k_ref[...] = kn.astype(BF16)


def _in_part(tm, x2, mod, seq_len, norm_w, w_in, qw, kw, ones_q, ones_k, lb_logits, rope):
    t = x2.shape[0]
    latent = rope is not None
    per_seq = max(seq_len // tm, 1)
    row = lambda i: (i, 0)
    in_specs = [pl.BlockSpec((tm, D_MODEL), row),
                _mod_spec(mod, tm, seq_len),
                _const_spec((1, D_MODEL)),
                _const_spec((D_MODEL, D_IN)),
                _const_spec((1, HEAD_DIM)),
                _const_spec((1, HEAD_DIM)),
                _const_spec((Q_WIDTH, Q_WIDTH)),
                _const_spec((KV_WIDTH, KV_WIDTH)),
                _const_spec(lb_logits.shape)]
    args = [x2, mod.rows, norm_w, w_in, qw, kw, ones_q, ones_k, lb_logits]
    out_specs = [pl.BlockSpec((tm, ZH_WIDTH), row), pl.BlockSpec((tm, Q_WIDTH), row),
                 pl.BlockSpec((tm, KV_WIDTH), row)]
    out_shape = [jax.ShapeDtypeStruct((t, ZH_WIDTH), F32), jax.ShapeDtypeStruct((t, Q_WIDTH), BF16),
                 jax.ShapeDtypeStruct((t, KV_WIDTH), BF16)]
    if seq_len <= tm:
        assert tm % seq_len == 0
        t_spec = pl.BlockSpec((tm // seq_len, KV_WIDTH, seq_len), lambda i: (i, 0, 0))
    else:
        t_spec = pl.BlockSpec((1, KV_WIDTH, tm), lambda i: (i // per_seq, 0, i % per_seq))
    if latent:
        in_specs += [pl.BlockSpec((tm, Q_WIDTH), lambda i: (i % per_seq, 0))] * 3
        args += list(rope)
        out_specs.append(t_spec)
        out_shape.append(jax.ShapeDtypeStruct((t // seq_len, KV_WIDTH, seq_len), BF16))
    else:
        out_specs += [t_spec, t_spec]
        out_shape += [jax.ShapeDtypeStruct((t // seq_len, KV_WIDTH, seq_len), F32)] * 2
    out_specs.append(pl.BlockSpec((tm, 2 * D_MODEL), row))
    out_shape.append(jax.ShapeDtypeStruct((t, 2 * D_MODEL), BF16))
    cost = (_mm_cost(tm, D_MODEL, D_IN) + _mm_cost(tm, Q_WIDTH, Q_WIDTH) + _mm_cost(tm, KV_WIDTH, KV_WIDTH))
    scratch = [] if w_in.dtype == BF16 else [pltpu.VMEM((D_MODEL, D_IN), BF16)]
    return t // tm, _Part(functools.partial(_in_stages, latent), args, in_specs, out_specs, out_shape, scratch, cost)


def _attn_cost(lq, lks):
    per_head = sum(_mm_cost(lk, HEAD_DIM, lq) + _mm_cost(HEAD_DIM + ONES_ROWS, lk, lq) for lk in lks)
    return N_HEADS * per_head


def _attn_stages(k_transposed, lookahead, ins, outs, _):
    n_seg = len(k_transposed)
    q_ref, kv_refs = ins[0], ins[1:]
    (o_ref,) = outs
    q = q_ref[...]
    lq = q.shape[0]
    ks = [(kv_refs[2 * s][0].T if k_transposed[s] else kv_refs[2 * s][...]).astype(BF16) for s in range(n_seg)]
    vts = [kv_refs[2 * s + 1][0].astype(BF16) for s in range(n_seg)]
    vts = [[jnp.concatenate([vt[g * HEAD_DIM:(g + 1) * HEAD_DIM, :],
                             jnp.ones((ONES_ROWS, vt.shape[1]), BF16)], axis=0) for g in range(N_KV)]
           for vt in vts]
    score_cost = sum(_mm_cost(k.shape[0], HEAD_DIM, lq) for k in ks)
    value_cost = sum(_mm_cost(HEAD_DIM + ONES_ROWS, k.shape[0], lq) for k in ks)

    def scores(h):
        g = h // (N_HEADS // N_KV)
        qh = q[:, h * HEAD_DIM:(h + 1) * HEAD_DIM]
        return [_dot_nt(k[:, g * HEAD_DIM:(g + 1) * HEAD_DIM], qh) for k in ks]

    pending = []
    for h in range(lookahead):
        pending.append(scores(h))
        yield score_cost
    outs_t = []
    for h in range(N_HEADS):
        if h + lookahead < N_HEADS:
            pending.append(scores(h + lookahead))
            yield score_cost
        st = pending.pop(0)
        g = h // (N_HEADS // N_KV)
        m = functools.reduce(jnp.maximum, [jnp.max(s, axis=0, keepdims=True) for s in st])
        ot = functools.reduce(jnp.add, [_dot(vt[g], jnp.exp2(s - m).astype(BF16)) for vt, s in zip(vts, st)])
        outs_t.append(ot[0:HEAD_DIM, :] / ot[HEAD_DIM:HEAD_DIM + 1, :])
        if h == N_HEADS - 1:
            o_ref[...] = jnp.concatenate(outs_t, axis=0).T.astype(BF16)
        yield value_cost


def _attn_part(q, kv_segments, n_batch, seq_len, lookahead):
    t = q.shape[0]
    tq = ATTN_Q_TILE
    per_seq = seq_len // tq
    in_specs = [pl.BlockSpec((tq, Q_WIDTH), lambda i: (i, 0))]
    args = [q]
    for k, v_t, layer in kv_segments:
        lk = v_t.shape[-1]
        if v_t.ndim == 4:
            slab = pl.BlockSpec((1, None, KV_WIDTH, lk), lambda i, layer=layer: (i // per_seq, layer, 0, 0))
        else:
            slab = pl.BlockSpec((1, KV_WIDTH, lk), lambda i: (i // per_seq, 0, 0))
        k_spec = slab if k.ndim == v_t.ndim else pl.BlockSpec((lk, KV_WIDTH), lambda i: (i // per_seq, 0))
        in_specs += [k_spec, slab]
        args += [k, v_t]
    stages = functools.partial(_attn_stages, tuple(k.ndim == v_t.ndim for k, v_t, _ in kv_segments), lookahead)
    return n_batch * per_seq, _Part(stages, args, in_specs, [pl.BlockSpec((tq, Q_WIDTH), lambda i: (i, 0))],
                                    [jax.ShapeDtypeStruct((t, Q_WIDTH), BF16)], [],
                                    _attn_cost(tq, [v_t.shape[-1] for _, v_t, _ in kv_segments]))


def _hgrn_unit_cost():
    blk = HG_BLOCK
    return (2 * _mm_cost(blk, blk, 2 * HG_DK)
            + 2 * _mm_cost(blk, HG_DK, 2 * blk) + _mm_cost(blk, blk, HG_DV)
            + (blk // HG_PAIR) * (_mm_cost(HG_DV, HG_PAIR, 2 * HG_DK) + _mm_cost(HG_PAIR, 2 * HG_DK, HG_DV)))


def _hgrn_stages(seq_len, hps, has_s0, has_sfin, ins, outs, scrs):
    tf_ref, tb_ref, xf_ref, xb_ref, q_ref, lf_ref, lb_ref, v_ref, g_ref, nw_ref = ins[:10]
    s0_ref = ins[10] if has_s0 else None
    o_ref = outs[0]
    sfin_ref = outs[1] if has_sfin else None
    kv_scr, ss_scr, qd_scr, oi_scr, dec_scr = scrs
    n_blk = seq_len // HG_BLOCK
    n_pair = seq_len // HG_PAIR
    per_blk = HG_BLOCK // HG_CHUNK
    pairs_per_blk = HG_BLOCK // HG_PAIR
    one_row = jnp.ones((1, HG_DK), F32)

    def chunk_cumsum(t_ref, x):
        hi, lo = _split_bf16(x)
        r = _dot(t_ref[...], jnp.concatenate([hi, lo], axis=1))
        return r[:, 0:HG_DK] + r[:, HG_DK:2 * HG_DK]

    def chunk_edge_rows(x, first):
        off = 0 if first else HG_CHUNK - 1
        return [x[c * HG_CHUNK + off:c * HG_CHUNK + off + 1, :] for c in range(per_blk)]

    def spread_rows(rows):
        return jnp.concatenate([jnp.broadcast_to(r, (HG_CHUNK, HG_DK)) for r in rows], axis=0)

    units = [(blk, h) for blk in range(n_blk) for h in range(hps)]
    stage1 = []
    for blk, h in units:
        rows = slice(blk * HG_BLOCK, (blk + 1) * HG_BLOCK)
        cols = slice(h * HG_DK, (h + 1) * HG_DK)
        logf_f = lf_ref[rows, cols]
        logf_b = lb_ref[rows, cols]
        k_f = 1.0 - jnp.exp(logf_f)
        k_b = 1.0 - jnp.exp(logf_b)
        a_f = chunk_cumsum(tf_ref, logf_f)
        a_b = chunk_cumsum(tb_ref, logf_b)
        stage1.append((rows, cols, k_f, k_b, a_f, a_b))
        yield 2 * _mm_cost(HG_BLOCK, HG_BLOCK, 2 * HG_DK)
    stage2 = []
    for (blk, h), (rows, cols, k_f, k_b, a_f, a_b) in zip(units, stage1):
        q = q_ref[rows, cols]
        vb = v_ref[rows, cols].astype(BF16)
        ea_f = jnp.exp(a_f)
        ea_b = jnp.exp(a_b)
        qd_f = q * ea_f
        qd_b = q * ea_b
        kd_f = k_f * jnp.exp(-a_f)
        kd_b = k_b * jnp.exp(-a_b)
        dec_f = chunk_edge_rows(ea_f, first=False)
        dec_b = chunk_edge_rows(ea_b, first=True)
        ke_f = kd_f * spread_rows(dec_f)
        ke_b = kd_b * spread_rows(dec_b)
        even = [c % 2 == 0 for c in range(per_blk)]
        qd_pair = jnp.concatenate(
            [qd_f * spread_rows([one_row if even[c] else dec_f[c - 1] for c in range(per_blk)]),
             qd_b * spread_rows([dec_b[c + 1] if even[c] else one_row for c in range(per_blk)])], axis=1)
        ke_pair = jnp.concatenate(
            [ke_f * spread_rows([dec_f[c + 1] if even[c] else one_row for c in range(per_blk)]),
             ke_b * spread_rows([one_row if even[c] else dec_b[c - 1] for c in range(per_blk)])], axis=1).astype(BF16)
        sc_f = _dot_nt(qd_f.astype(BF16), jnp.concatenate([kd_f, ke_f], axis=0).astype(BF16))
        sc_b = _dot_nt(qd_b.astype(BF16), jnp.concatenate([kd_b, ke_b], axis=0).astype(BF16))
        for p in range(pairs_per_blk):
            pr = slice(p * HG_PAIR, (p + 1) * HG_PAIR)
            kv_scr[h, blk * pairs_per_blk + p] = _dot_tn(vb[pr, :], ke_pair[pr, :])
            dec_scr[h, blk * pairs_per_blk + p, 0:1, :] = jnp.concatenate(
                [dec_f[2 * p] * dec_f[2 * p + 1], dec_b[2 * p] * dec_b[2 * p + 1]], axis=1)
        qd_scr[h, rows, :] = qd_pair.astype(BF16)
        stage2.append((rows, cols, vb, sc_f, sc_b))
        yield 2 * _mm_cost(HG_BLOCK, HG_DK, 2 * HG_BLOCK) + pairs_per_blk * _mm_cost(HG_DV, HG_PAIR, 2 * HG_DK)
    for rows, cols, vb, sc_f, sc_b in stage2:
        s = (sc_f[:, 0:HG_BLOCK].astype(BF16) * tf_ref[...] + sc_f[:, HG_BLOCK:].astype(BF16) * xf_ref[...]
             + sc_b[:, 0:HG_BLOCK].astype(BF16) * tb_ref[...] + sc_b[:, HG_BLOCK:].astype(BF16) * xb_ref[...])
        oi_scr[rows, cols] = _dot(s, vb)
        yield _mm_cost(HG_BLOCK, HG_BLOCK, HG_DV)

    for h in range(hps):
        if has_s0:
            s_f, s_b = s0_ref[0, 0, h].T, s0_ref[0, 1, h].T
        else:
            s_f = s_b = jnp.zeros((HG_DV, HG_DK), F32)
        for pf in range(n_pair):
            pb = n_pair - 1 - pf
            ss_scr[h, pf, :, 0:HG_DK] = s_f.astype(BF16)
            s_f = dec_scr[h, pf, 0:1, 0:HG_DK] * s_f + kv_scr[h, pf, :, 0:HG_DK]
            ss_scr[h, pb, :, HG_DK:2 * HG_DK] = s_b.astype(BF16)
            s_b = dec_scr[h, pb, 0:1, HG_DK:2 * HG_DK] * s_b + kv_scr[h, pb, :, HG_DK:2 * HG_DK]
        if has_sfin:
            sfin_ref[0, 0, h] = s_f.T
            sfin_ref[0, 1, h] = s_b.T

    for blk, h in units:
        cols = slice(h * HG_DK, (h + 1) * HG_DK)
        for p in range(pairs_per_blk):
            pi = blk * pairs_per_blk + p
            rows = slice(pi * HG_PAIR, (pi + 1) * HG_PAIR)
            oi_scr[rows, cols] = oi_scr[rows, cols] + _dot_nt(qd_scr[h, rows, :], ss_scr[h, pi])
        yield pairs_per_blk * _mm_cost(HG_PAIR, 2 * HG_DK, HG_DV)

    for h in range(hps):
        cols = slice(h * HG_DK, (h + 1) * HG_DK)
        o = _rms_rows(oi_scr[:, cols], nw_ref[...])
        o_ref[:, cols] = (o * g_ref[:, cols]).astype(BF16)


def _chunk_masks():
    r = np.arange(HG_BLOCK)
    same = (r[:, None] // HG_CHUNK) == (r[None, :] // HG_CHUNK)
    lower = same & (r[None, :] <= r[:, None])
    upper = same & (r[None, :] >= r[:, None])
    same_pair = (r[:, None] // HG_PAIR) == (r[None, :] // HG_PAIR)
    cross_f = same_pair & (r[:, None] // HG_CHUNK == r[None, :] // HG_CHUNK + 1)
    return tuple(jnp.asarray(m.astype(np.float32), dtype=BF16) for m in (lower, upper, cross_f, cross_f.T))


def _hgrn_part(zh, norm_w, n_batch, seq_len, hps, s0, want_final):
    t = zh.shape[0]
    n_pair = seq_len // HG_PAIR
    width = hps * HG_DK
    n_hb = HG_HEADS // hps
    masks = _chunk_masks()

    def seg_spec(seg):
        return pl.BlockSpec((seq_len, width), lambda i: (i // n_hb, seg * n_hb + i % n_hb))

    state_spec = pl.BlockSpec((1, 2, hps, HG_DK, HG_DV), lambda i: (i // n_hb, 0, i % n_hb, 0, 0))
    mask_spec = pl.BlockSpec((HG_BLOCK, HG_BLOCK), lambda i: (0, 0))
    in_specs = [mask_spec] * len(masks) + [seg_spec(s) for s in range(5)] + [pl.BlockSpec((1, HG_DV), lambda i: (0, 0))]
    args = list(masks) + [zh] * 5 + [norm_w]
    if s0 is not None:
        states, layer = s0
        in_specs.append(pl.BlockSpec((1, None, 2, hps, HG_DK, HG_DV),
                                     lambda i: (i // n_hb, layer, 0, i % n_hb, 0, 0)))
        args.append(states)
    out_specs = [pl.BlockSpec((seq_len, width), lambda i: (i // n_hb, i % n_hb))]
    out_shape = [jax.ShapeDtypeStruct((t, HG_WIDTH), BF16)]
    if want_final:
        out_specs.append(state_spec)
        out_shape.append(jax.ShapeDtypeStruct((n_batch, 2, HG_HEADS, HG_DK, HG_DV), F32))
    scratch = [pltpu.VMEM((hps, n_pair, HG_DV, 2 * HG_DK), F32),
               pltpu.VMEM((hps, n_pair, HG_DV, 2 * HG_DK), BF16),
               pltpu.VMEM((hps, seq_len, 2 * HG_DK), BF16),
               pltpu.VMEM((seq_len, width), F32),
               pltpu.VMEM((hps, n_pair, SUBLANES, 2 * HG_DK), F32)]
    stages = functools.partial(_hgrn_stages, seq_len, hps, s0 is not None, want_final)
    cost = hps * (seq_len // HG_BLOCK) * _hgrn_unit_cost()
    return n_batch * n_hb, _Part(stages, args, in_specs, out_specs, out_shape, scratch, cost)


def _out_stages(ins, outs, _):
    x_ref, mod_ref, oh_ref, oa_ref, g_ref, who_ref, wao_ref, wout_ref, nfw_ref, wff1_ref, wff2_ref, fnw_ref = ins
    (y_ref,) = outs
    tm = x_ref.shape[0]
    m = mod_ref[0]
    g1 = m[:, 2 * D_MODEL:3 * D_MODEL]
    sh2 = m[:, 3 * D_MODEL:4 * D_MODEL]
    sc2 = m[:, 4 * D_MODEL:5 * D_MODEL]
    g2 = m[:, 5 * D_MODEL:6 * D_MODEL]
    gates = g_ref[...].astype(F32)
    from_h = yield from _dot_by_cols(oh_ref[...], who_ref, (0, HG_WIDTH), (0, D_MODEL))
    from_a = yield from _dot_by_cols(oa_ref[...], wao_ref, (0, Q_WIDTH), (0, D_MODEL))
    merged = gates[:, 0:D_MODEL] * from_h + gates[:, D_MODEL:2 * D_MODEL] * from_a
    mixed = yield from _dot_by_cols(merged.astype(BF16), wout_ref, (0, D_MODEL), (0, D_MODEL))
    x1 = x_ref[...] + g1 * mixed
    h2 = (_rms_rows(x1, nfw_ref[...]) * (1.0 + sc2) + sh2).astype(BF16)
    acc = jnp.zeros_like(x1)
    for j in range(D_FF // D_MODEL):
        cols = slice(j * D_MODEL, (j + 1) * D_MODEL)
        hj = yield from _dot_by_cols(h2, wff1_ref, (0, D_MODEL), (cols.start, cols.stop))
        hj = jnp.maximum(hj, 0.0)
        acc = acc + (yield from _dot_by_cols((hj * hj).astype(BF16), wff2_ref, (cols.start, cols.stop), (0, D_MODEL)))
    y_ref[...] = _rms_rows(x1 + g2 * acc, fnw_ref[...])


def _out_part(tm, x2, mod, seq_len, oh, oa, gates, w_ho, w_ao, w_out, nfw, w_ff1, w_ff2, fnw):
    t = x2.shape[0]
    row = lambda i: (i, 0)
    in_specs = [pl.BlockSpec((tm, D_MODEL), row),
                _mod_spec(mod, tm, seq_len),
                pl.BlockSpec((tm, HG_WIDTH), row),
                pl.BlockSpec((tm, Q_WIDTH), row),
                pl.BlockSpec((tm, 2 * D_MODEL), row),
                _const_spec((HG_WIDTH, D_MODEL)),
                _const_spec((Q_WIDTH, D_MODEL)),
                _const_spec((D_MODEL, D_MODEL)),
                _const_spec((1, D_MODEL)),
                _const_spec((D_MODEL, D_FF)),
                _const_spec((D_FF, D_MODEL)),
                _const_spec((1, D_MODEL))]
    args = [x2, mod.rows, oh, oa, gates, w_ho, w_ao, w_out, nfw, w_ff1, w_ff2, fnw]
    cost = _mm_cost(tm, HG_WIDTH + Q_WIDTH, D_MODEL) + _mm_cost(tm, D_MODEL, D_MODEL) + 2 * _mm_cost(tm, D_MODEL, D_FF)
    return t // tm, _Part(_out_stages, args, in_specs, [pl.BlockSpec((tm, D_MODEL), row)],
                          [jax.ShapeDtypeStruct((t, D_MODEL), F32)], [], cost)


def _cast_stages(ins, outs, _):
    for src, dst in zip(ins, outs):
        dst[...] = src[...].astype(dst.dtype)
    yield 1


def _cast_part(arrays, n_steps):
    in_specs, out_specs, out_shape = [], [], []
    for a in arrays:
        rows, cols = a.shape
        assert rows % (n_steps * MXU_ROWS) == 0
        spec = pl.BlockSpec((rows // n_steps, cols), lambda i: (i, 0))
        in_specs.append(spec)
        out_specs.append(spec)
        out_shape.append(jax.ShapeDtypeStruct(a.shape, BF16))
    return n_steps, _Part(_cast_stages, list(arrays), in_specs, out_specs, out_shape, [], 1)


def _rope_tables(n_tokens):
    rows = n_tokens // GRID_W
    row = np.repeat(np.arange(rows, dtype=np.float32), GRID_W)
    col = np.tile(np.arange(GRID_W, dtype=np.float32), rows)
    axis_dim = HEAD_DIM // 2
    freqs = (ROPE_THETA ** (-np.arange(0, axis_dim, 2, dtype=np.float32) / axis_dim)).astype(np.float32)
    ang_r = row[:, None] * freqs
    ang_c = col[:, None] * freqs
    cr, sr, cc, sc = np.cos(ang_r), np.sin(ang_r), np.cos(ang_c), np.sin(ang_c)
    zero = np.zeros_like(sr)
    cos = np.concatenate([cr, cr, cc, cc], axis=-1)
    s_up = np.concatenate([-sr, zero, -sc, zero], axis=-1)
    s_dn = np.concatenate([zero, sr, zero, sc], axis=-1)
    return tuple(jnp.asarray(np.tile(a, (1, N_HEADS)), dtype=F32) for a in (cos, s_up, s_dn))


def _to_slab(kv):
    n, depth, length = kv.shape[:3]
    return jnp.transpose(kv, (0, 1, 3, 4, 2)).reshape(n, depth, KV_WIDTH, length)


def _from_slab(slab):
    n, _, length = slab.shape
    return jnp.transpose(slab.reshape(n, N_KV, HEAD_DIM, length), (0, 3, 1, 2))


def _block_ones(width):
    idx = np.arange(width) // HEAD_DIM
    return jnp.asarray((idx[:, None] == idx[None, :]).astype(np.float32), dtype=BF16)


def _same_steps(*counted_parts):
    steps = {n for n, _ in counted_parts}
    assert len(steps) == 1, steps
    return steps.pop(), [p for _, p in counted_parts]


def kernel(x_prompt, x_sample, cache_k, cache_v, state_hgrn, c, c_ctx, w_ada, b_ada, norm_mix_w, w_in, q_norm_w, k_norm_w, hgrn_lb_logits, hgrn_norm_w, w_hgrn_out, w_attn_out, w_out, norm_ffn_w, w_ff1, w_ff2, final_norm_w):
    n_p, l_p, _ = x_prompt.shape
    n_s, l_s, _ = x_sample.shape
    layer = 0

    mod = _modulation(c_ctx[None, :], c, w_ada[layer], b_ada[layer][None, :])
    mod_p = _Mod(mod, 0, True)
    mod_s = _Mod(mod, 1, False)

    in_small = (q_norm_w[layer][None, :], k_norm_w[layer][None, :],
                _block_ones(Q_WIDTH), _block_ones(KV_WIDTH), hgrn_lb_logits)
    nmw = norm_mix_w[layer][None, :]
    hnw = hgrn_norm_w[layer][None, :]
    xp2 = x_prompt.reshape(n_p * l_p, D_MODEL)
    xs2 = x_sample.reshape(n_s * l_s, D_MODEL)

    n_in, in_p = _in_part(MIX_TOKEN_TILE, xp2, mod_p, l_p, nmw, w_in[layer], *in_small, None)
    steps, parts = _same_steps((n_in, in_p), _cast_part(
        [w_in[layer], w_hgrn_out[layer], w_attn_out[layer], w_out[layer], w_ff1[layer], w_ff2[layer]], n_in))
    (zh_p, q_p, k_p, kt_p, vt_p, gates_p), (w_in_b, w_ho_b, w_ao_b, w_out_b, w_ff1_b, w_ff2_b) = _launch(
        "in_ctx", steps, parts)
    out_w = (w_ho_b, w_ao_b, w_out_b, norm_ffn_w[layer][None, :], w_ff1_b, w_ff2_b, final_norm_w[None, :])

    steps, parts = _same_steps(
        _attn_part(q_p, [(k_p, vt_p, None)], n_p, l_p, N_HEADS),
        _in_part(MIX_TOKEN_TILE, xs2, mod_s, l_s, nmw, w_in_b, *in_small, _rope_tables(l_s)),
        _hgrn_part(zh_p, hnw, n_p, l_p, HG_HEADS, None, True))
    (oa_p,), (zh_s, q_s, k_s, vt_s, gates_s), (oh_p, s_fin) = _launch("in_latent_mix_ctx", steps, parts,
                                                                      MIX_PACES_IN)

    segs = [(_to_slab(cache_k), _to_slab(cache_v), layer), (k_s, vt_s, None)]
    steps, parts = _same_steps(
        _out_part(MIX_TOKEN_TILE, xp2, mod_p, l_p, oh_p, oa_p, gates_p, *out_w),
        _attn_part(q_s, segs, n_s, l_s, ATTN_LOOKAHEAD),
        _hgrn_part(zh_s, hnw, n_s, l_s, 1, (state_hgrn, layer), False))
    (y_p,), (oa_s,), (oh_s,) = _launch("out_ctx_mix_latent", steps, parts, MIX_PACES_OUT)

    steps, parts = _same_steps(_out_part(SOLO_TOKEN_TILE, xs2, mod_s, l_s, oh_s, oa_s, gates_s, *out_w))
    ((y_s,),) = _launch("out_latent", steps, parts)

    new_k = _from_slab(kt_p)[:, None]
    new_v = _from_slab(vt_p)[:, None]
    new_s = s_fin.reshape(n_p, 1, 2, HG_HEADS, HG_DK, HG_DV)
    return (y_p.reshape(n_p, l_p, D_MODEL), y_s.reshape(n_s, l_s, D_MODEL), new_k, new_v, new_s)
```
